```python
import jax, jax.numpy as jnp
from jax import lax
import numpy as np

D_MODEL = 1024
BATCH = 8
SEQ = 8192
DEPTH = 1

CHUNK = 64
LEFT_CHUNKS = 8
BAND = (LEFT_CHUNKS + 1) * CHUNK
N_HEADS = 8
HEAD_DIM = 64
D_ATTN = N_HEADS * HEAD_DIM
REL_CLIP = 128
D_CONV = 512
CONV_WIDTH = 31
N_BRANCH = 2
D_IN = 3 * D_ATTN + 2 * D_CONV + N_BRANCH * D_MODEL
D_FF = 2816
EPS = 1e-6

kernel_name = "hybrid_chunked_attn_conformer_conv_macaron"


def rms_norm(x, g):
    xf = x.astype(jnp.float32)
    y = xf * lax.rsqrt(jnp.mean(xf * xf, axis=-1, keepdims=True) + EPS)
    return (y * g.astype(jnp.float32)).astype(x.dtype)


def layer_norm(x, g, b):
    xf = x.astype(jnp.float32)
    mu = jnp.mean(xf, axis=-1, keepdims=True)
    var = jnp.mean(jnp.square(xf - mu), axis=-1, keepdims=True)
    y = (xf - mu) * lax.rsqrt(var + EPS)
    return (y * g.astype(jnp.float32) + b.astype(jnp.float32)).astype(x.dtype)


def swiglu(x, w_gate, w_up, w_down):
    return (jax.nn.silu(x @ w_gate) * (x @ w_up)) @ w_down


def chunked_attention(q, k, v, rel_table):
    B, S, H, Dh = q.shape
    nc = S // CHUNK
    qc = q.reshape(B, nc, CHUNK, H, Dh)
    pad = ((0, 0), (LEFT_CHUNKS, 0), (0, 0), (0, 0), (0, 0))
    kp = jnp.pad(k.reshape(B, nc, CHUNK, H, Dh), pad)
    vp = jnp.pad(v.reshape(B, nc, CHUNK, H, Dh), pad)
    k_band = jnp.concatenate([kp[:, j:j + nc] for j in range(LEFT_CHUNKS + 1)], axis=2)
    v_band = jnp.concatenate([vp[:, j:j + nc] for j in range(LEFT_CHUNKS + 1)], axis=2)
    scores = jnp.einsum('bnqhd,bnkhd->bnhqk', qc, k_band).astype(jnp.float32) * (HEAD_DIM ** -0.5)
    qi = jnp.arange(CHUNK)[:, None]
    kj = jnp.arange(BAND)[None, :]
    dist = qi + LEFT_CHUNKS * CHUNK - kj
    idx = jnp.clip(dist, -REL_CLIP, REL_CLIP) + REL_CLIP
    bias = rel_table.astype(jnp.float32)[:, idx]
    scores = scores + bias[None, None]
    key_chunk = jnp.arange(nc)[:, None] + (jnp.arange(BAND) // CHUNK)[None, :] - LEFT_CHUNKS
    valid = (key_chunk >= 0)[None, :, None, None, :]
    scores = jnp.where(valid, scores, jnp.float32(-1e30))
    p = jax.nn.softmax(scores, axis=-1).astype(v.dtype)
    out = jnp.einsum('bnhqk,bnkhd->bnqhd', p, v_band)
    return out.reshape(B, S, H * Dh)


def conv_module(c_in, glu_bias, dw_w, dw_b, ln_g, ln_b, w_out):
    c = c_in + glu_bias
    c = c[..., :D_CONV] * jax.nn.sigmoid(c[..., D_CONV:])
    c = lax.conv_general_dilated(
        c, dw_w.astype(c.dtype), window_strides=(1,), padding=((CONV_WIDTH - 1, 0),),
        dimension_numbers=('NWC', 'WIO', 'NWC'), feature_group_count=D_CONV) + dw_b
    c = jax.nn.silu(layer_norm(c, ln_g, ln_b))
    return c @ w_out


def hybrid_mixer(u, w_in, gate_bias, rel_table, w_attn_out, conv_glu_bias, conv_dw_w,
                 conv_dw_b, conv_ln_g, conv_ln_b, conv_w_out, w_out):
    B, S, _ = u.shape
    proj = u @ w_in
    q, k, v, c_in, g = jnp.split(
        proj, [D_ATTN, 2 * D_ATTN, 3 * D_ATTN, 3 * D_ATTN + 2 * D_CONV], axis=-1)
    shp = (B, S, N_HEADS, HEAD_DIM)
    y_a = chunked_attention(q.reshape(shp), k.reshape(shp), v.reshape(shp), rel_table) @ w_attn_out
    y_b = conv_module(c_in, conv_glu_bias, conv_dw_w, conv_dw_b, conv_ln_g, conv_ln_b, conv_w_out)
    gates = jax.nn.sigmoid(g + gate_bias)
    merged = gates[..., :D_MODEL] * y_a + gates[..., D_MODEL:] * y_b
    return merged @ w_out


def _fwd_setup_inputs(seed: int = 0) -> dict:
    key = jax.random.key(seed)
    ks = jax.random.split(key, 32)
    L, D = DEPTH, D_MODEL

    def w(k, shape, fan_in):
        return jax.random.normal(k, shape, jnp.float32) * (fan_in ** -0.5)

    def gain(k, shape):
        return 1.0 + 0.05 * jax.random.normal(k, shape, jnp.float32)

    def small(k, shape, s=0.02):
        return s * jax.random.normal(k, shape, jnp.float32)

    return {
        "x": jax.random.normal(ks[0], (BATCH, SEQ, D), jnp.float32),
        "ffn1_norm_pre": gain(ks[1], (L, D)),
        "ffn1_w_gate": w(ks[2], (L, D, D_FF), D),
        "ffn1_w_up": w(ks[3], (L, D, D_FF), D),
        "ffn1_w_down": w(ks[4], (L, D_FF, D), D_FF),
        "ffn1_norm_post": gain(ks[5], (L, D)),
        "mix_norm_pre": gain(ks[6], (L, D)),
        "w_in": w(ks[7], (L, D, D_IN), D),
        "gate_bias": small(ks[8], (L, N_BRANCH * D), 0.1),
        "rel_table": small(ks[9], (L, N_HEADS, 2 * REL_CLIP + 1), 0.5),
        "w_attn_out": w(ks[10], (L, D_ATTN, D), D_ATTN),
        "conv_glu_bias": small(ks[11], (L, 2 * D_CONV)),
        "conv_dw_w": w(ks[12], (L, CONV_WIDTH, 1, D_CONV), CONV_WIDTH),
        "conv_dw_b": small(ks[13], (L, D_CONV)),
        "conv_ln_g": gain(ks[14], (L, D_CONV)),
        "conv_ln_b": small(ks[15], (L, D_CONV)),
        "conv_w_out": w(ks[16], (L, D_CONV, D), D_CONV),
        "w_out": w(ks[17], (L, D, D), D),
        "mix_norm_post": gain(ks[18], (L, D)),
        "ffn2_norm_pre": gain(ks[19], (L, D)),
        "ffn2_w_gate": w(ks[20], (L, D, D_FF), D),
        "ffn2_w_up": w(ks[21], (L, D, D_FF), D),
        "ffn2_w_down": w(ks[22], (L, D_FF, D), D_FF),
        "ffn2_norm_post": gain(ks[23], (L, D)),
    }


def _fwd_reference(x, ffn1_norm_pre, ffn1_w_gate, ffn1_w_up, ffn1_w_down, ffn1_norm_post,
              mix_norm_pre, w_in, gate_bias, rel_table, w_attn_out, conv_glu_bias,
              conv_dw_w, conv_dw_b, conv_ln_g, conv_ln_b, conv_w_out, w_out, mix_norm_post,
              ffn2_norm_pre, ffn2_w_gate, ffn2_w_up, ffn2_w_down, ffn2_norm_post):
    h = x
    for l in range(DEPTH):
        f = swiglu(rms_norm(h, ffn1_norm_pre[l]), ffn1_w_gate[l], ffn1_w_up[l], ffn1_w_down[l])
        h = h + 0.5 * rms_norm(f, ffn1_norm_post[l])
        m = hybrid_mixer(rms_norm(h, mix_norm_pre[l]), w_in[l], gate_bias[l], rel_table[l],
                         w_attn_out[l], conv_glu_bias[l], conv_dw_w[l], conv_dw_b[l],
                         conv_ln_g[l], conv_ln_b[l], conv_w_out[l], w_out[l])
        h = h + rms_norm(m, mix_norm_post[l])
        f = swiglu(rms_norm(h, ffn2_norm_pre[l]), ffn2_w_gate[l], ffn2_w_up[l], ffn2_w_down[l])
        h = h + 0.5 * rms_norm(f, ffn2_norm_post[l])
    return h


import jax as _jax
import jax.numpy as _jnp

TWIN_FORMAT = 'train_step'
FWD_PARAMS = ['x', 'ffn1_norm_pre', 'ffn1_w_gate', 'ffn1_w_up', 'ffn1_w_down', 'ffn1_norm_post', 'mix_norm_pre', 'w_in', 'gate_bias', 'rel_table', 'w_attn_out', 'conv_glu_bias', 'conv_dw_w', 'conv_dw_b', 'conv_ln_g', 'conv_ln_b', 'conv_w_out', 'w_out', 'mix_norm_post', 'ffn2_norm_pre', 'ffn2_w_gate', 'ffn2_w_up', 'ffn2_w_down', 'ffn2_norm_post']
TWIN_WEIGHTS = ['ffn1_norm_pre', 'ffn1_w_gate', 'ffn1_w_up', 'ffn1_w_down', 'ffn1_norm_post', 'mix_norm_pre', 'w_in', 'gate_bias', 'rel_table', 'w_attn_out', 'conv_glu_bias', 'conv_dw_w', 'conv_dw_b', 'conv_ln_g', 'conv_ln_b', 'conv_w_out', 'w_out', 'mix_norm_post', 'ffn2_norm_pre', 'ffn2_w_gate', 'ffn2_w_up', 'ffn2_w_down', 'ffn2_norm_post']
TWIN_DIFF_INPUT = 'x'
TWIN_INPUTS = ['x', 'ffn1_norm_pre', 'ffn1_w_gate', 'ffn1_w_up', 'ffn1_w_down', 'ffn1_norm_post', 'mix_norm_pre', 'w_in', 'gate_bias', 'rel_table', 'w_attn_out', 'conv_glu_bias', 'conv_dw_w', 'conv_dw_b', 'conv_ln_g', 'conv_ln_b', 'conv_w_out', 'w_out', 'mix_norm_post', 'ffn2_norm_pre', 'ffn2_w_gate', 'ffn2_w_up', 'ffn2_w_down', 'ffn2_norm_post', 'loss_target', 'm_ffn1_norm_pre', 'm_ffn1_w_gate', 'm_ffn1_w_up', 'm_ffn1_w_down', 'm_ffn1_norm_post', 'm_mix_norm_pre', 'm_w_in', 'm_gate_bias', 'm_rel_table', 'm_w_attn_out', 'm_conv_glu_bias', 'm_conv_dw_w', 'm_conv_dw_b', 'm_conv_ln_g', 'm_conv_ln_b', 'm_conv_w_out', 'm_w_out', 'm_mix_norm_post', 'm_ffn2_norm_pre', 'm_ffn2_w_gate', 'm_ffn2_w_up', 'm_ffn2_w_down', 'm_ffn2_norm_post', 'v_ffn1_norm_pre', 'v_ffn1_w_gate', 'v_ffn1_w_up', 'v_ffn1_w_down', 'v_ffn1_norm_post', 'v_mix_norm_pre', 'v_w_in', 'v_gate_bias', 'v_rel_table', 'v_w_attn_out', 'v_conv_glu_bias', 'v_conv_dw_w', 'v_conv_dw_b', 'v_conv_ln_g', 'v_conv_ln_b', 'v_conv_w_out', 'v_w_out', 'v_mix_norm_post', 'v_ffn2_norm_pre', 'v_ffn2_w_gate', 'v_ffn2_w_up', 'v_ffn2_w_down', 'v_ffn2_norm_post']
TWIN_OUTPUTS = ['loss', 'grad_x', 'grad_ffn1_norm_pre', 'grad_ffn1_w_gate', 'grad_ffn1_w_up', 'grad_ffn1_w_down', 'grad_ffn1_norm_post', 'grad_mix_norm_pre', 'grad_w_in', 'grad_gate_bias', 'grad_rel_table', 'grad_w_attn_out', 'grad_conv_glu_bias', 'grad_conv_dw_w', 'grad_conv_dw_b', 'grad_conv_ln_g', 'grad_conv_ln_b', 'grad_conv_w_out', 'grad_w_out', 'grad_mix_norm_post', 'grad_ffn2_norm_pre', 'grad_ffn2_w_gate', 'grad_ffn2_w_up', 'grad_ffn2_w_down', 'grad_ffn2_norm_post', 'delta_ffn1_norm_pre', 'delta_ffn1_w_gate', 'delta_ffn1_w_up', 'delta_ffn1_w_down', 'delta_ffn1_norm_post', 'delta_mix_norm_pre', 'delta_w_in', 'delta_gate_bias', 'delta_rel_table', 'delta_w_attn_out', 'delta_conv_glu_bias', 'delta_conv_dw_w', 'delta_conv_dw_b', 'delta_conv_ln_g', 'delta_conv_ln_b', 'delta_conv_w_out', 'delta_w_out', 'delta_mix_norm_post', 'delta_ffn2_norm_pre', 'delta_ffn2_w_gate', 'delta_ffn2_w_up', 'delta_ffn2_w_down', 'delta_ffn2_norm_post', 'new_m_ffn1_norm_pre', 'new_m_ffn1_w_gate', 'new_m_ffn1_w_up', 'new_m_ffn1_w_down', 'new_m_ffn1_norm_post', 'new_m_mix_norm_pre', 'new_m_w_in', 'new_m_gate_bias', 'new_m_rel_table', 'new_m_w_attn_out', 'new_m_conv_glu_bias', 'new_m_conv_dw_w', 'new_m_conv_dw_b', 'new_m_conv_ln_g', 'new_m_conv_ln_b', 'new_m_conv_w_out', 'new_m_w_out', 'new_m_mix_norm_post', 'new_m_ffn2_norm_pre', 'new_m_ffn2_w_gate', 'new_m_ffn2_w_up', 'new_m_ffn2_w_down', 'new_m_ffn2_norm_post', 'new_v_ffn1_norm_pre', 'new_v_ffn1_w_gate', 'new_v_ffn1_w_up', 'new_v_ffn1_w_down', 'new_v_ffn1_norm_post', 'new_v_mix_norm_pre', 'new_v_w_in', 'new_v_gate_bias', 'new_v_rel_table', 'new_v_w_attn_out', 'new_v_conv_glu_bias', 'new_v_conv_dw_w', 'new_v_conv_dw_b', 'new_v_conv_ln_g', 'new_v_conv_ln_b', 'new_v_conv_w_out', 'new_v_w_out', 'new_v_mix_norm_post', 'new_v_ffn2_norm_pre', 'new_v_ffn2_w_gate', 'new_v_ffn2_w_up', 'new_v_ffn2_w_down', 'new_v_ffn2_norm_post']
TWIN_LEAF_KINDS = {'loss': 'loss', 'grad_x': 'grad_x', 'grad_ffn1_norm_pre': 'grad_w', 'grad_ffn1_w_gate': 'grad_w', 'grad_ffn1_w_up': 'grad_w', 'grad_ffn1_w_down': 'grad_w', 'grad_ffn1_norm_post': 'grad_w', 'grad_mix_norm_pre': 'grad_w', 'grad_w_in': 'grad_w', 'grad_gate_bias': 'grad_w', 'grad_rel_table': 'grad_w', 'grad_w_attn_out': 'grad_w', 'grad_conv_glu_bias': 'grad_w', 'grad_conv_dw_w': 'grad_w', 'grad_conv_dw_b': 'grad_w', 'grad_conv_ln_g': 'grad_w', 'grad_conv_ln_b': 'grad_w', 'grad_conv_w_out': 'grad_w', 'grad_w_out': 'grad_w', 'grad_mix_norm_post': 'grad_w', 'grad_ffn2_norm_pre': 'grad_w', 'grad_ffn2_w_gate': 'grad_w', 'grad_ffn2_w_up': 'grad_w', 'grad_ffn2_w_down': 'grad_w', 'grad_ffn2_norm_post': 'grad_w', 'delta_ffn1_norm_pre': 'delta_w', 'delta_ffn1_w_gate': 'delta_w', 'delta_ffn1_w_up': 'delta_w', 'delta_ffn1_w_down': 'delta_w', 'delta_ffn1_norm_post': 'delta_w', 'delta_mix_norm_pre': 'delta_w', 'delta_w_in': 'delta_w', 'delta_gate_bias': 'delta_w', 'delta_rel_table': 'delta_w', 'delta_w_attn_out': 'delta_w', 'delta_conv_glu_bias': 'delta_w', 'delta_conv_dw_w': 'delta_w', 'delta_conv_dw_b': 'delta_w', 'delta_conv_ln_g': 'delta_w', 'delta_conv_ln_b': 'delta_w', 'delta_conv_w_out': 'delta_w', 'delta_w_out': 'delta_w', 'delta_mix_norm_post': 'delta_w', 'delta_ffn2_norm_pre': 'delta_w', 'delta_ffn2_w_gate': 'delta_w', 'delta_ffn2_w_up': 'delta_w', 'delta_ffn2_w_down': 'delta_w', 'delta_ffn2_norm_post': 'delta_w', 'new_m_ffn1_norm_pre': 'new_m', 'new_m_ffn1_w_gate': 'new_m', 'new_m_ffn1_w_up': 'new_m', 'new_m_ffn1_w_down': 'new_m', 'new_m_ffn1_norm_post': 'new_m', 'new_m_mix_norm_pre': 'new_m', 'new_m_w_in': 'new_m', 'new_m_gate_bias': 'new_m', 'new_m_rel_table': 'new_m', 'new_m_w_attn_out': 'new_m', 'new_m_conv_glu_bias': 'new_m', 'new_m_conv_dw_w': 'new_m', 'new_m_conv_dw_b': 'new_m', 'new_m_conv_ln_g': 'new_m', 'new_m_conv_ln_b': 'new_m', 'new_m_conv_w_out': 'new_m', 'new_m_w_out': 'new_m', 'new_m_mix_norm_post': 'new_m', 'new_m_ffn2_norm_pre': 'new_m', 'new_m_ffn2_w_gate': 'new_m', 'new_m_ffn2_w_up': 'new_m', 'new_m_ffn2_w_down': 'new_m', 'new_m_ffn2_norm_post': 'new_m', 'new_v_ffn1_norm_pre': 'new_v', 'new_v_ffn1_w_gate': 'new_v', 'new_v_ffn1_w_up': 'new_v', 'new_v_ffn1_w_down': 'new_v', 'new_v_ffn1_norm_post': 'new_v', 'new_v_mix_norm_pre': 'new_v', 'new_v_w_in': 'new_v', 'new_v_gate_bias': 'new_v', 'new_v_rel_table': 'new_v', 'new_v_w_attn_out': 'new_v', 'new_v_conv_glu_bias': 'new_v', 'new_v_conv_dw_w': 'new_v', 'new_v_conv_dw_b': 'new_v', 'new_v_conv_ln_g': 'new_v', 'new_v_conv_ln_b': 'new_v', 'new_v_conv_w_out': 'new_v', 'new_v_w_out': 'new_v', 'new_v_mix_norm_post': 'new_v', 'new_v_ffn2_norm_pre': 'new_v', 'new_v_ffn2_w_gate': 'new_v', 'new_v_ffn2_w_up': 'new_v', 'new_v_ffn2_w_down': 'new_v', 'new_v_ffn2_norm_post': 'new_v'}


def _forward(args):
    return _fwd_reference(*[args[k] for k in FWD_PARAMS])


def _output_shape():
    def fwd():
        inp = _fwd_setup_inputs(0)
        return _fwd_reference(*[inp[k] for k in FWD_PARAMS])
    out = _jax.eval_shape(fwd)
    return out.shape, out.dtype

N_MICROBATCH = 1
ADAM_LR = 0.001
ADAM_B1 = 0.9
ADAM_B2 = 0.999
ADAM_EPS = 1e-08
ADAM_WD = 0.01
ADAM_STEP = 10
PER_EXAMPLE_BATCH_AXIS = {'x': 0, 'loss_target': 0}
SHARED_INPUTS = []
_WEIGHT_DTYPES = {'ffn1_norm_pre': _jnp.float32, 'ffn1_w_gate': _jnp.float32, 'ffn1_w_up': _jnp.float32, 'ffn1_w_down': _jnp.float32, 'ffn1_norm_post': _jnp.float32, 'mix_norm_pre': _jnp.float32, 'w_in': _jnp.float32, 'gate_bias': _jnp.float32, 'rel_table': _jnp.float32, 'w_attn_out': _jnp.float32, 'conv_glu_bias': _jnp.float32, 'conv_dw_w': _jnp.float32, 'conv_dw_b': _jnp.float32, 'conv_ln_g': _jnp.float32, 'conv_ln_b': _jnp.float32, 'conv_w_out': _jnp.float32, 'w_out': _jnp.float32, 'mix_norm_post': _jnp.float32, 'ffn2_norm_pre': _jnp.float32, 'ffn2_w_gate': _jnp.float32, 'ffn2_w_up': _jnp.float32, 'ffn2_w_down': _jnp.float32, 'ffn2_norm_post': _jnp.float32}
MOMENT_SCALE = {'ffn1_norm_pre': 6.436750e-01, 'ffn1_w_gate': 2.335837e-01, 'ffn1_w_up': 2.571934e-01, 'ffn1_w_down': 4.285678e-01, 'ffn1_norm_post': 1.603755e+01, 'mix_norm_pre': 5.921454e-01, 'w_in': 2.788459e-01, 'gate_bias': 3.703914e-01, 'rel_table': 7.504259e-02, 'w_attn_out': 1.273413e-01, 'conv_glu_bias': 3.744316e+00, 'conv_dw_w': 8.060897e-01, 'conv_dw_b': 8.713945e+00, 'conv_ln_g': 3.342211e+00, 'conv_ln_b': 5.081684e+00, 'conv_w_out': 1.278745e+00, 'w_out': 1.297320e+00, 'mix_norm_post': 6.481896e+01, 'ffn2_norm_pre': 1.044916e+00, 'ffn2_w_gate': 2.887545e-01, 'ffn2_w_up': 5.336319e-01, 'ffn2_w_down': 9.009515e-01, 'ffn2_norm_post': 1.600070e+01}


def _to_microbatches(a, axis):
    t = _jnp.moveaxis(a, axis, 0)
    t = t.reshape((N_MICROBATCH, t.shape[0] // N_MICROBATCH) + t.shape[1:])
    return _jnp.moveaxis(t, 1, axis + 1)


def setup_inputs(seed: int = 0) -> dict:
    inp = _fwd_setup_inputs(seed)
    key = _jax.random.fold_in(_jax.random.key(seed), 7919)
    shape, _ = _output_shape()
    out = dict(inp)
    out["loss_target"] = _jax.random.normal(_jax.random.fold_in(key, 0), shape, _jnp.float32)
    for i, name in enumerate(TWIN_WEIGHTS):
        w = inp[name].astype(_jnp.float32)
        if MOMENT_SCALE is None:
            s = _jnp.sqrt(_jnp.mean(_jnp.square(w)) + 1e-30)
        else:
            s = MOMENT_SCALE[name]
        km, kv = _jax.random.split(_jax.random.fold_in(key, i + 1))
        out[name] = w
        out["m_" + name] = s * _jax.random.normal(km, w.shape, _jnp.float32)
        out["v_" + name] = (s * s) * _jax.random.uniform(kv, w.shape, _jnp.float32, 0.5, 1.5)
    if N_MICROBATCH > 1:
        for name, axis in PER_EXAMPLE_BATCH_AXIS.items():
            out[name] = _to_microbatches(out[name], axis)
    return {'x': out['x'], 'ffn1_norm_pre': out['ffn1_norm_pre'], 'ffn1_w_gate': out['ffn1_w_gate'], 'ffn1_w_up': out['ffn1_w_up'], 'ffn1_w_down': out['ffn1_w_down'], 'ffn1_norm_post': out['ffn1_norm_post'], 'mix_norm_pre': out['mix_norm_pre'], 'w_in': out['w_in'], 'gate_bias': out['gate_bias'], 'rel_table': out['rel_table'], 'w_attn_out': out['w_attn_out'], 'conv_glu_bias': out['conv_glu_bias'], 'conv_dw_w': out['conv_dw_w'], 'conv_dw_b': out['conv_dw_b'], 'conv_ln_g': out['conv_ln_g'], 'conv_ln_b': out['conv_ln_b'], 'conv_w_out': out['conv_w_out'], 'w_out': out['w_out'], 'mix_norm_post': out['mix_norm_post'], 'ffn2_norm_pre': out['ffn2_norm_pre'], 'ffn2_w_gate': out['ffn2_w_gate'], 'ffn2_w_up': out['ffn2_w_up'], 'ffn2_w_down': out['ffn2_w_down'], 'ffn2_norm_post': out['ffn2_norm_post'], 'loss_target': out['loss_target'], 'm_ffn1_norm_pre': out['m_ffn1_norm_pre'], 'm_ffn1_w_gate': out['m_ffn1_w_gate'], 'm_ffn1_w_up': out['m_ffn1_w_up'], 'm_ffn1_w_down': out['m_ffn1_w_down'], 'm_ffn1_norm_post': out['m_ffn1_norm_post'], 'm_mix_norm_pre': out['m_mix_norm_pre'], 'm_w_in': out['m_w_in'], 'm_gate_bias': out['m_gate_bias'], 'm_rel_table': out['m_rel_table'], 'm_w_attn_out': out['m_w_attn_out'], 'm_conv_glu_bias': out['m_conv_glu_bias'], 'm_conv_dw_w': out['m_conv_dw_w'], 'm_conv_dw_b': out['m_conv_dw_b'], 'm_conv_ln_g': out['m_conv_ln_g'], 'm_conv_ln_b': out['m_conv_ln_b'], 'm_conv_w_out': out['m_conv_w_out'], 'm_w_out': out['m_w_out'], 'm_mix_norm_post': out['m_mix_norm_post'], 'm_ffn2_norm_pre': out['m_ffn2_norm_pre'], 'm_ffn2_w_gate': out['m_ffn2_w_gate'], 'm_ffn2_w_up': out['m_ffn2_w_up'], 'm_ffn2_w_down': out['m_ffn2_w_down'], 'm_ffn2_norm_post': out['m_ffn2_norm_post'], 'v_ffn1_norm_pre': out['v_ffn1_norm_pre'], 'v_ffn1_w_gate': out['v_ffn1_w_gate'], 'v_ffn1_w_up': out['v_ffn1_w_up'], 'v_ffn1_w_down': out['v_ffn1_w_down'], 'v_ffn1_norm_post': out['v_ffn1_norm_post'], 'v_mix_norm_pre': out['v_mix_norm_pre'], 'v_w_in': out['v_w_in'], 'v_gate_bias': out['v_gate_bias'], 'v_rel_table': out['v_rel_table'], 'v_w_attn_out': out['v_w_attn_out'], 'v_conv_glu_bias': out['v_conv_glu_bias'], 'v_conv_dw_w': out['v_conv_dw_w'], 'v_conv_dw_b': out['v_conv_dw_b'], 'v_conv_ln_g': out['v_conv_ln_g'], 'v_conv_ln_b': out['v_conv_ln_b'], 'v_conv_w_out': out['v_conv_w_out'], 'v_w_out': out['v_w_out'], 'v_mix_norm_post': out['v_mix_norm_post'], 'v_ffn2_norm_pre': out['v_ffn2_norm_pre'], 'v_ffn2_w_gate': out['v_ffn2_w_gate'], 'v_ffn2_w_up': out['v_ffn2_w_up'], 'v_ffn2_w_down': out['v_ffn2_w_down'], 'v_ffn2_norm_post': out['v_ffn2_norm_post']}


def _loss(weights, diff, rest, loss_target):
    with _jax.named_scope("forward"):
        args = {**rest, TWIN_DIFF_INPUT: diff, **{k: w.astype(_WEIGHT_DTYPES[k]) for k, w in weights.items()}}
        y = _forward(args)
    with _jax.named_scope("loss_head"):
        err = _jnp.square(y.astype(_jnp.float32) - loss_target)
        return 0.5 * _jnp.sum(_jnp.mean(err, axis=-1)) if err.ndim else 0.5 * err


def _adamw(w, g, m, v):
    m = ADAM_B1 * m + (1.0 - ADAM_B1) * g
    v = ADAM_B2 * v + (1.0 - ADAM_B2) * _jnp.square(g)
    m_hat = m / (1.0 - ADAM_B1 ** ADAM_STEP)
    v_hat = v / (1.0 - ADAM_B2 ** ADAM_STEP)
    delta = -ADAM_LR * (m_hat / (_jnp.sqrt(v_hat) + ADAM_EPS) + ADAM_WD * w)
    return delta, m, v


def reference(x, ffn1_norm_pre, ffn1_w_gate, ffn1_w_up, ffn1_w_down, ffn1_norm_post, mix_norm_pre, w_in, gate_bias, rel_table, w_attn_out, conv_glu_bias, conv_dw_w, conv_dw_b, conv_ln_g, conv_ln_b, conv_w_out, w_out, mix_norm_post, ffn2_norm_pre, ffn2_w_gate, ffn2_w_up, ffn2_w_down, ffn2_norm_post, loss_target, m_ffn1_norm_pre, m_ffn1_w_gate, m_ffn1_w_up, m_ffn1_w_down, m_ffn1_norm_post, m_mix_norm_pre, m_w_in, m_gate_bias, m_rel_table, m_w_attn_out, m_conv_glu_bias, m_conv_dw_w, m_conv_dw_b, m_conv_ln_g, m_conv_ln_b, m_conv_w_out, m_w_out, m_mix_norm_post, m_ffn2_norm_pre, m_ffn2_w_gate, m_ffn2_w_up, m_ffn2_w_down, m_ffn2_norm_post, v_ffn1_norm_pre, v_ffn1_w_gate, v_ffn1_w_up, v_ffn1_w_down, v_ffn1_norm_post, v_mix_norm_pre, v_w_in, v_gate_bias, v_rel_table, v_w_attn_out, v_conv_glu_bias, v_conv_dw_w, v_conv_dw_b, v_conv_ln_g, v_conv_ln_b, v_conv_w_out, v_w_out, v_mix_norm_post, v_ffn2_norm_pre, v_ffn2_w_gate, v_ffn2_w_up, v_ffn2_w_down, v_ffn2_norm_post):
    given = dict(x=x, ffn1_norm_pre=ffn1_norm_pre, ffn1_w_gate=ffn1_w_gate, ffn1_w_up=ffn1_w_up, ffn1_w_down=ffn1_w_down, ffn1_norm_post=ffn1_norm_post, mix_norm_pre=mix_norm_pre, w_in=w_in, gate_bias=gate_bias, rel_table=rel_table, w_attn_out=w_attn_out, conv_glu_bias=conv_glu_bias, conv_dw_w=conv_dw_w, conv_dw_b=conv_dw_b, conv_ln_g=conv_ln_g, conv_ln_b=conv_ln_b, conv_w_out=conv_w_out, w_out=w_out, mix_norm_post=mix_norm_post, ffn2_norm_pre=ffn2_norm_pre, ffn2_w_gate=ffn2_w_gate, ffn2_w_up=ffn2_w_up, ffn2_w_down=ffn2_w_down, ffn2_norm_post=ffn2_norm_post, loss_target=loss_target, m_ffn1_norm_pre=m_ffn1_norm_pre, m_ffn1_w_gate=m_ffn1_w_gate, m_ffn1_w_up=m_ffn1_w_up, m_ffn1_w_down=m_ffn1_w_down, m_ffn1_norm_post=m_ffn1_norm_post, m_mix_norm_pre=m_mix_norm_pre, m_w_in=m_w_in, m_gate_bias=m_gate_bias, m_rel_table=m_rel_table, m_w_attn_out=m_w_attn_out, m_conv_glu_bias=m_conv_glu_bias, m_conv_dw_w=m_conv_dw_w, m_conv_dw_b=m_conv_dw_b, m_conv_ln_g=m_conv_ln_g, m_conv_ln_b=m_conv_ln_b, m_conv_w_out=m_conv_w_out, m_w_out=m_w_out, m_mix_norm_post=m_mix_norm_post, m_ffn2_norm_pre=m_ffn2_norm_pre, m_ffn2_w_gate=m_ffn2_w_gate, m_ffn2_w_up=m_ffn2_w_up, m_ffn2_w_down=m_ffn2_w_down, m_ffn2_norm_post=m_ffn2_norm_post, v_ffn1_norm_pre=v_ffn1_norm_pre, v_ffn1_w_gate=v_ffn1_w_gate, v_ffn1_w_up=v_ffn1_w_up, v_ffn1_w_down=v_ffn1_w_down, v_ffn1_norm_post=v_ffn1_norm_post, v_mix_norm_pre=v_mix_norm_pre, v_w_in=v_w_in, v_gate_bias=v_gate_bias, v_rel_table=v_rel_table, v_w_attn_out=v_w_attn_out, v_conv_glu_bias=v_conv_glu_bias, v_conv_dw_w=v_conv_dw_w, v_conv_dw_b=v_conv_dw_b, v_conv_ln_g=v_conv_ln_g, v_conv_ln_b=v_conv_ln_b, v_conv_w_out=v_conv_w_out, v_w_out=v_w_out, v_mix_norm_post=v_mix_norm_post, v_ffn2_norm_pre=v_ffn2_norm_pre, v_ffn2_w_gate=v_ffn2_w_gate, v_ffn2_w_up=v_ffn2_w_up, v_ffn2_w_down=v_ffn2_w_down, v_ffn2_norm_post=v_ffn2_norm_post)
    weights = {n: given[n] for n in TWIN_WEIGHTS}
    shared = {n: given[n] for n in SHARED_INPUTS}
    per_example = {n: given[n] for n in ['x']}
    grad_fn = _jax.value_and_grad(_loss, argnums=(0, 1))

    def one_microbatch(ex, loss_target):
        ex = dict(ex)
        diff = ex.pop(TWIN_DIFF_INPUT)
        return grad_fn(weights, diff, {**shared, **ex}, loss_target)

    if N_MICROBATCH == 1:
        loss, (grad_w, grad_x) = one_microbatch(per_example, given["loss_target"])
    else:
        def body(carry, xs):
            loss_sum, grad_sum = carry
            l_k, (gw_k, gx_k) = one_microbatch(xs[0], xs[1])
            with _jax.named_scope("update"):
                return (loss_sum + l_k, _jax.tree.map(_jnp.add, grad_sum, gw_k)), gx_k

        init = (_jnp.zeros((), _jnp.float32), _jax.tree.map(_jnp.zeros_like, weights))
        (loss, grad_w), grad_x = _jax.lax.scan(body, init, (per_example, given["loss_target"]))
    with _jax.named_scope("update"):
        delta_w, new_m, new_v = {}, {}, {}
        for n in TWIN_WEIGHTS:
            delta_w[n], new_m[n], new_v[n] = _adamw(weights[n], grad_w[n], given["m_" + n], given["v_" + n])
    return (loss, grad_x, *[grad_w[n] for n in TWIN_WEIGHTS], *[delta_w[n] for n in TWIN_WEIGHTS],
            *[new_m[n] for n in TWIN_WEIGHTS], *[new_v[n] for n in TWIN_WEIGHTS])
```

```python
import functools

import numpy as np
import jax
import jax.numpy as jnp
from jax import lax
from jax.experimental import pallas as pl
from jax.experimental.pallas import tpu as pltpu

F32 = jnp.float32
BF16 = jnp.bfloat16
MESH = pl.DeviceIdType.MESH
ANY = pl.BlockSpec(memory_space=pl.ANY)

EPS = 1e-6
CHUNK = 64
LEFT_CHUNKS = 8
N_HEADS = 8
HEAD_DIM = 64
D_ATTN = N_HEADS * HEAD_DIM
D_CONV = 512
CONV_WIDTH = 31
REL_CLIP = 128
N_CHIPS = 4
N_DEV = 8
Q_BLOCK = 4 * CHUNK
K_PAD = LEFT_CHUNKS * CHUNK
K_WIN = K_PAD + Q_BLOCK
REL_EXT = 1024
REL_PAD = 384
CONV_HALO = 32
CONV_TILE = 256
COL = 512
NEG = -1e30

ADAM_LR = 0.001
ADAM_B1 = 0.9
ADAM_B2 = 0.999
ADAM_EPS = 1e-08
ADAM_WD = 0.01
ADAM_STEP = 10

VMEM_LIMIT_BYTES = 48 * 1024 * 1024


def _cparams(n_grid):
    return pltpu.CompilerParams(dimension_semantics=("arbitrary",) * n_grid, vmem_limit_bytes=VMEM_LIMIT_BYTES)


def _row_tile(rows, want):
    if rows <= want:
        return rows
    for t in range(want - want % 16, 0, -16):
        if rows % t == 0:
            return t
    raise ValueError((rows, want))


def _dot(a, w, trans_w):
    dims = (((1,), (1,)), ((), ())) if trans_w else (((1,), (0,)), ((), ()))
    return lax.dot_general(a, w, dims, preferred_element_type=F32)


def _mm_nblk(name, a, w, *, trans_w, out_blocked, out_dtype, tm=1024):
    m, k = a.shape
    nj = w.shape[0]
    nb = w.shape[1] if trans_w else w.shape[2]
    tm = _row_tile(m, tm)

    def body(a_ref, w_ref, o_ref):
        o_ref[...] = _dot(a_ref[...], w_ref[...], trans_w).astype(o_ref.dtype)

    if out_blocked:
        out_shape, out_spec = (nj, m, nb), pl.BlockSpec((None, tm, nb), lambda j, i: (j, i, 0))
    else:
        out_shape, out_spec = (m, nj * nb), pl.BlockSpec((tm, nb), lambda j, i: (i, j))
    return pl.pallas_call(
        body, name=name, grid=(nj, m // tm),
        in_specs=[pl.BlockSpec((tm, k), lambda j, i: (i, 0)),
                  pl.BlockSpec((None,) + w.shape[1:], lambda j, i: (j, 0, 0))],
        out_specs=out_spec, out_shape=jax.ShapeDtypeStruct(out_shape, out_dtype),
        compiler_params=_cparams(2),
    )(a, w)


def _mm_kblk(name, pairs, *, trans_w, out_dtype, tm=1024):
    w0 = pairs[0][1]
    nj = w0.shape[0]
    n = w0.shape[1] if trans_w else w0.shape[2]
    kb = w0.shape[2] if trans_w else w0.shape[1]
    m = pairs[0][0].shape[-2]
    tm = _row_tile(m, tm)
    n_pairs = len(pairs)

    def body(*refs):
        o_ref, acc_ref = refs[2 * n_pairs], refs[2 * n_pairs + 1]
        j = pl.program_id(1)

        @pl.when(j == 0)
        def _():
            acc_ref[...] = jnp.zeros_like(acc_ref)

        part = _dot(refs[0][...], refs[1][...], trans_w)
        for p in range(1, n_pairs):
            part = part + _dot(refs[2 * p][...], refs[2 * p + 1][...], trans_w)
        acc_ref[...] += part

        @pl.when(j == nj - 1)
        def _():
            o_ref[...] = acc_ref[...].astype(o_ref.dtype)

    in_specs, args = [], []
    for a, w in pairs:
        if a.ndim == 3:
            in_specs.append(pl.BlockSpec((None, tm, kb), lambda i, j: (j, i, 0)))
        else:
            in_specs.append(pl.BlockSpec((tm, kb), lambda i, j: (i, j)))
        in_specs.append(pl.BlockSpec((None,) + w.shape[1:], lambda i, j: (j, 0, 0)))
        args += [a, w]
    return pl.pallas_call(
        body, name=name, grid=(m // tm, nj), in_specs=in_specs,
        out_specs=pl.BlockSpec((tm, n), lambda i, j: (i, 0)),
        out_shape=jax.ShapeDtypeStruct((m, n), out_dtype),
        scratch_shapes=[pltpu.VMEM((tm, n), F32)], compiler_params=_cparams(2),
    )(*args)


def _mm_tn(name, a, a_mode, b, b_mode, *, out_dtype=BF16, tt=1024):
    nj = N_CHIPS
    t = a.shape[-2]
    tt = _row_tile(t, tt)

    def spec(x, mode):
        if mode == "full":
            return x.shape[1], pl.BlockSpec((tt, x.shape[1]), lambda j, s: (s, 0))
        if mode == "col":
            cb = x.shape[1] // nj
            return cb, pl.BlockSpec((tt, cb), lambda j, s: (s, j))
        return x.shape[2], pl.BlockSpec((None, tt, x.shape[2]), lambda j, s: (j, s, 0))

    ca, a_spec = spec(a, a_mode)
    cb, b_spec = spec(b, b_mode)
    n_steps = t // tt

    def body(a_ref, b_ref, o_ref, acc_ref):
        s = pl.program_id(1)

        @pl.when(s == 0)
        def _():
            acc_ref[...] = jnp.zeros_like(acc_ref)

        acc_ref[...] += lax.dot_general(a_ref[...], b_ref[...], (((0,), (0,)), ((), ())),
                                        preferred_element_type=F32)

        @pl.when(s == n_steps - 1)
        def _():
            o_ref[...] = acc_ref[...].astype(o_ref.dtype)

    return pl.pallas_call(
        body, name=name, grid=(nj, n_steps), in_specs=[a_spec, b_spec],
        out_specs=pl.BlockSpec((None, ca, cb), lambda j, s: (j, 0, 0)),
        out_shape=jax.ShapeDtypeStruct((nj, ca, cb), out_dtype),
        scratch_shapes=[pltpu.VMEM((ca, cb), F32)], compiler_params=_cparams(2),
    )(a, b)


def _rowwise(name, fn, rows, vecs, row_outs, vec_outs, *, tm=256):
    nrows = rows[0][0].shape[0]
    tm = _row_tile(nrows, tm)
    n_r, n_v, n_ro, n_vo = len(rows), len(vecs), len(row_outs), len(vec_outs)

    def body(*refs):
        r_vals = [r[...] for r in refs[:n_r]]
        v_vals = [r[...] for r in refs[n_r:n_r + n_v]]
        ro_refs = refs[n_r + n_v:n_r + n_v + n_ro]
        vo_refs = refs[n_r + n_v + n_ro:]
        ro, vo = fn(r_vals, v_vals)
        for ref, val in zip(ro_refs, ro):
            ref[...] = val.astype(ref.dtype)
        if n_vo:
            @pl.when(pl.program_id(0) == 0)
            def _():
                for ref in vo_refs:
                    ref[...] = jnp.zeros_like(ref)

            for ref, val in zip(vo_refs, vo):
                ref[...] += val

    in_specs = [pl.BlockSpec((tm, cols), functools.partial(lambda i, cb: (i, cb), cb=cb)) for _, cols, cb in rows]
    in_specs += [pl.BlockSpec(v.shape, functools.partial(lambda i, nd: (0,) * nd, nd=v.ndim)) for v in vecs]
    out_specs = [pl.BlockSpec((tm, cols), lambda i: (i, 0)) for cols, _ in row_outs]
    out_specs += [pl.BlockSpec((1, cols), lambda i: (0, 0)) for cols in vec_outs]
    out_shape = [jax.ShapeDtypeStruct((nrows, cols), dt) for cols, dt in row_outs]
    out_shape += [jax.ShapeDtypeStruct((1, cols), F32) for cols in vec_outs]
    return pl.pallas_call(
        body, name=name, grid=(nrows // tm,), in_specs=in_specs, out_specs=out_specs, out_shape=out_shape,
        compiler_params=_cparams(1),
    )(*[r[0] for r in rows], *vecs)


def _whole(x):
    return (x, x.shape[1], 0)


def _colsum(x):
    return jnp.sum(x, axis=0, keepdims=True)


def _rstd(x):
    return lax.rsqrt(jnp.mean(x * x, axis=-1, keepdims=True) + EPS)


def _rms_bwd(dn, x, g):
    r = _rstd(x)
    c = dn * g
    dx = r * c - x * (r * r * r) * jnp.mean(c * x, axis=-1, keepdims=True)
    return dx, _colsum(dn * x * r)


def _rms_fwd(name, x, g):
    def fn(r, v):
        (xv,), (gv,) = r, v
        return [xv * _rstd(xv) * gv], []

    return _rowwise(name, fn, [_whole(x)], [g], [(x.shape[1], BF16)], [])[0]


def _swiglu_fwd(name, a, b):
    def fn(r, v):
        av, bv = r[0].astype(F32), r[1].astype(F32)
        return [av * jax.nn.sigmoid(av) * bv], []

    nj, t, fb = a.shape
    s = _rowwise(name, fn, [_whole(a.reshape(nj * t, fb)), _whole(b.reshape(nj * t, fb))], [], [(fb, BF16)], [])[0]
    return s.reshape(nj, t, fb)


def _swiglu_bwd(name, ds, a, b):
    def fn(r, v):
        dsv, av, bv = r[0].astype(F32), r[1].astype(F32), r[2].astype(F32)
        sig = jax.nn.sigmoid(av)
        da = dsv * bv * sig * (1.0 + av * (1.0 - sig))
        db = dsv * av * sig
        return [da, db], []

    nj, t, fb = a.shape
    flat = lambda x: _whole(x.reshape(nj * t, fb))
    da, db = _rowwise(name, fn, [flat(ds), flat(a), flat(b)], [], [(fb, BF16), (fb, BF16)], [])
    return da.reshape(nj, t, fb), db.reshape(nj, t, fb)


def _post_res_pre(name, f, resid, g_post, g_next, scale):
    def fn(r, v):
        fv, rv = r
        gp, gn = v
        h = rv + scale * (fv * _rstd(fv) * gp)
        return [h, h * _rstd(h) * gn], []

    d = f.shape[1]
    return _rowwise(name, fn, [_whole(f), _whole(resid)], [g_post, g_next], [(d, F32), (d, BF16)], [])


def _final_loss(name, f, resid, g_post, target, scale):
    d = f.shape[1]

    def fn(r, v):
        fv, rv, tv = r
        err = rv + scale * (fv * _rstd(fv) * v[0]) - tv
        return [err * (1.0 / d)], [_colsum(err * err)]

    return _rowwise(name, fn, [_whole(f), _whole(resid), _whole(target)], [g_post], [(d, F32)], [d])


def _post_bwd(dh, f, g_post, scale):
    return _rms_bwd(scale * dh, f, g_post)


def _post_bwd_call(name, dh, f, g_post, scale):
    def fn(r, v):
        df, dg = _post_bwd(r[0], r[1], v[0], scale)
        return [df], [dg]

    d = f.shape[1]
    return _rowwise(name, fn, [_whole(dh), _whole(f)], [g_post], [(d, BF16)], [d])


def _pre_bwd_post(name, dn, h, g_pre, dh_up, f_prev, g_post_prev, scale_prev):
    def fn(r, v):
        dnv, hv, upv, fv = r
        dx, dg_pre = _rms_bwd(dnv, hv, v[0])
        dh = upv + dx
        df, dg_post = _post_bwd(dh, fv, v[1], scale_prev)
        return [dh, df], [dg_pre, dg_post]

    d = h.shape[1]
    return _rowwise(name, fn, [_whole(dn), _whole(h), _whole(dh_up), _whole(f_prev)], [g_pre, g_post_prev],
                    [(d, F32), (d, BF16)], [d, d])


def _pre_bwd_first(name, dn, x, g_pre, dh_up):
    def fn(r, v):
        dx, dg_pre = _rms_bwd(r[0], r[1], v[0])
        return [r[2] + dx], [dg_pre]

    d = x.shape[1]
    return _rowwise(name, fn, [_whole(dn), _whole(x), _whole(dh_up)], [g_pre], [(d, F32)], [d])


def _gate_pieces(proj, d):
    first = (3 * D_ATTN + 2 * D_CONV) // COL
    n = d // COL
    return [(proj, COL, first + p) for p in range(2 * n)], n


def _merge_fwd(name, y_a, y_b, proj, gate_bias):
    d = y_a.shape[1]
    pieces, n = _gate_pieces(proj, d)

    def fn(r, v):
        ya, yb = r[0], r[1]
        g = jnp.concatenate([p.astype(F32) for p in r[2:]], axis=1) + v[0]
        gates = jax.nn.sigmoid(g)
        return [gates[:, :d] * ya + gates[:, d:] * yb], []

    return _rowwise(name, fn, [_whole(y_a), _whole(y_b)] + pieces, [gate_bias], [(d, BF16)], [])[0]


def _merge_bwd(name, dmerged, y_a, y_b, proj, gate_bias):
    d = y_a.shape[1]
    pieces, n = _gate_pieces(proj, d)

    def fn(r, v):
        dm, ya, yb = r[0], r[1], r[2]
        g = jnp.concatenate([p.astype(F32) for p in r[3:]], axis=1) + v[0]
        gates = jax.nn.sigmoid(g)
        ga, gb = gates[:, :d], gates[:, d:]
        dga = dm * ya * ga * (1.0 - ga)
        dgb = dm * yb * gb * (1.0 - gb)
        dgate = jnp.concatenate([dga, dgb], axis=1)
        return [dm * ga, dm * gb, dgate], [_colsum(dgate)]

    return _rowwise(name, fn, [_whole(dmerged), _whole(y_a), _whole(y_b)] + pieces, [gate_bias],
                    [(d, BF16), (d, BF16), (2 * d, BF16)], [2 * d])


def _adamw(name, w, g, m, v):
    def fn(r, _):
        wv, gv, mv, vv = r
        m2 = ADAM_B1 * mv + (1.0 - ADAM_B1) * gv
        v2 = ADAM_B2 * vv + (1.0 - ADAM_B2) * (gv * gv)
        m_hat = m2 / (1.0 - ADAM_B1 ** ADAM_STEP)
        v_hat = v2 / (1.0 - ADAM_B2 ** ADAM_STEP)
        delta = -ADAM_LR * (m_hat / (jnp.sqrt(v_hat) + ADAM_EPS) + ADAM_WD * wv)
        return [delta, m2, v2], []

    c = w.shape[1]
    return _rowwise(name, fn, [_whole(w), _whole(g), _whole(m), _whole(v)], [], [(c, F32)] * 3, [], tm=256)


def _add2(name, a, b):
    def fn(r, _):
        return [r[0].astype(F32) + r[1].astype(F32)], []

    shp = a.shape
    flat = lambda x: _whole(x.reshape(-1, shp[-1]))
    return _rowwise(name, fn, [flat(a), flat(b)], [], [(shp[-1], BF16)], [], tm=256)[0].reshape(shp)


def _add_chips(name, landed):
    nj, half, c = landed.shape
    tm = _row_tile(half, 256)

    def body(l_ref, o_ref):
        acc = l_ref[0].astype(F32)
        for s in range(1, nj):
            acc = acc + l_ref[s].astype(F32)
        o_ref[...] = acc

    return pl.pallas_call(
        body, name=name, grid=(half // tm,),
        in_specs=[pl.BlockSpec((nj, tm, c), lambda i: (0, i, 0))],
        out_specs=pl.BlockSpec((tm, c), lambda i: (i, 0)),
        out_shape=jax.ShapeDtypeStruct((half, c), F32), compiler_params=_cparams(1),
    )(landed)


def _rel_onehot():
    e = np.arange(REL_EXT)
    dist = K_PAD - (e - (Q_BLOCK - 1))
    idx = np.clip(dist, -REL_CLIP, REL_CLIP) + REL_CLIP
    return (np.arange(REL_PAD)[:, None] == idx[None, :]).astype(np.float32)


def _band_valid():
    qc = lax.broadcasted_iota(jnp.int32, (Q_BLOCK, K_WIN), 0) // CHUNK
    kc = lax.broadcasted_iota(jnp.int32, (Q_BLOCK, K_WIN), 1) // CHUNK
    return (kc >= qc) & (kc <= qc + LEFT_CHUNKS)


def _skew(x, left):
    row = lax.broadcasted_iota(jnp.int32, x.shape, 0)
    for bit in range(Q_BLOCK.bit_length() - 1):
        amount = 1 << bit
        rolled = pltpu.roll(x, REL_EXT - amount if left else amount, 1)
        x = jnp.where((row >> bit) & 1 == 1, rolled, x)
    return x


def _bias_expand(table_pad):
    onehot = jnp.asarray(_rel_onehot())

    def body(t_ref, oh_ref, o_ref):
        ext = jnp.dot(t_ref[...], oh_ref[...], precision=lax.Precision.HIGHEST, preferred_element_type=F32)
        valid = _band_valid()
        for h in range(N_HEADS):
            rows = jnp.broadcast_to(ext[h:h + 1, :], (Q_BLOCK, REL_EXT))
            rolled = _skew(pltpu.roll(rows, REL_EXT - (Q_BLOCK - 1), 1), left=False)
            o_ref[h] = jnp.where(valid, rolled[:, :K_WIN], NEG)

    return pl.pallas_call(
        body, name="bias_expand", out_shape=jax.ShapeDtypeStruct((N_HEADS, Q_BLOCK, K_WIN), F32),
        compiler_params=pltpu.CompilerParams(vmem_limit_bytes=VMEM_LIMIT_BYTES),
    )(table_pad, onehot)


def _bias_fold(dbias):
    onehot_t = jnp.asarray(_rel_onehot().T)

    def body(d_ref, oh_ref, o_ref, ext_ref):
        for h in range(N_HEADS):
            x = jnp.concatenate([d_ref[h], jnp.zeros((Q_BLOCK, REL_EXT - K_WIN), F32)], axis=1)
            rolled = _skew(pltpu.roll(x, Q_BLOCK - 1, 1), left=True)
            ext_ref[h:h + 1, :] = jnp.sum(rolled, axis=0, keepdims=True)
        o_ref[...] = jnp.dot(ext_ref[...], oh_ref[...], precision=lax.Precision.HIGHEST,
                             preferred_element_type=F32)

    return pl.pallas_call(
        body, name="bias_fold", out_shape=jax.ShapeDtypeStruct((N_HEADS, REL_PAD), F32),
        scratch_shapes=[pltpu.VMEM((N_HEADS, REL_EXT), F32)],
        compiler_params=pltpu.CompilerParams(vmem_limit_bytes=VMEM_LIMIT_BYTES),
    )(dbias, onehot_t)


def _scores(q, k, bias, i):
    s = lax.dot_general(q, k, (((1,), (1,)), ((), ())), preferred_element_type=F32) * (HEAD_DIM ** -0.5) + bias
    kpos = lax.broadcasted_iota(jnp.int32, (Q_BLOCK, K_WIN), 1) + i * Q_BLOCK
    return jnp.where(kpos >= K_PAD, s, NEG)


def _attn_specs(n_kv):
    q_spec = pl.BlockSpec((Q_BLOCK, 2 * HEAD_DIM), lambda p, i: (i, p))
    kv_specs = [pl.BlockSpec((Q_BLOCK, 2 * HEAD_DIM), functools.partial(lambda p, i, kk: (i + kk, p), kk=kk))
                for _ in range(n_kv) for kk in range(K_WIN // Q_BLOCK)]
    bias_spec = pl.BlockSpec((2, Q_BLOCK, K_WIN), lambda p, i: (p, 0, 0))
    return q_spec, kv_specs, bias_spec


def _attn_fwd(proj, kp, vp, bias):
    t = proj.shape[0]
    n_win = K_WIN // Q_BLOCK

    def body(q_ref, *refs):
        k_refs, v_refs = refs[:n_win], refs[n_win:2 * n_win]
        b_ref, o_ref, lse_ref = refs[2 * n_win:]
        i = pl.program_id(1)
        k = jnp.concatenate([r[...] for r in k_refs], axis=0)
        v = jnp.concatenate([r[...] for r in v_refs], axis=0)
        q = q_ref[...]
        for hh in range(2):
            lanes = slice(hh * HEAD_DIM, (hh + 1) * HEAD_DIM)
            s = _scores(q[:, lanes], k[:, lanes], b_ref[hh], i)
            m = jnp.max(s, axis=1, keepdims=True)
            p = jnp.exp(s - m)
            l = jnp.sum(p, axis=1, keepdims=True)
            o = jnp.dot(p.astype(BF16), v[:, lanes], preferred_element_type=F32) / l
            o_ref[:, lanes] = o.astype(o_ref.dtype)
            lse_ref[:, lanes] = jnp.broadcast_to(m + jnp.log(l), (Q_BLOCK, HEAD_DIM))

    q_spec, kv_specs, bias_spec = _attn_specs(2)
    out_spec = pl.BlockSpec((Q_BLOCK, 2 * HEAD_DIM), lambda p, i: (i, p))
    return pl.pallas_call(
        body, name="attn_fwd", grid=(N_HEADS // 2, t // Q_BLOCK),
        in_specs=[q_spec] + kv_specs + [bias_spec], out_specs=[out_spec, out_spec],
        out_shape=[jax.ShapeDtypeStruct((t, D_ATTN), BF16), jax.ShapeDtypeStruct((t, D_ATTN), F32)],
        compiler_params=_cparams(2),
    )(proj, *([kp] * n_win), *([vp] * n_win), bias)


def _attn_bwd(proj, kp, vp, bias, att, lse, datt):
    t = proj.shape[0]
    n_win = K_WIN // Q_BLOCK

    def body(q_ref, *refs):
        k_refs, v_refs = refs[:n_win], refs[n_win:2 * n_win]
        b_ref, o_ref, lse_ref, do_ref, dq_ref, dk_ref, dv_ref, db_ref = refs[2 * n_win:]
        i = pl.program_id(1)

        @pl.when(i == 0)
        def _():
            dk_ref[...] = jnp.zeros_like(dk_ref)
            dv_ref[...] = jnp.zeros_like(dv_ref)
            db_ref[...] = jnp.zeros_like(db_ref)

        k = jnp.concatenate([r[...] for r in k_refs], axis=0)
        v = jnp.concatenate([r[...] for r in v_refs], axis=0)
        q = q_ref[...]
        do = do_ref[...]
        rows = pl.ds(pl.multiple_of(i * Q_BLOCK, Q_BLOCK), K_WIN)
        scale = HEAD_DIM ** -0.5
        for hh in range(2):
            lanes = slice(hh * HEAD_DIM, (hh + 1) * HEAD_DIM)
            qh, kh, vh, doh = q[:, lanes], k[:, lanes], v[:, lanes], do[:, lanes]
            s = _scores(qh, kh, b_ref[hh], i)
            p = jnp.exp(s - lse_ref[:, lanes][:, :1])
            dp = lax.dot_general(doh, vh, (((1,), (1,)), ((), ())), preferred_element_type=F32)
            delta = jnp.sum(doh.astype(F32) * o_ref[:, lanes].astype(F32), axis=1, keepdims=True)
            ds = p * (dp - delta)
            db_ref[hh] += ds
            dsb = ds.astype(BF16)
            dq_ref[:, lanes] = (jnp.dot(dsb, kh, preferred_element_type=F32) * scale).astype(dq_ref.dtype)
            dk_ref[rows, lanes] += lax.dot_general(dsb, qh, (((0,), (0,)), ((), ())),
                                                   preferred_element_type=F32) * scale
            dv_ref[rows, lanes] += lax.dot_general(p.astype(BF16), doh, (((0,), (0,)), ((), ())),
                                                   preferred_element_type=F32)

    q_spec, kv_specs, bias_spec = _attn_specs(2)
    row_spec = pl.BlockSpec((Q_BLOCK, 2 * HEAD_DIM), lambda p, i: (i, p))
    full_spec = pl.BlockSpec((t + K_PAD, 2 * HEAD_DIM), lambda p, i: (0, p))
    return pl.pallas_call(
        body, name="attn_bwd", grid=(N_HEADS // 2, t // Q_BLOCK),
        in_specs=[q_spec] + kv_specs + [bias_spec, row_spec, row_spec, row_spec],
        out_specs=[row_spec, full_spec, full_spec, bias_spec],
        out_shape=[jax.ShapeDtypeStruct((t, D_ATTN), BF16), jax.ShapeDtypeStruct((t + K_PAD, D_ATTN), F32),
                   jax.ShapeDtypeStruct((t + K_PAD, D_ATTN), F32),
                   jax.ShapeDtypeStruct((N_HEADS, Q_BLOCK, K_WIN), F32)],
        compiler_params=_cparams(2),
    )(proj, *([kp] * n_win), *([vp] * n_win), bias, att, lse, datt)


CONV_LEAD = CONV_HALO - (CONV_WIDTH - 1)
CONV_LANES = 128
CONV_ROWS = 64


def _conv_specs(t):
    tt = _row_tile(t, CONV_TILE)
    per = tt // CONV_HALO
    n_halo = t // CONV_HALO
    tile = lambda cb: pl.BlockSpec((tt, COL), functools.partial(lambda i, cb: (i, cb), cb=cb))
    prev = lambda cb: pl.BlockSpec((CONV_HALO, COL),
                                   functools.partial(lambda i, cb: (jnp.maximum(i * per - 1, 0), cb), cb=cb))
    nxt = lambda cb: pl.BlockSpec((CONV_HALO, COL),
                                  functools.partial(lambda i, cb: (jnp.minimum((i + 1) * per, n_halo - 1), cb), cb=cb))
    vec = lambda shape: pl.BlockSpec(shape, lambda i: (0, 0))
    return tt, tile, prev, nxt, vec


def _glu(ca, cg, bias):
    return (ca.astype(F32) + bias[:, :D_CONV]) * jax.nn.sigmoid(cg.astype(F32) + bias[:, D_CONV:])


def _taps(ext_ref, tt, first_row, weight_of, out_ref):
    for r0 in range(0, tt, CONV_ROWS):
        for l0 in range(0, D_CONV, CONV_LANES):
            lanes = slice(l0, l0 + CONV_LANES)
            acc = jnp.zeros((CONV_ROWS, CONV_LANES), F32)
            for w in range(CONV_WIDTH):
                acc = acc + ext_ref[first_row(w) + r0:first_row(w) + r0 + CONV_ROWS, lanes] * weight_of(w)[:, lanes]
            out_ref[r0:r0 + CONV_ROWS, lanes] = acc


def _conv_fwd(proj, glu_bias, dw, dw_b, ln_g, ln_b):
    t = proj.shape[0]
    tt, tile, prev, nxt, vec = _conv_specs(t)
    ca_blk, cg_blk = 3 * D_ATTN // COL, 3 * D_ATTN // COL + 1

    def body(ca_ref, cg_ref, pa_ref, pg_ref, gb_ref, dw_ref, dwb_ref, g_ref, b_ref, cs_ref, c_ref, z_ref, ext_ref):
        i = pl.program_id(0)
        bias = gb_ref[...]
        c = _glu(ca_ref[...], cg_ref[...], bias)
        halo = _glu(pa_ref[...], pg_ref[...], bias)
        ext_ref[0:CONV_HALO, :] = jnp.where(i == 0, 0.0, halo)
        ext_ref[CONV_HALO:, :] = c
        c_ref[...] = c
        _taps(ext_ref, tt, lambda w: CONV_LEAD + w, lambda w: dw_ref[w:w + 1, :], z_ref)
        z = z_ref[...] + dwb_ref[...]
        z_ref[...] = z
        mu = jnp.mean(z, axis=-1, keepdims=True)
        zc = z - mu
        y = zc * lax.rsqrt(jnp.mean(zc * zc, axis=-1, keepdims=True) + EPS) * g_ref[...] + b_ref[...]
        cs_ref[...] = (y * jax.nn.sigmoid(y)).astype(cs_ref.dtype)

    out_spec = pl.BlockSpec((tt, D_CONV), lambda i: (i, 0))
    return pl.pallas_call(
        body, name="conv_fwd", grid=(t // tt,),
        in_specs=[tile(ca_blk), tile(cg_blk), prev(ca_blk), prev(cg_blk), vec(glu_bias.shape), vec(dw.shape),
                  vec(dw_b.shape), vec(ln_g.shape), vec(ln_b.shape)],
        out_specs=[out_spec] * 3,
        out_shape=[jax.ShapeDtypeStruct((t, D_CONV), BF16), jax.ShapeDtypeStruct((t, D_CONV), F32),
                   jax.ShapeDtypeStruct((t, D_CONV), F32)],
        scratch_shapes=[pltpu.VMEM((tt + CONV_HALO, D_CONV), F32)], compiler_params=_cparams(1),
    )(proj, proj, proj, proj, glu_bias, dw, dw_b, ln_g, ln_b)


def _conv_bwd(proj, c, z, dcs, glu_bias, dw, ln_g, ln_b):
    t = proj.shape[0]
    tt, tile, prev, nxt, vec = _conv_specs(t)
    n_tiles = t // tt
    ca_blk, cg_blk = 3 * D_ATTN // COL, 3 * D_ATTN // COL + 1

    def ln_bwd(zv, dcsv, g, b):
        mu = jnp.mean(zv, axis=-1, keepdims=True)
        zc = zv - mu
        rstd = lax.rsqrt(jnp.mean(zc * zc, axis=-1, keepdims=True) + EPS)
        zhat = zc * rstd
        y = zhat * g + b
        sig = jax.nn.sigmoid(y)
        dy = dcsv * sig * (1.0 + y * (1.0 - sig))
        dzh = dy * g
        dz = rstd * (dzh - jnp.mean(dzh, axis=-1, keepdims=True) - zhat * jnp.mean(dzh * zhat, axis=-1, keepdims=True))
        return dz, dy, zhat

    def body(ca_ref, cg_ref, c_ref, cprev_ref, z_ref, znext_ref, dcs_ref, dcsnext_ref, gb_ref, dw_ref, g_ref, b_ref,
             dcin_ref, ddw_ref, ddwb_ref, dg_ref, db_ref, dgb_ref, cext_ref, dzext_ref, dc_ref):
        i = pl.program_id(0)

        @pl.when(i == 0)
        def _():
            for ref in (ddw_ref, ddwb_ref, dg_ref, db_ref, dgb_ref):
                ref[...] = jnp.zeros_like(ref)

        g, b = g_ref[...], b_ref[...]
        dz, dy, zhat = ln_bwd(z_ref[...], dcs_ref[...], g, b)
        dz_next, _, _ = ln_bwd(znext_ref[...], dcsnext_ref[...], g, b)
        dg_ref[...] += _colsum(dy * zhat)
        db_ref[...] += _colsum(dy)
        ddwb_ref[...] += _colsum(dz)
        dzext_ref[0:tt, :] = dz
        dzext_ref[tt:, :] = jnp.where(i == n_tiles - 1, 0.0, dz_next)
        cext_ref[0:CONV_HALO, :] = jnp.where(i == 0, 0.0, cprev_ref[...])
        cext_ref[CONV_HALO:, :] = c_ref[...]
        _taps(dzext_ref, tt, lambda w: CONV_WIDTH - 1 - w, lambda w: dw_ref[w:w + 1, :], dc_ref)
        for w in range(CONV_WIDTH):
            ddw_ref[w:w + 1, :] += _colsum(cext_ref[CONV_LEAD + w:CONV_LEAD + w + tt, :] * dz)
        bias = gb_ref[...]
        a_in = ca_ref[...].astype(F32) + bias[:, :D_CONV]
        sg = jax.nn.sigmoid(cg_ref[...].astype(F32) + bias[:, D_CONV:])
        dc = dc_ref[...]
        dcin = jnp.concatenate([dc * sg, dc * a_in * sg * (1.0 - sg)], axis=1)
        dcin_ref[...] = dcin.astype(dcin_ref.dtype)
        dgb_ref[...] += _colsum(dcin)

    row = lambda: pl.BlockSpec((tt, D_CONV), lambda i: (i, 0))
    per = tt // CONV_HALO
    n_halo = t // CONV_HALO
    prev_row = pl.BlockSpec((CONV_HALO, D_CONV), lambda i: (jnp.maximum(i * per - 1, 0), 0))
    next_row = lambda: pl.BlockSpec((CONV_HALO, D_CONV), lambda i: (jnp.minimum((i + 1) * per, n_halo - 1), 0))
    acc = lambda shape: pl.BlockSpec(shape, lambda i: (0, 0))
    return pl.pallas_call(
        body, name="conv_bwd", grid=(n_tiles,),
        in_specs=[tile(ca_blk), tile(cg_blk), row(), prev_row, row(), next_row(), row(), next_row(),
                  vec(glu_bias.shape), vec(dw.shape), vec(ln_g.shape), vec(ln_b.shape)],
        out_specs=[pl.BlockSpec((tt, 2 * D_CONV), lambda i: (i, 0)), acc(dw.shape), acc((1, D_CONV)),
                   acc((1, D_CONV)), acc((1, D_CONV)), acc((1, 2 * D_CONV))],
        out_shape=[jax.ShapeDtypeStruct((t, 2 * D_CONV), BF16), jax.ShapeDtypeStruct(dw.shape, F32),
                   jax.ShapeDtypeStruct((1, D_CONV), F32), jax.ShapeDtypeStruct((1, D_CONV), F32),
                   jax.ShapeDtypeStruct((1, D_CONV), F32), jax.ShapeDtypeStruct((1, 2 * D_CONV), F32)],
        scratch_shapes=[pltpu.VMEM((tt + CONV_HALO, D_CONV), F32), pltpu.VMEM((tt + CONV_HALO, D_CONV), F32),
                        pltpu.VMEM((tt, D_CONV), F32)],
        compiler_params=_cparams(1),
    )(proj, proj, c, c, z, z, dcs, dcs, glu_bias, dw, ln_g, ln_b)


def _place():
    x, y, c = lax.axis_index("x"), lax.axis_index("y"), lax.axis_index("c")
    chips = [(1 - x, y), (x, 1 - y), (1 - x, 1 - y)]
    return x, y, c, chips


def _chip_index(chip):
    return 2 * chip[0] + chip[1]


def _half_rows(c, half):
    return pl.ds(pl.multiple_of(c * half, 16), half)


def _gather_weights(shards):
    n = len(shards)

    def body(*refs):
        w_refs, o_refs = refs[:n], refs[n:2 * n]
        send_sems, recv_sems, local_sems = refs[2 * n:]
        x, y, c, chips = _place()
        me = _chip_index((x, y))
        sibling = (x, y, 1 - c)
        local, first, passed = [], [], []
        for a in range(n):
            half = w_refs[a].shape[0] // 2
            mine = _half_rows(c, half)
            cp = pltpu.make_async_copy(w_refs[a], o_refs[a].at[me], local_sems.at[a])
            cp.start()
            local.append(cp)
            for k, chip in enumerate(chips):
                cp = pltpu.make_async_remote_copy(
                    src_ref=w_refs[a].at[mine, :], dst_ref=o_refs[a].at[me, mine, :],
                    send_sem=send_sems.at[6 * a + k], recv_sem=recv_sems.at[6 * a + k],
                    device_id=(chip[0], chip[1], c), device_id_type=MESH)
                cp.start()
                first.append(cp)
        for a in range(n):
            half = w_refs[a].shape[0] // 2
            mine = _half_rows(c, half)
            for k, chip in enumerate(chips):
                landed = o_refs[a].at[_chip_index(chip), mine, :]
                cp = pltpu.make_async_remote_copy(
                    src_ref=landed, dst_ref=landed, send_sem=send_sems.at[6 * a + 3 + k],
                    recv_sem=recv_sems.at[6 * a + 3 + k], device_id=sibling, device_id_type=MESH)
                first[3 * a + k].wait_recv()
                cp.start()
                passed.append(cp)
        for a in range(n):
            half = w_refs[a].shape[0] // 2
            theirs = _half_rows(1 - c, half)
            for k, chip in enumerate(chips):
                landed = o_refs[a].at[_chip_index(chip), theirs, :]
                pltpu.make_async_remote_copy(
                    src_ref=landed, dst_ref=landed, send_sem=send_sems.at[6 * a + 3 + k],
                    recv_sem=recv_sems.at[6 * a + 3 + k], device_id=sibling, device_id_type=MESH).wait_recv()
        for cp in first + passed:
            cp.wait_send()
        for cp in local:
            cp.wait()

    return pl.pallas_call(
        body, name="gather_weights", in_specs=[ANY] * n, out_specs=[ANY] * n,
        out_shape=[jax.ShapeDtypeStruct((N_CHIPS,) + w.shape, w.dtype) for w in shards],
        scratch_shapes=[pltpu.SemaphoreType.DMA((6 * n,)), pltpu.SemaphoreType.DMA((6 * n,)),
                        pltpu.SemaphoreType.DMA((n,))],
    )(*shards)


def _pair_exchange(grads):
    n = len(grads)

    def body(*refs):
        g_refs, mine_refs, land_refs = refs[:n], refs[n:2 * n], refs[2 * n:3 * n]
        send_sems, recv_sems, local_sems = refs[3 * n:]
        x, y, c, _ = _place()
        copies = []
        for a in range(n):
            half = g_refs[a].shape[1] // 2
            local = pltpu.make_async_copy(g_refs[a].at[:, _half_rows(c, half), :], mine_refs[a], local_sems.at[a])
            remote = pltpu.make_async_remote_copy(
                src_ref=g_refs[a].at[:, _half_rows(1 - c, half), :], dst_ref=land_refs[a],
                send_sem=send_sems.at[a], recv_sem=recv_sems.at[a], device_id=(x, y, 1 - c), device_id_type=MESH)
            local.start()
            remote.start()
            copies += [local, remote]
        for cp in copies:
            cp.wait()

    halves = [jax.ShapeDtypeStruct((g.shape[0], g.shape[1] // 2, g.shape[2]), g.dtype) for g in grads]
    out = pl.pallas_call(
        body, name="grad_pair_exchange", in_specs=[ANY] * n, out_specs=[ANY] * (2 * n), out_shape=halves + halves,
        scratch_shapes=[pltpu.SemaphoreType.DMA((n,)), pltpu.SemaphoreType.DMA((n,)), pltpu.SemaphoreType.DMA((n,))],
    )(*grads)
    return out[:n], out[n:]


def _to_owner(parts):
    n = len(parts)

    def body(*refs):
        p_refs, l_refs = refs[:n], refs[n:2 * n]
        send_sems, recv_sems, local_sems = refs[2 * n:]
        x, y, c, chips = _place()
        me = _chip_index((x, y))
        copies = []
        for a in range(n):
            local = pltpu.make_async_copy(p_refs[a].at[me], l_refs[a].at[me], local_sems.at[a])
            local.start()
            copies.append(local)
            for k, chip in enumerate(chips):
                remote = pltpu.make_async_remote_copy(
                    src_ref=p_refs[a].at[_chip_index(chip)], dst_ref=l_refs[a].at[me],
                    send_sem=send_sems.at[3 * a + k], recv_sem=recv_sems.at[3 * a + k],
                    device_id=(chip[0], chip[1], c), device_id_type=MESH)
                remote.start()
                copies.append(remote)
        for a in range(n):
            for k, chip in enumerate(chips):
                slot = l_refs[a].at[_chip_index(chip)]
                pltpu.make_async_remote_copy(
                    src_ref=slot, dst_ref=slot, send_sem=send_sems.at[3 * a + k], recv_sem=recv_sems.at[3 * a + k],
                    device_id=(chip[0], chip[1], c), device_id_type=MESH).wait_recv()
        for a in range(n):
            copies[4 * a].wait()
            for k in range(3):
                copies[4 * a + 1 + k].wait_send()

    return pl.pallas_call(
        body, name="grad_to_owner", in_specs=[ANY] * n, out_specs=[ANY] * n,
        out_shape=[jax.ShapeDtypeStruct(p.shape, p.dtype) for p in parts],
        scratch_shapes=[pltpu.SemaphoreType.DMA((3 * n,)), pltpu.SemaphoreType.DMA((3 * n,)),
                        pltpu.SemaphoreType.DMA((n,))],
    )(*parts)


def _join_halves(halves):
    n = len(halves)

    def body(*refs):
        h_refs, o_refs = refs[:n], refs[n:2 * n]
        send_sems, recv_sems, local_sems = refs[2 * n:]
        x, y, c, _ = _place()
        copies = []
        for a in range(n):
            half = h_refs[a].shape[0]
            rows = o_refs[a].at[_half_rows(c, half), :]
            local = pltpu.make_async_copy(h_refs[a], rows, local_sems.at[a])
            remote = pltpu.make_async_remote_copy(
                src_ref=h_refs[a], dst_ref=rows, send_sem=send_sems.at[a], recv_sem=recv_sems.at[a],
                device_id=(x, y, 1 - c), device_id_type=MESH)
            local.start()
            remote.start()
            copies += [local, remote]
        for a in range(n):
            half = h_refs[a].shape[0]
            theirs = o_refs[a].at[_half_rows(1 - c, half), :]
            copies[2 * a].wait()
            copies[2 * a + 1].wait_send()
            pltpu.make_async_remote_copy(
                src_ref=theirs, dst_ref=theirs, send_sem=send_sems.at[a], recv_sem=recv_sems.at[a],
                device_id=(x, y, 1 - c), device_id_type=MESH).wait_recv()

    return pl.pallas_call(
        body, name="grad_join_halves", in_specs=[ANY] * n, out_specs=[ANY] * n,
        out_shape=[jax.ShapeDtypeStruct((2 * h.shape[0], h.shape[1]), h.dtype) for h in halves],
        scratch_shapes=[pltpu.SemaphoreType.DMA((n,)), pltpu.SemaphoreType.DMA((n,)), pltpu.SemaphoreType.DMA((n,))],
    )(*halves)


def _all_devices(name, block):
    r, cols = block.shape

    def body(b_ref, all_ref, sum_ref, send_sems, recv_sems):
        x, y, c, _ = _place()
        me = 4 * x + 2 * y + c
        all_ref[me] = b_ref[...]
        flips = [(fx, fy, fc) for fx in (0, 1) for fy in (0, 1) for fc in (0, 1)][1:]
        copies = []
        for k, (fx, fy, fc) in enumerate(flips):
            cp = pltpu.make_async_remote_copy(
                src_ref=b_ref, dst_ref=all_ref.at[me], send_sem=send_sems.at[k], recv_sem=recv_sems.at[k],
                device_id=(x ^ fx, y ^ fy, c ^ fc), device_id_type=MESH)
            cp.start()
            copies.append(cp)
        for k, (fx, fy, fc) in enumerate(flips):
            slot = all_ref.at[4 * (x ^ fx) + 2 * (y ^ fy) + (c ^ fc)]
            pltpu.make_async_remote_copy(
                src_ref=slot, dst_ref=slot, send_sem=send_sems.at[k], recv_sem=recv_sems.at[k],
                device_id=(x ^ fx, y ^ fy, c ^ fc), device_id_type=MESH).wait_recv()
        for cp in copies:
            cp.wait_send()
        acc = all_ref[0]
        for d in range(1, N_DEV):
            acc = acc + all_ref[d]
        sum_ref[...] = acc

    vmem = pl.BlockSpec(memory_space=pltpu.VMEM)
    return pl.pallas_call(
        body, name=name, in_specs=[vmem], out_specs=[vmem, vmem],
        out_shape=[jax.ShapeDtypeStruct((N_DEV, r, cols), F32), jax.ShapeDtypeStruct((r, cols), F32)],
        scratch_shapes=[pltpu.SemaphoreType.DMA((N_DEV - 1,)), pltpu.SemaphoreType.DMA((N_DEV - 1,))],
    )(block)


PACK = 1024


def _packed_rows(shape):
    size, last = int(np.prod(shape)), shape[-1]
    cols = last if last <= PACK else PACK
    assert size % cols == 0
    return size // cols, cols


def _pack(vals):
    rows = []
    for v in vals:
        n_rows, cols = _packed_rows(v.shape)
        v = v.reshape(n_rows, cols).astype(F32)
        rows.append(jnp.pad(v, ((0, (-n_rows) % 8), (0, PACK - cols))))
    return jnp.concatenate(rows, axis=0)


def _unpack(buf, shapes):
    out, r = [], 0
    for shape in shapes:
        n_rows, cols = _packed_rows(shape)
        out.append(buf[r:r + n_rows, :cols].reshape(shape))
        r += n_rows + (-n_rows) % 8
    return out


def _ffn_fwd(tag, n, wg, wu, wd):
    a = _mm_nblk(tag + "_gate", n, wg, trans_w=False, out_blocked=True, out_dtype=BF16)
    b = _mm_nblk(tag + "_up", n, wu, trans_w=False, out_blocked=True, out_dtype=BF16)
    s = _swiglu_fwd(tag + "_swiglu", a, b)
    f = _mm_kblk(tag + "_down", [(s, wd)], trans_w=False, out_dtype=F32)
    return a, b, s, f


def _ffn_bwd(tag, df, n, a, b, s, wg, wu, wd):
    ds = _mm_nblk(tag + "_d_s", df, wd, trans_w=True, out_blocked=True, out_dtype=BF16)
    da, db = _swiglu_bwd(tag + "_d_swiglu", ds, a, b)
    g_wd = _mm_tn(tag + "_g_down", s, "blk", df, "full")
    g_wg = _mm_tn(tag + "_g_gate", n, "full", da, "blk")
    g_wu = _mm_tn(tag + "_g_up", n, "full", db, "blk")
    dn = _mm_kblk(tag + "_d_n", [(da, wg), (db, wu)], trans_w=True, out_dtype=F32)
    return dn, g_wg, g_wu, g_wd


def kernel(x, ffn1_norm_pre, ffn1_w_gate, ffn1_w_up, ffn1_w_down, ffn1_norm_post, mix_norm_pre, w_in, gate_bias, rel_table, w_attn_out, conv_glu_bias, conv_dw_w, conv_dw_b, conv_ln_g, conv_ln_b, conv_w_out, w_out, mix_norm_post, ffn2_norm_pre, ffn2_w_gate, ffn2_w_up, ffn2_w_down, ffn2_norm_post, loss_target, m_ffn1_norm_pre, m_ffn1_w_gate, m_ffn1_w_up, m_ffn1_w_down, m_ffn1_norm_post, m_mix_norm_pre, m_w_in, m_gate_bias, m_rel_table, m_w_attn_out, m_conv_glu_bias, m_conv_dw_w, m_conv_dw_b, m_conv_ln_g, m_conv_ln_b, m_conv_w_out, m_w_out, m_mix_norm_post, m_ffn2_norm_pre, m_ffn2_w_gate, m_ffn2_w_up, m_ffn2_w_down, m_ffn2_norm_post, v_ffn1_norm_pre, v_ffn1_w_gate, v_ffn1_w_up, v_ffn1_w_down, v_ffn1_norm_post, v_mix_norm_pre, v_w_in, v_gate_bias, v_rel_table, v_w_attn_out, v_conv_glu_bias, v_conv_dw_w, v_conv_dw_b, v_conv_ln_g, v_conv_ln_b, v_conv_w_out, v_w_out, v_mix_norm_post, v_ffn2_norm_pre, v_ffn2_w_gate, v_ffn2_w_up, v_ffn2_w_down, v_ffn2_norm_post):
    args = dict(locals())
    names = ['ffn1_norm_pre', 'ffn1_w_gate', 'ffn1_w_up', 'ffn1_w_down', 'ffn1_norm_post', 'mix_norm_pre', 'w_in',
             'gate_bias', 'rel_table', 'w_attn_out', 'conv_glu_bias', 'conv_dw_w', 'conv_dw_b', 'conv_ln_g',
             'conv_ln_b', 'conv_w_out', 'w_out', 'mix_norm_post', 'ffn2_norm_pre', 'ffn2_w_gate', 'ffn2_w_up',
             'ffn2_w_down', 'ffn2_norm_post']
    big = ['ffn1_w_gate', 'ffn1_w_up', 'ffn1_w_down', 'w_in', 'w_attn_out', 'conv_w_out', 'w_out', 'ffn2_w_gate',
           'ffn2_w_up', 'ffn2_w_down']
    small = [n for n in names if n not in big]

    xs, target = x[0], loss_target[0]
    t, d = xs.shape
    cx, cy = lax.axis_index("x"), lax.axis_index("y")
    chip = 2 * cx + cy

    dw_shard = conv_dw_w[0, :, 0, :]
    cshard = dw_shard.shape[1]
    dw_all, _ = _all_devices("gather_dw", _pack([dw_shard]))
    dw_full = jnp.concatenate([dw_all[2 * j, :CONV_WIDTH, :cshard] for j in range(N_CHIPS)], axis=1)
    dw_full = jnp.pad(dw_full, ((0, CONV_HALO - CONV_WIDTH), (0, 0)))
    gathered = _gather_weights([args[n][0].astype(BF16) for n in big])
    wg1, wu1, wd1, win, wao, wco, wout, wg2, wu2, wd2 = gathered

    n1 = _rms_fwd("ffn1_pre", xs, ffn1_norm_pre)
    a1, b1, s1, f1 = _ffn_fwd("ffn1", n1, wg1, wu1, wd1)
    h1, u = _post_res_pre("ffn1_post", f1, xs, ffn1_norm_post, mix_norm_pre, 0.5)
    proj = _mm_nblk("mix_in", u, win, trans_w=False, out_blocked=False, out_dtype=BF16)
    table_pad = jnp.pad(rel_table[0], ((0, 0), (0, REL_PAD - rel_table.shape[2])))
    bias = _bias_expand(table_pad)
    kp = jnp.pad(proj[:, D_ATTN:2 * D_ATTN], ((K_PAD, 0), (0, 0)))
    vp = jnp.pad(proj[:, 2 * D_ATTN:3 * D_ATTN], ((K_PAD, 0), (0, 0)))
    att, lse = _attn_fwd(proj, kp, vp, bias)
    cs, c_glu, z_conv = _conv_fwd(proj, conv_glu_bias, dw_full, conv_dw_b, conv_ln_g, conv_ln_b)
    y_a = _mm_nblk("attn_out", att, wao, trans_w=False, out_blocked=False, out_dtype=F32)
    y_b = _mm_nblk("conv_out", cs, wco, trans_w=False, out_blocked=False, out_dtype=F32)
    merged = _merge_fwd("mix_merge", y_a, y_b, proj, gate_bias)
    mo = _mm_kblk("mix_out", [(merged, wout)], trans_w=False, out_dtype=F32)
    h2, n2 = _post_res_pre("mix_post", mo, h1, mix_norm_post, ffn2_norm_pre, 1.0)
    a2, b2, s2, f2 = _ffn_fwd("ffn2", n2, wg2, wu2, wd2)
    dy, err2 = _final_loss("loss", f2, h2, ffn2_norm_post, target, 0.5)
    loss = lax.psum(0.5 * jnp.sum(err2) / d, ("x", "y", "c"))

    g = {}
    df2, g["ffn2_norm_post"] = _post_bwd_call("ffn2_d_post", dy, f2, ffn2_norm_post, 0.5)
    dn2, g["ffn2_w_gate"], g["ffn2_w_up"], g["ffn2_w_down"] = _ffn_bwd("ffn2", df2, n2, a2, b2, s2, wg2, wu2, wd2)
    dh2, dmo, g["ffn2_norm_pre"], g["mix_norm_post"] = _pre_bwd_post(
        "ffn2_d_pre", dn2, h2, ffn2_norm_pre, dy, mo, mix_norm_post, 1.0)
    dmerged = _mm_nblk("mix_d_merged", dmo, wout, trans_w=True, out_blocked=False, out_dtype=F32)
    g["w_out"] = _mm_tn("mix_g_out", merged, "col", dmo, "full")
    dy_a, dy_b, dgates, g["gate_bias"] = _merge_bwd("mix_d_merge", dmerged, y_a, y_b, proj, gate_bias)
    datt = _mm_kblk("attn_d_out", [(dy_a, wao)], trans_w=True, out_dtype=BF16)
    dcs = _mm_kblk("conv_d_out", [(dy_b, wco)], trans_w=True, out_dtype=F32)
    g["w_attn_out"] = _mm_tn("attn_g_out", att, "full", dy_a, "col")
    g["conv_w_out"] = _mm_tn("conv_g_out", cs, "full", dy_b, "col")
    dq, dkp, dvp, dbias = _attn_bwd(proj, kp, vp, bias, att, lse, datt)
    g["rel_table"] = _bias_fold(dbias)[:, :rel_table.shape[2]]
    dcin, g_dw, g["conv_dw_b"], g["conv_ln_g"], g["conv_ln_b"], g["conv_glu_bias"] = _conv_bwd(
        proj, c_glu, z_conv, dcs, conv_glu_bias, dw_full, conv_ln_g, conv_ln_b)
    dproj = jnp.concatenate([dq, dkp[K_PAD:].astype(BF16), dvp[K_PAD:].astype(BF16), dcin, dgates], axis=1)
    g["w_in"] = _mm_tn("mix_g_in", u, "full", dproj, "col")
    du = _mm_kblk("mix_d_in", [(dproj, win)], trans_w=True, out_dtype=F32)
    dh1, df1, g["mix_norm_pre"], g["ffn1_norm_post"] = _pre_bwd_post(
        "mix_d_pre", du, h1, mix_norm_pre, dh2, f1, ffn1_norm_post, 0.5)
    dn1, g["ffn1_w_gate"], g["ffn1_w_up"], g["ffn1_w_down"] = _ffn_bwd("ffn1", df1, n1, a1, b1, s1, wg1, wu1, wd1)
    grad_x, g["ffn1_norm_pre"] = _pre_bwd_first("ffn1_d_pre", dn1, xs, ffn1_norm_pre, dh1)

    mine, theirs = _pair_exchange([g[n] for n in big])
    parts = [_add2("pair_sum_" + n, m_, t_) for n, m_, t_ in zip(big, mine, theirs)]
    landed = _to_owner(parts)
    halves = [_add_chips("chip_sum_" + n, l_) for n, l_ in zip(big, landed)]
    for n, full in zip(big, _join_halves(halves)):
        g[n] = full

    g["conv_dw_w"] = g_dw[:CONV_WIDTH]
    _, small_sum = _all_devices("sum_small", _pack([g[n] for n in small]))
    for n, val in zip(small, _unpack(small_sum, [g[n].shape for n in small])):
        g[n] = val
    g["conv_dw_w"] = lax.dynamic_slice_in_dim(g["conv_dw_w"], chip * cshard, cshard, axis=1)

    grads, deltas, new_m, new_v = {}, {}, {}, {}
    for n in big:
        grads[n] = g[n][None]
        dl, m2, v2 = _adamw("adamw_" + n, args[n][0], g[n], args["m_" + n][0], args["v_" + n][0])
        deltas[n], new_m[n], new_v[n] = dl[None], m2[None], v2[None]
    shapes = [g[n].shape for n in small]
    packed = lambda pre: _pack([args[pre + n].reshape(shp) for n, shp in zip(small, shapes)])
    dl, m2, v2 = _adamw("adamw_small", packed(""), _pack([g[n] for n in small]), packed("m_"), packed("v_"))
    for n, a_, b_, c_ in zip(small, _unpack(dl, shapes), _unpack(m2, shapes), _unpack(v2, shapes)):
        shape = args[n].shape
        grads[n], deltas[n], new_m[n], new_v[n] = (g[n].reshape(shape), a_.reshape(shape), b_.reshape(shape),
                                                   c_.reshape(shape))

    return (loss, grad_x[None], *[grads[n] for n in names], *[deltas[n] for n in names],
            *[new_m[n] for n in names], *[new_v[n] for n in names])
```

```python
import functools

import numpy as np
import jax
import jax.numpy as jnp
from jax import lax
from jax.experimental import pallas as pl
from jax.experimental.pallas import tpu as pltpu

F32 = jnp.float32
BF16 = jnp.bfloat16
MESH = pl.DeviceIdType.MESH
ANY = pl.BlockSpec(memory_space=pl.ANY)

EPS = 1e-6
CHUNK = 64
LEFT_CHUNKS = 8
N_HEADS = 8
HEAD_DIM = 64
D_ATTN = N_HEADS * HEAD_DIM
D_CONV = 512
CONV_WIDTH = 31
REL_CLIP = 128
N_CHIPS = 4
N_DEV = 8
Q_BLOCK = 4 * CHUNK
K_PAD = LEFT_CHUNKS * CHUNK
K_WIN = K_PAD + Q_BLOCK
REL_EXT = 1024
REL_PAD = 384
CONV_HALO = 32
CONV_TILE = 256
COL = 512
NEG = -1e30

ADAM_LR = 0.001
ADAM_B1 = 0.9
ADAM_B2 = 0.999
ADAM_EPS = 1e-08
ADAM_WD = 0.01
ADAM_STEP = 10

VMEM_LIMIT_BYTES = 48 * 1024 * 1024


def _cparams(n_grid):
    return pltpu.CompilerParams(dimension_semantics=("arbitrary",) * n_grid, vmem_limit_bytes=VMEM_LIMIT_BYTES)


def _row_tile(rows, want):
    if rows <= want:
        return rows
    for t in range(want - want % 16, 0, -16):
        if rows % t == 0:
            return t
    raise ValueError((rows, want))


def _dot(a, w, trans_w):
    dims = (((1,), (1,)), ((), ())) if trans_w else (((1,), (0,)), ((), ()))
    return lax.dot_general(a, w, dims, preferred_element_type=F32)


def _mm_nblk(name, a, w, *, trans_w, out_blocked, out_dtype, tm=1024):
    m, k = a.shape
    nj = w.shape[0]
    nb = w.shape[1] if trans_w else w.shape[2]
    tm = _row_tile(m, tm)

    def body(a_ref, w_ref, o_ref):
        o_ref[...] = _dot(a_ref[...], w_ref[...], trans_w).astype(o_ref.dtype)

    if out_blocked:
        out_shape, out_spec = (nj, m, nb), pl.BlockSpec((None, tm, nb), lambda j, i: (j, i, 0))
    else:
        out_shape, out_spec = (m, nj * nb), pl.BlockSpec((tm, nb), lambda j, i: (i, j))
    return pl.pallas_call(
        body, name=name, grid=(nj, m // tm),
        in_specs=[pl.BlockSpec((tm, k), lambda j, i: (i, 0)),
                  pl.BlockSpec((None,) + w.shape[1:], lambda j, i: (j, 0, 0))],
        out_specs=out_spec, out_shape=jax.ShapeDtypeStruct(out_shape, out_dtype),
        compiler_params=_cparams(2),
    )(a, w)


def _mm_kblk(name, pairs, *, trans_w, out_dtype, tm=1024):
    w0 = pairs[0][1]
    nj = w0.shape[0]
    n = w0.shape[1] if trans_w else w0.shape[2]
    kb = w0.shape[2] if trans_w else w0.shape[1]
    m = pairs[0][0].shape[-2]
    tm = _row_tile(m, tm)
    n_pairs = len(pairs)

    def body(*refs):
        o_ref, acc_ref = refs[2 * n_pairs], refs[2 * n_pairs + 1]
        j = pl.program_id(1)

        @pl.when(j == 0)
        def _():
            acc_ref[...] = jnp.zeros_like(acc_ref)

        part = _dot(refs[0][...], refs[1][...], trans_w)
        for p in range(1, n_pairs):
            part = part + _dot(refs[2 * p][...], refs[2 * p + 1][...], trans_w)
        acc_ref[...] += part

        @pl.when(j == nj - 1)
        def _():
            o_ref[...] = acc_ref[...].astype(o_ref.dtype)

    in_specs, args = [], []
    for a, w in pairs:
        if a.ndim == 3:
            in_specs.append(pl.BlockSpec((None, tm, kb), lambda i, j: (j, i, 0)))
        else:
            in_specs.append(pl.BlockSpec((tm, kb), lambda i, j: (i, j)))
        in_specs.append(pl.BlockSpec((None,) + w.shape[1:], lambda i, j: (j, 0, 0)))
        args += [a, w]
    return pl.pallas_call(
        body, name=name, grid=(m // tm, nj), in_specs=in_specs,
        out_specs=pl.BlockSpec((tm, n), lambda i, j: (i, 0)),
        out_shape=jax.ShapeDtypeStruct((m, n), out_dtype),
        scratch_shapes=[pltpu.VMEM((tm, n), F32)], compiler_params=_cparams(2),
    )(*args)


def _mm_tn(name, a, a_mode, b, b_mode, *, out_dtype=BF16, tt=1024):
    nj = N_CHIPS
    t = a.shape[-2]
    tt = _row_tile(t, tt)

    def spec(x, mode):
        if mode == "full":
            return x.shape[1], pl.BlockSpec((tt, x.shape[1]), lambda j, s: (s, 0))
        if mode == "col":
            cb = x.shape[1] // nj
            return cb, pl.BlockSpec((tt, cb), lambda j, s: (s, j))
        return x.shape[2], pl.BlockSpec((None, tt, x.shape[2]), lambda j, s: (j, s, 0))

    ca, a_spec = spec(a, a_mode)
    cb, b_spec = spec(b, b_mode)
    n_steps = t // tt

    def body(a_ref, b_ref, o_ref, acc_ref):
        s = pl.program_id(1)

        @pl.when(s == 0)
        def _():
            acc_ref[...] = jnp.zeros_like(acc_ref)

        acc_ref[...] += lax.dot_general(a_ref[...], b_ref[...], (((0,), (0,)), ((), ())),
                                        preferred_element_type=F32)

        @pl.when(s == n_steps - 1)
        def _():
            o_ref[...] = acc_ref[...].astype(o_ref.dtype)

    return pl.pallas_call(
        body, name=name, grid=(nj, n_steps), in_specs=[a_spec, b_spec],
        out_specs=pl.BlockSpec((None, ca, cb), lambda j, s: (j, 0, 0)),
        out_shape=jax.ShapeDtypeStruct((nj, ca, cb), out_dtype),
        scratch_shapes=[pltpu.VMEM((ca, cb), F32)], compiler_params=_cparams(2),
    )(a, b)


def _rowwise(name, fn, rows, vecs, row_outs, vec_outs, *, tm=256):
    nrows = rows[0][0].shape[0]
    tm = _row_tile(nrows, tm)
    n_r, n_v, n_ro, n_vo = len(rows), len(vecs), len(row_outs), len(vec_outs)

    def body(*refs):
        r_vals = [r[...] for r in refs[:n_r]]
        v_vals = [r[...] for r in refs[n_r:n_r + n_v]]
        ro_refs = refs[n_r + n_v:n_r + n_v + n_ro]
        vo_refs = refs[n_r + n_v + n_ro:]
        ro, vo = fn(r_vals, v_vals)
        for ref, val in zip(ro_refs, ro):
            ref[...] = val.astype(ref.dtype)
        if n_vo:
            @pl.when(pl.program_id(0) == 0)
            def _():
                for ref in vo_refs:
                    ref[...] = jnp.zeros_like(ref)

            for ref, val in zip(vo_refs, vo):
                ref[...] += val

    in_specs = [pl.BlockSpec((tm, cols), functools.partial(lambda i, cb: (i, cb), cb=cb)) for _, cols, cb in rows]
    in_specs += [pl.BlockSpec(v.shape, functools.partial(lambda i, nd: (0,) * nd, nd=v.ndim)) for v in vecs]
    out_specs = [pl.BlockSpec((tm, cols), lambda i: (i, 0)) for cols, _ in row_outs]
    out_specs += [pl.BlockSpec((1, cols), lambda i: (0, 0)) for cols in vec_outs]
    out_shape = [jax.ShapeDtypeStruct((nrows, cols), dt) for cols, dt in row_outs]
    out_shape += [jax.ShapeDtypeStruct((1, cols), F32) for cols in vec_outs]
    return pl.pallas_call(
        body, name=name, grid=(nrows // tm,), in_specs=in_specs, out_specs=out_specs, out_shape=out_shape,
        compiler_params=_cparams(1),
    )(*[r[0] for r in rows], *vecs)


def _whole(x):
    return (x, x.shape[1], 0)


def _colsum(x):
    return jnp.sum(x, axis=0, keepdims=True)


def _rstd(x):
    return lax.rsqrt(jnp.mean(x * x, axis=-1, keepdims=True) + EPS)


def _rms_bwd(dn, x, g):
    r = _rstd(x)
    c = dn * g
    dx = r * c - x * (r * r * r) * jnp.mean(c * x, axis=-1, keepdims=True)
    return dx, _colsum(dn * x * r)


def _rms_fwd(name, x, g):
    def fn(r, v):
        (xv,), (gv,) = r, v
        return [xv * _rstd(xv) * gv], []

    return _rowwise(name, fn, [_whole(x)], [g], [(x.shape[1], BF16)], [])[0]


def _post_res_pre(name, f, resid, g_post, g_next, scale):
    def fn(r, v):
        fv, rv = r
        gp, gn = v
        h = rv + scale * (fv * _rstd(fv) * gp)
        return [h, h * _rstd(h) * gn], []

    d = f.shape[1]
    return _rowwise(name, fn, [_whole(f), _whole(resid)], [g_post, g_next], [(d, F32), (d, BF16)], [])


def _final_loss(name, f, resid, g_post, target, scale):
    d = f.shape[1]

    def fn(r, v):
        fv, rv, tv = r
        err = rv + scale * (fv * _rstd(fv) * v[0]) - tv
        return [err * (1.0 / d)], [_colsum(err * err)]

    return _rowwise(name, fn, [_whole(f), _whole(resid), _whole(target)], [g_post], [(d, F32)], [d])


def _post_bwd(dh, f, g_post, scale):
    return _rms_bwd(scale * dh, f, g_post)


def _post_bwd_call(name, dh, f, g_post, scale):
    def fn(r, v):
        df, dg = _post_bwd(r[0], r[1], v[0], scale)
        return [df], [dg]

    d = f.shape[1]
    return _rowwise(name, fn, [_whole(dh), _whole(f)], [g_post], [(d, BF16)], [d])


def _pre_bwd_post(name, dn, h, g_pre, dh_up, f_prev, g_post_prev, scale_prev):
    def fn(r, v):
        dnv, hv, upv, fv = r
        dx, dg_pre = _rms_bwd(dnv, hv, v[0])
        dh = upv + dx
        df, dg_post = _post_bwd(dh, fv, v[1], scale_prev)
        return [dh, df], [dg_pre, dg_post]

    d = h.shape[1]
    return _rowwise(name, fn, [_whole(dn), _whole(h), _whole(dh_up), _whole(f_prev)], [g_pre, g_post_prev],
                    [(d, F32), (d, BF16)], [d, d])


def _pre_bwd_first(name, dn, x, g_pre, dh_up):
    def fn(r, v):
        dx, dg_pre = _rms_bwd(r[0], r[1], v[0])
        return [r[2] + dx], [dg_pre]

    d = x.shape[1]
    return _rowwise(name, fn, [_whole(dn), _whole(x), _whole(dh_up)], [g_pre], [(d, F32)], [d])


def _gate_pieces(proj, d):
    first = (3 * D_ATTN + 2 * D_CONV) // COL
    n = d // COL
    return [(proj, COL, first + p) for p in range(2 * n)], n


def _merge_fwd(name, y_a, y_b, proj, gate_bias):
    d = y_a.shape[1]
    pieces, n = _gate_pieces(proj, d)

    def fn(r, v):
        ya, yb = r[0], r[1]
        g = jnp.concatenate([p.astype(F32) for p in r[2:]], axis=1) + v[0]
        gates = jax.nn.sigmoid(g)
        return [gates[:, :d] * ya + gates[:, d:] * yb], []

    return _rowwise(name, fn, [_whole(y_a), _whole(y_b)] + pieces, [gate_bias], [(d, BF16)], [])[0]


def _merge_bwd(name, dmerged, y_a, y_b, proj, gate_bias):
    d = y_a.shape[1]
    pieces, n = _gate_pieces(proj, d)

    def fn(r, v):
        dm, ya, yb = r[0], r[1], r[2]
        g = jnp.concatenate([p.astype(F32) for p in r[3:]], axis=1) + v[0]
        gates = jax.nn.sigmoid(g)
        ga, gb = gates[:, :d], gates[:, d:]
        dga = dm * ya * ga * (1.0 - ga)
        dgb = dm * yb * gb * (1.0 - gb)
        dgate = jnp.concatenate([dga, dgb], axis=1)
        return [dm * ga, dm * gb, dgate], [_colsum(dgate)]

    return _rowwise(name, fn, [_whole(dmerged), _whole(y_a), _whole(y_b)] + pieces, [gate_bias],
                    [(d, BF16), (d, BF16), (2 * d, BF16)], [2 * d])


def _adamw_math(wv, gv, mv, vv):
    m2 = ADAM_B1 * mv + (1.0 - ADAM_B1) * gv
    v2 = ADAM_B2 * vv + (1.0 - ADAM_B2) * (gv * gv)
    m_hat = m2 / (1.0 - ADAM_B1 ** ADAM_STEP)
    v_hat = v2 / (1.0 - ADAM_B2 ** ADAM_STEP)
    delta = -ADAM_LR * (m_hat / (jnp.sqrt(v_hat) + ADAM_EPS) + ADAM_WD * wv)
    return delta, m2, v2


def _adamw(name, w, g, m, v):
    def fn(r, _):
        return list(_adamw_math(*r)), []

    c = w.shape[1]
    return _rowwise(name, fn, [_whole(w), _whole(g), _whole(m), _whole(v)], [], [(c, F32)] * 3, [], tm=256)


POS_C, POS_CHIP, POS_PEER = 0, 1, 2


def _placed_call(body, name, pos, grid, in_specs, out_specs, out_shape, args):
    return pl.pallas_call(
        body, name=name, out_shape=out_shape, compiler_params=_cparams(len(grid)),
        grid_spec=pltpu.PrefetchScalarGridSpec(num_scalar_prefetch=1, grid=grid, in_specs=in_specs,
                                               out_specs=out_specs),
    )(pos, *args)


def _cast_into(name, pos, w):
    r, cols = w.shape
    tm = _row_tile(r, 256)

    def body(pos_ref, w_ref, o_ref):
        o_ref[...] = w_ref[...].astype(o_ref.dtype)

    return _placed_call(
        body, name, pos, (r // tm,), [pl.BlockSpec((tm, cols), lambda i, pos: (i, 0))],
        pl.BlockSpec((None, tm, cols), lambda i, pos: (pos[POS_CHIP], i, 0)),
        jax.ShapeDtypeStruct((N_CHIPS, r, cols), BF16), [w])


def _add_pair(name, pos, grad, landed):
    nj, half, cols = landed.shape
    tm = _row_tile(half, 256)
    nb = half // tm

    def body(pos_ref, g_ref, l_ref, o_ref):
        o_ref[...] = (g_ref[...].astype(F32) + l_ref[...].astype(F32)).astype(o_ref.dtype)

    spec = pl.BlockSpec((None, tm, cols), lambda j, i, pos: (j, i, 0))
    return _placed_call(
        body, name, pos, (nj, nb),
        [pl.BlockSpec((None, tm, cols), lambda j, i, pos: (j, pos[POS_C] * nb + i, 0)), spec], spec,
        jax.ShapeDtypeStruct(landed.shape, BF16), [grad, landed])


def _add_chips(name, pos, part, landed):
    _, half, cols = landed.shape
    tm = _row_tile(half, 256)

    def body(pos_ref, p_ref, l0_ref, l1_ref, l2_ref, o_ref):
        acc = p_ref[...].astype(F32)
        for ref in (l0_ref, l1_ref, l2_ref):
            acc = acc + ref[...].astype(F32)
        o_ref[...] = acc

    slot = lambda at: pl.BlockSpec((None, tm, cols), functools.partial(lambda i, pos, at: (pos[at], i, 0), at=at))
    return _placed_call(
        body, name, pos, (half // tm,), [slot(POS_CHIP)] + [slot(POS_PEER + k) for k in range(3)],
        pl.BlockSpec((tm, cols), lambda i, pos: (i, 0)), jax.ShapeDtypeStruct((half, cols), F32),
        [part, landed, landed, landed])


def _adamw_halves(name, pos, w, m, v, own, landed):
    r, cols = w.shape
    half = own.shape[0]
    tm = _row_tile(half, 256)
    nb = half // tm

    def body(pos_ref, w_ref, m_ref, v_ref, own_ref, land_ref, g_out, d_out, m_out, v_out):
        mine = pl.program_id(0) == pos_ref[POS_C]
        g = jnp.where(mine, own_ref[...], land_ref[...])
        delta, m2, v2 = _adamw_math(w_ref[...], g, m_ref[...], v_ref[...])
        g_out[...] = g
        d_out[...] = delta
        m_out[...] = m2
        v_out[...] = v2

    full = pl.BlockSpec((tm, cols), lambda h, i, pos: (h * nb + i, 0))
    part = pl.BlockSpec((tm, cols), lambda h, i, pos: (i, 0))
    return _placed_call(
        body, name, pos, (2, nb), [full, full, full, part, part], [full] * 4,
        [jax.ShapeDtypeStruct((r, cols), F32)] * 4, [w, m, v, own, landed])


def _rel_onehot():
    e = np.arange(REL_EXT)
    dist = K_PAD - (e - (Q_BLOCK - 1))
    idx = np.clip(dist, -REL_CLIP, REL_CLIP) + REL_CLIP
    return (np.arange(REL_PAD)[:, None] == idx[None, :]).astype(np.float32)


def _band_valid():
    qc = lax.broadcasted_iota(jnp.int32, (Q_BLOCK, K_WIN), 0) // CHUNK
    kc = lax.broadcasted_iota(jnp.int32, (Q_BLOCK, K_WIN), 1) // CHUNK
    return (kc >= qc) & (kc <= qc + LEFT_CHUNKS)


def _skew(x, left):
    row = lax.broadcasted_iota(jnp.int32, x.shape, 0)
    for bit in range(Q_BLOCK.bit_length() - 1):
        amount = 1 << bit
        rolled = pltpu.roll(x, REL_EXT - amount if left else amount, 1)
        x = jnp.where((row >> bit) & 1 == 1, rolled, x)
    return x


def _bias_expand(table_pad):
    onehot = jnp.asarray(_rel_onehot())

    def body(t_ref, oh_ref, o_ref):
        ext = jnp.dot(t_ref[...], oh_ref[...], precision=lax.Precision.HIGHEST, preferred_element_type=F32)
        valid = _band_valid()
        for h in range(N_HEADS):
            rows = jnp.broadcast_to(ext[h:h + 1, :], (Q_BLOCK, REL_EXT))
            rolled = _skew(pltpu.roll(rows, REL_EXT - (Q_BLOCK - 1), 1), left=False)
            o_ref[h] = jnp.where(valid, rolled[:, :K_WIN], NEG)

    return pl.pallas_call(
        body, name="bias_expand", out_shape=jax.ShapeDtypeStruct((N_HEADS, Q_BLOCK, K_WIN), F32),
        compiler_params=pltpu.CompilerParams(vmem_limit_bytes=VMEM_LIMIT_BYTES),
    )(table_pad, onehot)


def _bias_fold(dbias):
    onehot_t = jnp.asarray(_rel_onehot().T)

    def body(d_ref, oh_ref, o_ref, ext_ref):
        for h in range(N_HEADS):
            x = jnp.concatenate([d_ref[h], jnp.zeros((Q_BLOCK, REL_EXT - K_WIN), F32)], axis=1)
            rolled = _skew(pltpu.roll(x, Q_BLOCK - 1, 1), left=True)
            ext_ref[h:h + 1, :] = jnp.sum(rolled, axis=0, keepdims=True)
        o_ref[...] = jnp.dot(ext_ref[...], oh_ref[...], precision=lax.Precision.HIGHEST,
                             preferred_element_type=F32)

    return pl.pallas_call(
        body, name="bias_fold", out_shape=jax.ShapeDtypeStruct((N_HEADS, REL_PAD), F32),
        scratch_shapes=[pltpu.VMEM((N_HEADS, REL_EXT), F32)],
        compiler_params=pltpu.CompilerParams(vmem_limit_bytes=VMEM_LIMIT_BYTES),
    )(dbias, onehot_t)


def _scores(q, k, bias, i):
    s = lax.dot_general(q, k, (((1,), (1,)), ((), ())), preferred_element_type=F32) * (HEAD_DIM ** -0.5) + bias
    kpos = lax.broadcasted_iota(jnp.int32, (Q_BLOCK, K_WIN), 1) + i * Q_BLOCK
    return jnp.where(kpos >= K_PAD, s, NEG)


def _attn_specs(n_kv):
    q_spec = pl.BlockSpec((Q_BLOCK, 2 * HEAD_DIM), lambda p, i: (i, p))
    kv_specs = [pl.BlockSpec((Q_BLOCK, 2 * HEAD_DIM), functools.partial(lambda p, i, kk: (i + kk, p), kk=kk))
                for _ in range(n_kv) for kk in range(K_WIN // Q_BLOCK)]
    bias_spec = pl.BlockSpec((2, Q_BLOCK, K_WIN), lambda p, i: (p, 0, 0))
    return q_spec, kv_specs, bias_spec


def _attn_fwd(proj, kp, vp, bias):
    t = proj.shape[0]
    n_win = K_WIN // Q_BLOCK

    def body(q_ref, *refs):
        k_refs, v_refs = refs[:n_win], refs[n_win:2 * n_win]
        b_ref, o_ref, lse_ref = refs[2 * n_win:]
        i = pl.program_id(1)
        k = jnp.concatenate([r[...] for r in k_refs], axis=0)
        v = jnp.concatenate([r[...] for r in v_refs], axis=0)
        q = q_ref[...]
        for hh in range(2):
            lanes = slice(hh * HEAD_DIM, (hh + 1) * HEAD_DIM)
            s = _scores(q[:, lanes], k[:, lanes], b_ref[hh], i)
            m = jnp.max(s, axis=1, keepdims=True)
            p = jnp.exp(s - m)
            l = jnp.sum(p, axis=1, keepdims=True)
            o = jnp.dot(p.astype(BF16), v[:, lanes], preferred_element_type=F32) / l
            o_ref[:, lanes] = o.astype(o_ref.dtype)
            lse_ref[:, lanes] = jnp.broadcast_to(m + jnp.log(l), (Q_BLOCK, HEAD_DIM))

    q_spec, kv_specs, bias_spec = _attn_specs(2)
    out_spec = pl.BlockSpec((Q_BLOCK, 2 * HEAD_DIM), lambda p, i: (i, p))
    return pl.pallas_call(
        body, name="attn_fwd", grid=(N_HEADS // 2, t // Q_BLOCK),
        in_specs=[q_spec] + kv_specs + [bias_spec], out_specs=[out_spec, out_spec],
        out_shape=[jax.ShapeDtypeStruct((t, D_ATTN), BF16), jax.ShapeDtypeStruct((t, D_ATTN), F32)],
        compiler_params=_cparams(2),
    )(proj, *([kp] * n_win), *([vp] * n_win), bias)


def _attn_bwd(proj, kp, vp, bias, att, lse, datt):
    t = proj.shape[0]
    n_win = K_WIN // Q_BLOCK

    def body(q_ref, *refs):
        k_refs, v_refs = refs[:n_win], refs[n_win:2 * n_win]
        b_ref, o_ref, lse_ref, do_ref, dq_ref, dk_ref, dv_ref, db_ref = refs[2 * n_win:]
        i = pl.program_id(1)

        @pl.when(i == 0)
        def _():
            dk_ref[...] = jnp.zeros_like(dk_ref)
            dv_ref[...] = jnp.zeros_like(dv_ref)
            db_ref[...] = jnp.zeros_like(db_ref)

        k = jnp.concatenate([r[...] for r in k_refs], axis=0)
        v = jnp.concatenate([r[...] for r in v_refs], axis=0)
        q = q_ref[...]
        do = do_ref[...]
        rows = pl.ds(pl.multiple_of(i * Q_BLOCK, Q_BLOCK), K_WIN)
        scale = HEAD_DIM ** -0.5
        for hh in range(2):
            lanes = slice(hh * HEAD_DIM, (hh + 1) * HEAD_DIM)
            qh, kh, vh, doh = q[:, lanes], k[:, lanes], v[:, lanes], do[:, lanes]
            s = _scores(qh, kh, b_ref[hh], i)
            p = jnp.exp(s - lse_ref[:, lanes][:, :1])
            dp = lax.dot_general(doh, vh, (((1,), (1,)), ((), ())), preferred_element_type=F32)
            delta = jnp.sum(doh.astype(F32) * o_ref[:, lanes].astype(F32), axis=1, keepdims=True)
            ds = p * (dp - delta)
            db_ref[hh] += ds
            dsb = ds.astype(BF16)
            dq_ref[:, lanes] = (jnp.dot(dsb, kh, preferred_element_type=F32) * scale).astype(dq_ref.dtype)
            dk_ref[rows, lanes] += lax.dot_general(dsb, qh, (((0,), (0,)), ((), ())),
                                                   preferred_element_type=F32) * scale
            dv_ref[rows, lanes] += lax.dot_general(p.astype(BF16), doh, (((0,), (0,)), ((), ())),
                                                   preferred_element_type=F32)

    q_spec, kv_specs, bias_spec = _attn_specs(2)
    row_spec = pl.BlockSpec((Q_BLOCK, 2 * HEAD_DIM), lambda p, i: (i, p))
    full_spec = pl.BlockSpec((t + K_PAD, 2 * HEAD_DIM), lambda p, i: (0, p))
    return pl.pallas_call(
        body, name="attn_bwd", grid=(N_HEADS // 2, t // Q_BLOCK),
        in_specs=[q_spec] + kv_specs + [bias_spec, row_spec, row_spec, row_spec],
        out_specs=[row_spec, full_spec, full_spec, bias_spec],
        out_shape=[jax.ShapeDtypeStruct((t, D_ATTN), BF16), jax.ShapeDtypeStruct((t + K_PAD, D_ATTN), F32),
                   jax.ShapeDtypeStruct((t + K_PAD, D_ATTN), F32),
                   jax.ShapeDtypeStruct((N_HEADS, Q_BLOCK, K_WIN), F32)],
        compiler_params=_cparams(2),
    )(proj, *([kp] * n_win), *([vp] * n_win), bias, att, lse, datt)


CONV_LEAD = CONV_HALO - (CONV_WIDTH - 1)
CONV_LANES = 128
CONV_ROWS = 64


def _conv_specs(t):
    tt = _row_tile(t, CONV_TILE)
    per = tt // CONV_HALO
    n_halo = t // CONV_HALO
    tile = lambda cb: pl.BlockSpec((tt, COL), functools.partial(lambda i, cb: (i, cb), cb=cb))
    prev = lambda cb: pl.BlockSpec((CONV_HALO, COL),
                                   functools.partial(lambda i, cb: (jnp.maximum(i * per - 1, 0), cb), cb=cb))
    nxt = lambda cb: pl.BlockSpec((CONV_HALO, COL),
                                  functools.partial(lambda i, cb: (jnp.minimum((i + 1) * per, n_halo - 1), cb), cb=cb))
    vec = lambda shape: pl.BlockSpec(shape, lambda i: (0, 0))
    return tt, tile, prev, nxt, vec


def _glu(ca, cg, bias):
    return (ca.astype(F32) + bias[:, :D_CONV]) * jax.nn.sigmoid(cg.astype(F32) + bias[:, D_CONV:])


def _taps(ext_ref, tt, first_row, weight_of, out_ref):
    for r0 in range(0, tt, CONV_ROWS):
        for l0 in range(0, D_CONV, CONV_LANES):
            lanes = slice(l0, l0 + CONV_LANES)
            acc = jnp.zeros((CONV_ROWS, CONV_LANES), F32)
            for w in range(CONV_WIDTH):
                acc = acc + ext_ref[first_row(w) + r0:first_row(w) + r0 + CONV_ROWS, lanes] * weight_of(w)[:, lanes]
            out_ref[r0:r0 + CONV_ROWS, lanes] = acc


def _conv_fwd(proj, glu_bias, dw, dw_b, ln_g, ln_b):
    t = proj.shape[0]
    tt, tile, prev, nxt, vec = _conv_specs(t)
    ca_blk, cg_blk = 3 * D_ATTN // COL, 3 * D_ATTN // COL + 1

    def body(ca_ref, cg_ref, pa_ref, pg_ref, gb_ref, dw_ref, dwb_ref, g_ref, b_ref, cs_ref, c_ref, z_ref, ext_ref):
        i = pl.program_id(0)
        bias = gb_ref[...]
        c = _glu(ca_ref[...], cg_ref[...], bias)
        halo = _glu(pa_ref[...], pg_ref[...], bias)
        ext_ref[0:CONV_HALO, :] = jnp.where(i == 0, 0.0, halo)
        ext_ref[CONV_HALO:, :] = c
        c_ref[...] = c
        _taps(ext_ref, tt, lambda w: CONV_LEAD + w, lambda w: dw_ref[w:w + 1, :], z_ref)
        z = z_ref[...] + dwb_ref[...]
        z_ref[...] = z
        mu = jnp.mean(z, axis=-1, keepdims=True)
        zc = z - mu
        y = zc * lax.rsqrt(jnp.mean(zc * zc, axis=-1, keepdims=True) + EPS) * g_ref[...] + b_ref[...]
        cs_ref[...] = (y * jax.nn.sigmoid(y)).astype(cs_ref.dtype)

    out_spec = pl.BlockSpec((tt, D_CONV), lambda i: (i, 0))
    return pl.pallas_call(
        body, name="conv_fwd", grid=(t // tt,),
        in_specs=[tile(ca_blk), tile(cg_blk), prev(ca_blk), prev(cg_blk), vec(glu_bias.shape), vec(dw.shape),
                  vec(dw_b.shape), vec(ln_g.shape), vec(ln_b.shape)],
        out_specs=[out_spec] * 3,
        out_shape=[jax.ShapeDtypeStruct((t, D_CONV), BF16), jax.ShapeDtypeStruct((t, D_CONV), F32),
                   jax.ShapeDtypeStruct((t, D_CONV), F32)],
        scratch_shapes=[pltpu.VMEM((tt + CONV_HALO, D_CONV), F32)], compiler_params=_cparams(1),
    )(proj, proj, proj, proj, glu_bias, dw, dw_b, ln_g, ln_b)


def _conv_bwd(proj, c, z, dcs, glu_bias, dw, ln_g, ln_b):
    t = proj.shape[0]
    tt, tile, prev, nxt, vec = _conv_specs(t)
    n_tiles = t // tt
    ca_blk, cg_blk = 3 * D_ATTN // COL, 3 * D_ATTN // COL + 1

    def ln_bwd(zv, dcsv, g, b):
        mu = jnp.mean(zv, axis=-1, keepdims=True)
        zc = zv - mu
        rstd = lax.rsqrt(jnp.mean(zc * zc, axis=-1, keepdims=True) + EPS)
        zhat = zc * rstd
        y = zhat * g + b
        sig = jax.nn.sigmoid(y)
        dy = dcsv * sig * (1.0 + y * (1.0 - sig))
        dzh = dy * g
        dz = rstd * (dzh - jnp.mean(dzh, axis=-1, keepdims=True) - zhat * jnp.mean(dzh * zhat, axis=-1, keepdims=True))
        return dz, dy, zhat

    def body(ca_ref, cg_ref, c_ref, cprev_ref, z_ref, znext_ref, dcs_ref, dcsnext_ref, gb_ref, dw_ref, g_ref, b_ref,
             dcin_ref, ddw_ref, ddwb_ref, dg_ref, db_ref, dgb_ref, cext_ref, dzext_ref, dc_ref):
        i = pl.program_id(0)

        @pl.when(i == 0)
        def _():
            for ref in (ddw_ref, ddwb_ref, dg_ref, db_ref, dgb_ref):
                ref[...] = jnp.zeros_like(ref)

        g, b = g_ref[...], b_ref[...]
        dz, dy, zhat = ln_bwd(z_ref[...], dcs_ref[...], g, b)
        dz_next, _, _ = ln_bwd(znext_ref[...], dcsnext_ref[...], g, b)
        dg_ref[...] += _colsum(dy * zhat)
        db_ref[...] += _colsum(dy)
        ddwb_ref[...] += _colsum(dz)
        dzext_ref[0:tt, :] = dz
        dzext_ref[tt:, :] = jnp.where(i == n_tiles - 1, 0.0, dz_next)
        cext_ref[0:CONV_HALO, :] = jnp.where(i == 0, 0.0, cprev_ref[...])
        cext_ref[CONV_HALO:, :] = c_ref[...]
        _taps(dzext_ref, tt, lambda w: CONV_WIDTH - 1 - w, lambda w: dw_ref[w:w + 1, :], dc_ref)
        for w in range(CONV_WIDTH):
            ddw_ref[w:w + 1, :] += _colsum(cext_ref[CONV_LEAD + w:CONV_LEAD + w + tt, :] * dz)
        bias = gb_ref[...]
        a_in = ca_ref[...].astype(F32) + bias[:, :D_CONV]
        sg = jax.nn.sigmoid(cg_ref[...].astype(F32) + bias[:, D_CONV:])
        dc = dc_ref[...]
        dcin = jnp.concatenate([dc * sg, dc * a_in * sg * (1.0 - sg)], axis=1)
        dcin_ref[...] = dcin.astype(dcin_ref.dtype)
        dgb_ref[...] += _colsum(dcin)

    row = lambda: pl.BlockSpec((tt, D_CONV), lambda i: (i, 0))
    per = tt // CONV_HALO
    n_halo = t // CONV_HALO
    prev_row = pl.BlockSpec((CONV_HALO, D_CONV), lambda i: (jnp.maximum(i * per - 1, 0), 0))
    next_row = lambda: pl.BlockSpec((CONV_HALO, D_CONV), lambda i: (jnp.minimum((i + 1) * per, n_halo - 1), 0))
    acc = lambda shape: pl.BlockSpec(shape, lambda i: (0, 0))
    return pl.pallas_call(
        body, name="conv_bwd", grid=(n_tiles,),
        in_specs=[tile(ca_blk), tile(cg_blk), row(), prev_row, row(), next_row(), row(), next_row(),
                  vec(glu_bias.shape), vec(dw.shape), vec(ln_g.shape), vec(ln_b.shape)],
        out_specs=[pl.BlockSpec((tt, 2 * D_CONV), lambda i: (i, 0)), acc(dw.shape), acc((1, D_CONV)),
                   acc((1, D_CONV)), acc((1, D_CONV)), acc((1, 2 * D_CONV))],
        out_shape=[jax.ShapeDtypeStruct((t, 2 * D_CONV), BF16), jax.ShapeDtypeStruct(dw.shape, F32),
                   jax.ShapeDtypeStruct((1, D_CONV), F32), jax.ShapeDtypeStruct((1, D_CONV), F32),
                   jax.ShapeDtypeStruct((1, D_CONV), F32), jax.ShapeDtypeStruct((1, 2 * D_CONV), F32)],
        scratch_shapes=[pltpu.VMEM((tt + CONV_HALO, D_CONV), F32), pltpu.VMEM((tt + CONV_HALO, D_CONV), F32),
                        pltpu.VMEM((tt, D_CONV), F32)],
        compiler_params=_cparams(1),
    )(proj, proj, c, c, z, z, dcs, dcs, glu_bias, dw, ln_g, ln_b)


def _place():
    x, y, c = lax.axis_index("x"), lax.axis_index("y"), lax.axis_index("c")
    chips = [(1 - x, y), (x, 1 - y), (1 - x, 1 - y)]
    return x, y, c, chips


def _chip_index(chip):
    return 2 * chip[0] + chip[1]


def _half_rows(c, half):
    return pl.ds(pl.multiple_of(c * half, 16), half)


def _gather_weights(blocked):
    n = len(blocked)

    def body(*refs):
        o_refs = refs[n:2 * n]
        send_sems, recv_sems = refs[2 * n:]
        x, y, c, chips = _place()
        me = _chip_index((x, y))
        sibling = (x, y, 1 - c)
        first, passed = [], []
        for a in range(n):
            half = o_refs[a].shape[1] // 2
            mine = o_refs[a].at[me, _half_rows(c, half), :]
            for k, chip in enumerate(chips):
                cp = pltpu.make_async_remote_copy(
                    src_ref=mine, dst_ref=mine, send_sem=send_sems.at[6 * a + k], recv_sem=recv_sems.at[6 * a + k],
                    device_id=(chip[0], chip[1], c), device_id_type=MESH)
                cp.start()
                first.append(cp)
        for a in range(n):
            half = o_refs[a].shape[1] // 2
            for k, chip in enumerate(chips):
                landed = o_refs[a].at[_chip_index(chip), _half_rows(c, half), :]
                cp = pltpu.make_async_remote_copy(
                    src_ref=landed, dst_ref=landed, send_sem=send_sems.at[6 * a + 3 + k],
                    recv_sem=recv_sems.at[6 * a + 3 + k], device_id=sibling, device_id_type=MESH)
                first[3 * a + k].wait_recv()
                cp.start()
                passed.append(cp)
        for a in range(n):
            half = o_refs[a].shape[1] // 2
            for k, chip in enumerate(chips):
                landed = o_refs[a].at[_chip_index(chip), _half_rows(1 - c, half), :]
                pltpu.make_async_remote_copy(
                    src_ref=landed, dst_ref=landed, send_sem=send_sems.at[6 * a + 3 + k],
                    recv_sem=recv_sems.at[6 * a + 3 + k], device_id=sibling, device_id_type=MESH).wait_recv()
        for cp in first + passed:
            cp.wait_send()

    return pl.pallas_call(
        body, name="gather_weights", in_specs=[ANY] * n, out_specs=[ANY] * n,
        out_shape=[jax.ShapeDtypeStruct(w.shape, w.dtype) for w in blocked],
        input_output_aliases={a: a for a in range(n)},
        scratch_shapes=[pltpu.SemaphoreType.DMA((6 * n,)), pltpu.SemaphoreType.DMA((6 * n,))],
    )(*blocked)


def _pair_exchange(grads):
    n = len(grads)

    def body(*refs):
        g_refs, land_refs = refs[:n], refs[n:2 * n]
        send_sems, recv_sems = refs[2 * n:]
        x, y, c, _ = _place()
        copies = []
        for a in range(n):
            half = g_refs[a].shape[1] // 2
            cp = pltpu.make_async_remote_copy(
                src_ref=g_refs[a].at[:, _half_rows(1 - c, half), :], dst_ref=land_refs[a],
                send_sem=send_sems.at[a], recv_sem=recv_sems.at[a], device_id=(x, y, 1 - c), device_id_type=MESH)
            cp.start()
            copies.append(cp)
        for cp in copies:
            cp.wait()

    return pl.pallas_call(
        body, name="grad_pair_exchange", in_specs=[ANY] * n, out_specs=[ANY] * n,
        out_shape=[jax.ShapeDtypeStruct((g.shape[0], g.shape[1] // 2, g.shape[2]), g.dtype) for g in grads],
        scratch_shapes=[pltpu.SemaphoreType.DMA((n,)), pltpu.SemaphoreType.DMA((n,))],
    )(*grads)


def _to_owner(parts):
    n = len(parts)

    def body(*refs):
        p_refs, l_refs = refs[:n], refs[n:2 * n]
        send_sems, recv_sems = refs[2 * n:]
        x, y, c, chips = _place()
        me = _chip_index((x, y))
        copies = []
        for a in range(n):
            for k, chip in enumerate(chips):
                cp = pltpu.make_async_remote_copy(
                    src_ref=p_refs[a].at[_chip_index(chip)], dst_ref=l_refs[a].at[me],
                    send_sem=send_sems.at[3 * a + k], recv_sem=recv_sems.at[3 * a + k],
                    device_id=(chip[0], chip[1], c), device_id_type=MESH)
                cp.start()
                copies.append(cp)
        for a in range(n):
            for k, chip in enumerate(chips):
                slot = l_refs[a].at[_chip_index(chip)]
                pltpu.make_async_remote_copy(
                    src_ref=slot, dst_ref=slot, send_sem=send_sems.at[3 * a + k], recv_sem=recv_sems.at[3 * a + k],
                    device_id=(chip[0], chip[1], c), device_id_type=MESH).wait_recv()
        for cp in copies:
            cp.wait_send()

    return pl.pallas_call(
        body, name="grad_to_owner", in_specs=[ANY] * n, out_specs=[ANY] * n,
        out_shape=[jax.ShapeDtypeStruct(p.shape, p.dtype) for p in parts],
        scratch_shapes=[pltpu.SemaphoreType.DMA((3 * n,)), pltpu.SemaphoreType.DMA((3 * n,))],
    )(*parts)


def _swap_halves(halves):
    n = len(halves)

    def body(*refs):
        h_refs, o_refs = refs[:n], refs[n:2 * n]
        send_sems, recv_sems = refs[2 * n:]
        x, y, c, _ = _place()
        copies = []
        for a in range(n):
            cp = pltpu.make_async_remote_copy(
                src_ref=h_refs[a], dst_ref=o_refs[a], send_sem=send_sems.at[a], recv_sem=recv_sems.at[a],
                device_id=(x, y, 1 - c), device_id_type=MESH)
            cp.start()
            copies.append(cp)
        for cp in copies:
            cp.wait()

    return pl.pallas_call(
        body, name="grad_swap_halves", in_specs=[ANY] * n, out_specs=[ANY] * n,
        out_shape=[jax.ShapeDtypeStruct(h.shape, h.dtype) for h in halves],
        scratch_shapes=[pltpu.SemaphoreType.DMA((n,)), pltpu.SemaphoreType.DMA((n,))],
    )(*halves)


def _all_devices(name, block):
    r, cols = block.shape

    def body(b_ref, all_ref, sum_ref, send_sems, recv_sems):
        x, y, c, _ = _place()
        me = 4 * x + 2 * y + c
        all_ref[me] = b_ref[...]
        flips = [(fx, fy, fc) for fx in (0, 1) for fy in (0, 1) for fc in (0, 1)][1:]
        copies = []
        for k, (fx, fy, fc) in enumerate(flips):
            cp = pltpu.make_async_remote_copy(
                src_ref=b_ref, dst_ref=all_ref.at[me], send_sem=send_sems.at[k], recv_sem=recv_sems.at[k],
                device_id=(x ^ fx, y ^ fy, c ^ fc), device_id_type=MESH)
            cp.start()
            copies.append(cp)
        for k, (fx, fy, fc) in enumerate(flips):
            slot = all_ref.at[4 * (x ^ fx) + 2 * (y ^ fy) + (c ^ fc)]
            pltpu.make_async_remote_copy(
                src_ref=slot, dst_ref=slot, send_sem=send_sems.at[k], recv_sem=recv_sems.at[k],
                device_id=(x ^ fx, y ^ fy, c ^ fc), device_id_type=MESH).wait_recv()
        for cp in copies:
            cp.wait_send()
        acc = all_ref[0]
        for d in range(1, N_DEV):
            acc = acc + all_ref[d]
        sum_ref[...] = acc

    vmem = pl.BlockSpec(memory_space=pltpu.VMEM)
    return pl.pallas_call(
        body, name=name, in_specs=[vmem], out_specs=[vmem, vmem],
        out_shape=[jax.ShapeDtypeStruct((N_DEV, r, cols), F32), jax.ShapeDtypeStruct((r, cols), F32)],
        scratch_shapes=[pltpu.SemaphoreType.DMA((N_DEV - 1,)), pltpu.SemaphoreType.DMA((N_DEV - 1,))],
    )(block)


PACK = 1024


def _packed_rows(shape):
    size, last = int(np.prod(shape)), shape[-1]
    cols = last if last <= PACK else PACK
    assert size % cols == 0
    return size // cols, cols


def _pack(vals):
    rows = []
    for v in vals:
        n_rows, cols = _packed_rows(v.shape)
        v = v.reshape(n_rows, cols).astype(F32)
        rows.append(jnp.pad(v, ((0, (-n_rows) % 8), (0, PACK - cols))))
    return jnp.concatenate(rows, axis=0)


def _unpack(buf, shapes):
    out, r = [], 0
    for shape in shapes:
        n_rows, cols = _packed_rows(shape)
        out.append(buf[r:r + n_rows, :cols].reshape(shape))
        r += n_rows + (-n_rows) % 8
    return out


def _ffn_hidden(name, n, wg, wu, tm=1024):
    m, k = n.shape
    nj, _, fb = wg.shape
    tm = _row_tile(m, tm)

    def body(n_ref, wg_ref, wu_ref, a_ref, b_ref, s_ref):
        nv = n_ref[...]
        a = _dot(nv, wg_ref[...], False)
        b = _dot(nv, wu_ref[...], False)
        a_ref[...] = a.astype(a_ref.dtype)
        b_ref[...] = b.astype(b_ref.dtype)
        s_ref[...] = (a * jax.nn.sigmoid(a) * b).astype(s_ref.dtype)

    w_spec = pl.BlockSpec((None, k, fb), lambda j, i: (j, 0, 0))
    out_spec = pl.BlockSpec((None, tm, fb), lambda j, i: (j, i, 0))
    return pl.pallas_call(
        body, name=name, grid=(nj, m // tm),
        in_specs=[pl.BlockSpec((tm, k), lambda j, i: (i, 0)), w_spec, w_spec], out_specs=[out_spec] * 3,
        out_shape=[jax.ShapeDtypeStruct((nj, m, fb), BF16)] * 3, compiler_params=_cparams(2),
    )(n, wg, wu)


def _ffn_d_hidden(name, df, wd, a, b, tm=1024):
    m, k = df.shape
    nj, fb, _ = wd.shape
    tm = _row_tile(m, tm)

    def body(df_ref, wd_ref, a_ref, b_ref, da_ref, db_ref):
        ds = _dot(df_ref[...], wd_ref[...], True)
        av, bv = a_ref[...].astype(F32), b_ref[...].astype(F32)
        sig = jax.nn.sigmoid(av)
        da_ref[...] = (ds * bv * sig * (1.0 + av * (1.0 - sig))).astype(da_ref.dtype)
        db_ref[...] = (ds * av * sig).astype(db_ref.dtype)

    blk = pl.BlockSpec((None, tm, fb), lambda j, i: (j, i, 0))
    return pl.pallas_call(
        body, name=name, grid=(nj, m // tm),
        in_specs=[pl.BlockSpec((tm, k), lambda j, i: (i, 0)), pl.BlockSpec((None, fb, k), lambda j, i: (j, 0, 0)),
                  blk, blk],
        out_specs=[blk, blk], out_shape=[jax.ShapeDtypeStruct((nj, m, fb), BF16)] * 2,
        compiler_params=_cparams(2),
    )(df, wd, a, b)


def _ffn_fwd(tag, n, wg, wu, wd):
    a, b, s = _ffn_hidden(tag + "_hidden", n, wg, wu)
    f = _mm_kblk(tag + "_down", [(s, wd)], trans_w=False, out_dtype=F32)
    return a, b, s, f


def _ffn_bwd(tag, df, n, a, b, s, wg, wu, wd):
    da, db = _ffn_d_hidden(tag + "_d_hidden", df, wd, a, b)
    g_wd = _mm_tn(tag + "_g_down", s, "blk", df, "full")
    g_wg = _mm_tn(tag + "_g_gate", n, "full", da, "blk")
    g_wu = _mm_tn(tag + "_g_up", n, "full", db, "blk")
    dn = _mm_kblk(tag + "_d_n", [(da, wg), (db, wu)], trans_w=True, out_dtype=F32)
    return dn, g_wg, g_wu, g_wd


def kernel(x, ffn1_norm_pre, ffn1_w_gate, ffn1_w_up, ffn1_w_down, ffn1_norm_post, mix_norm_pre, w_in, gate_bias, rel_table, w_attn_out, conv_glu_bias, conv_dw_w, conv_dw_b, conv_ln_g, conv_ln_b, conv_w_out, w_out, mix_norm_post, ffn2_norm_pre, ffn2_w_gate, ffn2_w_up, ffn2_w_down, ffn2_norm_post, loss_target, m_ffn1_norm_pre, m_ffn1_w_gate, m_ffn1_w_up, m_ffn1_w_down, m_ffn1_norm_post, m_mix_norm_pre, m_w_in, m_gate_bias, m_rel_table, m_w_attn_out, m_conv_glu_bias, m_conv_dw_w, m_conv_dw_b, m_conv_ln_g, m_conv_ln_b, m_conv_w_out, m_w_out, m_mix_norm_post, m_ffn2_norm_pre, m_ffn2_w_gate, m_ffn2_w_up, m_ffn2_w_down, m_ffn2_norm_post, v_ffn1_norm_pre, v_ffn1_w_gate, v_ffn1_w_up, v_ffn1_w_down, v_ffn1_norm_post, v_mix_norm_pre, v_w_in, v_gate_bias, v_rel_table, v_w_attn_out, v_conv_glu_bias, v_conv_dw_w, v_conv_dw_b, v_conv_ln_g, v_conv_ln_b, v_conv_w_out, v_w_out, v_mix_norm_post, v_ffn2_norm_pre, v_ffn2_w_gate, v_ffn2_w_up, v_ffn2_w_down, v_ffn2_norm_post):
    args = dict(locals())
    names = ['ffn1_norm_pre', 'ffn1_w_gate', 'ffn1_w_up', 'ffn1_w_down', 'ffn1_norm_post', 'mix_norm_pre', 'w_in',
             'gate_bias', 'rel_table', 'w_attn_out', 'conv_glu_bias', 'conv_dw_w', 'conv_dw_b', 'conv_ln_g',
             'conv_ln_b', 'conv_w_out', 'w_out', 'mix_norm_post', 'ffn2_norm_pre', 'ffn2_w_gate', 'ffn2_w_up',
             'ffn2_w_down', 'ffn2_norm_post']
    big = ['ffn1_w_gate', 'ffn1_w_up', 'ffn1_w_down', 'w_in', 'w_attn_out', 'conv_w_out', 'w_out', 'ffn2_w_gate',
           'ffn2_w_up', 'ffn2_w_down']
    small = [n for n in names if n not in big]

    xs, target = x[0], loss_target[0]
    t, d = xs.shape
    cx, cy = lax.axis_index("x"), lax.axis_index("y")
    chip = 2 * cx + cy

    dw_shard = conv_dw_w[0, :, 0, :]
    cshard = dw_shard.shape[1]
    dw_all, _ = _all_devices("gather_dw", _pack([dw_shard]))
    dw_full = jnp.concatenate([dw_all[2 * j, :CONV_WIDTH, :cshard] for j in range(N_CHIPS)], axis=1)
    dw_full = jnp.pad(dw_full, ((0, CONV_HALO - CONV_WIDTH), (0, 0)))
    peers = [(1 - cx, cy), (cx, 1 - cy), (1 - cx, 1 - cy)]
    pos = jnp.stack([lax.axis_index("c"), chip] + [_chip_index(p) for p in peers]).astype(jnp.int32)
    gathered = _gather_weights([_cast_into("cast_" + n, pos, args[n][0]) for n in big])
    wg1, wu1, wd1, win, wao, wco, wout, wg2, wu2, wd2 = gathered

    n1 = _rms_fwd("ffn1_pre", xs, ffn1_norm_pre)
    a1, b1, s1, f1 = _ffn_fwd("ffn1", n1, wg1, wu1, wd1)
    h1, u = _post_res_pre("ffn1_post", f1, xs, ffn1_norm_post, mix_norm_pre, 0.5)
    proj = _mm_nblk("mix_in", u, win, trans_w=False, out_blocked=False, out_dtype=BF16)
    table_pad = jnp.pad(rel_table[0], ((0, 0), (0, REL_PAD - rel_table.shape[2])))
    bias = _bias_expand(table_pad)
    kp = jnp.pad(proj[:, D_ATTN:2 * D_ATTN], ((K_PAD, 0), (0, 0)))
    vp = jnp.pad(proj[:, 2 * D_ATTN:3 * D_ATTN], ((K_PAD, 0), (0, 0)))
    att, lse = _attn_fwd(proj, kp, vp, bias)
    cs, c_glu, z_conv = _conv_fwd(proj, conv_glu_bias, dw_full, conv_dw_b, conv_ln_g, conv_ln_b)
    y_a = _mm_nblk("attn_out", att, wao, trans_w=False, out_blocked=False, out_dtype=F32)
    y_b = _mm_nblk("conv_out", cs, wco, trans_w=False, out_blocked=False, out_dtype=F32)
    merged = _merge_fwd("mix_merge", y_a, y_b, proj, gate_bias)
    mo = _mm_kblk("mix_out", [(merged, wout)], trans_w=False, out_dtype=F32)
    h2, n2 = _post_res_pre("mix_post", mo, h1, mix_norm_post, ffn2_norm_pre, 1.0)
    a2, b2, s2, f2 = _ffn_fwd("ffn2", n2, wg2, wu2, wd2)
    dy, err2 = _final_loss("loss", f2, h2, ffn2_norm_post, target, 0.5)
    loss = lax.psum(0.5 * jnp.sum(err2) / d, ("x", "y", "c"))

    g = {}
    df2, g["ffn2_norm_post"] = _post_bwd_call("ffn2_d_post", dy, f2, ffn2_norm_post, 0.5)
    dn2, g["ffn2_w_gate"], g["ffn2_w_up"], g["ffn2_w_down"] = _ffn_bwd("ffn2", df2, n2, a2, b2, s2, wg2, wu2, wd2)
    dh2, dmo, g["ffn2_norm_pre"], g["mix_norm_post"] = _pre_bwd_post(
        "ffn2_d_pre", dn2, h2, ffn2_norm_pre, dy, mo, mix_norm_post, 1.0)
    dmerged = _mm_nblk("mix_d_merged", dmo, wout, trans_w=True, out_blocked=False, out_dtype=F32)
    g["w_out"] = _mm_tn("mix_g_out", merged, "col", dmo, "full")
    dy_a, dy_b, dgates, g["gate_bias"] = _merge_bwd("mix_d_merge", dmerged, y_a, y_b, proj, gate_bias)
    datt = _mm_kblk("attn_d_out", [(dy_a, wao)], trans_w=True, out_dtype=BF16)
    dcs = _mm_kblk("conv_d_out", [(dy_b, wco)], trans_w=True, out_dtype=F32)
    g["w_attn_out"] = _mm_tn("attn_g_out", att, "full", dy_a, "col")
    g["conv_w_out"] = _mm_tn("conv_g_out", cs, "full", dy_b, "col")
    dq, dkp, dvp, dbias = _attn_bwd(proj, kp, vp, bias, att, lse, datt)
    g["rel_table"] = _bias_fold(dbias)[:, :rel_table.shape[2]]
    dcin, g_dw, g["conv_dw_b"], g["conv_ln_g"], g["conv_ln_b"], g["conv_glu_bias"] = _conv_bwd(
        proj, c_glu, z_conv, dcs, conv_glu_bias, dw_full, conv_ln_g, conv_ln_b)
    dproj = jnp.concatenate([dq, dkp[K_PAD:].astype(BF16), dvp[K_PAD:].astype(BF16), dcin, dgates], axis=1)
    g["w_in"] = _mm_tn("mix_g_in", u, "full", dproj, "col")
    du = _mm_kblk("mix_d_in", [(dproj, win)], trans_w=True, out_dtype=F32)
    dh1, df1, g["mix_norm_pre"], g["ffn1_norm_post"] = _pre_bwd_post(
        "mix_d_pre", du, h1, mix_norm_pre, dh2, f1, ffn1_norm_post, 0.5)
    dn1, g["ffn1_w_gate"], g["ffn1_w_up"], g["ffn1_w_down"] = _ffn_bwd("ffn1", df1, n1, a1, b1, s1, wg1, wu1, wd1)
    grad_x, g["ffn1_norm_pre"] = _pre_bwd_first("ffn1_d_pre", dn1, xs, ffn1_norm_pre, dh1)

    theirs = _pair_exchange([g[n] for n in big])
    parts = [_add_pair("pair_sum_" + n, pos, g[n], t_) for n, t_ in zip(big, theirs)]
    landed = _to_owner(parts)
    halves = [_add_chips("chip_sum_" + n, pos, p_, l_) for n, p_, l_ in zip(big, parts, landed)]
    other_halves = _swap_halves(halves)

    g["conv_dw_w"] = g_dw[:CONV_WIDTH]
    _, small_sum = _all_devices("sum_small", _pack([g[n] for n in small]))
    for n, val in zip(small, _unpack(small_sum, [g[n].shape for n in small])):
        g[n] = val
    g["conv_dw_w"] = lax.dynamic_slice_in_dim(g["conv_dw_w"], chip * cshard, cshard, axis=1)

    grads, deltas, new_m, new_v = {}, {}, {}, {}
    for n, own, other in zip(big, halves, other_halves):
        gr, dl, m2, v2 = _adamw_halves("adamw_" + n, pos, args[n][0], args["m_" + n][0], args["v_" + n][0], own, other)
        grads[n], deltas[n], new_m[n], new_v[n] = gr[None], dl[None], m2[None], v2[None]
    shapes = [g[n].shape for n in small]
    packed = lambda pre: _pack([args[pre + n].reshape(shp) for n, shp in zip(small, shapes)])
    dl, m2, v2 = _adamw("adamw_small", packed(""), _pack([g[n] for n in small]), packed("m_"), packed("v_"))
    for n, a_, b_, c_ in zip(small, _unpack(dl, shapes), _unpack(m2, shapes), _unpack(v2, shapes)):
        shape = args[n].shape
        grads[n], deltas[n], new_m[n], new_v[n] = (g[n].reshape(shape), a_.reshape(shape), b_.reshape(shape),
                                                   c_.reshape(shape))

    return (loss, grad_x[None], *[grads[n] for n in names], *[deltas[n] for n in names],
            *[new_m[n] for n in names], *[new_v[n] for n in names])
```

```python
import functools

import numpy as np
import jax
import jax.numpy as jnp
from jax import lax
from jax.experimental import pallas as pl
from jax.experimental.pallas import tpu as pltpu

F32 = jnp.float32
BF16 = jnp.bfloat16
MESH = pl.DeviceIdType.MESH
ANY = pl.BlockSpec(memory_space=pl.ANY)

EPS = 1e-6
CHUNK = 64
LEFT_CHUNKS = 8
N_HEADS = 8
HEAD_DIM = 64
D_ATTN = N_HEADS * HEAD_DIM
D_CONV = 512
CONV_WIDTH = 31
REL_CLIP = 128
N_CHIPS = 4
N_DEV = 8
Q_BLOCK = 4 * CHUNK
K_PAD = LEFT_CHUNKS * CHUNK
K_WIN = K_PAD + Q_BLOCK
REL_EXT = 1024
REL_PAD = 384
CONV_HALO = 32
CONV_TILE = 256
COL = 512
NEG = -1e30

ADAM_LR = 0.001
ADAM_B1 = 0.9
ADAM_B2 = 0.999
ADAM_EPS = 1e-08
ADAM_WD = 0.01
ADAM_STEP = 10

VMEM_LIMIT_BYTES = 48 * 1024 * 1024


def _cparams(n_grid):
    return pltpu.CompilerParams(dimension_semantics=("arbitrary",) * n_grid, vmem_limit_bytes=VMEM_LIMIT_BYTES)


def _row_tile(rows, want):
    if rows <= want:
        return rows
    for t in range(want - want % 16, 0, -16):
        if rows % t == 0:
            return t
    raise ValueError((rows, want))


def _dot(a, w, trans_w):
    dims = (((1,), (1,)), ((), ())) if trans_w else (((1,), (0,)), ((), ()))
    return lax.dot_general(a, w, dims, preferred_element_type=F32)


class _Carry:
    LATE_STEPS = 2

    def __init__(self, ins, outs, aliases, sems, phases):
        self.ins, self.outs, self.aliases, self.sems, self.phases = ins, outs, aliases, sems, phases


def _call(name, core, grid, in_specs, out_specs, out_shape, scratch, args, carry=None):
    n_in, n_out, n_scr = len(in_specs), len(out_specs), len(scratch)
    if carry is None:
        out = pl.pallas_call(core, name=name, grid=grid, in_specs=in_specs, out_specs=out_specs, out_shape=out_shape,
                             scratch_shapes=scratch, compiler_params=_cparams(len(grid)))(*args)
        return list(out), []
    c_in, c_out = len(carry.ins), len(carry.outs)
    total = int(np.prod(grid))
    late = max(total - 1 - _Carry.LATE_STEPS, 0)

    def body(*refs):
        ins, refs = refs[:n_in], refs[n_in:]
        c_ins, refs = refs[:c_in], refs[c_in:]
        outs, refs = refs[:n_out], refs[n_out:]
        c_outs, refs = refs[:c_out], refs[c_out:]
        scr, c_sems = refs[:n_scr], refs[n_scr:]
        step = pl.program_id(0)
        for axis in range(1, len(grid)):
            step = step * grid[axis] + pl.program_id(axis)

        def run(when, at):
            for w, fn in carry.phases:
                if w == when:
                    pl.when(step == at)(functools.partial(fn, c_ins, c_outs, c_sems))

        run("first", 0)
        core(*ins, *outs, *scr)
        run("late", late)
        run("last", total - 1)

    out = pl.pallas_call(
        body, name=name, grid=grid, in_specs=list(in_specs) + [ANY] * c_in, out_specs=list(out_specs) + [ANY] * c_out,
        out_shape=list(out_shape) + list(carry.outs), scratch_shapes=list(scratch) + list(carry.sems),
        input_output_aliases={n_in + a: n_out + b for a, b in carry.aliases.items()},
        compiler_params=_cparams(len(grid)),
    )(*args, *carry.ins)
    return list(out[:n_out]), list(out[n_out:])


def _mm_nblk(name, a, w, *, trans_w, out_blocked, out_dtype, tm=1024, carry=None):
    m, k = a.shape
    nj = w.shape[0]
    nb = w.shape[1] if trans_w else w.shape[2]
    tm = _row_tile(m, tm)

    def core(a_ref, w_ref, o_ref):
        o_ref[...] = _dot(a_ref[...], w_ref[...], trans_w).astype(o_ref.dtype)

    if out_blocked:
        out_shape, out_spec = (nj, m, nb), pl.BlockSpec((None, tm, nb), lambda j, i: (j, i, 0))
    else:
        out_shape, out_spec = (m, nj * nb), pl.BlockSpec((tm, nb), lambda j, i: (i, j))
    out, carried = _call(
        name, core, (nj, m // tm),
        [pl.BlockSpec((tm, k), lambda j, i: (i, 0)), pl.BlockSpec((None,) + w.shape[1:], lambda j, i: (j, 0, 0))],
        [out_spec], [jax.ShapeDtypeStruct(out_shape, out_dtype)], [], [a, w], carry)
    return out[0] if carry is None else (out[0], carried)


def _mm_kblk(name, pairs, *, trans_w, out_dtype=F32, tm=512, epilogue=None, rows=(), vecs=(), row_outs=None,
             vec_outs=(), carry=None):
    w0 = pairs[0][1]
    nj = w0.shape[0]
    n = w0.shape[1] if trans_w else w0.shape[2]
    kb = w0.shape[2] if trans_w else w0.shape[1]
    m = pairs[0][0].shape[-2]
    tm = _row_tile(m, tm)
    n_pairs, n_rows, n_vecs = len(pairs), len(rows), len(vecs)
    if epilogue is None:
        epilogue, row_outs = (lambda acc, r, v: ([acc], [])), [(n, out_dtype)]
    n_ro, n_vo = len(row_outs), len(vec_outs)

    def core(*refs):
        pair_refs, refs = refs[:2 * n_pairs], refs[2 * n_pairs:]
        row_refs, refs = refs[:n_rows], refs[n_rows:]
        vec_refs, refs = refs[:n_vecs], refs[n_vecs:]
        ro_refs, refs = refs[:n_ro], refs[n_ro:]
        vo_refs, acc_ref = refs[:n_vo], refs[n_vo]
        i, j = pl.program_id(0), pl.program_id(1)

        @pl.when(j == 0)
        def _():
            acc_ref[...] = jnp.zeros_like(acc_ref)

        if n_vo:
            @pl.when((i == 0) & (j == 0))
            def _():
                for ref in vo_refs:
                    ref[...] = jnp.zeros_like(ref)

        part = _dot(pair_refs[0][...], pair_refs[1][...], trans_w)
        for p in range(1, n_pairs):
            part = part + _dot(pair_refs[2 * p][...], pair_refs[2 * p + 1][...], trans_w)
        acc_ref[...] += part

        @pl.when(j == nj - 1)
        def _():
            ro, vo = epilogue(acc_ref[...], [r[...] for r in row_refs], [v[...] for v in vec_refs])
            for ref, val in zip(ro_refs, ro):
                ref[...] = val.astype(ref.dtype)
            for ref, val in zip(vo_refs, vo):
                ref[...] += val

    in_specs, args = [], []
    for a, w in pairs:
        if a.ndim == 3:
            in_specs.append(pl.BlockSpec((None, tm, kb), lambda i, j: (j, i, 0)))
        else:
            in_specs.append(pl.BlockSpec((tm, kb), lambda i, j: (i, j)))
        in_specs.append(pl.BlockSpec((None,) + w.shape[1:], lambda i, j: (j, 0, 0)))
        args += [a, w]
    in_specs += [pl.BlockSpec((tm, r.shape[1]), lambda i, j: (i, 0)) for r in rows]
    in_specs += [pl.BlockSpec(v.shape, lambda i, j: (0, 0)) for v in vecs]
    out_specs = [pl.BlockSpec((tm, cols), lambda i, j: (i, 0)) for cols, _ in row_outs]
    out_specs += [pl.BlockSpec((1, cols), lambda i, j: (0, 0)) for cols in vec_outs]
    out_shape = [jax.ShapeDtypeStruct((m, cols), dt) for cols, dt in row_outs]
    out_shape += [jax.ShapeDtypeStruct((1, cols), F32) for cols in vec_outs]
    return _call(name, core, (m // tm, nj), in_specs, out_specs, out_shape, [pltpu.VMEM((tm, n), F32)],
                 args + list(rows) + list(vecs), carry)


def _mm_tn(name, a, a_mode, b, b_mode, *, out_dtype=BF16, tt=1024):
    nj = N_CHIPS
    t = a.shape[-2]
    tt = _row_tile(t, tt)

    def spec(x, mode):
        if mode == "full":
            return x.shape[1], pl.BlockSpec((tt, x.shape[1]), lambda j, s: (s, 0))
        if mode == "col":
            cb = x.shape[1] // nj
            return cb, pl.BlockSpec((tt, cb), lambda j, s: (s, j))
        return x.shape[2], pl.BlockSpec((None, tt, x.shape[2]), lambda j, s: (j, s, 0))

    ca, a_spec = spec(a, a_mode)
    cb, b_spec = spec(b, b_mode)
    n_steps = t // tt

    def body(a_ref, b_ref, o_ref, acc_ref):
        s = pl.program_id(1)

        @pl.when(s == 0)
        def _():
            acc_ref[...] = jnp.zeros_like(acc_ref)

        acc_ref[...] += lax.dot_general(a_ref[...], b_ref[...], (((0,), (0,)), ((), ())),
                                        preferred_element_type=F32)

        @pl.when(s == n_steps - 1)
        def _():
            o_ref[...] = acc_ref[...].astype(o_ref.dtype)

    return pl.pallas_call(
        body, name=name, grid=(nj, n_steps), in_specs=[a_spec, b_spec],
        out_specs=pl.BlockSpec((None, ca, cb), lambda j, s: (j, 0, 0)),
        out_shape=jax.ShapeDtypeStruct((nj, ca, cb), out_dtype),
        scratch_shapes=[pltpu.VMEM((ca, cb), F32)], compiler_params=_cparams(2),
    )(a, b)


def _rowwise(name, fn, rows, vecs, row_outs, vec_outs, *, tm=256):
    nrows = rows[0][0].shape[0]
    tm = _row_tile(nrows, tm)
    n_r, n_v, n_ro, n_vo = len(rows), len(vecs), len(row_outs), len(vec_outs)

    def body(*refs):
        r_vals = [r[...] for r in refs[:n_r]]
        v_vals = [r[...] for r in refs[n_r:n_r + n_v]]
        ro_refs = refs[n_r + n_v:n_r + n_v + n_ro]
        vo_refs = refs[n_r + n_v + n_ro:]
        ro, vo = fn(r_vals, v_vals)
        for ref, val in zip(ro_refs, ro):
            ref[...] = val.astype(ref.dtype)
        if n_vo:
            @pl.when(pl.program_id(0) == 0)
            def _():
                for ref in vo_refs:
                    ref[...] = jnp.zeros_like(ref)

            for ref, val in zip(vo_refs, vo):
                ref[...] += val

    in_specs = [pl.BlockSpec((tm, cols), functools.partial(lambda i, cb: (i, cb), cb=cb)) for _, cols, cb in rows]
    in_specs += [pl.BlockSpec(v.shape, functools.partial(lambda i, nd: (0,) * nd, nd=v.ndim)) for v in vecs]
    out_specs = [pl.BlockSpec((tm, cols), lambda i: (i, 0)) for cols, _ in row_outs]
    out_specs += [pl.BlockSpec((1, cols), lambda i: (0, 0)) for cols in vec_outs]
    out_shape = [jax.ShapeDtypeStruct((nrows, cols), dt) for cols, dt in row_outs]
    out_shape += [jax.ShapeDtypeStruct((1, cols), F32) for cols in vec_outs]
    return pl.pallas_call(
        body, name=name, grid=(nrows // tm,), in_specs=in_specs, out_specs=out_specs, out_shape=out_shape,
        compiler_params=_cparams(1),
    )(*[r[0] for r in rows], *vecs)


def _whole(x):
    return (x, x.shape[1], 0)


def _colsum(x):
    return jnp.sum(x, axis=0, keepdims=True)


def _rstd(x):
    return lax.rsqrt(jnp.mean(x * x, axis=-1, keepdims=True) + EPS)


def _rms_bwd(dn, x, g):
    r = _rstd(x)
    c = dn * g
    dx = r * c - x * (r * r * r) * jnp.mean(c * x, axis=-1, keepdims=True)
    return dx, _colsum(dn * x * r)


def _rms_fwd(name, x, g):
    def fn(r, v):
        (xv,), (gv,) = r, v
        return [xv * _rstd(xv) * gv], []

    return _rowwise(name, fn, [_whole(x)], [g], [(x.shape[1], BF16)], [])[0]


def _ep_post_res_pre(scale):
    def epilogue(acc, rows, vecs):
        (resid,), (g_post, g_next) = rows, vecs
        h = resid + scale * (acc * _rstd(acc) * g_post)
        return [acc, h, h * _rstd(h) * g_next], []

    return epilogue


def _post_bwd(dh, f, g_post, scale):
    return _rms_bwd(scale * dh, f, g_post)


def _ep_loss(scale, d):
    def epilogue(acc, rows, vecs):
        (resid, target), (g_post,) = rows, vecs
        err = resid + scale * (acc * _rstd(acc) * g_post) - target
        dy = err * (1.0 / d)
        df, dg_post = _post_bwd(dy, acc, g_post, scale)
        return [dy, df], [_colsum(err * err), dg_post]

    return epilogue


def _ep_pre_bwd_post(scale_prev):
    def epilogue(acc, rows, vecs):
        (h, dh_up, f_prev), (g_pre, g_post_prev) = rows, vecs
        dx, dg_pre = _rms_bwd(acc, h, g_pre)
        dh = dh_up + dx
        df, dg_post = _post_bwd(dh, f_prev, g_post_prev, scale_prev)
        return [dh, df], [dg_pre, dg_post]

    return epilogue


def _ep_pre_bwd_first():
    def epilogue(acc, rows, vecs):
        (x, dh_up), (g_pre,) = rows, vecs
        dx, dg_pre = _rms_bwd(acc, x, g_pre)
        return [dh_up + dx], [dg_pre]

    return epilogue


def _gate_pieces(proj, d):
    first = (3 * D_ATTN + 2 * D_CONV) // COL
    n = d // COL
    return [(proj, COL, first + p) for p in range(2 * n)], n


def _merge_fwd(name, y_a, y_b, proj, gate_bias):
    d = y_a.shape[1]
    pieces, n = _gate_pieces(proj, d)

    def fn(r, v):
        ya, yb = r[0], r[1]
        g = jnp.concatenate([p.astype(F32) for p in r[2:]], axis=1) + v[0]
        gates = jax.nn.sigmoid(g)
        return [gates[:, :d] * ya + gates[:, d:] * yb], []

    return _rowwise(name, fn, [_whole(y_a), _whole(y_b)] + pieces, [gate_bias], [(d, BF16)], [])[0]


def _merge_bwd(name, dmerged, y_a, y_b, proj, gate_bias):
    d = y_a.shape[1]
    pieces, n = _gate_pieces(proj, d)

    def fn(r, v):
        dm, ya, yb = r[0], r[1], r[2]
        g = jnp.concatenate([p.astype(F32) for p in r[3:]], axis=1) + v[0]
        gates = jax.nn.sigmoid(g)
        ga, gb = gates[:, :d], gates[:, d:]
        dga = dm * ya * ga * (1.0 - ga)
        dgb = dm * yb * gb * (1.0 - gb)
        dgate = jnp.concatenate([dga, dgb], axis=1)
        return [dm * ga, dm * gb, dgate], [_colsum(dgate)]

    return _rowwise(name, fn, [_whole(dmerged), _whole(y_a), _whole(y_b)] + pieces, [gate_bias],
                    [(d, BF16), (d, BF16), (2 * d, BF16)], [2 * d])


def _adamw_math(wv, gv, mv, vv):
    m2 = ADAM_B1 * mv + (1.0 - ADAM_B1) * gv
    v2 = ADAM_B2 * vv + (1.0 - ADAM_B2) * (gv * gv)
    m_hat = m2 / (1.0 - ADAM_B1 ** ADAM_STEP)
    v_hat = v2 / (1.0 - ADAM_B2 ** ADAM_STEP)
    delta = -ADAM_LR * (m_hat / (jnp.sqrt(v_hat) + ADAM_EPS) + ADAM_WD * wv)
    return delta, m2, v2


def _adamw(name, w, g, m, v):
    def fn(r, _):
        return list(_adamw_math(*r)), []

    c = w.shape[1]
    return _rowwise(name, fn, [_whole(w), _whole(g), _whole(m), _whole(v)], [], [(c, F32)] * 3, [], tm=256)


POS_C, POS_CHIP, POS_PEER = 0, 1, 2


def _placed_call(body, name, pos, grid, in_specs, out_specs, out_shape, args):
    return pl.pallas_call(
        body, name=name, out_shape=out_shape, compiler_params=_cparams(len(grid)),
        grid_spec=pltpu.PrefetchScalarGridSpec(num_scalar_prefetch=1, grid=grid, in_specs=in_specs,
                                               out_specs=out_specs),
    )(pos, *args)


def _cast_into(name, pos, w):
    r, cols = w.shape
    tm = _row_tile(r, 256)

    def body(pos_ref, w_ref, o_ref):
        o_ref[...] = w_ref[...].astype(o_ref.dtype)

    return _placed_call(
        body, name, pos, (r // tm,), [pl.BlockSpec((tm, cols), lambda i, pos: (i, 0))],
        pl.BlockSpec((None, tm, cols), lambda i, pos: (pos[POS_CHIP], i, 0)),
        jax.ShapeDtypeStruct((N_CHIPS, r, cols), BF16), [w])


def _add_pair(name, pos, grad, landed):
    nj, half, cols = landed.shape
    tm = _row_tile(half, 256)
    nb = half // tm

    def body(pos_ref, g_ref, l_ref, o_ref):
        o_ref[...] = (g_ref[...].astype(F32) + l_ref[...].astype(F32)).astype(o_ref.dtype)

    spec = pl.BlockSpec((None, tm, cols), lambda j, i, pos: (j, i, 0))
    return _placed_call(
        body, name, pos, (nj, nb),
        [pl.BlockSpec((None, tm, cols), lambda j, i, pos: (j, pos[POS_C] * nb + i, 0)), spec], spec,
        jax.ShapeDtypeStruct(landed.shape, BF16), [grad, landed])


def _add_chips(name, pos, part, landed):
    _, half, cols = landed.shape
    tm = _row_tile(half, 256)

    def body(pos_ref, p_ref, l0_ref, l1_ref, l2_ref, o_ref):
        acc = p_ref[...].astype(F32)
        for ref in (l0_ref, l1_ref, l2_ref):
            acc = acc + ref[...].astype(F32)
        o_ref[...] = acc

    slot = lambda at: pl.BlockSpec((None, tm, cols), functools.partial(lambda i, pos, at: (pos[at], i, 0), at=at))
    return _placed_call(
        body, name, pos, (half // tm,), [slot(POS_CHIP)] + [slot(POS_PEER + k) for k in range(3)],
        pl.BlockSpec((tm, cols), lambda i, pos: (i, 0)), jax.ShapeDtypeStruct((half, cols), F32),
        [part, landed, landed, landed])


def _adamw_halves(name, pos, w, m, v, own, landed):
    r, cols = w.shape
    half = own.shape[0]
    tm = _row_tile(half, 256)
    nb = half // tm

    def body(pos_ref, w_ref, m_ref, v_ref, own_ref, land_ref, g_out, d_out, m_out, v_out):
        mine = pl.program_id(0) == pos_ref[POS_C]
        g = jnp.where(mine, own_ref[...], land_ref[...])
        delta, m2, v2 = _adamw_math(w_ref[...], g, m_ref[...], v_ref[...])
        g_out[...] = g
        d_out[...] = delta
        m_out[...] = m2
        v_out[...] = v2

    full = pl.BlockSpec((tm, cols), lambda h, i, pos: (h * nb + i, 0))
    part = pl.BlockSpec((tm, cols), lambda h, i, pos: (i, 0))
    return _placed_call(
        body, name, pos, (2, nb), [full, full, full, part, part], [full] * 4,
        [jax.ShapeDtypeStruct((r, cols), F32)] * 4, [w, m, v, own, landed])


def _rel_onehot():
    e = np.arange(REL_EXT)
    dist = K_PAD - (e - (Q_BLOCK - 1))
    idx = np.clip(dist, -REL_CLIP, REL_CLIP) + REL_CLIP
    return (np.arange(REL_PAD)[:, None] == idx[None, :]).astype(np.float32)


def _band_valid():
    qc = lax.broadcasted_iota(jnp.int32, (Q_BLOCK, K_WIN), 0) // CHUNK
    kc = lax.broadcasted_iota(jnp.int32, (Q_BLOCK, K_WIN), 1) // CHUNK
    return (kc >= qc) & (kc <= qc + LEFT_CHUNKS)


def _skew(x, left):
    row = lax.broadcasted_iota(jnp.int32, x.shape, 0)
    for bit in range(Q_BLOCK.bit_length() - 1):
        amount = 1 << bit
        rolled = pltpu.roll(x, REL_EXT - amount if left else amount, 1)
        x = jnp.where((row >> bit) & 1 == 1, rolled, x)
    return x


def _bias_expand(table_pad):
    onehot = jnp.asarray(_rel_onehot())

    def body(t_ref, oh_ref, o_ref):
        ext = jnp.dot(t_ref[...], oh_ref[...], precision=lax.Precision.HIGHEST, preferred_element_type=F32)
        valid = _band_valid()
        for h in range(N_HEADS):
            rows = jnp.broadcast_to(ext[h:h + 1, :], (Q_BLOCK, REL_EXT))
            rolled = _skew(pltpu.roll(rows, REL_EXT - (Q_BLOCK - 1), 1), left=False)
            o_ref[h] = jnp.where(valid, rolled[:, :K_WIN], NEG)

    return pl.pallas_call(
        body, name="bias_expand", out_shape=jax.ShapeDtypeStruct((N_HEADS, Q_BLOCK, K_WIN), F32),
        compiler_params=pltpu.CompilerParams(vmem_limit_bytes=VMEM_LIMIT_BYTES),
    )(table_pad, onehot)


def _bias_fold(dbias):
    onehot_t = jnp.asarray(_rel_onehot().T)

    def body(d_ref, oh_ref, o_ref, ext_ref):
        for h in range(N_HEADS):
            x = jnp.concatenate([d_ref[h], jnp.zeros((Q_BLOCK, REL_EXT - K_WIN), F32)], axis=1)
            rolled = _skew(pltpu.roll(x, Q_BLOCK - 1, 1), left=True)
            ext_ref[h:h + 1, :] = jnp.sum(rolled, axis=0, keepdims=True)
        o_ref[...] = jnp.dot(ext_ref[...], oh_ref[...], precision=lax.Precision.HIGHEST,
                             preferred_element_type=F32)

    return pl.pallas_call(
        body, name="bias_fold", out_shape=jax.ShapeDtypeStruct((N_HEADS, REL_PAD), F32),
        scratch_shapes=[pltpu.VMEM((N_HEADS, REL_EXT), F32)],
        compiler_params=pltpu.CompilerParams(vmem_limit_bytes=VMEM_LIMIT_BYTES),
    )(dbias, onehot_t)


def _scores(q, k, bias, i):
    s = lax.dot_general(q, k, (((1,), (1,)), ((), ())), preferred_element_type=F32) * (HEAD_DIM ** -0.5) + bias
    kpos = lax.broadcasted_iota(jnp.int32, (Q_BLOCK, K_WIN), 1) + i * Q_BLOCK
    return jnp.where(kpos >= K_PAD, s, NEG)


def _attn_specs(n_kv):
    q_spec = pl.BlockSpec((Q_BLOCK, 2 * HEAD_DIM), lambda p, i: (i, p))
    kv_specs = [pl.BlockSpec((Q_BLOCK, 2 * HEAD_DIM), functools.partial(lambda p, i, kk: (i + kk, p), kk=kk))
                for _ in range(n_kv) for kk in range(K_WIN // Q_BLOCK)]
    bias_spec = pl.BlockSpec((2, Q_BLOCK, K_WIN), lambda p, i: (p, 0, 0))
    return q_spec, kv_specs, bias_spec


def _attn_fwd(proj, kp, vp, bias):
    t = proj.shape[0]
    n_win = K_WIN // Q_BLOCK

    def body(q_ref, *refs):
        k_refs, v_refs = refs[:n_win], refs[n_win:2 * n_win]
        b_ref, o_ref, lse_ref = refs[2 * n_win:]
        i = pl.program_id(1)
        k = jnp.concatenate([r[...] for r in k_refs], axis=0)
        v = jnp.concatenate([r[...] for r in v_refs], axis=0)
        q = q_ref[...]
        for hh in range(2):
            lanes = slice(hh * HEAD_DIM, (hh + 1) * HEAD_DIM)
            s = _scores(q[:, lanes], k[:, lanes], b_ref[hh], i)
            m = jnp.max(s, axis=1, keepdims=True)
            p = jnp.exp(s - m)
            l = jnp.sum(p, axis=1, keepdims=True)
            o = jnp.dot(p.astype(BF16), v[:, lanes], preferred_element_type=F32) / l
            o_ref[:, lanes] = o.astype(o_ref.dtype)
            lse_ref[:, lanes] = jnp.broadcast_to(m + jnp.log(l), (Q_BLOCK, HEAD_DIM))

    q_spec, kv_specs, bias_spec = _attn_specs(2)
    out_spec = pl.BlockSpec((Q_BLOCK, 2 * HEAD_DIM), lambda p, i: (i, p))
    return pl.pallas_call(
        body, name="attn_fwd", grid=(N_HEADS // 2, t // Q_BLOCK),
        in_specs=[q_spec] + kv_specs + [bias_spec], out_specs=[out_spec, out_spec],
        out_shape=[jax.ShapeDtypeStruct((t, D_ATTN), BF16), jax.ShapeDtypeStruct((t, D_ATTN), F32)],
        compiler_params=_cparams(2),
    )(proj, *([kp] * n_win), *([vp] * n_win), bias)


def _attn_bwd(proj, kp, vp, bias, att, lse, datt):
    t = proj.shape[0]
    n_win = K_WIN // Q_BLOCK

    def body(q_ref, *refs):
        k_refs, v_refs = refs[:n_win], refs[n_win:2 * n_win]
        b_ref, o_ref, lse_ref, do_ref, dq_ref, dk_ref, dv_ref, db_ref = refs[2 * n_win:]
        i = pl.program_id(1)

        @pl.when(i == 0)
        def _():
            dk_ref[...] = jnp.zeros_like(dk_ref)
            dv_ref[...] = jnp.zeros_like(dv_ref)
            db_ref[...] = jnp.zeros_like(db_ref)

        k = jnp.concatenate([r[...] for r in k_refs], axis=0)
        v = jnp.concatenate([r[...] for r in v_refs], axis=0)
        q = q_ref[...]
        do = do_ref[...]
        rows = pl.ds(pl.multiple_of(i * Q_BLOCK, Q_BLOCK), K_WIN)
        scale = HEAD_DIM ** -0.5
        for hh in range(2):
            lanes = slice(hh * HEAD_DIM, (hh + 1) * HEAD_DIM)
            qh, kh, vh, doh = q[:, lanes], k[:, lanes], v[:, lanes], do[:, lanes]
            s = _scores(qh, kh, b_ref[hh], i)
            p = jnp.exp(s - lse_ref[:, lanes][:, :1])
            dp = lax.dot_general(doh, vh, (((1,), (1,)), ((), ())), preferred_element_type=F32)
            delta = jnp.sum(doh.astype(F32) * o_ref[:, lanes].astype(F32), axis=1, keepdims=True)
            ds = p * (dp - delta)
            db_ref[hh] += ds
            dsb = ds.astype(BF16)
            dq_ref[:, lanes] = (jnp.dot(dsb, kh, preferred_element_type=F32) * scale).astype(dq_ref.dtype)
            dk_ref[rows, lanes] += lax.dot_general(dsb, qh, (((0,), (0,)), ((), ())),
                                                   preferred_element_type=F32) * scale
            dv_ref[rows, lanes] += lax.dot_general(p.astype(BF16), doh, (((0,), (0,)), ((), ())),
                                                   preferred_element_type=F32)

    q_spec, kv_specs, bias_spec = _attn_specs(2)
    row_spec = pl.BlockSpec((Q_BLOCK, 2 * HEAD_DIM), lambda p, i: (i, p))
    full_spec = pl.BlockSpec((t + K_PAD, 2 * HEAD_DIM), lambda p, i: (0, p))
    return pl.pallas_call(
        body, name="attn_bwd", grid=(N_HEADS // 2, t // Q_BLOCK),
        in_specs=[q_spec] + kv_specs + [bias_spec, row_spec, row_spec, row_spec],
        out_specs=[row_spec, full_spec, full_spec, bias_spec],
        out_shape=[jax.ShapeDtypeStruct((t, D_ATTN), BF16), jax.ShapeDtypeStruct((t + K_PAD, D_ATTN), F32),
                   jax.ShapeDtypeStruct((t + K_PAD, D_ATTN), F32),
                   jax.ShapeDtypeStruct((N_HEADS, Q_BLOCK, K_WIN), F32)],
        compiler_params=_cparams(2),
    )(proj, *([kp] * n_win), *([vp] * n_win), bias, att, lse, datt)


CONV_LEAD = CONV_HALO - (CONV_WIDTH - 1)
CONV_LANES = 128
CONV_ROWS = 64


def _conv_specs(t):
    tt = _row_tile(t, CONV_TILE)
    per = tt // CONV_HALO
    n_halo = t // CONV_HALO
    tile = lambda cb: pl.BlockSpec((tt, COL), functools.partial(lambda i, cb: (i, cb), cb=cb))
    prev = lambda cb: pl.BlockSpec((CONV_HALO, COL),
                                   functools.partial(lambda i, cb: (jnp.maximum(i * per - 1, 0), cb), cb=cb))
    nxt = lambda cb: pl.BlockSpec((CONV_HALO, COL),
                                  functools.partial(lambda i, cb: (jnp.minimum((i + 1) * per, n_halo - 1), cb), cb=cb))
    vec = lambda shape: pl.BlockSpec(shape, lambda i: (0, 0))
    return tt, tile, prev, nxt, vec


def _glu(ca, cg, bias):
    return (ca.astype(F32) + bias[:, :D_CONV]) * jax.nn.sigmoid(cg.astype(F32) + bias[:, D_CONV:])


def _taps(ext_ref, tt, first_row, weight_of, out_ref):
    for r0 in range(0, tt, CONV_ROWS):
        for l0 in range(0, D_CONV, CONV_LANES):
            lanes = slice(l0, l0 + CONV_LANES)
            acc = jnp.zeros((CONV_ROWS, CONV_LANES), F32)
            for w in range(CONV_WIDTH):
                acc = acc + ext_ref[first_row(w) + r0:first_row(w) + r0 + CONV_ROWS, lanes] * weight_of(w)[:, lanes]
            out_ref[r0:r0 + CONV_ROWS, lanes] = acc


def _conv_fwd(proj, glu_bias, dw, dw_b, ln_g, ln_b):
    t = proj.shape[0]
    tt, tile, prev, nxt, vec = _conv_specs(t)
    ca_blk, cg_blk = 3 * D_ATTN // COL, 3 * D_ATTN // COL + 1

    def body(ca_ref, cg_ref, pa_ref, pg_ref, gb_ref, dw_ref, dwb_ref, g_ref, b_ref, cs_ref, c_ref, z_ref, ext_ref):
        i = pl.program_id(0)
        bias = gb_ref[...]
        c = _glu(ca_ref[...], cg_ref[...], bias)
        halo = _glu(pa_ref[...], pg_ref[...], bias)
        ext_ref[0:CONV_HALO, :] = jnp.where(i == 0, 0.0, halo)
        ext_ref[CONV_HALO:, :] = c
        c_ref[...] = c
        _taps(ext_ref, tt, lambda w: CONV_LEAD + w, lambda w: dw_ref[w:w + 1, :], z_ref)
        z = z_ref[...] + dwb_ref[...]
        z_ref[...] = z
        mu = jnp.mean(z, axis=-1, keepdims=True)
        zc = z - mu
        y = zc * lax.rsqrt(jnp.mean(zc * zc, axis=-1, keepdims=True) + EPS) * g_ref[...] + b_ref[...]
        cs_ref[...] = (y * jax.nn.sigmoid(y)).astype(cs_ref.dtype)

    out_spec = pl.BlockSpec((tt, D_CONV), lambda i: (i, 0))
    return pl.pallas_call(
        body, name="conv_fwd", grid=(t // tt,),
        in_specs=[tile(ca_blk), tile(cg_blk), prev(ca_blk), prev(cg_blk), vec(glu_bias.shape), vec(dw.shape),
                  vec(dw_b.shape), vec(ln_g.shape), vec(ln_b.shape)],
        out_specs=[out_spec] * 3,
        out_shape=[jax.ShapeDtypeStruct((t, D_CONV), BF16), jax.ShapeDtypeStruct((t, D_CONV), F32),
                   jax.ShapeDtypeStruct((t, D_CONV), F32)],
        scratch_shapes=[pltpu.VMEM((tt + CONV_HALO, D_CONV), F32)], compiler_params=_cparams(1),
    )(proj, proj, proj, proj, glu_bias, dw, dw_b, ln_g, ln_b)


def _conv_bwd(proj, c, z, dcs, glu_bias, dw, ln_g, ln_b):
    t = proj.shape[0]
    tt, tile, prev, nxt, vec = _conv_specs(t)
    n_tiles = t // tt
    ca_blk, cg_blk = 3 * D_ATTN // COL, 3 * D_ATTN // COL + 1

    def ln_bwd(zv, dcsv, g, b):
        mu = jnp.mean(zv, axis=-1, keepdims=True)
        zc = zv - mu
        rstd = lax.rsqrt(jnp.mean(zc * zc, axis=-1, keepdims=True) + EPS)
        zhat = zc * rstd
        y = zhat * g + b
        sig = jax.nn.sigmoid(y)
        dy = dcsv * sig * (1.0 + y * (1.0 - sig))
        dzh = dy * g
        dz = rstd * (dzh - jnp.mean(dzh, axis=-1, keepdims=True) - zhat * jnp.mean(dzh * zhat, axis=-1, keepdims=True))
        return dz, dy, zhat

    def body(ca_ref, cg_ref, c_ref, cprev_ref, z_ref, znext_ref, dcs_ref, dcsnext_ref, gb_ref, dw_ref, g_ref, b_ref,
             dcin_ref, ddw_ref, ddwb_ref, dg_ref, db_ref, dgb_ref, cext_ref, dzext_ref, dc_ref):
        i = pl.program_id(0)

        @pl.when(i == 0)
        def _():
            for ref in (ddw_ref, ddwb_ref, dg_ref, db_ref, dgb_ref):
                ref[...] = jnp.zeros_like(ref)

        g, b = g_ref[...], b_ref[...]
        dz, dy, zhat = ln_bwd(z_ref[...], dcs_ref[...], g, b)
        dz_next, _, _ = ln_bwd(znext_ref[...], dcsnext_ref[...], g, b)
        dg_ref[...] += _colsum(dy * zhat)
        db_ref[...] += _colsum(dy)
        ddwb_ref[...] += _colsum(dz)
        dzext_ref[0:tt, :] = dz
        dzext_ref[tt:, :] = jnp.where(i == n_tiles - 1, 0.0, dz_next)
        cext_ref[0:CONV_HALO, :] = jnp.where(i == 0, 0.0, cprev_ref[...])
        cext_ref[CONV_HALO:, :] = c_ref[...]
        _taps(dzext_ref, tt, lambda w: CONV_WIDTH - 1 - w, lambda w: dw_ref[w:w + 1, :], dc_ref)
        for w in range(CONV_WIDTH):
            ddw_ref[w:w + 1, :] += _colsum(cext_ref[CONV_LEAD + w:CONV_LEAD + w + tt, :] * dz)
        bias = gb_ref[...]
        a_in = ca_ref[...].astype(F32) + bias[:, :D_CONV]
        sg = jax.nn.sigmoid(cg_ref[...].astype(F32) + bias[:, D_CONV:])
        dc = dc_ref[...]
        dcin = jnp.concatenate([dc * sg, dc * a_in * sg * (1.0 - sg)], axis=1)
        dcin_ref[...] = dcin.astype(dcin_ref.dtype)
        dgb_ref[...] += _colsum(dcin)

    row = lambda: pl.BlockSpec((tt, D_CONV), lambda i: (i, 0))
    per = tt // CONV_HALO
    n_halo = t // CONV_HALO
    prev_row = pl.BlockSpec((CONV_HALO, D_CONV), lambda i: (jnp.maximum(i * per - 1, 0), 0))
    next_row = lambda: pl.BlockSpec((CONV_HALO, D_CONV), lambda i: (jnp.minimum((i + 1) * per, n_halo - 1), 0))
    acc = lambda shape: pl.BlockSpec(shape, lambda i: (0, 0))
    return pl.pallas_call(
        body, name="conv_bwd", grid=(n_tiles,),
        in_specs=[tile(ca_blk), tile(cg_blk), row(), prev_row, row(), next_row(), row(), next_row(),
                  vec(glu_bias.shape), vec(dw.shape), vec(ln_g.shape), vec(ln_b.shape)],
        out_specs=[pl.BlockSpec((tt, 2 * D_CONV), lambda i: (i, 0)), acc(dw.shape), acc((1, D_CONV)),
                   acc((1, D_CONV)), acc((1, D_CONV)), acc((1, 2 * D_CONV))],
        out_shape=[jax.ShapeDtypeStruct((t, 2 * D_CONV), BF16), jax.ShapeDtypeStruct(dw.shape, F32),
                   jax.ShapeDtypeStruct((1, D_CONV), F32), jax.ShapeDtypeStruct((1, D_CONV), F32),
                   jax.ShapeDtypeStruct((1, D_CONV), F32), jax.ShapeDtypeStruct((1, 2 * D_CONV), F32)],
        scratch_shapes=[pltpu.VMEM((tt + CONV_HALO, D_CONV), F32), pltpu.VMEM((tt + CONV_HALO, D_CONV), F32),
                        pltpu.VMEM((tt, D_CONV), F32)],
        compiler_params=_cparams(1),
    )(proj, proj, c, c, z, z, dcs, dcs, glu_bias, dw, ln_g, ln_b)


def _place():
    x, y, c = lax.axis_index("x"), lax.axis_index("y"), lax.axis_index("c")
    chips = [(1 - x, y), (x, 1 - y), (1 - x, 1 - y)]
    return x, y, c, chips


def _chip_index(chip):
    return 2 * chip[0] + chip[1]


def _half_rows(c, half):
    return pl.ds(pl.multiple_of(c * half, 16), half)


def _run_carry(name, carry):
    c_in, c_out = len(carry.ins), len(carry.outs)

    def body(*refs):
        ins, outs, sems = refs[:c_in], refs[c_in:c_in + c_out], refs[c_in + c_out:]
        for when in ("first", "late", "last"):
            for w, fn in carry.phases:
                if w == when:
                    fn(ins, outs, sems)

    return pl.pallas_call(
        body, name=name, in_specs=[ANY] * c_in, out_specs=[ANY] * c_out, out_shape=list(carry.outs),
        scratch_shapes=list(carry.sems), input_output_aliases=dict(carry.aliases),
    )(*carry.ins)


def _gather_carry(blocked):
    n = len(blocked)

    def over_ici(o_refs, send_sems, recv_sems):
        x, y, c, chips = _place()
        me = _chip_index((x, y))
        copies = []
        for a in range(n):
            mine = o_refs[a].at[me, _half_rows(c, o_refs[a].shape[1] // 2), :]
            for k, chip in enumerate(chips):
                copies.append(pltpu.make_async_remote_copy(
                    src_ref=mine, dst_ref=mine, send_sem=send_sems.at[6 * a + k], recv_sem=recv_sems.at[6 * a + k],
                    device_id=(chip[0], chip[1], c), device_id_type=MESH))
        return copies

    def to_sibling(o_refs, send_sems, recv_sems, sent_by_me):
        x, y, c, chips = _place()
        copies = []
        for a in range(n):
            rows = _half_rows(c if sent_by_me else 1 - c, o_refs[a].shape[1] // 2)
            for k, chip in enumerate(chips):
                landed = o_refs[a].at[_chip_index(chip), rows, :]
                copies.append(pltpu.make_async_remote_copy(
                    src_ref=landed, dst_ref=landed, send_sem=send_sems.at[6 * a + 3 + k],
                    recv_sem=recv_sems.at[6 * a + 3 + k], device_id=(x, y, 1 - c), device_id_type=MESH))
        return copies

    def start(ins, outs, sems):
        for cp in over_ici(outs, *sems):
            cp.start()

    def hand_on(ins, outs, sems):
        for arrived, onward in zip(over_ici(outs, *sems), to_sibling(outs, *sems, True)):
            arrived.wait_recv()
            onward.start()

    def finish(ins, outs, sems):
        for cp in to_sibling(outs, *sems, False):
            cp.wait_recv()
        for cp in over_ici(outs, *sems) + to_sibling(outs, *sems, True):
            cp.wait_send()

    return _Carry(
        ins=list(blocked), outs=[jax.ShapeDtypeStruct(w.shape, w.dtype) for w in blocked],
        aliases={a: a for a in range(n)},
        sems=[pltpu.SemaphoreType.DMA((6 * n,)), pltpu.SemaphoreType.DMA((6 * n,))],
        phases=[("first", start), ("late", hand_on), ("last", finish)])


def _pair_exchange(name, grads):
    n = len(grads)

    def body(*refs):
        g_refs, land_refs = refs[:n], refs[n:2 * n]
        send_sems, recv_sems = refs[2 * n:]
        x, y, c, _ = _place()
        copies = []
        for a in range(n):
            half = g_refs[a].shape[1] // 2
            cp = pltpu.make_async_remote_copy(
                src_ref=g_refs[a].at[:, _half_rows(1 - c, half), :], dst_ref=land_refs[a],
                send_sem=send_sems.at[a], recv_sem=recv_sems.at[a], device_id=(x, y, 1 - c), device_id_type=MESH)
            cp.start()
            copies.append(cp)
        for cp in copies:
            cp.wait()

    return pl.pallas_call(
        body, name=name, in_specs=[ANY] * n, out_specs=[ANY] * n,
        out_shape=[jax.ShapeDtypeStruct((g.shape[0], g.shape[1] // 2, g.shape[2]), g.dtype) for g in grads],
        scratch_shapes=[pltpu.SemaphoreType.DMA((n,)), pltpu.SemaphoreType.DMA((n,))],
    )(*grads)


def _to_owner_carry(parts):
    n = len(parts)

    def sends(p_refs, l_refs, send_sems, recv_sems):
        x, y, c, chips = _place()
        me = _chip_index((x, y))
        return [pltpu.make_async_remote_copy(
            src_ref=p_refs[a].at[_chip_index(chip)], dst_ref=l_refs[a].at[me],
            send_sem=send_sems.at[3 * a + k], recv_sem=recv_sems.at[3 * a + k],
            device_id=(chip[0], chip[1], c), device_id_type=MESH) for a in range(n) for k, chip in enumerate(chips)]

    def start(ins, outs, sems):
        for cp in sends(ins, outs, *sems):
            cp.start()

    def finish(ins, outs, sems):
        x, y, c, chips = _place()
        send_sems, recv_sems = sems
        for a in range(n):
            for k, chip in enumerate(chips):
                slot = outs[a].at[_chip_index(chip)]
                pltpu.make_async_remote_copy(
                    src_ref=slot, dst_ref=slot, send_sem=send_sems.at[3 * a + k], recv_sem=recv_sems.at[3 * a + k],
                    device_id=(chip[0], chip[1], c), device_id_type=MESH).wait_recv()
        for cp in sends(ins, outs, *sems):
            cp.wait_send()

    return _Carry(
        ins=list(parts), outs=[jax.ShapeDtypeStruct(p.shape, p.dtype) for p in parts], aliases={},
        sems=[pltpu.SemaphoreType.DMA((3 * n,)), pltpu.SemaphoreType.DMA((3 * n,))],
        phases=[("first", start), ("last", finish)])


def _swap_halves(halves):
    n = len(halves)

    def body(*refs):
        h_refs, o_refs = refs[:n], refs[n:2 * n]
        send_sems, recv_sems = refs[2 * n:]
        x, y, c, _ = _place()
        copies = []
        for a in range(n):
            cp = pltpu.make_async_remote_copy(
                src_ref=h_refs[a], dst_ref=o_refs[a], send_sem=send_sems.at[a], recv_sem=recv_sems.at[a],
                device_id=(x, y, 1 - c), device_id_type=MESH)
            cp.start()
            copies.append(cp)
        for cp in copies:
            cp.wait()

    return pl.pallas_call(
        body, name="grad_swap_halves", in_specs=[ANY] * n, out_specs=[ANY] * n,
        out_shape=[jax.ShapeDtypeStruct(h.shape, h.dtype) for h in halves],
        scratch_shapes=[pltpu.SemaphoreType.DMA((n,)), pltpu.SemaphoreType.DMA((n,))],
    )(*halves)


def _all_devices(name, block):
    r, cols = block.shape

    def body(b_ref, all_ref, sum_ref, send_sems, recv_sems):
        x, y, c, _ = _place()
        me = 4 * x + 2 * y + c
        all_ref[me] = b_ref[...]
        flips = [(fx, fy, fc) for fx in (0, 1) for fy in (0, 1) for fc in (0, 1)][1:]
        copies = []
        for k, (fx, fy, fc) in enumerate(flips):
            cp = pltpu.make_async_remote_copy(
                src_ref=b_ref, dst_ref=all_ref.at[me], send_sem=send_sems.at[k], recv_sem=recv_sems.at[k],
                device_id=(x ^ fx, y ^ fy, c ^ fc), device_id_type=MESH)
            cp.start()
            copies.append(cp)
        for k, (fx, fy, fc) in enumerate(flips):
            slot = all_ref.at[4 * (x ^ fx) + 2 * (y ^ fy) + (c ^ fc)]
            pltpu.make_async_remote_copy(
                src_ref=slot, dst_ref=slot, send_sem=send_sems.at[k], recv_sem=recv_sems.at[k],
                device_id=(x ^ fx, y ^ fy, c ^ fc), device_id_type=MESH).wait_recv()
        for cp in copies:
            cp.wait_send()
        acc = all_ref[0]
        for d in range(1, N_DEV):
            acc = acc + all_ref[d]
        sum_ref[...] = acc

    vmem = pl.BlockSpec(memory_space=pltpu.VMEM)
    return pl.pallas_call(
        body, name=name, in_specs=[vmem], out_specs=[vmem, vmem],
        out_shape=[jax.ShapeDtypeStruct((N_DEV, r, cols), F32), jax.ShapeDtypeStruct((r, cols), F32)],
        scratch_shapes=[pltpu.SemaphoreType.DMA((N_DEV - 1,)), pltpu.SemaphoreType.DMA((N_DEV - 1,))],
    )(block)


PACK = 1024


def _packed_rows(shape):
    size, last = int(np.prod(shape)), shape[-1]
    cols = last if last <= PACK else PACK
    assert size % cols == 0
    return size // cols, cols


def _pack(vals):
    rows = []
    for v in vals:
        n_rows, cols = _packed_rows(v.shape)
        v = v.reshape(n_rows, cols).astype(F32)
        rows.append(jnp.pad(v, ((0, (-n_rows) % 8), (0, PACK - cols))))
    return jnp.concatenate(rows, axis=0)


def _unpack(buf, shapes):
    out, r = [], 0
    for shape in shapes:
        n_rows, cols = _packed_rows(shape)
        out.append(buf[r:r + n_rows, :cols].reshape(shape))
        r += n_rows + (-n_rows) % 8
    return out


def _ffn_hidden(name, n, wg, wu, tm=1024, carry=None):
    m, k = n.shape
    nj, _, fb = wg.shape
    tm = _row_tile(m, tm)

    def core(n_ref, wg_ref, wu_ref, a_ref, b_ref, s_ref):
        nv = n_ref[...]
        a = _dot(nv, wg_ref[...], False)
        b = _dot(nv, wu_ref[...], False)
        a_ref[...] = a.astype(a_ref.dtype)
        b_ref[...] = b.astype(b_ref.dtype)
        s_ref[...] = (a * jax.nn.sigmoid(a) * b).astype(s_ref.dtype)

    w_spec = pl.BlockSpec((None, k, fb), lambda j, i: (j, 0, 0))
    out_spec = pl.BlockSpec((None, tm, fb), lambda j, i: (j, i, 0))
    return _call(name, core, (nj, m // tm), [pl.BlockSpec((tm, k), lambda j, i: (i, 0)), w_spec, w_spec],
                 [out_spec] * 3, [jax.ShapeDtypeStruct((nj, m, fb), BF16)] * 3, [], [n, wg, wu], carry)


def _ffn_d_hidden(name, df, wd, a, b, tm=1024):
    m, k = df.shape
    nj, fb, _ = wd.shape
    tm = _row_tile(m, tm)

    def body(df_ref, wd_ref, a_ref, b_ref, da_ref, db_ref):
        ds = _dot(df_ref[...], wd_ref[...], True)
        av, bv = a_ref[...].astype(F32), b_ref[...].astype(F32)
        sig = jax.nn.sigmoid(av)
        da_ref[...] = (ds * bv * sig * (1.0 + av * (1.0 - sig))).astype(da_ref.dtype)
        db_ref[...] = (ds * av * sig).astype(db_ref.dtype)

    blk = pl.BlockSpec((None, tm, fb), lambda j, i: (j, i, 0))
    return pl.pallas_call(
        body, name=name, grid=(nj, m // tm),
        in_specs=[pl.BlockSpec((tm, k), lambda j, i: (i, 0)), pl.BlockSpec((None, fb, k), lambda j, i: (j, 0, 0)),
                  blk, blk],
        out_specs=[blk, blk], out_shape=[jax.ShapeDtypeStruct((nj, m, fb), BF16)] * 2,
        compiler_params=_cparams(2),
    )(df, wd, a, b)


def kernel(x, ffn1_norm_pre, ffn1_w_gate, ffn1_w_up, ffn1_w_down, ffn1_norm_post, mix_norm_pre, w_in, gate_bias, rel_table, w_attn_out, conv_glu_bias, conv_dw_w, conv_dw_b, conv_ln_g, conv_ln_b, conv_w_out, w_out, mix_norm_post, ffn2_norm_pre, ffn2_w_gate, ffn2_w_up, ffn2_w_down, ffn2_norm_post, loss_target, m_ffn1_norm_pre, m_ffn1_w_gate, m_ffn1_w_up, m_ffn1_w_down, m_ffn1_norm_post, m_mix_norm_pre, m_w_in, m_gate_bias, m_rel_table, m_w_attn_out, m_conv_glu_bias, m_conv_dw_w, m_conv_dw_b, m_conv_ln_g, m_conv_ln_b, m_conv_w_out, m_w_out, m_mix_norm_post, m_ffn2_norm_pre, m_ffn2_w_gate, m_ffn2_w_up, m_ffn2_w_down, m_ffn2_norm_post, v_ffn1_norm_pre, v_ffn1_w_gate, v_ffn1_w_up, v_ffn1_w_down, v_ffn1_norm_post, v_mix_norm_pre, v_w_in, v_gate_bias, v_rel_table, v_w_attn_out, v_conv_glu_bias, v_conv_dw_w, v_conv_dw_b, v_conv_ln_g, v_conv_ln_b, v_conv_w_out, v_w_out, v_mix_norm_post, v_ffn2_norm_pre, v_ffn2_w_gate, v_ffn2_w_up, v_ffn2_w_down, v_ffn2_norm_post):
    args = dict(locals())
    names = ['ffn1_norm_pre', 'ffn1_w_gate', 'ffn1_w_up', 'ffn1_w_down', 'ffn1_norm_post', 'mix_norm_pre', 'w_in',
             'gate_bias', 'rel_table', 'w_attn_out', 'conv_glu_bias', 'conv_dw_w', 'conv_dw_b', 'conv_ln_g',
             'conv_ln_b', 'conv_w_out', 'w_out', 'mix_norm_post', 'ffn2_norm_pre', 'ffn2_w_gate', 'ffn2_w_up',
             'ffn2_w_down', 'ffn2_norm_post']
    big = ['ffn1_w_gate', 'ffn1_w_up', 'ffn1_w_down', 'w_in', 'w_attn_out', 'conv_w_out', 'w_out', 'ffn2_w_gate',
           'ffn2_w_up', 'ffn2_w_down']
    small = [n for n in names if n not in big]

    xs, target = x[0], loss_target[0]
    t, d = xs.shape
    cx, cy = lax.axis_index("x"), lax.axis_index("y")
    chip = 2 * cx + cy

    dw_shard = conv_dw_w[0, :, 0, :]
    cshard = dw_shard.shape[1]
    dw_all, _ = _all_devices("gather_dw", _pack([dw_shard]))
    dw_full = jnp.concatenate([dw_all[2 * j, :CONV_WIDTH, :cshard] for j in range(N_CHIPS)], axis=1)
    dw_full = jnp.pad(dw_full, ((0, CONV_HALO - CONV_WIDTH), (0, 0)))
    peers = [(1 - cx, cy), (cx, 1 - cy), (1 - cx, 1 - cy)]
    pos = jnp.stack([lax.axis_index("c"), chip] + [_chip_index(p) for p in peers]).astype(jnp.int32)
    own = {n: _cast_into("cast_" + n, pos, args[n][0]) for n in big}
    gather = lambda *ns: _gather_carry([own[n] for n in ns])
    res_spec = [(d, F32), (d, F32), (d, BF16)]

    wg1, wu1, wd1 = _run_carry("gather_ffn1", gather("ffn1_w_gate", "ffn1_w_up", "ffn1_w_down"))
    n1 = _rms_fwd("ffn1_pre", xs, ffn1_norm_pre)
    (a1, b1, s1), (win, wao, wco, wout) = _ffn_hidden(
        "ffn1_hidden", n1, wg1, wu1, carry=gather("w_in", "w_attn_out", "conv_w_out", "w_out"))
    (f1, h1, u), (wg2,) = _mm_kblk(
        "ffn1_down", [(s1, wd1)], trans_w=False, epilogue=_ep_post_res_pre(0.5), rows=[xs],
        vecs=[ffn1_norm_post, mix_norm_pre], row_outs=res_spec, carry=gather("ffn2_w_gate"))
    proj, (wu2, wd2) = _mm_nblk("mix_in", u, win, trans_w=False, out_blocked=False, out_dtype=BF16,
                                carry=gather("ffn2_w_up", "ffn2_w_down"))
    table_pad = jnp.pad(rel_table[0], ((0, 0), (0, REL_PAD - rel_table.shape[2])))
    bias = _bias_expand(table_pad)
    kp = jnp.pad(proj[:, D_ATTN:2 * D_ATTN], ((K_PAD, 0), (0, 0)))
    vp = jnp.pad(proj[:, 2 * D_ATTN:3 * D_ATTN], ((K_PAD, 0), (0, 0)))
    att, lse = _attn_fwd(proj, kp, vp, bias)
    cs, c_glu, z_conv = _conv_fwd(proj, conv_glu_bias, dw_full, conv_dw_b, conv_ln_g, conv_ln_b)
    y_a = _mm_nblk("attn_out", att, wao, trans_w=False, out_blocked=False, out_dtype=F32)
    y_b = _mm_nblk("conv_out", cs, wco, trans_w=False, out_blocked=False, out_dtype=F32)
    merged = _merge_fwd("mix_merge", y_a, y_b, proj, gate_bias)
    (mo, h2, n2), _ = _mm_kblk(
        "mix_out", [(merged, wout)], trans_w=False, epilogue=_ep_post_res_pre(1.0), rows=[h1],
        vecs=[mix_norm_post, ffn2_norm_pre], row_outs=res_spec)
    (a2, b2, s2), _ = _ffn_hidden("ffn2_hidden", n2, wg2, wu2)
    g = {}
    (dy, df2, err2, g["ffn2_norm_post"]), _ = _mm_kblk(
        "ffn2_down", [(s2, wd2)], trans_w=False, epilogue=_ep_loss(0.5, d), rows=[h2, target],
        vecs=[ffn2_norm_post], row_outs=[(d, F32), (d, BF16)], vec_outs=[d, d])
    loss = lax.psum(0.5 * jnp.sum(err2) / d, ("x", "y", "c"))

    parts, landed = {}, {}

    def ffn_bwd(tag, df, n, a, b, s, wg, wu, wd, **epilogue):
        da, db = _ffn_d_hidden(tag + "_d_hidden", df, wd, a, b)
        group = [tag + "_w_down", tag + "_w_gate", tag + "_w_up"]
        local = [_mm_tn(tag + "_g_down", s, "blk", df, "full"), _mm_tn(tag + "_g_gate", n, "full", da, "blk"),
                 _mm_tn(tag + "_g_up", n, "full", db, "blk")]
        return _mm_kblk(tag + "_d_n", [(da, wg), (db, wu)], trans_w=True, carry=pair_sums(tag, group, local),
                        **epilogue), group

    def pair_sums(tag, group, local):
        theirs = _pair_exchange("pair_" + tag, local)
        for n, mine, other in zip(group, local, theirs):
            parts[n] = _add_pair("pair_sum_" + n, pos, mine, other)
        return _to_owner_carry([parts[n] for n in group])

    def keep(group, carried):
        for n, val in zip(group, carried):
            landed[n] = val

    ((dh2, dmo, g["ffn2_norm_pre"], g["mix_norm_post"]), carried), group = ffn_bwd(
        "ffn2", df2, n2, a2, b2, s2, wg2, wu2, wd2, epilogue=_ep_pre_bwd_post(1.0), rows=[h2, dy, mo],
        vecs=[ffn2_norm_pre, mix_norm_post], row_outs=[(d, F32), (d, BF16)], vec_outs=[d, d])
    keep(group, carried)
    dmerged = _mm_nblk("mix_d_merged", dmo, wout, trans_w=True, out_blocked=False, out_dtype=F32)
    g_wout = _mm_tn("mix_g_out", merged, "col", dmo, "full")
    dy_a, dy_b, dgates, g["gate_bias"] = _merge_bwd("mix_d_merge", dmerged, y_a, y_b, proj, gate_bias)
    (datt,), _ = _mm_kblk("attn_d_out", [(dy_a, wao)], trans_w=True, out_dtype=BF16)
    (dcs,), _ = _mm_kblk("conv_d_out", [(dy_b, wco)], trans_w=True, out_dtype=F32)
    g_wao = _mm_tn("attn_g_out", att, "full", dy_a, "col")
    g_wco = _mm_tn("conv_g_out", cs, "full", dy_b, "col")
    dq, dkp, dvp, dbias = _attn_bwd(proj, kp, vp, bias, att, lse, datt)
    g["rel_table"] = _bias_fold(dbias)[:, :rel_table.shape[2]]
    dcin, g_dw, g["conv_dw_b"], g["conv_ln_g"], g["conv_ln_b"], g["conv_glu_bias"] = _conv_bwd(
        proj, c_glu, z_conv, dcs, conv_glu_bias, dw_full, conv_ln_g, conv_ln_b)
    dproj = jnp.concatenate([dq, dkp[K_PAD:].astype(BF16), dvp[K_PAD:].astype(BF16), dcin, dgates], axis=1)
    g_win = _mm_tn("mix_g_in", u, "full", dproj, "col")
    group = ["w_out", "w_attn_out", "conv_w_out", "w_in"]
    (dh1, df1, g["mix_norm_pre"], g["ffn1_norm_post"]), carried = _mm_kblk(
        "mix_d_in", [(dproj, win)], trans_w=True, epilogue=_ep_pre_bwd_post(0.5), rows=[h1, dh2, f1],
        vecs=[mix_norm_pre, ffn1_norm_post], row_outs=[(d, F32), (d, BF16)], vec_outs=[d, d],
        carry=pair_sums("mix", group, [g_wout, g_wao, g_wco, g_win]))
    keep(group, carried)
    ((grad_x, g["ffn1_norm_pre"]), carried), group = ffn_bwd(
        "ffn1", df1, n1, a1, b1, s1, wg1, wu1, wd1, epilogue=_ep_pre_bwd_first(), rows=[xs, dh1],
        vecs=[ffn1_norm_pre], row_outs=[(d, F32)], vec_outs=[d])
    keep(group, carried)

    halves = [_add_chips("chip_sum_" + n, pos, parts[n], landed[n]) for n in big]
    other_halves = _swap_halves(halves)

    g["conv_dw_w"] = g_dw[:CONV_WIDTH]
    _, small_sum = _all_devices("sum_small", _pack([g[n] for n in small]))
    for n, val in zip(small, _unpack(small_sum, [g[n].shape for n in small])):
        g[n] = val
    g["conv_dw_w"] = lax.dynamic_slice_in_dim(g["conv_dw_w"], chip * cshard, cshard, axis=1)

    grads, deltas, new_m, new_v = {}, {}, {}, {}
    for n, own, other in zip(big, halves, other_halves):
        gr, dl, m2, v2 = _adamw_halves("adamw_" + n, pos, args[n][0], args["m_" + n][0], args["v_" + n][0], own, other)
        grads[n], deltas[n], new_m[n], new_v[n] = gr[None], dl[None], m2[None], v2[None]
    shapes = [g[n].shape for n in small]
    packed = lambda pre: _pack([args[pre + n].reshape(shp) for n, shp in zip(small, shapes)])
    dl, m2, v2 = _adamw("adamw_small", packed(""), _pack([g[n] for n in small]), packed("m_"), packed("v_"))
    for n, a_, b_, c_ in zip(small, _unpack(dl, shapes), _unpack(m2, shapes), _unpack(v2, shapes)):
        shape = args[n].shape
        grads[n], deltas[n], new_m[n], new_v[n] = (g[n].reshape(shape), a_.reshape(shape), b_.reshape(shape),
                                                   c_.reshape(shape))

    return (loss, grad_x[None], *[grads[n] for n in names], *[deltas[n] for n in names],
            *[new_m[n] for n in names], *[new_v[n] for n in names])
```

```python
import functools

import numpy as np
import jax
import jax.numpy as jnp
from jax import lax
from jax.experimental import pallas as pl
from jax.experimental.pallas import tpu as pltpu

F32 = jnp.float32
BF16 = jnp.bfloat16
MESH = pl.DeviceIdType.MESH
ANY = pl.BlockSpec(memory_space=pl.ANY)

EPS = 1e-6
CHUNK = 64
LEFT_CHUNKS = 8
N_HEADS = 8
HEAD_DIM = 64
D_ATTN = N_HEADS * HEAD_DIM
D_CONV = 512
CONV_WIDTH = 31
REL_CLIP = 128
N_CHIPS = 4
N_DEV = 8
Q_BLOCK = 4 * CHUNK
K_PAD = LEFT_CHUNKS * CHUNK
K_WIN = K_PAD + Q_BLOCK
REL_EXT = 1024
REL_PAD = 384
CONV_HALO = 32
CONV_TILE = 256
COL = 512
NEG = -1e30

ADAM_LR = 0.001
ADAM_B1 = 0.9
ADAM_B2 = 0.999
ADAM_EPS = 1e-08
ADAM_WD = 0.01
ADAM_STEP = 10

VMEM_LIMIT_BYTES = 56 * 1024 * 1024


def _cparams(n_grid):
    return pltpu.CompilerParams(dimension_semantics=("arbitrary",) * n_grid, vmem_limit_bytes=VMEM_LIMIT_BYTES)


def _row_tile(rows, want):
    if rows <= want:
        return rows
    for t in range(want - want % 16, 0, -16):
        if rows % t == 0:
            return t
    raise ValueError((rows, want))


def _dot(a, w, trans_w):
    dims = (((1,), (1,)), ((), ())) if trans_w else (((1,), (0,)), ((), ()))
    return lax.dot_general(a, w, dims, preferred_element_type=F32)


class _Carry:
    LATE_STEPS = 2

    def __init__(self, ins, outs, aliases, sems, phases):
        self.ins, self.outs, self.aliases, self.sems, self.phases = ins, outs, aliases, sems, phases


def _call(name, core, grid, in_specs, out_specs, out_shape, scratch, args, carry=None):
    n_in, n_out, n_scr = len(in_specs), len(out_specs), len(scratch)
    if carry is None:
        out = pl.pallas_call(core, name=name, grid=grid, in_specs=in_specs, out_specs=out_specs, out_shape=out_shape,
                             scratch_shapes=scratch, compiler_params=_cparams(len(grid)))(*args)
        return list(out), []
    c_in, c_out = len(carry.ins), len(carry.outs)
    total = int(np.prod(grid))
    late = max(total - 1 - _Carry.LATE_STEPS, 0)

    def body(*refs):
        ins, refs = refs[:n_in], refs[n_in:]
        c_ins, refs = refs[:c_in], refs[c_in:]
        outs, refs = refs[:n_out], refs[n_out:]
        c_outs, refs = refs[:c_out], refs[c_out:]
        scr, c_sems = refs[:n_scr], refs[n_scr:]
        step = pl.program_id(0)
        for axis in range(1, len(grid)):
            step = step * grid[axis] + pl.program_id(axis)

        def run(when, at):
            for w, fn in carry.phases:
                if w == when:
                    pl.when(step == at)(functools.partial(fn, c_ins, c_outs, c_sems))

        run("first", 0)
        core(*ins, *outs, *scr)
        run("late", late)
        run("last", total - 1)

    out = pl.pallas_call(
        body, name=name, grid=grid, in_specs=list(in_specs) + [ANY] * c_in, out_specs=list(out_specs) + [ANY] * c_out,
        out_shape=list(out_shape) + list(carry.outs), scratch_shapes=list(scratch) + list(carry.sems),
        input_output_aliases={n_in + a: n_out + b for a, b in carry.aliases.items()},
        compiler_params=_cparams(len(grid)),
    )(*args, *carry.ins)
    return list(out[:n_out]), list(out[n_out:])


def _mm_nblk(name, a, w, *, trans_w, out_blocked, out_dtype, tm=1024, carry=None):
    m, k = a.shape
    nj = w.shape[0]
    nb = w.shape[1] if trans_w else w.shape[2]
    tm = _row_tile(m, tm)

    def core(a_ref, w_ref, o_ref):
        o_ref[...] = _dot(a_ref[...], w_ref[...], trans_w).astype(o_ref.dtype)

    if out_blocked:
        out_shape, out_spec = (nj, m, nb), pl.BlockSpec((None, tm, nb), lambda j, i: (j, i, 0))
    else:
        out_shape, out_spec = (m, nj * nb), pl.BlockSpec((tm, nb), lambda j, i: (i, j))
    out, carried = _call(
        name, core, (nj, m // tm),
        [pl.BlockSpec((tm, k), lambda j, i: (i, 0)), pl.BlockSpec((None,) + w.shape[1:], lambda j, i: (j, 0, 0))],
        [out_spec], [jax.ShapeDtypeStruct(out_shape, out_dtype)], [], [a, w], carry)
    return out[0] if carry is None else (out[0], carried)


def _mm_kblk(name, pairs, *, trans_w, out_dtype=F32, tm=512, sub=256, epilogue=None, rows=(), vecs=(), row_outs=None,
             vec_outs=(), carry=None):
    w0 = pairs[0][1]
    nj = w0.shape[0]
    n = w0.shape[1] if trans_w else w0.shape[2]
    kb = w0.shape[2] if trans_w else w0.shape[1]
    m = pairs[0][0].shape[-2]
    tm = _row_tile(m, tm)
    ts = _row_tile(tm, sub)
    n_pairs, n_rows, n_vecs = len(pairs), len(rows), len(vecs)
    if epilogue is None:
        epilogue, row_outs = (lambda acc, r, v: ([acc], [])), [(n, out_dtype)]
    n_ro, n_vo = len(row_outs), len(vec_outs)

    def core(*refs):
        pair_refs, refs = refs[:2 * n_pairs], refs[2 * n_pairs:]
        row_refs, refs = refs[:n_rows], refs[n_rows:]
        vec_refs, refs = refs[:n_vecs], refs[n_vecs:]
        ro_refs, vo_refs = refs[:n_ro], refs[n_ro:]
        if n_vo:
            @pl.when(pl.program_id(0) == 0)
            def _():
                for ref in vo_refs:
                    ref[...] = jnp.zeros_like(ref)

        vec_vals = [v[...] for v in vec_refs]
        sums = None
        for r0 in range(0, tm, ts):
            sub_rows = slice(r0, r0 + ts)
            acc = None
            for p in range(n_pairs):
                a_ref, w_ref = pair_refs[2 * p], pair_refs[2 * p + 1]
                for j in range(nj):
                    a_blk = a_ref[j, sub_rows, :] if len(a_ref.shape) == 3 else a_ref[sub_rows, j * kb:(j + 1) * kb]
                    part = _dot(a_blk, w_ref[j], trans_w)
                    acc = part if acc is None else acc + part
            ro, vo = epilogue(acc, [r[sub_rows, :] for r in row_refs], vec_vals)
            for ref, val in zip(ro_refs, ro):
                ref[sub_rows, :] = val.astype(ref.dtype)
            sums = vo if sums is None else [s + v for s, v in zip(sums, vo)]
        for ref, val in zip(vo_refs, sums or []):
            ref[...] += val

    in_specs, args = [], []
    for a, w in pairs:
        if a.ndim == 3:
            in_specs.append(pl.BlockSpec((nj, tm, kb), lambda i: (0, i, 0)))
        else:
            in_specs.append(pl.BlockSpec((tm, nj * kb), lambda i: (i, 0)))
        in_specs.append(pl.BlockSpec(w.shape, lambda i: (0, 0, 0), pipeline_mode=pl.Buffered(1)))
        args += [a, w]
    in_specs += [pl.BlockSpec((tm, r.shape[1]), lambda i: (i, 0)) for r in rows]
    in_specs += [pl.BlockSpec(v.shape, lambda i: (0, 0)) for v in vecs]
    out_specs = [pl.BlockSpec((tm, cols), lambda i: (i, 0)) for cols, _ in row_outs]
    out_specs += [pl.BlockSpec((1, cols), lambda i: (0, 0)) for cols in vec_outs]
    out_shape = [jax.ShapeDtypeStruct((m, cols), dt) for cols, dt in row_outs]
    out_shape += [jax.ShapeDtypeStruct((1, cols), F32) for cols in vec_outs]
    return _call(name, core, (m // tm,), in_specs, out_specs, out_shape, [], args + list(rows) + list(vecs), carry)


def _mm_tn(name, a, a_mode, b, b_mode, *, out_dtype=BF16, tt=1024):
    nj = N_CHIPS
    t = a.shape[-2]
    tt = _row_tile(t, tt)

    def spec(x, mode):
        if mode == "full":
            return x.shape[1], pl.BlockSpec((tt, x.shape[1]), lambda j, s: (s, 0))
        if mode == "col":
            cb = x.shape[1] // nj
            return cb, pl.BlockSpec((tt, cb), lambda j, s: (s, j))
        return x.shape[2], pl.BlockSpec((None, tt, x.shape[2]), lambda j, s: (j, s, 0))

    ca, a_spec = spec(a, a_mode)
    cb, b_spec = spec(b, b_mode)
    n_steps = t // tt

    def body(a_ref, b_ref, o_ref, acc_ref):
        s = pl.program_id(1)

        @pl.when(s == 0)
        def _():
            acc_ref[...] = jnp.zeros_like(acc_ref)

        acc_ref[...] += lax.dot_general(a_ref[...], b_ref[...], (((0,), (0,)), ((), ())),
                                        preferred_element_type=F32)

        @pl.when(s == n_steps - 1)
        def _():
            o_ref[...] = acc_ref[...].astype(o_ref.dtype)

    return pl.pallas_call(
        body, name=name, grid=(nj, n_steps), in_specs=[a_spec, b_spec],
        out_specs=pl.BlockSpec((None, ca, cb), lambda j, s: (j, 0, 0)),
        out_shape=jax.ShapeDtypeStruct((nj, ca, cb), out_dtype),
        scratch_shapes=[pltpu.VMEM((ca, cb), F32)], compiler_params=_cparams(2),
    )(a, b)


def _rowwise(name, fn, rows, vecs, row_outs, vec_outs, *, tm=256):
    nrows = rows[0][0].shape[0]
    tm = _row_tile(nrows, tm)
    n_r, n_v, n_ro, n_vo = len(rows), len(vecs), len(row_outs), len(vec_outs)

    def body(*refs):
        r_vals = [r[...] for r in refs[:n_r]]
        v_vals = [r[...] for r in refs[n_r:n_r + n_v]]
        ro_refs = refs[n_r + n_v:n_r + n_v + n_ro]
        vo_refs = refs[n_r + n_v + n_ro:]
        ro, vo = fn(r_vals, v_vals)
        for ref, val in zip(ro_refs, ro):
            ref[...] = val.astype(ref.dtype)
        if n_vo:
            @pl.when(pl.program_id(0) == 0)
            def _():
                for ref in vo_refs:
                    ref[...] = jnp.zeros_like(ref)

            for ref, val in zip(vo_refs, vo):
                ref[...] += val

    in_specs = [pl.BlockSpec((tm, cols), functools.partial(lambda i, cb: (i, cb), cb=cb)) for _, cols, cb in rows]
    in_specs += [pl.BlockSpec(v.shape, functools.partial(lambda i, nd: (0,) * nd, nd=v.ndim)) for v in vecs]
    out_specs = [pl.BlockSpec((tm, cols), lambda i: (i, 0)) for cols, _ in row_outs]
    out_specs += [pl.BlockSpec((1, cols), lambda i: (0, 0)) for cols in vec_outs]
    out_shape = [jax.ShapeDtypeStruct((nrows, cols), dt) for cols, dt in row_outs]
    out_shape += [jax.ShapeDtypeStruct((1, cols), F32) for cols in vec_outs]
    return pl.pallas_call(
        body, name=name, grid=(nrows // tm,), in_specs=in_specs, out_specs=out_specs, out_shape=out_shape,
        compiler_params=_cparams(1),
    )(*[r[0] for r in rows], *vecs)


def _whole(x):
    return (x, x.shape[1], 0)


def _colsum(x):
    return jnp.sum(x, axis=0, keepdims=True)


def _rstd(x):
    return lax.rsqrt(jnp.mean(x * x, axis=-1, keepdims=True) + EPS)


def _rms_bwd(dn, x, g):
    r = _rstd(x)
    c = dn * g
    dx = r * c - x * (r * r * r) * jnp.mean(c * x, axis=-1, keepdims=True)
    return dx, _colsum(dn * x * r)


def _rms_fwd(name, x, g):
    def fn(r, v):
        (xv,), (gv,) = r, v
        return [xv * _rstd(xv) * gv], []

    return _rowwise(name, fn, [_whole(x)], [g], [(x.shape[1], BF16)], [])[0]


def _ep_post_res_pre(scale):
    def epilogue(acc, rows, vecs):
        (resid,), (g_post, g_next) = rows, vecs
        h = resid + scale * (acc * _rstd(acc) * g_post)
        return [acc, h, h * _rstd(h) * g_next], []

    return epilogue


def _post_bwd(dh, f, g_post, scale):
    return _rms_bwd(scale * dh, f, g_post)


def _ep_loss(scale, d):
    def epilogue(acc, rows, vecs):
        (resid, target), (g_post,) = rows, vecs
        err = resid + scale * (acc * _rstd(acc) * g_post) - target
        dy = err * (1.0 / d)
        df, dg_post = _post_bwd(dy, acc, g_post, scale)
        return [dy, df], [_colsum(err * err), dg_post]

    return epilogue


def _ep_pre_bwd_post(scale_prev):
    def epilogue(acc, rows, vecs):
        (h, dh_up, f_prev), (g_pre, g_post_prev) = rows, vecs
        dx, dg_pre = _rms_bwd(acc, h, g_pre)
        dh = dh_up + dx
        df, dg_post = _post_bwd(dh, f_prev, g_post_prev, scale_prev)
        return [dh, df], [dg_pre, dg_post]

    return epilogue


def _ep_pre_bwd_first():
    def epilogue(acc, rows, vecs):
        (x, dh_up), (g_pre,) = rows, vecs
        dx, dg_pre = _rms_bwd(acc, x, g_pre)
        return [dh_up + dx], [dg_pre]

    return epilogue


def _gate_pieces(proj, d):
    first = (3 * D_ATTN + 2 * D_CONV) // COL
    n = d // COL
    return [(proj, COL, first + p) for p in range(2 * n)], n


def _merge_fwd(name, y_a, y_b, proj, gate_bias):
    d = y_a.shape[1]
    pieces, n = _gate_pieces(proj, d)

    def fn(r, v):
        ya, yb = r[0], r[1]
        g = jnp.concatenate([p.astype(F32) for p in r[2:]], axis=1) + v[0]
        gates = jax.nn.sigmoid(g)
        return [gates[:, :d] * ya + gates[:, d:] * yb], []

    return _rowwise(name, fn, [_whole(y_a), _whole(y_b)] + pieces, [gate_bias], [(d, BF16)], [])[0]


def _merge_bwd(name, dmerged, y_a, y_b, proj, gate_bias):
    d = y_a.shape[1]
    pieces, n = _gate_pieces(proj, d)

    def fn(r, v):
        dm, ya, yb = r[0], r[1], r[2]
        g = jnp.concatenate([p.astype(F32) for p in r[3:]], axis=1) + v[0]
        gates = jax.nn.sigmoid(g)
        ga, gb = gates[:, :d], gates[:, d:]
        dga = dm * ya * ga * (1.0 - ga)
        dgb = dm * yb * gb * (1.0 - gb)
        dgate = jnp.concatenate([dga, dgb], axis=1)
        return [dm * ga, dm * gb, dgate], [_colsum(dgate)]

    return _rowwise(name, fn, [_whole(dmerged), _whole(y_a), _whole(y_b)] + pieces, [gate_bias],
                    [(d, BF16), (d, BF16), (2 * d, BF16)], [2 * d])


def _adamw_math(wv, gv, mv, vv):
    m2 = ADAM_B1 * mv + (1.0 - ADAM_B1) * gv
    v2 = ADAM_B2 * vv + (1.0 - ADAM_B2) * (gv * gv)
    m_hat = m2 / (1.0 - ADAM_B1 ** ADAM_STEP)
    v_hat = v2 / (1.0 - ADAM_B2 ** ADAM_STEP)
    delta = -ADAM_LR * (m_hat / (jnp.sqrt(v_hat) + ADAM_EPS) + ADAM_WD * wv)
    return delta, m2, v2


def _adamw(name, w, g, m, v):
    def fn(r, _):
        return list(_adamw_math(*r)), []

    c = w.shape[1]
    return _rowwise(name, fn, [_whole(w), _whole(g), _whole(m), _whole(v)], [], [(c, F32)] * 3, [], tm=256)


POS_C, POS_CHIP, POS_PEER = 0, 1, 2


def _placed_call(body, name, pos, grid, in_specs, out_specs, out_shape, args):
    return pl.pallas_call(
        body, name=name, out_shape=out_shape, compiler_params=_cparams(len(grid)),
        grid_spec=pltpu.PrefetchScalarGridSpec(num_scalar_prefetch=1, grid=grid, in_specs=in_specs,
                                               out_specs=out_specs),
    )(pos, *args)


def _cast_into(name, pos, w):
    r, cols = w.shape
    tm = _row_tile(r, 256)

    def body(pos_ref, w_ref, o_ref):
        o_ref[...] = w_ref[...].astype(o_ref.dtype)

    return _placed_call(
        body, name, pos, (r // tm,), [pl.BlockSpec((tm, cols), lambda i, pos: (i, 0))],
        pl.BlockSpec((None, tm, cols), lambda i, pos: (pos[POS_CHIP], i, 0)),
        jax.ShapeDtypeStruct((N_CHIPS, r, cols), BF16), [w])


def _add_pair(name, pos, grad, landed):
    nj, half, cols = landed.shape
    tm = _row_tile(half, 256)
    nb = half // tm

    def body(pos_ref, g_ref, l_ref, o_ref):
        o_ref[...] = (g_ref[...].astype(F32) + l_ref[...].astype(F32)).astype(o_ref.dtype)

    spec = pl.BlockSpec((None, tm, cols), lambda j, i, pos: (j, i, 0))
    return _placed_call(
        body, name, pos, (nj, nb),
        [pl.BlockSpec((None, tm, cols), lambda j, i, pos: (j, pos[POS_C] * nb + i, 0)), spec], spec,
        jax.ShapeDtypeStruct(landed.shape, BF16), [grad, landed])


def _add_chips(name, pos, part, landed):
    _, half, cols = landed.shape
    tm = _row_tile(half, 256)

    def body(pos_ref, p_ref, l0_ref, l1_ref, l2_ref, o_ref):
        acc = p_ref[...].astype(F32)
        for ref in (l0_ref, l1_ref, l2_ref):
            acc = acc + ref[...].astype(F32)
        o_ref[...] = acc

    slot = lambda at: pl.BlockSpec((None, tm, cols), functools.partial(lambda i, pos, at: (pos[at], i, 0), at=at))
    return _placed_call(
        body, name, pos, (half // tm,), [slot(POS_CHIP)] + [slot(POS_PEER + k) for k in range(3)],
        pl.BlockSpec((tm, cols), lambda i, pos: (i, 0)), jax.ShapeDtypeStruct((half, cols), F32),
        [part, landed, landed, landed])


def _adamw_halves(name, pos, w, m, v, own, landed):
    r, cols = w.shape
    half = own.shape[0]
    tm = _row_tile(half, 256)
    nb = half // tm

    def body(pos_ref, w_ref, m_ref, v_ref, own_ref, land_ref, g_out, d_out, m_out, v_out):
        mine = pl.program_id(0) == pos_ref[POS_C]
        g = jnp.where(mine, own_ref[...], land_ref[...])
        delta, m2, v2 = _adamw_math(w_ref[...], g, m_ref[...], v_ref[...])
        g_out[...] = g
        d_out[...] = delta
        m_out[...] = m2
        v_out[...] = v2

    full = pl.BlockSpec((tm, cols), lambda h, i, pos: (h * nb + i, 0))
    part = pl.BlockSpec((tm, cols), lambda h, i, pos: (i, 0))
    return _placed_call(
        body, name, pos, (2, nb), [full, full, full, part, part], [full] * 4,
        [jax.ShapeDtypeStruct((r, cols), F32)] * 4, [w, m, v, own, landed])


def _rel_onehot():
    e = np.arange(REL_EXT)
    dist = K_PAD - (e - (Q_BLOCK - 1))
    idx = np.clip(dist, -REL_CLIP, REL_CLIP) + REL_CLIP
    return (np.arange(REL_PAD)[:, None] == idx[None, :]).astype(np.float32)


def _band_valid():
    qc = lax.broadcasted_iota(jnp.int32, (Q_BLOCK, K_WIN), 0) // CHUNK
    kc = lax.broadcasted_iota(jnp.int32, (Q_BLOCK, K_WIN), 1) // CHUNK
    return (kc >= qc) & (kc <= qc + LEFT_CHUNKS)


def _skew(x, left):
    row = lax.broadcasted_iota(jnp.int32, x.shape, 0)
    for bit in range(Q_BLOCK.bit_length() - 1):
        amount = 1 << bit
        rolled = pltpu.roll(x, REL_EXT - amount if left else amount, 1)
        x = jnp.where((row >> bit) & 1 == 1, rolled, x)
    return x


def _bias_expand(table_pad):
    onehot = jnp.asarray(_rel_onehot())

    def body(t_ref, oh_ref, o_ref):
        ext = jnp.dot(t_ref[...], oh_ref[...], precision=lax.Precision.HIGHEST, preferred_element_type=F32)
        valid = _band_valid()
        for h in range(N_HEADS):
            rows = jnp.broadcast_to(ext[h:h + 1, :], (Q_BLOCK, REL_EXT))
            rolled = _skew(pltpu.roll(rows, REL_EXT - (Q_BLOCK - 1), 1), left=False)
            o_ref[h] = jnp.where(valid, rolled[:, :K_WIN], NEG)

    return pl.pallas_call(
        body, name="bias_expand", out_shape=jax.ShapeDtypeStruct((N_HEADS, Q_BLOCK, K_WIN), F32),
        compiler_params=pltpu.CompilerParams(vmem_limit_bytes=VMEM_LIMIT_BYTES),
    )(table_pad, onehot)


def _bias_fold(dbias):
    onehot_t = jnp.asarray(_rel_onehot().T)

    def body(d_ref, oh_ref, o_ref, ext_ref):
        for h in range(N_HEADS):
            x = jnp.concatenate([d_ref[h], jnp.zeros((Q_BLOCK, REL_EXT - K_WIN), F32)], axis=1)
            rolled = _skew(pltpu.roll(x, Q_BLOCK - 1, 1), left=True)
            ext_ref[h:h + 1, :] = jnp.sum(rolled, axis=0, keepdims=True)
        o_ref[...] = jnp.dot(ext_ref[...], oh_ref[...], precision=lax.Precision.HIGHEST,
                             preferred_element_type=F32)

    return pl.pallas_call(
        body, name="bias_fold", out_shape=jax.ShapeDtypeStruct((N_HEADS, REL_PAD), F32),
        scratch_shapes=[pltpu.VMEM((N_HEADS, REL_EXT), F32)],
        compiler_params=pltpu.CompilerParams(vmem_limit_bytes=VMEM_LIMIT_BYTES),
    )(dbias, onehot_t)


def _scores(q, k, bias, i):
    s = lax.dot_general(q, k, (((1,), (1,)), ((), ())), preferred_element_type=F32) * (HEAD_DIM ** -0.5) + bias
    kpos = lax.broadcasted_iota(jnp.int32, (Q_BLOCK, K_WIN), 1) + i * Q_BLOCK
    return jnp.where(kpos >= K_PAD, s, NEG)


def _attn_specs(n_kv):
    q_spec = pl.BlockSpec((Q_BLOCK, 2 * HEAD_DIM), lambda p, i: (i, p))
    kv_specs = [pl.BlockSpec((Q_BLOCK, 2 * HEAD_DIM), functools.partial(lambda p, i, kk: (i + kk, p), kk=kk))
                for _ in range(n_kv) for kk in range(K_WIN // Q_BLOCK)]
    bias_spec = pl.BlockSpec((2, Q_BLOCK, K_WIN), lambda p, i: (p, 0, 0))
    return q_spec, kv_specs, bias_spec


def _attn_fwd(proj, kp, vp, bias):
    t = proj.shape[0]
    n_win = K_WIN // Q_BLOCK

    def body(q_ref, *refs):
        k_refs, v_refs = refs[:n_win], refs[n_win:2 * n_win]
        b_ref, o_ref, lse_ref = refs[2 * n_win:]
        i = pl.program_id(1)
        k = jnp.concatenate([r[...] for r in k_refs], axis=0)
        v = jnp.concatenate([r[...] for r in v_refs], axis=0)
        q = q_ref[...]
        for hh in range(2):
            lanes = slice(hh * HEAD_DIM, (hh + 1) * HEAD_DIM)
            s = _scores(q[:, lanes], k[:, lanes], b_ref[hh], i)
            m = jnp.max(s, axis=1, keepdims=True)
            p = jnp.exp(s - m)
            l = jnp.sum(p, axis=1, keepdims=True)
            o = jnp.dot(p.astype(BF16), v[:, lanes], preferred_element_type=F32) / l
            o_ref[:, lanes] = o.astype(o_ref.dtype)
            lse_ref[:, lanes] = jnp.broadcast_to(m + jnp.log(l), (Q_BLOCK, HEAD_DIM))

    q_spec, kv_specs, bias_spec = _attn_specs(2)
    out_spec = pl.BlockSpec((Q_BLOCK, 2 * HEAD_DIM), lambda p, i: (i, p))
    return pl.pallas_call(
        body, name="attn_fwd", grid=(N_HEADS // 2, t // Q_BLOCK),
        in_specs=[q_spec] + kv_specs + [bias_spec], out_specs=[out_spec, out_spec],
        out_shape=[jax.ShapeDtypeStruct((t, D_ATTN), BF16), jax.ShapeDtypeStruct((t, D_ATTN), F32)],
        compiler_params=_cparams(2),
    )(proj, *([kp] * n_win), *([vp] * n_win), bias)


def _attn_bwd(proj, kp, vp, bias, att, lse, datt):
    t = proj.shape[0]
    n_win = K_WIN // Q_BLOCK

    def body(q_ref, *refs):
        k_refs, v_refs = refs[:n_win], refs[n_win:2 * n_win]
        b_ref, o_ref, lse_ref, do_ref, dq_ref, dk_ref, dv_ref, db_ref = refs[2 * n_win:]
        i = pl.program_id(1)

        @pl.when(i == 0)
        def _():
            dk_ref[...] = jnp.zeros_like(dk_ref)
            dv_ref[...] = jnp.zeros_like(dv_ref)
            db_ref[...] = jnp.zeros_like(db_ref)

        k = jnp.concatenate([r[...] for r in k_refs], axis=0)
        v = jnp.concatenate([r[...] for r in v_refs], axis=0)
        q = q_ref[...]
        do = do_ref[...]
        rows = pl.ds(pl.multiple_of(i * Q_BLOCK, Q_BLOCK), K_WIN)
        scale = HEAD_DIM ** -0.5
        for hh in range(2):
            lanes = slice(hh * HEAD_DIM, (hh + 1) * HEAD_DIM)
            qh, kh, vh, doh = q[:, lanes], k[:, lanes], v[:, lanes], do[:, lanes]
            s = _scores(qh, kh, b_ref[hh], i)
            p = jnp.exp(s - lse_ref[:, lanes][:, :1])
            dp = lax.dot_general(doh, vh, (((1,), (1,)), ((), ())), preferred_element_type=F32)
            delta = jnp.sum(doh.astype(F32) * o_ref[:, lanes].astype(F32), axis=1, keepdims=True)
            ds = p * (dp - delta)
            db_ref[hh] += ds
            dsb = ds.astype(BF16)
            dq_ref[:, lanes] = (jnp.dot(dsb, kh, preferred_element_type=F32) * scale).astype(dq_ref.dtype)
            dk_ref[rows, lanes] += lax.dot_general(dsb, qh, (((0,), (0,)), ((), ())),
                                                   preferred_element_type=F32) * scale
            dv_ref[rows, lanes] += lax.dot_general(p.astype(BF16), doh, (((0,), (0,)), ((), ())),
                                                   preferred_element_type=F32)

    q_spec, kv_specs, bias_spec = _attn_specs(2)
    row_spec = pl.BlockSpec((Q_BLOCK, 2 * HEAD_DIM), lambda p, i: (i, p))
    full_spec = pl.BlockSpec((t + K_PAD, 2 * HEAD_DIM), lambda p, i: (0, p))
    return pl.pallas_call(
        body, name="attn_bwd", grid=(N_HEADS // 2, t // Q_BLOCK),
        in_specs=[q_spec] + kv_specs + [bias_spec, row_spec, row_spec, row_spec],
        out_specs=[row_spec, full_spec, full_spec, bias_spec],
        out_shape=[jax.ShapeDtypeStruct((t, D_ATTN), BF16), jax.ShapeDtypeStruct((t + K_PAD, D_ATTN), F32),
                   jax.ShapeDtypeStruct((t + K_PAD, D_ATTN), F32),
                   jax.ShapeDtypeStruct((N_HEADS, Q_BLOCK, K_WIN), F32)],
        compiler_params=_cparams(2),
    )(proj, *([kp] * n_win), *([vp] * n_win), bias, att, lse, datt)


CONV_LEAD = CONV_HALO - (CONV_WIDTH - 1)
CONV_LANES = 128
CONV_ROWS = 64


def _conv_specs(t):
    tt = _row_tile(t, CONV_TILE)
    per = tt // CONV_HALO
    n_halo = t // CONV_HALO
    tile = lambda cb: pl.BlockSpec((tt, COL), functools.partial(lambda i, cb: (i, cb), cb=cb))
    prev = lambda cb: pl.BlockSpec((CONV_HALO, COL),
                                   functools.partial(lambda i, cb: (jnp.maximum(i * per - 1, 0), cb), cb=cb))
    nxt = lambda cb: pl.BlockSpec((CONV_HALO, COL),
                                  functools.partial(lambda i, cb: (jnp.minimum((i + 1) * per, n_halo - 1), cb), cb=cb))
    vec = lambda shape: pl.BlockSpec(shape, lambda i: (0, 0))
    return tt, tile, prev, nxt, vec


def _glu(ca, cg, bias):
    return (ca.astype(F32) + bias[:, :D_CONV]) * jax.nn.sigmoid(cg.astype(F32) + bias[:, D_CONV:])


def _taps(ext_ref, tt, first_row, weight_of, out_ref):
    for r0 in range(0, tt, CONV_ROWS):
        for l0 in range(0, D_CONV, CONV_LANES):
            lanes = slice(l0, l0 + CONV_LANES)
            acc = jnp.zeros((CONV_ROWS, CONV_LANES), F32)
            for w in range(CONV_WIDTH):
                acc = acc + ext_ref[first_row(w) + r0:first_row(w) + r0 + CONV_ROWS, lanes] * weight_of(w)[:, lanes]
            out_ref[r0:r0 + CONV_ROWS, lanes] = acc


def _conv_fwd(proj, glu_bias, dw, dw_b, ln_g, ln_b):
    t = proj.shape[0]
    tt, tile, prev, nxt, vec = _conv_specs(t)
    ca_blk, cg_blk = 3 * D_ATTN // COL, 3 * D_ATTN // COL + 1

    def body(ca_ref, cg_ref, pa_ref, pg_ref, gb_ref, dw_ref, dwb_ref, g_ref, b_ref, cs_ref, c_ref, z_ref, ext_ref):
        i = pl.program_id(0)
        bias = gb_ref[...]
        c = _glu(ca_ref[...], cg_ref[...], bias)
        halo = _glu(pa_ref[...], pg_ref[...], bias)
        ext_ref[0:CONV_HALO, :] = jnp.where(i == 0, 0.0, halo)
        ext_ref[CONV_HALO:, :] = c
        c_ref[...] = c
        _taps(ext_ref, tt, lambda w: CONV_LEAD + w, lambda w: dw_ref[w:w + 1, :], z_ref)
        z = z_ref[...] + dwb_ref[...]
        z_ref[...] = z
        mu = jnp.mean(z, axis=-1, keepdims=True)
        zc = z - mu
        y = zc * lax.rsqrt(jnp.mean(zc * zc, axis=-1, keepdims=True) + EPS) * g_ref[...] + b_ref[...]
        cs_ref[...] = (y * jax.nn.sigmoid(y)).astype(cs_ref.dtype)

    out_spec = pl.BlockSpec((tt, D_CONV), lambda i: (i, 0))
    return pl.pallas_call(
        body, name="conv_fwd", grid=(t // tt,),
        in_specs=[tile(ca_blk), tile(cg_blk), prev(ca_blk), prev(cg_blk), vec(glu_bias.shape), vec(dw.shape),
                  vec(dw_b.shape), vec(ln_g.shape), vec(ln_b.shape)],
        out_specs=[out_spec] * 3,
        out_shape=[jax.ShapeDtypeStruct((t, D_CONV), BF16), jax.ShapeDtypeStruct((t, D_CONV), F32),
                   jax.ShapeDtypeStruct((t, D_CONV), F32)],
        scratch_shapes=[pltpu.VMEM((tt + CONV_HALO, D_CONV), F32)], compiler_params=_cparams(1),
    )(proj, proj, proj, proj, glu_bias, dw, dw_b, ln_g, ln_b)


def _conv_bwd(proj, c, z, dcs, glu_bias, dw, ln_g, ln_b):
    t = proj.shape[0]
    tt, tile, prev, nxt, vec = _conv_specs(t)
    n_tiles = t // tt
    ca_blk, cg_blk = 3 * D_ATTN // COL, 3 * D_ATTN // COL + 1

    def ln_bwd(zv, dcsv, g, b):
        mu = jnp.mean(zv, axis=-1, keepdims=True)
        zc = zv - mu
        rstd = lax.rsqrt(jnp.mean(zc * zc, axis=-1, keepdims=True) + EPS)
        zhat = zc * rstd
        y = zhat * g + b
        sig = jax.nn.sigmoid(y)
        dy = dcsv * sig * (1.0 + y * (1.0 - sig))
        dzh = dy * g
        dz = rstd * (dzh - jnp.mean(dzh, axis=-1, keepdims=True) - zhat * jnp.mean(dzh * zhat, axis=-1, keepdims=True))
        return dz, dy, zhat

    def body(ca_ref, cg_ref, c_ref, cprev_ref, z_ref, znext_ref, dcs_ref, dcsnext_ref, gb_ref, dw_ref, g_ref, b_ref,
             dcin_ref, ddw_ref, ddwb_ref, dg_ref, db_ref, dgb_ref, cext_ref, dzext_ref, dc_ref):
        i = pl.program_id(0)

        @pl.when(i == 0)
        def _():
            for ref in (ddw_ref, ddwb_ref, dg_ref, db_ref, dgb_ref):
                ref[...] = jnp.zeros_like(ref)

        g, b = g_ref[...], b_ref[...]
        dz, dy, zhat = ln_bwd(z_ref[...], dcs_ref[...], g, b)
        dz_next, _, _ = ln_bwd(znext_ref[...], dcsnext_ref[...], g, b)
        dg_ref[...] += _colsum(dy * zhat)
        db_ref[...] += _colsum(dy)
        ddwb_ref[...] += _colsum(dz)
        dzext_ref[0:tt, :] = dz
        dzext_ref[tt:, :] = jnp.where(i == n_tiles - 1, 0.0, dz_next)
        cext_ref[0:CONV_HALO, :] = jnp.where(i == 0, 0.0, cprev_ref[...])
        cext_ref[CONV_HALO:, :] = c_ref[...]
        _taps(dzext_ref, tt, lambda w: CONV_WIDTH - 1 - w, lambda w: dw_ref[w:w + 1, :], dc_ref)
        for w in range(CONV_WIDTH):
            ddw_ref[w:w + 1, :] += _colsum(cext_ref[CONV_LEAD + w:CONV_LEAD + w + tt, :] * dz)
        bias = gb_ref[...]
        a_in = ca_ref[...].astype(F32) + bias[:, :D_CONV]
        sg = jax.nn.sigmoid(cg_ref[...].astype(F32) + bias[:, D_CONV:])
        dc = dc_ref[...]
        dcin = jnp.concatenate([dc * sg, dc * a_in * sg * (1.0 - sg)], axis=1)
        dcin_ref[...] = dcin.astype(dcin_ref.dtype)
        dgb_ref[...] += _colsum(dcin)

    row = lambda: pl.BlockSpec((tt, D_CONV), lambda i: (i, 0))
    per = tt // CONV_HALO
    n_halo = t // CONV_HALO
    prev_row = pl.BlockSpec((CONV_HALO, D_CONV), lambda i: (jnp.maximum(i * per - 1, 0), 0))
    next_row = lambda: pl.BlockSpec((CONV_HALO, D_CONV), lambda i: (jnp.minimum((i + 1) * per, n_halo - 1), 0))
    acc = lambda shape: pl.BlockSpec(shape, lambda i: (0, 0))
    return pl.pallas_call(
        body, name="conv_bwd", grid=(n_tiles,),
        in_specs=[tile(ca_blk), tile(cg_blk), row(), prev_row, row(), next_row(), row(), next_row(),
                  vec(glu_bias.shape), vec(dw.shape), vec(ln_g.shape), vec(ln_b.shape)],
        out_specs=[pl.BlockSpec((tt, 2 * D_CONV), lambda i: (i, 0)), acc(dw.shape), acc((1, D_CONV)),
                   acc((1, D_CONV)), acc((1, D_CONV)), acc((1, 2 * D_CONV))],
        out_shape=[jax.ShapeDtypeStruct((t, 2 * D_CONV), BF16), jax.ShapeDtypeStruct(dw.shape, F32),
                   jax.ShapeDtypeStruct((1, D_CONV), F32), jax.ShapeDtypeStruct((1, D_CONV), F32),
                   jax.ShapeDtypeStruct((1, D_CONV), F32), jax.ShapeDtypeStruct((1, 2 * D_CONV), F32)],
        scratch_shapes=[pltpu.VMEM((tt + CONV_HALO, D_CONV), F32), pltpu.VMEM((tt + CONV_HALO, D_CONV), F32),
                        pltpu.VMEM((tt, D_CONV), F32)],
        compiler_params=_cparams(1),
    )(proj, proj, c, c, z, z, dcs, dcs, glu_bias, dw, ln_g, ln_b)


def _place():
    x, y, c = lax.axis_index("x"), lax.axis_index("y"), lax.axis_index("c")
    chips = [(1 - x, y), (x, 1 - y), (1 - x, 1 - y)]
    return x, y, c, chips


def _chip_index(chip):
    return 2 * chip[0] + chip[1]


def _half_rows(c, half):
    return pl.ds(pl.multiple_of(c * half, 16), half)


def _run_carry(name, carry):
    c_in, c_out = len(carry.ins), len(carry.outs)

    def body(*refs):
        ins, outs, sems = refs[:c_in], refs[c_in:c_in + c_out], refs[c_in + c_out:]
        for when in ("first", "late", "last"):
            for w, fn in carry.phases:
                if w == when:
                    fn(ins, outs, sems)

    return pl.pallas_call(
        body, name=name, in_specs=[ANY] * c_in, out_specs=[ANY] * c_out, out_shape=list(carry.outs),
        scratch_shapes=list(carry.sems), input_output_aliases=dict(carry.aliases),
    )(*carry.ins)


def _gather_carry(blocked):
    n = len(blocked)

    def over_ici(o_refs, send_sems, recv_sems):
        x, y, c, chips = _place()
        me = _chip_index((x, y))
        copies = []
        for a in range(n):
            mine = o_refs[a].at[me, _half_rows(c, o_refs[a].shape[1] // 2), :]
            for k, chip in enumerate(chips):
                copies.append(pltpu.make_async_remote_copy(
                    src_ref=mine, dst_ref=mine, send_sem=send_sems.at[6 * a + k], recv_sem=recv_sems.at[6 * a + k],
                    device_id=(chip[0], chip[1], c), device_id_type=MESH))
        return copies

    def to_sibling(o_refs, send_sems, recv_sems, sent_by_me):
        x, y, c, chips = _place()
        copies = []
        for a in range(n):
            rows = _half_rows(c if sent_by_me else 1 - c, o_refs[a].shape[1] // 2)
            for k, chip in enumerate(chips):
                landed = o_refs[a].at[_chip_index(chip), rows, :]
                copies.append(pltpu.make_async_remote_copy(
                    src_ref=landed, dst_ref=landed, send_sem=send_sems.at[6 * a + 3 + k],
                    recv_sem=recv_sems.at[6 * a + 3 + k], device_id=(x, y, 1 - c), device_id_type=MESH))
        return copies

    def start(ins, outs, sems):
        for cp in over_ici(outs, *sems):
            cp.start()

    def hand_on(ins, outs, sems):
        for arrived, onward in zip(over_ici(outs, *sems), to_sibling(outs, *sems, True)):
            arrived.wait_recv()
            onward.start()

    def finish(ins, outs, sems):
        for cp in to_sibling(outs, *sems, False):
            cp.wait_recv()
        for cp in over_ici(outs, *sems) + to_sibling(outs, *sems, True):
            cp.wait_send()

    return _Carry(
        ins=list(blocked), outs=[jax.ShapeDtypeStruct(w.shape, w.dtype) for w in blocked],
        aliases={a: a for a in range(n)},
        sems=[pltpu.SemaphoreType.DMA((6 * n,)), pltpu.SemaphoreType.DMA((6 * n,))],
        phases=[("first", start), ("late", hand_on), ("last", finish)])


def _pair_exchange(name, grads):
    n = len(grads)

    def body(*refs):
        g_refs, land_refs = refs[:n], refs[n:2 * n]
        send_sems, recv_sems = refs[2 * n:]
        x, y, c, _ = _place()
        copies = []
        for a in range(n):
            half = g_refs[a].shape[1] // 2
            cp = pltpu.make_async_remote_copy(
                src_ref=g_refs[a].at[:, _half_rows(1 - c, half), :], dst_ref=land_refs[a],
                send_sem=send_sems.at[a], recv_sem=recv_sems.at[a], device_id=(x, y, 1 - c), device_id_type=MESH)
            cp.start()
            copies.append(cp)
        for cp in copies:
            cp.wait()

    return pl.pallas_call(
        body, name=name, in_specs=[ANY] * n, out_specs=[ANY] * n,
        out_shape=[jax.ShapeDtypeStruct((g.shape[0], g.shape[1] // 2, g.shape[2]), g.dtype) for g in grads],
        scratch_shapes=[pltpu.SemaphoreType.DMA((n,)), pltpu.SemaphoreType.DMA((n,))],
    )(*grads)


def _to_owner_carry(parts):
    n = len(parts)

    def sends(p_refs, l_refs, send_sems, recv_sems):
        x, y, c, chips = _place()
        me = _chip_index((x, y))
        return [pltpu.make_async_remote_copy(
            src_ref=p_refs[a].at[_chip_index(chip)], dst_ref=l_refs[a].at[me],
            send_sem=send_sems.at[3 * a + k], recv_sem=recv_sems.at[3 * a + k],
            device_id=(chip[0], chip[1], c), device_id_type=MESH) for a in range(n) for k, chip in enumerate(chips)]

    def start(ins, outs, sems):
        for cp in sends(ins, outs, *sems):
            cp.start()

    def finish(ins, outs, sems):
        x, y, c, chips = _place()
        send_sems, recv_sems = sems
        for a in range(n):
            for k, chip in enumerate(chips):
                slot = outs[a].at[_chip_index(chip)]
                pltpu.make_async_remote_copy(
                    src_ref=slot, dst_ref=slot, send_sem=send_sems.at[3 * a + k], recv_sem=recv_sems.at[3 * a + k],
                    device_id=(chip[0], chip[1], c), device_id_type=MESH).wait_recv()
        for cp in sends(ins, outs, *sems):
            cp.wait_send()

    return _Carry(
        ins=list(parts), outs=[jax.ShapeDtypeStruct(p.shape, p.dtype) for p in parts], aliases={},
        sems=[pltpu.SemaphoreType.DMA((3 * n,)), pltpu.SemaphoreType.DMA((3 * n,))],
        phases=[("first", start), ("last", finish)])


def _swap_halves(halves):
    n = len(halves)

    def body(*refs):
        h_refs, o_refs = refs[:n], refs[n:2 * n]
        send_sems, recv_sems = refs[2 * n:]
        x, y, c, _ = _place()
        copies = []
        for a in range(n):
            cp = pltpu.make_async_remote_copy(
                src_ref=h_refs[a], dst_ref=o_refs[a], send_sem=send_sems.at[a], recv_sem=recv_sems.at[a],
                device_id=(x, y, 1 - c), device_id_type=MESH)
            cp.start()
            copies.append(cp)
        for cp in copies:
            cp.wait()

    return pl.pallas_call(
        body, name="grad_swap_halves", in_specs=[ANY] * n, out_specs=[ANY] * n,
        out_shape=[jax.ShapeDtypeStruct(h.shape, h.dtype) for h in halves],
        scratch_shapes=[pltpu.SemaphoreType.DMA((n,)), pltpu.SemaphoreType.DMA((n,))],
    )(*halves)


def _all_devices(name, block):
    r, cols = block.shape

    def body(b_ref, all_ref, sum_ref, send_sems, recv_sems):
        x, y, c, _ = _place()
        me = 4 * x + 2 * y + c
        all_ref[me] = b_ref[...]
        flips = [(fx, fy, fc) for fx in (0, 1) for fy in (0, 1) for fc in (0, 1)][1:]
        copies = []
        for k, (fx, fy, fc) in enumerate(flips):
            cp = pltpu.make_async_remote_copy(
                src_ref=b_ref, dst_ref=all_ref.at[me], send_sem=send_sems.at[k], recv_sem=recv_sems.at[k],
                device_id=(x ^ fx, y ^ fy, c ^ fc), device_id_type=MESH)
            cp.start()
            copies.append(cp)
        for k, (fx, fy, fc) in enumerate(flips):
            slot = all_ref.at[4 * (x ^ fx) + 2 * (y ^ fy) + (c ^ fc)]
            pltpu.make_async_remote_copy(
                src_ref=slot, dst_ref=slot, send_sem=send_sems.at[k], recv_sem=recv_sems.at[k],
                device_id=(x ^ fx, y ^ fy, c ^ fc), device_id_type=MESH).wait_recv()
        for cp in copies:
            cp.wait_send()
        acc = all_ref[0]
        for d in range(1, N_DEV):
            acc = acc + all_ref[d]
        sum_ref[...] = acc

    vmem = pl.BlockSpec(memory_space=pltpu.VMEM)
    return pl.pallas_call(
        body, name=name, in_specs=[vmem], out_specs=[vmem, vmem],
        out_shape=[jax.ShapeDtypeStruct((N_DEV, r, cols), F32), jax.ShapeDtypeStruct((r, cols), F32)],
        scratch_shapes=[pltpu.SemaphoreType.DMA((N_DEV - 1,)), pltpu.SemaphoreType.DMA((N_DEV - 1,))],
    )(block)


PACK = 1024


def _packed_rows(shape, width):
    size, last = int(np.prod(shape)), shape[-1]
    cols = last if last <= width else width
    assert size % cols == 0
    return size // cols, cols


def _pack(vals, width=PACK):
    rows = []
    for v in vals:
        n_rows, cols = _packed_rows(v.shape, width)
        rows.append(jnp.pad(v.reshape(n_rows, cols).astype(F32), ((0, 0), (0, width - cols))))
    buf = jnp.concatenate(rows, axis=0)
    return jnp.pad(buf, ((0, (-buf.shape[0]) % 8), (0, 0)))


def _unpack(buf, shapes, width=PACK):
    out, r = [], 0
    for shape in shapes:
        n_rows, cols = _packed_rows(shape, width)
        out.append(buf[r:r + n_rows, :cols].reshape(shape))
        r += n_rows
    return out


def _ffn_hidden(name, n, wg, wu, tm=1024, carry=None):
    m, k = n.shape
    nj, fb, _ = wg.shape
    tm = _row_tile(m, tm)

    def core(n_ref, wg_ref, wu_ref, a_ref, b_ref, s_ref):
        nv = n_ref[...]
        a = _dot(nv, wg_ref[...], True)
        b = _dot(nv, wu_ref[...], True)
        a_ref[...] = a.astype(a_ref.dtype)
        b_ref[...] = b.astype(b_ref.dtype)
        s_ref[...] = (a * jax.nn.sigmoid(a) * b).astype(s_ref.dtype)

    w_spec = pl.BlockSpec((None, fb, k), lambda j, i: (j, 0, 0))
    out_spec = pl.BlockSpec((None, tm, fb), lambda j, i: (j, i, 0))
    return _call(name, core, (nj, m // tm), [pl.BlockSpec((tm, k), lambda j, i: (i, 0)), w_spec, w_spec],
                 [out_spec] * 3, [jax.ShapeDtypeStruct((nj, m, fb), BF16)] * 3, [], [n, wg, wu], carry)


def _ffn_d_hidden(name, df, wd, a, b, tm=1024):
    m, k = df.shape
    nj, fb, _ = wd.shape
    tm = _row_tile(m, tm)

    def body(df_ref, wd_ref, a_ref, b_ref, da_ref, db_ref):
        ds = _dot(df_ref[...], wd_ref[...], True)
        av, bv = a_ref[...].astype(F32), b_ref[...].astype(F32)
        sig = jax.nn.sigmoid(av)
        da_ref[...] = (ds * bv * sig * (1.0 + av * (1.0 - sig))).astype(da_ref.dtype)
        db_ref[...] = (ds * av * sig).astype(db_ref.dtype)

    blk = pl.BlockSpec((None, tm, fb), lambda j, i: (j, i, 0))
    return pl.pallas_call(
        body, name=name, grid=(nj, m // tm),
        in_specs=[pl.BlockSpec((tm, k), lambda j, i: (i, 0)), pl.BlockSpec((None, fb, k), lambda j, i: (j, 0, 0)),
                  blk, blk],
        out_specs=[blk, blk], out_shape=[jax.ShapeDtypeStruct((nj, m, fb), BF16)] * 2,
        compiler_params=_cparams(2),
    )(df, wd, a, b)


def kernel(x, ffn1_norm_pre, ffn1_w_gate, ffn1_w_up, ffn1_w_down, ffn1_norm_post, mix_norm_pre, w_in, gate_bias, rel_table, w_attn_out, conv_glu_bias, conv_dw_w, conv_dw_b, conv_ln_g, conv_ln_b, conv_w_out, w_out, mix_norm_post, ffn2_norm_pre, ffn2_w_gate, ffn2_w_up, ffn2_w_down, ffn2_norm_post, loss_target, m_ffn1_norm_pre, m_ffn1_w_gate, m_ffn1_w_up, m_ffn1_w_down, m_ffn1_norm_post, m_mix_norm_pre, m_w_in, m_gate_bias, m_rel_table, m_w_attn_out, m_conv_glu_bias, m_conv_dw_w, m_conv_dw_b, m_conv_ln_g, m_conv_ln_b, m_conv_w_out, m_w_out, m_mix_norm_post, m_ffn2_norm_pre, m_ffn2_w_gate, m_ffn2_w_up, m_ffn2_w_down, m_ffn2_norm_post, v_ffn1_norm_pre, v_ffn1_w_gate, v_ffn1_w_up, v_ffn1_w_down, v_ffn1_norm_post, v_mix_norm_pre, v_w_in, v_gate_bias, v_rel_table, v_w_attn_out, v_conv_glu_bias, v_conv_dw_w, v_conv_dw_b, v_conv_ln_g, v_conv_ln_b, v_conv_w_out, v_w_out, v_mix_norm_post, v_ffn2_norm_pre, v_ffn2_w_gate, v_ffn2_w_up, v_ffn2_w_down, v_ffn2_norm_post):
    args = dict(locals())
    names = ['ffn1_norm_pre', 'ffn1_w_gate', 'ffn1_w_up', 'ffn1_w_down', 'ffn1_norm_post', 'mix_norm_pre', 'w_in',
             'gate_bias', 'rel_table', 'w_attn_out', 'conv_glu_bias', 'conv_dw_w', 'conv_dw_b', 'conv_ln_g',
             'conv_ln_b', 'conv_w_out', 'w_out', 'mix_norm_post', 'ffn2_norm_pre', 'ffn2_w_gate', 'ffn2_w_up',
             'ffn2_w_down', 'ffn2_norm_post']
    big = ['ffn1_w_gate', 'ffn1_w_up', 'ffn1_w_down', 'w_in', 'w_attn_out', 'conv_w_out', 'w_out', 'ffn2_w_gate',
           'ffn2_w_up', 'ffn2_w_down']
    small = [n for n in names if n not in big]

    xs, target = x[0], loss_target[0]
    t, d = xs.shape
    cx, cy = lax.axis_index("x"), lax.axis_index("y")
    chip = 2 * cx + cy

    dw_shard = conv_dw_w[0, :, 0, :]
    cshard = dw_shard.shape[1]
    dw_all, _ = _all_devices("gather_dw", _pack([dw_shard], width=cshard))
    dw_full = jnp.concatenate([dw_all[2 * j, :CONV_WIDTH, :cshard] for j in range(N_CHIPS)], axis=1)
    dw_full = jnp.pad(dw_full, ((0, CONV_HALO - CONV_WIDTH), (0, 0)))
    peers = [(1 - cx, cy), (cx, 1 - cy), (1 - cx, 1 - cy)]
    pos = jnp.stack([lax.axis_index("c"), chip] + [_chip_index(p) for p in peers]).astype(jnp.int32)
    transposed = ("ffn1_w_gate", "ffn1_w_up", "ffn2_w_gate", "ffn2_w_up")
    weight_of = lambda n: n[2:] if n[:2] in ("m_", "v_") else n
    shard = lambda n: jnp.transpose(args[n][0]) if weight_of(n) in transposed else args[n][0]
    unshard = lambda n, v: (jnp.transpose(v) if n in transposed else v)[None]
    own = {n: _cast_into("cast_" + n, pos, shard(n)) for n in big}
    gather = lambda *ns: _gather_carry([own[n] for n in ns])
    res_spec = [(d, F32), (d, F32), (d, BF16)]

    wg1, wu1, wd1 = _run_carry("gather_ffn1", gather("ffn1_w_gate", "ffn1_w_up", "ffn1_w_down"))
    n1 = _rms_fwd("ffn1_pre", xs, ffn1_norm_pre)
    (a1, b1, s1), (win, wao, wco, wout) = _ffn_hidden(
        "ffn1_hidden", n1, wg1, wu1, carry=gather("w_in", "w_attn_out", "conv_w_out", "w_out"))
    (f1, h1, u), (wg2,) = _mm_kblk(
        "ffn1_down", [(s1, wd1)], trans_w=False, epilogue=_ep_post_res_pre(0.5), rows=[xs],
        vecs=[ffn1_norm_post, mix_norm_pre], row_outs=res_spec, carry=gather("ffn2_w_gate"))
    proj, (wu2, wd2) = _mm_nblk("mix_in", u, win, trans_w=False, out_blocked=False, out_dtype=BF16,
                                carry=gather("ffn2_w_up", "ffn2_w_down"))
    table_pad = jnp.pad(rel_table[0], ((0, 0), (0, REL_PAD - rel_table.shape[2])))
    bias = _bias_expand(table_pad)
    kp = jnp.pad(proj[:, D_ATTN:2 * D_ATTN], ((K_PAD, 0), (0, 0)))
    vp = jnp.pad(proj[:, 2 * D_ATTN:3 * D_ATTN], ((K_PAD, 0), (0, 0)))
    att, lse = _attn_fwd(proj, kp, vp, bias)
    cs, c_glu, z_conv = _conv_fwd(proj, conv_glu_bias, dw_full, conv_dw_b, conv_ln_g, conv_ln_b)
    y_a = _mm_nblk("attn_out", att, wao, trans_w=False, out_blocked=False, out_dtype=F32)
    y_b = _mm_nblk("conv_out", cs, wco, trans_w=False, out_blocked=False, out_dtype=F32)
    merged = _merge_fwd("mix_merge", y_a, y_b, proj, gate_bias)
    (mo, h2, n2), _ = _mm_kblk(
        "mix_out", [(merged, wout)], trans_w=False, epilogue=_ep_post_res_pre(1.0), rows=[h1],
        vecs=[mix_norm_post, ffn2_norm_pre], row_outs=res_spec)
    (a2, b2, s2), _ = _ffn_hidden("ffn2_hidden", n2, wg2, wu2)
    g = {}
    (dy, df2, err2, g["ffn2_norm_post"]), _ = _mm_kblk(
        "ffn2_down", [(s2, wd2)], trans_w=False, epilogue=_ep_loss(0.5, d), rows=[h2, target],
        vecs=[ffn2_norm_post], row_outs=[(d, F32), (d, BF16)], vec_outs=[d, d])
    loss = lax.psum(0.5 * jnp.sum(err2) / d, ("x", "y", "c"))

    parts, landed = {}, {}

    def ffn_bwd(tag, df, n, a, b, s, wg, wu, wd, **epilogue):
        da, db = _ffn_d_hidden(tag + "_d_hidden", df, wd, a, b)
        group = [tag + "_w_down", tag + "_w_gate", tag + "_w_up"]
        local = [_mm_tn(tag + "_g_down", s, "blk", df, "full"), _mm_tn(tag + "_g_gate", da, "blk", n, "full"),
                 _mm_tn(tag + "_g_up", db, "blk", n, "full")]
        return _mm_kblk(tag + "_d_n", [(da, wg), (db, wu)], trans_w=False, carry=pair_sums(tag, group, local),
                        **epilogue), group

    def pair_sums(tag, group, local):
        theirs = _pair_exchange("pair_" + tag, local)
        for n, mine, other in zip(group, local, theirs):
            parts[n] = _add_pair("pair_sum_" + n, pos, mine, other)
        return _to_owner_carry([parts[n] for n in group])

    def keep(group, carried):
        for n, val in zip(group, carried):
            landed[n] = val

    ((dh2, dmo, g["ffn2_norm_pre"], g["mix_norm_post"]), carried), group = ffn_bwd(
        "ffn2", df2, n2, a2, b2, s2, wg2, wu2, wd2, epilogue=_ep_pre_bwd_post(1.0), rows=[h2, dy, mo],
        vecs=[ffn2_norm_pre, mix_norm_post], row_outs=[(d, F32), (d, BF16)], vec_outs=[d, d])
    keep(group, carried)
    dmerged = _mm_nblk("mix_d_merged", dmo, wout, trans_w=True, out_blocked=False, out_dtype=F32)
    g_wout = _mm_tn("mix_g_out", merged, "col", dmo, "full")
    dy_a, dy_b, dgates, g["gate_bias"] = _merge_bwd("mix_d_merge", dmerged, y_a, y_b, proj, gate_bias)
    (datt,), _ = _mm_kblk("attn_d_out", [(dy_a, wao)], trans_w=True, out_dtype=BF16)
    (dcs,), _ = _mm_kblk("conv_d_out", [(dy_b, wco)], trans_w=True, out_dtype=F32)
    g_wao = _mm_tn("attn_g_out", att, "full", dy_a, "col")
    g_wco = _mm_tn("conv_g_out", cs, "full", dy_b, "col")
    dq, dkp, dvp, dbias = _attn_bwd(proj, kp, vp, bias, att, lse, datt)
    g["rel_table"] = _bias_fold(dbias)[:, :rel_table.shape[2]]
    dcin, g_dw, g["conv_dw_b"], g["conv_ln_g"], g["conv_ln_b"], g["conv_glu_bias"] = _conv_bwd(
        proj, c_glu, z_conv, dcs, conv_glu_bias, dw_full, conv_ln_g, conv_ln_b)
    dproj = jnp.concatenate([dq, dkp[K_PAD:].astype(BF16), dvp[K_PAD:].astype(BF16), dcin, dgates], axis=1)
    g_win = _mm_tn("mix_g_in", u, "full", dproj, "col")
    group = ["w_out", "w_attn_out", "conv_w_out", "w_in"]
    (dh1, df1, g["mix_norm_pre"], g["ffn1_norm_post"]), carried = _mm_kblk(
        "mix_d_in", [(dproj, win)], trans_w=True, epilogue=_ep_pre_bwd_post(0.5), rows=[h1, dh2, f1],
        vecs=[mix_norm_pre, ffn1_norm_post], row_outs=[(d, F32), (d, BF16)], vec_outs=[d, d],
        carry=pair_sums("mix", group, [g_wout, g_wao, g_wco, g_win]))
    keep(group, carried)
    ((grad_x, g["ffn1_norm_pre"]), carried), group = ffn_bwd(
        "ffn1", df1, n1, a1, b1, s1, wg1, wu1, wd1, epilogue=_ep_pre_bwd_first(), rows=[xs, dh1],
        vecs=[ffn1_norm_pre], row_outs=[(d, F32)], vec_outs=[d])
    keep(group, carried)

    halves = [_add_chips("chip_sum_" + n, pos, parts[n], landed[n]) for n in big]
    other_halves = _swap_halves(halves)

    g["conv_dw_w"] = g_dw[:CONV_WIDTH]
    _, small_sum = _all_devices("sum_small", _pack([g[n] for n in small]))
    for n, val in zip(small, _unpack(small_sum, [g[n].shape for n in small])):
        g[n] = val
    g["conv_dw_w"] = lax.dynamic_slice_in_dim(g["conv_dw_w"], chip * cshard, cshard, axis=1)

    grads, deltas, new_m, new_v = {}, {}, {}, {}
    for n, mine, other in zip(big, halves, other_halves):
        gr, dl, m2, v2 = _adamw_halves("adamw_" + n, pos, shard(n), shard("m_" + n), shard("v_" + n), mine, other)
        grads[n], deltas[n], new_m[n], new_v[n] = unshard(n, gr), unshard(n, dl), unshard(n, m2), unshard(n, v2)
    shapes = [g[n].shape for n in small]
    packed = lambda pre: _pack([args[pre + n].reshape(shp) for n, shp in zip(small, shapes)])
    dl, m2, v2 = _adamw("adamw_small", packed(""), _pack([g[n] for n in small]), packed("m_"), packed("v_"))
    for n, a_, b_, c_ in zip(small, _unpack(dl, shapes), _unpack(m2, shapes), _unpack(v2, shapes)):
        shape = args[n].shape
        grads[n], deltas[n], new_m[n], new_v[n] = (g[n].reshape(shape), a_.reshape(shape), b_.reshape(shape),
                                                   c_.reshape(shape))

    return (loss, grad_x[None], *[grads[n] for n in names], *[deltas[n] for n in names],
            *[new_m[n] for n in names], *[new_v[n] for n in names])
```

```python
import functools

import numpy as np
import jax
import jax.numpy as jnp
from jax import lax
from jax.experimental import pallas as pl
from jax.experimental.pallas import tpu as pltpu

F32 = jnp.float32
BF16 = jnp.bfloat16
MESH = pl.DeviceIdType.MESH
ANY = pl.BlockSpec(memory_space=pl.ANY)

EPS = 1e-6
CHUNK = 64
LEFT_CHUNKS = 8
N_HEADS = 8
HEAD_DIM = 64
D_ATTN = N_HEADS * HEAD_DIM
D_CONV = 512
CONV_WIDTH = 31
REL_CLIP = 128
N_CHIPS = 4
N_DEV = 8
Q_BLOCK = 4 * CHUNK
K_PAD = LEFT_CHUNKS * CHUNK
K_WIN = K_PAD + Q_BLOCK
REL_EXT = 1024
REL_PAD = 384
CONV_HALO = 32
CONV_TILE = 256
COL = 512
NEG = -1e30

ADAM_LR = 0.001
ADAM_B1 = 0.9
ADAM_B2 = 0.999
ADAM_EPS = 1e-08
ADAM_WD = 0.01
ADAM_STEP = 10

VMEM_LIMIT_BYTES = 56 * 1024 * 1024


def _cparams(n_grid):
    return pltpu.CompilerParams(dimension_semantics=("arbitrary",) * n_grid, vmem_limit_bytes=VMEM_LIMIT_BYTES)


def _row_tile(rows, want):
    if rows <= want:
        return rows
    for t in range(want - want % 16, 0, -16):
        if rows % t == 0:
            return t
    raise ValueError((rows, want))


def _dot(a, w, trans_w):
    dims = (((1,), (1,)), ((), ())) if trans_w else (((1,), (0,)), ((), ()))
    return lax.dot_general(a, w, dims, preferred_element_type=F32)


class _Carry:
    LATE_STEPS = 2

    def __init__(self, ins, outs, aliases, sems, phases):
        self.ins, self.outs, self.aliases, self.sems, self.phases = ins, outs, aliases, sems, phases


def _call(name, core, grid, in_specs, out_specs, out_shape, scratch, args, carry=None):
    n_in, n_out, n_scr = len(in_specs), len(out_specs), len(scratch)
    if carry is None:
        out = pl.pallas_call(core, name=name, grid=grid, in_specs=in_specs, out_specs=out_specs, out_shape=out_shape,
                             scratch_shapes=scratch, compiler_params=_cparams(len(grid)))(*args)
        return list(out), []
    c_in, c_out = len(carry.ins), len(carry.outs)
    total = int(np.prod(grid))
    late = max(total - 1 - _Carry.LATE_STEPS, 0)

    def body(*refs):
        ins, refs = refs[:n_in], refs[n_in:]
        c_ins, refs = refs[:c_in], refs[c_in:]
        outs, refs = refs[:n_out], refs[n_out:]
        c_outs, refs = refs[:c_out], refs[c_out:]
        scr, c_sems = refs[:n_scr], refs[n_scr:]
        step = pl.program_id(0)
        for axis in range(1, len(grid)):
            step = step * grid[axis] + pl.program_id(axis)

        def run(when, at):
            for w, fn in carry.phases:
                if w == when:
                    pl.when(step == at)(functools.partial(fn, c_ins, c_outs, c_sems))

        run("first", 0)
        core(*ins, *outs, *scr)
        run("late", late)
        run("last", total - 1)

    out = pl.pallas_call(
        body, name=name, grid=grid, in_specs=list(in_specs) + [ANY] * c_in, out_specs=list(out_specs) + [ANY] * c_out,
        out_shape=list(out_shape) + list(carry.outs), scratch_shapes=list(scratch) + list(carry.sems),
        input_output_aliases={n_in + a: n_out + b for a, b in carry.aliases.items()},
        compiler_params=_cparams(len(grid)),
    )(*args, *carry.ins)
    return list(out[:n_out]), list(out[n_out:])


def _mm_nblk(name, a, w, *, trans_w, out_blocked, out_dtype, tm=1024, carry=None):
    m, k = a.shape
    nj = w.shape[0]
    nb = w.shape[1] if trans_w else w.shape[2]
    tm = _row_tile(m, tm)

    def core(a_ref, w_ref, o_ref):
        o_ref[...] = _dot(a_ref[...], w_ref[...], trans_w).astype(o_ref.dtype)

    if out_blocked:
        out_shape, out_spec = (nj, m, nb), pl.BlockSpec((None, tm, nb), lambda j, i: (j, i, 0))
    else:
        out_shape, out_spec = (m, nj * nb), pl.BlockSpec((tm, nb), lambda j, i: (i, j))
    out, carried = _call(
        name, core, (nj, m // tm),
        [pl.BlockSpec((tm, k), lambda j, i: (i, 0)), pl.BlockSpec((None,) + w.shape[1:], lambda j, i: (j, 0, 0))],
        [out_spec], [jax.ShapeDtypeStruct(out_shape, out_dtype)], [], [a, w], carry)
    return out[0] if carry is None else (out[0], carried)


def _mm_kblk(name, pairs, *, trans_w, out_dtype=F32, tm=512, sub=256, epilogue=None, rows=(), vecs=(), row_outs=None,
             vec_outs=(), carry=None):
    w0 = pairs[0][1]
    nj = w0.shape[0]
    n = w0.shape[1] if trans_w else w0.shape[2]
    kb = w0.shape[2] if trans_w else w0.shape[1]
    m = pairs[0][0].shape[-2]
    tm = _row_tile(m, tm)
    ts = _row_tile(tm, sub)
    n_pairs, n_rows, n_vecs = len(pairs), len(rows), len(vecs)
    if epilogue is None:
        epilogue, row_outs = (lambda acc, r, v: ([acc], [])), [(n, out_dtype)]
    n_ro, n_vo = len(row_outs), len(vec_outs)

    def core(*refs):
        pair_refs, refs = refs[:2 * n_pairs], refs[2 * n_pairs:]
        row_refs, refs = refs[:n_rows], refs[n_rows:]
        vec_refs, refs = refs[:n_vecs], refs[n_vecs:]
        ro_refs, vo_refs = refs[:n_ro], refs[n_ro:]
        if n_vo:
            @pl.when(pl.program_id(0) == 0)
            def _():
                for ref in vo_refs:
                    ref[...] = jnp.zeros_like(ref)

        vec_vals = [v[...] for v in vec_refs]
        sums = None
        for r0 in range(0, tm, ts):
            sub_rows = slice(r0, r0 + ts)
            acc = None
            for p in range(n_pairs):
                a_ref, w_ref = pair_refs[2 * p], pair_refs[2 * p + 1]
                for j in range(nj):
                    a_blk = a_ref[j, sub_rows, :] if len(a_ref.shape) == 3 else a_ref[sub_rows, j * kb:(j + 1) * kb]
                    part = _dot(a_blk, w_ref[j], trans_w)
                    acc = part if acc is None else acc + part
            ro, vo = epilogue(acc, [r[sub_rows, :] for r in row_refs], vec_vals)
            for ref, val in zip(ro_refs, ro):
                ref[sub_rows, :] = val.astype(ref.dtype)
            sums = vo if sums is None else [s + v for s, v in zip(sums, vo)]
        for ref, val in zip(vo_refs, sums or []):
            ref[...] += val

    in_specs, args = [], []
    for a, w in pairs:
        if a.ndim == 3:
            in_specs.append(pl.BlockSpec((nj, tm, kb), lambda i: (0, i, 0)))
        else:
            in_specs.append(pl.BlockSpec((tm, nj * kb), lambda i: (i, 0)))
        in_specs.append(pl.BlockSpec(w.shape, lambda i: (0, 0, 0), pipeline_mode=pl.Buffered(1)))
        args += [a, w]
    in_specs += [pl.BlockSpec((tm, r.shape[1]), lambda i: (i, 0)) for r in rows]
    in_specs += [pl.BlockSpec(v.shape, lambda i: (0, 0)) for v in vecs]
    out_specs = [pl.BlockSpec((tm, cols), lambda i: (i, 0)) for cols, _ in row_outs]
    out_specs += [pl.BlockSpec((1, cols), lambda i: (0, 0)) for cols in vec_outs]
    out_shape = [jax.ShapeDtypeStruct((m, cols), dt) for cols, dt in row_outs]
    out_shape += [jax.ShapeDtypeStruct((1, cols), F32) for cols in vec_outs]
    return _call(name, core, (m // tm,), in_specs, out_specs, out_shape, [], args + list(rows) + list(vecs), carry)


def _mm_tn(name, a, a_mode, b, b_mode, *, out_dtype=BF16, tt=2048):
    nj = N_CHIPS
    t = a.shape[-2]
    tt = _row_tile(t, tt)

    def spec(x, mode):
        if mode == "full":
            return x.shape[1], pl.BlockSpec((tt, x.shape[1]), lambda j, s: (s, 0))
        if mode == "col":
            cb = x.shape[1] // nj
            return cb, pl.BlockSpec((tt, cb), lambda j, s: (s, j))
        return x.shape[2], pl.BlockSpec((None, tt, x.shape[2]), lambda j, s: (j, s, 0))

    ca, a_spec = spec(a, a_mode)
    cb, b_spec = spec(b, b_mode)
    n_steps = t // tt

    def body(a_ref, b_ref, o_ref, acc_ref):
        s = pl.program_id(1)

        @pl.when(s == 0)
        def _():
            acc_ref[...] = jnp.zeros_like(acc_ref)

        acc_ref[...] += lax.dot_general(a_ref[...], b_ref[...], (((0,), (0,)), ((), ())),
                                        preferred_element_type=F32)

        @pl.when(s == n_steps - 1)
        def _():
            o_ref[...] = acc_ref[...].astype(o_ref.dtype)

    return pl.pallas_call(
        body, name=name, grid=(nj, n_steps), in_specs=[a_spec, b_spec],
        out_specs=pl.BlockSpec((None, ca, cb), lambda j, s: (j, 0, 0)),
        out_shape=jax.ShapeDtypeStruct((nj, ca, cb), out_dtype),
        scratch_shapes=[pltpu.VMEM((ca, cb), F32)], compiler_params=_cparams(2),
    )(a, b)


def _rowwise(name, fn, rows, vecs, row_outs, vec_outs, *, tm=256):
    nrows = rows[0][0].shape[0]
    tm = _row_tile(nrows, tm)
    n_r, n_v, n_ro, n_vo = len(rows), len(vecs), len(row_outs), len(vec_outs)

    def body(*refs):
        r_vals = [r[...] for r in refs[:n_r]]
        v_vals = [r[...] for r in refs[n_r:n_r + n_v]]
        ro_refs = refs[n_r + n_v:n_r + n_v + n_ro]
        vo_refs = refs[n_r + n_v + n_ro:]
        ro, vo = fn(r_vals, v_vals)
        for ref, val in zip(ro_refs, ro):
            ref[...] = val.astype(ref.dtype)
        if n_vo:
            @pl.when(pl.program_id(0) == 0)
            def _():
                for ref in vo_refs:
                    ref[...] = jnp.zeros_like(ref)

            for ref, val in zip(vo_refs, vo):
                ref[...] += val

    in_specs = [pl.BlockSpec((tm, cols), functools.partial(lambda i, cb: (i, cb), cb=cb)) for _, cols, cb in rows]
    in_specs += [pl.BlockSpec(v.shape, functools.partial(lambda i, nd: (0,) * nd, nd=v.ndim)) for v in vecs]
    out_specs = [pl.BlockSpec((tm, cols), lambda i: (i, 0)) for cols, _ in row_outs]
    out_specs += [pl.BlockSpec((1, cols), lambda i: (0, 0)) for cols in vec_outs]
    out_shape = [jax.ShapeDtypeStruct((nrows, cols), dt) for cols, dt in row_outs]
    out_shape += [jax.ShapeDtypeStruct((1, cols), F32) for cols in vec_outs]
    return pl.pallas_call(
        body, name=name, grid=(nrows // tm,), in_specs=in_specs, out_specs=out_specs, out_shape=out_shape,
        compiler_params=_cparams(1),
    )(*[r[0] for r in rows], *vecs)


def _whole(x):
    return (x, x.shape[1], 0)


def _colsum(x):
    return jnp.sum(x, axis=0, keepdims=True)


def _rstd(x):
    return lax.rsqrt(jnp.mean(x * x, axis=-1, keepdims=True) + EPS)


def _rms_bwd(dn, x, g):
    r = _rstd(x)
    c = dn * g
    dx = r * c - x * (r * r * r) * jnp.mean(c * x, axis=-1, keepdims=True)
    return dx, _colsum(dn * x * r)


def _rms_fwd(name, x, g):
    def fn(r, v):
        (xv,), (gv,) = r, v
        return [xv * _rstd(xv) * gv], []

    return _rowwise(name, fn, [_whole(x)], [g], [(x.shape[1], BF16)], [])[0]


def _ep_post_res_pre(scale):
    def epilogue(acc, rows, vecs):
        (resid,), (g_post, g_next) = rows, vecs
        h = resid + scale * (acc * _rstd(acc) * g_post)
        return [acc, h, h * _rstd(h) * g_next], []

    return epilogue


def _post_bwd(dh, f, g_post, scale):
    return _rms_bwd(scale * dh, f, g_post)


def _ep_loss(scale, d):
    def epilogue(acc, rows, vecs):
        (resid, target), (g_post,) = rows, vecs
        err = resid + scale * (acc * _rstd(acc) * g_post) - target
        dy = err * (1.0 / d)
        df, dg_post = _post_bwd(dy, acc, g_post, scale)
        return [dy, df], [_colsum(err * err), dg_post]

    return epilogue


def _ep_pre_bwd_post(scale_prev):
    def epilogue(acc, rows, vecs):
        (h, dh_up, f_prev), (g_pre, g_post_prev) = rows, vecs
        dx, dg_pre = _rms_bwd(acc, h, g_pre)
        dh = dh_up + dx
        df, dg_post = _post_bwd(dh, f_prev, g_post_prev, scale_prev)
        return [dh, df], [dg_pre, dg_post]

    return epilogue


def _ep_pre_bwd_first():
    def epilogue(acc, rows, vecs):
        (x, dh_up), (g_pre,) = rows, vecs
        dx, dg_pre = _rms_bwd(acc, x, g_pre)
        return [dh_up + dx], [dg_pre]

    return epilogue


def _gate_specs(d, tm):
    first = (3 * D_ATTN + 2 * D_CONV) // COL
    return [pl.BlockSpec((tm, COL), functools.partial(lambda i, cb: (i, cb), cb=first + p)) for p in range(2 * d // COL)]


def _gate(piece_refs, bias_ref, c0, width):
    p, off = divmod(c0, COL)
    return jax.nn.sigmoid(piece_refs[p][:, off:off + width].astype(F32) + bias_ref[:, c0:c0 + width])


def _resident(w):
    return pl.BlockSpec(w.shape, functools.partial(lambda i, nd: (0,) * nd, nd=w.ndim), pipeline_mode=pl.Buffered(1))


def _mix_merge(att, cs, wao, wco, proj, gate_bias, tm=512):
    t = att.shape[0]
    nj, _, nb = wao.shape
    d = nj * nb
    tm = _row_tile(t, tm)
    gate_specs = _gate_specs(d, tm)
    n_p = len(gate_specs)

    def body(att_ref, cs_ref, wao_ref, wco_ref, *rest):
        pieces, (gb_ref, ya_ref, yb_ref, m_ref) = rest[:n_p], rest[n_p:]
        av, cv = att_ref[...], cs_ref[...]
        for j in range(nj):
            cols = slice(j * nb, (j + 1) * nb)
            ya = _dot(av, wao_ref[j], False)
            yb = _dot(cv, wco_ref[j], False)
            merged = _gate(pieces, gb_ref, j * nb, nb) * ya + _gate(pieces, gb_ref, d + j * nb, nb) * yb
            ya_ref[:, cols] = ya.astype(ya_ref.dtype)
            yb_ref[:, cols] = yb.astype(yb_ref.dtype)
            m_ref[:, cols] = merged.astype(m_ref.dtype)

    row = lambda x: pl.BlockSpec((tm, x.shape[1]), lambda i: (i, 0))
    out_spec = pl.BlockSpec((tm, d), lambda i: (i, 0))
    return pl.pallas_call(
        body, name="mix_merge", grid=(t // tm,),
        in_specs=[row(att), row(cs), _resident(wao), _resident(wco)] + gate_specs + [_resident(gate_bias)],
        out_specs=[out_spec] * 3, out_shape=[jax.ShapeDtypeStruct((t, d), BF16)] * 3, compiler_params=_cparams(1),
    )(att, cs, wao, wco, *([proj] * n_p), gate_bias)


def _mix_d_merge(dmo, wout, y_a, y_b, wao, wco, proj, gate_bias, tm=512):
    t, d = dmo.shape
    nj, _, nb = wao.shape
    ka, kc = wao.shape[1], wco.shape[1]
    tm = _row_tile(t, tm)
    gate_specs = _gate_specs(d, tm)
    n_p = len(gate_specs)

    def body(dmo_ref, wout_ref, ya_ref, yb_ref, wao_ref, wco_ref, *rest):
        pieces, (gb_ref, dya_ref, dyb_ref, dg_ref, datt_ref, dcs_ref, dgb_ref) = rest[:n_p], rest[n_p:]

        @pl.when(pl.program_id(0) == 0)
        def _():
            dgb_ref[...] = jnp.zeros_like(dgb_ref)

        dmo_v = dmo_ref[...]
        datt = dcs = None
        for j in range(nj):
            cols, cols_b = slice(j * nb, (j + 1) * nb), slice(d + j * nb, d + (j + 1) * nb)
            dm = _dot(dmo_v, wout_ref[j], True)
            ga, gb = _gate(pieces, gb_ref, j * nb, nb), _gate(pieces, gb_ref, d + j * nb, nb)
            dya, dyb = (dm * ga).astype(BF16), (dm * gb).astype(BF16)
            dga = dm * ya_ref[:, cols].astype(F32) * ga * (1.0 - ga)
            dgb = dm * yb_ref[:, cols].astype(F32) * gb * (1.0 - gb)
            dya_ref[:, cols], dyb_ref[:, cols] = dya, dyb
            dg_ref[:, cols], dg_ref[:, cols_b] = dga.astype(dg_ref.dtype), dgb.astype(dg_ref.dtype)
            dgb_ref[:, cols] += _colsum(dga)
            dgb_ref[:, cols_b] += _colsum(dgb)
            pa, pc = _dot(dya, wao_ref[j], True), _dot(dyb, wco_ref[j], True)
            datt, dcs = (pa, pc) if datt is None else (datt + pa, dcs + pc)
        datt_ref[...] = datt.astype(datt_ref.dtype)
        dcs_ref[...] = dcs.astype(dcs_ref.dtype)

    row = lambda cols: pl.BlockSpec((tm, cols), lambda i: (i, 0))
    return pl.pallas_call(
        body, name="mix_d_merge", grid=(t // tm,),
        in_specs=[row(d), _resident(wout), row(d), row(d), _resident(wao), _resident(wco)] + gate_specs
        + [_resident(gate_bias)],
        out_specs=[row(d), row(d), row(2 * d), row(ka), row(kc), pl.BlockSpec((1, 2 * d), lambda i: (0, 0))],
        out_shape=[jax.ShapeDtypeStruct((t, d), BF16), jax.ShapeDtypeStruct((t, d), BF16),
                   jax.ShapeDtypeStruct((t, 2 * d), BF16), jax.ShapeDtypeStruct((t, ka), BF16),
                   jax.ShapeDtypeStruct((t, kc), F32), jax.ShapeDtypeStruct((1, 2 * d), F32)],
        compiler_params=_cparams(1),
    )(dmo, wout, y_a, y_b, wao, wco, *([proj] * n_p), gate_bias)


def _adamw_math(wv, gv, mv, vv):
    m2 = ADAM_B1 * mv + (1.0 - ADAM_B1) * gv
    v2 = ADAM_B2 * vv + (1.0 - ADAM_B2) * (gv * gv)
    m_hat = m2 / (1.0 - ADAM_B1 ** ADAM_STEP)
    v_hat = v2 / (1.0 - ADAM_B2 ** ADAM_STEP)
    delta = -ADAM_LR * (m_hat / (jnp.sqrt(v_hat) + ADAM_EPS) + ADAM_WD * wv)
    return delta, m2, v2


def _adamw(name, w, g, m, v):
    def fn(r, _):
        return list(_adamw_math(*r)), []

    c = w.shape[1]
    return _rowwise(name, fn, [_whole(w), _whole(g), _whole(m), _whole(v)], [], [(c, F32)] * 3, [], tm=256)


POS_C, POS_CHIP, POS_PEER = 0, 1, 2


def _placed_call(body, name, pos, grid, in_specs, out_specs, out_shape, args):
    return pl.pallas_call(
        body, name=name, out_shape=out_shape, compiler_params=_cparams(len(grid)),
        grid_spec=pltpu.PrefetchScalarGridSpec(num_scalar_prefetch=1, grid=grid, in_specs=in_specs,
                                               out_specs=out_specs),
    )(pos, *args)


def _cast_into(name, pos, w):
    r, cols = w.shape
    tm = _row_tile(r, 256)

    def body(pos_ref, w_ref, o_ref):
        o_ref[...] = w_ref[...].astype(o_ref.dtype)

    return _placed_call(
        body, name, pos, (r // tm,), [pl.BlockSpec((tm, cols), lambda i, pos: (i, 0))],
        pl.BlockSpec((None, tm, cols), lambda i, pos: (pos[POS_CHIP], i, 0)),
        jax.ShapeDtypeStruct((N_CHIPS, r, cols), BF16), [w])


def _add_pair(name, pos, grad, landed):
    nj, half, cols = landed.shape
    tm = _row_tile(half, 256)
    nb = half // tm

    def body(pos_ref, g_ref, l_ref, o_ref):
        o_ref[...] = (g_ref[...].astype(F32) + l_ref[...].astype(F32)).astype(o_ref.dtype)

    spec = pl.BlockSpec((None, tm, cols), lambda j, i, pos: (j, i, 0))
    return _placed_call(
        body, name, pos, (nj, nb),
        [pl.BlockSpec((None, tm, cols), lambda j, i, pos: (j, pos[POS_C] * nb + i, 0)), spec], spec,
        jax.ShapeDtypeStruct(landed.shape, BF16), [grad, landed])


def _add_chips(name, pos, part, landed):
    _, half, cols = landed.shape
    tm = _row_tile(half, 256)

    def body(pos_ref, p_ref, l0_ref, l1_ref, l2_ref, o_ref):
        acc = p_ref[...].astype(F32)
        for ref in (l0_ref, l1_ref, l2_ref):
            acc = acc + ref[...].astype(F32)
        o_ref[...] = acc

    slot = lambda at: pl.BlockSpec((None, tm, cols), functools.partial(lambda i, pos, at: (pos[at], i, 0), at=at))
    return _placed_call(
        body, name, pos, (half // tm,), [slot(POS_CHIP)] + [slot(POS_PEER + k) for k in range(3)],
        pl.BlockSpec((tm, cols), lambda i, pos: (i, 0)), jax.ShapeDtypeStruct((half, cols), F32),
        [part, landed, landed, landed])


def _adamw_halves(name, pos, w, m, v, own, landed):
    r, cols = w.shape
    half = own.shape[0]
    tm = _row_tile(half, 256)
    nb = half // tm

    def body(pos_ref, w_ref, m_ref, v_ref, own_ref, land_ref, g_out, d_out, m_out, v_out):
        mine = pl.program_id(0) == pos_ref[POS_C]
        g = jnp.where(mine, own_ref[...], land_ref[...])
        delta, m2, v2 = _adamw_math(w_ref[...], g, m_ref[...], v_ref[...])
        g_out[...] = g
        d_out[...] = delta
        m_out[...] = m2
        v_out[...] = v2

    full = pl.BlockSpec((tm, cols), lambda h, i, pos: (h * nb + i, 0))
    part = pl.BlockSpec((tm, cols), lambda h, i, pos: (i, 0))
    return _placed_call(
        body, name, pos, (2, nb), [full, full, full, part, part], [full] * 4,
        [jax.ShapeDtypeStruct((r, cols), F32)] * 4, [w, m, v, own, landed])


def _rel_onehot():
    e = np.arange(REL_EXT)
    dist = K_PAD - (e - (Q_BLOCK - 1))
    idx = np.clip(dist, -REL_CLIP, REL_CLIP) + REL_CLIP
    return (np.arange(REL_PAD)[:, None] == idx[None, :]).astype(np.float32)


def _band_valid():
    qc = lax.broadcasted_iota(jnp.int32, (Q_BLOCK, K_WIN), 0) // CHUNK
    kc = lax.broadcasted_iota(jnp.int32, (Q_BLOCK, K_WIN), 1) // CHUNK
    return (kc >= qc) & (kc <= qc + LEFT_CHUNKS)


def _skew(x, left):
    row = lax.broadcasted_iota(jnp.int32, x.shape, 0)
    for bit in range(Q_BLOCK.bit_length() - 1):
        amount = 1 << bit
        rolled = pltpu.roll(x, REL_EXT - amount if left else amount, 1)
        x = jnp.where((row >> bit) & 1 == 1, rolled, x)
    return x


def _bias_expand(table_pad):
    onehot = jnp.asarray(_rel_onehot())

    def body(t_ref, oh_ref, o_ref):
        ext = jnp.dot(t_ref[...], oh_ref[...], precision=lax.Precision.HIGHEST, preferred_element_type=F32)
        valid = _band_valid()
        for h in range(N_HEADS):
            rows = jnp.broadcast_to(ext[h:h + 1, :], (Q_BLOCK, REL_EXT))
            rolled = _skew(pltpu.roll(rows, REL_EXT - (Q_BLOCK - 1), 1), left=False)
            o_ref[h] = jnp.where(valid, rolled[:, :K_WIN], NEG)

    return pl.pallas_call(
        body, name="bias_expand", out_shape=jax.ShapeDtypeStruct((N_HEADS, Q_BLOCK, K_WIN), F32),
        compiler_params=pltpu.CompilerParams(vmem_limit_bytes=VMEM_LIMIT_BYTES),
    )(table_pad, onehot)


def _bias_fold(dbias):
    onehot_t = jnp.asarray(_rel_onehot().T)

    def body(d_ref, oh_ref, o_ref, ext_ref):
        for h in range(N_HEADS):
            x = jnp.concatenate([d_ref[h], jnp.zeros((Q_BLOCK, REL_EXT - K_WIN), F32)], axis=1)
            rolled = _skew(pltpu.roll(x, Q_BLOCK - 1, 1), left=True)
            ext_ref[h:h + 1, :] = jnp.sum(rolled, axis=0, keepdims=True)
        o_ref[...] = jnp.dot(ext_ref[...], oh_ref[...], precision=lax.Precision.HIGHEST,
                             preferred_element_type=F32)

    return pl.pallas_call(
        body, name="bias_fold", out_shape=jax.ShapeDtypeStruct((N_HEADS, REL_PAD), F32),
        scratch_shapes=[pltpu.VMEM((N_HEADS, REL_EXT), F32)],
        compiler_params=pltpu.CompilerParams(vmem_limit_bytes=VMEM_LIMIT_BYTES),
    )(dbias, onehot_t)


def _scores(q, k, bias, i):
    s = lax.dot_general(q, k, (((1,), (1,)), ((), ())), preferred_element_type=F32) * (HEAD_DIM ** -0.5) + bias
    kpos = lax.broadcasted_iota(jnp.int32, (Q_BLOCK, K_WIN), 1) + i * Q_BLOCK
    return jnp.where(kpos >= K_PAD, s, NEG)


def _attn_specs(n_kv):
    q_spec = pl.BlockSpec((Q_BLOCK, 2 * HEAD_DIM), lambda p, i: (i, p))
    kv_specs = [pl.BlockSpec((Q_BLOCK, 2 * HEAD_DIM), functools.partial(lambda p, i, kk: (i + kk, p), kk=kk))
                for _ in range(n_kv) for kk in range(K_WIN // Q_BLOCK)]
    bias_spec = pl.BlockSpec((2, Q_BLOCK, K_WIN), lambda p, i: (p, 0, 0))
    return q_spec, kv_specs, bias_spec


def _attn_fwd(proj, kp, vp, bias):
    t = proj.shape[0]
    n_win = K_WIN // Q_BLOCK

    def body(q_ref, *refs):
        k_refs, v_refs = refs[:n_win], refs[n_win:2 * n_win]
        b_ref, o_ref, lse_ref = refs[2 * n_win:]
        i = pl.program_id(1)
        k = jnp.concatenate([r[...] for r in k_refs], axis=0)
        v = jnp.concatenate([r[...] for r in v_refs], axis=0)
        q = q_ref[...]
        for hh in range(2):
            lanes = slice(hh * HEAD_DIM, (hh + 1) * HEAD_DIM)
            s = _scores(q[:, lanes], k[:, lanes], b_ref[hh], i)
            m = jnp.max(s, axis=1, keepdims=True)
            p = jnp.exp(s - m)
            l = jnp.sum(p, axis=1, keepdims=True)
            o = jnp.dot(p.astype(BF16), v[:, lanes], preferred_element_type=F32) / l
            o_ref[:, lanes] = o.astype(o_ref.dtype)
            lse_ref[:, lanes] = jnp.broadcast_to(m + jnp.log(l), (Q_BLOCK, HEAD_DIM))

    q_spec, kv_specs, bias_spec = _attn_specs(2)
    out_spec = pl.BlockSpec((Q_BLOCK, 2 * HEAD_DIM), lambda p, i: (i, p))
    return pl.pallas_call(
        body, name="attn_fwd", grid=(N_HEADS // 2, t // Q_BLOCK),
        in_specs=[q_spec] + kv_specs + [bias_spec], out_specs=[out_spec, out_spec],
        out_shape=[jax.ShapeDtypeStruct((t, D_ATTN), BF16), jax.ShapeDtypeStruct((t, D_ATTN), F32)],
        compiler_params=_cparams(2),
    )(proj, *([kp] * n_win), *([vp] * n_win), bias)


def _attn_bwd(proj, kp, vp, bias, att, lse, datt):
    t = proj.shape[0]
    n_win = K_WIN // Q_BLOCK

    def body(q_ref, *refs):
        k_refs, v_refs = refs[:n_win], refs[n_win:2 * n_win]
        b_ref, o_ref, lse_ref, do_ref, dq_ref, dk_ref, dv_ref, db_ref = refs[2 * n_win:]
        i = pl.program_id(1)

        @pl.when(i == 0)
        def _():
            dk_ref[...] = jnp.zeros_like(dk_ref)
            dv_ref[...] = jnp.zeros_like(dv_ref)
            db_ref[...] = jnp.zeros_like(db_ref)

        k = jnp.concatenate([r[...] for r in k_refs], axis=0)
        v = jnp.concatenate([r[...] for r in v_refs], axis=0)
        q = q_ref[...]
        do = do_ref[...]
        rows = pl.ds(pl.multiple_of(i * Q_BLOCK, Q_BLOCK), K_WIN)
        scale = HEAD_DIM ** -0.5
        for hh in range(2):
            lanes = slice(hh * HEAD_DIM, (hh + 1) * HEAD_DIM)
            qh, kh, vh, doh = q[:, lanes], k[:, lanes], v[:, lanes], do[:, lanes]
            s = _scores(qh, kh, b_ref[hh], i)
            p = jnp.exp(s - lse_ref[:, lanes][:, :1])
            dp = lax.dot_general(doh, vh, (((1,), (1,)), ((), ())), preferred_element_type=F32)
            delta = jnp.sum(doh.astype(F32) * o_ref[:, lanes].astype(F32), axis=1, keepdims=True)
            ds = p * (dp - delta)
            db_ref[hh] += ds
            dsb = ds.astype(BF16)
            dq_ref[:, lanes] = (jnp.dot(dsb, kh, preferred_element_type=F32) * scale).astype(dq_ref.dtype)
            dk_ref[rows, lanes] += lax.dot_general(dsb, qh, (((0,), (0,)), ((), ())),
                                                   preferred_element_type=F32) * scale
            dv_ref[rows, lanes] += lax.dot_general(p.astype(BF16), doh, (((0,), (0,)), ((), ())),
                                                   preferred_element_type=F32)

    q_spec, kv_specs, bias_spec = _attn_specs(2)
    row_spec = pl.BlockSpec((Q_BLOCK, 2 * HEAD_DIM), lambda p, i: (i, p))
    full_spec = pl.BlockSpec((t + K_PAD, 2 * HEAD_DIM), lambda p, i: (0, p))
    return pl.pallas_call(
        body, name="attn_bwd", grid=(N_HEADS // 2, t // Q_BLOCK),
        in_specs=[q_spec] + kv_specs + [bias_spec, row_spec, row_spec, row_spec],
        out_specs=[row_spec, full_spec, full_spec, bias_spec],
        out_shape=[jax.ShapeDtypeStruct((t, D_ATTN), BF16), jax.ShapeDtypeStruct((t + K_PAD, D_ATTN), F32),
                   jax.ShapeDtypeStruct((t + K_PAD, D_ATTN), F32),
                   jax.ShapeDtypeStruct((N_HEADS, Q_BLOCK, K_WIN), F32)],
        compiler_params=_cparams(2),
    )(proj, *([kp] * n_win), *([vp] * n_win), bias, att, lse, datt)


CONV_LEAD = CONV_HALO - (CONV_WIDTH - 1)
CONV_LANES = 128
CONV_ROWS = 64


def _conv_specs(t):
    tt = _row_tile(t, CONV_TILE)
    per = tt // CONV_HALO
    n_halo = t // CONV_HALO
    tile = lambda cb: pl.BlockSpec((tt, COL), functools.partial(lambda i, cb: (i, cb), cb=cb))
    prev = lambda cb: pl.BlockSpec((CONV_HALO, COL),
                                   functools.partial(lambda i, cb: (jnp.maximum(i * per - 1, 0), cb), cb=cb))
    nxt = lambda cb: pl.BlockSpec((CONV_HALO, COL),
                                  functools.partial(lambda i, cb: (jnp.minimum((i + 1) * per, n_halo - 1), cb), cb=cb))
    vec = lambda shape: pl.BlockSpec(shape, lambda i: (0, 0))
    return tt, tile, prev, nxt, vec


def _glu(ca, cg, bias):
    return (ca.astype(F32) + bias[:, :D_CONV]) * jax.nn.sigmoid(cg.astype(F32) + bias[:, D_CONV:])


SUBLANES = 8


def _shift_copies(ext_ref):
    n = ext_ref.shape[1] - SUBLANES
    for s in range(1, SUBLANES):
        ext_ref[s, 0:n, :] = ext_ref[0, s:s + n, :]


def _window(ext_ref, offset, rows, lanes=slice(None)):
    s = offset % SUBLANES
    return ext_ref[s, offset - s:offset - s + rows, lanes]


def _taps(ext_ref, tt, first_row, weight_of, out_ref):
    for r0 in range(0, tt, CONV_ROWS):
        for l0 in range(0, D_CONV, CONV_LANES):
            lanes = slice(l0, l0 + CONV_LANES)
            acc = jnp.zeros((CONV_ROWS, CONV_LANES), F32)
            for w in range(CONV_WIDTH):
                acc = acc + _window(ext_ref, first_row(w) + r0, CONV_ROWS, lanes) * weight_of(w)[:, lanes]
            out_ref[r0:r0 + CONV_ROWS, lanes] = acc


def _conv_fwd(proj, glu_bias, dw, dw_b, ln_g, ln_b):
    t = proj.shape[0]
    tt, tile, prev, nxt, vec = _conv_specs(t)
    ca_blk, cg_blk = 3 * D_ATTN // COL, 3 * D_ATTN // COL + 1

    def body(ca_ref, cg_ref, pa_ref, pg_ref, gb_ref, dw_ref, dwb_ref, g_ref, b_ref, cs_ref, c_ref, z_ref, ext_ref):
        i = pl.program_id(0)
        bias = gb_ref[...]
        c = _glu(ca_ref[...], cg_ref[...], bias)
        halo = _glu(pa_ref[...], pg_ref[...], bias)
        ext_ref[0, 0:CONV_HALO, :] = jnp.where(i == 0, 0.0, halo)
        ext_ref[0, CONV_HALO:, :] = c
        _shift_copies(ext_ref)
        c_ref[...] = c
        _taps(ext_ref, tt, lambda w: CONV_LEAD + w, lambda w: dw_ref[w:w + 1, :], z_ref)
        z = z_ref[...] + dwb_ref[...]
        z_ref[...] = z
        mu = jnp.mean(z, axis=-1, keepdims=True)
        zc = z - mu
        y = zc * lax.rsqrt(jnp.mean(zc * zc, axis=-1, keepdims=True) + EPS) * g_ref[...] + b_ref[...]
        cs_ref[...] = (y * jax.nn.sigmoid(y)).astype(cs_ref.dtype)

    out_spec = pl.BlockSpec((tt, D_CONV), lambda i: (i, 0))
    return pl.pallas_call(
        body, name="conv_fwd", grid=(t // tt,),
        in_specs=[tile(ca_blk), tile(cg_blk), prev(ca_blk), prev(cg_blk), vec(glu_bias.shape), vec(dw.shape),
                  vec(dw_b.shape), vec(ln_g.shape), vec(ln_b.shape)],
        out_specs=[out_spec] * 3,
        out_shape=[jax.ShapeDtypeStruct((t, D_CONV), BF16), jax.ShapeDtypeStruct((t, D_CONV), F32),
                   jax.ShapeDtypeStruct((t, D_CONV), F32)],
        scratch_shapes=[pltpu.VMEM((SUBLANES, tt + CONV_HALO, D_CONV), F32)], compiler_params=_cparams(1),
    )(proj, proj, proj, proj, glu_bias, dw, dw_b, ln_g, ln_b)


def _conv_bwd(proj, c, z, dcs, glu_bias, dw, ln_g, ln_b):
    t = proj.shape[0]
    tt, tile, prev, nxt, vec = _conv_specs(t)
    n_tiles = t // tt
    ca_blk, cg_blk = 3 * D_ATTN // COL, 3 * D_ATTN // COL + 1

    def ln_bwd(zv, dcsv, g, b):
        mu = jnp.mean(zv, axis=-1, keepdims=True)
        zc = zv - mu
        rstd = lax.rsqrt(jnp.mean(zc * zc, axis=-1, keepdims=True) + EPS)
        zhat = zc * rstd
        y = zhat * g + b
        sig = jax.nn.sigmoid(y)
        dy = dcsv * sig * (1.0 + y * (1.0 - sig))
        dzh = dy * g
        dz = rstd * (dzh - jnp.mean(dzh, axis=-1, keepdims=True) - zhat * jnp.mean(dzh * zhat, axis=-1, keepdims=True))
        return dz, dy, zhat

    def body(ca_ref, cg_ref, c_ref, cprev_ref, z_ref, znext_ref, dcs_ref, dcsnext_ref, gb_ref, dw_ref, g_ref, b_ref,
             dcin_ref, ddw_ref, ddwb_ref, dg_ref, db_ref, dgb_ref, cext_ref, dzext_ref, dc_ref):
        i = pl.program_id(0)

        @pl.when(i == 0)
        def _():
            for ref in (ddw_ref, ddwb_ref, dg_ref, db_ref, dgb_ref):
                ref[...] = jnp.zeros_like(ref)

        g, b = g_ref[...], b_ref[...]
        dz, dy, zhat = ln_bwd(z_ref[...], dcs_ref[...], g, b)
        dz_next, _, _ = ln_bwd(znext_ref[...], dcsnext_ref[...], g, b)
        dg_ref[...] += _colsum(dy * zhat)
        db_ref[...] += _colsum(dy)
        ddwb_ref[...] += _colsum(dz)
        dzext_ref[0, 0:tt, :] = dz
        dzext_ref[0, tt:, :] = jnp.where(i == n_tiles - 1, 0.0, dz_next)
        _shift_copies(dzext_ref)
        cext_ref[0, 0:CONV_HALO, :] = jnp.where(i == 0, 0.0, cprev_ref[...])
        cext_ref[0, CONV_HALO:, :] = c_ref[...]
        _shift_copies(cext_ref)
        _taps(dzext_ref, tt, lambda w: CONV_WIDTH - 1 - w, lambda w: dw_ref[w:w + 1, :], dc_ref)
        for w in range(CONV_WIDTH):
            ddw_ref[w:w + 1, :] += _colsum(_window(cext_ref, CONV_LEAD + w, tt) * dz)
        bias = gb_ref[...]
        a_in = ca_ref[...].astype(F32) + bias[:, :D_CONV]
        sg = jax.nn.sigmoid(cg_ref[...].astype(F32) + bias[:, D_CONV:])
        dc = dc_ref[...]
        dcin = jnp.concatenate([dc * sg, dc * a_in * sg * (1.0 - sg)], axis=1)
        dcin_ref[...] = dcin.astype(dcin_ref.dtype)
        dgb_ref[...] += _colsum(dcin)

    row = lambda: pl.BlockSpec((tt, D_CONV), lambda i: (i, 0))
    per = tt // CONV_HALO
    n_halo = t // CONV_HALO
    prev_row = pl.BlockSpec((CONV_HALO, D_CONV), lambda i: (jnp.maximum(i * per - 1, 0), 0))
    next_row = lambda: pl.BlockSpec((CONV_HALO, D_CONV), lambda i: (jnp.minimum((i + 1) * per, n_halo - 1), 0))
    acc = lambda shape: pl.BlockSpec(shape, lambda i: (0, 0))
    return pl.pallas_call(
        body, name="conv_bwd", grid=(n_tiles,),
        in_specs=[tile(ca_blk), tile(cg_blk), row(), prev_row, row(), next_row(), row(), next_row(),
                  vec(glu_bias.shape), vec(dw.shape), vec(ln_g.shape), vec(ln_b.shape)],
        out_specs=[pl.BlockSpec((tt, 2 * D_CONV), lambda i: (i, 0)), acc(dw.shape), acc((1, D_CONV)),
                   acc((1, D_CONV)), acc((1, D_CONV)), acc((1, 2 * D_CONV))],
        out_shape=[jax.ShapeDtypeStruct((t, 2 * D_CONV), BF16), jax.ShapeDtypeStruct(dw.shape, F32),
                   jax.ShapeDtypeStruct((1, D_CONV), F32), jax.ShapeDtypeStruct((1, D_CONV), F32),
                   jax.ShapeDtypeStruct((1, D_CONV), F32), jax.ShapeDtypeStruct((1, 2 * D_CONV), F32)],
        scratch_shapes=[pltpu.VMEM((SUBLANES, tt + CONV_HALO, D_CONV), F32),
                        pltpu.VMEM((SUBLANES, tt + CONV_HALO, D_CONV), F32), pltpu.VMEM((tt, D_CONV), F32)],
        compiler_params=_cparams(1),
    )(proj, proj, c, c, z, z, dcs, dcs, glu_bias, dw, ln_g, ln_b)


def _place():
    x, y, c = lax.axis_index("x"), lax.axis_index("y"), lax.axis_index("c")
    chips = [(1 - x, y), (x, 1 - y), (1 - x, 1 - y)]
    return x, y, c, chips


def _chip_index(chip):
    return 2 * chip[0] + chip[1]


def _half_rows(c, half):
    return pl.ds(pl.multiple_of(c * half, 16), half)


def _run_carry(name, carry):
    c_in, c_out = len(carry.ins), len(carry.outs)

    def body(*refs):
        ins, outs, sems = refs[:c_in], refs[c_in:c_in + c_out], refs[c_in + c_out:]
        for when in ("first", "late", "last"):
            for w, fn in carry.phases:
                if w == when:
                    fn(ins, outs, sems)

    return pl.pallas_call(
        body, name=name, in_specs=[ANY] * c_in, out_specs=[ANY] * c_out, out_shape=list(carry.outs),
        scratch_shapes=list(carry.sems), input_output_aliases=dict(carry.aliases),
    )(*carry.ins)


def _gather_carry(blocked):
    n = len(blocked)

    def over_ici(o_refs, send_sems, recv_sems):
        x, y, c, chips = _place()
        me = _chip_index((x, y))
        copies = []
        for a in range(n):
            mine = o_refs[a].at[me, _half_rows(c, o_refs[a].shape[1] // 2), :]
            for k, chip in enumerate(chips):
                copies.append(pltpu.make_async_remote_copy(
                    src_ref=mine, dst_ref=mine, send_sem=send_sems.at[6 * a + k], recv_sem=recv_sems.at[6 * a + k],
                    device_id=(chip[0], chip[1], c), device_id_type=MESH))
        return copies

    def to_sibling(o_refs, send_sems, recv_sems, sent_by_me):
        x, y, c, chips = _place()
        copies = []
        for a in range(n):
            rows = _half_rows(c if sent_by_me else 1 - c, o_refs[a].shape[1] // 2)
            for k, chip in enumerate(chips):
                landed = o_refs[a].at[_chip_index(chip), rows, :]
                copies.append(pltpu.make_async_remote_copy(
                    src_ref=landed, dst_ref=landed, send_sem=send_sems.at[6 * a + 3 + k],
                    recv_sem=recv_sems.at[6 * a + 3 + k], device_id=(x, y, 1 - c), device_id_type=MESH))
        return copies

    def start(ins, outs, sems):
        for cp in over_ici(outs, *sems):
            cp.start()

    def hand_on(ins, outs, sems):
        for arrived, onward in zip(over_ici(outs, *sems), to_sibling(outs, *sems, True)):
            arrived.wait_recv()
            onward.start()

    def finish(ins, outs, sems):
        for cp in to_sibling(outs, *sems, False):
            cp.wait_recv()
        for cp in over_ici(outs, *sems) + to_sibling(outs, *sems, True):
            cp.wait_send()

    return _Carry(
        ins=list(blocked), outs=[jax.ShapeDtypeStruct(w.shape, w.dtype) for w in blocked],
        aliases={a: a for a in range(n)},
        sems=[pltpu.SemaphoreType.DMA((6 * n,)), pltpu.SemaphoreType.DMA((6 * n,))],
        phases=[("first", start), ("late", hand_on), ("last", finish)])


def _pair_exchange(name, grads):
    n = len(grads)

    def body(*refs):
        g_refs, land_refs = refs[:n], refs[n:2 * n]
        send_sems, recv_sems = refs[2 * n:]
        x, y, c, _ = _place()
        copies = []
        for a in range(n):
            half = g_refs[a].shape[1] // 2
            cp = pltpu.make_async_remote_copy(
                src_ref=g_refs[a].at[:, _half_rows(1 - c, half), :], dst_ref=land_refs[a],
                send_sem=send_sems.at[a], recv_sem=recv_sems.at[a], device_id=(x, y, 1 - c), device_id_type=MESH)
            cp.start()
            copies.append(cp)
        for cp in copies:
            cp.wait()

    return pl.pallas_call(
        body, name=name, in_specs=[ANY] * n, out_specs=[ANY] * n,
        out_shape=[jax.ShapeDtypeStruct((g.shape[0], g.shape[1] // 2, g.shape[2]), g.dtype) for g in grads],
        scratch_shapes=[pltpu.SemaphoreType.DMA((n,)), pltpu.SemaphoreType.DMA((n,))],
    )(*grads)


def _to_owner_carry(parts):
    n = len(parts)

    def sends(p_refs, l_refs, send_sems, recv_sems):
        x, y, c, chips = _place()
        me = _chip_index((x, y))
        return [pltpu.make_async_remote_copy(
            src_ref=p_refs[a].at[_chip_index(chip)], dst_ref=l_refs[a].at[me],
            send_sem=send_sems.at[3 * a + k], recv_sem=recv_sems.at[3 * a + k],
            device_id=(chip[0], chip[1], c), device_id_type=MESH) for a in range(n) for k, chip in enumerate(chips)]

    def start(ins, outs, sems):
        for cp in sends(ins, outs, *sems):
            cp.start()

    def finish(ins, outs, sems):
        x, y, c, chips = _place()
        send_sems, recv_sems = sems
        for a in range(n):
            for k, chip in enumerate(chips):
                slot = outs[a].at[_chip_index(chip)]
                pltpu.make_async_remote_copy(
                    src_ref=slot, dst_ref=slot, send_sem=send_sems.at[3 * a + k], recv_sem=recv_sems.at[3 * a + k],
                    device_id=(chip[0], chip[1], c), device_id_type=MESH).wait_recv()
        for cp in sends(ins, outs, *sems):
            cp.wait_send()

    return _Carry(
        ins=list(parts), outs=[jax.ShapeDtypeStruct(p.shape, p.dtype) for p in parts], aliases={},
        sems=[pltpu.SemaphoreType.DMA((3 * n,)), pltpu.SemaphoreType.DMA((3 * n,))],
        phases=[("first", start), ("last", finish)])


def _swap_halves(halves):
    n = len(halves)

    def body(*refs):
        h_refs, o_refs = refs[:n], refs[n:2 * n]
        send_sems, recv_sems = refs[2 * n:]
        x, y, c, _ = _place()
        copies = []
        for a in range(n):
            cp = pltpu.make_async_remote_copy(
                src_ref=h_refs[a], dst_ref=o_refs[a], send_sem=send_sems.at[a], recv_sem=recv_sems.at[a],
                device_id=(x, y, 1 - c), device_id_type=MESH)
            cp.start()
            copies.append(cp)
        for cp in copies:
            cp.wait()

    return pl.pallas_call(
        body, name="grad_swap_halves", in_specs=[ANY] * n, out_specs=[ANY] * n,
        out_shape=[jax.ShapeDtypeStruct(h.shape, h.dtype) for h in halves],
        scratch_shapes=[pltpu.SemaphoreType.DMA((n,)), pltpu.SemaphoreType.DMA((n,))],
    )(*halves)


def _all_devices(name, block):
    r, cols = block.shape

    def body(b_ref, all_ref, sum_ref, send_sems, recv_sems):
        x, y, c, _ = _place()
        me = 4 * x + 2 * y + c
        all_ref[me] = b_ref[...]
        flips = [(fx, fy, fc) for fx in (0, 1) for fy in (0, 1) for fc in (0, 1)][1:]
        copies = []
        for k, (fx, fy, fc) in enumerate(flips):
            cp = pltpu.make_async_remote_copy(
                src_ref=b_ref, dst_ref=all_ref.at[me], send_sem=send_sems.at[k], recv_sem=recv_sems.at[k],
                device_id=(x ^ fx, y ^ fy, c ^ fc), device_id_type=MESH)
            cp.start()
            copies.append(cp)
        for k, (fx, fy, fc) in enumerate(flips):
            slot = all_ref.at[4 * (x ^ fx) + 2 * (y ^ fy) + (c ^ fc)]
            pltpu.make_async_remote_copy(
                src_ref=slot, dst_ref=slot, send_sem=send_sems.at[k], recv_sem=recv_sems.at[k],
                device_id=(x ^ fx, y ^ fy, c ^ fc), device_id_type=MESH).wait_recv()
        for cp in copies:
            cp.wait_send()
        acc = all_ref[0]
        for d in range(1, N_DEV):
            acc = acc + all_ref[d]
        sum_ref[...] = acc

    vmem = pl.BlockSpec(memory_space=pltpu.VMEM)
    return pl.pallas_call(
        body, name=name, in_specs=[vmem], out_specs=[vmem, vmem],
        out_shape=[jax.ShapeDtypeStruct((N_DEV, r, cols), F32), jax.ShapeDtypeStruct((r, cols), F32)],
        scratch_shapes=[pltpu.SemaphoreType.DMA((N_DEV - 1,)), pltpu.SemaphoreType.DMA((N_DEV - 1,))],
    )(block)


PACK = 1024


def _packed_rows(shape, width):
    size, last = int(np.prod(shape)), shape[-1]
    cols = last if last <= width else width
    assert size % cols == 0
    return size // cols, cols


def _pack(vals, width=PACK):
    rows = []
    for v in vals:
        n_rows, cols = _packed_rows(v.shape, width)
        rows.append(jnp.pad(v.reshape(n_rows, cols).astype(F32), ((0, 0), (0, width - cols))))
    buf = jnp.concatenate(rows, axis=0)
    return jnp.pad(buf, ((0, (-buf.shape[0]) % 8), (0, 0)))


def _unpack(buf, shapes, width=PACK):
    out, r = [], 0
    for shape in shapes:
        n_rows, cols = _packed_rows(shape, width)
        out.append(buf[r:r + n_rows, :cols].reshape(shape))
        r += n_rows
    return out


def _ffn_hidden(name, n, wg, wu, tm=1024, carry=None):
    m, k = n.shape
    nj, fb, _ = wg.shape
    tm = _row_tile(m, tm)

    def core(n_ref, wg_ref, wu_ref, a_ref, b_ref, s_ref):
        nv = n_ref[...]
        a = _dot(nv, wg_ref[...], True)
        b = _dot(nv, wu_ref[...], True)
        a_ref[...] = a.astype(a_ref.dtype)
        b_ref[...] = b.astype(b_ref.dtype)
        s_ref[...] = (a * jax.nn.sigmoid(a) * b).astype(s_ref.dtype)

    w_spec = pl.BlockSpec((None, fb, k), lambda j, i: (j, 0, 0))
    out_spec = pl.BlockSpec((None, tm, fb), lambda j, i: (j, i, 0))
    return _call(name, core, (nj, m // tm), [pl.BlockSpec((tm, k), lambda j, i: (i, 0)), w_spec, w_spec],
                 [out_spec] * 3, [jax.ShapeDtypeStruct((nj, m, fb), BF16)] * 3, [], [n, wg, wu], carry)


def _ffn_d_hidden(name, df, wd, a, b, tm=1024):
    m, k = df.shape
    nj, fb, _ = wd.shape
    tm = _row_tile(m, tm)

    def body(df_ref, wd_ref, a_ref, b_ref, da_ref, db_ref):
        ds = _dot(df_ref[...], wd_ref[...], True)
        av, bv = a_ref[...].astype(F32), b_ref[...].astype(F32)
        sig = jax.nn.sigmoid(av)
        da_ref[...] = (ds * bv * sig * (1.0 + av * (1.0 - sig))).astype(da_ref.dtype)
        db_ref[...] = (ds * av * sig).astype(db_ref.dtype)

    blk = pl.BlockSpec((None, tm, fb), lambda j, i: (j, i, 0))
    return pl.pallas_call(
        body, name=name, grid=(nj, m // tm),
        in_specs=[pl.BlockSpec((tm, k), lambda j, i: (i, 0)), pl.BlockSpec((None, fb, k), lambda j, i: (j, 0, 0)),
                  blk, blk],
        out_specs=[blk, blk], out_shape=[jax.ShapeDtypeStruct((nj, m, fb), BF16)] * 2,
        compiler_params=_cparams(2),
    )(df, wd, a, b)


def kernel(x, ffn1_norm_pre, ffn1_w_gate, ffn1_w_up, ffn1_w_down, ffn1_norm_post, mix_norm_pre, w_in, gate_bias, rel_table, w_attn_out, conv_glu_bias, conv_dw_w, conv_dw_b, conv_ln_g, conv_ln_b, conv_w_out, w_out, mix_norm_post, ffn2_norm_pre, ffn2_w_gate, ffn2_w_up, ffn2_w_down, ffn2_norm_post, loss_target, m_ffn1_norm_pre, m_ffn1_w_gate, m_ffn1_w_up, m_ffn1_w_down, m_ffn1_norm_post, m_mix_norm_pre, m_w_in, m_gate_bias, m_rel_table, m_w_attn_out, m_conv_glu_bias, m_conv_dw_w, m_conv_dw_b, m_conv_ln_g, m_conv_ln_b, m_conv_w_out, m_w_out, m_mix_norm_post, m_ffn2_norm_pre, m_ffn2_w_gate, m_ffn2_w_up, m_ffn2_w_down, m_ffn2_norm_post, v_ffn1_norm_pre, v_ffn1_w_gate, v_ffn1_w_up, v_ffn1_w_down, v_ffn1_norm_post, v_mix_norm_pre, v_w_in, v_gate_bias, v_rel_table, v_w_attn_out, v_conv_glu_bias, v_conv_dw_w, v_conv_dw_b, v_conv_ln_g, v_conv_ln_b, v_conv_w_out, v_w_out, v_mix_norm_post, v_ffn2_norm_pre, v_ffn2_w_gate, v_ffn2_w_up, v_ffn2_w_down, v_ffn2_norm_post):
    args = dict(locals())
    names = ['ffn1_norm_pre', 'ffn1_w_gate', 'ffn1_w_up', 'ffn1_w_down', 'ffn1_norm_post', 'mix_norm_pre', 'w_in',
             'gate_bias', 'rel_table', 'w_attn_out', 'conv_glu_bias', 'conv_dw_w', 'conv_dw_b', 'conv_ln_g',
             'conv_ln_b', 'conv_w_out', 'w_out', 'mix_norm_post', 'ffn2_norm_pre', 'ffn2_w_gate', 'ffn2_w_up',
             'ffn2_w_down', 'ffn2_norm_post']
    big = ['ffn1_w_gate', 'ffn1_w_up', 'ffn1_w_down', 'w_in', 'w_attn_out', 'conv_w_out', 'w_out', 'ffn2_w_gate',
           'ffn2_w_up', 'ffn2_w_down']
    small = [n for n in names if n not in big]

    xs, target = x[0], loss_target[0]
    t, d = xs.shape
    cx, cy = lax.axis_index("x"), lax.axis_index("y")
    chip = 2 * cx + cy

    dw_shard = conv_dw_w[0, :, 0, :]
    cshard = dw_shard.shape[1]
    dw_all, _ = _all_devices("gather_dw", _pack([dw_shard], width=cshard))
    dw_full = jnp.concatenate([dw_all[2 * j, :CONV_WIDTH, :cshard] for j in range(N_CHIPS)], axis=1)
    dw_full = jnp.pad(dw_full, ((0, CONV_HALO - CONV_WIDTH), (0, 0)))
    peers = [(1 - cx, cy), (cx, 1 - cy), (1 - cx, 1 - cy)]
    pos = jnp.stack([lax.axis_index("c"), chip] + [_chip_index(p) for p in peers]).astype(jnp.int32)
    transposed = ("ffn1_w_gate", "ffn1_w_up", "ffn2_w_gate", "ffn2_w_up")
    weight_of = lambda n: n[2:] if n[:2] in ("m_", "v_") else n
    shard = lambda n: jnp.transpose(args[n][0]) if weight_of(n) in transposed else args[n][0]
    unshard = lambda n, v: (jnp.transpose(v) if n in transposed else v)[None]
    own = {n: _cast_into("cast_" + n, pos, shard(n)) for n in big}
    gather = lambda *ns: _gather_carry([own[n] for n in ns])
    res_spec = [(d, F32), (d, F32), (d, BF16)]

    wg1, wu1, wd1 = _run_carry("gather_ffn1", gather("ffn1_w_gate", "ffn1_w_up", "ffn1_w_down"))
    n1 = _rms_fwd("ffn1_pre", xs, ffn1_norm_pre)
    (a1, b1, s1), (win, wao, wco, wout) = _ffn_hidden(
        "ffn1_hidden", n1, wg1, wu1, carry=gather("w_in", "w_attn_out", "conv_w_out", "w_out"))
    (f1, h1, u), (wg2,) = _mm_kblk(
        "ffn1_down", [(s1, wd1)], trans_w=False, epilogue=_ep_post_res_pre(0.5), rows=[xs],
        vecs=[ffn1_norm_post, mix_norm_pre], row_outs=res_spec, carry=gather("ffn2_w_gate"))
    proj, (wu2, wd2) = _mm_nblk("mix_in", u, win, trans_w=False, out_blocked=False, out_dtype=BF16,
                                carry=gather("ffn2_w_up", "ffn2_w_down"))
    table_pad = jnp.pad(rel_table[0], ((0, 0), (0, REL_PAD - rel_table.shape[2])))
    bias = _bias_expand(table_pad)
    kp = jnp.pad(proj[:, D_ATTN:2 * D_ATTN], ((K_PAD, 0), (0, 0)))
    vp = jnp.pad(proj[:, 2 * D_ATTN:3 * D_ATTN], ((K_PAD, 0), (0, 0)))
    att, lse = _attn_fwd(proj, kp, vp, bias)
    cs, c_glu, z_conv = _conv_fwd(proj, conv_glu_bias, dw_full, conv_dw_b, conv_ln_g, conv_ln_b)
    y_a, y_b, merged = _mix_merge(att, cs, wao, wco, proj, gate_bias)
    (mo, h2, n2), _ = _mm_kblk(
        "mix_out", [(merged, wout)], trans_w=False, epilogue=_ep_post_res_pre(1.0), rows=[h1],
        vecs=[mix_norm_post, ffn2_norm_pre], row_outs=res_spec)
    (a2, b2, s2), _ = _ffn_hidden("ffn2_hidden", n2, wg2, wu2)
    g = {}
    (dy, df2, err2, g["ffn2_norm_post"]), _ = _mm_kblk(
        "ffn2_down", [(s2, wd2)], trans_w=False, epilogue=_ep_loss(0.5, d), rows=[h2, target],
        vecs=[ffn2_norm_post], row_outs=[(d, F32), (d, BF16)], vec_outs=[d, d])
    loss = lax.psum(0.5 * jnp.sum(err2) / d, ("x", "y", "c"))

    parts, landed = {}, {}

    def ffn_bwd(tag, df, n, a, b, s, wg, wu, wd, **epilogue):
        da, db = _ffn_d_hidden(tag + "_d_hidden", df, wd, a, b)
        group = [tag + "_w_down", tag + "_w_gate", tag + "_w_up"]
        local = [_mm_tn(tag + "_g_down", s, "blk", df, "full"), _mm_tn(tag + "_g_gate", da, "blk", n, "full"),
                 _mm_tn(tag + "_g_up", db, "blk", n, "full")]
        return _mm_kblk(tag + "_d_n", [(da, wg), (db, wu)], trans_w=False, carry=pair_sums(tag, group, local),
                        **epilogue), group

    def pair_sums(tag, group, local):
        theirs = _pair_exchange("pair_" + tag, local)
        for n, mine, other in zip(group, local, theirs):
            parts[n] = _add_pair("pair_sum_" + n, pos, mine, other)
        return _to_owner_carry([parts[n] for n in group])

    def keep(group, carried):
        for n, val in zip(group, carried):
            landed[n] = val

    ((dh2, dmo, g["ffn2_norm_pre"], g["mix_norm_post"]), carried), group = ffn_bwd(
        "ffn2", df2, n2, a2, b2, s2, wg2, wu2, wd2, epilogue=_ep_pre_bwd_post(1.0), rows=[h2, dy, mo],
        vecs=[ffn2_norm_pre, mix_norm_post], row_outs=[(d, F32), (d, BF16)], vec_outs=[d, d])
    keep(group, carried)
    g_wout = _mm_tn("mix_g_out", merged, "col", dmo, "full")
    dy_a, dy_b, dgates, datt, dcs, g["gate_bias"] = _mix_d_merge(dmo, wout, y_a, y_b, wao, wco, proj, gate_bias)
    g_wao = _mm_tn("attn_g_out", att, "full", dy_a, "col")
    g_wco = _mm_tn("conv_g_out", cs, "full", dy_b, "col")
    dq, dkp, dvp, dbias = _attn_bwd(proj, kp, vp, bias, att, lse, datt)
    g["rel_table"] = _bias_fold(dbias)[:, :rel_table.shape[2]]
    dcin, g_dw, g["conv_dw_b"], g["conv_ln_g"], g["conv_ln_b"], g["conv_glu_bias"] = _conv_bwd(
        proj, c_glu, z_conv, dcs, conv_glu_bias, dw_full, conv_ln_g, conv_ln_b)
    dproj = jnp.concatenate([dq, dkp[K_PAD:].astype(BF16), dvp[K_PAD:].astype(BF16), dcin, dgates], axis=1)
    g_win = _mm_tn("mix_g_in", u, "full", dproj, "col")
    group = ["w_out", "w_attn_out", "conv_w_out", "w_in"]
    (dh1, df1, g["mix_norm_pre"], g["ffn1_norm_post"]), carried = _mm_kblk(
        "mix_d_in", [(dproj, win)], trans_w=True, epilogue=_ep_pre_bwd_post(0.5), rows=[h1, dh2, f1],
        vecs=[mix_norm_pre, ffn1_norm_post], row_outs=[(d, F32), (d, BF16)], vec_outs=[d, d],
        carry=pair_sums("mix", group, [g_wout, g_wao, g_wco, g_win]))
    keep(group, carried)
    ((grad_x, g["ffn1_norm_pre"]), carried), group = ffn_bwd(
        "ffn1", df1, n1, a1, b1, s1, wg1, wu1, wd1, epilogue=_ep_pre_bwd_first(), rows=[xs, dh1],
        vecs=[ffn1_norm_pre], row_outs=[(d, F32)], vec_outs=[d])
    keep(group, carried)

    halves = [_add_chips("chip_sum_" + n, pos, parts[n], landed[n]) for n in big]
    other_halves = _swap_halves(halves)

    g["conv_dw_w"] = g_dw[:CONV_WIDTH]
    _, small_sum = _all_devices("sum_small", _pack([g[n] for n in small]))
    for n, val in zip(small, _unpack(small_sum, [g[n].shape for n in small])):
        g[n] = val
    g["conv_dw_w"] = lax.dynamic_slice_in_dim(g["conv_dw_w"], chip * cshard, cshard, axis=1)

    grads, deltas, new_m, new_v = {}, {}, {}, {}
    for n, mine, other in zip(big, halves, other_halves):
        gr, dl, m2, v2 = _adamw_halves("adamw_" + n, pos, shard(n), shard("m_" + n), shard("v_" + n), mine, other)
        grads[n], deltas[n], new_m[n], new_v[n] = unshard(n, gr), unshard(n, dl), unshard(n, m2), unshard(n, v2)
    shapes = [g[n].shape for n in small]
    packed = lambda pre: _pack([args[pre + n].reshape(shp) for n, shp in zip(small, shapes)])
    dl, m2, v2 = _adamw("adamw_small", packed(""), _pack([g[n] for n in small]), packed("m_"), packed("v_"))
    for n, a_, b_, c_ in zip(small, _unpack(dl, shapes), _unpack(m2, shapes), _unpack(v2, shapes)):
        shape = args[n].shape
        grads[n], deltas[n], new_m[n], new_v[n] = (g[n].reshape(shape), a_.reshape(shape), b_.reshape(shape),
                                                   c_.reshape(shape))

    return (loss, grad_x[None], *[grads[n] for n in names], *[deltas[n] for n in names],
            *[new_m[n] for n in names], *[new_v[n] for n in names])
```

```python
import functools

import numpy as np
import jax
import jax.numpy as jnp
from jax import lax
from jax.experimental import pallas as pl
from jax.experimental.pallas import tpu as pltpu

F32 = jnp.float32
BF16 = jnp.bfloat16
MESH = pl.DeviceIdType.MESH
ANY = pl.BlockSpec(memory_space=pl.ANY)

EPS = 1e-6
CHUNK = 64
LEFT_CHUNKS = 8
N_HEADS = 8
HEAD_DIM = 64
D_ATTN = N_HEADS * HEAD_DIM
D_CONV = 512
CONV_WIDTH = 31
REL_CLIP = 128
N_CHIPS = 4
N_DEV = 8
Q_BLOCK = 4 * CHUNK
K_PAD = LEFT_CHUNKS * CHUNK
K_WIN = K_PAD + Q_BLOCK
REL_EXT = 1024
REL_PAD = 384
CONV_HALO = 32
CONV_TILE = 256
COL = 512
NEG = -1e30

ADAM_LR = 0.001
ADAM_B1 = 0.9
ADAM_B2 = 0.999
ADAM_EPS = 1e-08
ADAM_WD = 0.01
ADAM_STEP = 10

VMEM_LIMIT_BYTES = 56 * 1024 * 1024


def _cparams(n_grid):
    return pltpu.CompilerParams(dimension_semantics=("arbitrary",) * n_grid, vmem_limit_bytes=VMEM_LIMIT_BYTES)


def _row_tile(rows, want):
    if rows <= want:
        return rows
    for t in range(want - want % 16, 0, -16):
        if rows % t == 0:
            return t
    raise ValueError((rows, want))


def _dot(a, w, trans_w):
    dims = (((1,), (1,)), ((), ())) if trans_w else (((1,), (0,)), ((), ()))
    return lax.dot_general(a, w, dims, preferred_element_type=F32)


class _Carry:
    LATE_STEPS = 2

    def __init__(self, ins, outs, aliases, sems, phases):
        self.ins, self.outs, self.aliases, self.sems, self.phases = ins, outs, aliases, sems, phases


def _call(name, core, grid, in_specs, out_specs, out_shape, scratch, args, carry=None):
    n_in, n_out, n_scr = len(in_specs), len(out_specs), len(scratch)
    if carry is None:
        out = pl.pallas_call(core, name=name, grid=grid, in_specs=in_specs, out_specs=out_specs, out_shape=out_shape,
                             scratch_shapes=scratch, compiler_params=_cparams(len(grid)))(*args)
        return list(out), []
    c_in, c_out = len(carry.ins), len(carry.outs)
    total = int(np.prod(grid))
    late = max(total - 1 - _Carry.LATE_STEPS, 0)

    def body(*refs):
        ins, refs = refs[:n_in], refs[n_in:]
        c_ins, refs = refs[:c_in], refs[c_in:]
        outs, refs = refs[:n_out], refs[n_out:]
        c_outs, refs = refs[:c_out], refs[c_out:]
        scr, c_sems = refs[:n_scr], refs[n_scr:]
        step = pl.program_id(0)
        for axis in range(1, len(grid)):
            step = step * grid[axis] + pl.program_id(axis)

        def run(when, at):
            for w, fn in carry.phases:
                if w == when:
                    pl.when(step == at)(functools.partial(fn, c_ins, c_outs, c_sems))

        run("first", 0)
        core(*ins, *outs, *scr)
        run("late", late)
        run("last", total - 1)

    out = pl.pallas_call(
        body, name=name, grid=grid, in_specs=list(in_specs) + [ANY] * c_in, out_specs=list(out_specs) + [ANY] * c_out,
        out_shape=list(out_shape) + list(carry.outs), scratch_shapes=list(scratch) + list(carry.sems),
        input_output_aliases={n_in + a: n_out + b for a, b in carry.aliases.items()},
        compiler_params=_cparams(len(grid)),
    )(*args, *carry.ins)
    return list(out[:n_out]), list(out[n_out:])


def _mm_nblk(name, a, w, *, trans_w, out_blocked, out_dtype, tm=1024, carry=None):
    m, k = a.shape
    nj = w.shape[0]
    nb = w.shape[1] if trans_w else w.shape[2]
    tm = _row_tile(m, tm)

    def core(a_ref, w_ref, o_ref):
        o_ref[...] = _dot(a_ref[...], w_ref[...], trans_w).astype(o_ref.dtype)

    if out_blocked:
        out_shape, out_spec = (nj, m, nb), pl.BlockSpec((None, tm, nb), lambda j, i: (j, i, 0))
    else:
        out_shape, out_spec = (m, nj * nb), pl.BlockSpec((tm, nb), lambda j, i: (i, j))
    out, carried = _call(
        name, core, (nj, m // tm),
        [pl.BlockSpec((tm, k), lambda j, i: (i, 0)), pl.BlockSpec((None,) + w.shape[1:], lambda j, i: (j, 0, 0))],
        [out_spec], [jax.ShapeDtypeStruct(out_shape, out_dtype)], [], [a, w], carry)
    return out[0] if carry is None else (out[0], carried)


def _mm_kblk(name, pairs, *, trans_w, out_dtype=F32, tm=512, sub=256, epilogue=None, rows=(), vecs=(), row_outs=None,
             vec_outs=(), carry=None):
    w0 = pairs[0][1]
    nj = w0.shape[0]
    n = w0.shape[1] if trans_w else w0.shape[2]
    kb = w0.shape[2] if trans_w else w0.shape[1]
    m = pairs[0][0].shape[-2]
    tm = _row_tile(m, tm)
    ts = _row_tile(tm, sub)
    n_pairs, n_rows, n_vecs = len(pairs), len(rows), len(vecs)
    if epilogue is None:
        epilogue, row_outs = (lambda acc, r, v: ([acc], [])), [(n, out_dtype)]
    n_ro, n_vo = len(row_outs), len(vec_outs)

    def core(*refs):
        pair_refs, refs = refs[:2 * n_pairs], refs[2 * n_pairs:]
        row_refs, refs = refs[:n_rows], refs[n_rows:]
        vec_refs, refs = refs[:n_vecs], refs[n_vecs:]
        ro_refs, vo_refs = refs[:n_ro], refs[n_ro:]
        if n_vo:
            @pl.when(pl.program_id(0) == 0)
            def _():
                for ref in vo_refs:
                    ref[...] = jnp.zeros_like(ref)

        vec_vals = [v[...] for v in vec_refs]
        sums = None
        for r0 in range(0, tm, ts):
            sub_rows = slice(r0, r0 + ts)
            acc = None
            for p in range(n_pairs):
                a_ref, w_ref = pair_refs[2 * p], pair_refs[2 * p + 1]
                for j in range(nj):
                    a_blk = a_ref[j, sub_rows, :] if len(a_ref.shape) == 3 else a_ref[sub_rows, j * kb:(j + 1) * kb]
                    part = _dot(a_blk, w_ref[j], trans_w)
                    acc = part if acc is None else acc + part
            ro, vo = epilogue(acc, [r[sub_rows, :] for r in row_refs], vec_vals)
            for ref, val in zip(ro_refs, ro):
                ref[sub_rows, :] = val.astype(ref.dtype)
            sums = vo if sums is None else [s + v for s, v in zip(sums, vo)]
        for ref, val in zip(vo_refs, sums or []):
            ref[...] += val

    in_specs, args = [], []
    for a, w in pairs:
        if a.ndim == 3:
            in_specs.append(pl.BlockSpec((nj, tm, kb), lambda i: (0, i, 0)))
        else:
            in_specs.append(pl.BlockSpec((tm, nj * kb), lambda i: (i, 0)))
        in_specs.append(pl.BlockSpec(w.shape, lambda i: (0, 0, 0), pipeline_mode=pl.Buffered(1)))
        args += [a, w]
    in_specs += [pl.BlockSpec((tm, r.shape[1]), lambda i: (i, 0)) for r in rows]
    in_specs += [pl.BlockSpec(v.shape, lambda i: (0, 0)) for v in vecs]
    out_specs = [pl.BlockSpec((tm, cols), lambda i: (i, 0)) for cols, _ in row_outs]
    out_specs += [pl.BlockSpec((1, cols), lambda i: (0, 0)) for cols in vec_outs]
    out_shape = [jax.ShapeDtypeStruct((m, cols), dt) for cols, dt in row_outs]
    out_shape += [jax.ShapeDtypeStruct((1, cols), F32) for cols in vec_outs]
    return _call(name, core, (m // tm,), in_specs, out_specs, out_shape, [], args + list(rows) + list(vecs), carry)


def _mm_tn(name, a, a_mode, b, b_mode, *, out_dtype=BF16, tt=2048):
    nj = N_CHIPS
    t = a.shape[-2]
    tt = _row_tile(t, tt)

    def spec(x, mode):
        if mode == "full":
            return x.shape[1], pl.BlockSpec((tt, x.shape[1]), lambda j, s: (s, 0))
        if mode == "col":
            cb = x.shape[1] // nj
            return cb, pl.BlockSpec((tt, cb), lambda j, s: (s, j))
        return x.shape[2], pl.BlockSpec((None, tt, x.shape[2]), lambda j, s: (j, s, 0))

    ca, a_spec = spec(a, a_mode)
    cb, b_spec = spec(b, b_mode)
    n_steps = t // tt

    def body(a_ref, b_ref, o_ref, acc_ref):
        s = pl.program_id(1)

        @pl.when(s == 0)
        def _():
            acc_ref[...] = jnp.zeros_like(acc_ref)

        acc_ref[...] += lax.dot_general(a_ref[...], b_ref[...], (((0,), (0,)), ((), ())),
                                        preferred_element_type=F32)

        @pl.when(s == n_steps - 1)
        def _():
            o_ref[...] = acc_ref[...].astype(o_ref.dtype)

    return pl.pallas_call(
        body, name=name, grid=(nj, n_steps), in_specs=[a_spec, b_spec],
        out_specs=pl.BlockSpec((None, ca, cb), lambda j, s: (j, 0, 0)),
        out_shape=jax.ShapeDtypeStruct((nj, ca, cb), out_dtype),
        scratch_shapes=[pltpu.VMEM((ca, cb), F32)], compiler_params=_cparams(2),
    )(a, b)


def _rowwise(name, fn, rows, vecs, row_outs, vec_outs, *, tm=256, carry=None):
    nrows = rows[0][0].shape[0]
    tm = _row_tile(nrows, tm)
    n_r, n_v, n_ro, n_vo = len(rows), len(vecs), len(row_outs), len(vec_outs)

    def body(*refs):
        r_vals = [r[...] for r in refs[:n_r]]
        v_vals = [r[...] for r in refs[n_r:n_r + n_v]]
        ro_refs = refs[n_r + n_v:n_r + n_v + n_ro]
        vo_refs = refs[n_r + n_v + n_ro:]
        ro, vo = fn(r_vals, v_vals)
        for ref, val in zip(ro_refs, ro):
            ref[...] = val.astype(ref.dtype)
        if n_vo:
            @pl.when(pl.program_id(0) == 0)
            def _():
                for ref in vo_refs:
                    ref[...] = jnp.zeros_like(ref)

            for ref, val in zip(vo_refs, vo):
                ref[...] += val

    in_specs = [pl.BlockSpec((tm, cols), functools.partial(lambda i, cb: (i, cb), cb=cb)) for _, cols, cb in rows]
    in_specs += [pl.BlockSpec(v.shape, functools.partial(lambda i, nd: (0,) * nd, nd=v.ndim)) for v in vecs]
    out_specs = [pl.BlockSpec((tm, cols), lambda i: (i, 0)) for cols, _ in row_outs]
    out_specs += [pl.BlockSpec((1, cols), lambda i: (0, 0)) for cols in vec_outs]
    out_shape = [jax.ShapeDtypeStruct((nrows, cols), dt) for cols, dt in row_outs]
    out_shape += [jax.ShapeDtypeStruct((1, cols), F32) for cols in vec_outs]
    out, carried = _call(name, body, (nrows // tm,), in_specs, out_specs, out_shape, [],
                         [r[0] for r in rows] + list(vecs), carry)
    return out if carry is None else (out, carried)


def _whole(x):
    return (x, x.shape[1], 0)


def _colsum(x):
    return jnp.sum(x, axis=0, keepdims=True)


def _rstd(x):
    return lax.rsqrt(jnp.mean(x * x, axis=-1, keepdims=True) + EPS)


def _rms_bwd(dn, x, g):
    r = _rstd(x)
    c = dn * g
    dx = r * c - x * (r * r * r) * jnp.mean(c * x, axis=-1, keepdims=True)
    return dx, _colsum(dn * x * r)


def _rms_fwd(name, x, g, carry):
    def fn(r, v):
        (xv,), (gv,) = r, v
        return [xv * _rstd(xv) * gv], []

    (n,), carried = _rowwise(name, fn, [_whole(x)], [g], [(x.shape[1], BF16)], [], carry=carry)
    return n, carried


def _ep_post_res_pre(scale):
    def epilogue(acc, rows, vecs):
        (resid,), (g_post, g_next) = rows, vecs
        h = resid + scale * (acc * _rstd(acc) * g_post)
        return [acc, h, h * _rstd(h) * g_next], []

    return epilogue


def _post_bwd(dh, f, g_post, scale):
    return _rms_bwd(scale * dh, f, g_post)


def _ep_loss(scale, d):
    def epilogue(acc, rows, vecs):
        (resid, target), (g_post,) = rows, vecs
        err = resid + scale * (acc * _rstd(acc) * g_post) - target
        dy = err * (1.0 / d)
        df, dg_post = _post_bwd(dy, acc, g_post, scale)
        return [dy, df], [_colsum(err * err), dg_post]

    return epilogue


def _ep_pre_bwd_post(scale_prev):
    def epilogue(acc, rows, vecs):
        (h, dh_up, f_prev), (g_pre, g_post_prev) = rows, vecs
        dx, dg_pre = _rms_bwd(acc, h, g_pre)
        dh = dh_up + dx
        df, dg_post = _post_bwd(dh, f_prev, g_post_prev, scale_prev)
        return [dh, df], [dg_pre, dg_post]

    return epilogue


def _ep_pre_bwd_first():
    def epilogue(acc, rows, vecs):
        (x, dh_up), (g_pre,) = rows, vecs
        dx, dg_pre = _rms_bwd(acc, x, g_pre)
        return [dh_up + dx], [dg_pre]

    return epilogue


def _gate_specs(d, tm):
    first = (3 * D_ATTN + 2 * D_CONV) // COL
    return [pl.BlockSpec((tm, COL), functools.partial(lambda i, cb: (i, cb), cb=first + p)) for p in range(2 * d // COL)]


def _gate(piece_refs, bias_ref, c0, width):
    p, off = divmod(c0, COL)
    return jax.nn.sigmoid(piece_refs[p][:, off:off + width].astype(F32) + bias_ref[:, c0:c0 + width])


def _resident(w):
    return pl.BlockSpec(w.shape, functools.partial(lambda i, nd: (0,) * nd, nd=w.ndim), pipeline_mode=pl.Buffered(1))


def _mix_merge(att, cs, wao, wco, proj, gate_bias, tm=512):
    t = att.shape[0]
    nj, _, nb = wao.shape
    d = nj * nb
    tm = _row_tile(t, tm)
    gate_specs = _gate_specs(d, tm)
    n_p = len(gate_specs)

    def body(att_ref, cs_ref, wao_ref, wco_ref, *rest):
        pieces, (gb_ref, ya_ref, yb_ref, m_ref) = rest[:n_p], rest[n_p:]
        av, cv = att_ref[...], cs_ref[...]
        for j in range(nj):
            cols = slice(j * nb, (j + 1) * nb)
            ya = _dot(av, wao_ref[j], False)
            yb = _dot(cv, wco_ref[j], False)
            merged = _gate(pieces, gb_ref, j * nb, nb) * ya + _gate(pieces, gb_ref, d + j * nb, nb) * yb
            ya_ref[:, cols] = ya.astype(ya_ref.dtype)
            yb_ref[:, cols] = yb.astype(yb_ref.dtype)
            m_ref[:, cols] = merged.astype(m_ref.dtype)

    row = lambda x: pl.BlockSpec((tm, x.shape[1]), lambda i: (i, 0))
    out_spec = pl.BlockSpec((tm, d), lambda i: (i, 0))
    return pl.pallas_call(
        body, name="mix_merge", grid=(t // tm,),
        in_specs=[row(att), row(cs), _resident(wao), _resident(wco)] + gate_specs + [_resident(gate_bias)],
        out_specs=[out_spec] * 3, out_shape=[jax.ShapeDtypeStruct((t, d), BF16)] * 3, compiler_params=_cparams(1),
    )(att, cs, wao, wco, *([proj] * n_p), gate_bias)


def _mix_d_merge(dmo, wout, y_a, y_b, wao, wco, proj, gate_bias, tm=512):
    t, d = dmo.shape
    nj, _, nb = wao.shape
    ka, kc = wao.shape[1], wco.shape[1]
    tm = _row_tile(t, tm)
    gate_specs = _gate_specs(d, tm)
    n_p = len(gate_specs)

    def body(dmo_ref, wout_ref, ya_ref, yb_ref, wao_ref, wco_ref, *rest):
        pieces, (gb_ref, dya_ref, dyb_ref, dg_ref, datt_ref, dcs_ref, dgb_ref) = rest[:n_p], rest[n_p:]

        @pl.when(pl.program_id(0) == 0)
        def _():
            dgb_ref[...] = jnp.zeros_like(dgb_ref)

        dmo_v = dmo_ref[...]
        datt = dcs = None
        for j in range(nj):
            cols, cols_b = slice(j * nb, (j + 1) * nb), slice(d + j * nb, d + (j + 1) * nb)
            dm = _dot(dmo_v, wout_ref[j], True)
            ga, gb = _gate(pieces, gb_ref, j * nb, nb), _gate(pieces, gb_ref, d + j * nb, nb)
            dya, dyb = (dm * ga).astype(BF16), (dm * gb).astype(BF16)
            dga = dm * ya_ref[:, cols].astype(F32) * ga * (1.0 - ga)
            dgb = dm * yb_ref[:, cols].astype(F32) * gb * (1.0 - gb)
            dya_ref[:, cols], dyb_ref[:, cols] = dya, dyb
            dg_ref[:, cols], dg_ref[:, cols_b] = dga.astype(dg_ref.dtype), dgb.astype(dg_ref.dtype)
            dgb_ref[:, cols] += _colsum(dga)
            dgb_ref[:, cols_b] += _colsum(dgb)
            pa, pc = _dot(dya, wao_ref[j], True), _dot(dyb, wco_ref[j], True)
            datt, dcs = (pa, pc) if datt is None else (datt + pa, dcs + pc)
        datt_ref[...] = datt.astype(datt_ref.dtype)
        dcs_ref[...] = dcs.astype(dcs_ref.dtype)

    row = lambda cols: pl.BlockSpec((tm, cols), lambda i: (i, 0))
    return pl.pallas_call(
        body, name="mix_d_merge", grid=(t // tm,),
        in_specs=[row(d), _resident(wout), row(d), row(d), _resident(wao), _resident(wco)] + gate_specs
        + [_resident(gate_bias)],
        out_specs=[row(d), row(d), row(2 * d), row(ka), row(kc), pl.BlockSpec((1, 2 * d), lambda i: (0, 0))],
        out_shape=[jax.ShapeDtypeStruct((t, d), BF16), jax.ShapeDtypeStruct((t, d), BF16),
                   jax.ShapeDtypeStruct((t, 2 * d), BF16), jax.ShapeDtypeStruct((t, ka), BF16),
                   jax.ShapeDtypeStruct((t, kc), F32), jax.ShapeDtypeStruct((1, 2 * d), F32)],
        compiler_params=_cparams(1),
    )(dmo, wout, y_a, y_b, wao, wco, *([proj] * n_p), gate_bias)


def _adamw_math(wv, gv, mv, vv):
    m2 = ADAM_B1 * mv + (1.0 - ADAM_B1) * gv
    v2 = ADAM_B2 * vv + (1.0 - ADAM_B2) * (gv * gv)
    m_hat = m2 / (1.0 - ADAM_B1 ** ADAM_STEP)
    v_hat = v2 / (1.0 - ADAM_B2 ** ADAM_STEP)
    delta = -ADAM_LR * (m_hat / (jnp.sqrt(v_hat) + ADAM_EPS) + ADAM_WD * wv)
    return delta, m2, v2


def _adamw(name, w, g, m, v):
    def fn(r, _):
        return list(_adamw_math(*r)), []

    c = w.shape[1]
    return _rowwise(name, fn, [_whole(w), _whole(g), _whole(m), _whole(v)], [], [(c, F32)] * 3, [], tm=256)


POS_C, POS_CHIP, POS_PEER = 0, 1, 2


def _placed_call(body, name, pos, grid, in_specs, out_specs, out_shape, args):
    return pl.pallas_call(
        body, name=name, out_shape=out_shape, compiler_params=_cparams(len(grid)),
        grid_spec=pltpu.PrefetchScalarGridSpec(num_scalar_prefetch=1, grid=grid, in_specs=in_specs,
                                               out_specs=out_specs),
    )(pos, *args)


def _cast_into(name, pos, w):
    r, cols = w.shape
    tm = _row_tile(r, 256)

    def body(pos_ref, w_ref, o_ref):
        o_ref[...] = w_ref[...].astype(o_ref.dtype)

    return _placed_call(
        body, name, pos, (r // tm,), [pl.BlockSpec((tm, cols), lambda i, pos: (i, 0))],
        pl.BlockSpec((None, tm, cols), lambda i, pos: (pos[POS_CHIP], i, 0)),
        jax.ShapeDtypeStruct((N_CHIPS, r, cols), BF16), [w])


def _add_pair(name, pos, grad, landed):
    nj, half, cols = landed.shape
    tm = _row_tile(half, 256)
    nb = half // tm

    def body(pos_ref, g_ref, l_ref, o_ref):
        o_ref[...] = (g_ref[...].astype(F32) + l_ref[...].astype(F32)).astype(o_ref.dtype)

    spec = pl.BlockSpec((None, tm, cols), lambda j, i, pos: (j, i, 0))
    return _placed_call(
        body, name, pos, (nj, nb),
        [pl.BlockSpec((None, tm, cols), lambda j, i, pos: (j, pos[POS_C] * nb + i, 0)), spec], spec,
        jax.ShapeDtypeStruct(landed.shape, BF16), [grad, landed])


def _add_chips(name, pos, part, landed):
    _, half, cols = landed.shape
    tm = _row_tile(half, 256)

    def body(pos_ref, p_ref, l0_ref, l1_ref, l2_ref, o_ref):
        acc = p_ref[...].astype(F32)
        for ref in (l0_ref, l1_ref, l2_ref):
            acc = acc + ref[...].astype(F32)
        o_ref[...] = acc

    slot = lambda at: pl.BlockSpec((None, tm, cols), functools.partial(lambda i, pos, at: (pos[at], i, 0), at=at))
    return _placed_call(
        body, name, pos, (half // tm,), [slot(POS_CHIP)] + [slot(POS_PEER + k) for k in range(3)],
        pl.BlockSpec((tm, cols), lambda i, pos: (i, 0)), jax.ShapeDtypeStruct((half, cols), F32),
        [part, landed, landed, landed])


def _adamw_halves(name, pos, w, m, v, own, landed):
    r, cols = w.shape
    half = own.shape[0]
    tm = _row_tile(half, 256)
    nb = half // tm

    def body(pos_ref, w_ref, m_ref, v_ref, own_ref, land_ref, g_out, d_out, m_out, v_out):
        mine = pl.program_id(0) == pos_ref[POS_C]
        g = jnp.where(mine, own_ref[...], land_ref[...])
        delta, m2, v2 = _adamw_math(w_ref[...], g, m_ref[...], v_ref[...])
        g_out[...] = g
        d_out[...] = delta
        m_out[...] = m2
        v_out[...] = v2

    full = pl.BlockSpec((tm, cols), lambda h, i, pos: (h * nb + i, 0))
    part = pl.BlockSpec((tm, cols), lambda h, i, pos: (i, 0))
    return _placed_call(
        body, name, pos, (2, nb), [full, full, full, part, part], [full] * 4,
        [jax.ShapeDtypeStruct((r, cols), F32)] * 4, [w, m, v, own, landed])


N_START = K_PAD // Q_BLOCK


def _rel_onehot(n_q):
    e = np.arange(REL_EXT)
    dist = K_PAD - (e - (n_q - 1))
    idx = np.clip(dist, -REL_CLIP, REL_CLIP) + REL_CLIP
    return (np.arange(REL_PAD)[:, None] == idx[None, :]).astype(np.float32)


def _skew(x, left):
    row = lax.broadcasted_iota(jnp.int32, x.shape, 0)
    for bit in range(x.shape[0].bit_length() - 1):
        amount = 1 << bit
        rolled = pltpu.roll(x, REL_EXT - amount if left else amount, 1)
        x = jnp.where((row >> bit) & 1 == 1, rolled, x)
    return x


def _bias_expand(table_pad):
    onehot = jnp.asarray(_rel_onehot(Q_BLOCK))

    def body(t_ref, oh_ref, o_ref):
        ext = jnp.dot(t_ref[...], oh_ref[...], precision=lax.Precision.HIGHEST, preferred_element_type=F32)
        qc = lax.broadcasted_iota(jnp.int32, (Q_BLOCK, K_WIN), 0) // CHUNK
        kpos = lax.broadcasted_iota(jnp.int32, (Q_BLOCK, K_WIN), 1)
        band = (kpos // CHUNK >= qc) & (kpos // CHUNK <= qc + LEFT_CHUNKS)
        for h in range(N_HEADS):
            rows = jnp.broadcast_to(ext[h:h + 1, :], (Q_BLOCK, REL_EXT))
            rolled = _skew(pltpu.roll(rows, REL_EXT - (Q_BLOCK - 1), 1), left=False)[:, :K_WIN]
            for v in range(N_START + 1):
                o_ref[v, h] = jnp.where(band & (kpos + v * Q_BLOCK >= K_PAD), rolled, NEG)

    return pl.pallas_call(
        body, name="bias_expand", out_shape=jax.ShapeDtypeStruct((N_START + 1, N_HEADS, Q_BLOCK, K_WIN), F32),
        compiler_params=pltpu.CompilerParams(vmem_limit_bytes=VMEM_LIMIT_BYTES),
    )(table_pad, onehot)


def _bias_fold(dbias):
    onehot_t = jnp.asarray(_rel_onehot(CHUNK).T)

    def body(d_ref, oh_ref, o_ref, ext_ref):
        for h in range(N_HEADS):
            x = jnp.concatenate([d_ref[h], jnp.zeros((CHUNK, REL_EXT - K_WIN), F32)], axis=1)
            rolled = _skew(pltpu.roll(x, CHUNK - 1, 1), left=True)
            ext_ref[h:h + 1, :] = jnp.sum(rolled, axis=0, keepdims=True)
        o_ref[...] = jnp.dot(ext_ref[...], oh_ref[...], precision=lax.Precision.HIGHEST,
                             preferred_element_type=F32)

    return pl.pallas_call(
        body, name="bias_fold", out_shape=jax.ShapeDtypeStruct((N_HEADS, REL_PAD), F32),
        scratch_shapes=[pltpu.VMEM((N_HEADS, REL_EXT), F32)],
        compiler_params=pltpu.CompilerParams(vmem_limit_bytes=VMEM_LIMIT_BYTES),
    )(dbias, onehot_t)


def _head_lanes():
    lane = lax.broadcasted_iota(jnp.int32, (1, 2 * HEAD_DIM), 1)
    return [lane < HEAD_DIM, lane >= HEAD_DIM]


def _only(mask, x, scale=None):
    x = jnp.where(mask, x, jnp.zeros_like(x))
    return x if scale is None else x * scale


def _contract_lanes(a, b):
    return lax.dot_general(a, b, (((1,), (1,)), ((), ())), preferred_element_type=F32)


def _contract_rows(a, b):
    return lax.dot_general(a, b, (((0,), (0,)), ((), ())), preferred_element_type=F32)


def _attn_specs(n_kv):
    q_spec = pl.BlockSpec((Q_BLOCK, 2 * HEAD_DIM), lambda p, i: (i, p))
    kv_specs = [pl.BlockSpec((Q_BLOCK, 2 * HEAD_DIM), functools.partial(lambda p, i, kk: (i + kk, p), kk=kk))
                for _ in range(n_kv) for kk in range(K_WIN // Q_BLOCK)]
    bias_spec = pl.BlockSpec((None, 2, Q_BLOCK, K_WIN), lambda p, i: (jnp.minimum(i, N_START), p, 0, 0))
    return q_spec, kv_specs, bias_spec


def _attn_fwd(proj, kp, vp, bias):
    t = proj.shape[0]
    n_win = K_WIN // Q_BLOCK

    def body(q_ref, *refs):
        k_refs, v_refs = refs[:n_win], refs[n_win:2 * n_win]
        b_ref, o_ref, lse_ref = refs[2 * n_win:]
        k = jnp.concatenate([r[...] for r in k_refs], axis=0)
        v = jnp.concatenate([r[...] for r in v_refs], axis=0)
        q = q_ref[...]
        o = lse = None
        for hh, lanes in enumerate(_head_lanes()):
            s = _contract_lanes(_only(lanes, q, HEAD_DIM ** -0.5), k) + b_ref[hh]
            m = jnp.max(s, axis=1, keepdims=True)
            p = jnp.exp(s - m)
            l = jnp.sum(p, axis=1, keepdims=True)
            oh = jnp.dot(p.astype(BF16), v, preferred_element_type=F32) / l
            lse_h = jnp.broadcast_to(m + jnp.log(l), oh.shape)
            o, lse = (oh, lse_h) if o is None else (jnp.where(lanes, oh, o), jnp.where(lanes, lse_h, lse))
        o_ref[...] = o.astype(o_ref.dtype)
        lse_ref[...] = lse

    q_spec, kv_specs, bias_spec = _attn_specs(2)
    out_spec = pl.BlockSpec((Q_BLOCK, 2 * HEAD_DIM), lambda p, i: (i, p))
    return pl.pallas_call(
        body, name="attn_fwd", grid=(N_HEADS // 2, t // Q_BLOCK),
        in_specs=[q_spec] + kv_specs + [bias_spec], out_specs=[out_spec, out_spec],
        out_shape=[jax.ShapeDtypeStruct((t, D_ATTN), BF16), jax.ShapeDtypeStruct((t, D_ATTN), F32)],
        compiler_params=_cparams(2),
    )(proj, *([kp] * n_win), *([vp] * n_win), bias)


def _attn_bwd(proj, kp, vp, bias, att, lse, datt):
    t = proj.shape[0]
    n_win = K_WIN // Q_BLOCK

    def body(q_ref, *refs):
        k_refs, v_refs = refs[:n_win], refs[n_win:2 * n_win]
        b_ref, o_ref, lse_ref, do_ref, dq_ref, dk_ref, dv_ref, db_ref = refs[2 * n_win:]
        i = pl.program_id(1)

        @pl.when(i == 0)
        def _():
            dk_ref[...] = jnp.zeros_like(dk_ref)
            dv_ref[...] = jnp.zeros_like(dv_ref)
            db_ref[...] = jnp.zeros_like(db_ref)

        k = jnp.concatenate([r[...] for r in k_refs], axis=0)
        v = jnp.concatenate([r[...] for r in v_refs], axis=0)
        q, do, o = q_ref[...], do_ref[...], o_ref[...].astype(F32)
        rows = pl.ds(pl.multiple_of(i * Q_BLOCK, Q_BLOCK), K_WIN)
        scale = HEAD_DIM ** -0.5
        dq = dk = dv = None
        for hh, lanes in enumerate(_head_lanes()):
            qh, doh = _only(lanes, q, scale), _only(lanes, do)
            s = _contract_lanes(qh, k) + b_ref[hh]
            p = jnp.exp(s - lse_ref[:, hh * HEAD_DIM:hh * HEAD_DIM + 1])
            delta = jnp.sum(doh.astype(F32) * o, axis=1, keepdims=True)
            ds = p * (_contract_lanes(doh, v) - delta)
            folded = ds[:CHUNK]
            for c in range(1, Q_BLOCK // CHUNK):
                folded = folded + pltpu.roll(ds[c * CHUNK:(c + 1) * CHUNK], K_WIN - c * CHUNK, 1)
            db_ref[hh] += folded
            dsb = ds.astype(BF16)
            dqh = jnp.dot(dsb, k, preferred_element_type=F32)
            dq = dqh if dq is None else jnp.where(lanes, dqh, dq)
            dkh, dvh = _contract_rows(dsb, qh), _contract_rows(p.astype(BF16), doh)
            dk, dv = (dkh, dvh) if dk is None else (dk + dkh, dv + dvh)
        dq_ref[...] = (dq * scale).astype(dq_ref.dtype)
        dk_ref[rows, :] += dk
        dv_ref[rows, :] += dv

    q_spec, kv_specs, bias_spec = _attn_specs(2)
    row_spec = pl.BlockSpec((Q_BLOCK, 2 * HEAD_DIM), lambda p, i: (i, p))
    full_spec = pl.BlockSpec((t + K_PAD, 2 * HEAD_DIM), lambda p, i: (0, p))
    return pl.pallas_call(
        body, name="attn_bwd", grid=(N_HEADS // 2, t // Q_BLOCK),
        in_specs=[q_spec] + kv_specs + [bias_spec, row_spec, row_spec, row_spec],
        out_specs=[row_spec, full_spec, full_spec, pl.BlockSpec((2, CHUNK, K_WIN), lambda p, i: (p, 0, 0))],
        out_shape=[jax.ShapeDtypeStruct((t, D_ATTN), BF16), jax.ShapeDtypeStruct((t + K_PAD, D_ATTN), F32),
                   jax.ShapeDtypeStruct((t + K_PAD, D_ATTN), F32),
                   jax.ShapeDtypeStruct((N_HEADS, CHUNK, K_WIN), F32)],
        compiler_params=_cparams(2),
    )(proj, *([kp] * n_win), *([vp] * n_win), bias, att, lse, datt)


CONV_LEAD = CONV_HALO - (CONV_WIDTH - 1)
CONV_LANES = 128
CONV_ROWS = 64


def _conv_specs(t):
    tt = _row_tile(t, CONV_TILE)
    per = tt // CONV_HALO
    n_halo = t // CONV_HALO
    tile = lambda cb: pl.BlockSpec((tt, COL), functools.partial(lambda i, cb: (i, cb), cb=cb))
    prev = lambda cb: pl.BlockSpec((CONV_HALO, COL),
                                   functools.partial(lambda i, cb: (jnp.maximum(i * per - 1, 0), cb), cb=cb))
    nxt = lambda cb: pl.BlockSpec((CONV_HALO, COL),
                                  functools.partial(lambda i, cb: (jnp.minimum((i + 1) * per, n_halo - 1), cb), cb=cb))
    vec = lambda shape: pl.BlockSpec(shape, lambda i: (0, 0))
    return tt, tile, prev, nxt, vec


def _glu(ca, cg, bias):
    return (ca.astype(F32) + bias[:, :D_CONV]) * jax.nn.sigmoid(cg.astype(F32) + bias[:, D_CONV:])


SUBLANES = 8


def _shift_copies(ext_ref):
    n = ext_ref.shape[1] - SUBLANES
    for s in range(1, SUBLANES):
        ext_ref[s, 0:n, :] = ext_ref[0, s:s + n, :]


def _window(ext_ref, offset, rows, lanes=slice(None)):
    s = offset % SUBLANES
    return ext_ref[s, offset - s:offset - s + rows, lanes]


def _taps(ext_ref, tt, first_row, weight_of, out_ref):
    for r0 in range(0, tt, CONV_ROWS):
        for l0 in range(0, D_CONV, CONV_LANES):
            lanes = slice(l0, l0 + CONV_LANES)
            acc = jnp.zeros((CONV_ROWS, CONV_LANES), F32)
            for w in range(CONV_WIDTH):
                acc = acc + _window(ext_ref, first_row(w) + r0, CONV_ROWS, lanes) * weight_of(w)[:, lanes]
            out_ref[r0:r0 + CONV_ROWS, lanes] = acc


def _conv_fwd(proj, glu_bias, dw, dw_b, ln_g, ln_b):
    t = proj.shape[0]
    tt, tile, prev, nxt, vec = _conv_specs(t)
    ca_blk, cg_blk = 3 * D_ATTN // COL, 3 * D_ATTN // COL + 1

    def body(ca_ref, cg_ref, pa_ref, pg_ref, gb_ref, dw_ref, dwb_ref, g_ref, b_ref, cs_ref, c_ref, z_ref, ext_ref):
        i = pl.program_id(0)
        bias = gb_ref[...]
        c = _glu(ca_ref[...], cg_ref[...], bias)
        halo = _glu(pa_ref[...], pg_ref[...], bias)
        ext_ref[0, 0:CONV_HALO, :] = jnp.where(i == 0, 0.0, halo)
        ext_ref[0, CONV_HALO:, :] = c
        _shift_copies(ext_ref)
        c_ref[...] = c
        _taps(ext_ref, tt, lambda w: CONV_LEAD + w, lambda w: dw_ref[w:w + 1, :], z_ref)
        z = z_ref[...] + dwb_ref[...]
        z_ref[...] = z
        mu = jnp.mean(z, axis=-1, keepdims=True)
        zc = z - mu
        y = zc * lax.rsqrt(jnp.mean(zc * zc, axis=-1, keepdims=True) + EPS) * g_ref[...] + b_ref[...]
        cs_ref[...] = (y * jax.nn.sigmoid(y)).astype(cs_ref.dtype)

    out_spec = pl.BlockSpec((tt, D_CONV), lambda i: (i, 0))
    return pl.pallas_call(
        body, name="conv_fwd", grid=(t // tt,),
        in_specs=[tile(ca_blk), tile(cg_blk), prev(ca_blk), prev(cg_blk), vec(glu_bias.shape), vec(dw.shape),
                  vec(dw_b.shape), vec(ln_g.shape), vec(ln_b.shape)],
        out_specs=[out_spec] * 3,
        out_shape=[jax.ShapeDtypeStruct((t, D_CONV), BF16), jax.ShapeDtypeStruct((t, D_CONV), F32),
                   jax.ShapeDtypeStruct((t, D_CONV), F32)],
        scratch_shapes=[pltpu.VMEM((SUBLANES, tt + CONV_HALO, D_CONV), F32)], compiler_params=_cparams(1),
    )(proj, proj, proj, proj, glu_bias, dw, dw_b, ln_g, ln_b)


def _conv_bwd(proj, c, z, dcs, glu_bias, dw, ln_g, ln_b):
    t = proj.shape[0]
    tt, tile, prev, nxt, vec = _conv_specs(t)
    n_tiles = t // tt
    ca_blk, cg_blk = 3 * D_ATTN // COL, 3 * D_ATTN // COL + 1

    def ln_bwd(zv, dcsv, g, b):
        mu = jnp.mean(zv, axis=-1, keepdims=True)
        zc = zv - mu
        rstd = lax.rsqrt(jnp.mean(zc * zc, axis=-1, keepdims=True) + EPS)
        zhat = zc * rstd
        y = zhat * g + b
        sig = jax.nn.sigmoid(y)
        dy = dcsv * sig * (1.0 + y * (1.0 - sig))
        dzh = dy * g
        dz = rstd * (dzh - jnp.mean(dzh, axis=-1, keepdims=True) - zhat * jnp.mean(dzh * zhat, axis=-1, keepdims=True))
        return dz, dy, zhat

    def body(ca_ref, cg_ref, c_ref, cprev_ref, z_ref, znext_ref, dcs_ref, dcsnext_ref, gb_ref, dw_ref, g_ref, b_ref,
             dcin_ref, ddw_ref, ddwb_ref, dg_ref, db_ref, dgb_ref, cext_ref, dzext_ref, dc_ref):
        i = pl.program_id(0)

        @pl.when(i == 0)
        def _():
            for ref in (ddw_ref, ddwb_ref, dg_ref, db_ref, dgb_ref):
                ref[...] = jnp.zeros_like(ref)

        g, b = g_ref[...], b_ref[...]
        dz, dy, zhat = ln_bwd(z_ref[...], dcs_ref[...], g, b)
        dz_next, _, _ = ln_bwd(znext_ref[...], dcsnext_ref[...], g, b)
        dg_ref[...] += _colsum(dy * zhat)
        db_ref[...] += _colsum(dy)
        ddwb_ref[...] += _colsum(dz)
        dzext_ref[0, 0:tt, :] = dz
        dzext_ref[0, tt:, :] = jnp.where(i == n_tiles - 1, 0.0, dz_next)
        _shift_copies(dzext_ref)
        cext_ref[0, 0:CONV_HALO, :] = jnp.where(i == 0, 0.0, cprev_ref[...])
        cext_ref[0, CONV_HALO:, :] = c_ref[...]
        _shift_copies(cext_ref)
        _taps(dzext_ref, tt, lambda w: CONV_WIDTH - 1 - w, lambda w: dw_ref[w:w + 1, :], dc_ref)
        for w in range(CONV_WIDTH):
            ddw_ref[w:w + 1, :] += _colsum(_window(cext_ref, CONV_LEAD + w, tt) * dz)
        bias = gb_ref[...]
        a_in = ca_ref[...].astype(F32) + bias[:, :D_CONV]
        sg = jax.nn.sigmoid(cg_ref[...].astype(F32) + bias[:, D_CONV:])
        dc = dc_ref[...]
        dcin = jnp.concatenate([dc * sg, dc * a_in * sg * (1.0 - sg)], axis=1)
        dcin_ref[...] = dcin.astype(dcin_ref.dtype)
        dgb_ref[...] += _colsum(dcin)

    row = lambda: pl.BlockSpec((tt, D_CONV), lambda i: (i, 0))
    per = tt // CONV_HALO
    n_halo = t // CONV_HALO
    prev_row = pl.BlockSpec((CONV_HALO, D_CONV), lambda i: (jnp.maximum(i * per - 1, 0), 0))
    next_row = lambda: pl.BlockSpec((CONV_HALO, D_CONV), lambda i: (jnp.minimum((i + 1) * per, n_halo - 1), 0))
    acc = lambda shape: pl.BlockSpec(shape, lambda i: (0, 0))
    return pl.pallas_call(
        body, name="conv_bwd", grid=(n_tiles,),
        in_specs=[tile(ca_blk), tile(cg_blk), row(), prev_row, row(), next_row(), row(), next_row(),
                  vec(glu_bias.shape), vec(dw.shape), vec(ln_g.shape), vec(ln_b.shape)],
        out_specs=[pl.BlockSpec((tt, 2 * D_CONV), lambda i: (i, 0)), acc(dw.shape), acc((1, D_CONV)),
                   acc((1, D_CONV)), acc((1, D_CONV)), acc((1, 2 * D_CONV))],
        out_shape=[jax.ShapeDtypeStruct((t, 2 * D_CONV), BF16), jax.ShapeDtypeStruct(dw.shape, F32),
                   jax.ShapeDtypeStruct((1, D_CONV), F32), jax.ShapeDtypeStruct((1, D_CONV), F32),
                   jax.ShapeDtypeStruct((1, D_CONV), F32), jax.ShapeDtypeStruct((1, 2 * D_CONV), F32)],
        scratch_shapes=[pltpu.VMEM((SUBLANES, tt + CONV_HALO, D_CONV), F32),
                        pltpu.VMEM((SUBLANES, tt + CONV_HALO, D_CONV), F32), pltpu.VMEM((tt, D_CONV), F32)],
        compiler_params=_cparams(1),
    )(proj, proj, c, c, z, z, dcs, dcs, glu_bias, dw, ln_g, ln_b)


def _place():
    x, y, c = lax.axis_index("x"), lax.axis_index("y"), lax.axis_index("c")
    chips = [(1 - x, y), (x, 1 - y), (1 - x, 1 - y)]
    return x, y, c, chips


def _chip_index(chip):
    return 2 * chip[0] + chip[1]


def _half_rows(c, half):
    return pl.ds(pl.multiple_of(c * half, 16), half)


def _gather_carry(blocked):
    n = len(blocked)

    def over_ici(o_refs, send_sems, recv_sems):
        x, y, c, chips = _place()
        me = _chip_index((x, y))
        copies = []
        for a in range(n):
            mine = o_refs[a].at[me, _half_rows(c, o_refs[a].shape[1] // 2), :]
            for k, chip in enumerate(chips):
                copies.append(pltpu.make_async_remote_copy(
                    src_ref=mine, dst_ref=mine, send_sem=send_sems.at[6 * a + k], recv_sem=recv_sems.at[6 * a + k],
                    device_id=(chip[0], chip[1], c), device_id_type=MESH))
        return copies

    def to_sibling(o_refs, send_sems, recv_sems, sent_by_me):
        x, y, c, chips = _place()
        copies = []
        for a in range(n):
            rows = _half_rows(c if sent_by_me else 1 - c, o_refs[a].shape[1] // 2)
            for k, chip in enumerate(chips):
                landed = o_refs[a].at[_chip_index(chip), rows, :]
                copies.append(pltpu.make_async_remote_copy(
                    src_ref=landed, dst_ref=landed, send_sem=send_sems.at[6 * a + 3 + k],
                    recv_sem=recv_sems.at[6 * a + 3 + k], device_id=(x, y, 1 - c), device_id_type=MESH))
        return copies

    def start(ins, outs, sems):
        for cp in over_ici(outs, *sems):
            cp.start()

    def hand_on(ins, outs, sems):
        for arrived, onward in zip(over_ici(outs, *sems), to_sibling(outs, *sems, True)):
            arrived.wait_recv()
            onward.start()

    def finish(ins, outs, sems):
        for cp in to_sibling(outs, *sems, False):
            cp.wait_recv()
        for cp in over_ici(outs, *sems) + to_sibling(outs, *sems, True):
            cp.wait_send()

    return _Carry(
        ins=list(blocked), outs=[jax.ShapeDtypeStruct(w.shape, w.dtype) for w in blocked],
        aliases={a: a for a in range(n)},
        sems=[pltpu.SemaphoreType.DMA((6 * n,)), pltpu.SemaphoreType.DMA((6 * n,))],
        phases=[("first", start), ("late", hand_on), ("last", finish)])


def _pair_exchange(name, grads):
    n = len(grads)

    def body(*refs):
        g_refs, land_refs = refs[:n], refs[n:2 * n]
        send_sems, recv_sems = refs[2 * n:]
        x, y, c, _ = _place()
        copies = []
        for a in range(n):
            half = g_refs[a].shape[1] // 2
            cp = pltpu.make_async_remote_copy(
                src_ref=g_refs[a].at[:, _half_rows(1 - c, half), :], dst_ref=land_refs[a],
                send_sem=send_sems.at[a], recv_sem=recv_sems.at[a], device_id=(x, y, 1 - c), device_id_type=MESH)
            cp.start()
            copies.append(cp)
        for cp in copies:
            cp.wait()

    return pl.pallas_call(
        body, name=name, in_specs=[ANY] * n, out_specs=[ANY] * n,
        out_shape=[jax.ShapeDtypeStruct((g.shape[0], g.shape[1] // 2, g.shape[2]), g.dtype) for g in grads],
        scratch_shapes=[pltpu.SemaphoreType.DMA((n,)), pltpu.SemaphoreType.DMA((n,))],
    )(*grads)


def _to_owner_carry(parts):
    n = len(parts)

    def sends(p_refs, l_refs, send_sems, recv_sems):
        x, y, c, chips = _place()
        me = _chip_index((x, y))
        return [pltpu.make_async_remote_copy(
            src_ref=p_refs[a].at[_chip_index(chip)], dst_ref=l_refs[a].at[me],
            send_sem=send_sems.at[3 * a + k], recv_sem=recv_sems.at[3 * a + k],
            device_id=(chip[0], chip[1], c), device_id_type=MESH) for a in range(n) for k, chip in enumerate(chips)]

    def start(ins, outs, sems):
        for cp in sends(ins, outs, *sems):
            cp.start()

    def finish(ins, outs, sems):
        x, y, c, chips = _place()
        send_sems, recv_sems = sems
        for a in range(n):
            for k, chip in enumerate(chips):
                slot = outs[a].at[_chip_index(chip)]
                pltpu.make_async_remote_copy(
                    src_ref=slot, dst_ref=slot, send_sem=send_sems.at[3 * a + k], recv_sem=recv_sems.at[3 * a + k],
                    device_id=(chip[0], chip[1], c), device_id_type=MESH).wait_recv()
        for cp in sends(ins, outs, *sems):
            cp.wait_send()

    return _Carry(
        ins=list(parts), outs=[jax.ShapeDtypeStruct(p.shape, p.dtype) for p in parts], aliases={},
        sems=[pltpu.SemaphoreType.DMA((3 * n,)), pltpu.SemaphoreType.DMA((3 * n,))],
        phases=[("first", start), ("last", finish)])


def _swap_halves(halves):
    n = len(halves)

    def body(*refs):
        h_refs, o_refs = refs[:n], refs[n:2 * n]
        send_sems, recv_sems = refs[2 * n:]
        x, y, c, _ = _place()
        copies = []
        for a in range(n):
            cp = pltpu.make_async_remote_copy(
                src_ref=h_refs[a], dst_ref=o_refs[a], send_sem=send_sems.at[a], recv_sem=recv_sems.at[a],
                device_id=(x, y, 1 - c), device_id_type=MESH)
            cp.start()
            copies.append(cp)
        for cp in copies:
            cp.wait()

    return pl.pallas_call(
        body, name="grad_swap_halves", in_specs=[ANY] * n, out_specs=[ANY] * n,
        out_shape=[jax.ShapeDtypeStruct(h.shape, h.dtype) for h in halves],
        scratch_shapes=[pltpu.SemaphoreType.DMA((n,)), pltpu.SemaphoreType.DMA((n,))],
    )(*halves)


def _all_devices(name, block):
    r, cols = block.shape

    def body(b_ref, all_ref, sum_ref, send_sems, recv_sems):
        x, y, c, _ = _place()
        me = 4 * x + 2 * y + c
        all_ref[me] = b_ref[...]
        flips = [(fx, fy, fc) for fx in (0, 1) for fy in (0, 1) for fc in (0, 1)][1:]
        copies = []
        for k, (fx, fy, fc) in enumerate(flips):
            cp = pltpu.make_async_remote_copy(
                src_ref=b_ref, dst_ref=all_ref.at[me], send_sem=send_sems.at[k], recv_sem=recv_sems.at[k],
                device_id=(x ^ fx, y ^ fy, c ^ fc), device_id_type=MESH)
            cp.start()
            copies.append(cp)
        for k, (fx, fy, fc) in enumerate(flips):
            slot = all_ref.at[4 * (x ^ fx) + 2 * (y ^ fy) + (c ^ fc)]
            pltpu.make_async_remote_copy(
                src_ref=slot, dst_ref=slot, send_sem=send_sems.at[k], recv_sem=recv_sems.at[k],
                device_id=(x ^ fx, y ^ fy, c ^ fc), device_id_type=MESH).wait_recv()
        for cp in copies:
            cp.wait_send()
        acc = all_ref[0]
        for d in range(1, N_DEV):
            acc = acc + all_ref[d]
        sum_ref[...] = acc

    vmem = pl.BlockSpec(memory_space=pltpu.VMEM)
    return pl.pallas_call(
        body, name=name, in_specs=[vmem], out_specs=[vmem, vmem],
        out_shape=[jax.ShapeDtypeStruct((N_DEV, r, cols), F32), jax.ShapeDtypeStruct((r, cols), F32)],
        scratch_shapes=[pltpu.SemaphoreType.DMA((N_DEV - 1,)), pltpu.SemaphoreType.DMA((N_DEV - 1,))],
    )(block)


PACK = 1024


def _packed_rows(shape, width):
    size, last = int(np.prod(shape)), shape[-1]
    cols = last if last <= width else width
    assert size % cols == 0
    return size // cols, cols


def _pack(vals, width=PACK):
    rows = []
    for v in vals:
        n_rows, cols = _packed_rows(v.shape, width)
        rows.append(jnp.pad(v.reshape(n_rows, cols).astype(F32), ((0, 0), (0, width - cols))))
    buf = jnp.concatenate(rows, axis=0)
    return jnp.pad(buf, ((0, (-buf.shape[0]) % 8), (0, 0)))


def _unpack(buf, shapes, width=PACK):
    out, r = [], 0
    for shape in shapes:
        n_rows, cols = _packed_rows(shape, width)
        out.append(buf[r:r + n_rows, :cols].reshape(shape))
        r += n_rows
    return out


def _ffn_hidden(name, n, wg, wu, tm=1024, carry=None):
    m, k = n.shape
    nj, fb, _ = wg.shape
    tm = _row_tile(m, tm)

    def core(n_ref, wg_ref, wu_ref, a_ref, b_ref, s_ref):
        nv = n_ref[...]
        a = _dot(nv, wg_ref[...], True)
        b = _dot(nv, wu_ref[...], True)
        a_ref[...] = a.astype(a_ref.dtype)
        b_ref[...] = b.astype(b_ref.dtype)
        s_ref[...] = (a * jax.nn.sigmoid(a) * b).astype(s_ref.dtype)

    w_spec = pl.BlockSpec((None, fb, k), lambda j, i: (j, 0, 0))
    out_spec = pl.BlockSpec((None, tm, fb), lambda j, i: (j, i, 0))
    return _call(name, core, (nj, m // tm), [pl.BlockSpec((tm, k), lambda j, i: (i, 0)), w_spec, w_spec],
                 [out_spec] * 3, [jax.ShapeDtypeStruct((nj, m, fb), BF16)] * 3, [], [n, wg, wu], carry)


def _ffn_d_hidden(name, df, wd, a, b, tm=512):
    m, k = df.shape
    nj, fb, _ = wd.shape
    tm = _row_tile(m, tm)

    def body(df_ref, wd_ref, a_ref, b_ref, da_ref, db_ref):
        dfv = df_ref[...]
        for j in range(nj):
            ds = _dot(dfv, wd_ref[j], True)
            av, bv = a_ref[j].astype(F32), b_ref[j].astype(F32)
            sig = jax.nn.sigmoid(av)
            da_ref[j] = (ds * bv * sig * (1.0 + av * (1.0 - sig))).astype(da_ref.dtype)
            db_ref[j] = (ds * av * sig).astype(db_ref.dtype)

    blk = pl.BlockSpec((nj, tm, fb), lambda i: (0, i, 0))
    return pl.pallas_call(
        body, name=name, grid=(m // tm,),
        in_specs=[pl.BlockSpec((tm, k), lambda i: (i, 0)), _resident(wd), blk, blk],
        out_specs=[blk, blk], out_shape=[jax.ShapeDtypeStruct((nj, m, fb), BF16)] * 2,
        compiler_params=_cparams(1),
    )(df, wd, a, b)


def kernel(x, ffn1_norm_pre, ffn1_w_gate, ffn1_w_up, ffn1_w_down, ffn1_norm_post, mix_norm_pre, w_in, gate_bias, rel_table, w_attn_out, conv_glu_bias, conv_dw_w, conv_dw_b, conv_ln_g, conv_ln_b, conv_w_out, w_out, mix_norm_post, ffn2_norm_pre, ffn2_w_gate, ffn2_w_up, ffn2_w_down, ffn2_norm_post, loss_target, m_ffn1_norm_pre, m_ffn1_w_gate, m_ffn1_w_up, m_ffn1_w_down, m_ffn1_norm_post, m_mix_norm_pre, m_w_in, m_gate_bias, m_rel_table, m_w_attn_out, m_conv_glu_bias, m_conv_dw_w, m_conv_dw_b, m_conv_ln_g, m_conv_ln_b, m_conv_w_out, m_w_out, m_mix_norm_post, m_ffn2_norm_pre, m_ffn2_w_gate, m_ffn2_w_up, m_ffn2_w_down, m_ffn2_norm_post, v_ffn1_norm_pre, v_ffn1_w_gate, v_ffn1_w_up, v_ffn1_w_down, v_ffn1_norm_post, v_mix_norm_pre, v_w_in, v_gate_bias, v_rel_table, v_w_attn_out, v_conv_glu_bias, v_conv_dw_w, v_conv_dw_b, v_conv_ln_g, v_conv_ln_b, v_conv_w_out, v_w_out, v_mix_norm_post, v_ffn2_norm_pre, v_ffn2_w_gate, v_ffn2_w_up, v_ffn2_w_down, v_ffn2_norm_post):
    args = dict(locals())
    names = ['ffn1_norm_pre', 'ffn1_w_gate', 'ffn1_w_up', 'ffn1_w_down', 'ffn1_norm_post', 'mix_norm_pre', 'w_in',
             'gate_bias', 'rel_table', 'w_attn_out', 'conv_glu_bias', 'conv_dw_w', 'conv_dw_b', 'conv_ln_g',
             'conv_ln_b', 'conv_w_out', 'w_out', 'mix_norm_post', 'ffn2_norm_pre', 'ffn2_w_gate', 'ffn2_w_up',
             'ffn2_w_down', 'ffn2_norm_post']
    big = ['ffn1_w_gate', 'ffn1_w_up', 'ffn1_w_down', 'w_in', 'w_attn_out', 'conv_w_out', 'w_out', 'ffn2_w_gate',
           'ffn2_w_up', 'ffn2_w_down']
    small = [n for n in names if n not in big]

    xs, target = x[0], loss_target[0]
    t, d = xs.shape
    cx, cy = lax.axis_index("x"), lax.axis_index("y")
    chip = 2 * cx + cy

    dw_shard = conv_dw_w[0, :, 0, :]
    cshard = dw_shard.shape[1]
    dw_all, _ = _all_devices("gather_dw", _pack([dw_shard], width=cshard))
    dw_full = jnp.concatenate([dw_all[2 * j, :CONV_WIDTH, :cshard] for j in range(N_CHIPS)], axis=1)
    dw_full = jnp.pad(dw_full, ((0, CONV_HALO - CONV_WIDTH), (0, 0)))
    peers = [(1 - cx, cy), (cx, 1 - cy), (1 - cx, 1 - cy)]
    pos = jnp.stack([lax.axis_index("c"), chip] + [_chip_index(p) for p in peers]).astype(jnp.int32)
    transposed = ("ffn1_w_gate", "ffn1_w_up", "ffn2_w_gate", "ffn2_w_up")
    weight_of = lambda n: n[2:] if n[:2] in ("m_", "v_") else n
    shard = lambda n: jnp.transpose(args[n][0]) if weight_of(n) in transposed else args[n][0]
    unshard = lambda n, v: (jnp.transpose(v) if n in transposed else v)[None]
    own = {n: _cast_into("cast_" + n, pos, shard(n)) for n in big}
    gather = lambda *ns: _gather_carry([own[n] for n in ns])
    res_spec = [(d, F32), (d, F32), (d, BF16)]

    n1, (wg1, wu1, wd1) = _rms_fwd("ffn1_pre", xs, ffn1_norm_pre, gather("ffn1_w_gate", "ffn1_w_up", "ffn1_w_down"))
    (a1, b1, s1), (win, wao, wco, wout) = _ffn_hidden(
        "ffn1_hidden", n1, wg1, wu1, carry=gather("w_in", "w_attn_out", "conv_w_out", "w_out"))
    (f1, h1, u), (wg2,) = _mm_kblk(
        "ffn1_down", [(s1, wd1)], trans_w=False, epilogue=_ep_post_res_pre(0.5), rows=[xs],
        vecs=[ffn1_norm_post, mix_norm_pre], row_outs=res_spec, carry=gather("ffn2_w_gate"))
    proj, (wu2, wd2) = _mm_nblk("mix_in", u, win, trans_w=False, out_blocked=False, out_dtype=BF16,
                                carry=gather("ffn2_w_up", "ffn2_w_down"))
    table_pad = jnp.pad(rel_table[0], ((0, 0), (0, REL_PAD - rel_table.shape[2])))
    bias = _bias_expand(table_pad)
    kp = jnp.pad(proj[:, D_ATTN:2 * D_ATTN], ((K_PAD, 0), (0, 0)))
    vp = jnp.pad(proj[:, 2 * D_ATTN:3 * D_ATTN], ((K_PAD, 0), (0, 0)))
    att, lse = _attn_fwd(proj, kp, vp, bias)
    cs, c_glu, z_conv = _conv_fwd(proj, conv_glu_bias, dw_full, conv_dw_b, conv_ln_g, conv_ln_b)
    y_a, y_b, merged = _mix_merge(att, cs, wao, wco, proj, gate_bias)
    (mo, h2, n2), _ = _mm_kblk(
        "mix_out", [(merged, wout)], trans_w=False, epilogue=_ep_post_res_pre(1.0), rows=[h1],
        vecs=[mix_norm_post, ffn2_norm_pre], row_outs=res_spec)
    (a2, b2, s2), _ = _ffn_hidden("ffn2_hidden", n2, wg2, wu2)
    g = {}
    (dy, df2, err2, g["ffn2_norm_post"]), _ = _mm_kblk(
        "ffn2_down", [(s2, wd2)], trans_w=False, epilogue=_ep_loss(0.5, d), rows=[h2, target],
        vecs=[ffn2_norm_post], row_outs=[(d, F32), (d, BF16)], vec_outs=[d, d])
    loss = lax.psum(0.5 * jnp.sum(err2) / d, ("x", "y", "c"))

    parts, landed = {}, {}

    def ffn_bwd(tag, df, n, a, b, s, wg, wu, wd, **epilogue):
        da, db = _ffn_d_hidden(tag + "_d_hidden", df, wd, a, b)
        group = [tag + "_w_down", tag + "_w_gate", tag + "_w_up"]
        local = [_mm_tn(tag + "_g_down", s, "blk", df, "full"), _mm_tn(tag + "_g_gate", da, "blk", n, "full"),
                 _mm_tn(tag + "_g_up", db, "blk", n, "full")]
        return _mm_kblk(tag + "_d_n", [(da, wg), (db, wu)], trans_w=False, carry=pair_sums(tag, group, local),
                        **epilogue), group

    def pair_sums(tag, group, local):
        theirs = _pair_exchange("pair_" + tag, local)
        for n, mine, other in zip(group, local, theirs):
            parts[n] = _add_pair("pair_sum_" + n, pos, mine, other)
        return _to_owner_carry([parts[n] for n in group])

    def keep(group, carried):
        for n, val in zip(group, carried):
            landed[n] = val

    ((dh2, dmo, g["ffn2_norm_pre"], g["mix_norm_post"]), carried), group = ffn_bwd(
        "ffn2", df2, n2, a2, b2, s2, wg2, wu2, wd2, epilogue=_ep_pre_bwd_post(1.0), rows=[h2, dy, mo],
        vecs=[ffn2_norm_pre, mix_norm_post], row_outs=[(d, F32), (d, BF16)], vec_outs=[d, d])
    keep(group, carried)
    g_wout = _mm_tn("mix_g_out", merged, "col", dmo, "full")
    dy_a, dy_b, dgates, datt, dcs, g["gate_bias"] = _mix_d_merge(dmo, wout, y_a, y_b, wao, wco, proj, gate_bias)
    g_wao = _mm_tn("attn_g_out", att, "full", dy_a, "col")
    g_wco = _mm_tn("conv_g_out", cs, "full", dy_b, "col")
    dq, dkp, dvp, dbias = _attn_bwd(proj, kp, vp, bias, att, lse, datt)
    g["rel_table"] = _bias_fold(dbias)[:, :rel_table.shape[2]]
    dcin, g_dw, g["conv_dw_b"], g["conv_ln_g"], g["conv_ln_b"], g["conv_glu_bias"] = _conv_bwd(
        proj, c_glu, z_conv, dcs, conv_glu_bias, dw_full, conv_ln_g, conv_ln_b)
    dproj = jnp.concatenate([dq, dkp[K_PAD:].astype(BF16), dvp[K_PAD:].astype(BF16), dcin, dgates], axis=1)
    g_win = _mm_tn("mix_g_in", u, "full", dproj, "col")
    group = ["w_out", "w_attn_out", "conv_w_out", "w_in"]
    (dh1, df1, g["mix_norm_pre"], g["ffn1_norm_post"]), carried = _mm_kblk(
        "mix_d_in", [(dproj, win)], trans_w=True, epilogue=_ep_pre_bwd_post(0.5), rows=[h1, dh2, f1],
        vecs=[mix_norm_pre, ffn1_norm_post], row_outs=[(d, F32), (d, BF16)], vec_outs=[d, d],
        carry=pair_sums("mix", group, [g_wout, g_wao, g_wco, g_win]))
    keep(group, carried)
    ((grad_x, g["ffn1_norm_pre"]), carried), group = ffn_bwd(
        "ffn1", df1, n1, a1, b1, s1, wg1, wu1, wd1, epilogue=_ep_pre_bwd_first(), rows=[xs, dh1],
        vecs=[ffn1_norm_pre], row_outs=[(d, F32)], vec_outs=[d])
    keep(group, carried)

    halves = [_add_chips("chip_sum_" + n, pos, parts[n], landed[n]) for n in big]
    other_halves = _swap_halves(halves)

    g["conv_dw_w"] = g_dw[:CONV_WIDTH]
    _, small_sum = _all_devices("sum_small", _pack([g[n] for n in small]))
    for n, val in zip(small, _unpack(small_sum, [g[n].shape for n in small])):
        g[n] = val
    g["conv_dw_w"] = lax.dynamic_slice_in_dim(g["conv_dw_w"], chip * cshard, cshard, axis=1)

    grads, deltas, new_m, new_v = {}, {}, {}, {}
    for n, mine, other in zip(big, halves, other_halves):
        gr, dl, m2, v2 = _adamw_halves("adamw_" + n, pos, shard(n), shard("m_" + n), shard("v_" + n), mine, other)
        grads[n], deltas[n], new_m[n], new_v[n] = unshard(n, gr), unshard(n, dl), unshard(n, m2), unshard(n, v2)
    shapes = [g[n].shape for n in small]
    packed = lambda pre: _pack([args[pre + n].reshape(shp) for n, shp in zip(small, shapes)])
    dl, m2, v2 = _adamw("adamw_small", packed(""), _pack([g[n] for n in small]), packed("m_"), packed("v_"))
    for n, a_, b_, c_ in zip(small, _unpack(dl, shapes), _unpack(m2, shapes), _unpack(v2, shapes)):
        shape = args[n].shape
        grads[n], deltas[n], new_m[n], new_v[n] = (g[n].reshape(shape), a_.reshape(shape), b_.reshape(shape),
                                                   c_.reshape(shape))

    return (loss, grad_x[None], *[grads[n] for n in names], *[deltas[n] for n in names],
            *[new_m[n] for n in names], *[new_v[n] for n in names])
```

```python
import functools

import numpy as np
import jax
import jax.numpy as jnp
from jax import lax
from jax.experimental import pallas as pl
from jax.experimental.pallas import tpu as pltpu

F32 = jnp.float32
BF16 = jnp.bfloat16
MESH = pl.DeviceIdType.MESH
ANY = pl.BlockSpec(memory_space=pl.ANY)

EPS = 1e-6
CHUNK = 64
LEFT_CHUNKS = 8
N_HEADS = 8
HEAD_DIM = 64
D_ATTN = N_HEADS * HEAD_DIM
D_CONV = 512
CONV_WIDTH = 31
REL_CLIP = 128
N_CHIPS = 4
N_DEV = 8
Q_BLOCK = 4 * CHUNK
K_PAD = LEFT_CHUNKS * CHUNK
K_WIN = K_PAD + Q_BLOCK
REL_EXT = 1024
REL_PAD = 384
CONV_HALO = 32
CONV_TILE = 256
COL = 512
NEG = -1e30

ADAM_LR = 0.001
ADAM_B1 = 0.9
ADAM_B2 = 0.999
ADAM_EPS = 1e-08
ADAM_WD = 0.01
ADAM_STEP = 10

VMEM_LIMIT_BYTES = 56 * 1024 * 1024


def _cparams(n_grid):
    return pltpu.CompilerParams(dimension_semantics=("arbitrary",) * n_grid, vmem_limit_bytes=VMEM_LIMIT_BYTES)


def _row_tile(rows, want):
    if rows <= want:
        return rows
    for t in range(want - want % 16, 0, -16):
        if rows % t == 0:
            return t
    raise ValueError((rows, want))


def _dot(a, w, trans_w):
    dims = (((1,), (1,)), ((), ())) if trans_w else (((1,), (0,)), ((), ()))
    return lax.dot_general(a, w, dims, preferred_element_type=F32)


class _Carry:
    LATE_STEPS = 2

    def __init__(self, ins, outs, aliases, sems, phases):
        self.ins, self.outs, self.aliases, self.sems, self.phases = ins, outs, aliases, sems, phases


def _call(name, core, grid, in_specs, out_specs, out_shape, scratch, args, carry=None):
    n_in, n_out, n_scr = len(in_specs), len(out_specs), len(scratch)
    if carry is None:
        out = pl.pallas_call(core, name=name, grid=grid, in_specs=in_specs, out_specs=out_specs, out_shape=out_shape,
                             scratch_shapes=scratch, compiler_params=_cparams(len(grid)))(*args)
        return list(out), []
    c_in, c_out = len(carry.ins), len(carry.outs)
    total = int(np.prod(grid))
    late = max(total - 1 - _Carry.LATE_STEPS, 0)

    def body(*refs):
        ins, refs = refs[:n_in], refs[n_in:]
        c_ins, refs = refs[:c_in], refs[c_in:]
        outs, refs = refs[:n_out], refs[n_out:]
        c_outs, refs = refs[:c_out], refs[c_out:]
        scr, c_sems = refs[:n_scr], refs[n_scr:]
        step = pl.program_id(0)
        for axis in range(1, len(grid)):
            step = step * grid[axis] + pl.program_id(axis)

        def run(when, at):
            for w, fn in carry.phases:
                if w == when:
                    pl.when(step == at)(functools.partial(fn, c_ins, c_outs, c_sems))

        run("first", 0)
        core(*ins, *outs, *scr)
        run("late", late)
        run("last", total - 1)

    out = pl.pallas_call(
        body, name=name, grid=grid, in_specs=list(in_specs) + [ANY] * c_in, out_specs=list(out_specs) + [ANY] * c_out,
        out_shape=list(out_shape) + list(carry.outs), scratch_shapes=list(scratch) + list(carry.sems),
        input_output_aliases={n_in + a: n_out + b for a, b in carry.aliases.items()},
        compiler_params=_cparams(len(grid)),
    )(*args, *carry.ins)
    return list(out[:n_out]), list(out[n_out:])


def _mm_nblk(name, a, w, *, trans_w, out_blocked, out_dtype, tm=1024, carry=None):
    m, k = a.shape
    nj = w.shape[0]
    nb = w.shape[1] if trans_w else w.shape[2]
    tm = _row_tile(m, tm)

    def core(a_ref, w_ref, o_ref):
        o_ref[...] = _dot(a_ref[...], w_ref[...], trans_w).astype(o_ref.dtype)

    if out_blocked:
        out_shape, out_spec = (nj, m, nb), pl.BlockSpec((None, tm, nb), lambda j, i: (j, i, 0))
    else:
        out_shape, out_spec = (m, nj * nb), pl.BlockSpec((tm, nb), lambda j, i: (i, j))
    out, carried = _call(
        name, core, (nj, m // tm),
        [pl.BlockSpec((tm, k), lambda j, i: (i, 0)), pl.BlockSpec((None,) + w.shape[1:], lambda j, i: (j, 0, 0))],
        [out_spec], [jax.ShapeDtypeStruct(out_shape, out_dtype)], [], [a, w], carry)
    return out[0] if carry is None else (out[0], carried)


def _mm_kblk(name, pairs, *, trans_w, out_dtype=F32, tm=512, sub=256, epilogue=None, rows=(), vecs=(), row_outs=None,
             vec_outs=(), carry=None):
    w0 = pairs[0][1]
    nj = w0.shape[0]
    n = w0.shape[1] if trans_w else w0.shape[2]
    kb = w0.shape[2] if trans_w else w0.shape[1]
    m = pairs[0][0].shape[-2]
    tm = _row_tile(m, tm)
    ts = _row_tile(tm, sub)
    n_pairs, n_rows, n_vecs = len(pairs), len(rows), len(vecs)
    if epilogue is None:
        epilogue, row_outs = (lambda acc, r, v: ([acc], [])), [(n, out_dtype)]
    n_ro, n_vo = len(row_outs), len(vec_outs)

    def core(*refs):
        pair_refs, refs = refs[:2 * n_pairs], refs[2 * n_pairs:]
        row_refs, refs = refs[:n_rows], refs[n_rows:]
        vec_refs, refs = refs[:n_vecs], refs[n_vecs:]
        ro_refs, vo_refs = refs[:n_ro], refs[n_ro:]
        if n_vo:
            @pl.when(pl.program_id(0) == 0)
            def _():
                for ref in vo_refs:
                    ref[...] = jnp.zeros_like(ref)

        vec_vals = [v[...] for v in vec_refs]
        sums = None
        for r0 in range(0, tm, ts):
            sub_rows = slice(r0, r0 + ts)
            acc = None
            for p in range(n_pairs):
                a_ref, w_ref = pair_refs[2 * p], pair_refs[2 * p + 1]
                for j in range(nj):
                    a_blk = a_ref[j, sub_rows, :] if len(a_ref.shape) == 3 else a_ref[sub_rows, j * kb:(j + 1) * kb]
                    part = _dot(a_blk, w_ref[j], trans_w)
                    acc = part if acc is None else acc + part
            ro, vo = epilogue(acc, [r[sub_rows, :] for r in row_refs], vec_vals)
            for ref, val in zip(ro_refs, ro):
                ref[sub_rows, :] = val.astype(ref.dtype)
            sums = vo if sums is None else [s + v for s, v in zip(sums, vo)]
        for ref, val in zip(vo_refs, sums or []):
            ref[...] += val

    in_specs, args = [], []
    for a, w in pairs:
        if a.ndim == 3:
            in_specs.append(pl.BlockSpec((nj, tm, kb), lambda i: (0, i, 0)))
        else:
            in_specs.append(pl.BlockSpec((tm, nj * kb), lambda i: (i, 0)))
        in_specs.append(pl.BlockSpec(w.shape, lambda i: (0, 0, 0), pipeline_mode=pl.Buffered(1)))
        args += [a, w]
    in_specs += [pl.BlockSpec((tm, r.shape[1]), lambda i: (i, 0)) for r in rows]
    in_specs += [pl.BlockSpec(v.shape, lambda i: (0, 0)) for v in vecs]
    out_specs = [pl.BlockSpec((tm, cols), lambda i: (i, 0)) for cols, _ in row_outs]
    out_specs += [pl.BlockSpec((1, cols), lambda i: (0, 0)) for cols in vec_outs]
    out_shape = [jax.ShapeDtypeStruct((m, cols), dt) for cols, dt in row_outs]
    out_shape += [jax.ShapeDtypeStruct((1, cols), F32) for cols in vec_outs]
    return _call(name, core, (m // tm,), in_specs, out_specs, out_shape, [], args + list(rows) + list(vecs), carry)


def _mm_tn(name, a, a_mode, b, b_mode, *, out_dtype=BF16, tt=2048):
    nj = N_CHIPS
    t = a.shape[-2]
    tt = _row_tile(t, tt)

    def spec(x, mode):
        if mode == "full":
            return x.shape[1], pl.BlockSpec((tt, x.shape[1]), lambda j, s: (s, 0))
        if mode == "col":
            cb = x.shape[1] // nj
            return cb, pl.BlockSpec((tt, cb), lambda j, s: (s, j))
        return x.shape[2], pl.BlockSpec((None, tt, x.shape[2]), lambda j, s: (j, s, 0))

    ca, a_spec = spec(a, a_mode)
    cb, b_spec = spec(b, b_mode)
    n_steps = t // tt

    def body(a_ref, b_ref, o_ref, acc_ref):
        s = pl.program_id(1)

        @pl.when(s == 0)
        def _():
            acc_ref[...] = jnp.zeros_like(acc_ref)

        acc_ref[...] += lax.dot_general(a_ref[...], b_ref[...], (((0,), (0,)), ((), ())),
                                        preferred_element_type=F32)

        @pl.when(s == n_steps - 1)
        def _():
            o_ref[...] = acc_ref[...].astype(o_ref.dtype)

    return pl.pallas_call(
        body, name=name, grid=(nj, n_steps), in_specs=[a_spec, b_spec],
        out_specs=pl.BlockSpec((None, ca, cb), lambda j, s: (j, 0, 0)),
        out_shape=jax.ShapeDtypeStruct((nj, ca, cb), out_dtype),
        scratch_shapes=[pltpu.VMEM((ca, cb), F32)], compiler_params=_cparams(2),
    )(a, b)


def _rowwise(name, fn, rows, vecs, row_outs, vec_outs, *, tm=256, carry=None):
    nrows = rows[0][0].shape[0]
    tm = _row_tile(nrows, tm)
    n_r, n_v, n_ro, n_vo = len(rows), len(vecs), len(row_outs), len(vec_outs)

    def body(*refs):
        r_vals = [r[...] for r in refs[:n_r]]
        v_vals = [r[...] for r in refs[n_r:n_r + n_v]]
        ro_refs = refs[n_r + n_v:n_r + n_v + n_ro]
        vo_refs = refs[n_r + n_v + n_ro:]
        ro, vo = fn(r_vals, v_vals)
        for ref, val in zip(ro_refs, ro):
            ref[...] = val.astype(ref.dtype)
        if n_vo:
            @pl.when(pl.program_id(0) == 0)
            def _():
                for ref in vo_refs:
                    ref[...] = jnp.zeros_like(ref)

            for ref, val in zip(vo_refs, vo):
                ref[...] += val

    in_specs = [pl.BlockSpec((tm, cols), functools.partial(lambda i, cb: (i, cb), cb=cb)) for _, cols, cb in rows]
    in_specs += [pl.BlockSpec(v.shape, functools.partial(lambda i, nd: (0,) * nd, nd=v.ndim)) for v in vecs]
    out_specs = [pl.BlockSpec((tm, cols), lambda i: (i, 0)) for cols, _ in row_outs]
    out_specs += [pl.BlockSpec((1, cols), lambda i: (0, 0)) for cols in vec_outs]
    out_shape = [jax.ShapeDtypeStruct((nrows, cols), dt) for cols, dt in row_outs]
    out_shape += [jax.ShapeDtypeStruct((1, cols), F32) for cols in vec_outs]
    out, carried = _call(name, body, (nrows // tm,), in_specs, out_specs, out_shape, [],
                         [r[0] for r in rows] + list(vecs), carry)
    return out if carry is None else (out, carried)


def _whole(x):
    return (x, x.shape[1], 0)


def _colsum(x):
    return jnp.sum(x, axis=0, keepdims=True)


def _rstd(x):
    return lax.rsqrt(jnp.mean(x * x, axis=-1, keepdims=True) + EPS)


def _rms_bwd(dn, x, g):
    r = _rstd(x)
    c = dn * g
    dx = r * c - x * (r * r * r) * jnp.mean(c * x, axis=-1, keepdims=True)
    return dx, _colsum(dn * x * r)


def _rms_fwd(name, x, g, carry):
    def fn(r, v):
        (xv,), (gv,) = r, v
        return [xv * _rstd(xv) * gv], []

    (n,), carried = _rowwise(name, fn, [_whole(x)], [g], [(x.shape[1], BF16)], [], carry=carry)
    return n, carried


def _ep_post_res_pre(scale):
    def epilogue(acc, rows, vecs):
        (resid,), (g_post, g_next) = rows, vecs
        h = resid + scale * (acc * _rstd(acc) * g_post)
        return [acc, h, h * _rstd(h) * g_next], []

    return epilogue


def _post_bwd(dh, f, g_post, scale):
    return _rms_bwd(scale * dh, f, g_post)


def _ep_loss(scale, d):
    def epilogue(acc, rows, vecs):
        (resid, target), (g_post,) = rows, vecs
        err = resid + scale * (acc * _rstd(acc) * g_post) - target
        dy = err * (1.0 / d)
        df, dg_post = _post_bwd(dy, acc, g_post, scale)
        return [dy, df], [_colsum(err * err), dg_post]

    return epilogue


def _ep_pre_bwd_post(scale_prev):
    def epilogue(acc, rows, vecs):
        (h, dh_up, f_prev), (g_pre, g_post_prev) = rows, vecs
        dx, dg_pre = _rms_bwd(acc, h, g_pre)
        dh = dh_up + dx
        df, dg_post = _post_bwd(dh, f_prev, g_post_prev, scale_prev)
        return [dh, df], [dg_pre, dg_post]

    return epilogue


def _ep_pre_bwd_first():
    def epilogue(acc, rows, vecs):
        (x, dh_up), (g_pre,) = rows, vecs
        dx, dg_pre = _rms_bwd(acc, x, g_pre)
        return [dh_up + dx], [dg_pre]

    return epilogue


def _gate_specs(d, tm):
    first = (3 * D_ATTN + 2 * D_CONV) // COL
    return [pl.BlockSpec((tm, COL), functools.partial(lambda i, cb: (i, cb), cb=first + p)) for p in range(2 * d // COL)]


def _gate(piece_refs, bias_ref, c0, width):
    p, off = divmod(c0, COL)
    return jax.nn.sigmoid(piece_refs[p][:, off:off + width].astype(F32) + bias_ref[:, c0:c0 + width])


def _resident(w):
    return pl.BlockSpec(w.shape, functools.partial(lambda i, nd: (0,) * nd, nd=w.ndim), pipeline_mode=pl.Buffered(1))


def _mix_merge(att, cs, wao, wco, proj, gate_bias, tm=512):
    t = att.shape[0]
    nj, _, nb = wao.shape
    d = nj * nb
    tm = _row_tile(t, tm)
    gate_specs = _gate_specs(d, tm)
    n_p = len(gate_specs)

    def body(att_ref, cs_ref, wao_ref, wco_ref, *rest):
        pieces, (gb_ref, ya_ref, yb_ref, m_ref) = rest[:n_p], rest[n_p:]
        av, cv = att_ref[...], cs_ref[...]
        for j in range(nj):
            cols = slice(j * nb, (j + 1) * nb)
            ya = _dot(av, wao_ref[j], False)
            yb = _dot(cv, wco_ref[j], False)
            merged = _gate(pieces, gb_ref, j * nb, nb) * ya + _gate(pieces, gb_ref, d + j * nb, nb) * yb
            ya_ref[:, cols] = ya.astype(ya_ref.dtype)
            yb_ref[:, cols] = yb.astype(yb_ref.dtype)
            m_ref[:, cols] = merged.astype(m_ref.dtype)

    row = lambda x: pl.BlockSpec((tm, x.shape[1]), lambda i: (i, 0))
    out_spec = pl.BlockSpec((tm, d), lambda i: (i, 0))
    return pl.pallas_call(
        body, name="mix_merge", grid=(t // tm,),
        in_specs=[row(att), row(cs), _resident(wao), _resident(wco)] + gate_specs + [_resident(gate_bias)],
        out_specs=[out_spec] * 3, out_shape=[jax.ShapeDtypeStruct((t, d), BF16)] * 3, compiler_params=_cparams(1),
    )(att, cs, wao, wco, *([proj] * n_p), gate_bias)


def _mix_d_merge(dmo, wout, y_a, y_b, wao, wco, proj, gate_bias, tm=512):
    t, d = dmo.shape
    nj, _, nb = wao.shape
    ka, kc = wao.shape[1], wco.shape[1]
    tm = _row_tile(t, tm)
    gate_specs = _gate_specs(d, tm)
    n_p = len(gate_specs)

    def body(dmo_ref, wout_ref, ya_ref, yb_ref, wao_ref, wco_ref, *rest):
        pieces, (gb_ref, dya_ref, dyb_ref, dg_ref, datt_ref, dcs_ref, dgb_ref) = rest[:n_p], rest[n_p:]

        @pl.when(pl.program_id(0) == 0)
        def _():
            dgb_ref[...] = jnp.zeros_like(dgb_ref)

        dmo_v = dmo_ref[...]
        datt = dcs = None
        for j in range(nj):
            cols, cols_b = slice(j * nb, (j + 1) * nb), slice(d + j * nb, d + (j + 1) * nb)
            dm = _dot(dmo_v, wout_ref[j], True)
            ga, gb = _gate(pieces, gb_ref, j * nb, nb), _gate(pieces, gb_ref, d + j * nb, nb)
            dya, dyb = (dm * ga).astype(BF16), (dm * gb).astype(BF16)
            dga = dm * ya_ref[:, cols].astype(F32) * ga * (1.0 - ga)
            dgb = dm * yb_ref[:, cols].astype(F32) * gb * (1.0 - gb)
            dya_ref[:, cols], dyb_ref[:, cols] = dya, dyb
            dg_ref[:, cols], dg_ref[:, cols_b] = dga.astype(dg_ref.dtype), dgb.astype(dg_ref.dtype)
            dgb_ref[:, cols] += _colsum(dga)
            dgb_ref[:, cols_b] += _colsum(dgb)
            pa, pc = _dot(dya, wao_ref[j], True), _dot(dyb, wco_ref[j], True)
            datt, dcs = (pa, pc) if datt is None else (datt + pa, dcs + pc)
        datt_ref[...] = datt.astype(datt_ref.dtype)
        dcs_ref[...] = dcs.astype(dcs_ref.dtype)

    row = lambda cols: pl.BlockSpec((tm, cols), lambda i: (i, 0))
    return pl.pallas_call(
        body, name="mix_d_merge", grid=(t // tm,),
        in_specs=[row(d), _resident(wout), row(d), row(d), _resident(wao), _resident(wco)] + gate_specs
        + [_resident(gate_bias)],
        out_specs=[row(d), row(d), row(2 * d), row(ka), row(kc), pl.BlockSpec((1, 2 * d), lambda i: (0, 0))],
        out_shape=[jax.ShapeDtypeStruct((t, d), BF16), jax.ShapeDtypeStruct((t, d), BF16),
                   jax.ShapeDtypeStruct((t, 2 * d), BF16), jax.ShapeDtypeStruct((t, ka), BF16),
                   jax.ShapeDtypeStruct((t, kc), F32), jax.ShapeDtypeStruct((1, 2 * d), F32)],
        compiler_params=_cparams(1),
    )(dmo, wout, y_a, y_b, wao, wco, *([proj] * n_p), gate_bias)


def _adamw_math(wv, gv, mv, vv):
    m2 = ADAM_B1 * mv + (1.0 - ADAM_B1) * gv
    v2 = ADAM_B2 * vv + (1.0 - ADAM_B2) * (gv * gv)
    m_hat = m2 / (1.0 - ADAM_B1 ** ADAM_STEP)
    v_hat = v2 / (1.0 - ADAM_B2 ** ADAM_STEP)
    delta = -ADAM_LR * (m_hat / (jnp.sqrt(v_hat) + ADAM_EPS) + ADAM_WD * wv)
    return delta, m2, v2


def _adamw(name, w, g, m, v):
    def fn(r, _):
        return list(_adamw_math(*r)), []

    c = w.shape[1]
    return _rowwise(name, fn, [_whole(w), _whole(g), _whole(m), _whole(v)], [], [(c, F32)] * 3, [], tm=256)


POS_C, POS_CHIP, POS_PEER = 0, 1, 2


def _placed_call(body, name, pos, grid, in_specs, out_specs, out_shape, args):
    return pl.pallas_call(
        body, name=name, out_shape=out_shape, compiler_params=_cparams(len(grid)),
        grid_spec=pltpu.PrefetchScalarGridSpec(num_scalar_prefetch=1, grid=grid, in_specs=in_specs,
                                               out_specs=out_specs),
    )(pos, *args)


def _cast_into(name, pos, w):
    r, cols = w.shape
    tm = _row_tile(r, 256)

    def body(pos_ref, w_ref, o_ref):
        o_ref[...] = w_ref[...].astype(o_ref.dtype)

    return _placed_call(
        body, name, pos, (r // tm,), [pl.BlockSpec((tm, cols), lambda i, pos: (i, 0))],
        pl.BlockSpec((None, tm, cols), lambda i, pos: (pos[POS_CHIP], i, 0)),
        jax.ShapeDtypeStruct((N_CHIPS, r, cols), BF16), [w])


def _add_pair(name, pos, grad, landed):
    nj, half, cols = landed.shape
    tm = _row_tile(half, 256)
    nb = half // tm

    def body(pos_ref, g_ref, l_ref, o_ref):
        o_ref[...] = (g_ref[...].astype(F32) + l_ref[...].astype(F32)).astype(o_ref.dtype)

    spec = pl.BlockSpec((None, tm, cols), lambda j, i, pos: (j, i, 0))
    return _placed_call(
        body, name, pos, (nj, nb),
        [pl.BlockSpec((None, tm, cols), lambda j, i, pos: (j, pos[POS_C] * nb + i, 0)), spec], spec,
        jax.ShapeDtypeStruct(landed.shape, BF16), [grad, landed])


def _add_chips(name, pos, part, landed):
    _, half, cols = landed.shape
    tm = _row_tile(half, 256)

    def body(pos_ref, p_ref, l0_ref, l1_ref, l2_ref, o_ref):
        acc = p_ref[...].astype(F32)
        for ref in (l0_ref, l1_ref, l2_ref):
            acc = acc + ref[...].astype(F32)
        o_ref[...] = acc

    slot = lambda at: pl.BlockSpec((None, tm, cols), functools.partial(lambda i, pos, at: (pos[at], i, 0), at=at))
    return _placed_call(
        body, name, pos, (half // tm,), [slot(POS_CHIP)] + [slot(POS_PEER + k) for k in range(3)],
        pl.BlockSpec((tm, cols), lambda i, pos: (i, 0)), jax.ShapeDtypeStruct((half, cols), F32),
        [part, landed, landed, landed])


def _adamw_halves(name, pos, w, m, v, own, landed):
    r, cols = w.shape
    half = own.shape[0]
    tm = _row_tile(half, 256)
    nb = half // tm

    def body(pos_ref, w_ref, m_ref, v_ref, own_ref, land_ref, g_out, d_out, m_out, v_out):
        mine = pl.program_id(0) == pos_ref[POS_C]
        g = jnp.where(mine, own_ref[...], land_ref[...])
        delta, m2, v2 = _adamw_math(w_ref[...], g, m_ref[...], v_ref[...])
        g_out[...] = g
        d_out[...] = delta
        m_out[...] = m2
        v_out[...] = v2

    full = pl.BlockSpec((tm, cols), lambda h, i, pos: (h * nb + i, 0))
    part = pl.BlockSpec((tm, cols), lambda h, i, pos: (i, 0))
    return _placed_call(
        body, name, pos, (2, nb), [full, full, full, part, part], [full] * 4,
        [jax.ShapeDtypeStruct((r, cols), F32)] * 4, [w, m, v, own, landed])


N_START = K_PAD // Q_BLOCK


def _rel_onehot(n_q):
    e = np.arange(REL_EXT)
    dist = K_PAD - (e - (n_q - 1))
    idx = np.clip(dist, -REL_CLIP, REL_CLIP) + REL_CLIP
    return (np.arange(REL_PAD)[:, None] == idx[None, :]).astype(np.float32)


def _skew(x, left):
    row = lax.broadcasted_iota(jnp.int32, x.shape, 0)
    for bit in range(x.shape[0].bit_length() - 1):
        amount = 1 << bit
        rolled = pltpu.roll(x, REL_EXT - amount if left else amount, 1)
        x = jnp.where((row >> bit) & 1 == 1, rolled, x)
    return x


def _bias_expand(table_pad):
    onehot = jnp.asarray(_rel_onehot(Q_BLOCK))

    def body(t_ref, oh_ref, o_ref):
        ext = jnp.dot(t_ref[...], oh_ref[...], precision=lax.Precision.HIGHEST, preferred_element_type=F32)
        qc = lax.broadcasted_iota(jnp.int32, (Q_BLOCK, K_WIN), 0) // CHUNK
        kpos = lax.broadcasted_iota(jnp.int32, (Q_BLOCK, K_WIN), 1)
        band = (kpos // CHUNK >= qc) & (kpos // CHUNK <= qc + LEFT_CHUNKS)
        for h in range(N_HEADS):
            rows = jnp.broadcast_to(ext[h:h + 1, :], (Q_BLOCK, REL_EXT))
            rolled = _skew(pltpu.roll(rows, REL_EXT - (Q_BLOCK - 1), 1), left=False)[:, :K_WIN]
            for v in range(N_START + 1):
                o_ref[v, h] = jnp.where(band & (kpos + v * Q_BLOCK >= K_PAD), rolled, NEG)

    return pl.pallas_call(
        body, name="bias_expand", out_shape=jax.ShapeDtypeStruct((N_START + 1, N_HEADS, Q_BLOCK, K_WIN), F32),
        compiler_params=pltpu.CompilerParams(vmem_limit_bytes=VMEM_LIMIT_BYTES),
    )(table_pad, onehot)


def _bias_fold(dbias):
    onehot_t = jnp.asarray(_rel_onehot(CHUNK).T)

    def body(d_ref, oh_ref, o_ref, ext_ref):
        for h in range(N_HEADS):
            x = jnp.concatenate([d_ref[h], jnp.zeros((CHUNK, REL_EXT - K_WIN), F32)], axis=1)
            rolled = _skew(pltpu.roll(x, CHUNK - 1, 1), left=True)
            ext_ref[h:h + 1, :] = jnp.sum(rolled, axis=0, keepdims=True)
        o_ref[...] = jnp.dot(ext_ref[...], oh_ref[...], precision=lax.Precision.HIGHEST,
                             preferred_element_type=F32)

    return pl.pallas_call(
        body, name="bias_fold", out_shape=jax.ShapeDtypeStruct((N_HEADS, REL_PAD), F32),
        scratch_shapes=[pltpu.VMEM((N_HEADS, REL_EXT), F32)],
        compiler_params=pltpu.CompilerParams(vmem_limit_bytes=VMEM_LIMIT_BYTES),
    )(dbias, onehot_t)


def _head_lanes():
    lane = lax.broadcasted_iota(jnp.int32, (1, 2 * HEAD_DIM), 1)
    return [lane < HEAD_DIM, lane >= HEAD_DIM]


def _only(mask, x, scale=None):
    x = jnp.where(mask, x, jnp.zeros_like(x))
    return x if scale is None else x * scale


def _contract_lanes(a, b):
    return lax.dot_general(a, b, (((1,), (1,)), ((), ())), preferred_element_type=F32)


def _contract_rows(a, b):
    return lax.dot_general(a, b, (((0,), (0,)), ((), ())), preferred_element_type=F32)


def _attn_specs():
    pairs = D_ATTN // (2 * HEAD_DIM)
    q_spec = pl.BlockSpec((Q_BLOCK, 2 * HEAD_DIM), lambda p, i: (i, p))
    kv_specs = [pl.BlockSpec((Q_BLOCK, 2 * HEAD_DIM),
                             functools.partial(lambda p, i, kk, c0: (jnp.maximum(i + kk - N_START, 0), c0 + p),
                                               kk=kk, c0=c0))
                for c0 in (pairs, 2 * pairs) for kk in range(K_WIN // Q_BLOCK)]
    bias_spec = pl.BlockSpec((None, 2, Q_BLOCK, K_WIN), lambda p, i: (jnp.minimum(i, N_START), p, 0, 0))
    return q_spec, kv_specs, bias_spec


def _attn_fwd(proj, bias):
    t = proj.shape[0]
    n_win = K_WIN // Q_BLOCK

    def body(q_ref, *refs):
        k_refs, v_refs = refs[:n_win], refs[n_win:2 * n_win]
        b_ref, o_ref, lse_ref = refs[2 * n_win:]
        k = jnp.concatenate([r[...] for r in k_refs], axis=0)
        v = jnp.concatenate([r[...] for r in v_refs], axis=0)
        q = q_ref[...]
        o = lse = None
        for hh, lanes in enumerate(_head_lanes()):
            s = _contract_lanes(_only(lanes, q, HEAD_DIM ** -0.5), k) + b_ref[hh]
            m = jnp.max(s, axis=1, keepdims=True)
            p = jnp.exp(s - m)
            l = jnp.sum(p, axis=1, keepdims=True)
            oh = jnp.dot(p.astype(BF16), v, preferred_element_type=F32) / l
            lse_h = jnp.broadcast_to(m + jnp.log(l), oh.shape)
            o, lse = (oh, lse_h) if o is None else (jnp.where(lanes, oh, o), jnp.where(lanes, lse_h, lse))
        o_ref[...] = o.astype(o_ref.dtype)
        lse_ref[...] = lse

    q_spec, kv_specs, bias_spec = _attn_specs()
    out_spec = pl.BlockSpec((Q_BLOCK, 2 * HEAD_DIM), lambda p, i: (i, p))
    return pl.pallas_call(
        body, name="attn_fwd", grid=(N_HEADS // 2, t // Q_BLOCK),
        in_specs=[q_spec] + kv_specs + [bias_spec], out_specs=[out_spec, out_spec],
        out_shape=[jax.ShapeDtypeStruct((t, D_ATTN), BF16), jax.ShapeDtypeStruct((t, D_ATTN), F32)],
        compiler_params=_cparams(2),
    )(*([proj] * (1 + 2 * n_win)), bias)


def _attn_bwd(proj, bias, att, lse, datt):
    t = proj.shape[0]
    n_win = K_WIN // Q_BLOCK

    def body(q_ref, *refs):
        k_refs, v_refs = refs[:n_win], refs[n_win:2 * n_win]
        b_ref, o_ref, lse_ref, do_ref, dq_ref, dk_ref, dv_ref, db_ref = refs[2 * n_win:]
        i = pl.program_id(1)

        @pl.when(i == 0)
        def _():
            dk_ref[...] = jnp.zeros_like(dk_ref)
            dv_ref[...] = jnp.zeros_like(dv_ref)
            db_ref[...] = jnp.zeros_like(db_ref)

        k = jnp.concatenate([r[...] for r in k_refs], axis=0)
        v = jnp.concatenate([r[...] for r in v_refs], axis=0)
        q, do, o = q_ref[...], do_ref[...], o_ref[...].astype(F32)
        rows = pl.ds(pl.multiple_of(i * Q_BLOCK, Q_BLOCK), K_WIN)
        scale = HEAD_DIM ** -0.5
        dq = dk = dv = None
        for hh, lanes in enumerate(_head_lanes()):
            qh, doh = _only(lanes, q, scale), _only(lanes, do)
            s = _contract_lanes(qh, k) + b_ref[hh]
            p = jnp.exp(s - lse_ref[:, hh * HEAD_DIM:hh * HEAD_DIM + 1])
            delta = jnp.sum(doh.astype(F32) * o, axis=1, keepdims=True)
            ds = p * (_contract_lanes(doh, v) - delta)
            folded = ds[:CHUNK]
            for c in range(1, Q_BLOCK // CHUNK):
                folded = folded + pltpu.roll(ds[c * CHUNK:(c + 1) * CHUNK], K_WIN - c * CHUNK, 1)
            db_ref[hh] += folded
            dsb = ds.astype(BF16)
            dqh = jnp.dot(dsb, k, preferred_element_type=F32)
            dq = dqh if dq is None else jnp.where(lanes, dqh, dq)
            dkh, dvh = _contract_rows(dsb, qh), _contract_rows(p.astype(BF16), doh)
            dk, dv = (dkh, dvh) if dk is None else (dk + dkh, dv + dvh)
        dq_ref[...] = (dq * scale).astype(dq_ref.dtype)
        dk_ref[rows, :] += dk
        dv_ref[rows, :] += dv

    q_spec, kv_specs, bias_spec = _attn_specs()
    row_spec = pl.BlockSpec((Q_BLOCK, 2 * HEAD_DIM), lambda p, i: (i, p))
    full_spec = pl.BlockSpec((t + K_PAD, 2 * HEAD_DIM), lambda p, i: (0, p))
    return pl.pallas_call(
        body, name="attn_bwd", grid=(N_HEADS // 2, t // Q_BLOCK),
        in_specs=[q_spec] + kv_specs + [bias_spec, row_spec, row_spec, row_spec],
        out_specs=[row_spec, full_spec, full_spec, pl.BlockSpec((2, CHUNK, K_WIN), lambda p, i: (p, 0, 0))],
        out_shape=[jax.ShapeDtypeStruct((t, D_ATTN), BF16), jax.ShapeDtypeStruct((t + K_PAD, D_ATTN), F32),
                   jax.ShapeDtypeStruct((t + K_PAD, D_ATTN), F32),
                   jax.ShapeDtypeStruct((N_HEADS, CHUNK, K_WIN), F32)],
        compiler_params=_cparams(2),
    )(*([proj] * (1 + 2 * n_win)), bias, att, lse, datt)


CONV_LEAD = CONV_HALO - (CONV_WIDTH - 1)
CONV_LANES = 128
CONV_ROWS = 64


def _conv_specs(t):
    tt = _row_tile(t, CONV_TILE)
    per = tt // CONV_HALO
    n_halo = t // CONV_HALO
    tile = lambda cb: pl.BlockSpec((tt, COL), functools.partial(lambda i, cb: (i, cb), cb=cb))
    prev = lambda cb: pl.BlockSpec((CONV_HALO, COL),
                                   functools.partial(lambda i, cb: (jnp.maximum(i * per - 1, 0), cb), cb=cb))
    nxt = lambda cb: pl.BlockSpec((CONV_HALO, COL),
                                  functools.partial(lambda i, cb: (jnp.minimum((i + 1) * per, n_halo - 1), cb), cb=cb))
    vec = lambda shape: pl.BlockSpec(shape, lambda i: (0, 0))
    return tt, tile, prev, nxt, vec


def _glu(ca, cg, bias):
    return (ca.astype(F32) + bias[:, :D_CONV]) * jax.nn.sigmoid(cg.astype(F32) + bias[:, D_CONV:])


SUBLANES = 8


def _shift_copies(ext_ref):
    n = ext_ref.shape[1] - SUBLANES
    for s in range(1, SUBLANES):
        ext_ref[s, 0:n, :] = ext_ref[0, s:s + n, :]


def _tap_tiles(ext_ref, first_row, r0, lanes):
    n_g = CONV_ROWS // SUBLANES
    for s in range(SUBLANES):
        taps = [w for w in range(CONV_WIDTH) if first_row(w) % SUBLANES == s]
        if not taps:
            continue
        lo = min(first_row(w) for w in taps) - s
        n_tiles = (max(first_row(w) for w in taps) - s - lo) // SUBLANES + n_g
        tiles = [ext_ref[s, r0 + lo + SUBLANES * b:r0 + lo + SUBLANES * (b + 1), lanes] for b in range(n_tiles)]
        for w in taps:
            k = (first_row(w) - s - lo) // SUBLANES
            yield w, tiles[k:k + n_g]


def _taps(ext_ref, tt, first_row, w_ref, out_ref):
    n_g = CONV_ROWS // SUBLANES
    for l0 in range(0, D_CONV, CONV_LANES):
        lanes = slice(l0, l0 + CONV_LANES)
        for r0 in range(0, tt, CONV_ROWS):
            acc = [jnp.zeros((SUBLANES, CONV_LANES), F32)] * n_g
            for w, tiles in _tap_tiles(ext_ref, first_row, r0, lanes):
                weight = jnp.broadcast_to(w_ref[w:w + 1, lanes], (SUBLANES, CONV_LANES))
                acc = [a + t * weight for a, t in zip(acc, tiles)]
            for g in range(n_g):
                out_ref[r0 + SUBLANES * g:r0 + SUBLANES * (g + 1), lanes] = acc[g]


def _tap_sums(ext_ref, tt, first_row, x_ref, out_ref):
    n_g = CONV_ROWS // SUBLANES
    for l0 in range(0, D_CONV, CONV_LANES):
        lanes = slice(l0, l0 + CONV_LANES)
        acc = [jnp.zeros((SUBLANES, CONV_LANES), F32)] * CONV_WIDTH
        for r0 in range(0, tt, CONV_ROWS):
            x = [x_ref[0, r0 + SUBLANES * g:r0 + SUBLANES * (g + 1), lanes] for g in range(n_g)]
            for w, tiles in _tap_tiles(ext_ref, first_row, r0, lanes):
                part = tiles[0] * x[0]
                for g in range(1, n_g):
                    part = part + tiles[g] * x[g]
                acc[w] = acc[w] + part
        for w in range(CONV_WIDTH):
            out_ref[w:w + 1, lanes] += jnp.sum(acc[w], axis=0, keepdims=True)


def _conv_fwd(proj, glu_bias, dw, dw_b, ln_g, ln_b):
    t = proj.shape[0]
    tt, tile, prev, nxt, vec = _conv_specs(t)
    ca_blk, cg_blk = 3 * D_ATTN // COL, 3 * D_ATTN // COL + 1

    def body(ca_ref, cg_ref, pa_ref, pg_ref, gb_ref, dw_ref, dwb_ref, g_ref, b_ref, cs_ref, c_ref, z_ref, ext_ref):
        i = pl.program_id(0)
        bias = gb_ref[...]
        c = _glu(ca_ref[...], cg_ref[...], bias)
        halo = _glu(pa_ref[...], pg_ref[...], bias)
        ext_ref[0, 0:CONV_HALO, :] = jnp.where(i == 0, 0.0, halo)
        ext_ref[0, CONV_HALO:, :] = c
        _shift_copies(ext_ref)
        c_ref[...] = c
        _taps(ext_ref, tt, lambda w: CONV_LEAD + w, dw_ref, z_ref)
        z = z_ref[...] + dwb_ref[...]
        z_ref[...] = z
        mu = jnp.mean(z, axis=-1, keepdims=True)
        zc = z - mu
        y = zc * lax.rsqrt(jnp.mean(zc * zc, axis=-1, keepdims=True) + EPS) * g_ref[...] + b_ref[...]
        cs_ref[...] = (y * jax.nn.sigmoid(y)).astype(cs_ref.dtype)

    out_spec = pl.BlockSpec((tt, D_CONV), lambda i: (i, 0))
    return pl.pallas_call(
        body, name="conv_fwd", grid=(t // tt,),
        in_specs=[tile(ca_blk), tile(cg_blk), prev(ca_blk), prev(cg_blk), vec(glu_bias.shape), vec(dw.shape),
                  vec(dw_b.shape), vec(ln_g.shape), vec(ln_b.shape)],
        out_specs=[out_spec] * 3,
        out_shape=[jax.ShapeDtypeStruct((t, D_CONV), BF16), jax.ShapeDtypeStruct((t, D_CONV), F32),
                   jax.ShapeDtypeStruct((t, D_CONV), F32)],
        scratch_shapes=[pltpu.VMEM((SUBLANES, tt + CONV_HALO, D_CONV), F32)], compiler_params=_cparams(1),
    )(proj, proj, proj, proj, glu_bias, dw, dw_b, ln_g, ln_b)


def _conv_bwd(proj, c, z, dcs, glu_bias, dw, ln_g, ln_b):
    t = proj.shape[0]
    tt, tile, prev, nxt, vec = _conv_specs(t)
    n_tiles = t // tt
    ca_blk, cg_blk = 3 * D_ATTN // COL, 3 * D_ATTN // COL + 1

    def ln_bwd(zv, dcsv, g, b):
        mu = jnp.mean(zv, axis=-1, keepdims=True)
        zc = zv - mu
        rstd = lax.rsqrt(jnp.mean(zc * zc, axis=-1, keepdims=True) + EPS)
        zhat = zc * rstd
        y = zhat * g + b
        sig = jax.nn.sigmoid(y)
        dy = dcsv * sig * (1.0 + y * (1.0 - sig))
        dzh = dy * g
        dz = rstd * (dzh - jnp.mean(dzh, axis=-1, keepdims=True) - zhat * jnp.mean(dzh * zhat, axis=-1, keepdims=True))
        return dz, dy, zhat

    def body(ca_ref, cg_ref, c_ref, cprev_ref, z_ref, znext_ref, dcs_ref, dcsnext_ref, gb_ref, dw_ref, g_ref, b_ref,
             dcin_ref, ddw_ref, ddwb_ref, dg_ref, db_ref, dgb_ref, cext_ref, dzext_ref, dc_ref):
        i = pl.program_id(0)

        @pl.when(i == 0)
        def _():
            for ref in (ddw_ref, ddwb_ref, dg_ref, db_ref, dgb_ref):
                ref[...] = jnp.zeros_like(ref)

        g, b = g_ref[...], b_ref[...]
        dz, dy, zhat = ln_bwd(z_ref[...], dcs_ref[...], g, b)
        dz_next, _, _ = ln_bwd(znext_ref[...], dcsnext_ref[...], g, b)
        dg_ref[...] += _colsum(dy * zhat)
        db_ref[...] += _colsum(dy)
        ddwb_ref[...] += _colsum(dz)
        dzext_ref[0, 0:tt, :] = dz
        dzext_ref[0, tt:, :] = jnp.where(i == n_tiles - 1, 0.0, dz_next)
        _shift_copies(dzext_ref)
        cext_ref[0, 0:CONV_HALO, :] = jnp.where(i == 0, 0.0, cprev_ref[...])
        cext_ref[0, CONV_HALO:, :] = c_ref[...]
        _shift_copies(cext_ref)
        _taps(dzext_ref, tt, lambda w: CONV_WIDTH - 1 - w, dw_ref, dc_ref)
        _tap_sums(cext_ref, tt, lambda w: CONV_LEAD + w, dzext_ref, ddw_ref)
        bias = gb_ref[...]
        a_in = ca_ref[...].astype(F32) + bias[:, :D_CONV]
        sg = jax.nn.sigmoid(cg_ref[...].astype(F32) + bias[:, D_CONV:])
        dc = dc_ref[...]
        dcin = jnp.concatenate([dc * sg, dc * a_in * sg * (1.0 - sg)], axis=1)
        dcin_ref[...] = dcin.astype(dcin_ref.dtype)
        dgb_ref[...] += _colsum(dcin)

    row = lambda: pl.BlockSpec((tt, D_CONV), lambda i: (i, 0))
    per = tt // CONV_HALO
    n_halo = t // CONV_HALO
    prev_row = pl.BlockSpec((CONV_HALO, D_CONV), lambda i: (jnp.maximum(i * per - 1, 0), 0))
    next_row = lambda: pl.BlockSpec((CONV_HALO, D_CONV), lambda i: (jnp.minimum((i + 1) * per, n_halo - 1), 0))
    acc = lambda shape: pl.BlockSpec(shape, lambda i: (0, 0))
    return pl.pallas_call(
        body, name="conv_bwd", grid=(n_tiles,),
        in_specs=[tile(ca_blk), tile(cg_blk), row(), prev_row, row(), next_row(), row(), next_row(),
                  vec(glu_bias.shape), vec(dw.shape), vec(ln_g.shape), vec(ln_b.shape)],
        out_specs=[pl.BlockSpec((tt, 2 * D_CONV), lambda i: (i, 0)), acc(dw.shape), acc((1, D_CONV)),
                   acc((1, D_CONV)), acc((1, D_CONV)), acc((1, 2 * D_CONV))],
        out_shape=[jax.ShapeDtypeStruct((t, 2 * D_CONV), BF16), jax.ShapeDtypeStruct(dw.shape, F32),
                   jax.ShapeDtypeStruct((1, D_CONV), F32), jax.ShapeDtypeStruct((1, D_CONV), F32),
                   jax.ShapeDtypeStruct((1, D_CONV), F32), jax.ShapeDtypeStruct((1, 2 * D_CONV), F32)],
        scratch_shapes=[pltpu.VMEM((SUBLANES, tt + CONV_HALO, D_CONV), F32),
                        pltpu.VMEM((SUBLANES, tt + CONV_HALO, D_CONV), F32), pltpu.VMEM((tt, D_CONV), F32)],
        compiler_params=_cparams(1),
    )(proj, proj, c, c, z, z, dcs, dcs, glu_bias, dw, ln_g, ln_b)


def _place():
    x, y, c = lax.axis_index("x"), lax.axis_index("y"), lax.axis_index("c")
    chips = [(1 - x, y), (x, 1 - y), (1 - x, 1 - y)]
    return x, y, c, chips


def _chip_index(chip):
    return 2 * chip[0] + chip[1]


def _half_rows(c, half):
    return pl.ds(pl.multiple_of(c * half, 16), half)


def _gather_carry(blocked):
    n = len(blocked)

    def over_ici(o_refs, send_sems, recv_sems):
        x, y, c, chips = _place()
        me = _chip_index((x, y))
        copies = []
        for a in range(n):
            mine = o_refs[a].at[me, _half_rows(c, o_refs[a].shape[1] // 2), :]
            for k, chip in enumerate(chips):
                copies.append(pltpu.make_async_remote_copy(
                    src_ref=mine, dst_ref=mine, send_sem=send_sems.at[6 * a + k], recv_sem=recv_sems.at[6 * a + k],
                    device_id=(chip[0], chip[1], c), device_id_type=MESH))
        return copies

    def to_sibling(o_refs, send_sems, recv_sems, sent_by_me):
        x, y, c, chips = _place()
        copies = []
        for a in range(n):
            rows = _half_rows(c if sent_by_me else 1 - c, o_refs[a].shape[1] // 2)
            for k, chip in enumerate(chips):
                landed = o_refs[a].at[_chip_index(chip), rows, :]
                copies.append(pltpu.make_async_remote_copy(
                    src_ref=landed, dst_ref=landed, send_sem=send_sems.at[6 * a + 3 + k],
                    recv_sem=recv_sems.at[6 * a + 3 + k], device_id=(x, y, 1 - c), device_id_type=MESH))
        return copies

    def start(ins, outs, sems):
        for cp in over_ici(outs, *sems):
            cp.start()

    def hand_on(ins, outs, sems):
        for arrived, onward in zip(over_ici(outs, *sems), to_sibling(outs, *sems, True)):
            arrived.wait_recv()
            onward.start()

    def finish(ins, outs, sems):
        for cp in to_sibling(outs, *sems, False):
            cp.wait_recv()
        for cp in over_ici(outs, *sems) + to_sibling(outs, *sems, True):
            cp.wait_send()

    return _Carry(
        ins=list(blocked), outs=[jax.ShapeDtypeStruct(w.shape, w.dtype) for w in blocked],
        aliases={a: a for a in range(n)},
        sems=[pltpu.SemaphoreType.DMA((6 * n,)), pltpu.SemaphoreType.DMA((6 * n,))],
        phases=[("first", start), ("late", hand_on), ("last", finish)])


def _pair_exchange(name, grads):
    n = len(grads)

    def body(*refs):
        g_refs, land_refs = refs[:n], refs[n:2 * n]
        send_sems, recv_sems = refs[2 * n:]
        x, y, c, _ = _place()
        copies = []
        for a in range(n):
            half = g_refs[a].shape[1] // 2
            cp = pltpu.make_async_remote_copy(
                src_ref=g_refs[a].at[:, _half_rows(1 - c, half), :], dst_ref=land_refs[a],
                send_sem=send_sems.at[a], recv_sem=recv_sems.at[a], device_id=(x, y, 1 - c), device_id_type=MESH)
            cp.start()
            copies.append(cp)
        for cp in copies:
            cp.wait()

    return pl.pallas_call(
        body, name=name, in_specs=[ANY] * n, out_specs=[ANY] * n,
        out_shape=[jax.ShapeDtypeStruct((g.shape[0], g.shape[1] // 2, g.shape[2]), g.dtype) for g in grads],
        scratch_shapes=[pltpu.SemaphoreType.DMA((n,)), pltpu.SemaphoreType.DMA((n,))],
    )(*grads)


def _to_owner_carry(parts):
    n = len(parts)

    def sends(p_refs, l_refs, send_sems, recv_sems):
        x, y, c, chips = _place()
        me = _chip_index((x, y))
        return [pltpu.make_async_remote_copy(
            src_ref=p_refs[a].at[_chip_index(chip)], dst_ref=l_refs[a].at[me],
            send_sem=send_sems.at[3 * a + k], recv_sem=recv_sems.at[3 * a + k],
            device_id=(chip[0], chip[1], c), device_id_type=MESH) for a in range(n) for k, chip in enumerate(chips)]

    def start(ins, outs, sems):
        for cp in sends(ins, outs, *sems):
            cp.start()

    def finish(ins, outs, sems):
        x, y, c, chips = _place()
        send_sems, recv_sems = sems
        for a in range(n):
            for k, chip in enumerate(chips):
                slot = outs[a].at[_chip_index(chip)]
                pltpu.make_async_remote_copy(
                    src_ref=slot, dst_ref=slot, send_sem=send_sems.at[3 * a + k], recv_sem=recv_sems.at[3 * a + k],
                    device_id=(chip[0], chip[1], c), device_id_type=MESH).wait_recv()
        for cp in sends(ins, outs, *sems):
            cp.wait_send()

    return _Carry(
        ins=list(parts), outs=[jax.ShapeDtypeStruct(p.shape, p.dtype) for p in parts], aliases={},
        sems=[pltpu.SemaphoreType.DMA((3 * n,)), pltpu.SemaphoreType.DMA((3 * n,))],
        phases=[("first", start), ("last", finish)])


def _swap_halves(halves):
    n = len(halves)

    def body(*refs):
        h_refs, o_refs = refs[:n], refs[n:2 * n]
        send_sems, recv_sems = refs[2 * n:]
        x, y, c, _ = _place()
        copies = []
        for a in range(n):
            cp = pltpu.make_async_remote_copy(
                src_ref=h_refs[a], dst_ref=o_refs[a], send_sem=send_sems.at[a], recv_sem=recv_sems.at[a],
                device_id=(x, y, 1 - c), device_id_type=MESH)
            cp.start()
            copies.append(cp)
        for cp in copies:
            cp.wait()

    return pl.pallas_call(
        body, name="grad_swap_halves", in_specs=[ANY] * n, out_specs=[ANY] * n,
        out_shape=[jax.ShapeDtypeStruct(h.shape, h.dtype) for h in halves],
        scratch_shapes=[pltpu.SemaphoreType.DMA((n,)), pltpu.SemaphoreType.DMA((n,))],
    )(*halves)


def _all_devices(name, block):
    r, cols = block.shape

    def body(b_ref, all_ref, sum_ref, send_sems, recv_sems):
        x, y, c, _ = _place()
        me = 4 * x + 2 * y + c
        all_ref[me] = b_ref[...]
        flips = [(fx, fy, fc) for fx in (0, 1) for fy in (0, 1) for fc in (0, 1)][1:]
        copies = []
        for k, (fx, fy, fc) in enumerate(flips):
            cp = pltpu.make_async_remote_copy(
                src_ref=b_ref, dst_ref=all_ref.at[me], send_sem=send_sems.at[k], recv_sem=recv_sems.at[k],
                device_id=(x ^ fx, y ^ fy, c ^ fc), device_id_type=MESH)
            cp.start()
            copies.append(cp)
        for k, (fx, fy, fc) in enumerate(flips):
            slot = all_ref.at[4 * (x ^ fx) + 2 * (y ^ fy) + (c ^ fc)]
            pltpu.make_async_remote_copy(
                src_ref=slot, dst_ref=slot, send_sem=send_sems.at[k], recv_sem=recv_sems.at[k],
                device_id=(x ^ fx, y ^ fy, c ^ fc), device_id_type=MESH).wait_recv()
        for cp in copies:
            cp.wait_send()
        acc = all_ref[0]
        for d in range(1, N_DEV):
            acc = acc + all_ref[d]
        sum_ref[...] = acc

    vmem = pl.BlockSpec(memory_space=pltpu.VMEM)
    return pl.pallas_call(
        body, name=name, in_specs=[vmem], out_specs=[vmem, vmem],
        out_shape=[jax.ShapeDtypeStruct((N_DEV, r, cols), F32), jax.ShapeDtypeStruct((r, cols), F32)],
        scratch_shapes=[pltpu.SemaphoreType.DMA((N_DEV - 1,)), pltpu.SemaphoreType.DMA((N_DEV - 1,))],
    )(block)


PACK = 1024


def _packed_rows(shape, width):
    size, last = int(np.prod(shape)), shape[-1]
    cols = last if last <= width else width
    assert size % cols == 0
    return size // cols, cols


def _pack(vals, width=PACK):
    rows = []
    for v in vals:
        n_rows, cols = _packed_rows(v.shape, width)
        rows.append(jnp.pad(v.reshape(n_rows, cols).astype(F32), ((0, 0), (0, width - cols))))
    buf = jnp.concatenate(rows, axis=0)
    return jnp.pad(buf, ((0, (-buf.shape[0]) % 8), (0, 0)))


def _unpack(buf, shapes, width=PACK):
    out, r = [], 0
    for shape in shapes:
        n_rows, cols = _packed_rows(shape, width)
        out.append(buf[r:r + n_rows, :cols].reshape(shape))
        r += n_rows
    return out


def _ffn_hidden(name, n, wg, wu, tm=1024, carry=None):
    m, k = n.shape
    nj, fb, _ = wg.shape
    tm = _row_tile(m, tm)

    def core(n_ref, wg_ref, wu_ref, a_ref, b_ref, s_ref):
        nv = n_ref[...]
        a = _dot(nv, wg_ref[...], True)
        b = _dot(nv, wu_ref[...], True)
        a_ref[...] = a.astype(a_ref.dtype)
        b_ref[...] = b.astype(b_ref.dtype)
        s_ref[...] = (a * jax.nn.sigmoid(a) * b).astype(s_ref.dtype)

    w_spec = pl.BlockSpec((None, fb, k), lambda j, i: (j, 0, 0))
    out_spec = pl.BlockSpec((None, tm, fb), lambda j, i: (j, i, 0))
    return _call(name, core, (nj, m // tm), [pl.BlockSpec((tm, k), lambda j, i: (i, 0)), w_spec, w_spec],
                 [out_spec] * 3, [jax.ShapeDtypeStruct((nj, m, fb), BF16)] * 3, [], [n, wg, wu], carry)


def _ffn_d_hidden(name, df, wd, a, b, tm=512):
    m, k = df.shape
    nj, fb, _ = wd.shape
    tm = _row_tile(m, tm)

    def body(df_ref, wd_ref, a_ref, b_ref, da_ref, db_ref):
        dfv = df_ref[...]
        for j in range(nj):
            ds = _dot(dfv, wd_ref[j], True)
            av, bv = a_ref[j].astype(F32), b_ref[j].astype(F32)
            sig = jax.nn.sigmoid(av)
            da_ref[j] = (ds * bv * sig * (1.0 + av * (1.0 - sig))).astype(da_ref.dtype)
            db_ref[j] = (ds * av * sig).astype(db_ref.dtype)

    blk = pl.BlockSpec((nj, tm, fb), lambda i: (0, i, 0))
    return pl.pallas_call(
        body, name=name, grid=(m // tm,),
        in_specs=[pl.BlockSpec((tm, k), lambda i: (i, 0)), _resident(wd), blk, blk],
        out_specs=[blk, blk], out_shape=[jax.ShapeDtypeStruct((nj, m, fb), BF16)] * 2,
        compiler_params=_cparams(1),
    )(df, wd, a, b)


def kernel(x, ffn1_norm_pre, ffn1_w_gate, ffn1_w_up, ffn1_w_down, ffn1_norm_post, mix_norm_pre, w_in, gate_bias, rel_table, w_attn_out, conv_glu_bias, conv_dw_w, conv_dw_b, conv_ln_g, conv_ln_b, conv_w_out, w_out, mix_norm_post, ffn2_norm_pre, ffn2_w_gate, ffn2_w_up, ffn2_w_down, ffn2_norm_post, loss_target, m_ffn1_norm_pre, m_ffn1_w_gate, m_ffn1_w_up, m_ffn1_w_down, m_ffn1_norm_post, m_mix_norm_pre, m_w_in, m_gate_bias, m_rel_table, m_w_attn_out, m_conv_glu_bias, m_conv_dw_w, m_conv_dw_b, m_conv_ln_g, m_conv_ln_b, m_conv_w_out, m_w_out, m_mix_norm_post, m_ffn2_norm_pre, m_ffn2_w_gate, m_ffn2_w_up, m_ffn2_w_down, m_ffn2_norm_post, v_ffn1_norm_pre, v_ffn1_w_gate, v_ffn1_w_up, v_ffn1_w_down, v_ffn1_norm_post, v_mix_norm_pre, v_w_in, v_gate_bias, v_rel_table, v_w_attn_out, v_conv_glu_bias, v_conv_dw_w, v_conv_dw_b, v_conv_ln_g, v_conv_ln_b, v_conv_w_out, v_w_out, v_mix_norm_post, v_ffn2_norm_pre, v_ffn2_w_gate, v_ffn2_w_up, v_ffn2_w_down, v_ffn2_norm_post):
    args = dict(locals())
    names = ['ffn1_norm_pre', 'ffn1_w_gate', 'ffn1_w_up', 'ffn1_w_down', 'ffn1_norm_post', 'mix_norm_pre', 'w_in',
             'gate_bias', 'rel_table', 'w_attn_out', 'conv_glu_bias', 'conv_dw_w', 'conv_dw_b', 'conv_ln_g',
             'conv_ln_b', 'conv_w_out', 'w_out', 'mix_norm_post', 'ffn2_norm_pre', 'ffn2_w_gate', 'ffn2_w_up',
             'ffn2_w_down', 'ffn2_norm_post']
    big = ['ffn1_w_gate', 'ffn1_w_up', 'ffn1_w_down', 'w_in', 'w_attn_out', 'conv_w_out', 'w_out', 'ffn2_w_gate',
           'ffn2_w_up', 'ffn2_w_down']
    small = [n for n in names if n not in big]

    xs, target = x[0], loss_target[0]
    t, d = xs.shape
    cx, cy = lax.axis_index("x"), lax.axis_index("y")
    chip = 2 * cx + cy

    dw_shard = conv_dw_w[0, :, 0, :]
    cshard = dw_shard.shape[1]
    dw_all, _ = _all_devices("gather_dw", _pack([dw_shard], width=cshard))
    dw_full = jnp.concatenate([dw_all[2 * j, :CONV_WIDTH, :cshard] for j in range(N_CHIPS)], axis=1)
    dw_full = jnp.pad(dw_full, ((0, CONV_HALO - CONV_WIDTH), (0, 0)))
    peers = [(1 - cx, cy), (cx, 1 - cy), (1 - cx, 1 - cy)]
    pos = jnp.stack([lax.axis_index("c"), chip] + [_chip_index(p) for p in peers]).astype(jnp.int32)
    transposed = ("ffn1_w_gate", "ffn1_w_up", "ffn2_w_gate", "ffn2_w_up")
    weight_of = lambda n: n[2:] if n[:2] in ("m_", "v_") else n
    shard = lambda n: jnp.transpose(args[n][0]) if weight_of(n) in transposed else args[n][0]
    unshard = lambda n, v: (jnp.transpose(v) if n in transposed else v)[None]
    own = {n: _cast_into("cast_" + n, pos, shard(n)) for n in big}
    gather = lambda *ns: _gather_carry([own[n] for n in ns])
    res_spec = [(d, F32), (d, F32), (d, BF16)]

    n1, (wg1, wu1) = _rms_fwd("ffn1_pre", xs, ffn1_norm_pre, gather("ffn1_w_gate", "ffn1_w_up"))
    (a1, b1, s1), (wd1, win, wao, wco, wout) = _ffn_hidden(
        "ffn1_hidden", n1, wg1, wu1, carry=gather("ffn1_w_down", "w_in", "w_attn_out", "conv_w_out", "w_out"))
    (f1, h1, u), (wg2,) = _mm_kblk(
        "ffn1_down", [(s1, wd1)], trans_w=False, epilogue=_ep_post_res_pre(0.5), rows=[xs],
        vecs=[ffn1_norm_post, mix_norm_pre], row_outs=res_spec, carry=gather("ffn2_w_gate"))
    proj, (wu2, wd2) = _mm_nblk("mix_in", u, win, trans_w=False, out_blocked=False, out_dtype=BF16,
                                carry=gather("ffn2_w_up", "ffn2_w_down"))
    table_pad = jnp.pad(rel_table[0], ((0, 0), (0, REL_PAD - rel_table.shape[2])))
    bias = _bias_expand(table_pad)
    att, lse = _attn_fwd(proj, bias)
    cs, c_glu, z_conv = _conv_fwd(proj, conv_glu_bias, dw_full, conv_dw_b, conv_ln_g, conv_ln_b)
    y_a, y_b, merged = _mix_merge(att, cs, wao, wco, proj, gate_bias)
    (mo, h2, n2), _ = _mm_kblk(
        "mix_out", [(merged, wout)], trans_w=False, epilogue=_ep_post_res_pre(1.0), rows=[h1],
        vecs=[mix_norm_post, ffn2_norm_pre], row_outs=res_spec)
    (a2, b2, s2), _ = _ffn_hidden("ffn2_hidden", n2, wg2, wu2)
    g = {}
    (dy, df2, err2, g["ffn2_norm_post"]), _ = _mm_kblk(
        "ffn2_down", [(s2, wd2)], trans_w=False, epilogue=_ep_loss(0.5, d), rows=[h2, target],
        vecs=[ffn2_norm_post], row_outs=[(d, F32), (d, BF16)], vec_outs=[d, d])
    loss = lax.psum(0.5 * jnp.sum(err2) / d, ("x", "y", "c"))

    parts, landed = {}, {}

    def ffn_bwd(tag, df, n, a, b, s, wg, wu, wd, **epilogue):
        da, db = _ffn_d_hidden(tag + "_d_hidden", df, wd, a, b)
        group = [tag + "_w_down", tag + "_w_gate", tag + "_w_up"]
        local = [_mm_tn(tag + "_g_down", s, "blk", df, "full"), _mm_tn(tag + "_g_gate", da, "blk", n, "full"),
                 _mm_tn(tag + "_g_up", db, "blk", n, "full")]
        return _mm_kblk(tag + "_d_n", [(da, wg), (db, wu)], trans_w=False, carry=pair_sums(tag, group, local),
                        **epilogue), group

    def pair_sums(tag, group, local):
        theirs = _pair_exchange("pair_" + tag, local)
        for n, mine, other in zip(group, local, theirs):
            parts[n] = _add_pair("pair_sum_" + n, pos, mine, other)
        return _to_owner_carry([parts[n] for n in group])

    def keep(group, carried):
        for n, val in zip(group, carried):
            landed[n] = val

    ((dh2, dmo, g["ffn2_norm_pre"], g["mix_norm_post"]), carried), group = ffn_bwd(
        "ffn2", df2, n2, a2, b2, s2, wg2, wu2, wd2, epilogue=_ep_pre_bwd_post(1.0), rows=[h2, dy, mo],
        vecs=[ffn2_norm_pre, mix_norm_post], row_outs=[(d, F32), (d, BF16)], vec_outs=[d, d])
    keep(group, carried)
    g_wout = _mm_tn("mix_g_out", merged, "col", dmo, "full")
    dy_a, dy_b, dgates, datt, dcs, g["gate_bias"] = _mix_d_merge(dmo, wout, y_a, y_b, wao, wco, proj, gate_bias)
    g_wao = _mm_tn("attn_g_out", att, "full", dy_a, "col")
    g_wco = _mm_tn("conv_g_out", cs, "full", dy_b, "col")
    dq, dkp, dvp, dbias = _attn_bwd(proj, bias, att, lse, datt)
    g["rel_table"] = _bias_fold(dbias)[:, :rel_table.shape[2]]
    dcin, g_dw, g["conv_dw_b"], g["conv_ln_g"], g["conv_ln_b"], g["conv_glu_bias"] = _conv_bwd(
        proj, c_glu, z_conv, dcs, conv_glu_bias, dw_full, conv_ln_g, conv_ln_b)
    dproj = jnp.concatenate([dq, dkp[K_PAD:].astype(BF16), dvp[K_PAD:].astype(BF16), dcin, dgates], axis=1)
    g_win = _mm_tn("mix_g_in", u, "full", dproj, "col")
    group = ["w_out", "w_attn_out", "conv_w_out", "w_in"]
    (dh1, df1, g["mix_norm_pre"], g["ffn1_norm_post"]), carried = _mm_kblk(
        "mix_d_in", [(dproj, win)], trans_w=True, epilogue=_ep_pre_bwd_post(0.5), rows=[h1, dh2, f1],
        vecs=[mix_norm_pre, ffn1_norm_post], row_outs=[(d, F32), (d, BF16)], vec_outs=[d, d],
        carry=pair_sums("mix", group, [g_wout, g_wao, g_wco, g_win]))
    keep(group, carried)
    ((grad_x, g["ffn1_norm_pre"]), carried), group = ffn_bwd(
        "ffn1", df1, n1, a1, b1, s1, wg1, wu1, wd1, epilogue=_ep_pre_bwd_first(), rows=[xs, dh1],
        vecs=[ffn1_norm_pre], row_outs=[(d, F32)], vec_outs=[d])
    keep(group, carried)

    halves = [_add_chips("chip_sum_" + n, pos, parts[n], landed[n]) for n in big]
    other_halves = _swap_halves(halves)

    g["conv_dw_w"] = g_dw[:CONV_WIDTH]
    _, small_sum = _all_devices("sum_small", _pack([g[n] for n in small]))
    for n, val in zip(small, _unpack(small_sum, [g[n].shape for n in small])):
        g[n] = val
    g["conv_dw_w"] = lax.dynamic_slice_in_dim(g["conv_dw_w"], chip * cshard, cshard, axis=1)

    grads, deltas, new_m, new_v = {}, {}, {}, {}
    for n, mine, other in zip(big, halves, other_halves):
        gr, dl, m2, v2 = _adamw_halves("adamw_" + n, pos, shard(n), shard("m_" + n), shard("v_" + n), mine, other)
        grads[n], deltas[n], new_m[n], new_v[n] = unshard(n, gr), unshard(n, dl), unshard(n, m2), unshard(n, v2)
    shapes = [g[n].shape for n in small]
    packed = lambda pre: _pack([args[pre + n].reshape(shp) for n, shp in zip(small, shapes)])
    dl, m2, v2 = _adamw("adamw_small", packed(""), _pack([g[n] for n in small]), packed("m_"), packed("v_"))
    for n, a_, b_, c_ in zip(small, _unpack(dl, shapes), _unpack(m2, shapes), _unpack(v2, shapes)):
        shape = args[n].shape
        grads[n], deltas[n], new_m[n], new_v[n] = (g[n].reshape(shape), a_.reshape(shape), b_.reshape(shape),
                                                   c_.reshape(shape))

    return (loss, grad_x[None], *[grads[n] for n in names], *[deltas[n] for n in names],
            *[new_m[n] for n in names], *[new_v[n] for n in names])
```

```python
import functools

import numpy as np
import jax
import jax.numpy as jnp
from jax import lax
from jax.experimental import pallas as pl
from jax.experimental.pallas import tpu as pltpu

F32 = jnp.float32
BF16 = jnp.bfloat16
MESH = pl.DeviceIdType.MESH
ANY = pl.BlockSpec(memory_space=pl.ANY)

EPS = 1e-6
CHUNK = 64
LEFT_CHUNKS = 8
N_HEADS = 8
HEAD_DIM = 64
D_ATTN = N_HEADS * HEAD_DIM
D_CONV = 512
CONV_WIDTH = 31
REL_CLIP = 128
N_CHIPS = 4
N_DEV = 8
Q_BLOCK = 4 * CHUNK
K_PAD = LEFT_CHUNKS * CHUNK
K_WIN = K_PAD + Q_BLOCK
REL_EXT = 1024
REL_PAD = 384
CONV_HALO = 32
CONV_TILE = 256
COL = 512
NEG = -1e30

ADAM_LR = 0.001
ADAM_B1 = 0.9
ADAM_B2 = 0.999
ADAM_EPS = 1e-08
ADAM_WD = 0.01
ADAM_STEP = 10

VMEM_LIMIT_BYTES = 56 * 1024 * 1024


def _cparams(n_grid):
    return pltpu.CompilerParams(dimension_semantics=("arbitrary",) * n_grid, vmem_limit_bytes=VMEM_LIMIT_BYTES)


def _row_tile(rows, want):
    if rows <= want:
        return rows
    for t in range(want - want % 16, 0, -16):
        if rows % t == 0:
            return t
    raise ValueError((rows, want))


def _dot(a, w, trans_w):
    dims = (((1,), (1,)), ((), ())) if trans_w else (((1,), (0,)), ((), ()))
    return lax.dot_general(a, w, dims, preferred_element_type=F32)


class _Carry:
    LATE_STEPS = 2

    def __init__(self, ins, outs, aliases, sems, phases):
        self.ins, self.outs, self.aliases, self.sems, self.phases = ins, outs, aliases, sems, phases


def _call(name, core, grid, in_specs, out_specs, out_shape, scratch, args, carry=None):
    n_in, n_out, n_scr = len(in_specs), len(out_specs), len(scratch)
    if carry is None:
        out = pl.pallas_call(core, name=name, grid=grid, in_specs=in_specs, out_specs=out_specs, out_shape=out_shape,
                             scratch_shapes=scratch, compiler_params=_cparams(len(grid)))(*args)
        return list(out), []
    c_in, c_out = len(carry.ins), len(carry.outs)
    total = int(np.prod(grid))
    late = max(total - 1 - _Carry.LATE_STEPS, 0)

    def body(*refs):
        ins, refs = refs[:n_in], refs[n_in:]
        c_ins, refs = refs[:c_in], refs[c_in:]
        outs, refs = refs[:n_out], refs[n_out:]
        c_outs, refs = refs[:c_out], refs[c_out:]
        scr, c_sems = refs[:n_scr], refs[n_scr:]
        step = pl.program_id(0)
        for axis in range(1, len(grid)):
            step = step * grid[axis] + pl.program_id(axis)

        def run(when, at):
            for w, fn in carry.phases:
                if w == when:
                    pl.when(step == at)(functools.partial(fn, c_ins, c_outs, c_sems))

        run("first", 0)
        core(*ins, *outs, *scr)
        run("late", late)
        run("last", total - 1)

    out = pl.pallas_call(
        body, name=name, grid=grid, in_specs=list(in_specs) + [ANY] * c_in, out_specs=list(out_specs) + [ANY] * c_out,
        out_shape=list(out_shape) + list(carry.outs), scratch_shapes=list(scratch) + list(carry.sems),
        input_output_aliases={n_in + a: n_out + b for a, b in carry.aliases.items()},
        compiler_params=_cparams(len(grid)),
    )(*args, *carry.ins)
    return list(out[:n_out]), list(out[n_out:])


def _mm_nblk(name, a, w, *, trans_w, out_blocked, out_dtype, tm=1024, carry=None):
    m, k = a.shape
    nj = w.shape[0]
    nb = w.shape[1] if trans_w else w.shape[2]
    tm = _row_tile(m, tm)

    def core(a_ref, w_ref, o_ref):
        o_ref[...] = _dot(a_ref[...], w_ref[...], trans_w).astype(o_ref.dtype)

    if out_blocked:
        out_shape, out_spec = (nj, m, nb), pl.BlockSpec((None, tm, nb), lambda j, i: (j, i, 0))
    else:
        out_shape, out_spec = (m, nj * nb), pl.BlockSpec((tm, nb), lambda j, i: (i, j))
    out, carried = _call(
        name, core, (nj, m // tm),
        [pl.BlockSpec((tm, k), lambda j, i: (i, 0)), pl.BlockSpec((None,) + w.shape[1:], lambda j, i: (j, 0, 0))],
        [out_spec], [jax.ShapeDtypeStruct(out_shape, out_dtype)], [], [a, w], carry)
    return out[0] if carry is None else (out[0], carried)


def _mm_kblk(name, pairs, *, trans_w, out_dtype=F32, tm=512, sub=256, epilogue=None, rows=(), vecs=(), row_outs=None,
             vec_outs=(), carry=None):
    w0 = pairs[0][1]
    nj = w0.shape[0]
    n = w0.shape[1] if trans_w else w0.shape[2]
    kb = w0.shape[2] if trans_w else w0.shape[1]
    m = pairs[0][0].shape[-2]
    tm = _row_tile(m, tm)
    ts = _row_tile(tm, sub)
    n_pairs, n_rows, n_vecs = len(pairs), len(rows), len(vecs)
    if epilogue is None:
        epilogue, row_outs = (lambda acc, r, v: ([acc], [])), [(n, out_dtype)]
    n_ro, n_vo = len(row_outs), len(vec_outs)

    def core(*refs):
        pair_refs, refs = refs[:2 * n_pairs], refs[2 * n_pairs:]
        row_refs, refs = refs[:n_rows], refs[n_rows:]
        vec_refs, refs = refs[:n_vecs], refs[n_vecs:]
        ro_refs, vo_refs = refs[:n_ro], refs[n_ro:]
        if n_vo:
            @pl.when(pl.program_id(0) == 0)
            def _():
                for ref in vo_refs:
                    ref[...] = jnp.zeros_like(ref)

        vec_vals = [v[...] for v in vec_refs]
        sums = None
        for r0 in range(0, tm, ts):
            sub_rows = slice(r0, r0 + ts)
            acc = None
            for p in range(n_pairs):
                a_ref, w_ref = pair_refs[2 * p], pair_refs[2 * p + 1]
                for j in range(nj):
                    a_blk = a_ref[j, sub_rows, :] if len(a_ref.shape) == 3 else a_ref[sub_rows, j * kb:(j + 1) * kb]
                    part = _dot(a_blk, w_ref[j], trans_w)
                    acc = part if acc is None else acc + part
            ro, vo = epilogue(acc, [r[sub_rows, :] for r in row_refs], vec_vals)
            for ref, val in zip(ro_refs, ro):
                ref[sub_rows, :] = val.astype(ref.dtype)
            sums = vo if sums is None else [s + v for s, v in zip(sums, vo)]
        for ref, val in zip(vo_refs, sums or []):
            ref[...] += val

    in_specs, args = [], []
    for a, w in pairs:
        if a.ndim == 3:
            in_specs.append(pl.BlockSpec((nj, tm, kb), lambda i: (0, i, 0)))
        else:
            in_specs.append(pl.BlockSpec((tm, nj * kb), lambda i: (i, 0)))
        in_specs.append(pl.BlockSpec(w.shape, lambda i: (0, 0, 0), pipeline_mode=pl.Buffered(1)))
        args += [a, w]
    in_specs += [pl.BlockSpec((tm, r.shape[1]), lambda i: (i, 0)) for r in rows]
    in_specs += [pl.BlockSpec(v.shape, lambda i: (0, 0)) for v in vecs]
    out_specs = [pl.BlockSpec((tm, cols), lambda i: (i, 0)) for cols, _ in row_outs]
    out_specs += [pl.BlockSpec((1, cols), lambda i: (0, 0)) for cols in vec_outs]
    out_shape = [jax.ShapeDtypeStruct((m, cols), dt) for cols, dt in row_outs]
    out_shape += [jax.ShapeDtypeStruct((1, cols), F32) for cols in vec_outs]
    return _call(name, core, (m // tm,), in_specs, out_specs, out_shape, [], args + list(rows) + list(vecs), carry)


def _mm_tn(name, a, a_mode, b, b_mode, *, out_dtype=BF16, tt=2048):
    nj = N_CHIPS
    t = a.shape[-2]
    tt = _row_tile(t, tt)

    def spec(x, mode):
        if mode == "full":
            return x.shape[1], pl.BlockSpec((tt, x.shape[1]), lambda j, s: (s, 0))
        if mode == "col":
            cb = x.shape[1] // nj
            return cb, pl.BlockSpec((tt, cb), lambda j, s: (s, j))
        return x.shape[2], pl.BlockSpec((None, tt, x.shape[2]), lambda j, s: (j, s, 0))

    ca, a_spec = spec(a, a_mode)
    cb, b_spec = spec(b, b_mode)
    n_steps = t // tt

    def body(a_ref, b_ref, o_ref, acc_ref):
        s = pl.program_id(1)

        @pl.when(s == 0)
        def _():
            acc_ref[...] = jnp.zeros_like(acc_ref)

        acc_ref[...] += lax.dot_general(a_ref[...], b_ref[...], (((0,), (0,)), ((), ())),
                                        preferred_element_type=F32)

        @pl.when(s == n_steps - 1)
        def _():
            o_ref[...] = acc_ref[...].astype(o_ref.dtype)

    return pl.pallas_call(
        body, name=name, grid=(nj, n_steps), in_specs=[a_spec, b_spec],
        out_specs=pl.BlockSpec((None, ca, cb), lambda j, s: (j, 0, 0)),
        out_shape=jax.ShapeDtypeStruct((nj, ca, cb), out_dtype),
        scratch_shapes=[pltpu.VMEM((ca, cb), F32)], compiler_params=_cparams(2),
    )(a, b)


def _rowwise(name, fn, rows, vecs, row_outs, vec_outs, *, tm=256, carry=None):
    nrows = rows[0][0].shape[0]
    tm = _row_tile(nrows, tm)
    n_r, n_v, n_ro, n_vo = len(rows), len(vecs), len(row_outs), len(vec_outs)

    def body(*refs):
        r_vals = [r[...] for r in refs[:n_r]]
        v_vals = [r[...] for r in refs[n_r:n_r + n_v]]
        ro_refs = refs[n_r + n_v:n_r + n_v + n_ro]
        vo_refs = refs[n_r + n_v + n_ro:]
        ro, vo = fn(r_vals, v_vals)
        for ref, val in zip(ro_refs, ro):
            ref[...] = val.astype(ref.dtype)
        if n_vo:
            @pl.when(pl.program_id(0) == 0)
            def _():
                for ref in vo_refs:
                    ref[...] = jnp.zeros_like(ref)

            for ref, val in zip(vo_refs, vo):
                ref[...] += val

    in_specs = [pl.BlockSpec((tm, cols), functools.partial(lambda i, cb: (i, cb), cb=cb)) for _, cols, cb in rows]
    in_specs += [pl.BlockSpec(v.shape, functools.partial(lambda i, nd: (0,) * nd, nd=v.ndim)) for v in vecs]
    out_specs = [pl.BlockSpec((tm, cols), lambda i: (i, 0)) for cols, _ in row_outs]
    out_specs += [pl.BlockSpec((1, cols), lambda i: (0, 0)) for cols in vec_outs]
    out_shape = [jax.ShapeDtypeStruct((nrows, cols), dt) for cols, dt in row_outs]
    out_shape += [jax.ShapeDtypeStruct((1, cols), F32) for cols in vec_outs]
    out, carried = _call(name, body, (nrows // tm,), in_specs, out_specs, out_shape, [],
                         [r[0] for r in rows] + list(vecs), carry)
    return out if carry is None else (out, carried)


def _whole(x):
    return (x, x.shape[1], 0)


def _colsum(x):
    return jnp.sum(x, axis=0, keepdims=True)


def _rstd(x):
    return lax.rsqrt(jnp.mean(x * x, axis=-1, keepdims=True) + EPS)


def _rms_bwd(dn, x, g):
    r = _rstd(x)
    c = dn * g
    dx = r * c - x * (r * r * r) * jnp.mean(c * x, axis=-1, keepdims=True)
    return dx, _colsum(dn * x * r)


def _rms_fwd(name, x, g, carry):
    def fn(r, v):
        (xv,), (gv,) = r, v
        return [xv * _rstd(xv) * gv], []

    (n,), carried = _rowwise(name, fn, [_whole(x)], [g], [(x.shape[1], BF16)], [], carry=carry)
    return n, carried


def _ep_post_res_pre(scale):
    def epilogue(acc, rows, vecs):
        (resid,), (g_post, g_next) = rows, vecs
        h = resid + scale * (acc * _rstd(acc) * g_post)
        return [acc, h, h * _rstd(h) * g_next], []

    return epilogue


def _post_bwd(dh, f, g_post, scale):
    return _rms_bwd(scale * dh, f, g_post)


def _ep_loss(scale, d):
    def epilogue(acc, rows, vecs):
        (resid, target), (g_post,) = rows, vecs
        err = resid + scale * (acc * _rstd(acc) * g_post) - target
        dy = err * (1.0 / d)
        df, dg_post = _post_bwd(dy, acc, g_post, scale)
        return [dy, df], [_colsum(err * err), dg_post]

    return epilogue


def _ep_pre_bwd_post(scale_prev):
    def epilogue(acc, rows, vecs):
        (h, dh_up, f_prev), (g_pre, g_post_prev) = rows, vecs
        dx, dg_pre = _rms_bwd(acc, h, g_pre)
        dh = dh_up + dx
        df, dg_post = _post_bwd(dh, f_prev, g_post_prev, scale_prev)
        return [dh, df], [dg_pre, dg_post]

    return epilogue


def _ep_pre_bwd_first():
    def epilogue(acc, rows, vecs):
        (x, dh_up), (g_pre,) = rows, vecs
        dx, dg_pre = _rms_bwd(acc, x, g_pre)
        return [dh_up + dx], [dg_pre]

    return epilogue


def _gate_specs(d, tm):
    first = (3 * D_ATTN + 2 * D_CONV) // COL
    return [pl.BlockSpec((tm, COL), functools.partial(lambda i, cb: (i, cb), cb=first + p)) for p in range(2 * d // COL)]


def _gate(piece_refs, bias_ref, c0, width):
    p, off = divmod(c0, COL)
    return jax.nn.sigmoid(piece_refs[p][:, off:off + width].astype(F32) + bias_ref[:, c0:c0 + width])


def _resident(w):
    return pl.BlockSpec(w.shape, functools.partial(lambda i, nd: (0,) * nd, nd=w.ndim), pipeline_mode=pl.Buffered(1))


def _mix_merge(att, cs, wao, wco, proj, gate_bias, tm=512):
    t = att.shape[0]
    nj, _, nb = wao.shape
    d = nj * nb
    tm = _row_tile(t, tm)
    gate_specs = _gate_specs(d, tm)
    n_p = len(gate_specs)

    def body(att_ref, cs_ref, wao_ref, wco_ref, *rest):
        pieces, (gb_ref, ya_ref, yb_ref, m_ref) = rest[:n_p], rest[n_p:]
        av, cv = att_ref[...], cs_ref[...]
        for j in range(nj):
            cols = slice(j * nb, (j + 1) * nb)
            ya = _dot(av, wao_ref[j], False)
            yb = _dot(cv, wco_ref[j], False)
            merged = _gate(pieces, gb_ref, j * nb, nb) * ya + _gate(pieces, gb_ref, d + j * nb, nb) * yb
            ya_ref[:, cols] = ya.astype(ya_ref.dtype)
            yb_ref[:, cols] = yb.astype(yb_ref.dtype)
            m_ref[:, cols] = merged.astype(m_ref.dtype)

    row = lambda x: pl.BlockSpec((tm, x.shape[1]), lambda i: (i, 0))
    out_spec = pl.BlockSpec((tm, d), lambda i: (i, 0))
    return pl.pallas_call(
        body, name="mix_merge", grid=(t // tm,),
        in_specs=[row(att), row(cs), _resident(wao), _resident(wco)] + gate_specs + [_resident(gate_bias)],
        out_specs=[out_spec] * 3, out_shape=[jax.ShapeDtypeStruct((t, d), BF16)] * 3, compiler_params=_cparams(1),
    )(att, cs, wao, wco, *([proj] * n_p), gate_bias)


def _mix_d_merge(dmo, wout, y_a, y_b, wao, wco, proj, gate_bias, tm=512):
    t, d = dmo.shape
    nj, _, nb = wao.shape
    ka, kc = wao.shape[1], wco.shape[1]
    tm = _row_tile(t, tm)
    gate_specs = _gate_specs(d, tm)
    n_p = len(gate_specs)

    def body(dmo_ref, wout_ref, ya_ref, yb_ref, wao_ref, wco_ref, *rest):
        pieces, (gb_ref, dya_ref, dyb_ref, dg_ref, datt_ref, dcs_ref, dgb_ref) = rest[:n_p], rest[n_p:]

        @pl.when(pl.program_id(0) == 0)
        def _():
            dgb_ref[...] = jnp.zeros_like(dgb_ref)

        dmo_v = dmo_ref[...]
        datt = dcs = None
        for j in range(nj):
            cols, cols_b = slice(j * nb, (j + 1) * nb), slice(d + j * nb, d + (j + 1) * nb)
            dm = _dot(dmo_v, wout_ref[j], True)
            ga, gb = _gate(pieces, gb_ref, j * nb, nb), _gate(pieces, gb_ref, d + j * nb, nb)
            dya, dyb = (dm * ga).astype(BF16), (dm * gb).astype(BF16)
            dga = dm * ya_ref[:, cols].astype(F32) * ga * (1.0 - ga)
            dgb = dm * yb_ref[:, cols].astype(F32) * gb * (1.0 - gb)
            dya_ref[:, cols], dyb_ref[:, cols] = dya, dyb
            dg_ref[:, cols], dg_ref[:, cols_b] = dga.astype(dg_ref.dtype), dgb.astype(dg_ref.dtype)
            dgb_ref[:, cols] += _colsum(dga)
            dgb_ref[:, cols_b] += _colsum(dgb)
            pa, pc = _dot(dya, wao_ref[j], True), _dot(dyb, wco_ref[j], True)
            datt, dcs = (pa, pc) if datt is None else (datt + pa, dcs + pc)
        datt_ref[...] = datt.astype(datt_ref.dtype)
        dcs_ref[...] = dcs.astype(dcs_ref.dtype)

    row = lambda cols: pl.BlockSpec((tm, cols), lambda i: (i, 0))
    return pl.pallas_call(
        body, name="mix_d_merge", grid=(t // tm,),
        in_specs=[row(d), _resident(wout), row(d), row(d), _resident(wao), _resident(wco)] + gate_specs
        + [_resident(gate_bias)],
        out_specs=[row(d), row(d), row(2 * d), row(ka), row(kc), pl.BlockSpec((1, 2 * d), lambda i: (0, 0))],
        out_shape=[jax.ShapeDtypeStruct((t, d), BF16), jax.ShapeDtypeStruct((t, d), BF16),
                   jax.ShapeDtypeStruct((t, 2 * d), BF16), jax.ShapeDtypeStruct((t, ka), BF16),
                   jax.ShapeDtypeStruct((t, kc), F32), jax.ShapeDtypeStruct((1, 2 * d), F32)],
        compiler_params=_cparams(1),
    )(dmo, wout, y_a, y_b, wao, wco, *([proj] * n_p), gate_bias)


def _adamw_math(wv, gv, mv, vv):
    m2 = ADAM_B1 * mv + (1.0 - ADAM_B1) * gv
    v2 = ADAM_B2 * vv + (1.0 - ADAM_B2) * (gv * gv)
    m_hat = m2 / (1.0 - ADAM_B1 ** ADAM_STEP)
    v_hat = v2 / (1.0 - ADAM_B2 ** ADAM_STEP)
    delta = -ADAM_LR * (m_hat / (jnp.sqrt(v_hat) + ADAM_EPS) + ADAM_WD * wv)
    return delta, m2, v2


def _adamw(name, w, g, m, v):
    def fn(r, _):
        return list(_adamw_math(*r)), []

    c = w.shape[1]
    return _rowwise(name, fn, [_whole(w), _whole(g), _whole(m), _whole(v)], [], [(c, F32)] * 3, [], tm=256)


POS_C, POS_CHIP, POS_PEER = 0, 1, 2


def _placed_call(body, name, pos, grid, in_specs, out_specs, out_shape, args):
    return pl.pallas_call(
        body, name=name, out_shape=out_shape, compiler_params=_cparams(len(grid)),
        grid_spec=pltpu.PrefetchScalarGridSpec(num_scalar_prefetch=1, grid=grid, in_specs=in_specs,
                                               out_specs=out_specs),
    )(pos, *args)


def _cast_into(name, pos, w):
    r, cols = w.shape
    tm = _row_tile(r, 256)

    def body(pos_ref, w_ref, o_ref):
        o_ref[...] = w_ref[...].astype(o_ref.dtype)

    return _placed_call(
        body, name, pos, (r // tm,), [pl.BlockSpec((tm, cols), lambda i, pos: (i, 0))],
        pl.BlockSpec((None, tm, cols), lambda i, pos: (pos[POS_CHIP], i, 0)),
        jax.ShapeDtypeStruct((N_CHIPS, r, cols), BF16), [w])


def _add_pair(name, pos, grad, landed):
    nj, half, cols = landed.shape
    tm = _row_tile(half, 256)
    nb = half // tm

    def body(pos_ref, g_ref, l_ref, o_ref):
        o_ref[...] = (g_ref[...].astype(F32) + l_ref[...].astype(F32)).astype(o_ref.dtype)

    spec = pl.BlockSpec((None, tm, cols), lambda j, i, pos: (j, i, 0))
    return _placed_call(
        body, name, pos, (nj, nb),
        [pl.BlockSpec((None, tm, cols), lambda j, i, pos: (j, pos[POS_C] * nb + i, 0)), spec], spec,
        jax.ShapeDtypeStruct(landed.shape, BF16), [grad, landed])


def _add_chips(name, pos, part, landed):
    _, half, cols = landed.shape
    tm = _row_tile(half, 256)

    def body(pos_ref, p_ref, l0_ref, l1_ref, l2_ref, o_ref):
        acc = p_ref[...].astype(F32)
        for ref in (l0_ref, l1_ref, l2_ref):
            acc = acc + ref[...].astype(F32)
        o_ref[...] = acc

    slot = lambda at: pl.BlockSpec((None, tm, cols), functools.partial(lambda i, pos, at: (pos[at], i, 0), at=at))
    return _placed_call(
        body, name, pos, (half // tm,), [slot(POS_CHIP)] + [slot(POS_PEER + k) for k in range(3)],
        pl.BlockSpec((tm, cols), lambda i, pos: (i, 0)), jax.ShapeDtypeStruct((half, cols), F32),
        [part, landed, landed, landed])


def _adamw_halves(name, pos, w, m, v, own, landed):
    r, cols = w.shape
    half = own.shape[0]
    tm = _row_tile(half, 256)
    nb = half // tm

    def body(pos_ref, w_ref, m_ref, v_ref, own_ref, land_ref, g_out, d_out, m_out, v_out):
        mine = pl.program_id(0) == pos_ref[POS_C]
        g = jnp.where(mine, own_ref[...], land_ref[...])
        delta, m2, v2 = _adamw_math(w_ref[...], g, m_ref[...], v_ref[...])
        g_out[...] = g
        d_out[...] = delta
        m_out[...] = m2
        v_out[...] = v2

    full = pl.BlockSpec((tm, cols), lambda h, i, pos: (h * nb + i, 0))
    part = pl.BlockSpec((tm, cols), lambda h, i, pos: (i, 0))
    return _placed_call(
        body, name, pos, (2, nb), [full, full, full, part, part], [full] * 4,
        [jax.ShapeDtypeStruct((r, cols), F32)] * 4, [w, m, v, own, landed])


N_START = K_PAD // Q_BLOCK


def _rel_onehot(n_q):
    e = np.arange(REL_EXT)
    dist = K_PAD - (e - (n_q - 1))
    idx = np.clip(dist, -REL_CLIP, REL_CLIP) + REL_CLIP
    return (np.arange(REL_PAD)[:, None] == idx[None, :]).astype(np.float32)


def _skew(x, left):
    row = lax.broadcasted_iota(jnp.int32, x.shape, 0)
    for bit in range(x.shape[0].bit_length() - 1):
        amount = 1 << bit
        rolled = pltpu.roll(x, REL_EXT - amount if left else amount, 1)
        x = jnp.where((row >> bit) & 1 == 1, rolled, x)
    return x


def _bias_expand(table_pad):
    onehot = jnp.asarray(_rel_onehot(Q_BLOCK))

    def body(t_ref, oh_ref, o_ref):
        ext = jnp.dot(t_ref[...], oh_ref[...], precision=lax.Precision.HIGHEST, preferred_element_type=F32)
        qc = lax.broadcasted_iota(jnp.int32, (Q_BLOCK, K_WIN), 0) // CHUNK
        kpos = lax.broadcasted_iota(jnp.int32, (Q_BLOCK, K_WIN), 1)
        band = (kpos // CHUNK >= qc) & (kpos // CHUNK <= qc + LEFT_CHUNKS)
        for h in range(N_HEADS):
            rows = jnp.broadcast_to(ext[h:h + 1, :], (Q_BLOCK, REL_EXT))
            rolled = _skew(pltpu.roll(rows, REL_EXT - (Q_BLOCK - 1), 1), left=False)[:, :K_WIN]
            for v in range(N_START + 1):
                o_ref[v, h] = jnp.where(band & (kpos + v * Q_BLOCK >= K_PAD), rolled, NEG)

    return pl.pallas_call(
        body, name="bias_expand", out_shape=jax.ShapeDtypeStruct((N_START + 1, N_HEADS, Q_BLOCK, K_WIN), F32),
        compiler_params=pltpu.CompilerParams(vmem_limit_bytes=VMEM_LIMIT_BYTES),
    )(table_pad, onehot)


def _bias_fold(dbias):
    onehot_t = jnp.asarray(_rel_onehot(CHUNK).T)

    def body(d_ref, oh_ref, o_ref, ext_ref):
        for h in range(N_HEADS):
            x = jnp.concatenate([d_ref[h], jnp.zeros((CHUNK, REL_EXT - K_WIN), F32)], axis=1)
            rolled = _skew(pltpu.roll(x, CHUNK - 1, 1), left=True)
            ext_ref[h:h + 1, :] = jnp.sum(rolled, axis=0, keepdims=True)
        o_ref[...] = jnp.dot(ext_ref[...], oh_ref[...], precision=lax.Precision.HIGHEST,
                             preferred_element_type=F32)

    return pl.pallas_call(
        body, name="bias_fold", out_shape=jax.ShapeDtypeStruct((N_HEADS, REL_PAD), F32),
        scratch_shapes=[pltpu.VMEM((N_HEADS, REL_EXT), F32)],
        compiler_params=pltpu.CompilerParams(vmem_limit_bytes=VMEM_LIMIT_BYTES),
    )(dbias, onehot_t)


def _head_lanes():
    lane = lax.broadcasted_iota(jnp.int32, (1, 2 * HEAD_DIM), 1)
    return [lane < HEAD_DIM, lane >= HEAD_DIM]


def _only(mask, x, scale=None):
    x = jnp.where(mask, x, jnp.zeros_like(x))
    return x if scale is None else x * scale


def _contract_lanes(a, b):
    return lax.dot_general(a, b, (((1,), (1,)), ((), ())), preferred_element_type=F32)


def _contract_rows(a, b):
    return lax.dot_general(a, b, (((0,), (0,)), ((), ())), preferred_element_type=F32)


PAIR = 2 * HEAD_DIM
N_PAIRS = D_ATTN // PAIR


def _attn_specs(pairs):
    width = pairs * PAIR
    per = D_ATTN // width
    row_spec = pl.BlockSpec((Q_BLOCK, width), lambda g, i: (i, g))
    kv_specs = [pl.BlockSpec((Q_BLOCK, width),
                             functools.partial(lambda g, i, kk, c0: (jnp.maximum(i + kk - N_START, 0), c0 + g),
                                               kk=kk, c0=c0))
                for c0 in (per, 2 * per) for kk in range(K_WIN // Q_BLOCK)]
    bias_spec = pl.BlockSpec((None, 2 * pairs, Q_BLOCK, K_WIN), lambda g, i: (jnp.minimum(i, N_START), g, 0, 0))
    return row_spec, kv_specs, bias_spec


def _attn_fwd(proj, bias, pairs=N_PAIRS):
    t = proj.shape[0]
    n_win = K_WIN // Q_BLOCK

    def body(q_ref, *refs):
        k_refs, v_refs = refs[:n_win], refs[n_win:2 * n_win]
        b_ref, o_ref, lse_ref = refs[2 * n_win:]
        for pp in range(pairs):
            cols = slice(pp * PAIR, (pp + 1) * PAIR)
            k = jnp.concatenate([r[:, cols] for r in k_refs], axis=0)
            v = jnp.concatenate([r[:, cols] for r in v_refs], axis=0)
            q = q_ref[:, cols]
            o = lse = None
            for hh, lanes in enumerate(_head_lanes()):
                s = _contract_lanes(_only(lanes, q, HEAD_DIM ** -0.5), k) + b_ref[2 * pp + hh]
                m = jnp.max(s, axis=1, keepdims=True)
                p = jnp.exp(s - m)
                l = jnp.sum(p, axis=1, keepdims=True)
                oh = jnp.dot(p.astype(BF16), v, preferred_element_type=F32) / l
                lse_h = jnp.broadcast_to(m + jnp.log(l), oh.shape)
                o, lse = (oh, lse_h) if o is None else (jnp.where(lanes, oh, o), jnp.where(lanes, lse_h, lse))
            o_ref[:, cols] = o.astype(o_ref.dtype)
            lse_ref[:, cols] = lse

    row_spec, kv_specs, bias_spec = _attn_specs(pairs)
    return pl.pallas_call(
        body, name="attn_fwd", grid=(N_PAIRS // pairs, t // Q_BLOCK),
        in_specs=[row_spec] + kv_specs + [bias_spec], out_specs=[row_spec, row_spec],
        out_shape=[jax.ShapeDtypeStruct((t, D_ATTN), BF16), jax.ShapeDtypeStruct((t, D_ATTN), F32)],
        compiler_params=_cparams(2),
    )(*([proj] * (1 + 2 * n_win)), bias)


def _attn_bwd(proj, bias, att, lse, datt, pairs=2):
    t = proj.shape[0]
    n_win = K_WIN // Q_BLOCK
    n_blocks = t // Q_BLOCK

    def body(q_ref, *refs):
        k_refs, v_refs = refs[:n_win], refs[n_win:2 * n_win]
        b_ref, o_ref, lse_ref, do_ref, dq_ref, dk_ref, dv_ref, db_ref, dk_acc, dv_acc = refs[2 * n_win:]
        i = pl.program_id(1)

        @pl.when(i == 0)
        def _():
            dk_acc[...] = jnp.zeros_like(dk_acc)
            dv_acc[...] = jnp.zeros_like(dv_acc)
            db_ref[...] = jnp.zeros_like(db_ref)

        rows = pl.ds(pl.multiple_of(i * Q_BLOCK, Q_BLOCK), K_WIN)
        scale = HEAD_DIM ** -0.5
        for pp in range(pairs):
            cols = slice(pp * PAIR, (pp + 1) * PAIR)
            k = jnp.concatenate([r[:, cols] for r in k_refs], axis=0)
            v = jnp.concatenate([r[:, cols] for r in v_refs], axis=0)
            q, do, o = q_ref[:, cols], do_ref[:, cols], o_ref[:, cols].astype(F32)
            dq = dk = dv = None
            for hh, lanes in enumerate(_head_lanes()):
                qh, doh = _only(lanes, q, scale), _only(lanes, do)
                s = _contract_lanes(qh, k) + b_ref[2 * pp + hh]
                lse_col = pp * PAIR + hh * HEAD_DIM
                p = jnp.exp(s - lse_ref[:, lse_col:lse_col + 1])
                delta = jnp.sum(doh.astype(F32) * o, axis=1, keepdims=True)
                ds = p * (_contract_lanes(doh, v) - delta)
                folded = ds[:CHUNK]
                for c in range(1, Q_BLOCK // CHUNK):
                    folded = folded + pltpu.roll(ds[c * CHUNK:(c + 1) * CHUNK], K_WIN - c * CHUNK, 1)
                db_ref[2 * pp + hh] += folded
                dsb = ds.astype(BF16)
                dqh = jnp.dot(dsb, k, preferred_element_type=F32)
                dq = dqh if dq is None else jnp.where(lanes, dqh, dq)
                dkh, dvh = _contract_rows(dsb, qh), _contract_rows(p.astype(BF16), doh)
                dk, dv = (dkh, dvh) if dk is None else (dk + dkh, dv + dvh)
            dq_ref[:, cols] = (dq * scale).astype(dq_ref.dtype)
            dk_acc[rows, cols] += dk
            dv_acc[rows, cols] += dv

        @pl.when(i == n_blocks - 1)
        def _():
            dk_ref[...] = dk_acc[K_PAD:, :].astype(dk_ref.dtype)
            dv_ref[...] = dv_acc[K_PAD:, :].astype(dv_ref.dtype)

    width = pairs * PAIR
    row_spec, kv_specs, bias_spec = _attn_specs(pairs)
    full_spec = pl.BlockSpec((t, width), lambda g, i: (0, g))
    return pl.pallas_call(
        body, name="attn_bwd", grid=(N_PAIRS // pairs, n_blocks),
        in_specs=[row_spec] + kv_specs + [bias_spec, row_spec, row_spec, row_spec],
        out_specs=[row_spec, full_spec, full_spec,
                   pl.BlockSpec((2 * pairs, CHUNK, K_WIN), lambda g, i: (g, 0, 0))],
        out_shape=[jax.ShapeDtypeStruct((t, D_ATTN), BF16)] * 3 + [jax.ShapeDtypeStruct((N_HEADS, CHUNK, K_WIN), F32)],
        scratch_shapes=[pltpu.VMEM((t + K_PAD, width), F32)] * 2, compiler_params=_cparams(2),
    )(*([proj] * (1 + 2 * n_win)), bias, att, lse, datt)


CONV_LEAD = CONV_HALO - (CONV_WIDTH - 1)
CONV_LANES = 128
CONV_ROWS = 64


def _conv_specs(t):
    tt = _row_tile(t, CONV_TILE)
    per = tt // CONV_HALO
    n_halo = t // CONV_HALO
    tile = lambda cb: pl.BlockSpec((tt, COL), functools.partial(lambda i, cb: (i, cb), cb=cb))
    prev = lambda cb: pl.BlockSpec((CONV_HALO, COL),
                                   functools.partial(lambda i, cb: (jnp.maximum(i * per - 1, 0), cb), cb=cb))
    nxt = lambda cb: pl.BlockSpec((CONV_HALO, COL),
                                  functools.partial(lambda i, cb: (jnp.minimum((i + 1) * per, n_halo - 1), cb), cb=cb))
    vec = lambda shape: pl.BlockSpec(shape, lambda i: (0, 0))
    return tt, tile, prev, nxt, vec


def _glu(ca, cg, bias):
    return (ca.astype(F32) + bias[:, :D_CONV]) * jax.nn.sigmoid(cg.astype(F32) + bias[:, D_CONV:])


SUBLANES = 8


def _shift_copies(ext_ref):
    n = ext_ref.shape[1] - SUBLANES
    for s in range(1, SUBLANES):
        ext_ref[s, 0:n, :] = ext_ref[0, s:s + n, :]


def _tap_tiles(ext_ref, first_row, r0, lanes):
    n_g = CONV_ROWS // SUBLANES
    for s in range(SUBLANES):
        taps = [w for w in range(CONV_WIDTH) if first_row(w) % SUBLANES == s]
        if not taps:
            continue
        lo = min(first_row(w) for w in taps) - s
        n_tiles = (max(first_row(w) for w in taps) - s - lo) // SUBLANES + n_g
        tiles = [ext_ref[s, r0 + lo + SUBLANES * b:r0 + lo + SUBLANES * (b + 1), lanes] for b in range(n_tiles)]
        for w in taps:
            k = (first_row(w) - s - lo) // SUBLANES
            yield w, tiles[k:k + n_g]


def _taps(ext_ref, tt, first_row, w_ref, out_ref):
    n_g = CONV_ROWS // SUBLANES
    for l0 in range(0, D_CONV, CONV_LANES):
        lanes = slice(l0, l0 + CONV_LANES)
        for r0 in range(0, tt, CONV_ROWS):
            acc = [jnp.zeros((SUBLANES, CONV_LANES), F32)] * n_g
            for w, tiles in _tap_tiles(ext_ref, first_row, r0, lanes):
                weight = jnp.broadcast_to(w_ref[w:w + 1, lanes], (SUBLANES, CONV_LANES))
                acc = [a + t * weight for a, t in zip(acc, tiles)]
            for g in range(n_g):
                out_ref[r0 + SUBLANES * g:r0 + SUBLANES * (g + 1), lanes] = acc[g]


def _tap_sums(ext_ref, tt, first_row, x_ref, out_ref):
    n_g = CONV_ROWS // SUBLANES
    for l0 in range(0, D_CONV, CONV_LANES):
        lanes = slice(l0, l0 + CONV_LANES)
        acc = [jnp.zeros((SUBLANES, CONV_LANES), F32)] * CONV_WIDTH
        for r0 in range(0, tt, CONV_ROWS):
            x = [x_ref[0, r0 + SUBLANES * g:r0 + SUBLANES * (g + 1), lanes] for g in range(n_g)]
            for w, tiles in _tap_tiles(ext_ref, first_row, r0, lanes):
                part = tiles[0] * x[0]
                for g in range(1, n_g):
                    part = part + tiles[g] * x[g]
                acc[w] = acc[w] + part
        for w in range(CONV_WIDTH):
            out_ref[w:w + 1, lanes] += jnp.sum(acc[w], axis=0, keepdims=True)


def _conv_fwd(proj, glu_bias, dw, dw_b, ln_g, ln_b):
    t = proj.shape[0]
    tt, tile, prev, nxt, vec = _conv_specs(t)
    ca_blk, cg_blk = 3 * D_ATTN // COL, 3 * D_ATTN // COL + 1

    def body(ca_ref, cg_ref, pa_ref, pg_ref, gb_ref, dw_ref, dwb_ref, g_ref, b_ref, cs_ref, c_ref, z_ref, ext_ref):
        i = pl.program_id(0)
        bias = gb_ref[...]
        c = _glu(ca_ref[...], cg_ref[...], bias)
        halo = _glu(pa_ref[...], pg_ref[...], bias)
        ext_ref[0, 0:CONV_HALO, :] = jnp.where(i == 0, 0.0, halo)
        ext_ref[0, CONV_HALO:, :] = c
        _shift_copies(ext_ref)
        c_ref[...] = c
        _taps(ext_ref, tt, lambda w: CONV_LEAD + w, dw_ref, z_ref)
        z = z_ref[...] + dwb_ref[...]
        z_ref[...] = z
        mu = jnp.mean(z, axis=-1, keepdims=True)
        zc = z - mu
        y = zc * lax.rsqrt(jnp.mean(zc * zc, axis=-1, keepdims=True) + EPS) * g_ref[...] + b_ref[...]
        cs_ref[...] = (y * jax.nn.sigmoid(y)).astype(cs_ref.dtype)

    out_spec = pl.BlockSpec((tt, D_CONV), lambda i: (i, 0))
    return pl.pallas_call(
        body, name="conv_fwd", grid=(t // tt,),
        in_specs=[tile(ca_blk), tile(cg_blk), prev(ca_blk), prev(cg_blk), vec(glu_bias.shape), vec(dw.shape),
                  vec(dw_b.shape), vec(ln_g.shape), vec(ln_b.shape)],
        out_specs=[out_spec] * 3,
        out_shape=[jax.ShapeDtypeStruct((t, D_CONV), BF16), jax.ShapeDtypeStruct((t, D_CONV), F32),
                   jax.ShapeDtypeStruct((t, D_CONV), F32)],
        scratch_shapes=[pltpu.VMEM((SUBLANES, tt + CONV_HALO, D_CONV), F32)], compiler_params=_cparams(1),
    )(proj, proj, proj, proj, glu_bias, dw, dw_b, ln_g, ln_b)


def _conv_bwd(proj, c, z, dcs, glu_bias, dw, ln_g, ln_b):
    t = proj.shape[0]
    tt, tile, prev, nxt, vec = _conv_specs(t)
    n_tiles = t // tt
    ca_blk, cg_blk = 3 * D_ATTN // COL, 3 * D_ATTN // COL + 1

    def ln_bwd(zv, dcsv, g, b):
        mu = jnp.mean(zv, axis=-1, keepdims=True)
        zc = zv - mu
        rstd = lax.rsqrt(jnp.mean(zc * zc, axis=-1, keepdims=True) + EPS)
        zhat = zc * rstd
        y = zhat * g + b
        sig = jax.nn.sigmoid(y)
        dy = dcsv * sig * (1.0 + y * (1.0 - sig))
        dzh = dy * g
        dz = rstd * (dzh - jnp.mean(dzh, axis=-1, keepdims=True) - zhat * jnp.mean(dzh * zhat, axis=-1, keepdims=True))
        return dz, dy, zhat

    def body(ca_ref, cg_ref, c_ref, cprev_ref, z_ref, znext_ref, dcs_ref, dcsnext_ref, gb_ref, dw_ref, g_ref, b_ref,
             dcin_ref, ddw_ref, ddwb_ref, dg_ref, db_ref, dgb_ref, cext_ref, dzext_ref, dc_ref):
        i = pl.program_id(0)

        @pl.when(i == 0)
        def _():
            for ref in (ddw_ref, ddwb_ref, dg_ref, db_ref, dgb_ref):
                ref[...] = jnp.zeros_like(ref)

        g, b = g_ref[...], b_ref[...]
        dz, dy, zhat = ln_bwd(z_ref[...], dcs_ref[...], g, b)
        dz_next, _, _ = ln_bwd(znext_ref[...], dcsnext_ref[...], g, b)
        dg_ref[...] += _colsum(dy * zhat)
        db_ref[...] += _colsum(dy)
        ddwb_ref[...] += _colsum(dz)
        dzext_ref[0, 0:tt, :] = dz
        dzext_ref[0, tt:, :] = jnp.where(i == n_tiles - 1, 0.0, dz_next)
        _shift_copies(dzext_ref)
        cext_ref[0, 0:CONV_HALO, :] = jnp.where(i == 0, 0.0, cprev_ref[...])
        cext_ref[0, CONV_HALO:, :] = c_ref[...]
        _shift_copies(cext_ref)
        _taps(dzext_ref, tt, lambda w: CONV_WIDTH - 1 - w, dw_ref, dc_ref)
        _tap_sums(cext_ref, tt, lambda w: CONV_LEAD + w, dzext_ref, ddw_ref)
        bias = gb_ref[...]
        a_in = ca_ref[...].astype(F32) + bias[:, :D_CONV]
        sg = jax.nn.sigmoid(cg_ref[...].astype(F32) + bias[:, D_CONV:])
        dc = dc_ref[...]
        dcin = jnp.concatenate([dc * sg, dc * a_in * sg * (1.0 - sg)], axis=1)
        dcin_ref[...] = dcin.astype(dcin_ref.dtype)
        dgb_ref[...] += _colsum(dcin)

    row = lambda: pl.BlockSpec((tt, D_CONV), lambda i: (i, 0))
    per = tt // CONV_HALO
    n_halo = t // CONV_HALO
    prev_row = pl.BlockSpec((CONV_HALO, D_CONV), lambda i: (jnp.maximum(i * per - 1, 0), 0))
    next_row = lambda: pl.BlockSpec((CONV_HALO, D_CONV), lambda i: (jnp.minimum((i + 1) * per, n_halo - 1), 0))
    acc = lambda shape: pl.BlockSpec(shape, lambda i: (0, 0))
    return pl.pallas_call(
        body, name="conv_bwd", grid=(n_tiles,),
        in_specs=[tile(ca_blk), tile(cg_blk), row(), prev_row, row(), next_row(), row(), next_row(),
                  vec(glu_bias.shape), vec(dw.shape), vec(ln_g.shape), vec(ln_b.shape)],
        out_specs=[pl.BlockSpec((tt, 2 * D_CONV), lambda i: (i, 0)), acc(dw.shape), acc((1, D_CONV)),
                   acc((1, D_CONV)), acc((1, D_CONV)), acc((1, 2 * D_CONV))],
        out_shape=[jax.ShapeDtypeStruct((t, 2 * D_CONV), BF16), jax.ShapeDtypeStruct(dw.shape, F32),
                   jax.ShapeDtypeStruct((1, D_CONV), F32), jax.ShapeDtypeStruct((1, D_CONV), F32),
                   jax.ShapeDtypeStruct((1, D_CONV), F32), jax.ShapeDtypeStruct((1, 2 * D_CONV), F32)],
        scratch_shapes=[pltpu.VMEM((SUBLANES, tt + CONV_HALO, D_CONV), F32),
                        pltpu.VMEM((SUBLANES, tt + CONV_HALO, D_CONV), F32), pltpu.VMEM((tt, D_CONV), F32)],
        compiler_params=_cparams(1),
    )(proj, proj, c, c, z, z, dcs, dcs, glu_bias, dw, ln_g, ln_b)


def _place():
    x, y, c = lax.axis_index("x"), lax.axis_index("y"), lax.axis_index("c")
    chips = [(1 - x, y), (x, 1 - y), (1 - x, 1 - y)]
    return x, y, c, chips


def _chip_index(chip):
    return 2 * chip[0] + chip[1]


def _half_rows(c, half):
    return pl.ds(pl.multiple_of(c * half, 16), half)


def _gather_carry(blocked):
    n = len(blocked)

    def over_ici(o_refs, send_sems, recv_sems):
        x, y, c, chips = _place()
        me = _chip_index((x, y))
        copies = []
        for a in range(n):
            mine = o_refs[a].at[me, _half_rows(c, o_refs[a].shape[1] // 2), :]
            for k, chip in enumerate(chips):
                copies.append(pltpu.make_async_remote_copy(
                    src_ref=mine, dst_ref=mine, send_sem=send_sems.at[6 * a + k], recv_sem=recv_sems.at[6 * a + k],
                    device_id=(chip[0], chip[1], c), device_id_type=MESH))
        return copies

    def to_sibling(o_refs, send_sems, recv_sems, sent_by_me):
        x, y, c, chips = _place()
        copies = []
        for a in range(n):
            rows = _half_rows(c if sent_by_me else 1 - c, o_refs[a].shape[1] // 2)
            for k, chip in enumerate(chips):
                landed = o_refs[a].at[_chip_index(chip), rows, :]
                copies.append(pltpu.make_async_remote_copy(
                    src_ref=landed, dst_ref=landed, send_sem=send_sems.at[6 * a + 3 + k],
                    recv_sem=recv_sems.at[6 * a + 3 + k], device_id=(x, y, 1 - c), device_id_type=MESH))
        return copies

    def start(ins, outs, sems):
        for cp in over_ici(outs, *sems):
            cp.start()

    def hand_on(ins, outs, sems):
        for arrived, onward in zip(over_ici(outs, *sems), to_sibling(outs, *sems, True)):
            arrived.wait_recv()
            onward.start()

    def finish(ins, outs, sems):
        for cp in to_sibling(outs, *sems, False):
            cp.wait_recv()
        for cp in over_ici(outs, *sems) + to_sibling(outs, *sems, True):
            cp.wait_send()

    return _Carry(
        ins=list(blocked), outs=[jax.ShapeDtypeStruct(w.shape, w.dtype) for w in blocked],
        aliases={a: a for a in range(n)},
        sems=[pltpu.SemaphoreType.DMA((6 * n,)), pltpu.SemaphoreType.DMA((6 * n,))],
        phases=[("first", start), ("late", hand_on), ("last", finish)])


def _pair_exchange(name, grads):
    n = len(grads)

    def body(*refs):
        g_refs, land_refs = refs[:n], refs[n:2 * n]
        send_sems, recv_sems = refs[2 * n:]
        x, y, c, _ = _place()
        copies = []
        for a in range(n):
            half = g_refs[a].shape[1] // 2
            cp = pltpu.make_async_remote_copy(
                src_ref=g_refs[a].at[:, _half_rows(1 - c, half), :], dst_ref=land_refs[a],
                send_sem=send_sems.at[a], recv_sem=recv_sems.at[a], device_id=(x, y, 1 - c), device_id_type=MESH)
            cp.start()
            copies.append(cp)
        for cp in copies:
            cp.wait()

    return pl.pallas_call(
        body, name=name, in_specs=[ANY] * n, out_specs=[ANY] * n,
        out_shape=[jax.ShapeDtypeStruct((g.shape[0], g.shape[1] // 2, g.shape[2]), g.dtype) for g in grads],
        scratch_shapes=[pltpu.SemaphoreType.DMA((n,)), pltpu.SemaphoreType.DMA((n,))],
    )(*grads)


def _to_owner_carry(parts):
    n = len(parts)

    def sends(p_refs, l_refs, send_sems, recv_sems):
        x, y, c, chips = _place()
        me = _chip_index((x, y))
        return [pltpu.make_async_remote_copy(
            src_ref=p_refs[a].at[_chip_index(chip)], dst_ref=l_refs[a].at[me],
            send_sem=send_sems.at[3 * a + k], recv_sem=recv_sems.at[3 * a + k],
            device_id=(chip[0], chip[1], c), device_id_type=MESH) for a in range(n) for k, chip in enumerate(chips)]

    def start(ins, outs, sems):
        for cp in sends(ins, outs, *sems):
            cp.start()

    def finish(ins, outs, sems):
        x, y, c, chips = _place()
        send_sems, recv_sems = sems
        for a in range(n):
            for k, chip in enumerate(chips):
                slot = outs[a].at[_chip_index(chip)]
                pltpu.make_async_remote_copy(
                    src_ref=slot, dst_ref=slot, send_sem=send_sems.at[3 * a + k], recv_sem=recv_sems.at[3 * a + k],
                    device_id=(chip[0], chip[1], c), device_id_type=MESH).wait_recv()
        for cp in sends(ins, outs, *sems):
            cp.wait_send()

    return _Carry(
        ins=list(parts), outs=[jax.ShapeDtypeStruct(p.shape, p.dtype) for p in parts], aliases={},
        sems=[pltpu.SemaphoreType.DMA((3 * n,)), pltpu.SemaphoreType.DMA((3 * n,))],
        phases=[("first", start), ("last", finish)])


def _swap_halves(halves):
    n = len(halves)

    def body(*refs):
        h_refs, o_refs = refs[:n], refs[n:2 * n]
        send_sems, recv_sems = refs[2 * n:]
        x, y, c, _ = _place()
        copies = []
        for a in range(n):
            cp = pltpu.make_async_remote_copy(
                src_ref=h_refs[a], dst_ref=o_refs[a], send_sem=send_sems.at[a], recv_sem=recv_sems.at[a],
                device_id=(x, y, 1 - c), device_id_type=MESH)
            cp.start()
            copies.append(cp)
        for cp in copies:
            cp.wait()

    return pl.pallas_call(
        body, name="grad_swap_halves", in_specs=[ANY] * n, out_specs=[ANY] * n,
        out_shape=[jax.ShapeDtypeStruct(h.shape, h.dtype) for h in halves],
        scratch_shapes=[pltpu.SemaphoreType.DMA((n,)), pltpu.SemaphoreType.DMA((n,))],
    )(*halves)


def _all_devices(name, block):
    r, cols = block.shape

    def body(b_ref, all_ref, sum_ref, send_sems, recv_sems):
        x, y, c, _ = _place()
        me = 4 * x + 2 * y + c
        all_ref[me] = b_ref[...]
        flips = [(fx, fy, fc) for fx in (0, 1) for fy in (0, 1) for fc in (0, 1)][1:]
        copies = []
        for k, (fx, fy, fc) in enumerate(flips):
            cp = pltpu.make_async_remote_copy(
                src_ref=b_ref, dst_ref=all_ref.at[me], send_sem=send_sems.at[k], recv_sem=recv_sems.at[k],
                device_id=(x ^ fx, y ^ fy, c ^ fc), device_id_type=MESH)
            cp.start()
            copies.append(cp)
        for k, (fx, fy, fc) in enumerate(flips):
            slot = all_ref.at[4 * (x ^ fx) + 2 * (y ^ fy) + (c ^ fc)]
            pltpu.make_async_remote_copy(
                src_ref=slot, dst_ref=slot, send_sem=send_sems.at[k], recv_sem=recv_sems.at[k],
                device_id=(x ^ fx, y ^ fy, c ^ fc), device_id_type=MESH).wait_recv()
        for cp in copies:
            cp.wait_send()
        acc = all_ref[0]
        for d in range(1, N_DEV):
            acc = acc + all_ref[d]
        sum_ref[...] = acc

    vmem = pl.BlockSpec(memory_space=pltpu.VMEM)
    return pl.pallas_call(
        body, name=name, in_specs=[vmem], out_specs=[vmem, vmem],
        out_shape=[jax.ShapeDtypeStruct((N_DEV, r, cols), F32), jax.ShapeDtypeStruct((r, cols), F32)],
        scratch_shapes=[pltpu.SemaphoreType.DMA((N_DEV - 1,)), pltpu.SemaphoreType.DMA((N_DEV - 1,))],
    )(block)


PACK = 1024


def _packed_rows(shape, width):
    size, last = int(np.prod(shape)), shape[-1]
    cols = last if last <= width else width
    assert size % cols == 0
    return size // cols, cols


def _pack(vals, width=PACK):
    rows = []
    for v in vals:
        n_rows, cols = _packed_rows(v.shape, width)
        rows.append(jnp.pad(v.reshape(n_rows, cols).astype(F32), ((0, 0), (0, width - cols))))
    buf = jnp.concatenate(rows, axis=0)
    return jnp.pad(buf, ((0, (-buf.shape[0]) % 8), (0, 0)))


def _unpack(buf, shapes, width=PACK):
    out, r = [], 0
    for shape in shapes:
        n_rows, cols = _packed_rows(shape, width)
        out.append(buf[r:r + n_rows, :cols].reshape(shape))
        r += n_rows
    return out


def _ffn_hidden(name, n, wg, wu, tm=1024, carry=None):
    m, k = n.shape
    nj, fb, _ = wg.shape
    tm = _row_tile(m, tm)

    def core(n_ref, wg_ref, wu_ref, a_ref, b_ref, s_ref):
        nv = n_ref[...]
        a = _dot(nv, wg_ref[...], True)
        b = _dot(nv, wu_ref[...], True)
        a_ref[...] = a.astype(a_ref.dtype)
        b_ref[...] = b.astype(b_ref.dtype)
        s_ref[...] = (a * jax.nn.sigmoid(a) * b).astype(s_ref.dtype)

    w_spec = pl.BlockSpec((None, fb, k), lambda j, i: (j, 0, 0))
    out_spec = pl.BlockSpec((None, tm, fb), lambda j, i: (j, i, 0))
    return _call(name, core, (nj, m // tm), [pl.BlockSpec((tm, k), lambda j, i: (i, 0)), w_spec, w_spec],
                 [out_spec] * 3, [jax.ShapeDtypeStruct((nj, m, fb), BF16)] * 3, [], [n, wg, wu], carry)


def _ffn_d_hidden(name, df, wd, a, b, tm=512):
    m, k = df.shape
    nj, fb, _ = wd.shape
    tm = _row_tile(m, tm)

    def body(df_ref, wd_ref, a_ref, b_ref, da_ref, db_ref):
        dfv = df_ref[...]
        for j in range(nj):
            ds = _dot(dfv, wd_ref[j], True)
            av, bv = a_ref[j].astype(F32), b_ref[j].astype(F32)
            sig = jax.nn.sigmoid(av)
            da_ref[j] = (ds * bv * sig * (1.0 + av * (1.0 - sig))).astype(da_ref.dtype)
            db_ref[j] = (ds * av * sig).astype(db_ref.dtype)

    blk = pl.BlockSpec((nj, tm, fb), lambda i: (0, i, 0))
    return pl.pallas_call(
        body, name=name, grid=(m // tm,),
        in_specs=[pl.BlockSpec((tm, k), lambda i: (i, 0)), _resident(wd), blk, blk],
        out_specs=[blk, blk], out_shape=[jax.ShapeDtypeStruct((nj, m, fb), BF16)] * 2,
        compiler_params=_cparams(1),
    )(df, wd, a, b)


def kernel(x, ffn1_norm_pre, ffn1_w_gate, ffn1_w_up, ffn1_w_down, ffn1_norm_post, mix_norm_pre, w_in, gate_bias, rel_table, w_attn_out, conv_glu_bias, conv_dw_w, conv_dw_b, conv_ln_g, conv_ln_b, conv_w_out, w_out, mix_norm_post, ffn2_norm_pre, ffn2_w_gate, ffn2_w_up, ffn2_w_down, ffn2_norm_post, loss_target, m_ffn1_norm_pre, m_ffn1_w_gate, m_ffn1_w_up, m_ffn1_w_down, m_ffn1_norm_post, m_mix_norm_pre, m_w_in, m_gate_bias, m_rel_table, m_w_attn_out, m_conv_glu_bias, m_conv_dw_w, m_conv_dw_b, m_conv_ln_g, m_conv_ln_b, m_conv_w_out, m_w_out, m_mix_norm_post, m_ffn2_norm_pre, m_ffn2_w_gate, m_ffn2_w_up, m_ffn2_w_down, m_ffn2_norm_post, v_ffn1_norm_pre, v_ffn1_w_gate, v_ffn1_w_up, v_ffn1_w_down, v_ffn1_norm_post, v_mix_norm_pre, v_w_in, v_gate_bias, v_rel_table, v_w_attn_out, v_conv_glu_bias, v_conv_dw_w, v_conv_dw_b, v_conv_ln_g, v_conv_ln_b, v_conv_w_out, v_w_out, v_mix_norm_post, v_ffn2_norm_pre, v_ffn2_w_gate, v_ffn2_w_up, v_ffn2_w_down, v_ffn2_norm_post):
    args = dict(locals())
    names = ['ffn1_norm_pre', 'ffn1_w_gate', 'ffn1_w_up', 'ffn1_w_down', 'ffn1_norm_post', 'mix_norm_pre', 'w_in',
             'gate_bias', 'rel_table', 'w_attn_out', 'conv_glu_bias', 'conv_dw_w', 'conv_dw_b', 'conv_ln_g',
             'conv_ln_b', 'conv_w_out', 'w_out', 'mix_norm_post', 'ffn2_norm_pre', 'ffn2_w_gate', 'ffn2_w_up',
             'ffn2_w_down', 'ffn2_norm_post']
    big = ['ffn1_w_gate', 'ffn1_w_up', 'ffn1_w_down', 'w_in', 'w_attn_out', 'conv_w_out', 'w_out', 'ffn2_w_gate',
           'ffn2_w_up', 'ffn2_w_down']
    small = [n for n in names if n not in big]

    xs, target = x[0], loss_target[0]
    t, d = xs.shape
    cx, cy = lax.axis_index("x"), lax.axis_index("y")
    chip = 2 * cx + cy

    dw_shard = conv_dw_w[0, :, 0, :]
    cshard = dw_shard.shape[1]
    dw_all, _ = _all_devices("gather_dw", _pack([dw_shard], width=cshard))
    dw_full = jnp.concatenate([dw_all[2 * j, :CONV_WIDTH, :cshard] for j in range(N_CHIPS)], axis=1)
    dw_full = jnp.pad(dw_full, ((0, CONV_HALO - CONV_WIDTH), (0, 0)))
    peers = [(1 - cx, cy), (cx, 1 - cy), (1 - cx, 1 - cy)]
    pos = jnp.stack([lax.axis_index("c"), chip] + [_chip_index(p) for p in peers]).astype(jnp.int32)
    transposed = ("ffn1_w_gate", "ffn1_w_up", "ffn2_w_gate", "ffn2_w_up")
    weight_of = lambda n: n[2:] if n[:2] in ("m_", "v_") else n
    shard = lambda n: jnp.transpose(args[n][0]) if weight_of(n) in transposed else args[n][0]
    unshard = lambda n, v: (jnp.transpose(v) if n in transposed else v)[None]
    own = {n: _cast_into("cast_" + n, pos, shard(n)) for n in big}
    gather = lambda *ns: _gather_carry([own[n] for n in ns])
    res_spec = [(d, F32), (d, F32), (d, BF16)]

    n1, (wg1, wu1) = _rms_fwd("ffn1_pre", xs, ffn1_norm_pre, gather("ffn1_w_gate", "ffn1_w_up"))
    (a1, b1, s1), (wd1, win, wao, wco, wout) = _ffn_hidden(
        "ffn1_hidden", n1, wg1, wu1, carry=gather("ffn1_w_down", "w_in", "w_attn_out", "conv_w_out", "w_out"))
    (f1, h1, u), (wg2,) = _mm_kblk(
        "ffn1_down", [(s1, wd1)], trans_w=False, epilogue=_ep_post_res_pre(0.5), rows=[xs],
        vecs=[ffn1_norm_post, mix_norm_pre], row_outs=res_spec, carry=gather("ffn2_w_gate"))
    proj, (wu2, wd2) = _mm_nblk("mix_in", u, win, trans_w=False, out_blocked=False, out_dtype=BF16,
                                carry=gather("ffn2_w_up", "ffn2_w_down"))
    table_pad = jnp.pad(rel_table[0], ((0, 0), (0, REL_PAD - rel_table.shape[2])))
    bias = _bias_expand(table_pad)
    att, lse = _attn_fwd(proj, bias)
    cs, c_glu, z_conv = _conv_fwd(proj, conv_glu_bias, dw_full, conv_dw_b, conv_ln_g, conv_ln_b)
    y_a, y_b, merged = _mix_merge(att, cs, wao, wco, proj, gate_bias)
    (mo, h2, n2), _ = _mm_kblk(
        "mix_out", [(merged, wout)], trans_w=False, epilogue=_ep_post_res_pre(1.0), rows=[h1],
        vecs=[mix_norm_post, ffn2_norm_pre], row_outs=res_spec)
    (a2, b2, s2), _ = _ffn_hidden("ffn2_hidden", n2, wg2, wu2)
    g = {}
    (dy, df2, err2, g["ffn2_norm_post"]), _ = _mm_kblk(
        "ffn2_down", [(s2, wd2)], trans_w=False, epilogue=_ep_loss(0.5, d), rows=[h2, target],
        vecs=[ffn2_norm_post], row_outs=[(d, F32), (d, BF16)], vec_outs=[d, d])
    loss = lax.psum(0.5 * jnp.sum(err2) / d, ("x", "y", "c"))

    parts, landed = {}, {}

    def ffn_bwd(tag, df, n, a, b, s, wg, wu, wd, **epilogue):
        da, db = _ffn_d_hidden(tag + "_d_hidden", df, wd, a, b)
        group = [tag + "_w_down", tag + "_w_gate", tag + "_w_up"]
        local = [_mm_tn(tag + "_g_down", s, "blk", df, "full"), _mm_tn(tag + "_g_gate", da, "blk", n, "full"),
                 _mm_tn(tag + "_g_up", db, "blk", n, "full")]
        return _mm_kblk(tag + "_d_n", [(da, wg), (db, wu)], trans_w=False, carry=pair_sums(tag, group, local),
                        **epilogue), group

    def pair_sums(tag, group, local):
        theirs = _pair_exchange("pair_" + tag, local)
        for n, mine, other in zip(group, local, theirs):
            parts[n] = _add_pair("pair_sum_" + n, pos, mine, other)
        return _to_owner_carry([parts[n] for n in group])

    def keep(group, carried):
        for n, val in zip(group, carried):
            landed[n] = val

    ((dh2, dmo, g["ffn2_norm_pre"], g["mix_norm_post"]), carried), group = ffn_bwd(
        "ffn2", df2, n2, a2, b2, s2, wg2, wu2, wd2, epilogue=_ep_pre_bwd_post(1.0), rows=[h2, dy, mo],
        vecs=[ffn2_norm_pre, mix_norm_post], row_outs=[(d, F32), (d, BF16)], vec_outs=[d, d])
    keep(group, carried)
    g_wout = _mm_tn("mix_g_out", merged, "col", dmo, "full")
    dy_a, dy_b, dgates, datt, dcs, g["gate_bias"] = _mix_d_merge(dmo, wout, y_a, y_b, wao, wco, proj, gate_bias)
    g_wao = _mm_tn("attn_g_out", att, "full", dy_a, "col")
    g_wco = _mm_tn("conv_g_out", cs, "full", dy_b, "col")
    dq, dk, dv, dbias = _attn_bwd(proj, bias, att, lse, datt)
    g["rel_table"] = _bias_fold(dbias)[:, :rel_table.shape[2]]
    dcin, g_dw, g["conv_dw_b"], g["conv_ln_g"], g["conv_ln_b"], g["conv_glu_bias"] = _conv_bwd(
        proj, c_glu, z_conv, dcs, conv_glu_bias, dw_full, conv_ln_g, conv_ln_b)
    dproj = jnp.concatenate([dq, dk, dv, dcin, dgates], axis=1)
    g_win = _mm_tn("mix_g_in", u, "full", dproj, "col")
    group = ["w_out", "w_attn_out", "conv_w_out", "w_in"]
    (dh1, df1, g["mix_norm_pre"], g["ffn1_norm_post"]), carried = _mm_kblk(
        "mix_d_in", [(dproj, win)], trans_w=True, epilogue=_ep_pre_bwd_post(0.5), rows=[h1, dh2, f1],
        vecs=[mix_norm_pre, ffn1_norm_post], row_outs=[(d, F32), (d, BF16)], vec_outs=[d, d],
        carry=pair_sums("mix", group, [g_wout, g_wao, g_wco, g_win]))
    keep(group, carried)
    ((grad_x, g["ffn1_norm_pre"]), carried), group = ffn_bwd(
        "ffn1", df1, n1, a1, b1, s1, wg1, wu1, wd1, epilogue=_ep_pre_bwd_first(), rows=[xs, dh1],
        vecs=[ffn1_norm_pre], row_outs=[(d, F32)], vec_outs=[d])
    keep(group, carried)

    halves = [_add_chips("chip_sum_" + n, pos, parts[n], landed[n]) for n in big]
    other_halves = _swap_halves(halves)

    g["conv_dw_w"] = g_dw[:CONV_WIDTH]
    _, small_sum = _all_devices("sum_small", _pack([g[n] for n in small]))
    for n, val in zip(small, _unpack(small_sum, [g[n].shape for n in small])):
        g[n] = val
    g["conv_dw_w"] = lax.dynamic_slice_in_dim(g["conv_dw_w"], chip * cshard, cshard, axis=1)

    grads, deltas, new_m, new_v = {}, {}, {}, {}
    for n, mine, other in zip(big, halves, other_halves):
        gr, dl, m2, v2 = _adamw_halves("adamw_" + n, pos, shard(n), shard("m_" + n), shard("v_" + n), mine, other)
        grads[n], deltas[n], new_m[n], new_v[n] = unshard(n, gr), unshard(n, dl), unshard(n, m2), unshard(n, v2)
    shapes = [g[n].shape for n in small]
    packed = lambda pre: _pack([args[pre + n].reshape(shp) for n, shp in zip(small, shapes)])
    dl, m2, v2 = _adamw("adamw_small", packed(""), _pack([g[n] for n in small]), packed("m_"), packed("v_"))
    for n, a_, b_, c_ in zip(small, _unpack(dl, shapes), _unpack(m2, shapes), _unpack(v2, shapes)):
        shape = args[n].shape
        grads[n], deltas[n], new_m[n], new_v[n] = (g[n].reshape(shape), a_.reshape(shape), b_.reshape(shape),
                                                   c_.reshape(shape))

    return (loss, grad_x[None], *[grads[n] for n in names], *[deltas[n] for n in names],
            *[new_m[n] for n in names], *[new_v[n] for n in names])
```

```python
import functools

import numpy as np
import jax
import jax.numpy as jnp
from jax import lax
from jax.experimental import pallas as pl
from jax.experimental.pallas import tpu as pltpu

F32 = jnp.float32
BF16 = jnp.bfloat16
MESH = pl.DeviceIdType.MESH
ANY = pl.BlockSpec(memory_space=pl.ANY)

EPS = 1e-6
CHUNK = 64
LEFT_CHUNKS = 8
N_HEADS = 8
HEAD_DIM = 64
D_ATTN = N_HEADS * HEAD_DIM
D_CONV = 512
CONV_WIDTH = 31
REL_CLIP = 128
N_CHIPS = 4
N_DEV = 8
Q_BLOCK = 4 * CHUNK
K_PAD = LEFT_CHUNKS * CHUNK
K_WIN = K_PAD + Q_BLOCK
REL_EXT = 1024
REL_PAD = 384
CONV_HALO = 32
CONV_TILE = 256
COL = 512
NEG = -1e30

ADAM_LR = 0.001
ADAM_B1 = 0.9
ADAM_B2 = 0.999
ADAM_EPS = 1e-08
ADAM_WD = 0.01
ADAM_STEP = 10

VMEM_LIMIT_BYTES = 56 * 1024 * 1024


def _cparams(n_grid):
    return pltpu.CompilerParams(dimension_semantics=("arbitrary",) * n_grid, vmem_limit_bytes=VMEM_LIMIT_BYTES)


def _row_tile(rows, want):
    if rows <= want:
        return rows
    for t in range(want - want % 16, 0, -16):
        if rows % t == 0:
            return t
    raise ValueError((rows, want))


def _dot(a, w, trans_w):
    dims = (((1,), (1,)), ((), ())) if trans_w else (((1,), (0,)), ((), ()))
    return lax.dot_general(a, w, dims, preferred_element_type=F32)


class _Carry:
    LATE_STEPS = 2

    def __init__(self, ins, outs, aliases, sems, phases):
        self.ins, self.outs, self.aliases, self.sems, self.phases = ins, outs, aliases, sems, phases


def _call(name, core, grid, in_specs, out_specs, out_shape, scratch, args, carry=None):
    n_in, n_out, n_scr = len(in_specs), len(out_specs), len(scratch)
    if carry is None:
        out = pl.pallas_call(core, name=name, grid=grid, in_specs=in_specs, out_specs=out_specs, out_shape=out_shape,
                             scratch_shapes=scratch, compiler_params=_cparams(len(grid)))(*args)
        return list(out), []
    c_in, c_out = len(carry.ins), len(carry.outs)
    total = int(np.prod(grid))
    late = max(total - 1 - _Carry.LATE_STEPS, 0)

    def body(*refs):
        ins, refs = refs[:n_in], refs[n_in:]
        c_ins, refs = refs[:c_in], refs[c_in:]
        outs, refs = refs[:n_out], refs[n_out:]
        c_outs, refs = refs[:c_out], refs[c_out:]
        scr, c_sems = refs[:n_scr], refs[n_scr:]
        step = pl.program_id(0)
        for axis in range(1, len(grid)):
            step = step * grid[axis] + pl.program_id(axis)

        def run(when, at):
            for w, fn in carry.phases:
                if w == when:
                    pl.when(step == at)(functools.partial(fn, c_ins, c_outs, c_sems))

        run("first", 0)
        core(*ins, *outs, *scr)
        run("late", late)
        run("last", total - 1)

    out = pl.pallas_call(
        body, name=name, grid=grid, in_specs=list(in_specs) + [ANY] * c_in, out_specs=list(out_specs) + [ANY] * c_out,
        out_shape=list(out_shape) + list(carry.outs), scratch_shapes=list(scratch) + list(carry.sems),
        input_output_aliases={n_in + a: n_out + b for a, b in carry.aliases.items()},
        compiler_params=_cparams(len(grid)),
    )(*args, *carry.ins)
    return list(out[:n_out]), list(out[n_out:])


def _mm_nblk(name, a, w, *, trans_w, out_blocked, out_dtype, tm=1024, carry=None):
    m, k = a.shape
    nj = w.shape[0]
    nb = w.shape[1] if trans_w else w.shape[2]
    tm = _row_tile(m, tm)

    def core(a_ref, w_ref, o_ref):
        o_ref[...] = _dot(a_ref[...], w_ref[...], trans_w).astype(o_ref.dtype)

    if out_blocked:
        out_shape, out_spec = (nj, m, nb), pl.BlockSpec((None, tm, nb), lambda j, i: (j, i, 0))
    else:
        out_shape, out_spec = (m, nj * nb), pl.BlockSpec((tm, nb), lambda j, i: (i, j))
    out, carried = _call(
        name, core, (nj, m // tm),
        [pl.BlockSpec((tm, k), lambda j, i: (i, 0)), pl.BlockSpec((None,) + w.shape[1:], lambda j, i: (j, 0, 0))],
        [out_spec], [jax.ShapeDtypeStruct(out_shape, out_dtype)], [], [a, w], carry)
    return out[0] if carry is None else (out[0], carried)


def _mm_kblk(name, pairs, *, trans_w, out_dtype=F32, tm=512, sub=256, epilogue=None, rows=(), vecs=(), row_outs=None,
             vec_outs=(), carry=None):
    w0 = pairs[0][1]
    n = w0.shape[1] if trans_w else w0.shape[2]
    blocks = [(w.shape[0], w.shape[2] if trans_w else w.shape[1]) for _, w in pairs]
    m = pairs[0][0].shape[-2]
    tm = _row_tile(m, tm)
    ts = _row_tile(tm, sub)
    n_pairs, n_rows, n_vecs = len(pairs), len(rows), len(vecs)
    if epilogue is None:
        epilogue, row_outs = (lambda acc, r, v: ([acc], [])), [(n, out_dtype)]
    n_ro, n_vo = len(row_outs), len(vec_outs)

    def core(*refs):
        pair_refs, refs = refs[:2 * n_pairs], refs[2 * n_pairs:]
        row_refs, refs = refs[:n_rows], refs[n_rows:]
        vec_refs, refs = refs[:n_vecs], refs[n_vecs:]
        ro_refs, vo_refs = refs[:n_ro], refs[n_ro:]
        if n_vo:
            @pl.when(pl.program_id(0) == 0)
            def _():
                for ref in vo_refs:
                    ref[...] = jnp.zeros_like(ref)

        vec_vals = [v[...] for v in vec_refs]
        sums = None
        for r0 in range(0, tm, ts):
            sub_rows = slice(r0, r0 + ts)
            acc = None
            for p in range(n_pairs):
                a_ref, w_ref = pair_refs[2 * p], pair_refs[2 * p + 1]
                nj, kb = blocks[p]
                for j in range(nj):
                    a_blk = a_ref[j, sub_rows, :] if len(a_ref.shape) == 3 else a_ref[sub_rows, j * kb:(j + 1) * kb]
                    part = _dot(a_blk, w_ref[j], trans_w)
                    acc = part if acc is None else acc + part
            ro, vo = epilogue(acc, [r[sub_rows, :] for r in row_refs], vec_vals)
            for ref, val in zip(ro_refs, ro):
                ref[sub_rows, :] = val.astype(ref.dtype)
            sums = vo if sums is None else [s + v for s, v in zip(sums, vo)]
        for ref, val in zip(vo_refs, sums or []):
            ref[...] += val

    in_specs, args = [], []
    for (a, w), (nj, kb) in zip(pairs, blocks):
        if a.ndim == 3:
            in_specs.append(pl.BlockSpec((nj, tm, kb), lambda i: (0, i, 0)))
        else:
            in_specs.append(pl.BlockSpec((tm, nj * kb), lambda i: (i, 0)))
        in_specs.append(pl.BlockSpec(w.shape, lambda i: (0, 0, 0), pipeline_mode=pl.Buffered(1)))
        args += [a, w]
    in_specs += [pl.BlockSpec((tm, r.shape[1]), lambda i: (i, 0)) for r in rows]
    in_specs += [pl.BlockSpec(v.shape, lambda i: (0, 0)) for v in vecs]
    out_specs = [pl.BlockSpec((tm, cols), lambda i: (i, 0)) for cols, _ in row_outs]
    out_specs += [pl.BlockSpec((1, cols), lambda i: (0, 0)) for cols in vec_outs]
    out_shape = [jax.ShapeDtypeStruct((m, cols), dt) for cols, dt in row_outs]
    out_shape += [jax.ShapeDtypeStruct((1, cols), F32) for cols in vec_outs]
    return _call(name, core, (m // tm,), in_specs, out_specs, out_shape, [], args + list(rows) + list(vecs), carry)


def _mm_tn(name, a, a_mode, b, b_mode, *, out_dtype=BF16, tt=2048):
    nj = N_CHIPS
    t = a.shape[-2]
    tt = _row_tile(t, tt)

    def spec(x, mode):
        if mode == "full":
            return x.shape[1], pl.BlockSpec((tt, x.shape[1]), lambda j, s: (s, 0))
        if mode == "col":
            cb = x.shape[1] // nj
            return cb, pl.BlockSpec((tt, cb), lambda j, s: (s, j))
        return x.shape[2], pl.BlockSpec((None, tt, x.shape[2]), lambda j, s: (j, s, 0))

    ca, a_spec = spec(a, a_mode)
    cb, b_spec = spec(b, b_mode)
    n_steps = t // tt

    def body(a_ref, b_ref, o_ref, acc_ref):
        s = pl.program_id(1)

        @pl.when(s == 0)
        def _():
            acc_ref[...] = jnp.zeros_like(acc_ref)

        acc_ref[...] += lax.dot_general(a_ref[...], b_ref[...], (((0,), (0,)), ((), ())),
                                        preferred_element_type=F32)

        @pl.when(s == n_steps - 1)
        def _():
            o_ref[...] = acc_ref[...].astype(o_ref.dtype)

    return pl.pallas_call(
        body, name=name, grid=(nj, n_steps), in_specs=[a_spec, b_spec],
        out_specs=pl.BlockSpec((None, ca, cb), lambda j, s: (j, 0, 0)),
        out_shape=jax.ShapeDtypeStruct((nj, ca, cb), out_dtype),
        scratch_shapes=[pltpu.VMEM((ca, cb), F32)], compiler_params=_cparams(2),
    )(a, b)


def _mm_tn_wide(name, a, b, cb, *, out_dtype=BF16, tt=1024):
    t, ca = a.shape
    nb = b.shape[1] // cb
    tt = _row_tile(t, tt)
    n_steps = t // tt

    def body(a_ref, b_ref, o_ref, acc_ref):
        s = pl.program_id(0)

        @pl.when(s == 0)
        def _():
            acc_ref[...] = jnp.zeros_like(acc_ref)

        av = a_ref[...]
        for j in range(nb):
            acc_ref[j] += lax.dot_general(av, b_ref[:, j * cb:(j + 1) * cb], (((0,), (0,)), ((), ())),
                                          preferred_element_type=F32)

        @pl.when(s == n_steps - 1)
        def _():
            o_ref[...] = acc_ref[...].astype(o_ref.dtype)

    return pl.pallas_call(
        body, name=name, grid=(n_steps,),
        in_specs=[pl.BlockSpec((tt, ca), lambda s: (s, 0)), pl.BlockSpec((tt, nb * cb), lambda s: (s, 0))],
        out_specs=pl.BlockSpec((nb, ca, cb), lambda s: (0, 0, 0)),
        out_shape=jax.ShapeDtypeStruct((nb, ca, cb), out_dtype),
        scratch_shapes=[pltpu.VMEM((nb, ca, cb), F32)], compiler_params=_cparams(1),
    )(a, b)


def _rowwise(name, fn, rows, vecs, row_outs, vec_outs, *, tm=256, carry=None):
    nrows = rows[0][0].shape[0]
    tm = _row_tile(nrows, tm)
    n_r, n_v, n_ro, n_vo = len(rows), len(vecs), len(row_outs), len(vec_outs)

    def body(*refs):
        r_vals = [r[...] for r in refs[:n_r]]
        v_vals = [r[...] for r in refs[n_r:n_r + n_v]]
        ro_refs = refs[n_r + n_v:n_r + n_v + n_ro]
        vo_refs = refs[n_r + n_v + n_ro:]
        ro, vo = fn(r_vals, v_vals)
        for ref, val in zip(ro_refs, ro):
            ref[...] = val.astype(ref.dtype)
        if n_vo:
            @pl.when(pl.program_id(0) == 0)
            def _():
                for ref in vo_refs:
                    ref[...] = jnp.zeros_like(ref)

            for ref, val in zip(vo_refs, vo):
                ref[...] += val

    in_specs = [pl.BlockSpec((tm, cols), functools.partial(lambda i, cb: (i, cb), cb=cb)) for _, cols, cb in rows]
    in_specs += [pl.BlockSpec(v.shape, functools.partial(lambda i, nd: (0,) * nd, nd=v.ndim)) for v in vecs]
    out_specs = [pl.BlockSpec((tm, cols), lambda i: (i, 0)) for cols, _ in row_outs]
    out_specs += [pl.BlockSpec((1, cols), lambda i: (0, 0)) for cols in vec_outs]
    out_shape = [jax.ShapeDtypeStruct((nrows, cols), dt) for cols, dt in row_outs]
    out_shape += [jax.ShapeDtypeStruct((1, cols), F32) for cols in vec_outs]
    out, carried = _call(name, body, (nrows // tm,), in_specs, out_specs, out_shape, [],
                         [r[0] for r in rows] + list(vecs), carry)
    return out if carry is None else (out, carried)


def _whole(x):
    return (x, x.shape[1], 0)


def _colsum(x):
    return jnp.sum(x, axis=0, keepdims=True)


def _rstd(x):
    return lax.rsqrt(jnp.mean(x * x, axis=-1, keepdims=True) + EPS)


def _rms_bwd(dn, x, g):
    r = _rstd(x)
    c = dn * g
    dx = r * c - x * (r * r * r) * jnp.mean(c * x, axis=-1, keepdims=True)
    return dx, _colsum(dn * x * r)


def _rms_fwd(name, x, g, carry):
    def fn(r, v):
        (xv,), (gv,) = r, v
        return [xv * _rstd(xv) * gv], []

    (n,), carried = _rowwise(name, fn, [_whole(x)], [g], [(x.shape[1], BF16)], [], carry=carry)
    return n, carried


def _ep_post_res_pre(scale):
    def epilogue(acc, rows, vecs):
        (resid,), (g_post, g_next) = rows, vecs
        h = resid + scale * (acc * _rstd(acc) * g_post)
        return [acc, h, h * _rstd(h) * g_next], []

    return epilogue


def _post_bwd(dh, f, g_post, scale):
    return _rms_bwd(scale * dh, f, g_post)


def _ep_loss(scale, d):
    def epilogue(acc, rows, vecs):
        (resid, target), (g_post,) = rows, vecs
        err = resid + scale * (acc * _rstd(acc) * g_post) - target
        dy = err * (1.0 / d)
        df, dg_post = _post_bwd(dy, acc, g_post, scale)
        return [dy, df], [_colsum(err * err), dg_post]

    return epilogue


def _ep_pre_bwd_post(scale_prev):
    def epilogue(acc, rows, vecs):
        (h, dh_up, f_prev), (g_pre, g_post_prev) = rows, vecs
        dx, dg_pre = _rms_bwd(acc, h, g_pre)
        dh = dh_up + dx
        df, dg_post = _post_bwd(dh, f_prev, g_post_prev, scale_prev)
        return [dh, df], [dg_pre, dg_post]

    return epilogue


def _ep_pre_bwd_first():
    def epilogue(acc, rows, vecs):
        (x, dh_up), (g_pre,) = rows, vecs
        dx, dg_pre = _rms_bwd(acc, x, g_pre)
        return [dh_up + dx], [dg_pre]

    return epilogue


def _gate_specs(d, tm):
    first = (3 * D_ATTN + 2 * D_CONV) // COL
    return [pl.BlockSpec((tm, COL), functools.partial(lambda i, cb: (i, cb), cb=first + p)) for p in range(2 * d // COL)]


def _gate(piece_refs, bias_ref, c0, width):
    p, off = divmod(c0, COL)
    return jax.nn.sigmoid(piece_refs[p][:, off:off + width].astype(F32) + bias_ref[:, c0:c0 + width])


def _resident(w):
    return pl.BlockSpec(w.shape, functools.partial(lambda i, nd: (0,) * nd, nd=w.ndim), pipeline_mode=pl.Buffered(1))


def _mix_merge(att, cs, wao, wco, proj, gate_bias, tm=512):
    t = att.shape[0]
    nj, _, nb = wao.shape
    d = nj * nb
    tm = _row_tile(t, tm)
    gate_specs = _gate_specs(d, tm)
    n_p = len(gate_specs)

    def body(att_ref, cs_ref, wao_ref, wco_ref, *rest):
        pieces, (gb_ref, ya_ref, yb_ref, m_ref) = rest[:n_p], rest[n_p:]
        av, cv = att_ref[...], cs_ref[...]
        for j in range(nj):
            cols = slice(j * nb, (j + 1) * nb)
            ya = _dot(av, wao_ref[j], False)
            yb = _dot(cv, wco_ref[j], False)
            merged = _gate(pieces, gb_ref, j * nb, nb) * ya + _gate(pieces, gb_ref, d + j * nb, nb) * yb
            ya_ref[:, cols] = ya.astype(ya_ref.dtype)
            yb_ref[:, cols] = yb.astype(yb_ref.dtype)
            m_ref[:, cols] = merged.astype(m_ref.dtype)

    row = lambda x: pl.BlockSpec((tm, x.shape[1]), lambda i: (i, 0))
    out_spec = pl.BlockSpec((tm, d), lambda i: (i, 0))
    return pl.pallas_call(
        body, name="mix_merge", grid=(t // tm,),
        in_specs=[row(att), row(cs), _resident(wao), _resident(wco)] + gate_specs + [_resident(gate_bias)],
        out_specs=[out_spec] * 3, out_shape=[jax.ShapeDtypeStruct((t, d), BF16)] * 3, compiler_params=_cparams(1),
    )(att, cs, wao, wco, *([proj] * n_p), gate_bias)


def _mix_d_merge(dmo, wout, y_a, y_b, wao, wco, proj, gate_bias, tm=512):
    t, d = dmo.shape
    nj, _, nb = wao.shape
    ka, kc = wao.shape[1], wco.shape[1]
    tm = _row_tile(t, tm)
    gate_specs = _gate_specs(d, tm)
    n_p = len(gate_specs)

    def body(dmo_ref, wout_ref, ya_ref, yb_ref, wao_ref, wco_ref, *rest):
        pieces, (gb_ref, dya_ref, dyb_ref, dg_ref, datt_ref, dcs_ref, dgb_ref) = rest[:n_p], rest[n_p:]

        @pl.when(pl.program_id(0) == 0)
        def _():
            dgb_ref[...] = jnp.zeros_like(dgb_ref)

        dmo_v = dmo_ref[...]
        datt = dcs = None
        for j in range(nj):
            cols, cols_b = slice(j * nb, (j + 1) * nb), slice(d + j * nb, d + (j + 1) * nb)
            dm = _dot(dmo_v, wout_ref[j], True)
            ga, gb = _gate(pieces, gb_ref, j * nb, nb), _gate(pieces, gb_ref, d + j * nb, nb)
            dya, dyb = (dm * ga).astype(BF16), (dm * gb).astype(BF16)
            dga = dm * ya_ref[:, cols].astype(F32) * ga * (1.0 - ga)
            dgb = dm * yb_ref[:, cols].astype(F32) * gb * (1.0 - gb)
            dya_ref[:, cols], dyb_ref[:, cols] = dya, dyb
            dg_ref[:, cols], dg_ref[:, cols_b] = dga.astype(dg_ref.dtype), dgb.astype(dg_ref.dtype)
            dgb_ref[:, cols] += _colsum(dga)
            dgb_ref[:, cols_b] += _colsum(dgb)
            pa, pc = _dot(dya, wao_ref[j], True), _dot(dyb, wco_ref[j], True)
            datt, dcs = (pa, pc) if datt is None else (datt + pa, dcs + pc)
        datt_ref[...] = datt.astype(datt_ref.dtype)
        dcs_ref[...] = dcs.astype(dcs_ref.dtype)

    row = lambda cols: pl.BlockSpec((tm, cols), lambda i: (i, 0))
    return pl.pallas_call(
        body, name="mix_d_merge", grid=(t // tm,),
        in_specs=[row(d), _resident(wout), row(d), row(d), _resident(wao), _resident(wco)] + gate_specs
        + [_resident(gate_bias)],
        out_specs=[row(d), row(d), row(2 * d), row(ka), row(kc), pl.BlockSpec((1, 2 * d), lambda i: (0, 0))],
        out_shape=[jax.ShapeDtypeStruct((t, d), BF16), jax.ShapeDtypeStruct((t, d), BF16),
                   jax.ShapeDtypeStruct((t, 2 * d), BF16), jax.ShapeDtypeStruct((t, ka), BF16),
                   jax.ShapeDtypeStruct((t, kc), F32), jax.ShapeDtypeStruct((1, 2 * d), F32)],
        compiler_params=_cparams(1),
    )(dmo, wout, y_a, y_b, wao, wco, *([proj] * n_p), gate_bias)


def _adamw_math(wv, gv, mv, vv):
    m2 = ADAM_B1 * mv + (1.0 - ADAM_B1) * gv
    v2 = ADAM_B2 * vv + (1.0 - ADAM_B2) * (gv * gv)
    m_hat = m2 / (1.0 - ADAM_B1 ** ADAM_STEP)
    v_hat = v2 / (1.0 - ADAM_B2 ** ADAM_STEP)
    delta = -ADAM_LR * (m_hat / (jnp.sqrt(v_hat) + ADAM_EPS) + ADAM_WD * wv)
    return delta, m2, v2


def _adamw(name, w, g, m, v):
    def fn(r, _):
        return list(_adamw_math(*r)), []

    c = w.shape[1]
    return _rowwise(name, fn, [_whole(w), _whole(g), _whole(m), _whole(v)], [], [(c, F32)] * 3, [], tm=256)


POS_C, POS_CHIP, POS_PEER = 0, 1, 2


def _placed_call(body, name, pos, grid, in_specs, out_specs, out_shape, args):
    return pl.pallas_call(
        body, name=name, out_shape=out_shape, compiler_params=_cparams(len(grid)),
        grid_spec=pltpu.PrefetchScalarGridSpec(num_scalar_prefetch=1, grid=grid, in_specs=in_specs,
                                               out_specs=out_specs),
    )(pos, *args)


def _cast_into(name, pos, w):
    r, cols = w.shape
    tm = _row_tile(r, 256)

    def body(pos_ref, w_ref, o_ref):
        o_ref[...] = w_ref[...].astype(o_ref.dtype)

    return _placed_call(
        body, name, pos, (r // tm,), [pl.BlockSpec((tm, cols), lambda i, pos: (i, 0))],
        pl.BlockSpec((None, tm, cols), lambda i, pos: (pos[POS_CHIP], i, 0)),
        jax.ShapeDtypeStruct((N_CHIPS, r, cols), BF16), [w])


def _add_pair(name, pos, grad, landed):
    nj, half, cols = landed.shape
    tm = _row_tile(half, 256)
    nb = half // tm

    def body(pos_ref, g_ref, l_ref, o_ref):
        o_ref[...] = (g_ref[...].astype(F32) + l_ref[...].astype(F32)).astype(o_ref.dtype)

    spec = pl.BlockSpec((None, tm, cols), lambda j, i, pos: (j, i, 0))
    return _placed_call(
        body, name, pos, (nj, nb),
        [pl.BlockSpec((None, tm, cols), lambda j, i, pos: (j, pos[POS_C] * nb + i, 0)), spec], spec,
        jax.ShapeDtypeStruct(landed.shape, BF16), [grad, landed])


def _add_chips(name, pos, part, landed):
    _, half, cols = landed.shape
    tm = _row_tile(half, 256)

    def body(pos_ref, p_ref, l0_ref, l1_ref, l2_ref, o_ref):
        acc = p_ref[...].astype(F32)
        for ref in (l0_ref, l1_ref, l2_ref):
            acc = acc + ref[...].astype(F32)
        o_ref[...] = acc

    slot = lambda at: pl.BlockSpec((None, tm, cols), functools.partial(lambda i, pos, at: (pos[at], i, 0), at=at))
    return _placed_call(
        body, name, pos, (half // tm,), [slot(POS_CHIP)] + [slot(POS_PEER + k) for k in range(3)],
        pl.BlockSpec((tm, cols), lambda i, pos: (i, 0)), jax.ShapeDtypeStruct((half, cols), F32),
        [part, landed, landed, landed])


def _adamw_halves(name, pos, w, m, v, own, landed):
    r, cols = w.shape
    half = own.shape[0]
    tm = _row_tile(half, 256)
    nb = half // tm

    def body(pos_ref, w_ref, m_ref, v_ref, own_ref, land_ref, g_out, d_out, m_out, v_out):
        mine = pl.program_id(0) == pos_ref[POS_C]
        g = jnp.where(mine, own_ref[...], land_ref[...])
        delta, m2, v2 = _adamw_math(w_ref[...], g, m_ref[...], v_ref[...])
        g_out[...] = g
        d_out[...] = delta
        m_out[...] = m2
        v_out[...] = v2

    full = pl.BlockSpec((tm, cols), lambda h, i, pos: (h * nb + i, 0))
    part = pl.BlockSpec((tm, cols), lambda h, i, pos: (i, 0))
    return _placed_call(
        body, name, pos, (2, nb), [full, full, full, part, part], [full] * 4,
        [jax.ShapeDtypeStruct((r, cols), F32)] * 4, [w, m, v, own, landed])


N_START = K_PAD // Q_BLOCK


def _rel_onehot(n_q):
    e = np.arange(REL_EXT)
    dist = K_PAD - (e - (n_q - 1))
    idx = np.clip(dist, -REL_CLIP, REL_CLIP) + REL_CLIP
    return (np.arange(REL_PAD)[:, None] == idx[None, :]).astype(np.float32)


def _skew(x, left):
    row = lax.broadcasted_iota(jnp.int32, x.shape, 0)
    for bit in range(x.shape[0].bit_length() - 1):
        amount = 1 << bit
        rolled = pltpu.roll(x, REL_EXT - amount if left else amount, 1)
        x = jnp.where((row >> bit) & 1 == 1, rolled, x)
    return x


def _bias_expand(table_pad):
    onehot = jnp.asarray(_rel_onehot(Q_BLOCK))

    def body(t_ref, oh_ref, o_ref):
        ext = jnp.dot(t_ref[...], oh_ref[...], precision=lax.Precision.HIGHEST, preferred_element_type=F32)
        qc = lax.broadcasted_iota(jnp.int32, (Q_BLOCK, K_WIN), 0) // CHUNK
        kpos = lax.broadcasted_iota(jnp.int32, (Q_BLOCK, K_WIN), 1)
        band = (kpos // CHUNK >= qc) & (kpos // CHUNK <= qc + LEFT_CHUNKS)
        for h in range(N_HEADS):
            rows = jnp.broadcast_to(ext[h:h + 1, :], (Q_BLOCK, REL_EXT))
            rolled = _skew(pltpu.roll(rows, REL_EXT - (Q_BLOCK - 1), 1), left=False)[:, :K_WIN]
            for v in range(N_START + 1):
                o_ref[v, h] = jnp.where(band & (kpos + v * Q_BLOCK >= K_PAD), rolled, NEG)

    return pl.pallas_call(
        body, name="bias_expand", out_shape=jax.ShapeDtypeStruct((N_START + 1, N_HEADS, Q_BLOCK, K_WIN), F32),
        compiler_params=pltpu.CompilerParams(vmem_limit_bytes=VMEM_LIMIT_BYTES),
    )(table_pad, onehot)


def _bias_fold(dbias):
    onehot_t = jnp.asarray(_rel_onehot(CHUNK).T)

    def body(d_ref, oh_ref, o_ref, ext_ref):
        for h in range(N_HEADS):
            x = jnp.concatenate([d_ref[h], jnp.zeros((CHUNK, REL_EXT - K_WIN), F32)], axis=1)
            rolled = _skew(pltpu.roll(x, CHUNK - 1, 1), left=True)
            ext_ref[h:h + 1, :] = jnp.sum(rolled, axis=0, keepdims=True)
        o_ref[...] = jnp.dot(ext_ref[...], oh_ref[...], precision=lax.Precision.HIGHEST,
                             preferred_element_type=F32)

    return pl.pallas_call(
        body, name="bias_fold", out_shape=jax.ShapeDtypeStruct((N_HEADS, REL_PAD), F32),
        scratch_shapes=[pltpu.VMEM((N_HEADS, REL_EXT), F32)],
        compiler_params=pltpu.CompilerParams(vmem_limit_bytes=VMEM_LIMIT_BYTES),
    )(dbias, onehot_t)


def _head_lanes():
    lane = lax.broadcasted_iota(jnp.int32, (1, 2 * HEAD_DIM), 1)
    return [lane < HEAD_DIM, lane >= HEAD_DIM]


def _only(mask, x, scale=None):
    x = jnp.where(mask, x, jnp.zeros_like(x))
    return x if scale is None else x * scale


def _contract_lanes(a, b):
    return lax.dot_general(a, b, (((1,), (1,)), ((), ())), preferred_element_type=F32)


def _contract_rows(a, b):
    return lax.dot_general(a, b, (((0,), (0,)), ((), ())), preferred_element_type=F32)


PAIR = 2 * HEAD_DIM
N_PAIRS = D_ATTN // PAIR


def _attn_specs(pairs):
    width = pairs * PAIR
    per = D_ATTN // width
    row_spec = pl.BlockSpec((Q_BLOCK, width), lambda g, i: (i, g))
    kv_specs = [pl.BlockSpec((Q_BLOCK, width),
                             functools.partial(lambda g, i, kk, c0: (jnp.maximum(i + kk - N_START, 0), c0 + g),
                                               kk=kk, c0=c0))
                for c0 in (per, 2 * per) for kk in range(K_WIN // Q_BLOCK)]
    bias_spec = pl.BlockSpec((None, 2 * pairs, Q_BLOCK, K_WIN), lambda g, i: (jnp.minimum(i, N_START), g, 0, 0))
    return row_spec, kv_specs, bias_spec


def _attn_fwd(proj, bias, pairs=N_PAIRS):
    t = proj.shape[0]
    n_win = K_WIN // Q_BLOCK

    def body(q_ref, *refs):
        k_refs, v_refs = refs[:n_win], refs[n_win:2 * n_win]
        b_ref, o_ref, lse_ref = refs[2 * n_win:]
        for pp in range(pairs):
            cols = slice(pp * PAIR, (pp + 1) * PAIR)
            k = jnp.concatenate([r[:, cols] for r in k_refs], axis=0)
            v = jnp.concatenate([r[:, cols] for r in v_refs], axis=0)
            q = q_ref[:, cols]
            o = lse = None
            for hh, lanes in enumerate(_head_lanes()):
                s = _contract_lanes(_only(lanes, q, HEAD_DIM ** -0.5), k) + b_ref[2 * pp + hh]
                m = jnp.max(s, axis=1, keepdims=True)
                p = jnp.exp(s - m)
                l = jnp.sum(p, axis=1, keepdims=True)
                oh = jnp.dot(p.astype(BF16), v, preferred_element_type=F32) / l
                lse_h = jnp.broadcast_to(m + jnp.log(l), oh.shape)
                o, lse = (oh, lse_h) if o is None else (jnp.where(lanes, oh, o), jnp.where(lanes, lse_h, lse))
            o_ref[:, cols] = o.astype(o_ref.dtype)
            lse_ref[:, cols] = lse

    row_spec, kv_specs, bias_spec = _attn_specs(pairs)
    return pl.pallas_call(
        body, name="attn_fwd", grid=(N_PAIRS // pairs, t // Q_BLOCK),
        in_specs=[row_spec] + kv_specs + [bias_spec], out_specs=[row_spec, row_spec],
        out_shape=[jax.ShapeDtypeStruct((t, D_ATTN), BF16), jax.ShapeDtypeStruct((t, D_ATTN), F32)],
        compiler_params=_cparams(2),
    )(*([proj] * (1 + 2 * n_win)), bias)


def _attn_bwd(proj, bias, att, lse, datt, pairs=2):
    t = proj.shape[0]
    n_win = K_WIN // Q_BLOCK
    n_blocks = t // Q_BLOCK

    def body(q_ref, *refs):
        k_refs, v_refs = refs[:n_win], refs[n_win:2 * n_win]
        b_ref, o_ref, lse_ref, do_ref, dq_ref, dk_ref, dv_ref, db_ref, dk_acc, dv_acc = refs[2 * n_win:]
        i = pl.program_id(1)

        @pl.when(i == 0)
        def _():
            dk_acc[...] = jnp.zeros_like(dk_acc)
            dv_acc[...] = jnp.zeros_like(dv_acc)
            db_ref[...] = jnp.zeros_like(db_ref)

        rows = pl.ds(pl.multiple_of(i * Q_BLOCK, Q_BLOCK), K_WIN)
        scale = HEAD_DIM ** -0.5
        for pp in range(pairs):
            cols = slice(pp * PAIR, (pp + 1) * PAIR)
            k = jnp.concatenate([r[:, cols] for r in k_refs], axis=0)
            v = jnp.concatenate([r[:, cols] for r in v_refs], axis=0)
            q, do, o = q_ref[:, cols], do_ref[:, cols], o_ref[:, cols].astype(F32)
            dq = dk = dv = None
            for hh, lanes in enumerate(_head_lanes()):
                qh, doh = _only(lanes, q, scale), _only(lanes, do)
                s = _contract_lanes(qh, k) + b_ref[2 * pp + hh]
                lse_col = pp * PAIR + hh * HEAD_DIM
                p = jnp.exp(s - lse_ref[:, lse_col:lse_col + 1])
                delta = jnp.sum(doh.astype(F32) * o, axis=1, keepdims=True)
                ds = p * (_contract_lanes(doh, v) - delta)
                folded = ds[:CHUNK]
                for c in range(1, Q_BLOCK // CHUNK):
                    folded = folded + pltpu.roll(ds[c * CHUNK:(c + 1) * CHUNK], K_WIN - c * CHUNK, 1)
                db_ref[2 * pp + hh] += folded
                dsb = ds.astype(BF16)
                dqh = jnp.dot(dsb, k, preferred_element_type=F32)
                dq = dqh if dq is None else jnp.where(lanes, dqh, dq)
                dkh, dvh = _contract_rows(dsb, qh), _contract_rows(p.astype(BF16), doh)
                dk, dv = (dkh, dvh) if dk is None else (dk + dkh, dv + dvh)
            dq_ref[:, cols] = (dq * scale).astype(dq_ref.dtype)
            dk_acc[rows, cols] += dk
            dv_acc[rows, cols] += dv

        @pl.when(i == n_blocks - 1)
        def _():
            dk_ref[...] = dk_acc[K_PAD:, :].astype(dk_ref.dtype)
            dv_ref[...] = dv_acc[K_PAD:, :].astype(dv_ref.dtype)

    width = pairs * PAIR
    row_spec, kv_specs, bias_spec = _attn_specs(pairs)
    full_spec = pl.BlockSpec((t, width), lambda g, i: (0, g))
    return pl.pallas_call(
        body, name="attn_bwd", grid=(N_PAIRS // pairs, n_blocks),
        in_specs=[row_spec] + kv_specs + [bias_spec, row_spec, row_spec, row_spec],
        out_specs=[row_spec, full_spec, full_spec,
                   pl.BlockSpec((2 * pairs, CHUNK, K_WIN), lambda g, i: (g, 0, 0))],
        out_shape=[jax.ShapeDtypeStruct((t, D_ATTN), BF16)] * 3 + [jax.ShapeDtypeStruct((N_HEADS, CHUNK, K_WIN), F32)],
        scratch_shapes=[pltpu.VMEM((t + K_PAD, width), F32)] * 2, compiler_params=_cparams(2),
    )(*([proj] * (1 + 2 * n_win)), bias, att, lse, datt)


CONV_LEAD = CONV_HALO - (CONV_WIDTH - 1)
CONV_LANES = 128
CONV_ROWS = 64


def _conv_specs(t):
    tt = _row_tile(t, CONV_TILE)
    per = tt // CONV_HALO
    n_halo = t // CONV_HALO
    tile = lambda cb: pl.BlockSpec((tt, COL), functools.partial(lambda i, cb: (i, cb), cb=cb))
    prev = lambda cb: pl.BlockSpec((CONV_HALO, COL),
                                   functools.partial(lambda i, cb: (jnp.maximum(i * per - 1, 0), cb), cb=cb))
    nxt = lambda cb: pl.BlockSpec((CONV_HALO, COL),
                                  functools.partial(lambda i, cb: (jnp.minimum((i + 1) * per, n_halo - 1), cb), cb=cb))
    vec = lambda shape: pl.BlockSpec(shape, lambda i: (0, 0))
    return tt, tile, prev, nxt, vec


def _glu(ca, cg, bias):
    return (ca.astype(F32) + bias[:, :D_CONV]) * jax.nn.sigmoid(cg.astype(F32) + bias[:, D_CONV:])


SUBLANES = 8


def _shift_copies(ext_ref):
    n = ext_ref.shape[1] - SUBLANES
    for s in range(1, SUBLANES):
        ext_ref[s, 0:n, :] = ext_ref[0, s:s + n, :]


def _tap_tiles(ext_ref, first_row, r0, lanes):
    n_g = CONV_ROWS // SUBLANES
    for s in range(SUBLANES):
        taps = [w for w in range(CONV_WIDTH) if first_row(w) % SUBLANES == s]
        if not taps:
            continue
        lo = min(first_row(w) for w in taps) - s
        n_tiles = (max(first_row(w) for w in taps) - s - lo) // SUBLANES + n_g
        tiles = [ext_ref[s, r0 + lo + SUBLANES * b:r0 + lo + SUBLANES * (b + 1), lanes] for b in range(n_tiles)]
        for w in taps:
            k = (first_row(w) - s - lo) // SUBLANES
            yield w, tiles[k:k + n_g]


def _taps(ext_ref, tt, first_row, w_ref, out_ref):
    n_g = CONV_ROWS // SUBLANES
    for l0 in range(0, D_CONV, CONV_LANES):
        lanes = slice(l0, l0 + CONV_LANES)
        for r0 in range(0, tt, CONV_ROWS):
            acc = [jnp.zeros((SUBLANES, CONV_LANES), F32)] * n_g
            for w, tiles in _tap_tiles(ext_ref, first_row, r0, lanes):
                weight = jnp.broadcast_to(w_ref[w:w + 1, lanes], (SUBLANES, CONV_LANES))
                acc = [a + t * weight for a, t in zip(acc, tiles)]
            for g in range(n_g):
                out_ref[r0 + SUBLANES * g:r0 + SUBLANES * (g + 1), lanes] = acc[g]


def _tap_sums(ext_ref, tt, first_row, x_ref, out_ref):
    n_g = CONV_ROWS // SUBLANES
    for l0 in range(0, D_CONV, CONV_LANES):
        lanes = slice(l0, l0 + CONV_LANES)
        acc = [jnp.zeros((SUBLANES, CONV_LANES), F32)] * CONV_WIDTH
        for r0 in range(0, tt, CONV_ROWS):
            x = [x_ref[0, r0 + SUBLANES * g:r0 + SUBLANES * (g + 1), lanes] for g in range(n_g)]
            for w, tiles in _tap_tiles(ext_ref, first_row, r0, lanes):
                part = tiles[0] * x[0]
                for g in range(1, n_g):
                    part = part + tiles[g] * x[g]
                acc[w] = acc[w] + part
        for w in range(CONV_WIDTH):
            out_ref[w:w + 1, lanes] += jnp.sum(acc[w], axis=0, keepdims=True)


def _conv_fwd(proj, glu_bias, dw, dw_b, ln_g, ln_b):
    t = proj.shape[0]
    tt, tile, prev, nxt, vec = _conv_specs(t)
    ca_blk, cg_blk = 3 * D_ATTN // COL, 3 * D_ATTN // COL + 1

    def body(ca_ref, cg_ref, pa_ref, pg_ref, gb_ref, dw_ref, dwb_ref, g_ref, b_ref, cs_ref, c_ref, z_ref, ext_ref):
        i = pl.program_id(0)
        bias = gb_ref[...]
        c = _glu(ca_ref[...], cg_ref[...], bias)
        halo = _glu(pa_ref[...], pg_ref[...], bias)
        ext_ref[0, 0:CONV_HALO, :] = jnp.where(i == 0, 0.0, halo)
        ext_ref[0, CONV_HALO:, :] = c
        _shift_copies(ext_ref)
        c_ref[...] = c
        _taps(ext_ref, tt, lambda w: CONV_LEAD + w, dw_ref, z_ref)
        z = z_ref[...] + dwb_ref[...]
        z_ref[...] = z
        mu = jnp.mean(z, axis=-1, keepdims=True)
        zc = z - mu
        y = zc * lax.rsqrt(jnp.mean(zc * zc, axis=-1, keepdims=True) + EPS) * g_ref[...] + b_ref[...]
        cs_ref[...] = (y * jax.nn.sigmoid(y)).astype(cs_ref.dtype)

    out_spec = pl.BlockSpec((tt, D_CONV), lambda i: (i, 0))
    return pl.pallas_call(
        body, name="conv_fwd", grid=(t // tt,),
        in_specs=[tile(ca_blk), tile(cg_blk), prev(ca_blk), prev(cg_blk), vec(glu_bias.shape), vec(dw.shape),
                  vec(dw_b.shape), vec(ln_g.shape), vec(ln_b.shape)],
        out_specs=[out_spec] * 3,
        out_shape=[jax.ShapeDtypeStruct((t, D_CONV), BF16), jax.ShapeDtypeStruct((t, D_CONV), F32),
                   jax.ShapeDtypeStruct((t, D_CONV), F32)],
        scratch_shapes=[pltpu.VMEM((SUBLANES, tt + CONV_HALO, D_CONV), F32)], compiler_params=_cparams(1),
    )(proj, proj, proj, proj, glu_bias, dw, dw_b, ln_g, ln_b)


def _conv_bwd(proj, c, z, dcs, glu_bias, dw, ln_g, ln_b):
    t = proj.shape[0]
    tt, tile, prev, nxt, vec = _conv_specs(t)
    n_tiles = t // tt
    ca_blk, cg_blk = 3 * D_ATTN // COL, 3 * D_ATTN // COL + 1

    def ln_bwd(zv, dcsv, g, b):
        mu = jnp.mean(zv, axis=-1, keepdims=True)
        zc = zv - mu
        rstd = lax.rsqrt(jnp.mean(zc * zc, axis=-1, keepdims=True) + EPS)
        zhat = zc * rstd
        y = zhat * g + b
        sig = jax.nn.sigmoid(y)
        dy = dcsv * sig * (1.0 + y * (1.0 - sig))
        dzh = dy * g
        dz = rstd * (dzh - jnp.mean(dzh, axis=-1, keepdims=True) - zhat * jnp.mean(dzh * zhat, axis=-1, keepdims=True))
        return dz, dy, zhat

    def body(ca_ref, cg_ref, c_ref, cprev_ref, z_ref, znext_ref, dcs_ref, dcsnext_ref, gb_ref, dw_ref, g_ref, b_ref,
             dcin_ref, ddw_ref, ddwb_ref, dg_ref, db_ref, dgb_ref, cext_ref, dzext_ref, dc_ref):
        i = pl.program_id(0)

        @pl.when(i == 0)
        def _():
            for ref in (ddw_ref, ddwb_ref, dg_ref, db_ref, dgb_ref):
                ref[...] = jnp.zeros_like(ref)

        g, b = g_ref[...], b_ref[...]
        dz, dy, zhat = ln_bwd(z_ref[...], dcs_ref[...], g, b)
        dz_next, _, _ = ln_bwd(znext_ref[...], dcsnext_ref[...], g, b)
        dg_ref[...] += _colsum(dy * zhat)
        db_ref[...] += _colsum(dy)
        ddwb_ref[...] += _colsum(dz)
        dzext_ref[0, 0:tt, :] = dz
        dzext_ref[0, tt:, :] = jnp.where(i == n_tiles - 1, 0.0, dz_next)
        _shift_copies(dzext_ref)
        cext_ref[0, 0:CONV_HALO, :] = jnp.where(i == 0, 0.0, cprev_ref[...])
        cext_ref[0, CONV_HALO:, :] = c_ref[...]
        _shift_copies(cext_ref)
        _taps(dzext_ref, tt, lambda w: CONV_WIDTH - 1 - w, dw_ref, dc_ref)
        _tap_sums(cext_ref, tt, lambda w: CONV_LEAD + w, dzext_ref, ddw_ref)
        bias = gb_ref[...]
        a_in = ca_ref[...].astype(F32) + bias[:, :D_CONV]
        sg = jax.nn.sigmoid(cg_ref[...].astype(F32) + bias[:, D_CONV:])
        dc = dc_ref[...]
        dcin = jnp.concatenate([dc * sg, dc * a_in * sg * (1.0 - sg)], axis=1)
        dcin_ref[...] = dcin.astype(dcin_ref.dtype)
        dgb_ref[...] += _colsum(dcin)

    row = lambda: pl.BlockSpec((tt, D_CONV), lambda i: (i, 0))
    per = tt // CONV_HALO
    n_halo = t // CONV_HALO
    prev_row = pl.BlockSpec((CONV_HALO, D_CONV), lambda i: (jnp.maximum(i * per - 1, 0), 0))
    next_row = lambda: pl.BlockSpec((CONV_HALO, D_CONV), lambda i: (jnp.minimum((i + 1) * per, n_halo - 1), 0))
    acc = lambda shape: pl.BlockSpec(shape, lambda i: (0, 0))
    return pl.pallas_call(
        body, name="conv_bwd", grid=(n_tiles,),
        in_specs=[tile(ca_blk), tile(cg_blk), row(), prev_row, row(), next_row(), row(), next_row(),
                  vec(glu_bias.shape), vec(dw.shape), vec(ln_g.shape), vec(ln_b.shape)],
        out_specs=[pl.BlockSpec((tt, 2 * D_CONV), lambda i: (i, 0)), acc(dw.shape), acc((1, D_CONV)),
                   acc((1, D_CONV)), acc((1, D_CONV)), acc((1, 2 * D_CONV))],
        out_shape=[jax.ShapeDtypeStruct((t, 2 * D_CONV), BF16), jax.ShapeDtypeStruct(dw.shape, F32),
                   jax.ShapeDtypeStruct((1, D_CONV), F32), jax.ShapeDtypeStruct((1, D_CONV), F32),
                   jax.ShapeDtypeStruct((1, D_CONV), F32), jax.ShapeDtypeStruct((1, 2 * D_CONV), F32)],
        scratch_shapes=[pltpu.VMEM((SUBLANES, tt + CONV_HALO, D_CONV), F32),
                        pltpu.VMEM((SUBLANES, tt + CONV_HALO, D_CONV), F32), pltpu.VMEM((tt, D_CONV), F32)],
        compiler_params=_cparams(1),
    )(proj, proj, c, c, z, z, dcs, dcs, glu_bias, dw, ln_g, ln_b)


def _place():
    x, y, c = lax.axis_index("x"), lax.axis_index("y"), lax.axis_index("c")
    chips = [(1 - x, y), (x, 1 - y), (1 - x, 1 - y)]
    return x, y, c, chips


def _chip_index(chip):
    return 2 * chip[0] + chip[1]


def _half_rows(c, half):
    return pl.ds(pl.multiple_of(c * half, 16), half)


def _gather_carry(blocked):
    n = len(blocked)

    def over_ici(o_refs, send_sems, recv_sems):
        x, y, c, chips = _place()
        me = _chip_index((x, y))
        copies = []
        for a in range(n):
            mine = o_refs[a].at[me, _half_rows(c, o_refs[a].shape[1] // 2), :]
            for k, chip in enumerate(chips):
                copies.append(pltpu.make_async_remote_copy(
                    src_ref=mine, dst_ref=mine, send_sem=send_sems.at[6 * a + k], recv_sem=recv_sems.at[6 * a + k],
                    device_id=(chip[0], chip[1], c), device_id_type=MESH))
        return copies

    def to_sibling(o_refs, send_sems, recv_sems, sent_by_me):
        x, y, c, chips = _place()
        copies = []
        for a in range(n):
            rows = _half_rows(c if sent_by_me else 1 - c, o_refs[a].shape[1] // 2)
            for k, chip in enumerate(chips):
                landed = o_refs[a].at[_chip_index(chip), rows, :]
                copies.append(pltpu.make_async_remote_copy(
                    src_ref=landed, dst_ref=landed, send_sem=send_sems.at[6 * a + 3 + k],
                    recv_sem=recv_sems.at[6 * a + 3 + k], device_id=(x, y, 1 - c), device_id_type=MESH))
        return copies

    def start(ins, outs, sems):
        for cp in over_ici(outs, *sems):
            cp.start()

    def hand_on(ins, outs, sems):
        for arrived, onward in zip(over_ici(outs, *sems), to_sibling(outs, *sems, True)):
            arrived.wait_recv()
            onward.start()

    def finish(ins, outs, sems):
        for cp in to_sibling(outs, *sems, False):
            cp.wait_recv()
        for cp in over_ici(outs, *sems) + to_sibling(outs, *sems, True):
            cp.wait_send()

    return _Carry(
        ins=list(blocked), outs=[jax.ShapeDtypeStruct(w.shape, w.dtype) for w in blocked],
        aliases={a: a for a in range(n)},
        sems=[pltpu.SemaphoreType.DMA((6 * n,)), pltpu.SemaphoreType.DMA((6 * n,))],
        phases=[("first", start), ("late", hand_on), ("last", finish)])


def _pair_exchange(name, grads):
    n = len(grads)

    def body(*refs):
        g_refs, land_refs = refs[:n], refs[n:2 * n]
        send_sems, recv_sems = refs[2 * n:]
        x, y, c, _ = _place()
        copies = []
        for a in range(n):
            half = g_refs[a].shape[1] // 2
            cp = pltpu.make_async_remote_copy(
                src_ref=g_refs[a].at[:, _half_rows(1 - c, half), :], dst_ref=land_refs[a],
                send_sem=send_sems.at[a], recv_sem=recv_sems.at[a], device_id=(x, y, 1 - c), device_id_type=MESH)
            cp.start()
            copies.append(cp)
        for cp in copies:
            cp.wait()

    return pl.pallas_call(
        body, name=name, in_specs=[ANY] * n, out_specs=[ANY] * n,
        out_shape=[jax.ShapeDtypeStruct((g.shape[0], g.shape[1] // 2, g.shape[2]), g.dtype) for g in grads],
        scratch_shapes=[pltpu.SemaphoreType.DMA((n,)), pltpu.SemaphoreType.DMA((n,))],
    )(*grads)


def _to_owner_carry(parts):
    n = len(parts)

    def sends(p_refs, l_refs, send_sems, recv_sems):
        x, y, c, chips = _place()
        me = _chip_index((x, y))
        return [pltpu.make_async_remote_copy(
            src_ref=p_refs[a].at[_chip_index(chip)], dst_ref=l_refs[a].at[me],
            send_sem=send_sems.at[3 * a + k], recv_sem=recv_sems.at[3 * a + k],
            device_id=(chip[0], chip[1], c), device_id_type=MESH) for a in range(n) for k, chip in enumerate(chips)]

    def start(ins, outs, sems):
        for cp in sends(ins, outs, *sems):
            cp.start()

    def finish(ins, outs, sems):
        x, y, c, chips = _place()
        send_sems, recv_sems = sems
        for a in range(n):
            for k, chip in enumerate(chips):
                slot = outs[a].at[_chip_index(chip)]
                pltpu.make_async_remote_copy(
                    src_ref=slot, dst_ref=slot, send_sem=send_sems.at[3 * a + k], recv_sem=recv_sems.at[3 * a + k],
                    device_id=(chip[0], chip[1], c), device_id_type=MESH).wait_recv()
        for cp in sends(ins, outs, *sems):
            cp.wait_send()

    return _Carry(
        ins=list(parts), outs=[jax.ShapeDtypeStruct(p.shape, p.dtype) for p in parts], aliases={},
        sems=[pltpu.SemaphoreType.DMA((3 * n,)), pltpu.SemaphoreType.DMA((3 * n,))],
        phases=[("first", start), ("last", finish)])


def _swap_halves(halves):
    n = len(halves)

    def body(*refs):
        h_refs, o_refs = refs[:n], refs[n:2 * n]
        send_sems, recv_sems = refs[2 * n:]
        x, y, c, _ = _place()
        copies = []
        for a in range(n):
            cp = pltpu.make_async_remote_copy(
                src_ref=h_refs[a], dst_ref=o_refs[a], send_sem=send_sems.at[a], recv_sem=recv_sems.at[a],
                device_id=(x, y, 1 - c), device_id_type=MESH)
            cp.start()
            copies.append(cp)
        for cp in copies:
            cp.wait()

    return pl.pallas_call(
        body, name="grad_swap_halves", in_specs=[ANY] * n, out_specs=[ANY] * n,
        out_shape=[jax.ShapeDtypeStruct(h.shape, h.dtype) for h in halves],
        scratch_shapes=[pltpu.SemaphoreType.DMA((n,)), pltpu.SemaphoreType.DMA((n,))],
    )(*halves)


def _all_devices(name, block):
    r, cols = block.shape

    def body(b_ref, all_ref, sum_ref, send_sems, recv_sems):
        x, y, c, _ = _place()
        me = 4 * x + 2 * y + c
        all_ref[me] = b_ref[...]
        flips = [(fx, fy, fc) for fx in (0, 1) for fy in (0, 1) for fc in (0, 1)][1:]
        copies = []
        for k, (fx, fy, fc) in enumerate(flips):
            cp = pltpu.make_async_remote_copy(
                src_ref=b_ref, dst_ref=all_ref.at[me], send_sem=send_sems.at[k], recv_sem=recv_sems.at[k],
                device_id=(x ^ fx, y ^ fy, c ^ fc), device_id_type=MESH)
            cp.start()
            copies.append(cp)
        for k, (fx, fy, fc) in enumerate(flips):
            slot = all_ref.at[4 * (x ^ fx) + 2 * (y ^ fy) + (c ^ fc)]
            pltpu.make_async_remote_copy(
                src_ref=slot, dst_ref=slot, send_sem=send_sems.at[k], recv_sem=recv_sems.at[k],
                device_id=(x ^ fx, y ^ fy, c ^ fc), device_id_type=MESH).wait_recv()
        for cp in copies:
            cp.wait_send()
        acc = all_ref[0]
        for d in range(1, N_DEV):
            acc = acc + all_ref[d]
        sum_ref[...] = acc

    vmem = pl.BlockSpec(memory_space=pltpu.VMEM)
    return pl.pallas_call(
        body, name=name, in_specs=[vmem], out_specs=[vmem, vmem],
        out_shape=[jax.ShapeDtypeStruct((N_DEV, r, cols), F32), jax.ShapeDtypeStruct((r, cols), F32)],
        scratch_shapes=[pltpu.SemaphoreType.DMA((N_DEV - 1,)), pltpu.SemaphoreType.DMA((N_DEV - 1,))],
    )(block)


PACK = 1024


def _packed_rows(shape, width):
    size, last = int(np.prod(shape)), shape[-1]
    cols = last if last <= width else width
    assert size % cols == 0
    return size // cols, cols


def _pack(vals, width=PACK):
    rows = []
    for v in vals:
        n_rows, cols = _packed_rows(v.shape, width)
        rows.append(jnp.pad(v.reshape(n_rows, cols).astype(F32), ((0, 0), (0, width - cols))))
    buf = jnp.concatenate(rows, axis=0)
    return jnp.pad(buf, ((0, (-buf.shape[0]) % 8), (0, 0)))


def _unpack(buf, shapes, width=PACK):
    out, r = [], 0
    for shape in shapes:
        n_rows, cols = _packed_rows(shape, width)
        out.append(buf[r:r + n_rows, :cols].reshape(shape))
        r += n_rows
    return out


def _ffn_hidden(name, n, wg, wu, tm=1024, carry=None):
    m, k = n.shape
    nj, fb, _ = wg.shape
    tm = _row_tile(m, tm)

    def core(n_ref, wg_ref, wu_ref, a_ref, b_ref, s_ref):
        nv = n_ref[...]
        a = _dot(nv, wg_ref[...], True)
        b = _dot(nv, wu_ref[...], True)
        a_ref[...] = a.astype(a_ref.dtype)
        b_ref[...] = b.astype(b_ref.dtype)
        s_ref[...] = (a * jax.nn.sigmoid(a) * b).astype(s_ref.dtype)

    w_spec = pl.BlockSpec((None, fb, k), lambda j, i: (j, 0, 0))
    out_spec = pl.BlockSpec((None, tm, fb), lambda j, i: (j, i, 0))
    return _call(name, core, (nj, m // tm), [pl.BlockSpec((tm, k), lambda j, i: (i, 0)), w_spec, w_spec],
                 [out_spec] * 3, [jax.ShapeDtypeStruct((nj, m, fb), BF16)] * 3, [], [n, wg, wu], carry)


def _ffn_d_hidden(name, df, wd, a, b, tm=512):
    m, k = df.shape
    nj, fb, _ = wd.shape
    tm = _row_tile(m, tm)

    def body(df_ref, wd_ref, a_ref, b_ref, da_ref, db_ref):
        dfv = df_ref[...]
        for j in range(nj):
            ds = _dot(dfv, wd_ref[j], True)
            av, bv = a_ref[j].astype(F32), b_ref[j].astype(F32)
            sig = jax.nn.sigmoid(av)
            da_ref[j] = (ds * bv * sig * (1.0 + av * (1.0 - sig))).astype(da_ref.dtype)
            db_ref[j] = (ds * av * sig).astype(db_ref.dtype)

    blk = pl.BlockSpec((nj, tm, fb), lambda i: (0, i, 0))
    return pl.pallas_call(
        body, name=name, grid=(m // tm,),
        in_specs=[pl.BlockSpec((tm, k), lambda i: (i, 0)), _resident(wd), blk, blk],
        out_specs=[blk, blk], out_shape=[jax.ShapeDtypeStruct((nj, m, fb), BF16)] * 2,
        compiler_params=_cparams(1),
    )(df, wd, a, b)


def kernel(x, ffn1_norm_pre, ffn1_w_gate, ffn1_w_up, ffn1_w_down, ffn1_norm_post, mix_norm_pre, w_in, gate_bias, rel_table, w_attn_out, conv_glu_bias, conv_dw_w, conv_dw_b, conv_ln_g, conv_ln_b, conv_w_out, w_out, mix_norm_post, ffn2_norm_pre, ffn2_w_gate, ffn2_w_up, ffn2_w_down, ffn2_norm_post, loss_target, m_ffn1_norm_pre, m_ffn1_w_gate, m_ffn1_w_up, m_ffn1_w_down, m_ffn1_norm_post, m_mix_norm_pre, m_w_in, m_gate_bias, m_rel_table, m_w_attn_out, m_conv_glu_bias, m_conv_dw_w, m_conv_dw_b, m_conv_ln_g, m_conv_ln_b, m_conv_w_out, m_w_out, m_mix_norm_post, m_ffn2_norm_pre, m_ffn2_w_gate, m_ffn2_w_up, m_ffn2_w_down, m_ffn2_norm_post, v_ffn1_norm_pre, v_ffn1_w_gate, v_ffn1_w_up, v_ffn1_w_down, v_ffn1_norm_post, v_mix_norm_pre, v_w_in, v_gate_bias, v_rel_table, v_w_attn_out, v_conv_glu_bias, v_conv_dw_w, v_conv_dw_b, v_conv_ln_g, v_conv_ln_b, v_conv_w_out, v_w_out, v_mix_norm_post, v_ffn2_norm_pre, v_ffn2_w_gate, v_ffn2_w_up, v_ffn2_w_down, v_ffn2_norm_post):
    args = dict(locals())
    names = ['ffn1_norm_pre', 'ffn1_w_gate', 'ffn1_w_up', 'ffn1_w_down', 'ffn1_norm_post', 'mix_norm_pre', 'w_in',
             'gate_bias', 'rel_table', 'w_attn_out', 'conv_glu_bias', 'conv_dw_w', 'conv_dw_b', 'conv_ln_g',
             'conv_ln_b', 'conv_w_out', 'w_out', 'mix_norm_post', 'ffn2_norm_pre', 'ffn2_w_gate', 'ffn2_w_up',
             'ffn2_w_down', 'ffn2_norm_post']
    big = ['ffn1_w_gate', 'ffn1_w_up', 'ffn1_w_down', 'w_in', 'w_attn_out', 'conv_w_out', 'w_out', 'ffn2_w_gate',
           'ffn2_w_up', 'ffn2_w_down']
    small = [n for n in names if n not in big]

    xs, target = x[0], loss_target[0]
    t, d = xs.shape
    cx, cy = lax.axis_index("x"), lax.axis_index("y")
    chip = 2 * cx + cy

    dw_shard = conv_dw_w[0, :, 0, :]
    cshard = dw_shard.shape[1]
    dw_all, _ = _all_devices("gather_dw", _pack([dw_shard], width=cshard))
    dw_full = jnp.concatenate([dw_all[2 * j, :CONV_WIDTH, :cshard] for j in range(N_CHIPS)], axis=1)
    dw_full = jnp.pad(dw_full, ((0, CONV_HALO - CONV_WIDTH), (0, 0)))
    peers = [(1 - cx, cy), (cx, 1 - cy), (1 - cx, 1 - cy)]
    pos = jnp.stack([lax.axis_index("c"), chip] + [_chip_index(p) for p in peers]).astype(jnp.int32)
    transposed = ("ffn1_w_gate", "ffn1_w_up", "ffn2_w_gate", "ffn2_w_up")
    weight_of = lambda n: n[2:] if n[:2] in ("m_", "v_") else n
    shard = lambda n: jnp.transpose(args[n][0]) if weight_of(n) in transposed else args[n][0]
    unshard = lambda n, v: (jnp.transpose(v) if n in transposed else v)[None]
    own = {n: _cast_into("cast_" + n, pos, shard(n)) for n in big}
    gather = lambda *ns: _gather_carry([own[n] for n in ns])
    res_spec = [(d, F32), (d, F32), (d, BF16)]

    n1, (wg1, wu1) = _rms_fwd("ffn1_pre", xs, ffn1_norm_pre, gather("ffn1_w_gate", "ffn1_w_up"))
    (a1, b1, s1), (wd1, win, wao, wco, wout) = _ffn_hidden(
        "ffn1_hidden", n1, wg1, wu1, carry=gather("ffn1_w_down", "w_in", "w_attn_out", "conv_w_out", "w_out"))
    (f1, h1, u), (wg2,) = _mm_kblk(
        "ffn1_down", [(s1, wd1)], trans_w=False, epilogue=_ep_post_res_pre(0.5), rows=[xs],
        vecs=[ffn1_norm_post, mix_norm_pre], row_outs=res_spec, carry=gather("ffn2_w_gate"))
    proj, (wu2, wd2) = _mm_nblk("mix_in", u, win, trans_w=False, out_blocked=False, out_dtype=BF16,
                                carry=gather("ffn2_w_up", "ffn2_w_down"))
    table_pad = jnp.pad(rel_table[0], ((0, 0), (0, REL_PAD - rel_table.shape[2])))
    bias = _bias_expand(table_pad)
    att, lse = _attn_fwd(proj, bias)
    cs, c_glu, z_conv = _conv_fwd(proj, conv_glu_bias, dw_full, conv_dw_b, conv_ln_g, conv_ln_b)
    y_a, y_b, merged = _mix_merge(att, cs, wao, wco, proj, gate_bias)
    (mo, h2, n2), _ = _mm_kblk(
        "mix_out", [(merged, wout)], trans_w=False, epilogue=_ep_post_res_pre(1.0), rows=[h1],
        vecs=[mix_norm_post, ffn2_norm_pre], row_outs=res_spec)
    (a2, b2, s2), _ = _ffn_hidden("ffn2_hidden", n2, wg2, wu2)
    g = {}
    (dy, df2, err2, g["ffn2_norm_post"]), _ = _mm_kblk(
        "ffn2_down", [(s2, wd2)], trans_w=False, epilogue=_ep_loss(0.5, d), rows=[h2, target],
        vecs=[ffn2_norm_post], row_outs=[(d, F32), (d, BF16)], vec_outs=[d, d])
    loss = lax.psum(0.5 * jnp.sum(err2) / d, ("x", "y", "c"))

    parts, landed = {}, {}

    def ffn_bwd(tag, df, n, a, b, s, wg, wu, wd, **epilogue):
        da, db = _ffn_d_hidden(tag + "_d_hidden", df, wd, a, b)
        group = [tag + "_w_down", tag + "_w_gate", tag + "_w_up"]
        local = [_mm_tn(tag + "_g_down", s, "blk", df, "full"), _mm_tn(tag + "_g_gate", da, "blk", n, "full"),
                 _mm_tn(tag + "_g_up", db, "blk", n, "full")]
        return _mm_kblk(tag + "_d_n", [(da, wg), (db, wu)], trans_w=False, carry=pair_sums(tag, group, local),
                        **epilogue), group

    def pair_sums(tag, group, local):
        theirs = _pair_exchange("pair_" + tag, local)
        for n, mine, other in zip(group, local, theirs):
            parts[n] = _add_pair("pair_sum_" + n, pos, mine, other)
        return _to_owner_carry([parts[n] for n in group])

    def keep(group, carried):
        for n, val in zip(group, carried):
            landed[n] = val

    ((dh2, dmo, g["ffn2_norm_pre"], g["mix_norm_post"]), carried), group = ffn_bwd(
        "ffn2", df2, n2, a2, b2, s2, wg2, wu2, wd2, epilogue=_ep_pre_bwd_post(1.0), rows=[h2, dy, mo],
        vecs=[ffn2_norm_pre, mix_norm_post], row_outs=[(d, F32), (d, BF16)], vec_outs=[d, d])
    keep(group, carried)
    g_wout = _mm_tn("mix_g_out", merged, "col", dmo, "full")
    dy_a, dy_b, dgates, datt, dcs, g["gate_bias"] = _mix_d_merge(dmo, wout, y_a, y_b, wao, wco, proj, gate_bias)
    g_wao = _mm_tn("attn_g_out", att, "full", dy_a, "col")
    g_wco = _mm_tn("conv_g_out", cs, "full", dy_b, "col")
    dq, dk, dv, dbias = _attn_bwd(proj, bias, att, lse, datt)
    g["rel_table"] = _bias_fold(dbias)[:, :rel_table.shape[2]]
    dcin, g_dw, g["conv_dw_b"], g["conv_ln_g"], g["conv_ln_b"], g["conv_glu_bias"] = _conv_bwd(
        proj, c_glu, z_conv, dcs, conv_glu_bias, dw_full, conv_ln_g, conv_ln_b)
    pieces = [("q", dq), ("k", dk), ("v", dv), ("conv", dcin), ("gates", dgates)]
    n_in = win.shape[0] * win.shape[2]
    win_cols = jnp.transpose(jnp.transpose(win, (1, 0, 2)).reshape(d, n_in // COL, COL), (1, 0, 2))
    g_cols = jnp.concatenate([_mm_tn_wide("mix_g_in_" + tag, u, piece, COL) for tag, piece in pieces], axis=0)
    g_win = jnp.transpose(jnp.transpose(g_cols, (1, 0, 2)).reshape(d, win.shape[0], win.shape[2]), (1, 0, 2))
    bounds = np.cumsum([0] + [piece.shape[1] // COL for _, piece in pieces])
    group = ["w_out", "w_attn_out", "conv_w_out", "w_in"]
    (dh1, df1, g["mix_norm_pre"], g["ffn1_norm_post"]), carried = _mm_kblk(
        "mix_d_in", [(piece, win_cols[lo:hi]) for (_, piece), lo, hi in zip(pieces, bounds[:-1], bounds[1:])],
        trans_w=True, epilogue=_ep_pre_bwd_post(0.5), rows=[h1, dh2, f1],
        vecs=[mix_norm_pre, ffn1_norm_post], row_outs=[(d, F32), (d, BF16)], vec_outs=[d, d],
        carry=pair_sums("mix", group, [g_wout, g_wao, g_wco, g_win]))
    keep(group, carried)
    ((grad_x, g["ffn1_norm_pre"]), carried), group = ffn_bwd(
        "ffn1", df1, n1, a1, b1, s1, wg1, wu1, wd1, epilogue=_ep_pre_bwd_first(), rows=[xs, dh1],
        vecs=[ffn1_norm_pre], row_outs=[(d, F32)], vec_outs=[d])
    keep(group, carried)

    halves = [_add_chips("chip_sum_" + n, pos, parts[n], landed[n]) for n in big]
    other_halves = _swap_halves(halves)

    g["conv_dw_w"] = g_dw[:CONV_WIDTH]
    _, small_sum = _all_devices("sum_small", _pack([g[n] for n in small]))
    for n, val in zip(small, _unpack(small_sum, [g[n].shape for n in small])):
        g[n] = val
    g["conv_dw_w"] = lax.dynamic_slice_in_dim(g["conv_dw_w"], chip * cshard, cshard, axis=1)

    grads, deltas, new_m, new_v = {}, {}, {}, {}
    for n, mine, other in zip(big, halves, other_halves):
        gr, dl, m2, v2 = _adamw_halves("adamw_" + n, pos, shard(n), shard("m_" + n), shard("v_" + n), mine, other)
        grads[n], deltas[n], new_m[n], new_v[n] = unshard(n, gr), unshard(n, dl), unshard(n, m2), unshard(n, v2)
    shapes = [g[n].shape for n in small]
    packed = lambda pre: _pack([args[pre + n].reshape(shp) for n, shp in zip(small, shapes)])
    dl, m2, v2 = _adamw("adamw_small", packed(""), _pack([g[n] for n in small]), packed("m_"), packed("v_"))
    for n, a_, b_, c_ in zip(small, _unpack(dl, shapes), _unpack(m2, shapes), _unpack(v2, shapes)):
        shape = args[n].shape
        grads[n], deltas[n], new_m[n], new_v[n] = (g[n].reshape(shape), a_.reshape(shape), b_.reshape(shape),
                                                   c_.reshape(shape))

    return (loss, grad_x[None], *[grads[n] for n in names], *[deltas[n] for n in names],
            *[new_m[n] for n in names], *[new_v[n] for n in names])
```

```python
import functools

import numpy as np
import jax
import jax.numpy as jnp
from jax import lax
from jax.experimental import pallas as pl
from jax.experimental.pallas import tpu as pltpu

F32 = jnp.float32
BF16 = jnp.bfloat16
MESH = pl.DeviceIdType.MESH
ANY = pl.BlockSpec(memory_space=pl.ANY)

EPS = 1e-6
CHUNK = 64
LEFT_CHUNKS = 8
N_HEADS = 8
HEAD_DIM = 64
D_ATTN = N_HEADS * HEAD_DIM
D_CONV = 512
CONV_WIDTH = 31
REL_CLIP = 128
N_CHIPS = 4
N_DEV = 8
Q_BLOCK = 4 * CHUNK
K_PAD = LEFT_CHUNKS * CHUNK
K_WIN = K_PAD + Q_BLOCK
REL_EXT = 1024
REL_PAD = 384
CONV_HALO = 32
CONV_TILE = 256
COL = 512
NEG = -1e30

ADAM_LR = 0.001
ADAM_B1 = 0.9
ADAM_B2 = 0.999
ADAM_EPS = 1e-08
ADAM_WD = 0.01
ADAM_STEP = 10

VMEM_LIMIT_BYTES = 56 * 1024 * 1024


def _cparams(n_grid):
    return pltpu.CompilerParams(dimension_semantics=("arbitrary",) * n_grid, vmem_limit_bytes=VMEM_LIMIT_BYTES)


def _row_tile(rows, want):
    if rows <= want:
        return rows
    for t in range(want - want % 16, 0, -16):
        if rows % t == 0:
            return t
    raise ValueError((rows, want))


def _dot(a, w, trans_w):
    dims = (((1,), (1,)), ((), ())) if trans_w else (((1,), (0,)), ((), ()))
    return lax.dot_general(a, w, dims, preferred_element_type=F32)


class _Carry:
    LATE_STEPS = 2

    def __init__(self, ins, outs, aliases, sems, phases):
        self.ins, self.outs, self.aliases, self.sems, self.phases = ins, outs, aliases, sems, phases


def _call(name, core, grid, in_specs, out_specs, out_shape, scratch, args, carry=None):
    n_in, n_out, n_scr = len(in_specs), len(out_specs), len(scratch)
    if carry is None:
        out = pl.pallas_call(core, name=name, grid=grid, in_specs=in_specs, out_specs=out_specs, out_shape=out_shape,
                             scratch_shapes=scratch, compiler_params=_cparams(len(grid)))(*args)
        return list(out), []
    c_in, c_out = len(carry.ins), len(carry.outs)
    total = int(np.prod(grid))
    late = max(total - 1 - _Carry.LATE_STEPS, 0)

    def body(*refs):
        ins, refs = refs[:n_in], refs[n_in:]
        c_ins, refs = refs[:c_in], refs[c_in:]
        outs, refs = refs[:n_out], refs[n_out:]
        c_outs, refs = refs[:c_out], refs[c_out:]
        scr, c_sems = refs[:n_scr], refs[n_scr:]
        step = pl.program_id(0)
        for axis in range(1, len(grid)):
            step = step * grid[axis] + pl.program_id(axis)

        def run(when, at):
            for w, fn in carry.phases:
                if w == when:
                    pl.when(step == at)(functools.partial(fn, c_ins, c_outs, c_sems))

        run("first", 0)
        core(*ins, *outs, *scr)
        run("late", late)
        run("last", total - 1)

    out = pl.pallas_call(
        body, name=name, grid=grid, in_specs=list(in_specs) + [ANY] * c_in, out_specs=list(out_specs) + [ANY] * c_out,
        out_shape=list(out_shape) + list(carry.outs), scratch_shapes=list(scratch) + list(carry.sems),
        input_output_aliases={n_in + a: n_out + b for a, b in carry.aliases.items()},
        compiler_params=_cparams(len(grid)),
    )(*args, *carry.ins)
    return list(out[:n_out]), list(out[n_out:])


def _mm_nblk(name, a, w, *, trans_w, out_blocked, out_dtype, tm=1024, carry=None):
    m, k = a.shape
    nj = w.shape[0]
    nb = w.shape[1] if trans_w else w.shape[2]
    tm = _row_tile(m, tm)

    def core(a_ref, w_ref, o_ref):
        o_ref[...] = _dot(a_ref[...], w_ref[...], trans_w).astype(o_ref.dtype)

    if out_blocked:
        out_shape, out_spec = (nj, m, nb), pl.BlockSpec((None, tm, nb), lambda j, i: (j, i, 0))
    else:
        out_shape, out_spec = (m, nj * nb), pl.BlockSpec((tm, nb), lambda j, i: (i, j))
    out, carried = _call(
        name, core, (nj, m // tm),
        [pl.BlockSpec((tm, k), lambda j, i: (i, 0)), pl.BlockSpec((None,) + w.shape[1:], lambda j, i: (j, 0, 0))],
        [out_spec], [jax.ShapeDtypeStruct(out_shape, out_dtype)], [], [a, w], carry)
    return out[0] if carry is None else (out[0], carried)


def _mm_kblk(name, pairs, *, trans_w, out_dtype=F32, tm=512, sub=256, epilogue=None, rows=(), vecs=(), row_outs=None,
             vec_outs=(), carry=None):
    w0 = pairs[0][1]
    n = w0.shape[1] if trans_w else w0.shape[2]
    blocks = [(w.shape[0], w.shape[2] if trans_w else w.shape[1]) for _, w in pairs]
    m = pairs[0][0].shape[-2]
    tm = _row_tile(m, tm)
    ts = _row_tile(tm, sub)
    n_pairs, n_rows, n_vecs = len(pairs), len(rows), len(vecs)
    if epilogue is None:
        epilogue, row_outs = (lambda acc, r, v: ([acc], [])), [(n, out_dtype)]
    n_ro, n_vo = len(row_outs), len(vec_outs)

    def core(*refs):
        pair_refs, refs = refs[:2 * n_pairs], refs[2 * n_pairs:]
        row_refs, refs = refs[:n_rows], refs[n_rows:]
        vec_refs, refs = refs[:n_vecs], refs[n_vecs:]
        ro_refs, vo_refs = refs[:n_ro], refs[n_ro:]
        if n_vo:
            @pl.when(pl.program_id(0) == 0)
            def _():
                for ref in vo_refs:
                    ref[...] = jnp.zeros_like(ref)

        vec_vals = [v[...] for v in vec_refs]
        sums = None
        for r0 in range(0, tm, ts):
            sub_rows = slice(r0, r0 + ts)
            acc = None
            for p in range(n_pairs):
                a_ref, w_ref = pair_refs[2 * p], pair_refs[2 * p + 1]
                nj, kb = blocks[p]
                for j in range(nj):
                    a_blk = a_ref[j, sub_rows, :] if len(a_ref.shape) == 3 else a_ref[sub_rows, j * kb:(j + 1) * kb]
                    part = _dot(a_blk, w_ref[j], trans_w)
                    acc = part if acc is None else acc + part
            ro, vo = epilogue(acc, [r[sub_rows, :] for r in row_refs], vec_vals)
            for ref, val in zip(ro_refs, ro):
                ref[sub_rows, :] = val.astype(ref.dtype)
            sums = vo if sums is None else [s + v for s, v in zip(sums, vo)]
        for ref, val in zip(vo_refs, sums or []):
            ref[...] += val

    in_specs, args = [], []
    for (a, w), (nj, kb) in zip(pairs, blocks):
        if a.ndim == 3:
            in_specs.append(pl.BlockSpec((nj, tm, kb), lambda i: (0, i, 0)))
        else:
            in_specs.append(pl.BlockSpec((tm, nj * kb), lambda i: (i, 0)))
        in_specs.append(pl.BlockSpec(w.shape, lambda i: (0, 0, 0), pipeline_mode=pl.Buffered(1)))
        args += [a, w]
    in_specs += [pl.BlockSpec((tm, r.shape[1]), lambda i: (i, 0)) for r in rows]
    in_specs += [pl.BlockSpec(v.shape, lambda i: (0, 0)) for v in vecs]
    out_specs = [pl.BlockSpec((tm, cols), lambda i: (i, 0)) for cols, _ in row_outs]
    out_specs += [pl.BlockSpec((1, cols), lambda i: (0, 0)) for cols in vec_outs]
    out_shape = [jax.ShapeDtypeStruct((m, cols), dt) for cols, dt in row_outs]
    out_shape += [jax.ShapeDtypeStruct((1, cols), F32) for cols in vec_outs]
    return _call(name, core, (m // tm,), in_specs, out_specs, out_shape, [], args + list(rows) + list(vecs), carry)


def _mm_tn(name, a, a_mode, b, b_mode, *, out_dtype=BF16, tt=2048):
    nj = N_CHIPS
    t = a.shape[-2]
    tt = _row_tile(t, tt)

    def spec(x, mode):
        if mode == "full":
            return x.shape[1], pl.BlockSpec((tt, x.shape[1]), lambda j, s: (s, 0))
        if mode == "col":
            cb = x.shape[1] // nj
            return cb, pl.BlockSpec((tt, cb), lambda j, s: (s, j))
        return x.shape[2], pl.BlockSpec((None, tt, x.shape[2]), lambda j, s: (j, s, 0))

    ca, a_spec = spec(a, a_mode)
    cb, b_spec = spec(b, b_mode)
    n_steps = t // tt

    def body(a_ref, b_ref, o_ref, acc_ref):
        s = pl.program_id(1)

        @pl.when(s == 0)
        def _():
            acc_ref[...] = jnp.zeros_like(acc_ref)

        acc_ref[...] += lax.dot_general(a_ref[...], b_ref[...], (((0,), (0,)), ((), ())),
                                        preferred_element_type=F32)

        @pl.when(s == n_steps - 1)
        def _():
            o_ref[...] = acc_ref[...].astype(o_ref.dtype)

    return pl.pallas_call(
        body, name=name, grid=(nj, n_steps), in_specs=[a_spec, b_spec],
        out_specs=pl.BlockSpec((None, ca, cb), lambda j, s: (j, 0, 0)),
        out_shape=jax.ShapeDtypeStruct((nj, ca, cb), out_dtype),
        scratch_shapes=[pltpu.VMEM((ca, cb), F32)], compiler_params=_cparams(2),
    )(a, b)


def _mm_tn_wide(name, a, b, cb, *, a_split=1, out_dtype=BF16, tt=1024):
    t, ca = a.shape
    nb = b.shape[1] // cb
    tt = _row_tile(t, tt)
    n_steps = t // tt
    piece = ca // a_split

    def body(a_ref, b_ref, o_ref, acc_ref):
        s = pl.program_id(0)

        @pl.when(s == 0)
        def _():
            acc_ref[...] = jnp.zeros_like(acc_ref)

        for j in range(nb):
            bv = b_ref[:, j * cb:(j + 1) * cb]
            for c in range(a_split):
                rows = slice(c * piece, (c + 1) * piece)
                acc_ref[j, rows, :] += lax.dot_general(a_ref[:, rows], bv, (((0,), (0,)), ((), ())),
                                                       preferred_element_type=F32)

        @pl.when(s == n_steps - 1)
        def _():
            o_ref[...] = acc_ref[...].astype(o_ref.dtype)

    return pl.pallas_call(
        body, name=name, grid=(n_steps,),
        in_specs=[pl.BlockSpec((tt, ca), lambda s: (s, 0)), pl.BlockSpec((tt, nb * cb), lambda s: (s, 0))],
        out_specs=pl.BlockSpec((nb, ca, cb), lambda s: (0, 0, 0)),
        out_shape=jax.ShapeDtypeStruct((nb, ca, cb), out_dtype),
        scratch_shapes=[pltpu.VMEM((nb, ca, cb), F32)], compiler_params=_cparams(1),
    )(a, b)


def _rowwise(name, fn, rows, vecs, row_outs, vec_outs, *, tm=256, carry=None):
    nrows = rows[0][0].shape[0]
    tm = _row_tile(nrows, tm)
    n_r, n_v, n_ro, n_vo = len(rows), len(vecs), len(row_outs), len(vec_outs)

    def body(*refs):
        r_vals = [r[...] for r in refs[:n_r]]
        v_vals = [r[...] for r in refs[n_r:n_r + n_v]]
        ro_refs = refs[n_r + n_v:n_r + n_v + n_ro]
        vo_refs = refs[n_r + n_v + n_ro:]
        ro, vo = fn(r_vals, v_vals)
        for ref, val in zip(ro_refs, ro):
            ref[...] = val.astype(ref.dtype)
        if n_vo:
            @pl.when(pl.program_id(0) == 0)
            def _():
                for ref in vo_refs:
                    ref[...] = jnp.zeros_like(ref)

            for ref, val in zip(vo_refs, vo):
                ref[...] += val

    in_specs = [pl.BlockSpec((tm, cols), functools.partial(lambda i, cb: (i, cb), cb=cb)) for _, cols, cb in rows]
    in_specs += [pl.BlockSpec(v.shape, functools.partial(lambda i, nd: (0,) * nd, nd=v.ndim)) for v in vecs]
    out_specs = [pl.BlockSpec((tm, cols), lambda i: (i, 0)) for cols, _ in row_outs]
    out_specs += [pl.BlockSpec((1, cols), lambda i: (0, 0)) for cols in vec_outs]
    out_shape = [jax.ShapeDtypeStruct((nrows, cols), dt) for cols, dt in row_outs]
    out_shape += [jax.ShapeDtypeStruct((1, cols), F32) for cols in vec_outs]
    out, carried = _call(name, body, (nrows // tm,), in_specs, out_specs, out_shape, [],
                         [r[0] for r in rows] + list(vecs), carry)
    return out if carry is None else (out, carried)


def _whole(x):
    return (x, x.shape[1], 0)


def _colsum(x):
    return jnp.sum(x, axis=0, keepdims=True)


def _rstd(x):
    return lax.rsqrt(jnp.mean(x * x, axis=-1, keepdims=True) + EPS)


def _rms_bwd(dn, x, g):
    r = _rstd(x)
    c = dn * g
    dx = r * c - x * (r * r * r) * jnp.mean(c * x, axis=-1, keepdims=True)
    return dx, _colsum(dn * x * r)


def _rms_fwd(name, x, g, carry):
    def fn(r, v):
        (xv,), (gv,) = r, v
        return [xv * _rstd(xv) * gv], []

    (n,), carried = _rowwise(name, fn, [_whole(x)], [g], [(x.shape[1], BF16)], [], carry=carry)
    return n, carried


def _ep_post_res_pre(scale):
    def epilogue(acc, rows, vecs):
        (resid,), (g_post, g_next) = rows, vecs
        h = resid + scale * (acc * _rstd(acc) * g_post)
        return [acc, h, h * _rstd(h) * g_next], []

    return epilogue


def _post_bwd(dh, f, g_post, scale):
    return _rms_bwd(scale * dh, f, g_post)


def _ep_loss(scale, d):
    def epilogue(acc, rows, vecs):
        (resid, target), (g_post,) = rows, vecs
        err = resid + scale * (acc * _rstd(acc) * g_post) - target
        dy = err * (1.0 / d)
        df, dg_post = _post_bwd(dy, acc, g_post, scale)
        return [dy, df], [_colsum(err * err), dg_post]

    return epilogue


def _ep_pre_bwd_post(scale_prev):
    def epilogue(acc, rows, vecs):
        (h, dh_up, f_prev), (g_pre, g_post_prev) = rows, vecs
        dx, dg_pre = _rms_bwd(acc, h, g_pre)
        dh = dh_up + dx
        df, dg_post = _post_bwd(dh, f_prev, g_post_prev, scale_prev)
        return [dh, df], [dg_pre, dg_post]

    return epilogue


def _ep_pre_bwd_first():
    def epilogue(acc, rows, vecs):
        (x, dh_up), (g_pre,) = rows, vecs
        dx, dg_pre = _rms_bwd(acc, x, g_pre)
        return [dh_up + dx], [dg_pre]

    return epilogue


def _gate_specs(d, tm):
    first = (3 * D_ATTN + 2 * D_CONV) // COL
    return [pl.BlockSpec((tm, COL), functools.partial(lambda i, cb: (i, cb), cb=first + p)) for p in range(2 * d // COL)]


def _gate(piece_refs, bias_ref, c0, width):
    p, off = divmod(c0, COL)
    return jax.nn.sigmoid(piece_refs[p][:, off:off + width].astype(F32) + bias_ref[:, c0:c0 + width])


def _resident(w):
    return pl.BlockSpec(w.shape, functools.partial(lambda i, nd: (0,) * nd, nd=w.ndim), pipeline_mode=pl.Buffered(1))


def _mix_merge(att, cs, wao, wco, proj, gate_bias, tm=512):
    t = att.shape[0]
    nj, _, nb = wao.shape
    d = nj * nb
    tm = _row_tile(t, tm)
    gate_specs = _gate_specs(d, tm)
    n_p = len(gate_specs)

    def body(att_ref, cs_ref, wao_ref, wco_ref, *rest):
        pieces, (gb_ref, ya_ref, yb_ref, m_ref) = rest[:n_p], rest[n_p:]
        av, cv = att_ref[...], cs_ref[...]
        for j in range(nj):
            cols = slice(j * nb, (j + 1) * nb)
            ya = _dot(av, wao_ref[j], False)
            yb = _dot(cv, wco_ref[j], False)
            merged = _gate(pieces, gb_ref, j * nb, nb) * ya + _gate(pieces, gb_ref, d + j * nb, nb) * yb
            ya_ref[:, cols] = ya.astype(ya_ref.dtype)
            yb_ref[:, cols] = yb.astype(yb_ref.dtype)
            m_ref[:, cols] = merged.astype(m_ref.dtype)

    row = lambda x: pl.BlockSpec((tm, x.shape[1]), lambda i: (i, 0))
    out_spec = pl.BlockSpec((tm, d), lambda i: (i, 0))
    return pl.pallas_call(
        body, name="mix_merge", grid=(t // tm,),
        in_specs=[row(att), row(cs), _resident(wao), _resident(wco)] + gate_specs + [_resident(gate_bias)],
        out_specs=[out_spec] * 3, out_shape=[jax.ShapeDtypeStruct((t, d), BF16)] * 3, compiler_params=_cparams(1),
    )(att, cs, wao, wco, *([proj] * n_p), gate_bias)


def _mix_d_merge(dmo, wout, y_a, y_b, wao, wco, proj, gate_bias, tm=512):
    t, d = dmo.shape
    nj, _, nb = wao.shape
    ka, kc = wao.shape[1], wco.shape[1]
    tm = _row_tile(t, tm)
    gate_specs = _gate_specs(d, tm)
    n_p = len(gate_specs)

    def body(dmo_ref, wout_ref, ya_ref, yb_ref, wao_ref, wco_ref, *rest):
        pieces, (gb_ref, dya_ref, dyb_ref, dg_ref, datt_ref, dcs_ref, dgb_ref) = rest[:n_p], rest[n_p:]

        @pl.when(pl.program_id(0) == 0)
        def _():
            dgb_ref[...] = jnp.zeros_like(dgb_ref)

        dmo_v = dmo_ref[...]
        datt = dcs = None
        for j in range(nj):
            cols, cols_b = slice(j * nb, (j + 1) * nb), slice(d + j * nb, d + (j + 1) * nb)
            dm = _dot(dmo_v, wout_ref[j], True)
            ga, gb = _gate(pieces, gb_ref, j * nb, nb), _gate(pieces, gb_ref, d + j * nb, nb)
            dya, dyb = (dm * ga).astype(BF16), (dm * gb).astype(BF16)
            dga = dm * ya_ref[:, cols].astype(F32) * ga * (1.0 - ga)
            dgb = dm * yb_ref[:, cols].astype(F32) * gb * (1.0 - gb)
            dya_ref[:, cols], dyb_ref[:, cols] = dya, dyb
            dg_ref[:, cols], dg_ref[:, cols_b] = dga.astype(dg_ref.dtype), dgb.astype(dg_ref.dtype)
            dgb_ref[:, cols] += _colsum(dga)
            dgb_ref[:, cols_b] += _colsum(dgb)
            pa, pc = _dot(dya, wao_ref[j], True), _dot(dyb, wco_ref[j], True)
            datt, dcs = (pa, pc) if datt is None else (datt + pa, dcs + pc)
        datt_ref[...] = datt.astype(datt_ref.dtype)
        dcs_ref[...] = dcs.astype(dcs_ref.dtype)

    row = lambda cols: pl.BlockSpec((tm, cols), lambda i: (i, 0))
    return pl.pallas_call(
        body, name="mix_d_merge", grid=(t // tm,),
        in_specs=[row(d), _resident(wout), row(d), row(d), _resident(wao), _resident(wco)] + gate_specs
        + [_resident(gate_bias)],
        out_specs=[row(d), row(d), row(2 * d), row(ka), row(kc), pl.BlockSpec((1, 2 * d), lambda i: (0, 0))],
        out_shape=[jax.ShapeDtypeStruct((t, d), BF16), jax.ShapeDtypeStruct((t, d), BF16),
                   jax.ShapeDtypeStruct((t, 2 * d), BF16), jax.ShapeDtypeStruct((t, ka), BF16),
                   jax.ShapeDtypeStruct((t, kc), F32), jax.ShapeDtypeStruct((1, 2 * d), F32)],
        compiler_params=_cparams(1),
    )(dmo, wout, y_a, y_b, wao, wco, *([proj] * n_p), gate_bias)


def _adamw_math(wv, gv, mv, vv):
    m2 = ADAM_B1 * mv + (1.0 - ADAM_B1) * gv
    v2 = ADAM_B2 * vv + (1.0 - ADAM_B2) * (gv * gv)
    m_hat = m2 / (1.0 - ADAM_B1 ** ADAM_STEP)
    v_hat = v2 / (1.0 - ADAM_B2 ** ADAM_STEP)
    delta = -ADAM_LR * (m_hat / (jnp.sqrt(v_hat) + ADAM_EPS) + ADAM_WD * wv)
    return delta, m2, v2


def _adamw(name, w, g, m, v):
    def fn(r, _):
        return list(_adamw_math(*r)), []

    c = w.shape[1]
    return _rowwise(name, fn, [_whole(w), _whole(g), _whole(m), _whole(v)], [], [(c, F32)] * 3, [], tm=256)


POS_C, POS_CHIP, POS_PEER = 0, 1, 2


def _placed_call(body, name, pos, grid, in_specs, out_specs, out_shape, args):
    return pl.pallas_call(
        body, name=name, out_shape=out_shape, compiler_params=_cparams(len(grid)),
        grid_spec=pltpu.PrefetchScalarGridSpec(num_scalar_prefetch=1, grid=grid, in_specs=in_specs,
                                               out_specs=out_specs),
    )(pos, *args)


def _cast_into(name, pos, w):
    r, cols = w.shape
    tm = _row_tile(r, 1024)

    def body(pos_ref, w_ref, o_ref):
        o_ref[...] = w_ref[...].astype(o_ref.dtype)

    return _placed_call(
        body, name, pos, (r // tm,), [pl.BlockSpec((tm, cols), lambda i, pos: (i, 0))],
        pl.BlockSpec((None, tm, cols), lambda i, pos: (pos[POS_CHIP], i, 0)),
        jax.ShapeDtypeStruct((N_CHIPS, r, cols), BF16), [w])


def _add_pair(name, pos, grad, landed):
    nj, half, cols = landed.shape
    tm = _row_tile(half, 512)
    nb = half // tm

    def body(pos_ref, g_ref, l_ref, o_ref):
        o_ref[...] = (g_ref[...].astype(F32) + l_ref[...].astype(F32)).astype(o_ref.dtype)

    spec = pl.BlockSpec((None, tm, cols), lambda j, i, pos: (j, i, 0))
    return _placed_call(
        body, name, pos, (nj, nb),
        [pl.BlockSpec((None, tm, cols), lambda j, i, pos: (j, pos[POS_C] * nb + i, 0)), spec], spec,
        jax.ShapeDtypeStruct(landed.shape, BF16), [grad, landed])


def _add_chips(name, pos, part, landed):
    _, half, cols = landed.shape
    tm = _row_tile(half, 512)

    def body(pos_ref, p_ref, l0_ref, l1_ref, l2_ref, o_ref):
        acc = p_ref[...].astype(F32)
        for ref in (l0_ref, l1_ref, l2_ref):
            acc = acc + ref[...].astype(F32)
        o_ref[...] = acc

    slot = lambda at: pl.BlockSpec((None, tm, cols), functools.partial(lambda i, pos, at: (pos[at], i, 0), at=at))
    return _placed_call(
        body, name, pos, (half // tm,), [slot(POS_CHIP)] + [slot(POS_PEER + k) for k in range(3)],
        pl.BlockSpec((tm, cols), lambda i, pos: (i, 0)), jax.ShapeDtypeStruct((half, cols), F32),
        [part, landed, landed, landed])


def _adamw_halves(name, pos, w, m, v, own, landed):
    r, cols = w.shape
    half = own.shape[0]
    tm = _row_tile(half, 256)
    nb = half // tm

    def body(pos_ref, w_ref, m_ref, v_ref, own_ref, land_ref, g_out, d_out, m_out, v_out):
        mine = pl.program_id(0) == pos_ref[POS_C]
        g = jnp.where(mine, own_ref[...], land_ref[...])
        delta, m2, v2 = _adamw_math(w_ref[...], g, m_ref[...], v_ref[...])
        g_out[...] = g
        d_out[...] = delta
        m_out[...] = m2
        v_out[...] = v2

    full = pl.BlockSpec((tm, cols), lambda h, i, pos: (h * nb + i, 0))
    part = pl.BlockSpec((tm, cols), lambda h, i, pos: (i, 0))
    return _placed_call(
        body, name, pos, (2, nb), [full, full, full, part, part], [full] * 4,
        [jax.ShapeDtypeStruct((r, cols), F32)] * 4, [w, m, v, own, landed])


N_START = K_PAD // Q_BLOCK


def _rel_onehot(n_q):
    e = np.arange(REL_EXT)
    dist = K_PAD - (e - (n_q - 1))
    idx = np.clip(dist, -REL_CLIP, REL_CLIP) + REL_CLIP
    return (np.arange(REL_PAD)[:, None] == idx[None, :]).astype(np.float32)


def _skew(x, left):
    row = lax.broadcasted_iota(jnp.int32, x.shape, 0)
    for bit in range(x.shape[0].bit_length() - 1):
        amount = 1 << bit
        rolled = pltpu.roll(x, REL_EXT - amount if left else amount, 1)
        x = jnp.where((row >> bit) & 1 == 1, rolled, x)
    return x


def _bias_expand(table_pad):
    onehot = jnp.asarray(_rel_onehot(Q_BLOCK))

    def body(t_ref, oh_ref, o_ref):
        ext = jnp.dot(t_ref[...], oh_ref[...], precision=lax.Precision.HIGHEST, preferred_element_type=F32)
        qc = lax.broadcasted_iota(jnp.int32, (Q_BLOCK, K_WIN), 0) // CHUNK
        kpos = lax.broadcasted_iota(jnp.int32, (Q_BLOCK, K_WIN), 1)
        band = (kpos // CHUNK >= qc) & (kpos // CHUNK <= qc + LEFT_CHUNKS)
        for h in range(N_HEADS):
            rows = jnp.broadcast_to(ext[h:h + 1, :], (Q_BLOCK, REL_EXT))
            rolled = _skew(pltpu.roll(rows, REL_EXT - (Q_BLOCK - 1), 1), left=False)[:, :K_WIN]
            for v in range(N_START + 1):
                o_ref[v, h] = jnp.where(band & (kpos + v * Q_BLOCK >= K_PAD), rolled, NEG)

    return pl.pallas_call(
        body, name="bias_expand", out_shape=jax.ShapeDtypeStruct((N_START + 1, N_HEADS, Q_BLOCK, K_WIN), F32),
        compiler_params=pltpu.CompilerParams(vmem_limit_bytes=VMEM_LIMIT_BYTES),
    )(table_pad, onehot)


def _bias_fold(dbias):
    onehot_t = jnp.asarray(_rel_onehot(CHUNK).T)

    def body(d_ref, oh_ref, o_ref, ext_ref):
        for h in range(N_HEADS):
            x = jnp.concatenate([d_ref[h], jnp.zeros((CHUNK, REL_EXT - K_WIN), F32)], axis=1)
            rolled = _skew(pltpu.roll(x, CHUNK - 1, 1), left=True)
            ext_ref[h:h + 1, :] = jnp.sum(rolled, axis=0, keepdims=True)
        o_ref[...] = jnp.dot(ext_ref[...], oh_ref[...], precision=lax.Precision.HIGHEST,
                             preferred_element_type=F32)

    return pl.pallas_call(
        body, name="bias_fold", out_shape=jax.ShapeDtypeStruct((N_HEADS, REL_PAD), F32),
        scratch_shapes=[pltpu.VMEM((N_HEADS, REL_EXT), F32)],
        compiler_params=pltpu.CompilerParams(vmem_limit_bytes=VMEM_LIMIT_BYTES),
    )(dbias, onehot_t)


def _head_lanes():
    lane = lax.broadcasted_iota(jnp.int32, (1, 2 * HEAD_DIM), 1)
    return [lane < HEAD_DIM, lane >= HEAD_DIM]


def _only(mask, x, scale=None):
    x = jnp.where(mask, x, jnp.zeros_like(x))
    return x if scale is None else x * scale


def _contract_lanes(a, b):
    return lax.dot_general(a, b, (((1,), (1,)), ((), ())), preferred_element_type=F32)


def _contract_rows(a, b):
    return lax.dot_general(a, b, (((0,), (0,)), ((), ())), preferred_element_type=F32)


PAIR = 2 * HEAD_DIM
N_PAIRS = D_ATTN // PAIR


def _attn_specs(pairs):
    width = pairs * PAIR
    per = D_ATTN // width
    row_spec = pl.BlockSpec((Q_BLOCK, width), lambda g, i: (i, g))
    kv_specs = [pl.BlockSpec((Q_BLOCK, width),
                             functools.partial(lambda g, i, kk, c0: (jnp.maximum(i + kk - N_START, 0), c0 + g),
                                               kk=kk, c0=c0))
                for c0 in (per, 2 * per) for kk in range(K_WIN // Q_BLOCK)]
    bias_spec = pl.BlockSpec((None, 2 * pairs, Q_BLOCK, K_WIN), lambda g, i: (jnp.minimum(i, N_START), g, 0, 0))
    return row_spec, kv_specs, bias_spec


def _attn_fwd(proj, bias, pairs=N_PAIRS):
    t = proj.shape[0]
    n_win = K_WIN // Q_BLOCK

    def body(q_ref, *refs):
        k_refs, v_refs = refs[:n_win], refs[n_win:2 * n_win]
        b_ref, o_ref, lse_ref = refs[2 * n_win:]
        for pp in range(pairs):
            cols = slice(pp * PAIR, (pp + 1) * PAIR)
            k = jnp.concatenate([r[:, cols] for r in k_refs], axis=0)
            v = jnp.concatenate([r[:, cols] for r in v_refs], axis=0)
            q = q_ref[:, cols]
            o = lse = None
            for hh, lanes in enumerate(_head_lanes()):
                s = _contract_lanes(_only(lanes, q, HEAD_DIM ** -0.5), k) + b_ref[2 * pp + hh]
                m = jnp.max(s, axis=1, keepdims=True)
                p = jnp.exp(s - m)
                l = jnp.sum(p, axis=1, keepdims=True)
                oh = jnp.dot(p.astype(BF16), v, preferred_element_type=F32) / l
                lse_h = jnp.broadcast_to(m + jnp.log(l), oh.shape)
                o, lse = (oh, lse_h) if o is None else (jnp.where(lanes, oh, o), jnp.where(lanes, lse_h, lse))
            o_ref[:, cols] = o.astype(o_ref.dtype)
            lse_ref[:, cols] = lse

    row_spec, kv_specs, bias_spec = _attn_specs(pairs)
    return pl.pallas_call(
        body, name="attn_fwd", grid=(N_PAIRS // pairs, t // Q_BLOCK),
        in_specs=[row_spec] + kv_specs + [bias_spec], out_specs=[row_spec, row_spec],
        out_shape=[jax.ShapeDtypeStruct((t, D_ATTN), BF16), jax.ShapeDtypeStruct((t, D_ATTN), F32)],
        compiler_params=_cparams(2),
    )(*([proj] * (1 + 2 * n_win)), bias)


def _attn_bwd(proj, bias, att, lse, datt, pairs=2):
    t = proj.shape[0]
    n_win = K_WIN // Q_BLOCK
    n_blocks = t // Q_BLOCK

    def body(q_ref, *refs):
        k_refs, v_refs = refs[:n_win], refs[n_win:2 * n_win]
        b_ref, o_ref, lse_ref, do_ref, dq_ref, dk_ref, dv_ref, db_ref, dk_acc, dv_acc = refs[2 * n_win:]
        i = pl.program_id(1)

        @pl.when(i == 0)
        def _():
            dk_acc[...] = jnp.zeros_like(dk_acc)
            dv_acc[...] = jnp.zeros_like(dv_acc)
            db_ref[...] = jnp.zeros_like(db_ref)

        rows = pl.ds(pl.multiple_of(i * Q_BLOCK, Q_BLOCK), K_WIN)
        scale = HEAD_DIM ** -0.5
        for pp in range(pairs):
            cols = slice(pp * PAIR, (pp + 1) * PAIR)
            k = jnp.concatenate([r[:, cols] for r in k_refs], axis=0)
            v = jnp.concatenate([r[:, cols] for r in v_refs], axis=0)
            q, do, o = q_ref[:, cols], do_ref[:, cols], o_ref[:, cols].astype(F32)
            dq = dk = dv = None
            for hh, lanes in enumerate(_head_lanes()):
                qh, doh = _only(lanes, q, scale), _only(lanes, do)
                s = _contract_lanes(qh, k) + b_ref[2 * pp + hh]
                lse_col = pp * PAIR + hh * HEAD_DIM
                p = jnp.exp(s - lse_ref[:, lse_col:lse_col + 1])
                delta = jnp.sum(doh.astype(F32) * o, axis=1, keepdims=True)
                ds = p * (_contract_lanes(doh, v) - delta)
                folded = ds[:CHUNK]
                for c in range(1, Q_BLOCK // CHUNK):
                    folded = folded + pltpu.roll(ds[c * CHUNK:(c + 1) * CHUNK], K_WIN - c * CHUNK, 1)
                db_ref[2 * pp + hh] += folded
                dsb = ds.astype(BF16)
                dqh = jnp.dot(dsb, k, preferred_element_type=F32)
                dq = dqh if dq is None else jnp.where(lanes, dqh, dq)
                dkh, dvh = _contract_rows(dsb, qh), _contract_rows(p.astype(BF16), doh)
                dk, dv = (dkh, dvh) if dk is None else (dk + dkh, dv + dvh)
            dq_ref[:, cols] = (dq * scale).astype(dq_ref.dtype)
            dk_acc[rows, cols] += dk
            dv_acc[rows, cols] += dv

        @pl.when(i == n_blocks - 1)
        def _():
            dk_ref[...] = dk_acc[K_PAD:, :].astype(dk_ref.dtype)
            dv_ref[...] = dv_acc[K_PAD:, :].astype(dv_ref.dtype)

    width = pairs * PAIR
    row_spec, kv_specs, bias_spec = _attn_specs(pairs)
    full_spec = pl.BlockSpec((t, width), lambda g, i: (0, g))
    return pl.pallas_call(
        body, name="attn_bwd", grid=(N_PAIRS // pairs, n_blocks),
        in_specs=[row_spec] + kv_specs + [bias_spec, row_spec, row_spec, row_spec],
        out_specs=[row_spec, full_spec, full_spec,
                   pl.BlockSpec((2 * pairs, CHUNK, K_WIN), lambda g, i: (g, 0, 0))],
        out_shape=[jax.ShapeDtypeStruct((t, D_ATTN), BF16)] * 3 + [jax.ShapeDtypeStruct((N_HEADS, CHUNK, K_WIN), F32)],
        scratch_shapes=[pltpu.VMEM((t + K_PAD, width), F32)] * 2, compiler_params=_cparams(2),
    )(*([proj] * (1 + 2 * n_win)), bias, att, lse, datt)


CONV_LEAD = CONV_HALO - (CONV_WIDTH - 1)
CONV_LANES = 128
CONV_ROWS = 64


def _conv_specs(t):
    tt = _row_tile(t, CONV_TILE)
    per = tt // CONV_HALO
    n_halo = t // CONV_HALO
    tile = lambda cb: pl.BlockSpec((tt, COL), functools.partial(lambda i, cb: (i, cb), cb=cb))
    prev = lambda cb: pl.BlockSpec((CONV_HALO, COL),
                                   functools.partial(lambda i, cb: (jnp.maximum(i * per - 1, 0), cb), cb=cb))
    nxt = lambda cb: pl.BlockSpec((CONV_HALO, COL),
                                  functools.partial(lambda i, cb: (jnp.minimum((i + 1) * per, n_halo - 1), cb), cb=cb))
    vec = lambda shape: pl.BlockSpec(shape, lambda i: (0, 0))
    return tt, tile, prev, nxt, vec


def _glu(ca, cg, bias):
    return (ca.astype(F32) + bias[:, :D_CONV]) * jax.nn.sigmoid(cg.astype(F32) + bias[:, D_CONV:])


SUBLANES = 8


def _shift_copies(ext_ref):
    n = ext_ref.shape[1] - SUBLANES
    for s in range(1, SUBLANES):
        ext_ref[s, 0:n, :] = ext_ref[0, s:s + n, :]


def _tap_tiles(ext_ref, first_row, r0, lanes):
    n_g = CONV_ROWS // SUBLANES
    for s in range(SUBLANES):
        taps = [w for w in range(CONV_WIDTH) if first_row(w) % SUBLANES == s]
        if not taps:
            continue
        lo = min(first_row(w) for w in taps) - s
        n_tiles = (max(first_row(w) for w in taps) - s - lo) // SUBLANES + n_g
        tiles = [ext_ref[s, r0 + lo + SUBLANES * b:r0 + lo + SUBLANES * (b + 1), lanes] for b in range(n_tiles)]
        for w in taps:
            k = (first_row(w) - s - lo) // SUBLANES
            yield w, tiles[k:k + n_g]


def _taps(ext_ref, tt, first_row, w_ref, out_ref):
    n_g = CONV_ROWS // SUBLANES
    for l0 in range(0, D_CONV, CONV_LANES):
        lanes = slice(l0, l0 + CONV_LANES)
        for r0 in range(0, tt, CONV_ROWS):
            acc = [jnp.zeros((SUBLANES, CONV_LANES), F32)] * n_g
            for w, tiles in _tap_tiles(ext_ref, first_row, r0, lanes):
                weight = jnp.broadcast_to(w_ref[w:w + 1, lanes], (SUBLANES, CONV_LANES))
                acc = [a + t * weight for a, t in zip(acc, tiles)]
            for g in range(n_g):
                out_ref[r0 + SUBLANES * g:r0 + SUBLANES * (g + 1), lanes] = acc[g]


def _tap_sums(ext_ref, tt, first_row, x_ref, out_ref):
    n_g = CONV_ROWS // SUBLANES
    for l0 in range(0, D_CONV, CONV_LANES):
        lanes = slice(l0, l0 + CONV_LANES)
        acc = [jnp.zeros((SUBLANES, CONV_LANES), F32)] * CONV_WIDTH
        for r0 in range(0, tt, CONV_ROWS):
            x = [x_ref[0, r0 + SUBLANES * g:r0 + SUBLANES * (g + 1), lanes] for g in range(n_g)]
            for w, tiles in _tap_tiles(ext_ref, first_row, r0, lanes):
                part = tiles[0] * x[0]
                for g in range(1, n_g):
                    part = part + tiles[g] * x[g]
                acc[w] = acc[w] + part
        for w in range(CONV_WIDTH):
            out_ref[w:w + 1, lanes] += jnp.sum(acc[w], axis=0, keepdims=True)


def _conv_fwd(proj, glu_bias, dw, dw_b, ln_g, ln_b):
    t = proj.shape[0]
    tt, tile, prev, nxt, vec = _conv_specs(t)
    ca_blk, cg_blk = 3 * D_ATTN // COL, 3 * D_ATTN // COL + 1

    def body(ca_ref, cg_ref, pa_ref, pg_ref, gb_ref, dw_ref, dwb_ref, g_ref, b_ref, cs_ref, c_ref, z_ref, ext_ref):
        i = pl.program_id(0)
        bias = gb_ref[...]
        c = _glu(ca_ref[...], cg_ref[...], bias)
        halo = _glu(pa_ref[...], pg_ref[...], bias)
        ext_ref[0, 0:CONV_HALO, :] = jnp.where(i == 0, 0.0, halo)
        ext_ref[0, CONV_HALO:, :] = c
        _shift_copies(ext_ref)
        c_ref[...] = c
        _taps(ext_ref, tt, lambda w: CONV_LEAD + w, dw_ref, z_ref)
        z = z_ref[...] + dwb_ref[...]
        z_ref[...] = z
        mu = jnp.mean(z, axis=-1, keepdims=True)
        zc = z - mu
        y = zc * lax.rsqrt(jnp.mean(zc * zc, axis=-1, keepdims=True) + EPS) * g_ref[...] + b_ref[...]
        cs_ref[...] = (y * jax.nn.sigmoid(y)).astype(cs_ref.dtype)

    out_spec = pl.BlockSpec((tt, D_CONV), lambda i: (i, 0))
    return pl.pallas_call(
        body, name="conv_fwd", grid=(t // tt,),
        in_specs=[tile(ca_blk), tile(cg_blk), prev(ca_blk), prev(cg_blk), vec(glu_bias.shape), vec(dw.shape),
                  vec(dw_b.shape), vec(ln_g.shape), vec(ln_b.shape)],
        out_specs=[out_spec] * 3,
        out_shape=[jax.ShapeDtypeStruct((t, D_CONV), BF16), jax.ShapeDtypeStruct((t, D_CONV), F32),
                   jax.ShapeDtypeStruct((t, D_CONV), F32)],
        scratch_shapes=[pltpu.VMEM((SUBLANES, tt + CONV_HALO, D_CONV), F32)], compiler_params=_cparams(1),
    )(proj, proj, proj, proj, glu_bias, dw, dw_b, ln_g, ln_b)


def _conv_bwd(proj, c, z, dcs, glu_bias, dw, ln_g, ln_b):
    t = proj.shape[0]
    tt, tile, prev, nxt, vec = _conv_specs(t)
    n_tiles = t // tt
    ca_blk, cg_blk = 3 * D_ATTN // COL, 3 * D_ATTN // COL + 1

    def ln_bwd(zv, dcsv, g, b):
        mu = jnp.mean(zv, axis=-1, keepdims=True)
        zc = zv - mu
        rstd = lax.rsqrt(jnp.mean(zc * zc, axis=-1, keepdims=True) + EPS)
        zhat = zc * rstd
        y = zhat * g + b
        sig = jax.nn.sigmoid(y)
        dy = dcsv * sig * (1.0 + y * (1.0 - sig))
        dzh = dy * g
        dz = rstd * (dzh - jnp.mean(dzh, axis=-1, keepdims=True) - zhat * jnp.mean(dzh * zhat, axis=-1, keepdims=True))
        return dz, dy, zhat

    def body(ca_ref, cg_ref, c_ref, cprev_ref, z_ref, znext_ref, dcs_ref, dcsnext_ref, gb_ref, dw_ref, g_ref, b_ref,
             dcin_ref, ddw_ref, ddwb_ref, dg_ref, db_ref, dgb_ref, cext_ref, dzext_ref, dc_ref):
        i = pl.program_id(0)

        @pl.when(i == 0)
        def _():
            for ref in (ddw_ref, ddwb_ref, dg_ref, db_ref, dgb_ref):
                ref[...] = jnp.zeros_like(ref)

        g, b = g_ref[...], b_ref[...]
        dz, dy, zhat = ln_bwd(z_ref[...], dcs_ref[...], g, b)
        dz_next, _, _ = ln_bwd(znext_ref[...], dcsnext_ref[...], g, b)
        dg_ref[...] += _colsum(dy * zhat)
        db_ref[...] += _colsum(dy)
        ddwb_ref[...] += _colsum(dz)
        dzext_ref[0, 0:tt, :] = dz
        dzext_ref[0, tt:, :] = jnp.where(i == n_tiles - 1, 0.0, dz_next)
        _shift_copies(dzext_ref)
        cext_ref[0, 0:CONV_HALO, :] = jnp.where(i == 0, 0.0, cprev_ref[...])
        cext_ref[0, CONV_HALO:, :] = c_ref[...]
        _shift_copies(cext_ref)
        _taps(dzext_ref, tt, lambda w: CONV_WIDTH - 1 - w, dw_ref, dc_ref)
        _tap_sums(cext_ref, tt, lambda w: CONV_LEAD + w, dzext_ref, ddw_ref)
        bias = gb_ref[...]
        a_in = ca_ref[...].astype(F32) + bias[:, :D_CONV]
        sg = jax.nn.sigmoid(cg_ref[...].astype(F32) + bias[:, D_CONV:])
        dc = dc_ref[...]
        dcin = jnp.concatenate([dc * sg, dc * a_in * sg * (1.0 - sg)], axis=1)
        dcin_ref[...] = dcin.astype(dcin_ref.dtype)
        dgb_ref[...] += _colsum(dcin)

    row = lambda: pl.BlockSpec((tt, D_CONV), lambda i: (i, 0))
    per = tt // CONV_HALO
    n_halo = t // CONV_HALO
    prev_row = pl.BlockSpec((CONV_HALO, D_CONV), lambda i: (jnp.maximum(i * per - 1, 0), 0))
    next_row = lambda: pl.BlockSpec((CONV_HALO, D_CONV), lambda i: (jnp.minimum((i + 1) * per, n_halo - 1), 0))
    acc = lambda shape: pl.BlockSpec(shape, lambda i: (0, 0))
    return pl.pallas_call(
        body, name="conv_bwd", grid=(n_tiles,),
        in_specs=[tile(ca_blk), tile(cg_blk), row(), prev_row, row(), next_row(), row(), next_row(),
                  vec(glu_bias.shape), vec(dw.shape), vec(ln_g.shape), vec(ln_b.shape)],
        out_specs=[pl.BlockSpec((tt, 2 * D_CONV), lambda i: (i, 0)), acc(dw.shape), acc((1, D_CONV)),
                   acc((1, D_CONV)), acc((1, D_CONV)), acc((1, 2 * D_CONV))],
        out_shape=[jax.ShapeDtypeStruct((t, 2 * D_CONV), BF16), jax.ShapeDtypeStruct(dw.shape, F32),
                   jax.ShapeDtypeStruct((1, D_CONV), F32), jax.ShapeDtypeStruct((1, D_CONV), F32),
                   jax.ShapeDtypeStruct((1, D_CONV), F32), jax.ShapeDtypeStruct((1, 2 * D_CONV), F32)],
        scratch_shapes=[pltpu.VMEM((SUBLANES, tt + CONV_HALO, D_CONV), F32),
                        pltpu.VMEM((SUBLANES, tt + CONV_HALO, D_CONV), F32), pltpu.VMEM((tt, D_CONV), F32)],
        compiler_params=_cparams(1),
    )(proj, proj, c, c, z, z, dcs, dcs, glu_bias, dw, ln_g, ln_b)


def _place():
    x, y, c = lax.axis_index("x"), lax.axis_index("y"), lax.axis_index("c")
    chips = [(1 - x, y), (x, 1 - y), (1 - x, 1 - y)]
    return x, y, c, chips


def _chip_index(chip):
    return 2 * chip[0] + chip[1]


def _half_rows(c, half):
    return pl.ds(pl.multiple_of(c * half, 16), half)


def _gather_carry(blocked):
    n = len(blocked)

    def over_ici(o_refs, send_sems, recv_sems):
        x, y, c, chips = _place()
        me = _chip_index((x, y))
        copies = []
        for a in range(n):
            mine = o_refs[a].at[me, _half_rows(c, o_refs[a].shape[1] // 2), :]
            for k, chip in enumerate(chips):
                copies.append(pltpu.make_async_remote_copy(
                    src_ref=mine, dst_ref=mine, send_sem=send_sems.at[6 * a + k], recv_sem=recv_sems.at[6 * a + k],
                    device_id=(chip[0], chip[1], c), device_id_type=MESH))
        return copies

    def to_sibling(o_refs, send_sems, recv_sems, sent_by_me):
        x, y, c, chips = _place()
        copies = []
        for a in range(n):
            rows = _half_rows(c if sent_by_me else 1 - c, o_refs[a].shape[1] // 2)
            for k, chip in enumerate(chips):
                landed = o_refs[a].at[_chip_index(chip), rows, :]
                copies.append(pltpu.make_async_remote_copy(
                    src_ref=landed, dst_ref=landed, send_sem=send_sems.at[6 * a + 3 + k],
                    recv_sem=recv_sems.at[6 * a + 3 + k], device_id=(x, y, 1 - c), device_id_type=MESH))
        return copies

    def start(ins, outs, sems):
        for cp in over_ici(outs, *sems):
            cp.start()

    def hand_on(ins, outs, sems):
        for arrived, onward in zip(over_ici(outs, *sems), to_sibling(outs, *sems, True)):
            arrived.wait_recv()
            onward.start()

    def finish(ins, outs, sems):
        for cp in to_sibling(outs, *sems, False):
            cp.wait_recv()
        for cp in over_ici(outs, *sems) + to_sibling(outs, *sems, True):
            cp.wait_send()

    return _Carry(
        ins=list(blocked), outs=[jax.ShapeDtypeStruct(w.shape, w.dtype) for w in blocked],
        aliases={a: a for a in range(n)},
        sems=[pltpu.SemaphoreType.DMA((6 * n,)), pltpu.SemaphoreType.DMA((6 * n,))],
        phases=[("first", start), ("late", hand_on), ("last", finish)])


def _pair_exchange(name, grads):
    n = len(grads)

    def body(*refs):
        g_refs, land_refs = refs[:n], refs[n:2 * n]
        send_sems, recv_sems = refs[2 * n:]
        x, y, c, _ = _place()
        copies = []
        for a in range(n):
            half = g_refs[a].shape[1] // 2
            cp = pltpu.make_async_remote_copy(
                src_ref=g_refs[a].at[:, _half_rows(1 - c, half), :], dst_ref=land_refs[a],
                send_sem=send_sems.at[a], recv_sem=recv_sems.at[a], device_id=(x, y, 1 - c), device_id_type=MESH)
            cp.start()
            copies.append(cp)
        for cp in copies:
            cp.wait()

    return pl.pallas_call(
        body, name=name, in_specs=[ANY] * n, out_specs=[ANY] * n,
        out_shape=[jax.ShapeDtypeStruct((g.shape[0], g.shape[1] // 2, g.shape[2]), g.dtype) for g in grads],
        scratch_shapes=[pltpu.SemaphoreType.DMA((n,)), pltpu.SemaphoreType.DMA((n,))],
    )(*grads)


def _to_owner_carry(parts):
    n = len(parts)

    def sends(p_refs, l_refs, send_sems, recv_sems):
        x, y, c, chips = _place()
        me = _chip_index((x, y))
        return [pltpu.make_async_remote_copy(
            src_ref=p_refs[a].at[_chip_index(chip)], dst_ref=l_refs[a].at[me],
            send_sem=send_sems.at[3 * a + k], recv_sem=recv_sems.at[3 * a + k],
            device_id=(chip[0], chip[1], c), device_id_type=MESH) for a in range(n) for k, chip in enumerate(chips)]

    def start(ins, outs, sems):
        for cp in sends(ins, outs, *sems):
            cp.start()

    def finish(ins, outs, sems):
        x, y, c, chips = _place()
        send_sems, recv_sems = sems
        for a in range(n):
            for k, chip in enumerate(chips):
                slot = outs[a].at[_chip_index(chip)]
                pltpu.make_async_remote_copy(
                    src_ref=slot, dst_ref=slot, send_sem=send_sems.at[3 * a + k], recv_sem=recv_sems.at[3 * a + k],
                    device_id=(chip[0], chip[1], c), device_id_type=MESH).wait_recv()
        for cp in sends(ins, outs, *sems):
            cp.wait_send()

    return _Carry(
        ins=list(parts), outs=[jax.ShapeDtypeStruct(p.shape, p.dtype) for p in parts], aliases={},
        sems=[pltpu.SemaphoreType.DMA((3 * n,)), pltpu.SemaphoreType.DMA((3 * n,))],
        phases=[("first", start), ("last", finish)])


def _swap_halves(halves):
    n = len(halves)

    def body(*refs):
        h_refs, o_refs = refs[:n], refs[n:2 * n]
        send_sems, recv_sems = refs[2 * n:]
        x, y, c, _ = _place()
        copies = []
        for a in range(n):
            cp = pltpu.make_async_remote_copy(
                src_ref=h_refs[a], dst_ref=o_refs[a], send_sem=send_sems.at[a], recv_sem=recv_sems.at[a],
                device_id=(x, y, 1 - c), device_id_type=MESH)
            cp.start()
            copies.append(cp)
        for cp in copies:
            cp.wait()

    return pl.pallas_call(
        body, name="grad_swap_halves", in_specs=[ANY] * n, out_specs=[ANY] * n,
        out_shape=[jax.ShapeDtypeStruct(h.shape, h.dtype) for h in halves],
        scratch_shapes=[pltpu.SemaphoreType.DMA((n,)), pltpu.SemaphoreType.DMA((n,))],
    )(*halves)


def _all_devices(name, block):
    r, cols = block.shape

    def body(b_ref, all_ref, sum_ref, send_sems, recv_sems):
        x, y, c, _ = _place()
        me = 4 * x + 2 * y + c
        all_ref[me] = b_ref[...]
        flips = [(fx, fy, fc) for fx in (0, 1) for fy in (0, 1) for fc in (0, 1)][1:]
        copies = []
        for k, (fx, fy, fc) in enumerate(flips):
            cp = pltpu.make_async_remote_copy(
                src_ref=b_ref, dst_ref=all_ref.at[me], send_sem=send_sems.at[k], recv_sem=recv_sems.at[k],
                device_id=(x ^ fx, y ^ fy, c ^ fc), device_id_type=MESH)
            cp.start()
            copies.append(cp)
        for k, (fx, fy, fc) in enumerate(flips):
            slot = all_ref.at[4 * (x ^ fx) + 2 * (y ^ fy) + (c ^ fc)]
            pltpu.make_async_remote_copy(
                src_ref=slot, dst_ref=slot, send_sem=send_sems.at[k], recv_sem=recv_sems.at[k],
                device_id=(x ^ fx, y ^ fy, c ^ fc), device_id_type=MESH).wait_recv()
        for cp in copies:
            cp.wait_send()
        acc = all_ref[0]
        for d in range(1, N_DEV):
            acc = acc + all_ref[d]
        sum_ref[...] = acc

    vmem = pl.BlockSpec(memory_space=pltpu.VMEM)
    return pl.pallas_call(
        body, name=name, in_specs=[vmem], out_specs=[vmem, vmem],
        out_shape=[jax.ShapeDtypeStruct((N_DEV, r, cols), F32), jax.ShapeDtypeStruct((r, cols), F32)],
        scratch_shapes=[pltpu.SemaphoreType.DMA((N_DEV - 1,)), pltpu.SemaphoreType.DMA((N_DEV - 1,))],
    )(block)


PACK = 1024


def _packed_rows(shape, width):
    size, last = int(np.prod(shape)), shape[-1]
    cols = last if last <= width else width
    assert size % cols == 0
    return size // cols, cols


def _pack(vals, width=PACK):
    rows = []
    for v in vals:
        n_rows, cols = _packed_rows(v.shape, width)
        rows.append(jnp.pad(v.reshape(n_rows, cols).astype(F32), ((0, 0), (0, width - cols))))
    buf = jnp.concatenate(rows, axis=0)
    return jnp.pad(buf, ((0, (-buf.shape[0]) % 8), (0, 0)))


def _unpack(buf, shapes, width=PACK):
    out, r = [], 0
    for shape in shapes:
        n_rows, cols = _packed_rows(shape, width)
        out.append(buf[r:r + n_rows, :cols].reshape(shape))
        r += n_rows
    return out


FFN_SPLIT = 2


def _ffn_hidden(name, n, wg, wu, tm=1024, carry=None):
    m, k = n.shape
    f = wg.shape[0]
    fb = f // FFN_SPLIT
    tm = _row_tile(m, tm)

    def core(n_ref, wg_ref, wu_ref, a_ref, b_ref, s_ref):
        nv = n_ref[...]
        a = _dot(nv, wg_ref[...], True)
        b = _dot(nv, wu_ref[...], True)
        a_ref[...] = a.astype(a_ref.dtype)
        b_ref[...] = b.astype(b_ref.dtype)
        s_ref[...] = (a * jax.nn.sigmoid(a) * b).astype(s_ref.dtype)

    w_spec = pl.BlockSpec((fb, k), lambda j, i: (j, 0))
    out_spec = pl.BlockSpec((tm, fb), lambda j, i: (i, j))
    return _call(name, core, (FFN_SPLIT, m // tm), [pl.BlockSpec((tm, k), lambda j, i: (i, 0)), w_spec, w_spec],
                 [out_spec] * 3, [jax.ShapeDtypeStruct((m, f), BF16)] * 3, [], [n, wg, wu], carry)


def _ffn_d_hidden(name, df, wd, a, b, tm=512):
    m, k = df.shape
    f = wd.shape[0]
    fb = f // FFN_SPLIT
    tm = _row_tile(m, tm)

    def body(df_ref, wd_ref, a_ref, b_ref, da_ref, db_ref):
        dfv = df_ref[...]
        for j in range(FFN_SPLIT):
            cols = slice(j * fb, (j + 1) * fb)
            ds = _dot(dfv, wd_ref[cols, :], True)
            av, bv = a_ref[:, cols].astype(F32), b_ref[:, cols].astype(F32)
            sig = jax.nn.sigmoid(av)
            da_ref[:, cols] = (ds * bv * sig * (1.0 + av * (1.0 - sig))).astype(da_ref.dtype)
            db_ref[:, cols] = (ds * av * sig).astype(db_ref.dtype)

    row = pl.BlockSpec((tm, f), lambda i: (i, 0))
    return pl.pallas_call(
        body, name=name, grid=(m // tm,),
        in_specs=[pl.BlockSpec((tm, k), lambda i: (i, 0)), _resident(wd), row, row],
        out_specs=[row, row], out_shape=[jax.ShapeDtypeStruct((m, f), BF16)] * 2,
        compiler_params=_cparams(1),
    )(df, wd, a, b)


def kernel(x, ffn1_norm_pre, ffn1_w_gate, ffn1_w_up, ffn1_w_down, ffn1_norm_post, mix_norm_pre, w_in, gate_bias, rel_table, w_attn_out, conv_glu_bias, conv_dw_w, conv_dw_b, conv_ln_g, conv_ln_b, conv_w_out, w_out, mix_norm_post, ffn2_norm_pre, ffn2_w_gate, ffn2_w_up, ffn2_w_down, ffn2_norm_post, loss_target, m_ffn1_norm_pre, m_ffn1_w_gate, m_ffn1_w_up, m_ffn1_w_down, m_ffn1_norm_post, m_mix_norm_pre, m_w_in, m_gate_bias, m_rel_table, m_w_attn_out, m_conv_glu_bias, m_conv_dw_w, m_conv_dw_b, m_conv_ln_g, m_conv_ln_b, m_conv_w_out, m_w_out, m_mix_norm_post, m_ffn2_norm_pre, m_ffn2_w_gate, m_ffn2_w_up, m_ffn2_w_down, m_ffn2_norm_post, v_ffn1_norm_pre, v_ffn1_w_gate, v_ffn1_w_up, v_ffn1_w_down, v_ffn1_norm_post, v_mix_norm_pre, v_w_in, v_gate_bias, v_rel_table, v_w_attn_out, v_conv_glu_bias, v_conv_dw_w, v_conv_dw_b, v_conv_ln_g, v_conv_ln_b, v_conv_w_out, v_w_out, v_mix_norm_post, v_ffn2_norm_pre, v_ffn2_w_gate, v_ffn2_w_up, v_ffn2_w_down, v_ffn2_norm_post):
    args = dict(locals())
    names = ['ffn1_norm_pre', 'ffn1_w_gate', 'ffn1_w_up', 'ffn1_w_down', 'ffn1_norm_post', 'mix_norm_pre', 'w_in',
             'gate_bias', 'rel_table', 'w_attn_out', 'conv_glu_bias', 'conv_dw_w', 'conv_dw_b', 'conv_ln_g',
             'conv_ln_b', 'conv_w_out', 'w_out', 'mix_norm_post', 'ffn2_norm_pre', 'ffn2_w_gate', 'ffn2_w_up',
             'ffn2_w_down', 'ffn2_norm_post']
    big = ['ffn1_w_gate', 'ffn1_w_up', 'ffn1_w_down', 'w_in', 'w_attn_out', 'conv_w_out', 'w_out', 'ffn2_w_gate',
           'ffn2_w_up', 'ffn2_w_down']
    small = [n for n in names if n not in big]

    xs, target = x[0], loss_target[0]
    t, d = xs.shape
    cx, cy = lax.axis_index("x"), lax.axis_index("y")
    chip = 2 * cx + cy

    dw_shard = conv_dw_w[0, :, 0, :]
    cshard = dw_shard.shape[1]
    dw_all, _ = _all_devices("gather_dw", _pack([dw_shard], width=cshard))
    dw_full = jnp.concatenate([dw_all[2 * j, :CONV_WIDTH, :cshard] for j in range(N_CHIPS)], axis=1)
    dw_full = jnp.pad(dw_full, ((0, CONV_HALO - CONV_WIDTH), (0, 0)))
    peers = [(1 - cx, cy), (cx, 1 - cy), (1 - cx, 1 - cy)]
    pos = jnp.stack([lax.axis_index("c"), chip] + [_chip_index(p) for p in peers]).astype(jnp.int32)
    transposed = ("ffn1_w_gate", "ffn1_w_up", "ffn2_w_gate", "ffn2_w_up")
    weight_of = lambda n: n[2:] if n[:2] in ("m_", "v_") else n
    shard = lambda n: jnp.transpose(args[n][0]) if weight_of(n) in transposed else args[n][0]
    unshard = lambda n, v: (jnp.transpose(v) if n in transposed else v)[None]
    own = {n: _cast_into("cast_" + n, pos, shard(n)) for n in big}
    gather = lambda *ns: _gather_carry([own[n] for n in ns])
    res_spec = [(d, F32), (d, F32), (d, BF16)]
    whole = lambda w: w.reshape(-1, w.shape[-1])

    n1, (wg1, wu1) = _rms_fwd("ffn1_pre", xs, ffn1_norm_pre, gather("ffn1_w_gate", "ffn1_w_up"))
    (a1, b1, s1), (wd1, win, wao, wco, wout) = _ffn_hidden(
        "ffn1_hidden", n1, whole(wg1), whole(wu1),
        carry=gather("ffn1_w_down", "w_in", "w_attn_out", "conv_w_out", "w_out"))
    (f1, h1, u), (wg2,) = _mm_kblk(
        "ffn1_down", [(s1, whole(wd1)[None])], trans_w=False, epilogue=_ep_post_res_pre(0.5), rows=[xs],
        vecs=[ffn1_norm_post, mix_norm_pre], row_outs=res_spec, carry=gather("ffn2_w_gate"))
    proj, (wu2, wd2) = _mm_nblk("mix_in", u, win, trans_w=False, out_blocked=False, out_dtype=BF16,
                                carry=gather("ffn2_w_up", "ffn2_w_down"))
    table_pad = jnp.pad(rel_table[0], ((0, 0), (0, REL_PAD - rel_table.shape[2])))
    bias = _bias_expand(table_pad)
    att, lse = _attn_fwd(proj, bias)
    cs, c_glu, z_conv = _conv_fwd(proj, conv_glu_bias, dw_full, conv_dw_b, conv_ln_g, conv_ln_b)
    y_a, y_b, merged = _mix_merge(att, cs, wao, wco, proj, gate_bias)
    (mo, h2, n2), _ = _mm_kblk(
        "mix_out", [(merged, wout)], trans_w=False, epilogue=_ep_post_res_pre(1.0), rows=[h1],
        vecs=[mix_norm_post, ffn2_norm_pre], row_outs=res_spec)
    (a2, b2, s2), _ = _ffn_hidden("ffn2_hidden", n2, whole(wg2), whole(wu2))
    g = {}
    (dy, df2, err2, g["ffn2_norm_post"]), _ = _mm_kblk(
        "ffn2_down", [(s2, whole(wd2)[None])], trans_w=False, epilogue=_ep_loss(0.5, d), rows=[h2, target],
        vecs=[ffn2_norm_post], row_outs=[(d, F32), (d, BF16)], vec_outs=[d, d])
    loss = lax.psum(0.5 * jnp.sum(err2) / d, ("x", "y", "c"))

    parts, landed = {}, {}

    def ffn_bwd(tag, df, n, a, b, s, wg, wu, wd, **epilogue):
        da, db = _ffn_d_hidden(tag + "_d_hidden", df, whole(wd), a, b)
        group = [tag + "_w_down", tag + "_w_gate", tag + "_w_up"]
        local = [_mm_tn_wide(tag + "_g_" + what, hidden, other, d, a_split=FFN_SPLIT).reshape(wd.shape)
                 for what, hidden, other in (("down", s, df), ("gate", da, n), ("up", db, n))]
        return _mm_kblk(tag + "_d_n", [(da, whole(wg)[None]), (db, whole(wu)[None])], trans_w=False,
                        carry=pair_sums(tag, group, local), **epilogue), group

    def pair_sums(tag, group, local):
        theirs = _pair_exchange("pair_" + tag, local)
        for n, mine, other in zip(group, local, theirs):
            parts[n] = _add_pair("pair_sum_" + n, pos, mine, other)
        return _to_owner_carry([parts[n] for n in group])

    def keep(group, carried):
        for n, val in zip(group, carried):
            landed[n] = val

    ((dh2, dmo, g["ffn2_norm_pre"], g["mix_norm_post"]), carried), group = ffn_bwd(
        "ffn2", df2, n2, a2, b2, s2, wg2, wu2, wd2, epilogue=_ep_pre_bwd_post(1.0), rows=[h2, dy, mo],
        vecs=[ffn2_norm_pre, mix_norm_post], row_outs=[(d, F32), (d, BF16)], vec_outs=[d, d])
    keep(group, carried)
    g_wout = _mm_tn("mix_g_out", merged, "col", dmo, "full")
    dy_a, dy_b, dgates, datt, dcs, g["gate_bias"] = _mix_d_merge(dmo, wout, y_a, y_b, wao, wco, proj, gate_bias)
    g_wao = _mm_tn("attn_g_out", att, "full", dy_a, "col")
    g_wco = _mm_tn("conv_g_out", cs, "full", dy_b, "col")
    dq, dk, dv, dbias = _attn_bwd(proj, bias, att, lse, datt)
    g["rel_table"] = _bias_fold(dbias)[:, :rel_table.shape[2]]
    dcin, g_dw, g["conv_dw_b"], g["conv_ln_g"], g["conv_ln_b"], g["conv_glu_bias"] = _conv_bwd(
        proj, c_glu, z_conv, dcs, conv_glu_bias, dw_full, conv_ln_g, conv_ln_b)
    pieces = [("q", dq), ("k", dk), ("v", dv), ("conv", dcin), ("gates", dgates)]
    n_in = win.shape[0] * win.shape[2]
    win_cols = jnp.transpose(jnp.transpose(win, (1, 0, 2)).reshape(d, n_in // COL, COL), (1, 0, 2))
    g_cols = jnp.concatenate([_mm_tn_wide("mix_g_in_" + tag, u, piece, COL) for tag, piece in pieces], axis=0)
    g_win = jnp.transpose(jnp.transpose(g_cols, (1, 0, 2)).reshape(d, win.shape[0], win.shape[2]), (1, 0, 2))
    bounds = np.cumsum([0] + [piece.shape[1] // COL for _, piece in pieces])
    group = ["w_out", "w_attn_out", "conv_w_out", "w_in"]
    (dh1, df1, g["mix_norm_pre"], g["ffn1_norm_post"]), carried = _mm_kblk(
        "mix_d_in", [(piece, win_cols[lo:hi]) for (_, piece), lo, hi in zip(pieces, bounds[:-1], bounds[1:])],
        trans_w=True, epilogue=_ep_pre_bwd_post(0.5), rows=[h1, dh2, f1],
        vecs=[mix_norm_pre, ffn1_norm_post], row_outs=[(d, F32), (d, BF16)], vec_outs=[d, d],
        carry=pair_sums("mix", group, [g_wout, g_wao, g_wco, g_win]))
    keep(group, carried)
    ((grad_x, g["ffn1_norm_pre"]), carried), group = ffn_bwd(
        "ffn1", df1, n1, a1, b1, s1, wg1, wu1, wd1, epilogue=_ep_pre_bwd_first(), rows=[xs, dh1],
        vecs=[ffn1_norm_pre], row_outs=[(d, F32)], vec_outs=[d])
    keep(group, carried)

    halves = [_add_chips("chip_sum_" + n, pos, parts[n], landed[n]) for n in big]
    other_halves = _swap_halves(halves)

    g["conv_dw_w"] = g_dw[:CONV_WIDTH]
    _, small_sum = _all_devices("sum_small", _pack([g[n] for n in small]))
    for n, val in zip(small, _unpack(small_sum, [g[n].shape for n in small])):
        g[n] = val
    g["conv_dw_w"] = lax.dynamic_slice_in_dim(g["conv_dw_w"], chip * cshard, cshard, axis=1)

    grads, deltas, new_m, new_v = {}, {}, {}, {}
    for n, mine, other in zip(big, halves, other_halves):
        gr, dl, m2, v2 = _adamw_halves("adamw_" + n, pos, shard(n), shard("m_" + n), shard("v_" + n), mine, other)
        grads[n], deltas[n], new_m[n], new_v[n] = unshard(n, gr), unshard(n, dl), unshard(n, m2), unshard(n, v2)
    shapes = [g[n].shape for n in small]
    packed = lambda pre: _pack([args[pre + n].reshape(shp) for n, shp in zip(small, shapes)])
    dl, m2, v2 = _adamw("adamw_small", packed(""), _pack([g[n] for n in small]), packed("m_"), packed("v_"))
    for n, a_, b_, c_ in zip(small, _unpack(dl, shapes), _unpack(m2, shapes), _unpack(v2, shapes)):
        shape = args[n].shape
        grads[n], deltas[n], new_m[n], new_v[n] = (g[n].reshape(shape), a_.reshape(shape), b_.reshape(shape),
                                                   c_.reshape(shape))

    return (loss, grad_x[None], *[grads[n] for n in names], *[deltas[n] for n in names],
            *[new_m[n] for n in names], *[new_v[n] for n in names])
```

```python
import functools

import numpy as np
import jax
import jax.numpy as jnp
from jax import lax
from jax.experimental import pallas as pl
from jax.experimental.pallas import tpu as pltpu

F32 = jnp.float32
BF16 = jnp.bfloat16
MESH = pl.DeviceIdType.MESH
ANY = pl.BlockSpec(memory_space=pl.ANY)

EPS = 1e-6
CHUNK = 64
LEFT_CHUNKS = 8
N_HEADS = 8
HEAD_DIM = 64
D_ATTN = N_HEADS * HEAD_DIM
D_CONV = 512
CONV_WIDTH = 31
REL_CLIP = 128
N_CHIPS = 4
N_DEV = 8
Q_BLOCK = 4 * CHUNK
K_PAD = LEFT_CHUNKS * CHUNK
K_WIN = K_PAD + Q_BLOCK
REL_EXT = 1024
REL_PAD = 384
CONV_HALO = 32
CONV_TILE = 256
COL = 512
NEG = -1e30

ADAM_LR = 0.001
ADAM_B1 = 0.9
ADAM_B2 = 0.999
ADAM_EPS = 1e-08
ADAM_WD = 0.01
ADAM_STEP = 10

VMEM_LIMIT_BYTES = 56 * 1024 * 1024


def _cparams(n_grid):
    return pltpu.CompilerParams(dimension_semantics=("arbitrary",) * n_grid, vmem_limit_bytes=VMEM_LIMIT_BYTES)


def _row_tile(rows, want):
    if rows <= want:
        return rows
    for t in range(want - want % 16, 0, -16):
        if rows % t == 0:
            return t
    raise ValueError((rows, want))


def _dot(a, w, trans_w):
    dims = (((1,), (1,)), ((), ())) if trans_w else (((1,), (0,)), ((), ()))
    return lax.dot_general(a, w, dims, preferred_element_type=F32)


class _Carry:
    LATE_STEPS = 2

    def __init__(self, ins, outs, aliases, sems, phases):
        self.ins, self.outs, self.aliases, self.sems, self.phases = ins, outs, aliases, sems, phases


def _call(name, core, grid, in_specs, out_specs, out_shape, scratch, args, carry=None):
    n_in, n_out, n_scr = len(in_specs), len(out_specs), len(scratch)
    if carry is None:
        out = pl.pallas_call(core, name=name, grid=grid, in_specs=in_specs, out_specs=out_specs, out_shape=out_shape,
                             scratch_shapes=scratch, compiler_params=_cparams(len(grid)))(*args)
        return list(out), []
    c_in, c_out = len(carry.ins), len(carry.outs)
    total = int(np.prod(grid))
    late = max(total - 1 - _Carry.LATE_STEPS, 0)

    def body(*refs):
        ins, refs = refs[:n_in], refs[n_in:]
        c_ins, refs = refs[:c_in], refs[c_in:]
        outs, refs = refs[:n_out], refs[n_out:]
        c_outs, refs = refs[:c_out], refs[c_out:]
        scr, c_sems = refs[:n_scr], refs[n_scr:]
        step = pl.program_id(0)
        for axis in range(1, len(grid)):
            step = step * grid[axis] + pl.program_id(axis)

        def run(when, at):
            for w, fn in carry.phases:
                if w == when:
                    pl.when(step == at)(functools.partial(fn, c_ins, c_outs, c_sems))

        run("first", 0)
        core(*ins, *outs, *scr)
        run("late", late)
        run("last", total - 1)

    out = pl.pallas_call(
        body, name=name, grid=grid, in_specs=list(in_specs) + [ANY] * c_in, out_specs=list(out_specs) + [ANY] * c_out,
        out_shape=list(out_shape) + list(carry.outs), scratch_shapes=list(scratch) + list(carry.sems),
        input_output_aliases={n_in + a: n_out + b for a, b in carry.aliases.items()},
        compiler_params=_cparams(len(grid)),
    )(*args, *carry.ins)
    return list(out[:n_out]), list(out[n_out:])


def _mm_nblk(name, a, w, *, trans_w, out_blocked, out_dtype, tm=1024, carry=None):
    m, k = a.shape
    nj = w.shape[0]
    nb = w.shape[1] if trans_w else w.shape[2]
    tm = _row_tile(m, tm)

    def core(a_ref, w_ref, o_ref):
        o_ref[...] = _dot(a_ref[...], w_ref[...], trans_w).astype(o_ref.dtype)

    if out_blocked:
        out_shape, out_spec = (nj, m, nb), pl.BlockSpec((None, tm, nb), lambda j, i: (j, i, 0))
    else:
        out_shape, out_spec = (m, nj * nb), pl.BlockSpec((tm, nb), lambda j, i: (i, j))
    out, carried = _call(
        name, core, (nj, m // tm),
        [pl.BlockSpec((tm, k), lambda j, i: (i, 0)), pl.BlockSpec((None,) + w.shape[1:], lambda j, i: (j, 0, 0))],
        [out_spec], [jax.ShapeDtypeStruct(out_shape, out_dtype)], [], [a, w], carry)
    return out[0] if carry is None else (out[0], carried)


def _mm_kblk(name, pairs, *, trans_w, out_dtype=F32, tm=512, sub=256, epilogue=None, rows=(), vecs=(), row_outs=None,
             vec_outs=(), carry=None):
    w0 = pairs[0][1]
    n = w0.shape[1] if trans_w else w0.shape[2]
    blocks = [(w.shape[0], w.shape[2] if trans_w else w.shape[1]) for _, w in pairs]
    m = pairs[0][0].shape[-2]
    tm = _row_tile(m, tm)
    ts = _row_tile(tm, sub)
    n_pairs, n_rows, n_vecs = len(pairs), len(rows), len(vecs)
    if epilogue is None:
        epilogue, row_outs = (lambda acc, r, v: ([acc], [])), [(n, out_dtype)]
    n_ro, n_vo = len(row_outs), len(vec_outs)

    def core(*refs):
        pair_refs, refs = refs[:2 * n_pairs], refs[2 * n_pairs:]
        row_refs, refs = refs[:n_rows], refs[n_rows:]
        vec_refs, refs = refs[:n_vecs], refs[n_vecs:]
        ro_refs, vo_refs = refs[:n_ro], refs[n_ro:]
        if n_vo:
            @pl.when(pl.program_id(0) == 0)
            def _():
                for ref in vo_refs:
                    ref[...] = jnp.zeros_like(ref)

        vec_vals = [v[...] for v in vec_refs]
        sums = None
        for r0 in range(0, tm, ts):
            sub_rows = slice(r0, r0 + ts)
            acc = None
            for p in range(n_pairs):
                a_ref, w_ref = pair_refs[2 * p], pair_refs[2 * p + 1]
                nj, kb = blocks[p]
                for j in range(nj):
                    a_blk = a_ref[j, sub_rows, :] if len(a_ref.shape) == 3 else a_ref[sub_rows, j * kb:(j + 1) * kb]
                    part = _dot(a_blk, w_ref[j], trans_w)
                    acc = part if acc is None else acc + part
            ro, vo = epilogue(acc, [r[sub_rows, :].astype(F32) for r in row_refs], vec_vals)
            for ref, val in zip(ro_refs, ro):
                ref[sub_rows, :] = val.astype(ref.dtype)
            sums = vo if sums is None else [s + v for s, v in zip(sums, vo)]
        for ref, val in zip(vo_refs, sums or []):
            ref[...] += val

    in_specs, args = [], []
    for (a, w), (nj, kb) in zip(pairs, blocks):
        if a.ndim == 3:
            in_specs.append(pl.BlockSpec((nj, tm, kb), lambda i: (0, i, 0)))
        else:
            in_specs.append(pl.BlockSpec((tm, nj * kb), lambda i: (i, 0)))
        in_specs.append(pl.BlockSpec(w.shape, lambda i: (0, 0, 0), pipeline_mode=pl.Buffered(1)))
        args += [a, w]
    in_specs += [pl.BlockSpec((tm, r.shape[1]), lambda i: (i, 0)) for r in rows]
    in_specs += [pl.BlockSpec(v.shape, lambda i: (0, 0)) for v in vecs]
    out_specs = [pl.BlockSpec((tm, cols), lambda i: (i, 0)) for cols, _ in row_outs]
    out_specs += [pl.BlockSpec((1, cols), lambda i: (0, 0)) for cols in vec_outs]
    out_shape = [jax.ShapeDtypeStruct((m, cols), dt) for cols, dt in row_outs]
    out_shape += [jax.ShapeDtypeStruct((1, cols), F32) for cols in vec_outs]
    return _call(name, core, (m // tm,), in_specs, out_specs, out_shape, [], args + list(rows) + list(vecs), carry)


def _mm_tn(name, a, a_mode, b, b_mode, *, out_dtype=BF16, tt=2048):
    nj = N_CHIPS
    t = a.shape[-2]
    tt = _row_tile(t, tt)

    def spec(x, mode):
        if mode == "full":
            return x.shape[1], pl.BlockSpec((tt, x.shape[1]), lambda j, s: (s, 0))
        if mode == "col":
            cb = x.shape[1] // nj
            return cb, pl.BlockSpec((tt, cb), lambda j, s: (s, j))
        return x.shape[2], pl.BlockSpec((None, tt, x.shape[2]), lambda j, s: (j, s, 0))

    ca, a_spec = spec(a, a_mode)
    cb, b_spec = spec(b, b_mode)
    n_steps = t // tt

    def body(a_ref, b_ref, o_ref, acc_ref):
        s = pl.program_id(1)

        @pl.when(s == 0)
        def _():
            acc_ref[...] = jnp.zeros_like(acc_ref)

        acc_ref[...] += lax.dot_general(a_ref[...], b_ref[...], (((0,), (0,)), ((), ())),
                                        preferred_element_type=F32)

        @pl.when(s == n_steps - 1)
        def _():
            o_ref[...] = acc_ref[...].astype(o_ref.dtype)

    return pl.pallas_call(
        body, name=name, grid=(nj, n_steps), in_specs=[a_spec, b_spec],
        out_specs=pl.BlockSpec((None, ca, cb), lambda j, s: (j, 0, 0)),
        out_shape=jax.ShapeDtypeStruct((nj, ca, cb), out_dtype),
        scratch_shapes=[pltpu.VMEM((ca, cb), F32)], compiler_params=_cparams(2),
    )(a, b)


def _mm_tn_wide(name, a, b, cb, *, a_split=1, out_dtype=BF16, tt=1024):
    t, ca = a.shape
    nb = b.shape[1] // cb
    tt = _row_tile(t, tt)
    n_steps = t // tt
    piece = ca // a_split

    def body(a_ref, b_ref, o_ref, acc_ref):
        s = pl.program_id(0)

        @pl.when(s == 0)
        def _():
            acc_ref[...] = jnp.zeros_like(acc_ref)

        for j in range(nb):
            bv = b_ref[:, j * cb:(j + 1) * cb]
            for c in range(a_split):
                rows = slice(c * piece, (c + 1) * piece)
                acc_ref[j, rows, :] += lax.dot_general(a_ref[:, rows], bv, (((0,), (0,)), ((), ())),
                                                       preferred_element_type=F32)

        @pl.when(s == n_steps - 1)
        def _():
            o_ref[...] = acc_ref[...].astype(o_ref.dtype)

    return pl.pallas_call(
        body, name=name, grid=(n_steps,),
        in_specs=[pl.BlockSpec((tt, ca), lambda s: (s, 0)), pl.BlockSpec((tt, nb * cb), lambda s: (s, 0))],
        out_specs=pl.BlockSpec((nb, ca, cb), lambda s: (0, 0, 0)),
        out_shape=jax.ShapeDtypeStruct((nb, ca, cb), out_dtype),
        scratch_shapes=[pltpu.VMEM((nb, ca, cb), F32)], compiler_params=_cparams(1),
    )(a, b)


def _rowwise(name, fn, rows, vecs, row_outs, vec_outs, *, tm=256, carry=None):
    nrows = rows[0][0].shape[0]
    tm = _row_tile(nrows, tm)
    n_r, n_v, n_ro, n_vo = len(rows), len(vecs), len(row_outs), len(vec_outs)

    def body(*refs):
        r_vals = [r[...] for r in refs[:n_r]]
        v_vals = [r[...] for r in refs[n_r:n_r + n_v]]
        ro_refs = refs[n_r + n_v:n_r + n_v + n_ro]
        vo_refs = refs[n_r + n_v + n_ro:]
        ro, vo = fn(r_vals, v_vals)
        for ref, val in zip(ro_refs, ro):
            ref[...] = val.astype(ref.dtype)
        if n_vo:
            @pl.when(pl.program_id(0) == 0)
            def _():
                for ref in vo_refs:
                    ref[...] = jnp.zeros_like(ref)

            for ref, val in zip(vo_refs, vo):
                ref[...] += val

    in_specs = [pl.BlockSpec((tm, cols), functools.partial(lambda i, cb: (i, cb), cb=cb)) for _, cols, cb in rows]
    in_specs += [pl.BlockSpec(v.shape, functools.partial(lambda i, nd: (0,) * nd, nd=v.ndim)) for v in vecs]
    out_specs = [pl.BlockSpec((tm, cols), lambda i: (i, 0)) for cols, _ in row_outs]
    out_specs += [pl.BlockSpec((1, cols), lambda i: (0, 0)) for cols in vec_outs]
    out_shape = [jax.ShapeDtypeStruct((nrows, cols), dt) for cols, dt in row_outs]
    out_shape += [jax.ShapeDtypeStruct((1, cols), F32) for cols in vec_outs]
    out, carried = _call(name, body, (nrows // tm,), in_specs, out_specs, out_shape, [],
                         [r[0] for r in rows] + list(vecs), carry)
    return out if carry is None else (out, carried)


def _whole(x):
    return (x, x.shape[1], 0)


def _colsum(x):
    return jnp.sum(x, axis=0, keepdims=True)


def _rstd(x):
    return lax.rsqrt(jnp.mean(x * x, axis=-1, keepdims=True) + EPS)


def _rms_bwd(dn, x, g):
    r = _rstd(x)
    c = dn * g
    dx = r * c - x * (r * r * r) * jnp.mean(c * x, axis=-1, keepdims=True)
    return dx, _colsum(dn * x * r)


def _rms_fwd(name, x, g, carry):
    def fn(r, v):
        (xv,), (gv,) = r, v
        return [xv * _rstd(xv) * gv], []

    (n,), carried = _rowwise(name, fn, [_whole(x)], [g], [(x.shape[1], BF16)], [], carry=carry)
    return n, carried


def _ep_post_res_pre(scale):
    def epilogue(acc, rows, vecs):
        (resid,), (g_post, g_next) = rows, vecs
        h = resid + scale * (acc * _rstd(acc) * g_post)
        return [acc, h, h * _rstd(h) * g_next], []

    return epilogue


def _post_bwd(dh, f, g_post, scale):
    return _rms_bwd(scale * dh, f, g_post)


def _ep_loss(scale, d):
    def epilogue(acc, rows, vecs):
        (resid, target), (g_post,) = rows, vecs
        err = resid + scale * (acc * _rstd(acc) * g_post) - target
        dy = err * (1.0 / d)
        df, dg_post = _post_bwd(dy, acc, g_post, scale)
        return [dy, df], [_colsum(err * err), dg_post]

    return epilogue


def _ep_pre_bwd_post(scale_prev):
    def epilogue(acc, rows, vecs):
        (h, dh_up, f_prev), (g_pre, g_post_prev) = rows, vecs
        dx, dg_pre = _rms_bwd(acc, h, g_pre)
        dh = dh_up + dx
        df, dg_post = _post_bwd(dh, f_prev, g_post_prev, scale_prev)
        return [dh, df], [dg_pre, dg_post]

    return epilogue


def _ep_pre_bwd_first():
    def epilogue(acc, rows, vecs):
        (x, dh_up), (g_pre,) = rows, vecs
        dx, dg_pre = _rms_bwd(acc, x, g_pre)
        return [dh_up + dx], [dg_pre]

    return epilogue


def _gate_specs(d, tm):
    first = (3 * D_ATTN + 2 * D_CONV) // COL
    return [pl.BlockSpec((tm, COL), functools.partial(lambda i, cb: (i, cb), cb=first + p)) for p in range(2 * d // COL)]


def _gate(piece_refs, bias_ref, c0, width):
    p, off = divmod(c0, COL)
    return jax.nn.sigmoid(piece_refs[p][:, off:off + width].astype(F32) + bias_ref[:, c0:c0 + width])


def _resident(w):
    return pl.BlockSpec(w.shape, functools.partial(lambda i, nd: (0,) * nd, nd=w.ndim), pipeline_mode=pl.Buffered(1))


def _mix_merge(att, cs, wao, wco, proj, gate_bias, tm=512):
    t = att.shape[0]
    nj, _, nb = wao.shape
    d = nj * nb
    tm = _row_tile(t, tm)
    gate_specs = _gate_specs(d, tm)
    n_p = len(gate_specs)

    def body(att_ref, cs_ref, wao_ref, wco_ref, *rest):
        pieces, (gb_ref, ya_ref, yb_ref, m_ref) = rest[:n_p], rest[n_p:]
        av, cv = att_ref[...], cs_ref[...]
        for j in range(nj):
            cols = slice(j * nb, (j + 1) * nb)
            ya = _dot(av, wao_ref[j], False)
            yb = _dot(cv, wco_ref[j], False)
            merged = _gate(pieces, gb_ref, j * nb, nb) * ya + _gate(pieces, gb_ref, d + j * nb, nb) * yb
            ya_ref[:, cols] = ya.astype(ya_ref.dtype)
            yb_ref[:, cols] = yb.astype(yb_ref.dtype)
            m_ref[:, cols] = merged.astype(m_ref.dtype)

    row = lambda x: pl.BlockSpec((tm, x.shape[1]), lambda i: (i, 0))
    out_spec = pl.BlockSpec((tm, d), lambda i: (i, 0))
    return pl.pallas_call(
        body, name="mix_merge", grid=(t // tm,),
        in_specs=[row(att), row(cs), _resident(wao), _resident(wco)] + gate_specs + [_resident(gate_bias)],
        out_specs=[out_spec] * 3, out_shape=[jax.ShapeDtypeStruct((t, d), BF16)] * 3, compiler_params=_cparams(1),
    )(att, cs, wao, wco, *([proj] * n_p), gate_bias)


def _mix_d_merge(dmo, wout, y_a, y_b, wao, wco, proj, gate_bias, tm=512):
    t, d = dmo.shape
    nj, _, nb = wao.shape
    ka, kc = wao.shape[1], wco.shape[1]
    tm = _row_tile(t, tm)
    gate_specs = _gate_specs(d, tm)
    n_p = len(gate_specs)

    def body(dmo_ref, wout_ref, ya_ref, yb_ref, wao_ref, wco_ref, *rest):
        pieces, (gb_ref, dya_ref, dyb_ref, dg_ref, datt_ref, dcs_ref, dgb_ref) = rest[:n_p], rest[n_p:]

        @pl.when(pl.program_id(0) == 0)
        def _():
            dgb_ref[...] = jnp.zeros_like(dgb_ref)

        dmo_v = dmo_ref[...]
        datt = dcs = None
        for j in range(nj):
            cols, cols_b = slice(j * nb, (j + 1) * nb), slice(d + j * nb, d + (j + 1) * nb)
            dm = _dot(dmo_v, wout_ref[j], True)
            ga, gb = _gate(pieces, gb_ref, j * nb, nb), _gate(pieces, gb_ref, d + j * nb, nb)
            dya, dyb = (dm * ga).astype(BF16), (dm * gb).astype(BF16)
            dga = dm * ya_ref[:, cols].astype(F32) * ga * (1.0 - ga)
            dgb = dm * yb_ref[:, cols].astype(F32) * gb * (1.0 - gb)
            dya_ref[:, cols], dyb_ref[:, cols] = dya, dyb
            dg_ref[:, cols], dg_ref[:, cols_b] = dga.astype(dg_ref.dtype), dgb.astype(dg_ref.dtype)
            dgb_ref[:, cols] += _colsum(dga)
            dgb_ref[:, cols_b] += _colsum(dgb)
            pa, pc = _dot(dya, wao_ref[j], True), _dot(dyb, wco_ref[j], True)
            datt, dcs = (pa, pc) if datt is None else (datt + pa, dcs + pc)
        datt_ref[...] = datt.astype(datt_ref.dtype)
        dcs_ref[...] = dcs.astype(dcs_ref.dtype)

    row = lambda cols: pl.BlockSpec((tm, cols), lambda i: (i, 0))
    return pl.pallas_call(
        body, name="mix_d_merge", grid=(t // tm,),
        in_specs=[row(d), _resident(wout), row(d), row(d), _resident(wao), _resident(wco)] + gate_specs
        + [_resident(gate_bias)],
        out_specs=[row(d), row(d), row(2 * d), row(ka), row(kc), pl.BlockSpec((1, 2 * d), lambda i: (0, 0))],
        out_shape=[jax.ShapeDtypeStruct((t, d), BF16), jax.ShapeDtypeStruct((t, d), BF16),
                   jax.ShapeDtypeStruct((t, 2 * d), BF16), jax.ShapeDtypeStruct((t, ka), BF16),
                   jax.ShapeDtypeStruct((t, kc), F32), jax.ShapeDtypeStruct((1, 2 * d), F32)],
        compiler_params=_cparams(1),
    )(dmo, wout, y_a, y_b, wao, wco, *([proj] * n_p), gate_bias)


def _adamw_math(wv, gv, mv, vv):
    m2 = ADAM_B1 * mv + (1.0 - ADAM_B1) * gv
    v2 = ADAM_B2 * vv + (1.0 - ADAM_B2) * (gv * gv)
    m_hat = m2 / (1.0 - ADAM_B1 ** ADAM_STEP)
    v_hat = v2 / (1.0 - ADAM_B2 ** ADAM_STEP)
    delta = -ADAM_LR * (m_hat / (jnp.sqrt(v_hat) + ADAM_EPS) + ADAM_WD * wv)
    return delta, m2, v2


def _adamw(name, w, g, m, v):
    def fn(r, _):
        return list(_adamw_math(*r)), []

    c = w.shape[1]
    return _rowwise(name, fn, [_whole(w), _whole(g), _whole(m), _whole(v)], [], [(c, F32)] * 3, [], tm=256)


POS_C, POS_CHIP, POS_PEER = 0, 1, 2


def _placed_call(body, name, pos, grid, in_specs, out_specs, out_shape, args):
    return pl.pallas_call(
        body, name=name, out_shape=out_shape, compiler_params=_cparams(len(grid)),
        grid_spec=pltpu.PrefetchScalarGridSpec(num_scalar_prefetch=1, grid=grid, in_specs=in_specs,
                                               out_specs=out_specs),
    )(pos, *args)


def _cast_into(name, pos, w):
    r, cols = w.shape
    tm = _row_tile(r, 1024)

    def body(pos_ref, w_ref, o_ref):
        o_ref[...] = w_ref[...].astype(o_ref.dtype)

    return _placed_call(
        body, name, pos, (r // tm,), [pl.BlockSpec((tm, cols), lambda i, pos: (i, 0))],
        pl.BlockSpec((None, tm, cols), lambda i, pos: (pos[POS_CHIP], i, 0)),
        jax.ShapeDtypeStruct((N_CHIPS, r, cols), BF16), [w])


def _add_pair(name, pos, grad, landed):
    nj, half, cols = landed.shape
    tm = _row_tile(half, 512)
    nb = half // tm

    def body(pos_ref, g_ref, l_ref, o_ref):
        o_ref[...] = (g_ref[...].astype(F32) + l_ref[...].astype(F32)).astype(o_ref.dtype)

    spec = pl.BlockSpec((None, tm, cols), lambda j, i, pos: (j, i, 0))
    return _placed_call(
        body, name, pos, (nj, nb),
        [pl.BlockSpec((None, tm, cols), lambda j, i, pos: (j, pos[POS_C] * nb + i, 0)), spec], spec,
        jax.ShapeDtypeStruct(landed.shape, BF16), [grad, landed])


def _add_chips(name, pos, part, landed):
    _, half, cols = landed.shape
    tm = _row_tile(half, 512)

    def body(pos_ref, p_ref, l0_ref, l1_ref, l2_ref, o_ref):
        acc = p_ref[...].astype(F32)
        for ref in (l0_ref, l1_ref, l2_ref):
            acc = acc + ref[...].astype(F32)
        o_ref[...] = acc

    slot = lambda at: pl.BlockSpec((None, tm, cols), functools.partial(lambda i, pos, at: (pos[at], i, 0), at=at))
    return _placed_call(
        body, name, pos, (half // tm,), [slot(POS_CHIP)] + [slot(POS_PEER + k) for k in range(3)],
        pl.BlockSpec((tm, cols), lambda i, pos: (i, 0)), jax.ShapeDtypeStruct((half, cols), F32),
        [part, landed, landed, landed])


def _adamw_halves(name, pos, w, m, v, own, landed):
    r, cols = w.shape
    half = own.shape[0]
    tm = _row_tile(half, 256)
    nb = half // tm

    def body(pos_ref, w_ref, m_ref, v_ref, own_ref, land_ref, g_out, d_out, m_out, v_out):
        mine = pl.program_id(0) == pos_ref[POS_C]
        g = jnp.where(mine, own_ref[...], land_ref[...])
        delta, m2, v2 = _adamw_math(w_ref[...], g, m_ref[...], v_ref[...])
        g_out[...] = g
        d_out[...] = delta
        m_out[...] = m2
        v_out[...] = v2

    full = pl.BlockSpec((tm, cols), lambda h, i, pos: (h * nb + i, 0))
    part = pl.BlockSpec((tm, cols), lambda h, i, pos: (i, 0))
    return _placed_call(
        body, name, pos, (2, nb), [full, full, full, part, part], [full] * 4,
        [jax.ShapeDtypeStruct((r, cols), F32)] * 4, [w, m, v, own, landed])


N_START = K_PAD // Q_BLOCK


def _rel_onehot(n_q):
    e = np.arange(REL_EXT)
    dist = K_PAD - (e - (n_q - 1))
    idx = np.clip(dist, -REL_CLIP, REL_CLIP) + REL_CLIP
    return (np.arange(REL_PAD)[:, None] == idx[None, :]).astype(np.float32)


def _skew(x, left):
    row = lax.broadcasted_iota(jnp.int32, x.shape, 0)
    for bit in range(x.shape[0].bit_length() - 1):
        amount = 1 << bit
        rolled = pltpu.roll(x, REL_EXT - amount if left else amount, 1)
        x = jnp.where((row >> bit) & 1 == 1, rolled, x)
    return x


def _bias_expand(table_pad):
    onehot = jnp.asarray(_rel_onehot(Q_BLOCK))

    def body(t_ref, oh_ref, o_ref):
        ext = jnp.dot(t_ref[...], oh_ref[...], precision=lax.Precision.HIGHEST, preferred_element_type=F32)
        qc = lax.broadcasted_iota(jnp.int32, (Q_BLOCK, K_WIN), 0) // CHUNK
        kpos = lax.broadcasted_iota(jnp.int32, (Q_BLOCK, K_WIN), 1)
        band = (kpos // CHUNK >= qc) & (kpos // CHUNK <= qc + LEFT_CHUNKS)
        for h in range(N_HEADS):
            rows = jnp.broadcast_to(ext[h:h + 1, :], (Q_BLOCK, REL_EXT))
            rolled = _skew(pltpu.roll(rows, REL_EXT - (Q_BLOCK - 1), 1), left=False)[:, :K_WIN]
            for v in range(N_START + 1):
                o_ref[v, h] = jnp.where(band & (kpos + v * Q_BLOCK >= K_PAD), rolled, NEG)

    return pl.pallas_call(
        body, name="bias_expand", out_shape=jax.ShapeDtypeStruct((N_START + 1, N_HEADS, Q_BLOCK, K_WIN), F32),
        compiler_params=pltpu.CompilerParams(vmem_limit_bytes=VMEM_LIMIT_BYTES),
    )(table_pad, onehot)


def _bias_fold(dbias):
    onehot_t = jnp.asarray(_rel_onehot(CHUNK).T)

    def body(d_ref, oh_ref, o_ref, ext_ref):
        for h in range(N_HEADS):
            x = jnp.concatenate([d_ref[h], jnp.zeros((CHUNK, REL_EXT - K_WIN), F32)], axis=1)
            rolled = _skew(pltpu.roll(x, CHUNK - 1, 1), left=True)
            ext_ref[h:h + 1, :] = jnp.sum(rolled, axis=0, keepdims=True)
        o_ref[...] = jnp.dot(ext_ref[...], oh_ref[...], precision=lax.Precision.HIGHEST,
                             preferred_element_type=F32)

    return pl.pallas_call(
        body, name="bias_fold", out_shape=jax.ShapeDtypeStruct((N_HEADS, REL_PAD), F32),
        scratch_shapes=[pltpu.VMEM((N_HEADS, REL_EXT), F32)],
        compiler_params=pltpu.CompilerParams(vmem_limit_bytes=VMEM_LIMIT_BYTES),
    )(dbias, onehot_t)


def _head_lanes():
    lane = lax.broadcasted_iota(jnp.int32, (1, 2 * HEAD_DIM), 1)
    return [lane < HEAD_DIM, lane >= HEAD_DIM]


def _only(mask, x, scale=None):
    x = jnp.where(mask, x, jnp.zeros_like(x))
    return x if scale is None else x * scale


def _contract_lanes(a, b):
    return lax.dot_general(a, b, (((1,), (1,)), ((), ())), preferred_element_type=F32)


def _contract_rows(a, b):
    return lax.dot_general(a, b, (((0,), (0,)), ((), ())), preferred_element_type=F32)


PAIR = 2 * HEAD_DIM
N_PAIRS = D_ATTN // PAIR


def _attn_specs(pairs):
    width = pairs * PAIR
    per = D_ATTN // width
    row_spec = pl.BlockSpec((Q_BLOCK, width), lambda g, i: (i, g))
    kv_specs = [pl.BlockSpec((Q_BLOCK, width),
                             functools.partial(lambda g, i, kk, c0: (jnp.maximum(i + kk - N_START, 0), c0 + g),
                                               kk=kk, c0=c0))
                for c0 in (per, 2 * per) for kk in range(K_WIN // Q_BLOCK)]
    bias_spec = pl.BlockSpec((None, 2 * pairs, Q_BLOCK, K_WIN), lambda g, i: (jnp.minimum(i, N_START), g, 0, 0))
    return row_spec, kv_specs, bias_spec


def _attn_fwd(proj, bias, pairs=N_PAIRS):
    t = proj.shape[0]
    n_win = K_WIN // Q_BLOCK

    def body(q_ref, *refs):
        k_refs, v_refs = refs[:n_win], refs[n_win:2 * n_win]
        b_ref, o_ref, lse_ref = refs[2 * n_win:]
        for pp in range(pairs):
            cols = slice(pp * PAIR, (pp + 1) * PAIR)
            k = jnp.concatenate([r[:, cols] for r in k_refs], axis=0)
            v = jnp.concatenate([r[:, cols] for r in v_refs], axis=0)
            q = q_ref[:, cols]
            o = lse = None
            for hh, lanes in enumerate(_head_lanes()):
                s = _contract_lanes(_only(lanes, q, HEAD_DIM ** -0.5), k) + b_ref[2 * pp + hh]
                m = jnp.max(s, axis=1, keepdims=True)
                p = jnp.exp(s - m)
                l = jnp.sum(p, axis=1, keepdims=True)
                oh = jnp.dot(p.astype(BF16), v, preferred_element_type=F32) / l
                lse_h = jnp.broadcast_to(m + jnp.log(l), oh.shape)
                o, lse = (oh, lse_h) if o is None else (jnp.where(lanes, oh, o), jnp.where(lanes, lse_h, lse))
            o_ref[:, cols] = o.astype(o_ref.dtype)
            lse_ref[:, cols] = lse

    row_spec, kv_specs, bias_spec = _attn_specs(pairs)
    return pl.pallas_call(
        body, name="attn_fwd", grid=(N_PAIRS // pairs, t // Q_BLOCK),
        in_specs=[row_spec] + kv_specs + [bias_spec], out_specs=[row_spec, row_spec],
        out_shape=[jax.ShapeDtypeStruct((t, D_ATTN), BF16), jax.ShapeDtypeStruct((t, D_ATTN), F32)],
        compiler_params=_cparams(2),
    )(*([proj] * (1 + 2 * n_win)), bias)


def _attn_bwd(proj, bias, att, lse, datt, pairs=2):
    t = proj.shape[0]
    n_win = K_WIN // Q_BLOCK
    n_blocks = t // Q_BLOCK

    def body(q_ref, *refs):
        k_refs, v_refs = refs[:n_win], refs[n_win:2 * n_win]
        b_ref, o_ref, lse_ref, do_ref, dq_ref, dk_ref, dv_ref, db_ref, dk_acc, dv_acc = refs[2 * n_win:]
        i = pl.program_id(1)

        @pl.when(i == 0)
        def _():
            dk_acc[...] = jnp.zeros_like(dk_acc)
            dv_acc[...] = jnp.zeros_like(dv_acc)
            db_ref[...] = jnp.zeros_like(db_ref)

        rows = pl.ds(pl.multiple_of(i * Q_BLOCK, Q_BLOCK), K_WIN)
        scale = HEAD_DIM ** -0.5
        for pp in range(pairs):
            cols = slice(pp * PAIR, (pp + 1) * PAIR)
            k = jnp.concatenate([r[:, cols] for r in k_refs], axis=0)
            v = jnp.concatenate([r[:, cols] for r in v_refs], axis=0)
            q, do, o = q_ref[:, cols], do_ref[:, cols], o_ref[:, cols].astype(F32)
            dq = dk = dv = None
            for hh, lanes in enumerate(_head_lanes()):
                qh, doh = _only(lanes, q, scale), _only(lanes, do)
                s = _contract_lanes(qh, k) + b_ref[2 * pp + hh]
                lse_col = pp * PAIR + hh * HEAD_DIM
                p = jnp.exp(s - lse_ref[:, lse_col:lse_col + 1])
                delta = jnp.sum(doh.astype(F32) * o, axis=1, keepdims=True)
                ds = p * (_contract_lanes(doh, v) - delta)
                folded = ds[:CHUNK]
                for c in range(1, Q_BLOCK // CHUNK):
                    folded = folded + pltpu.roll(ds[c * CHUNK:(c + 1) * CHUNK], K_WIN - c * CHUNK, 1)
                db_ref[2 * pp + hh] += folded
                dsb = ds.astype(BF16)
                dqh = jnp.dot(dsb, k, preferred_element_type=F32)
                dq = dqh if dq is None else jnp.where(lanes, dqh, dq)
                dkh, dvh = _contract_rows(dsb, qh), _contract_rows(p.astype(BF16), doh)
                dk, dv = (dkh, dvh) if dk is None else (dk + dkh, dv + dvh)
            dq_ref[:, cols] = (dq * scale).astype(dq_ref.dtype)
            dk_acc[rows, cols] += dk
            dv_acc[rows, cols] += dv

        @pl.when(i == n_blocks - 1)
        def _():
            dk_ref[...] = dk_acc[K_PAD:, :].astype(dk_ref.dtype)
            dv_ref[...] = dv_acc[K_PAD:, :].astype(dv_ref.dtype)

    width = pairs * PAIR
    row_spec, kv_specs, bias_spec = _attn_specs(pairs)
    full_spec = pl.BlockSpec((t, width), lambda g, i: (0, g))
    return pl.pallas_call(
        body, name="attn_bwd", grid=(N_PAIRS // pairs, n_blocks),
        in_specs=[row_spec] + kv_specs + [bias_spec, row_spec, row_spec, row_spec],
        out_specs=[row_spec, full_spec, full_spec,
                   pl.BlockSpec((2 * pairs, CHUNK, K_WIN), lambda g, i: (g, 0, 0))],
        out_shape=[jax.ShapeDtypeStruct((t, D_ATTN), BF16)] * 3 + [jax.ShapeDtypeStruct((N_HEADS, CHUNK, K_WIN), F32)],
        scratch_shapes=[pltpu.VMEM((t + K_PAD, width), F32)] * 2, compiler_params=_cparams(2),
    )(*([proj] * (1 + 2 * n_win)), bias, att, lse, datt)


CONV_LEAD = CONV_HALO - (CONV_WIDTH - 1)
CONV_LANES = 128
CONV_ROWS = 64


def _conv_specs(t):
    tt = _row_tile(t, CONV_TILE)
    per = tt // CONV_HALO
    n_halo = t // CONV_HALO
    tile = lambda cb: pl.BlockSpec((tt, COL), functools.partial(lambda i, cb: (i, cb), cb=cb))
    prev = lambda cb: pl.BlockSpec((CONV_HALO, COL),
                                   functools.partial(lambda i, cb: (jnp.maximum(i * per - 1, 0), cb), cb=cb))
    nxt = lambda cb: pl.BlockSpec((CONV_HALO, COL),
                                  functools.partial(lambda i, cb: (jnp.minimum((i + 1) * per, n_halo - 1), cb), cb=cb))
    vec = lambda shape: pl.BlockSpec(shape, lambda i: (0, 0))
    return tt, tile, prev, nxt, vec


def _glu(ca, cg, bias):
    return (ca.astype(F32) + bias[:, :D_CONV]) * jax.nn.sigmoid(cg.astype(F32) + bias[:, D_CONV:])


SUBLANES = 8


def _shift_copies(ext_ref):
    n = ext_ref.shape[1] - SUBLANES
    for s in range(1, SUBLANES):
        ext_ref[s, 0:n, :] = ext_ref[0, s:s + n, :]


def _tap_tiles(ext_ref, first_row, r0, lanes):
    n_g = CONV_ROWS // SUBLANES
    for s in range(SUBLANES):
        taps = [w for w in range(CONV_WIDTH) if first_row(w) % SUBLANES == s]
        if not taps:
            continue
        lo = min(first_row(w) for w in taps) - s
        n_tiles = (max(first_row(w) for w in taps) - s - lo) // SUBLANES + n_g
        tiles = [ext_ref[s, r0 + lo + SUBLANES * b:r0 + lo + SUBLANES * (b + 1), lanes] for b in range(n_tiles)]
        for w in taps:
            k = (first_row(w) - s - lo) // SUBLANES
            yield w, tiles[k:k + n_g]


def _taps(ext_ref, tt, first_row, w_ref, out_ref):
    n_g = CONV_ROWS // SUBLANES
    for l0 in range(0, D_CONV, CONV_LANES):
        lanes = slice(l0, l0 + CONV_LANES)
        for r0 in range(0, tt, CONV_ROWS):
            acc = [jnp.zeros((SUBLANES, CONV_LANES), F32)] * n_g
            for w, tiles in _tap_tiles(ext_ref, first_row, r0, lanes):
                weight = jnp.broadcast_to(w_ref[w:w + 1, lanes], (SUBLANES, CONV_LANES))
                acc = [a + t * weight for a, t in zip(acc, tiles)]
            for g in range(n_g):
                out_ref[r0 + SUBLANES * g:r0 + SUBLANES * (g + 1), lanes] = acc[g]


def _tap_sums(ext_ref, tt, first_row, x_ref, out_ref):
    n_g = CONV_ROWS // SUBLANES
    for l0 in range(0, D_CONV, CONV_LANES):
        lanes = slice(l0, l0 + CONV_LANES)
        acc = [jnp.zeros((SUBLANES, CONV_LANES), F32)] * CONV_WIDTH
        for r0 in range(0, tt, CONV_ROWS):
            x = [x_ref[0, r0 + SUBLANES * g:r0 + SUBLANES * (g + 1), lanes] for g in range(n_g)]
            for w, tiles in _tap_tiles(ext_ref, first_row, r0, lanes):
                part = tiles[0] * x[0]
                for g in range(1, n_g):
                    part = part + tiles[g] * x[g]
                acc[w] = acc[w] + part
        for w in range(CONV_WIDTH):
            out_ref[w:w + 1, lanes] += jnp.sum(acc[w], axis=0, keepdims=True)


def _conv_fwd(proj, glu_bias, dw, dw_b, ln_g, ln_b):
    t = proj.shape[0]
    tt, tile, prev, nxt, vec = _conv_specs(t)
    ca_blk, cg_blk = 3 * D_ATTN // COL, 3 * D_ATTN // COL + 1

    def body(ca_ref, cg_ref, pa_ref, pg_ref, gb_ref, dw_ref, dwb_ref, g_ref, b_ref, cs_ref, c_ref, z_ref, ext_ref):
        i = pl.program_id(0)
        bias = gb_ref[...]
        c = _glu(ca_ref[...], cg_ref[...], bias)
        halo = _glu(pa_ref[...], pg_ref[...], bias)
        ext_ref[0, 0:CONV_HALO, :] = jnp.where(i == 0, 0.0, halo)
        ext_ref[0, CONV_HALO:, :] = c
        _shift_copies(ext_ref)
        c_ref[...] = c
        _taps(ext_ref, tt, lambda w: CONV_LEAD + w, dw_ref, z_ref)
        z = z_ref[...] + dwb_ref[...]
        z_ref[...] = z
        mu = jnp.mean(z, axis=-1, keepdims=True)
        zc = z - mu
        y = zc * lax.rsqrt(jnp.mean(zc * zc, axis=-1, keepdims=True) + EPS) * g_ref[...] + b_ref[...]
        cs_ref[...] = (y * jax.nn.sigmoid(y)).astype(cs_ref.dtype)

    out_spec = pl.BlockSpec((tt, D_CONV), lambda i: (i, 0))
    return pl.pallas_call(
        body, name="conv_fwd", grid=(t // tt,),
        in_specs=[tile(ca_blk), tile(cg_blk), prev(ca_blk), prev(cg_blk), vec(glu_bias.shape), vec(dw.shape),
                  vec(dw_b.shape), vec(ln_g.shape), vec(ln_b.shape)],
        out_specs=[out_spec] * 3,
        out_shape=[jax.ShapeDtypeStruct((t, D_CONV), BF16), jax.ShapeDtypeStruct((t, D_CONV), F32),
                   jax.ShapeDtypeStruct((t, D_CONV), F32)],
        scratch_shapes=[pltpu.VMEM((SUBLANES, tt + CONV_HALO, D_CONV), F32)], compiler_params=_cparams(1),
    )(proj, proj, proj, proj, glu_bias, dw, dw_b, ln_g, ln_b)


def _conv_bwd(proj, c, z, dcs, glu_bias, dw, ln_g, ln_b):
    t = proj.shape[0]
    tt, tile, prev, nxt, vec = _conv_specs(t)
    n_tiles = t // tt
    ca_blk, cg_blk = 3 * D_ATTN // COL, 3 * D_ATTN // COL + 1

    def ln_bwd(zv, dcsv, g, b):
        mu = jnp.mean(zv, axis=-1, keepdims=True)
        zc = zv - mu
        rstd = lax.rsqrt(jnp.mean(zc * zc, axis=-1, keepdims=True) + EPS)
        zhat = zc * rstd
        y = zhat * g + b
        sig = jax.nn.sigmoid(y)
        dy = dcsv * sig * (1.0 + y * (1.0 - sig))
        dzh = dy * g
        dz = rstd * (dzh - jnp.mean(dzh, axis=-1, keepdims=True) - zhat * jnp.mean(dzh * zhat, axis=-1, keepdims=True))
        return dz, dy, zhat

    def body(ca_ref, cg_ref, c_ref, cprev_ref, z_ref, znext_ref, dcs_ref, dcsnext_ref, gb_ref, dw_ref, g_ref, b_ref,
             dcin_ref, ddw_ref, ddwb_ref, dg_ref, db_ref, dgb_ref, cext_ref, dzext_ref, dc_ref):
        i = pl.program_id(0)

        @pl.when(i == 0)
        def _():
            for ref in (ddw_ref, ddwb_ref, dg_ref, db_ref, dgb_ref):
                ref[...] = jnp.zeros_like(ref)

        g, b = g_ref[...], b_ref[...]
        dz, dy, zhat = ln_bwd(z_ref[...], dcs_ref[...], g, b)
        dz_next, _, _ = ln_bwd(znext_ref[...], dcsnext_ref[...], g, b)
        dg_ref[...] += _colsum(dy * zhat)
        db_ref[...] += _colsum(dy)
        ddwb_ref[...] += _colsum(dz)
        dzext_ref[0, 0:tt, :] = dz
        dzext_ref[0, tt:, :] = jnp.where(i == n_tiles - 1, 0.0, dz_next)
        _shift_copies(dzext_ref)
        cext_ref[0, 0:CONV_HALO, :] = jnp.where(i == 0, 0.0, cprev_ref[...])
        cext_ref[0, CONV_HALO:, :] = c_ref[...]
        _shift_copies(cext_ref)
        _taps(dzext_ref, tt, lambda w: CONV_WIDTH - 1 - w, dw_ref, dc_ref)
        _tap_sums(cext_ref, tt, lambda w: CONV_LEAD + w, dzext_ref, ddw_ref)
        bias = gb_ref[...]
        a_in = ca_ref[...].astype(F32) + bias[:, :D_CONV]
        sg = jax.nn.sigmoid(cg_ref[...].astype(F32) + bias[:, D_CONV:])
        dc = dc_ref[...]
        dcin = jnp.concatenate([dc * sg, dc * a_in * sg * (1.0 - sg)], axis=1)
        dcin_ref[...] = dcin.astype(dcin_ref.dtype)
        dgb_ref[...] += _colsum(dcin)

    row = lambda: pl.BlockSpec((tt, D_CONV), lambda i: (i, 0))
    per = tt // CONV_HALO
    n_halo = t // CONV_HALO
    prev_row = pl.BlockSpec((CONV_HALO, D_CONV), lambda i: (jnp.maximum(i * per - 1, 0), 0))
    next_row = lambda: pl.BlockSpec((CONV_HALO, D_CONV), lambda i: (jnp.minimum((i + 1) * per, n_halo - 1), 0))
    acc = lambda shape: pl.BlockSpec(shape, lambda i: (0, 0))
    return pl.pallas_call(
        body, name="conv_bwd", grid=(n_tiles,),
        in_specs=[tile(ca_blk), tile(cg_blk), row(), prev_row, row(), next_row(), row(), next_row(),
                  vec(glu_bias.shape), vec(dw.shape), vec(ln_g.shape), vec(ln_b.shape)],
        out_specs=[pl.BlockSpec((tt, 2 * D_CONV), lambda i: (i, 0)), acc(dw.shape), acc((1, D_CONV)),
                   acc((1, D_CONV)), acc((1, D_CONV)), acc((1, 2 * D_CONV))],
        out_shape=[jax.ShapeDtypeStruct((t, 2 * D_CONV), BF16), jax.ShapeDtypeStruct(dw.shape, F32),
                   jax.ShapeDtypeStruct((1, D_CONV), F32), jax.ShapeDtypeStruct((1, D_CONV), F32),
                   jax.ShapeDtypeStruct((1, D_CONV), F32), jax.ShapeDtypeStruct((1, 2 * D_CONV), F32)],
        scratch_shapes=[pltpu.VMEM((SUBLANES, tt + CONV_HALO, D_CONV), F32),
                        pltpu.VMEM((SUBLANES, tt + CONV_HALO, D_CONV), F32), pltpu.VMEM((tt, D_CONV), F32)],
        compiler_params=_cparams(1),
    )(proj, proj, c, c, z, z, dcs, dcs, glu_bias, dw, ln_g, ln_b)


def _place():
    x, y, c = lax.axis_index("x"), lax.axis_index("y"), lax.axis_index("c")
    chips = [(1 - x, y), (x, 1 - y), (1 - x, 1 - y)]
    return x, y, c, chips


def _chip_index(chip):
    return 2 * chip[0] + chip[1]


def _half_rows(c, half):
    return pl.ds(pl.multiple_of(c * half, 16), half)


def _gather_carry(blocked):
    n = len(blocked)

    def over_ici(o_refs, send_sems, recv_sems):
        x, y, c, chips = _place()
        me = _chip_index((x, y))
        copies = []
        for a in range(n):
            mine = o_refs[a].at[me, _half_rows(c, o_refs[a].shape[1] // 2), :]
            for k, chip in enumerate(chips):
                copies.append(pltpu.make_async_remote_copy(
                    src_ref=mine, dst_ref=mine, send_sem=send_sems.at[6 * a + k], recv_sem=recv_sems.at[6 * a + k],
                    device_id=(chip[0], chip[1], c), device_id_type=MESH))
        return copies

    def to_sibling(o_refs, send_sems, recv_sems, sent_by_me):
        x, y, c, chips = _place()
        copies = []
        for a in range(n):
            rows = _half_rows(c if sent_by_me else 1 - c, o_refs[a].shape[1] // 2)
            for k, chip in enumerate(chips):
                landed = o_refs[a].at[_chip_index(chip), rows, :]
                copies.append(pltpu.make_async_remote_copy(
                    src_ref=landed, dst_ref=landed, send_sem=send_sems.at[6 * a + 3 + k],
                    recv_sem=recv_sems.at[6 * a + 3 + k], device_id=(x, y, 1 - c), device_id_type=MESH))
        return copies

    def start(ins, outs, sems):
        for cp in over_ici(outs, *sems):
            cp.start()

    def hand_on(ins, outs, sems):
        for arrived, onward in zip(over_ici(outs, *sems), to_sibling(outs, *sems, True)):
            arrived.wait_recv()
            onward.start()

    def finish(ins, outs, sems):
        for cp in to_sibling(outs, *sems, False):
            cp.wait_recv()
        for cp in over_ici(outs, *sems) + to_sibling(outs, *sems, True):
            cp.wait_send()

    return _Carry(
        ins=list(blocked), outs=[jax.ShapeDtypeStruct(w.shape, w.dtype) for w in blocked],
        aliases={a: a for a in range(n)},
        sems=[pltpu.SemaphoreType.DMA((6 * n,)), pltpu.SemaphoreType.DMA((6 * n,))],
        phases=[("first", start), ("late", hand_on), ("last", finish)])


def _pair_exchange(name, grads):
    n = len(grads)

    def body(*refs):
        g_refs, land_refs = refs[:n], refs[n:2 * n]
        send_sems, recv_sems = refs[2 * n:]
        x, y, c, _ = _place()
        copies = []
        for a in range(n):
            half = g_refs[a].shape[1] // 2
            cp = pltpu.make_async_remote_copy(
                src_ref=g_refs[a].at[:, _half_rows(1 - c, half), :], dst_ref=land_refs[a],
                send_sem=send_sems.at[a], recv_sem=recv_sems.at[a], device_id=(x, y, 1 - c), device_id_type=MESH)
            cp.start()
            copies.append(cp)
        for cp in copies:
            cp.wait()

    return pl.pallas_call(
        body, name=name, in_specs=[ANY] * n, out_specs=[ANY] * n,
        out_shape=[jax.ShapeDtypeStruct((g.shape[0], g.shape[1] // 2, g.shape[2]), g.dtype) for g in grads],
        scratch_shapes=[pltpu.SemaphoreType.DMA((n,)), pltpu.SemaphoreType.DMA((n,))],
    )(*grads)


def _to_owner_carry(parts):
    n = len(parts)

    def sends(p_refs, l_refs, send_sems, recv_sems):
        x, y, c, chips = _place()
        me = _chip_index((x, y))
        return [pltpu.make_async_remote_copy(
            src_ref=p_refs[a].at[_chip_index(chip)], dst_ref=l_refs[a].at[me],
            send_sem=send_sems.at[3 * a + k], recv_sem=recv_sems.at[3 * a + k],
            device_id=(chip[0], chip[1], c), device_id_type=MESH) for a in range(n) for k, chip in enumerate(chips)]

    def start(ins, outs, sems):
        for cp in sends(ins, outs, *sems):
            cp.start()

    def finish(ins, outs, sems):
        x, y, c, chips = _place()
        send_sems, recv_sems = sems
        for a in range(n):
            for k, chip in enumerate(chips):
                slot = outs[a].at[_chip_index(chip)]
                pltpu.make_async_remote_copy(
                    src_ref=slot, dst_ref=slot, send_sem=send_sems.at[3 * a + k], recv_sem=recv_sems.at[3 * a + k],
                    device_id=(chip[0], chip[1], c), device_id_type=MESH).wait_recv()
        for cp in sends(ins, outs, *sems):
            cp.wait_send()

    return _Carry(
        ins=list(parts), outs=[jax.ShapeDtypeStruct(p.shape, p.dtype) for p in parts], aliases={},
        sems=[pltpu.SemaphoreType.DMA((3 * n,)), pltpu.SemaphoreType.DMA((3 * n,))],
        phases=[("first", start), ("last", finish)])


def _swap_halves(halves):
    n = len(halves)

    def body(*refs):
        h_refs, o_refs = refs[:n], refs[n:2 * n]
        send_sems, recv_sems = refs[2 * n:]
        x, y, c, _ = _place()
        copies = []
        for a in range(n):
            cp = pltpu.make_async_remote_copy(
                src_ref=h_refs[a], dst_ref=o_refs[a], send_sem=send_sems.at[a], recv_sem=recv_sems.at[a],
                device_id=(x, y, 1 - c), device_id_type=MESH)
            cp.start()
            copies.append(cp)
        for cp in copies:
            cp.wait()

    return pl.pallas_call(
        body, name="grad_swap_halves", in_specs=[ANY] * n, out_specs=[ANY] * n,
        out_shape=[jax.ShapeDtypeStruct(h.shape, h.dtype) for h in halves],
        scratch_shapes=[pltpu.SemaphoreType.DMA((n,)), pltpu.SemaphoreType.DMA((n,))],
    )(*halves)


def _all_devices(name, block):
    r, cols = block.shape

    def body(b_ref, all_ref, sum_ref, send_sems, recv_sems):
        x, y, c, _ = _place()
        me = 4 * x + 2 * y + c
        all_ref[me] = b_ref[...]
        flips = [(fx, fy, fc) for fx in (0, 1) for fy in (0, 1) for fc in (0, 1)][1:]
        copies = []
        for k, (fx, fy, fc) in enumerate(flips):
            cp = pltpu.make_async_remote_copy(
                src_ref=b_ref, dst_ref=all_ref.at[me], send_sem=send_sems.at[k], recv_sem=recv_sems.at[k],
                device_id=(x ^ fx, y ^ fy, c ^ fc), device_id_type=MESH)
            cp.start()
            copies.append(cp)
        for k, (fx, fy, fc) in enumerate(flips):
            slot = all_ref.at[4 * (x ^ fx) + 2 * (y ^ fy) + (c ^ fc)]
            pltpu.make_async_remote_copy(
                src_ref=slot, dst_ref=slot, send_sem=send_sems.at[k], recv_sem=recv_sems.at[k],
                device_id=(x ^ fx, y ^ fy, c ^ fc), device_id_type=MESH).wait_recv()
        for cp in copies:
            cp.wait_send()
        acc = all_ref[0]
        for d in range(1, N_DEV):
            acc = acc + all_ref[d]
        sum_ref[...] = acc

    vmem = pl.BlockSpec(memory_space=pltpu.VMEM)
    return pl.pallas_call(
        body, name=name, in_specs=[vmem], out_specs=[vmem, vmem],
        out_shape=[jax.ShapeDtypeStruct((N_DEV, r, cols), F32), jax.ShapeDtypeStruct((r, cols), F32)],
        scratch_shapes=[pltpu.SemaphoreType.DMA((N_DEV - 1,)), pltpu.SemaphoreType.DMA((N_DEV - 1,))],
    )(block)


PACK = 1024


def _packed_rows(shape, width):
    size, last = int(np.prod(shape)), shape[-1]
    cols = last if last <= width else width
    assert size % cols == 0
    return size // cols, cols


def _pack(vals, width=PACK):
    rows = []
    for v in vals:
        n_rows, cols = _packed_rows(v.shape, width)
        rows.append(jnp.pad(v.reshape(n_rows, cols).astype(F32), ((0, 0), (0, width - cols))))
    buf = jnp.concatenate(rows, axis=0)
    return jnp.pad(buf, ((0, (-buf.shape[0]) % 8), (0, 0)))


def _unpack(buf, shapes, width=PACK):
    out, r = [], 0
    for shape in shapes:
        n_rows, cols = _packed_rows(shape, width)
        out.append(buf[r:r + n_rows, :cols].reshape(shape))
        r += n_rows
    return out


FFN_SPLIT = 2


def _ffn_hidden(name, n, wg, wu, tm=1024, carry=None):
    m, k = n.shape
    f = wg.shape[0]
    fb = f // FFN_SPLIT
    tm = _row_tile(m, tm)

    def core(n_ref, wg_ref, wu_ref, a_ref, b_ref, s_ref):
        nv = n_ref[...]
        a = _dot(nv, wg_ref[...], True)
        b = _dot(nv, wu_ref[...], True)
        a_ref[...] = a.astype(a_ref.dtype)
        b_ref[...] = b.astype(b_ref.dtype)
        s_ref[...] = (a * jax.nn.sigmoid(a) * b).astype(s_ref.dtype)

    w_spec = pl.BlockSpec((fb, k), lambda j, i: (j, 0))
    out_spec = pl.BlockSpec((tm, fb), lambda j, i: (i, j))
    return _call(name, core, (FFN_SPLIT, m // tm), [pl.BlockSpec((tm, k), lambda j, i: (i, 0)), w_spec, w_spec],
                 [out_spec] * 3, [jax.ShapeDtypeStruct((m, f), BF16)] * 3, [], [n, wg, wu], carry)


def _ffn_d_hidden(name, df, wd, a, b, tm=512):
    m, k = df.shape
    f = wd.shape[0]
    fb = f // FFN_SPLIT
    tm = _row_tile(m, tm)

    def body(df_ref, wd_ref, a_ref, b_ref, da_ref, db_ref):
        dfv = df_ref[...]
        for j in range(FFN_SPLIT):
            cols = slice(j * fb, (j + 1) * fb)
            ds = _dot(dfv, wd_ref[cols, :], True)
            av, bv = a_ref[:, cols].astype(F32), b_ref[:, cols].astype(F32)
            sig = jax.nn.sigmoid(av)
            da_ref[:, cols] = (ds * bv * sig * (1.0 + av * (1.0 - sig))).astype(da_ref.dtype)
            db_ref[:, cols] = (ds * av * sig).astype(db_ref.dtype)

    row = pl.BlockSpec((tm, f), lambda i: (i, 0))
    return pl.pallas_call(
        body, name=name, grid=(m // tm,),
        in_specs=[pl.BlockSpec((tm, k), lambda i: (i, 0)), _resident(wd), row, row],
        out_specs=[row, row], out_shape=[jax.ShapeDtypeStruct((m, f), BF16)] * 2,
        compiler_params=_cparams(1),
    )(df, wd, a, b)


def kernel(x, ffn1_norm_pre, ffn1_w_gate, ffn1_w_up, ffn1_w_down, ffn1_norm_post, mix_norm_pre, w_in, gate_bias, rel_table, w_attn_out, conv_glu_bias, conv_dw_w, conv_dw_b, conv_ln_g, conv_ln_b, conv_w_out, w_out, mix_norm_post, ffn2_norm_pre, ffn2_w_gate, ffn2_w_up, ffn2_w_down, ffn2_norm_post, loss_target, m_ffn1_norm_pre, m_ffn1_w_gate, m_ffn1_w_up, m_ffn1_w_down, m_ffn1_norm_post, m_mix_norm_pre, m_w_in, m_gate_bias, m_rel_table, m_w_attn_out, m_conv_glu_bias, m_conv_dw_w, m_conv_dw_b, m_conv_ln_g, m_conv_ln_b, m_conv_w_out, m_w_out, m_mix_norm_post, m_ffn2_norm_pre, m_ffn2_w_gate, m_ffn2_w_up, m_ffn2_w_down, m_ffn2_norm_post, v_ffn1_norm_pre, v_ffn1_w_gate, v_ffn1_w_up, v_ffn1_w_down, v_ffn1_norm_post, v_mix_norm_pre, v_w_in, v_gate_bias, v_rel_table, v_w_attn_out, v_conv_glu_bias, v_conv_dw_w, v_conv_dw_b, v_conv_ln_g, v_conv_ln_b, v_conv_w_out, v_w_out, v_mix_norm_post, v_ffn2_norm_pre, v_ffn2_w_gate, v_ffn2_w_up, v_ffn2_w_down, v_ffn2_norm_post):
    args = dict(locals())
    names = ['ffn1_norm_pre', 'ffn1_w_gate', 'ffn1_w_up', 'ffn1_w_down', 'ffn1_norm_post', 'mix_norm_pre', 'w_in',
             'gate_bias', 'rel_table', 'w_attn_out', 'conv_glu_bias', 'conv_dw_w', 'conv_dw_b', 'conv_ln_g',
             'conv_ln_b', 'conv_w_out', 'w_out', 'mix_norm_post', 'ffn2_norm_pre', 'ffn2_w_gate', 'ffn2_w_up',
             'ffn2_w_down', 'ffn2_norm_post']
    big = ['ffn1_w_gate', 'ffn1_w_up', 'ffn1_w_down', 'w_in', 'w_attn_out', 'conv_w_out', 'w_out', 'ffn2_w_gate',
           'ffn2_w_up', 'ffn2_w_down']
    small = [n for n in names if n not in big]

    xs, target = x[0], loss_target[0]
    t, d = xs.shape
    cx, cy = lax.axis_index("x"), lax.axis_index("y")
    chip = 2 * cx + cy

    dw_shard = conv_dw_w[0, :, 0, :]
    cshard = dw_shard.shape[1]
    dw_all, _ = _all_devices("gather_dw", _pack([dw_shard], width=cshard))
    dw_full = jnp.concatenate([dw_all[2 * j, :CONV_WIDTH, :cshard] for j in range(N_CHIPS)], axis=1)
    dw_full = jnp.pad(dw_full, ((0, CONV_HALO - CONV_WIDTH), (0, 0)))
    peers = [(1 - cx, cy), (cx, 1 - cy), (1 - cx, 1 - cy)]
    pos = jnp.stack([lax.axis_index("c"), chip] + [_chip_index(p) for p in peers]).astype(jnp.int32)
    transposed = ("ffn1_w_gate", "ffn1_w_up", "ffn2_w_gate", "ffn2_w_up")
    weight_of = lambda n: n[2:] if n[:2] in ("m_", "v_") else n
    shard = lambda n: jnp.transpose(args[n][0]) if weight_of(n) in transposed else args[n][0]
    unshard = lambda n, v: (jnp.transpose(v) if n in transposed else v)[None]
    own = {n: _cast_into("cast_" + n, pos, shard(n)) for n in big}
    gather = lambda *ns: _gather_carry([own[n] for n in ns])
    res_spec = [(d, BF16), (d, F32), (d, BF16)]
    whole = lambda w: w.reshape(-1, w.shape[-1])

    n1, (wg1, wu1) = _rms_fwd("ffn1_pre", xs, ffn1_norm_pre, gather("ffn1_w_gate", "ffn1_w_up"))
    (a1, b1, s1), (wd1, win, wao, wco, wout) = _ffn_hidden(
        "ffn1_hidden", n1, whole(wg1), whole(wu1),
        carry=gather("ffn1_w_down", "w_in", "w_attn_out", "conv_w_out", "w_out"))
    (f1, h1, u), (wg2,) = _mm_kblk(
        "ffn1_down", [(s1, whole(wd1)[None])], trans_w=False, epilogue=_ep_post_res_pre(0.5), rows=[xs],
        vecs=[ffn1_norm_post, mix_norm_pre], row_outs=res_spec, carry=gather("ffn2_w_gate"))
    proj, (wu2, wd2) = _mm_nblk("mix_in", u, win, trans_w=False, out_blocked=False, out_dtype=BF16,
                                carry=gather("ffn2_w_up", "ffn2_w_down"))
    table_pad = jnp.pad(rel_table[0], ((0, 0), (0, REL_PAD - rel_table.shape[2])))
    bias = _bias_expand(table_pad)
    att, lse = _attn_fwd(proj, bias)
    cs, c_glu, z_conv = _conv_fwd(proj, conv_glu_bias, dw_full, conv_dw_b, conv_ln_g, conv_ln_b)
    y_a, y_b, merged = _mix_merge(att, cs, wao, wco, proj, gate_bias)
    (mo, h2, n2), _ = _mm_kblk(
        "mix_out", [(merged, wout)], trans_w=False, epilogue=_ep_post_res_pre(1.0), rows=[h1],
        vecs=[mix_norm_post, ffn2_norm_pre], row_outs=res_spec)
    (a2, b2, s2), _ = _ffn_hidden("ffn2_hidden", n2, whole(wg2), whole(wu2))
    g = {}
    (dy, df2, err2, g["ffn2_norm_post"]), _ = _mm_kblk(
        "ffn2_down", [(s2, whole(wd2)[None])], trans_w=False, epilogue=_ep_loss(0.5, d), rows=[h2, target],
        vecs=[ffn2_norm_post], row_outs=[(d, BF16), (d, BF16)], vec_outs=[d, d])
    loss = lax.psum(0.5 * jnp.sum(err2) / d, ("x", "y", "c"))

    parts, landed = {}, {}

    def ffn_bwd(tag, df, n, a, b, s, wg, wu, wd, **epilogue):
        da, db = _ffn_d_hidden(tag + "_d_hidden", df, whole(wd), a, b)
        group = [tag + "_w_down", tag + "_w_gate", tag + "_w_up"]
        local = [_mm_tn_wide(tag + "_g_" + what, hidden, other, d, a_split=FFN_SPLIT).reshape(wd.shape)
                 for what, hidden, other in (("down", s, df), ("gate", da, n), ("up", db, n))]
        return _mm_kblk(tag + "_d_n", [(da, whole(wg)[None]), (db, whole(wu)[None])], trans_w=False,
                        carry=pair_sums(tag, group, local), **epilogue), group

    def pair_sums(tag, group, local):
        theirs = _pair_exchange("pair_" + tag, local)
        for n, mine, other in zip(group, local, theirs):
            parts[n] = _add_pair("pair_sum_" + n, pos, mine, other)
        return _to_owner_carry([parts[n] for n in group])

    def keep(group, carried):
        for n, val in zip(group, carried):
            landed[n] = val

    ((dh2, dmo, g["ffn2_norm_pre"], g["mix_norm_post"]), carried), group = ffn_bwd(
        "ffn2", df2, n2, a2, b2, s2, wg2, wu2, wd2, epilogue=_ep_pre_bwd_post(1.0), rows=[h2, dy, mo],
        vecs=[ffn2_norm_pre, mix_norm_post], row_outs=[(d, BF16), (d, BF16)], vec_outs=[d, d])
    keep(group, carried)
    g_wout = _mm_tn("mix_g_out", merged, "col", dmo, "full")
    dy_a, dy_b, dgates, datt, dcs, g["gate_bias"] = _mix_d_merge(dmo, wout, y_a, y_b, wao, wco, proj, gate_bias)
    g_wao = _mm_tn("attn_g_out", att, "full", dy_a, "col")
    g_wco = _mm_tn("conv_g_out", cs, "full", dy_b, "col")
    dq, dk, dv, dbias = _attn_bwd(proj, bias, att, lse, datt)
    g["rel_table"] = _bias_fold(dbias)[:, :rel_table.shape[2]]
    dcin, g_dw, g["conv_dw_b"], g["conv_ln_g"], g["conv_ln_b"], g["conv_glu_bias"] = _conv_bwd(
        proj, c_glu, z_conv, dcs, conv_glu_bias, dw_full, conv_ln_g, conv_ln_b)
    pieces = [("q", dq), ("k", dk), ("v", dv), ("conv", dcin), ("gates", dgates)]
    n_in = win.shape[0] * win.shape[2]
    win_cols = jnp.transpose(jnp.transpose(win, (1, 0, 2)).reshape(d, n_in // COL, COL), (1, 0, 2))
    g_cols = jnp.concatenate([_mm_tn_wide("mix_g_in_" + tag, u, piece, COL) for tag, piece in pieces], axis=0)
    g_win = jnp.transpose(jnp.transpose(g_cols, (1, 0, 2)).reshape(d, win.shape[0], win.shape[2]), (1, 0, 2))
    bounds = np.cumsum([0] + [piece.shape[1] // COL for _, piece in pieces])
    group = ["w_out", "w_attn_out", "conv_w_out", "w_in"]
    (dh1, df1, g["mix_norm_pre"], g["ffn1_norm_post"]), carried = _mm_kblk(
        "mix_d_in", [(piece, win_cols[lo:hi]) for (_, piece), lo, hi in zip(pieces, bounds[:-1], bounds[1:])],
        trans_w=True, epilogue=_ep_pre_bwd_post(0.5), rows=[h1, dh2, f1],
        vecs=[mix_norm_pre, ffn1_norm_post], row_outs=[(d, BF16), (d, BF16)], vec_outs=[d, d],
        carry=pair_sums("mix", group, [g_wout, g_wao, g_wco, g_win]))
    keep(group, carried)
    ((grad_x, g["ffn1_norm_pre"]), carried), group = ffn_bwd(
        "ffn1", df1, n1, a1, b1, s1, wg1, wu1, wd1, epilogue=_ep_pre_bwd_first(), rows=[xs, dh1],
        vecs=[ffn1_norm_pre], row_outs=[(d, F32)], vec_outs=[d])
    keep(group, carried)

    halves = [_add_chips("chip_sum_" + n, pos, parts[n], landed[n]) for n in big]
    other_halves = _swap_halves(halves)

    g["conv_dw_w"] = g_dw[:CONV_WIDTH]
    _, small_sum = _all_devices("sum_small", _pack([g[n] for n in small]))
    for n, val in zip(small, _unpack(small_sum, [g[n].shape for n in small])):
        g[n] = val
    g["conv_dw_w"] = lax.dynamic_slice_in_dim(g["conv_dw_w"], chip * cshard, cshard, axis=1)

    grads, deltas, new_m, new_v = {}, {}, {}, {}
    for n, mine, other in zip(big, halves, other_halves):
        gr, dl, m2, v2 = _adamw_halves("adamw_" + n, pos, shard(n), shard("m_" + n), shard("v_" + n), mine, other)
        grads[n], deltas[n], new_m[n], new_v[n] = unshard(n, gr), unshard(n, dl), unshard(n, m2), unshard(n, v2)
    shapes = [g[n].shape for n in small]
    packed = lambda pre: _pack([args[pre + n].reshape(shp) for n, shp in zip(small, shapes)])
    dl, m2, v2 = _adamw("adamw_small", packed(""), _pack([g[n] for n in small]), packed("m_"), packed("v_"))
    for n, a_, b_, c_ in zip(small, _unpack(dl, shapes), _unpack(m2, shapes), _unpack(v2, shapes)):
        shape = args[n].shape
        grads[n], deltas[n], new_m[n], new_v[n] = (g[n].reshape(shape), a_.reshape(shape), b_.reshape(shape),
                                                   c_.reshape(shape))

    return (loss, grad_x[None], *[grads[n] for n in names], *[deltas[n] for n in names],
            *[new_m[n] for n in names], *[new_v[n] for n in names])
```

```python
import functools

import numpy as np
import jax
import jax.numpy as jnp
from jax import lax
from jax.experimental import pallas as pl
from jax.experimental.pallas import tpu as pltpu

F32 = jnp.float32
BF16 = jnp.bfloat16
MESH = pl.DeviceIdType.MESH
ANY = pl.BlockSpec(memory_space=pl.ANY)

EPS = 1e-6
CHUNK = 64
LEFT_CHUNKS = 8
N_HEADS = 8
HEAD_DIM = 64
D_ATTN = N_HEADS * HEAD_DIM
D_CONV = 512
CONV_WIDTH = 31
REL_CLIP = 128
N_CHIPS = 4
N_DEV = 8
Q_BLOCK = 4 * CHUNK
K_PAD = LEFT_CHUNKS * CHUNK
K_WIN = K_PAD + Q_BLOCK
REL_EXT = 1024
REL_PAD = 384
CONV_HALO = 32
CONV_TILE = 512
COL = 512
NEG = -1e30

ADAM_LR = 0.001
ADAM_B1 = 0.9
ADAM_B2 = 0.999
ADAM_EPS = 1e-08
ADAM_WD = 0.01
ADAM_STEP = 10

VMEM_LIMIT_BYTES = 56 * 1024 * 1024


def _cparams(n_grid):
    return pltpu.CompilerParams(dimension_semantics=("arbitrary",) * n_grid, vmem_limit_bytes=VMEM_LIMIT_BYTES)


def _row_tile(rows, want):
    if rows <= want:
        return rows
    for t in range(want - want % 16, 0, -16):
        if rows % t == 0:
            return t
    raise ValueError((rows, want))


def _dot(a, w, trans_w):
    dims = (((1,), (1,)), ((), ())) if trans_w else (((1,), (0,)), ((), ()))
    return lax.dot_general(a, w, dims, preferred_element_type=F32)


class _Carry:
    LATE_STEPS = 2

    def __init__(self, ins, outs, aliases, sems, phases):
        self.ins, self.outs, self.aliases, self.sems, self.phases = ins, outs, aliases, sems, phases


def _call(name, core, grid, in_specs, out_specs, out_shape, scratch, args, carry=None):
    n_in, n_out, n_scr = len(in_specs), len(out_specs), len(scratch)
    if carry is None:
        out = pl.pallas_call(core, name=name, grid=grid, in_specs=in_specs, out_specs=out_specs, out_shape=out_shape,
                             scratch_shapes=scratch, compiler_params=_cparams(len(grid)))(*args)
        return list(out), []
    c_in, c_out = len(carry.ins), len(carry.outs)
    total = int(np.prod(grid))
    late = max(total - 1 - _Carry.LATE_STEPS, 0)

    def body(*refs):
        ins, refs = refs[:n_in], refs[n_in:]
        c_ins, refs = refs[:c_in], refs[c_in:]
        outs, refs = refs[:n_out], refs[n_out:]
        c_outs, refs = refs[:c_out], refs[c_out:]
        scr, c_sems = refs[:n_scr], refs[n_scr:]
        step = pl.program_id(0)
        for axis in range(1, len(grid)):
            step = step * grid[axis] + pl.program_id(axis)

        def run(when, at):
            for w, fn in carry.phases:
                if w == when:
                    pl.when(step == at)(functools.partial(fn, c_ins, c_outs, c_sems))

        run("first", 0)
        core(*ins, *outs, *scr)
        run("late", late)
        run("last", total - 1)

    out = pl.pallas_call(
        body, name=name, grid=grid, in_specs=list(in_specs) + [ANY] * c_in, out_specs=list(out_specs) + [ANY] * c_out,
        out_shape=list(out_shape) + list(carry.outs), scratch_shapes=list(scratch) + list(carry.sems),
        input_output_aliases={n_in + a: n_out + b for a, b in carry.aliases.items()},
        compiler_params=_cparams(len(grid)),
    )(*args, *carry.ins)
    return list(out[:n_out]), list(out[n_out:])


def _mm_nblk(name, a, w, *, trans_w, out_blocked, out_dtype, tm=1024, carry=None):
    m, k = a.shape
    nj = w.shape[0]
    nb = w.shape[1] if trans_w else w.shape[2]
    tm = _row_tile(m, tm)

    def core(a_ref, w_ref, o_ref):
        o_ref[...] = _dot(a_ref[...], w_ref[...], trans_w).astype(o_ref.dtype)

    if out_blocked:
        out_shape, out_spec = (nj, m, nb), pl.BlockSpec((None, tm, nb), lambda j, i: (j, i, 0))
    else:
        out_shape, out_spec = (m, nj * nb), pl.BlockSpec((tm, nb), lambda j, i: (i, j))
    out, carried = _call(
        name, core, (nj, m // tm),
        [pl.BlockSpec((tm, k), lambda j, i: (i, 0)), pl.BlockSpec((None,) + w.shape[1:], lambda j, i: (j, 0, 0))],
        [out_spec], [jax.ShapeDtypeStruct(out_shape, out_dtype)], [], [a, w], carry)
    return out[0] if carry is None else (out[0], carried)


def _mm_kblk(name, pairs, *, trans_w, out_dtype=F32, tm=512, sub=256, epilogue=None, rows=(), vecs=(), row_outs=None,
             vec_outs=(), carry=None):
    w0 = pairs[0][1]
    n = w0.shape[1] if trans_w else w0.shape[2]
    blocks = [(w.shape[0], w.shape[2] if trans_w else w.shape[1]) for _, w in pairs]
    m = pairs[0][0].shape[-2]
    tm = _row_tile(m, tm)
    ts = _row_tile(tm, sub)
    n_pairs, n_rows, n_vecs = len(pairs), len(rows), len(vecs)
    if epilogue is None:
        epilogue, row_outs = (lambda acc, r, v: ([acc], [])), [(n, out_dtype)]
    n_ro, n_vo = len(row_outs), len(vec_outs)

    def core(*refs):
        pair_refs, refs = refs[:2 * n_pairs], refs[2 * n_pairs:]
        row_refs, refs = refs[:n_rows], refs[n_rows:]
        vec_refs, refs = refs[:n_vecs], refs[n_vecs:]
        ro_refs, vo_refs = refs[:n_ro], refs[n_ro:]
        if n_vo:
            @pl.when(pl.program_id(0) == 0)
            def _():
                for ref in vo_refs:
                    ref[...] = jnp.zeros_like(ref)

        vec_vals = [v[...] for v in vec_refs]
        sums = None
        for r0 in range(0, tm, ts):
            sub_rows = slice(r0, r0 + ts)
            acc = None
            for p in range(n_pairs):
                a_ref, w_ref = pair_refs[2 * p], pair_refs[2 * p + 1]
                nj, kb = blocks[p]
                for j in range(nj):
                    a_blk = a_ref[j, sub_rows, :] if len(a_ref.shape) == 3 else a_ref[sub_rows, j * kb:(j + 1) * kb]
                    part = _dot(a_blk, w_ref[j], trans_w)
                    acc = part if acc is None else acc + part
            ro, vo = epilogue(acc, [r[sub_rows, :] for r in row_refs], vec_vals)
            for ref, val in zip(ro_refs, ro):
                ref[sub_rows, :] = val.astype(ref.dtype)
            sums = vo if sums is None else [s + v for s, v in zip(sums, vo)]
        for ref, val in zip(vo_refs, sums or []):
            ref[...] += val

    in_specs, args = [], []
    for (a, w), (nj, kb) in zip(pairs, blocks):
        if a.ndim == 3:
            in_specs.append(pl.BlockSpec((nj, tm, kb), lambda i: (0, i, 0)))
        else:
            in_specs.append(pl.BlockSpec((tm, nj * kb), lambda i: (i, 0)))
        in_specs.append(pl.BlockSpec(w.shape, lambda i: (0, 0, 0), pipeline_mode=pl.Buffered(1)))
        args += [a, w]
    in_specs += [pl.BlockSpec((tm, r.shape[1]), lambda i: (i, 0)) for r in rows]
    in_specs += [pl.BlockSpec(v.shape, lambda i: (0, 0)) for v in vecs]
    out_specs = [pl.BlockSpec((tm, cols), lambda i: (i, 0)) for cols, _ in row_outs]
    out_specs += [pl.BlockSpec((1, cols), lambda i: (0, 0)) for cols in vec_outs]
    out_shape = [jax.ShapeDtypeStruct((m, cols), dt) for cols, dt in row_outs]
    out_shape += [jax.ShapeDtypeStruct((1, cols), F32) for cols in vec_outs]
    return _call(name, core, (m // tm,), in_specs, out_specs, out_shape, [], args + list(rows) + list(vecs), carry)


def _mm_tn(name, a, a_mode, b, b_mode, *, out_dtype=BF16, tt=2048):
    nj = N_CHIPS
    t = a.shape[-2]
    tt = _row_tile(t, tt)

    def spec(x, mode):
        if mode == "full":
            return x.shape[1], pl.BlockSpec((tt, x.shape[1]), lambda j, s: (s, 0))
        if mode == "col":
            cb = x.shape[1] // nj
            return cb, pl.BlockSpec((tt, cb), lambda j, s: (s, j))
        return x.shape[2], pl.BlockSpec((None, tt, x.shape[2]), lambda j, s: (j, s, 0))

    ca, a_spec = spec(a, a_mode)
    cb, b_spec = spec(b, b_mode)
    n_steps = t // tt

    def body(a_ref, b_ref, o_ref, acc_ref):
        s = pl.program_id(1)

        @pl.when(s == 0)
        def _():
            acc_ref[...] = jnp.zeros_like(acc_ref)

        acc_ref[...] += lax.dot_general(a_ref[...], b_ref[...], (((0,), (0,)), ((), ())),
                                        preferred_element_type=F32)

        @pl.when(s == n_steps - 1)
        def _():
            o_ref[...] = acc_ref[...].astype(o_ref.dtype)

    return pl.pallas_call(
        body, name=name, grid=(nj, n_steps), in_specs=[a_spec, b_spec],
        out_specs=pl.BlockSpec((None, ca, cb), lambda j, s: (j, 0, 0)),
        out_shape=jax.ShapeDtypeStruct((nj, ca, cb), out_dtype),
        scratch_shapes=[pltpu.VMEM((ca, cb), F32)], compiler_params=_cparams(2),
    )(a, b)


def _mm_tn_wide(name, a, b, cb, *, a_split=1, out_dtype=BF16, tt=1024):
    t, ca = a.shape
    nb = b.shape[1] // cb
    tt = _row_tile(t, tt)
    n_steps = t // tt
    piece = ca // a_split

    def body(a_ref, b_ref, o_ref, acc_ref):
        s = pl.program_id(0)

        @pl.when(s == 0)
        def _():
            acc_ref[...] = jnp.zeros_like(acc_ref)

        for j in range(nb):
            bv = b_ref[:, j * cb:(j + 1) * cb]
            for c in range(a_split):
                rows = slice(c * piece, (c + 1) * piece)
                acc_ref[j, rows, :] += lax.dot_general(a_ref[:, rows], bv, (((0,), (0,)), ((), ())),
                                                       preferred_element_type=F32)

        @pl.when(s == n_steps - 1)
        def _():
            o_ref[...] = acc_ref[...].astype(o_ref.dtype)

    return pl.pallas_call(
        body, name=name, grid=(n_steps,),
        in_specs=[pl.BlockSpec((tt, ca), lambda s: (s, 0)), pl.BlockSpec((tt, nb * cb), lambda s: (s, 0))],
        out_specs=pl.BlockSpec((nb, ca, cb), lambda s: (0, 0, 0)),
        out_shape=jax.ShapeDtypeStruct((nb, ca, cb), out_dtype),
        scratch_shapes=[pltpu.VMEM((nb, ca, cb), F32)], compiler_params=_cparams(1),
    )(a, b)


def _rowwise(name, fn, rows, vecs, row_outs, vec_outs, *, tm=256, carry=None):
    nrows = rows[0][0].shape[0]
    tm = _row_tile(nrows, tm)
    n_r, n_v, n_ro, n_vo = len(rows), len(vecs), len(row_outs), len(vec_outs)

    def body(*refs):
        r_vals = [r[...] for r in refs[:n_r]]
        v_vals = [r[...] for r in refs[n_r:n_r + n_v]]
        ro_refs = refs[n_r + n_v:n_r + n_v + n_ro]
        vo_refs = refs[n_r + n_v + n_ro:]
        ro, vo = fn(r_vals, v_vals)
        for ref, val in zip(ro_refs, ro):
            ref[...] = val.astype(ref.dtype)
        if n_vo:
            @pl.when(pl.program_id(0) == 0)
            def _():
                for ref in vo_refs:
                    ref[...] = jnp.zeros_like(ref)

            for ref, val in zip(vo_refs, vo):
                ref[...] += val

    in_specs = [pl.BlockSpec((tm, cols), functools.partial(lambda i, cb: (i, cb), cb=cb)) for _, cols, cb in rows]
    in_specs += [pl.BlockSpec(v.shape, functools.partial(lambda i, nd: (0,) * nd, nd=v.ndim)) for v in vecs]
    out_specs = [pl.BlockSpec((tm, cols), lambda i: (i, 0)) for cols, _ in row_outs]
    out_specs += [pl.BlockSpec((1, cols), lambda i: (0, 0)) for cols in vec_outs]
    out_shape = [jax.ShapeDtypeStruct((nrows, cols), dt) for cols, dt in row_outs]
    out_shape += [jax.ShapeDtypeStruct((1, cols), F32) for cols in vec_outs]
    out, carried = _call(name, body, (nrows // tm,), in_specs, out_specs, out_shape, [],
                         [r[0] for r in rows] + list(vecs), carry)
    return out if carry is None else (out, carried)


def _whole(x):
    return (x, x.shape[1], 0)


def _colsum(x):
    return jnp.sum(x, axis=0, keepdims=True)


def _rstd(x):
    return lax.rsqrt(jnp.mean(x * x, axis=-1, keepdims=True) + EPS)


def _rms_bwd(dn, x, g):
    r = _rstd(x)
    c = dn * g
    dx = r * c - x * (r * r * r) * jnp.mean(c * x, axis=-1, keepdims=True)
    return dx, _colsum(dn * x * r)


def _rms_fwd(name, x, g, carry):
    def fn(r, v):
        (xv,), (gv,) = r, v
        return [xv * _rstd(xv) * gv], []

    (n,), carried = _rowwise(name, fn, [_whole(x)], [g], [(x.shape[1], BF16)], [], carry=carry)
    return n, carried


def _ep_post_res_pre(scale):
    def epilogue(acc, rows, vecs):
        (resid,), (g_post, g_next) = rows, vecs
        h = resid + scale * (acc * _rstd(acc) * g_post)
        return [acc, h, h * _rstd(h) * g_next], []

    return epilogue


def _post_bwd(dh, f, g_post, scale):
    return _rms_bwd(scale * dh, f, g_post)


def _ep_loss(scale, d):
    def epilogue(acc, rows, vecs):
        (resid, target), (g_post,) = rows, vecs
        err = resid + scale * (acc * _rstd(acc) * g_post) - target
        dy = err * (1.0 / d)
        df, dg_post = _post_bwd(dy, acc, g_post, scale)
        return [dy, df], [_colsum(err * err), dg_post]

    return epilogue


def _ep_pre_bwd_post(scale_prev):
    def epilogue(acc, rows, vecs):
        (h, dh_up, f_prev), (g_pre, g_post_prev) = rows, vecs
        dx, dg_pre = _rms_bwd(acc, h, g_pre)
        dh = dh_up + dx
        df, dg_post = _post_bwd(dh, f_prev, g_post_prev, scale_prev)
        return [dh, df], [dg_pre, dg_post]

    return epilogue


def _ep_pre_bwd_first():
    def epilogue(acc, rows, vecs):
        (x, dh_up), (g_pre,) = rows, vecs
        dx, dg_pre = _rms_bwd(acc, x, g_pre)
        return [dh_up + dx], [dg_pre]

    return epilogue


def _gate_specs(d, tm):
    first = (3 * D_ATTN + 2 * D_CONV) // COL
    return [pl.BlockSpec((tm, COL), functools.partial(lambda i, cb: (i, cb), cb=first + p)) for p in range(2 * d // COL)]


def _gate(piece_refs, bias_ref, c0, width):
    p, off = divmod(c0, COL)
    return jax.nn.sigmoid(piece_refs[p][:, off:off + width].astype(F32) + bias_ref[:, c0:c0 + width])


def _resident(w):
    return pl.BlockSpec(w.shape, functools.partial(lambda i, nd: (0,) * nd, nd=w.ndim), pipeline_mode=pl.Buffered(1))


def _mix_merge(att, cs, wao, wco, proj, gate_bias, tm=512):
    t = att.shape[0]
    nj, _, nb = wao.shape
    d = nj * nb
    tm = _row_tile(t, tm)
    gate_specs = _gate_specs(d, tm)
    n_p = len(gate_specs)

    def body(att_ref, cs_ref, wao_ref, wco_ref, *rest):
        pieces, (gb_ref, ya_ref, yb_ref, m_ref) = rest[:n_p], rest[n_p:]
        av, cv = att_ref[...], cs_ref[...]
        for j in range(nj):
            cols = slice(j * nb, (j + 1) * nb)
            ya = _dot(av, wao_ref[j], False)
            yb = _dot(cv, wco_ref[j], False)
            merged = _gate(pieces, gb_ref, j * nb, nb) * ya + _gate(pieces, gb_ref, d + j * nb, nb) * yb
            ya_ref[:, cols] = ya.astype(ya_ref.dtype)
            yb_ref[:, cols] = yb.astype(yb_ref.dtype)
            m_ref[:, cols] = merged.astype(m_ref.dtype)

    row = lambda x: pl.BlockSpec((tm, x.shape[1]), lambda i: (i, 0))
    out_spec = pl.BlockSpec((tm, d), lambda i: (i, 0))
    return pl.pallas_call(
        body, name="mix_merge", grid=(t // tm,),
        in_specs=[row(att), row(cs), _resident(wao), _resident(wco)] + gate_specs + [_resident(gate_bias)],
        out_specs=[out_spec] * 3, out_shape=[jax.ShapeDtypeStruct((t, d), BF16)] * 3, compiler_params=_cparams(1),
    )(att, cs, wao, wco, *([proj] * n_p), gate_bias)


def _mix_d_merge(dmo, wout, y_a, y_b, wao, wco, proj, gate_bias, tm=512):
    t, d = dmo.shape
    nj, _, nb = wao.shape
    ka, kc = wao.shape[1], wco.shape[1]
    tm = _row_tile(t, tm)
    gate_specs = _gate_specs(d, tm)
    n_p = len(gate_specs)

    def body(dmo_ref, wout_ref, ya_ref, yb_ref, wao_ref, wco_ref, *rest):
        pieces, (gb_ref, dya_ref, dyb_ref, dg_ref, datt_ref, dcs_ref, dgb_ref) = rest[:n_p], rest[n_p:]

        @pl.when(pl.program_id(0) == 0)
        def _():
            dgb_ref[...] = jnp.zeros_like(dgb_ref)

        dmo_v = dmo_ref[...]
        datt = dcs = None
        for j in range(nj):
            cols, cols_b = slice(j * nb, (j + 1) * nb), slice(d + j * nb, d + (j + 1) * nb)
            dm = _dot(dmo_v, wout_ref[j], True)
            ga, gb = _gate(pieces, gb_ref, j * nb, nb), _gate(pieces, gb_ref, d + j * nb, nb)
            dya, dyb = (dm * ga).astype(BF16), (dm * gb).astype(BF16)
            dga = dm * ya_ref[:, cols].astype(F32) * ga * (1.0 - ga)
            dgb = dm * yb_ref[:, cols].astype(F32) * gb * (1.0 - gb)
            dya_ref[:, cols], dyb_ref[:, cols] = dya, dyb
            dg_ref[:, cols], dg_ref[:, cols_b] = dga.astype(dg_ref.dtype), dgb.astype(dg_ref.dtype)
            dgb_ref[:, cols] += _colsum(dga)
            dgb_ref[:, cols_b] += _colsum(dgb)
            pa, pc = _dot(dya, wao_ref[j], True), _dot(dyb, wco_ref[j], True)
            datt, dcs = (pa, pc) if datt is None else (datt + pa, dcs + pc)
        datt_ref[...] = datt.astype(datt_ref.dtype)
        dcs_ref[...] = dcs.astype(dcs_ref.dtype)

    row = lambda cols: pl.BlockSpec((tm, cols), lambda i: (i, 0))
    return pl.pallas_call(
        body, name="mix_d_merge", grid=(t // tm,),
        in_specs=[row(d), _resident(wout), row(d), row(d), _resident(wao), _resident(wco)] + gate_specs
        + [_resident(gate_bias)],
        out_specs=[row(d), row(d), row(2 * d), row(ka), row(kc), pl.BlockSpec((1, 2 * d), lambda i: (0, 0))],
        out_shape=[jax.ShapeDtypeStruct((t, d), BF16), jax.ShapeDtypeStruct((t, d), BF16),
                   jax.ShapeDtypeStruct((t, 2 * d), BF16), jax.ShapeDtypeStruct((t, ka), BF16),
                   jax.ShapeDtypeStruct((t, kc), F32), jax.ShapeDtypeStruct((1, 2 * d), F32)],
        compiler_params=_cparams(1),
    )(dmo, wout, y_a, y_b, wao, wco, *([proj] * n_p), gate_bias)


def _adamw_math(wv, gv, mv, vv):
    m2 = ADAM_B1 * mv + (1.0 - ADAM_B1) * gv
    v2 = ADAM_B2 * vv + (1.0 - ADAM_B2) * (gv * gv)
    m_hat = m2 / (1.0 - ADAM_B1 ** ADAM_STEP)
    v_hat = v2 / (1.0 - ADAM_B2 ** ADAM_STEP)
    delta = -ADAM_LR * (m_hat / (jnp.sqrt(v_hat) + ADAM_EPS) + ADAM_WD * wv)
    return delta, m2, v2


def _adamw(name, w, g, m, v):
    def fn(r, _):
        return list(_adamw_math(*r)), []

    c = w.shape[1]
    return _rowwise(name, fn, [_whole(w), _whole(g), _whole(m), _whole(v)], [], [(c, F32)] * 3, [], tm=256)


POS_C, POS_CHIP, POS_PEER = 0, 1, 2


def _placed_call(body, name, pos, grid, in_specs, out_specs, out_shape, args):
    return pl.pallas_call(
        body, name=name, out_shape=out_shape, compiler_params=_cparams(len(grid)),
        grid_spec=pltpu.PrefetchScalarGridSpec(num_scalar_prefetch=1, grid=grid, in_specs=in_specs,
                                               out_specs=out_specs),
    )(pos, *args)


def _cast_into(name, pos, w):
    r, cols = w.shape
    tm = _row_tile(r, 1024)

    def body(pos_ref, w_ref, o_ref):
        o_ref[...] = w_ref[...].astype(o_ref.dtype)

    return _placed_call(
        body, name, pos, (r // tm,), [pl.BlockSpec((tm, cols), lambda i, pos: (i, 0))],
        pl.BlockSpec((None, tm, cols), lambda i, pos: (pos[POS_CHIP], i, 0)),
        jax.ShapeDtypeStruct((N_CHIPS, r, cols), BF16), [w])


def _add_pair(name, pos, grad, landed):
    nj, half, cols = landed.shape
    tm = _row_tile(half, 512)
    nb = half // tm

    def body(pos_ref, g_ref, l_ref, o_ref):
        o_ref[...] = (g_ref[...].astype(F32) + l_ref[...].astype(F32)).astype(o_ref.dtype)

    spec = pl.BlockSpec((None, tm, cols), lambda j, i, pos: (j, i, 0))
    return _placed_call(
        body, name, pos, (nj, nb),
        [pl.BlockSpec((None, tm, cols), lambda j, i, pos: (j, pos[POS_C] * nb + i, 0)), spec], spec,
        jax.ShapeDtypeStruct(landed.shape, BF16), [grad, landed])


def _add_chips(name, pos, part, landed):
    _, half, cols = landed.shape
    tm = _row_tile(half, 512)

    def body(pos_ref, p_ref, l0_ref, l1_ref, l2_ref, o_ref):
        acc = p_ref[...].astype(F32)
        for ref in (l0_ref, l1_ref, l2_ref):
            acc = acc + ref[...].astype(F32)
        o_ref[...] = acc

    slot = lambda at: pl.BlockSpec((None, tm, cols), functools.partial(lambda i, pos, at: (pos[at], i, 0), at=at))
    return _placed_call(
        body, name, pos, (half // tm,), [slot(POS_CHIP)] + [slot(POS_PEER + k) for k in range(3)],
        pl.BlockSpec((tm, cols), lambda i, pos: (i, 0)), jax.ShapeDtypeStruct((half, cols), F32),
        [part, landed, landed, landed])


def _adamw_halves(name, pos, w, m, v, own, landed):
    r, cols = w.shape
    half = own.shape[0]
    tm = _row_tile(half, 256)
    nb = half // tm

    def body(pos_ref, w_ref, m_ref, v_ref, own_ref, land_ref, g_out, d_out, m_out, v_out):
        mine = pl.program_id(0) == pos_ref[POS_C]
        g = jnp.where(mine, own_ref[...], land_ref[...])
        delta, m2, v2 = _adamw_math(w_ref[...], g, m_ref[...], v_ref[...])
        g_out[...] = g
        d_out[...] = delta
        m_out[...] = m2
        v_out[...] = v2

    full = pl.BlockSpec((tm, cols), lambda h, i, pos: (h * nb + i, 0))
    part = pl.BlockSpec((tm, cols), lambda h, i, pos: (i, 0))
    return _placed_call(
        body, name, pos, (2, nb), [full, full, full, part, part], [full] * 4,
        [jax.ShapeDtypeStruct((r, cols), F32)] * 4, [w, m, v, own, landed])


N_START = K_PAD // Q_BLOCK


def _rel_onehot(n_q):
    e = np.arange(REL_EXT)
    dist = K_PAD - (e - (n_q - 1))
    idx = np.clip(dist, -REL_CLIP, REL_CLIP) + REL_CLIP
    return (np.arange(REL_PAD)[:, None] == idx[None, :]).astype(np.float32)


def _skew(x, left):
    row = lax.broadcasted_iota(jnp.int32, x.shape, 0)
    for bit in range(x.shape[0].bit_length() - 1):
        amount = 1 << bit
        rolled = pltpu.roll(x, REL_EXT - amount if left else amount, 1)
        x = jnp.where((row >> bit) & 1 == 1, rolled, x)
    return x


def _bias_expand(table_pad, carry):
    onehot = jnp.asarray(_rel_onehot(Q_BLOCK))

    def core(t_ref, oh_ref, o_ref):
        ext = jnp.dot(t_ref[...], oh_ref[...], precision=lax.Precision.HIGHEST, preferred_element_type=F32)
        qc = lax.broadcasted_iota(jnp.int32, (Q_BLOCK, K_WIN), 0) // CHUNK
        kpos = lax.broadcasted_iota(jnp.int32, (Q_BLOCK, K_WIN), 1)
        band = (kpos // CHUNK >= qc) & (kpos // CHUNK <= qc + LEFT_CHUNKS)
        rows = jnp.broadcast_to(ext, (Q_BLOCK, REL_EXT))
        rolled = _skew(pltpu.roll(rows, REL_EXT - (Q_BLOCK - 1), 1), left=False)[:, :K_WIN]
        for v in range(N_START + 1):
            o_ref[v] = jnp.where(band & (kpos + v * Q_BLOCK >= K_PAD), rolled, NEG)

    (bias,), carried = _call(
        "bias_expand", core, (N_HEADS,),
        [pl.BlockSpec((None, 1, REL_PAD), lambda h: (h, 0, 0)), pl.BlockSpec(onehot.shape, lambda h: (0, 0))],
        [pl.BlockSpec((N_START + 1, None, Q_BLOCK, K_WIN), lambda h: (0, h, 0, 0))],
        [jax.ShapeDtypeStruct((N_START + 1, N_HEADS, Q_BLOCK, K_WIN), F32)], [],
        [table_pad.reshape(N_HEADS, 1, REL_PAD), onehot], carry)
    return bias, carried


def _bias_fold(dbias):
    onehot_t = jnp.asarray(_rel_onehot(CHUNK).T)

    def body(d_ref, oh_ref, o_ref, ext_ref):
        for h in range(N_HEADS):
            x = jnp.concatenate([d_ref[h], jnp.zeros((CHUNK, REL_EXT - K_WIN), F32)], axis=1)
            rolled = _skew(pltpu.roll(x, CHUNK - 1, 1), left=True)
            ext_ref[h:h + 1, :] = jnp.sum(rolled, axis=0, keepdims=True)
        o_ref[...] = jnp.dot(ext_ref[...], oh_ref[...], precision=lax.Precision.HIGHEST,
                             preferred_element_type=F32)

    return pl.pallas_call(
        body, name="bias_fold", out_shape=jax.ShapeDtypeStruct((N_HEADS, REL_PAD), F32),
        scratch_shapes=[pltpu.VMEM((N_HEADS, REL_EXT), F32)],
        compiler_params=pltpu.CompilerParams(vmem_limit_bytes=VMEM_LIMIT_BYTES),
    )(dbias, onehot_t)


def _head_lanes():
    lane = lax.broadcasted_iota(jnp.int32, (1, 2 * HEAD_DIM), 1)
    return [lane < HEAD_DIM, lane >= HEAD_DIM]


def _only(mask, x, scale=None):
    x = jnp.where(mask, x, jnp.zeros_like(x))
    return x if scale is None else x * scale


def _contract_lanes(a, b):
    return lax.dot_general(a, b, (((1,), (1,)), ((), ())), preferred_element_type=F32)


def _contract_rows(a, b):
    return lax.dot_general(a, b, (((0,), (0,)), ((), ())), preferred_element_type=F32)


PAIR = 2 * HEAD_DIM
N_PAIRS = D_ATTN // PAIR


def _attn_specs(pairs):
    width = pairs * PAIR
    per = D_ATTN // width
    row_spec = pl.BlockSpec((Q_BLOCK, width), lambda g, i: (i, g))
    kv_specs = [pl.BlockSpec((Q_BLOCK, width),
                             functools.partial(lambda g, i, kk, c0: (jnp.maximum(i + kk - N_START, 0), c0 + g),
                                               kk=kk, c0=c0))
                for c0 in (per, 2 * per) for kk in range(K_WIN // Q_BLOCK)]
    bias_spec = pl.BlockSpec((None, 2 * pairs, Q_BLOCK, K_WIN), lambda g, i: (jnp.minimum(i, N_START), g, 0, 0))
    return row_spec, kv_specs, bias_spec


def _attn_fwd(proj, bias, pairs=N_PAIRS):
    t = proj.shape[0]
    n_win = K_WIN // Q_BLOCK

    def body(q_ref, *refs):
        k_refs, v_refs = refs[:n_win], refs[n_win:2 * n_win]
        b_ref, o_ref, lse_ref = refs[2 * n_win:]
        for pp in range(pairs):
            cols = slice(pp * PAIR, (pp + 1) * PAIR)
            k = jnp.concatenate([r[:, cols] for r in k_refs], axis=0)
            v = jnp.concatenate([r[:, cols] for r in v_refs], axis=0)
            q = q_ref[:, cols]
            o = lse = None
            for hh, lanes in enumerate(_head_lanes()):
                s = _contract_lanes(_only(lanes, q, HEAD_DIM ** -0.5), k) + b_ref[2 * pp + hh]
                m = jnp.max(s, axis=1, keepdims=True)
                p = jnp.exp(s - m)
                l = jnp.sum(p, axis=1, keepdims=True)
                oh = jnp.dot(p.astype(BF16), v, preferred_element_type=F32) / l
                lse_h = jnp.broadcast_to(m + jnp.log(l), oh.shape)
                o, lse = (oh, lse_h) if o is None else (jnp.where(lanes, oh, o), jnp.where(lanes, lse_h, lse))
            o_ref[:, cols] = o.astype(o_ref.dtype)
            lse_ref[:, cols] = lse

    row_spec, kv_specs, bias_spec = _attn_specs(pairs)
    return pl.pallas_call(
        body, name="attn_fwd", grid=(N_PAIRS // pairs, t // Q_BLOCK),
        in_specs=[row_spec] + kv_specs + [bias_spec], out_specs=[row_spec, row_spec],
        out_shape=[jax.ShapeDtypeStruct((t, D_ATTN), BF16), jax.ShapeDtypeStruct((t, D_ATTN), F32)],
        compiler_params=_cparams(2),
    )(*([proj] * (1 + 2 * n_win)), bias)


def _attn_bwd(proj, bias, att, lse, datt, pairs=2):
    t = proj.shape[0]
    n_win = K_WIN // Q_BLOCK
    n_blocks = t // Q_BLOCK

    def body(q_ref, *refs):
        k_refs, v_refs = refs[:n_win], refs[n_win:2 * n_win]
        b_ref, o_ref, lse_ref, do_ref, dq_ref, dk_ref, dv_ref, db_ref, dk_acc, dv_acc = refs[2 * n_win:]
        i = pl.program_id(1)

        @pl.when(i == 0)
        def _():
            dk_acc[...] = jnp.zeros_like(dk_acc)
            dv_acc[...] = jnp.zeros_like(dv_acc)
            db_ref[...] = jnp.zeros_like(db_ref)

        rows = pl.ds(pl.multiple_of(i * Q_BLOCK, Q_BLOCK), K_WIN)
        scale = HEAD_DIM ** -0.5
        for pp in range(pairs):
            cols = slice(pp * PAIR, (pp + 1) * PAIR)
            k = jnp.concatenate([r[:, cols] for r in k_refs], axis=0)
            v = jnp.concatenate([r[:, cols] for r in v_refs], axis=0)
            q, do, o = q_ref[:, cols], do_ref[:, cols], o_ref[:, cols].astype(F32)
            dq = dk = dv = None
            for hh, lanes in enumerate(_head_lanes()):
                qh, doh = _only(lanes, q, scale), _only(lanes, do)
                s = _contract_lanes(qh, k) + b_ref[2 * pp + hh]
                lse_col = pp * PAIR + hh * HEAD_DIM
                p = jnp.exp(s - lse_ref[:, lse_col:lse_col + 1])
                delta = jnp.sum(doh.astype(F32) * o, axis=1, keepdims=True)
                ds = p * (_contract_lanes(doh, v) - delta)
                folded = ds[:CHUNK]
                for c in range(1, Q_BLOCK // CHUNK):
                    folded = folded + pltpu.roll(ds[c * CHUNK:(c + 1) * CHUNK], K_WIN - c * CHUNK, 1)
                db_ref[2 * pp + hh] += folded
                dsb = ds.astype(BF16)
                dqh = jnp.dot(dsb, k, preferred_element_type=F32)
                dq = dqh if dq is None else jnp.where(lanes, dqh, dq)
                dkh, dvh = _contract_rows(dsb, qh), _contract_rows(p.astype(BF16), doh)
                dk, dv = (dkh, dvh) if dk is None else (dk + dkh, dv + dvh)
            dq_ref[:, cols] = (dq * scale).astype(dq_ref.dtype)
            dk_acc[rows, cols] += dk
            dv_acc[rows, cols] += dv

        @pl.when(i == n_blocks - 1)
        def _():
            dk_ref[...] = dk_acc[K_PAD:, :].astype(dk_ref.dtype)
            dv_ref[...] = dv_acc[K_PAD:, :].astype(dv_ref.dtype)

    width = pairs * PAIR
    row_spec, kv_specs, bias_spec = _attn_specs(pairs)
    full_spec = pl.BlockSpec((t, width), lambda g, i: (0, g))
    return pl.pallas_call(
        body, name="attn_bwd", grid=(N_PAIRS // pairs, n_blocks),
        in_specs=[row_spec] + kv_specs + [bias_spec, row_spec, row_spec, row_spec],
        out_specs=[row_spec, full_spec, full_spec,
                   pl.BlockSpec((2 * pairs, CHUNK, K_WIN), lambda g, i: (g, 0, 0))],
        out_shape=[jax.ShapeDtypeStruct((t, D_ATTN), BF16)] * 3 + [jax.ShapeDtypeStruct((N_HEADS, CHUNK, K_WIN), F32)],
        scratch_shapes=[pltpu.VMEM((t + K_PAD, width), F32)] * 2, compiler_params=_cparams(2),
    )(*([proj] * (1 + 2 * n_win)), bias, att, lse, datt)


CONV_LEAD = CONV_HALO - (CONV_WIDTH - 1)
CONV_LANES = 128
CONV_ROWS = 64


def _conv_specs(t):
    tt = _row_tile(t, CONV_TILE)
    per = tt // CONV_HALO
    n_halo = t // CONV_HALO
    tile = lambda cb: pl.BlockSpec((tt, COL), functools.partial(lambda i, cb: (i, cb), cb=cb))
    prev = lambda cb: pl.BlockSpec((CONV_HALO, COL),
                                   functools.partial(lambda i, cb: (jnp.maximum(i * per - 1, 0), cb), cb=cb))
    nxt = lambda cb: pl.BlockSpec((CONV_HALO, COL),
                                  functools.partial(lambda i, cb: (jnp.minimum((i + 1) * per, n_halo - 1), cb), cb=cb))
    vec = lambda shape: pl.BlockSpec(shape, lambda i: (0, 0))
    return tt, tile, prev, nxt, vec


def _glu(ca, cg, bias):
    return (ca.astype(F32) + bias[:, :D_CONV]) * jax.nn.sigmoid(cg.astype(F32) + bias[:, D_CONV:])


SUBLANES = 8


def _shift_copies(ext_ref):
    n = ext_ref.shape[1] - SUBLANES
    for s in range(1, SUBLANES):
        ext_ref[s, 0:n, :] = ext_ref[0, s:s + n, :]


def _tap_tiles(ext_ref, first_row, r0, lanes):
    n_g = CONV_ROWS // SUBLANES
    for s in range(SUBLANES):
        taps = [w for w in range(CONV_WIDTH) if first_row(w) % SUBLANES == s]
        if not taps:
            continue
        lo = min(first_row(w) for w in taps) - s
        n_tiles = (max(first_row(w) for w in taps) - s - lo) // SUBLANES + n_g
        tiles = [ext_ref[s, r0 + lo + SUBLANES * b:r0 + lo + SUBLANES * (b + 1), lanes] for b in range(n_tiles)]
        for w in taps:
            k = (first_row(w) - s - lo) // SUBLANES
            yield w, tiles[k:k + n_g]


def _taps(ext_ref, tt, first_row, w_ref, out_ref):
    n_g = CONV_ROWS // SUBLANES
    for l0 in range(0, D_CONV, CONV_LANES):
        lanes = slice(l0, l0 + CONV_LANES)
        for r0 in range(0, tt, CONV_ROWS):
            acc = [jnp.zeros((SUBLANES, CONV_LANES), F32)] * n_g
            for w, tiles in _tap_tiles(ext_ref, first_row, r0, lanes):
                weight = jnp.broadcast_to(w_ref[w:w + 1, lanes], (SUBLANES, CONV_LANES))
                acc = [a + t * weight for a, t in zip(acc, tiles)]
            for g in range(n_g):
                out_ref[r0 + SUBLANES * g:r0 + SUBLANES * (g + 1), lanes] = acc[g]


def _tap_sums(ext_ref, tt, first_row, x_ref, out_ref):
    n_g = CONV_ROWS // SUBLANES
    for l0 in range(0, D_CONV, CONV_LANES):
        lanes = slice(l0, l0 + CONV_LANES)
        acc = [jnp.zeros((SUBLANES, CONV_LANES), F32)] * CONV_WIDTH
        for r0 in range(0, tt, CONV_ROWS):
            x = [x_ref[0, r0 + SUBLANES * g:r0 + SUBLANES * (g + 1), lanes] for g in range(n_g)]
            for w, tiles in _tap_tiles(ext_ref, first_row, r0, lanes):
                part = tiles[0] * x[0]
                for g in range(1, n_g):
                    part = part + tiles[g] * x[g]
                acc[w] = acc[w] + part
        for w in range(CONV_WIDTH):
            out_ref[w:w + 1, lanes] += jnp.sum(acc[w], axis=0, keepdims=True)


def _conv_fwd(proj, glu_bias, dw, dw_b, ln_g, ln_b):
    t = proj.shape[0]
    tt, tile, prev, nxt, vec = _conv_specs(t)
    ca_blk, cg_blk = 3 * D_ATTN // COL, 3 * D_ATTN // COL + 1

    def body(ca_ref, cg_ref, pa_ref, pg_ref, gb_ref, dw_ref, dwb_ref, g_ref, b_ref, cs_ref, c_ref, z_ref, ext_ref):
        i = pl.program_id(0)
        bias = gb_ref[...]
        c = _glu(ca_ref[...], cg_ref[...], bias)
        halo = _glu(pa_ref[...], pg_ref[...], bias)
        ext_ref[0, 0:CONV_HALO, :] = jnp.where(i == 0, 0.0, halo)
        ext_ref[0, CONV_HALO:, :] = c
        _shift_copies(ext_ref)
        c_ref[...] = c
        _taps(ext_ref, tt, lambda w: CONV_LEAD + w, dw_ref, z_ref)
        z = z_ref[...] + dwb_ref[...]
        z_ref[...] = z
        mu = jnp.mean(z, axis=-1, keepdims=True)
        zc = z - mu
        y = zc * lax.rsqrt(jnp.mean(zc * zc, axis=-1, keepdims=True) + EPS) * g_ref[...] + b_ref[...]
        cs_ref[...] = (y * jax.nn.sigmoid(y)).astype(cs_ref.dtype)

    out_spec = pl.BlockSpec((tt, D_CONV), lambda i: (i, 0))
    return pl.pallas_call(
        body, name="conv_fwd", grid=(t // tt,),
        in_specs=[tile(ca_blk), tile(cg_blk), prev(ca_blk), prev(cg_blk), vec(glu_bias.shape), vec(dw.shape),
                  vec(dw_b.shape), vec(ln_g.shape), vec(ln_b.shape)],
        out_specs=[out_spec] * 3,
        out_shape=[jax.ShapeDtypeStruct((t, D_CONV), BF16), jax.ShapeDtypeStruct((t, D_CONV), F32),
                   jax.ShapeDtypeStruct((t, D_CONV), F32)],
        scratch_shapes=[pltpu.VMEM((SUBLANES, tt + CONV_HALO, D_CONV), F32)], compiler_params=_cparams(1),
    )(proj, proj, proj, proj, glu_bias, dw, dw_b, ln_g, ln_b)


def _conv_bwd(proj, c, z, dcs, glu_bias, dw, ln_g, ln_b):
    t = proj.shape[0]
    tt, tile, prev, nxt, vec = _conv_specs(t)
    n_tiles = t // tt
    ca_blk, cg_blk = 3 * D_ATTN // COL, 3 * D_ATTN // COL + 1

    def ln_bwd(zv, dcsv, g, b):
        mu = jnp.mean(zv, axis=-1, keepdims=True)
        zc = zv - mu
        rstd = lax.rsqrt(jnp.mean(zc * zc, axis=-1, keepdims=True) + EPS)
        zhat = zc * rstd
        y = zhat * g + b
        sig = jax.nn.sigmoid(y)
        dy = dcsv * sig * (1.0 + y * (1.0 - sig))
        dzh = dy * g
        dz = rstd * (dzh - jnp.mean(dzh, axis=-1, keepdims=True) - zhat * jnp.mean(dzh * zhat, axis=-1, keepdims=True))
        return dz, dy, zhat

    def body(ca_ref, cg_ref, c_ref, cprev_ref, z_ref, znext_ref, dcs_ref, dcsnext_ref, gb_ref, dw_ref, g_ref, b_ref,
             dcin_ref, ddw_ref, ddwb_ref, dg_ref, db_ref, dgb_ref, cext_ref, dzext_ref, dc_ref):
        i = pl.program_id(0)

        @pl.when(i == 0)
        def _():
            for ref in (ddw_ref, ddwb_ref, dg_ref, db_ref, dgb_ref):
                ref[...] = jnp.zeros_like(ref)

        g, b = g_ref[...], b_ref[...]
        dz, dy, zhat = ln_bwd(z_ref[...], dcs_ref[...], g, b)
        dz_next, _, _ = ln_bwd(znext_ref[...], dcsnext_ref[...], g, b)
        dg_ref[...] += _colsum(dy * zhat)
        db_ref[...] += _colsum(dy)
        ddwb_ref[...] += _colsum(dz)
        dzext_ref[0, 0:tt, :] = dz
        dzext_ref[0, tt:, :] = jnp.where(i == n_tiles - 1, 0.0, dz_next)
        _shift_copies(dzext_ref)
        cext_ref[0, 0:CONV_HALO, :] = jnp.where(i == 0, 0.0, cprev_ref[...])
        cext_ref[0, CONV_HALO:, :] = c_ref[...]
        _shift_copies(cext_ref)
        _taps(dzext_ref, tt, lambda w: CONV_WIDTH - 1 - w, dw_ref, dc_ref)
        _tap_sums(cext_ref, tt, lambda w: CONV_LEAD + w, dzext_ref, ddw_ref)
        bias = gb_ref[...]
        a_in = ca_ref[...].astype(F32) + bias[:, :D_CONV]
        sg = jax.nn.sigmoid(cg_ref[...].astype(F32) + bias[:, D_CONV:])
        dc = dc_ref[...]
        dcin = jnp.concatenate([dc * sg, dc * a_in * sg * (1.0 - sg)], axis=1)
        dcin_ref[...] = dcin.astype(dcin_ref.dtype)
        dgb_ref[...] += _colsum(dcin)

    row = lambda: pl.BlockSpec((tt, D_CONV), lambda i: (i, 0))
    per = tt // CONV_HALO
    n_halo = t // CONV_HALO
    prev_row = pl.BlockSpec((CONV_HALO, D_CONV), lambda i: (jnp.maximum(i * per - 1, 0), 0))
    next_row = lambda: pl.BlockSpec((CONV_HALO, D_CONV), lambda i: (jnp.minimum((i + 1) * per, n_halo - 1), 0))
    acc = lambda shape: pl.BlockSpec(shape, lambda i: (0, 0))
    return pl.pallas_call(
        body, name="conv_bwd", grid=(n_tiles,),
        in_specs=[tile(ca_blk), tile(cg_blk), row(), prev_row, row(), next_row(), row(), next_row(),
                  vec(glu_bias.shape), vec(dw.shape), vec(ln_g.shape), vec(ln_b.shape)],
        out_specs=[pl.BlockSpec((tt, 2 * D_CONV), lambda i: (i, 0)), acc(dw.shape), acc((1, D_CONV)),
                   acc((1, D_CONV)), acc((1, D_CONV)), acc((1, 2 * D_CONV))],
        out_shape=[jax.ShapeDtypeStruct((t, 2 * D_CONV), BF16), jax.ShapeDtypeStruct(dw.shape, F32),
                   jax.ShapeDtypeStruct((1, D_CONV), F32), jax.ShapeDtypeStruct((1, D_CONV), F32),
                   jax.ShapeDtypeStruct((1, D_CONV), F32), jax.ShapeDtypeStruct((1, 2 * D_CONV), F32)],
        scratch_shapes=[pltpu.VMEM((SUBLANES, tt + CONV_HALO, D_CONV), F32),
                        pltpu.VMEM((SUBLANES, tt + CONV_HALO, D_CONV), F32), pltpu.VMEM((tt, D_CONV), F32)],
        compiler_params=_cparams(1),
    )(proj, proj, c, c, z, z, dcs, dcs, glu_bias, dw, ln_g, ln_b)


def _place():
    x, y, c = lax.axis_index("x"), lax.axis_index("y"), lax.axis_index("c")
    chips = [(1 - x, y), (x, 1 - y), (1 - x, 1 - y)]
    return x, y, c, chips


def _chip_index(chip):
    return 2 * chip[0] + chip[1]


def _half_rows(c, half):
    return pl.ds(pl.multiple_of(c * half, 16), half)


def _gather_carry(blocked):
    n = len(blocked)

    def over_ici(o_refs, send_sems, recv_sems):
        x, y, c, chips = _place()
        me = _chip_index((x, y))
        copies = []
        for a in range(n):
            mine = o_refs[a].at[me, _half_rows(c, o_refs[a].shape[1] // 2), :]
            for k, chip in enumerate(chips):
                copies.append(pltpu.make_async_remote_copy(
                    src_ref=mine, dst_ref=mine, send_sem=send_sems.at[6 * a + k], recv_sem=recv_sems.at[6 * a + k],
                    device_id=(chip[0], chip[1], c), device_id_type=MESH))
        return copies

    def to_sibling(o_refs, send_sems, recv_sems, sent_by_me):
        x, y, c, chips = _place()
        copies = []
        for a in range(n):
            rows = _half_rows(c if sent_by_me else 1 - c, o_refs[a].shape[1] // 2)
            for k, chip in enumerate(chips):
                landed = o_refs[a].at[_chip_index(chip), rows, :]
                copies.append(pltpu.make_async_remote_copy(
                    src_ref=landed, dst_ref=landed, send_sem=send_sems.at[6 * a + 3 + k],
                    recv_sem=recv_sems.at[6 * a + 3 + k], device_id=(x, y, 1 - c), device_id_type=MESH))
        return copies

    def start(ins, outs, sems):
        for cp in over_ici(outs, *sems):
            cp.start()

    def hand_on(ins, outs, sems):
        for arrived, onward in zip(over_ici(outs, *sems), to_sibling(outs, *sems, True)):
            arrived.wait_recv()
            onward.start()

    def finish(ins, outs, sems):
        for cp in to_sibling(outs, *sems, False):
            cp.wait_recv()
        for cp in over_ici(outs, *sems) + to_sibling(outs, *sems, True):
            cp.wait_send()

    return _Carry(
        ins=list(blocked), outs=[jax.ShapeDtypeStruct(w.shape, w.dtype) for w in blocked],
        aliases={a: a for a in range(n)},
        sems=[pltpu.SemaphoreType.DMA((6 * n,)), pltpu.SemaphoreType.DMA((6 * n,))],
        phases=[("first", start), ("late", hand_on), ("last", finish)])


def _pair_exchange(name, grads):
    n = len(grads)

    def body(*refs):
        g_refs, land_refs = refs[:n], refs[n:2 * n]
        send_sems, recv_sems = refs[2 * n:]
        x, y, c, _ = _place()
        copies = []
        for a in range(n):
            half = g_refs[a].shape[1] // 2
            cp = pltpu.make_async_remote_copy(
                src_ref=g_refs[a].at[:, _half_rows(1 - c, half), :], dst_ref=land_refs[a],
                send_sem=send_sems.at[a], recv_sem=recv_sems.at[a], device_id=(x, y, 1 - c), device_id_type=MESH)
            cp.start()
            copies.append(cp)
        for cp in copies:
            cp.wait()

    return pl.pallas_call(
        body, name=name, in_specs=[ANY] * n, out_specs=[ANY] * n,
        out_shape=[jax.ShapeDtypeStruct((g.shape[0], g.shape[1] // 2, g.shape[2]), g.dtype) for g in grads],
        scratch_shapes=[pltpu.SemaphoreType.DMA((n,)), pltpu.SemaphoreType.DMA((n,))],
    )(*grads)


def _to_owner_carry(parts):
    n = len(parts)

    def sends(p_refs, l_refs, send_sems, recv_sems):
        x, y, c, chips = _place()
        me = _chip_index((x, y))
        return [pltpu.make_async_remote_copy(
            src_ref=p_refs[a].at[_chip_index(chip)], dst_ref=l_refs[a].at[me],
            send_sem=send_sems.at[3 * a + k], recv_sem=recv_sems.at[3 * a + k],
            device_id=(chip[0], chip[1], c), device_id_type=MESH) for a in range(n) for k, chip in enumerate(chips)]

    def start(ins, outs, sems):
        for cp in sends(ins, outs, *sems):
            cp.start()

    def finish(ins, outs, sems):
        x, y, c, chips = _place()
        send_sems, recv_sems = sems
        for a in range(n):
            for k, chip in enumerate(chips):
                slot = outs[a].at[_chip_index(chip)]
                pltpu.make_async_remote_copy(
                    src_ref=slot, dst_ref=slot, send_sem=send_sems.at[3 * a + k], recv_sem=recv_sems.at[3 * a + k],
                    device_id=(chip[0], chip[1], c), device_id_type=MESH).wait_recv()
        for cp in sends(ins, outs, *sems):
            cp.wait_send()

    return _Carry(
        ins=list(parts), outs=[jax.ShapeDtypeStruct(p.shape, p.dtype) for p in parts], aliases={},
        sems=[pltpu.SemaphoreType.DMA((3 * n,)), pltpu.SemaphoreType.DMA((3 * n,))],
        phases=[("first", start), ("last", finish)])


def _swap_halves(halves):
    n = len(halves)

    def body(*refs):
        h_refs, o_refs = refs[:n], refs[n:2 * n]
        send_sems, recv_sems = refs[2 * n:]
        x, y, c, _ = _place()
        copies = []
        for a in range(n):
            cp = pltpu.make_async_remote_copy(
                src_ref=h_refs[a], dst_ref=o_refs[a], send_sem=send_sems.at[a], recv_sem=recv_sems.at[a],
                device_id=(x, y, 1 - c), device_id_type=MESH)
            cp.start()
            copies.append(cp)
        for cp in copies:
            cp.wait()

    return pl.pallas_call(
        body, name="grad_swap_halves", in_specs=[ANY] * n, out_specs=[ANY] * n,
        out_shape=[jax.ShapeDtypeStruct(h.shape, h.dtype) for h in halves],
        scratch_shapes=[pltpu.SemaphoreType.DMA((n,)), pltpu.SemaphoreType.DMA((n,))],
    )(*halves)


def _all_devices(name, block):
    r, cols = block.shape

    def body(b_ref, all_ref, sum_ref, send_sems, recv_sems):
        x, y, c, _ = _place()
        me = 4 * x + 2 * y + c
        all_ref[me] = b_ref[...]
        flips = [(fx, fy, fc) for fx in (0, 1) for fy in (0, 1) for fc in (0, 1)][1:]
        copies = []
        for k, (fx, fy, fc) in enumerate(flips):
            cp = pltpu.make_async_remote_copy(
                src_ref=b_ref, dst_ref=all_ref.at[me], send_sem=send_sems.at[k], recv_sem=recv_sems.at[k],
                device_id=(x ^ fx, y ^ fy, c ^ fc), device_id_type=MESH)
            cp.start()
            copies.append(cp)
        for k, (fx, fy, fc) in enumerate(flips):
            slot = all_ref.at[4 * (x ^ fx) + 2 * (y ^ fy) + (c ^ fc)]
            pltpu.make_async_remote_copy(
                src_ref=slot, dst_ref=slot, send_sem=send_sems.at[k], recv_sem=recv_sems.at[k],
                device_id=(x ^ fx, y ^ fy, c ^ fc), device_id_type=MESH).wait_recv()
        for cp in copies:
            cp.wait_send()
        acc = all_ref[0]
        for d in range(1, N_DEV):
            acc = acc + all_ref[d]
        sum_ref[...] = acc

    vmem = pl.BlockSpec(memory_space=pltpu.VMEM)
    return pl.pallas_call(
        body, name=name, in_specs=[vmem], out_specs=[vmem, vmem],
        out_shape=[jax.ShapeDtypeStruct((N_DEV, r, cols), F32), jax.ShapeDtypeStruct((r, cols), F32)],
        scratch_shapes=[pltpu.SemaphoreType.DMA((N_DEV - 1,)), pltpu.SemaphoreType.DMA((N_DEV - 1,))],
    )(block)


PACK = 1024


def _packed_rows(shape, width):
    size, last = int(np.prod(shape)), shape[-1]
    cols = last if last <= width else width
    assert size % cols == 0
    return size // cols, cols


def _pack(vals, width=PACK):
    rows = []
    for v in vals:
        n_rows, cols = _packed_rows(v.shape, width)
        rows.append(jnp.pad(v.reshape(n_rows, cols).astype(F32), ((0, 0), (0, width - cols))))
    buf = jnp.concatenate(rows, axis=0)
    return jnp.pad(buf, ((0, (-buf.shape[0]) % 8), (0, 0)))


def _unpack(buf, shapes, width=PACK):
    out, r = [], 0
    for shape in shapes:
        n_rows, cols = _packed_rows(shape, width)
        out.append(buf[r:r + n_rows, :cols].reshape(shape))
        r += n_rows
    return out


FFN_SPLIT = 2


def _ffn_hidden(name, n, wg, wu, tm=1024, carry=None):
    m, k = n.shape
    f = wg.shape[0]
    fb = f // FFN_SPLIT
    tm = _row_tile(m, tm)

    def core(n_ref, wg_ref, wu_ref, a_ref, b_ref, s_ref):
        nv = n_ref[...]
        a = _dot(nv, wg_ref[...], True)
        b = _dot(nv, wu_ref[...], True)
        a_ref[...] = a.astype(a_ref.dtype)
        b_ref[...] = b.astype(b_ref.dtype)
        s_ref[...] = (a * jax.nn.sigmoid(a) * b).astype(s_ref.dtype)

    w_spec = pl.BlockSpec((fb, k), lambda j, i: (j, 0))
    out_spec = pl.BlockSpec((tm, fb), lambda j, i: (i, j))
    return _call(name, core, (FFN_SPLIT, m // tm), [pl.BlockSpec((tm, k), lambda j, i: (i, 0)), w_spec, w_spec],
                 [out_spec] * 3, [jax.ShapeDtypeStruct((m, f), BF16)] * 3, [], [n, wg, wu], carry)


def _ffn_d_hidden(name, df, wd, a, b, tm=512):
    m, k = df.shape
    f = wd.shape[0]
    fb = f // FFN_SPLIT
    tm = _row_tile(m, tm)

    def body(df_ref, wd_ref, a_ref, b_ref, da_ref, db_ref):
        dfv = df_ref[...]
        for j in range(FFN_SPLIT):
            cols = slice(j * fb, (j + 1) * fb)
            ds = _dot(dfv, wd_ref[cols, :], True)
            av, bv = a_ref[:, cols].astype(F32), b_ref[:, cols].astype(F32)
            sig = jax.nn.sigmoid(av)
            da_ref[:, cols] = (ds * bv * sig * (1.0 + av * (1.0 - sig))).astype(da_ref.dtype)
            db_ref[:, cols] = (ds * av * sig).astype(db_ref.dtype)

    row = pl.BlockSpec((tm, f), lambda i: (i, 0))
    return pl.pallas_call(
        body, name=name, grid=(m // tm,),
        in_specs=[pl.BlockSpec((tm, k), lambda i: (i, 0)), _resident(wd), row, row],
        out_specs=[row, row], out_shape=[jax.ShapeDtypeStruct((m, f), BF16)] * 2,
        compiler_params=_cparams(1),
    )(df, wd, a, b)


def kernel(x, ffn1_norm_pre, ffn1_w_gate, ffn1_w_up, ffn1_w_down, ffn1_norm_post, mix_norm_pre, w_in, gate_bias, rel_table, w_attn_out, conv_glu_bias, conv_dw_w, conv_dw_b, conv_ln_g, conv_ln_b, conv_w_out, w_out, mix_norm_post, ffn2_norm_pre, ffn2_w_gate, ffn2_w_up, ffn2_w_down, ffn2_norm_post, loss_target, m_ffn1_norm_pre, m_ffn1_w_gate, m_ffn1_w_up, m_ffn1_w_down, m_ffn1_norm_post, m_mix_norm_pre, m_w_in, m_gate_bias, m_rel_table, m_w_attn_out, m_conv_glu_bias, m_conv_dw_w, m_conv_dw_b, m_conv_ln_g, m_conv_ln_b, m_conv_w_out, m_w_out, m_mix_norm_post, m_ffn2_norm_pre, m_ffn2_w_gate, m_ffn2_w_up, m_ffn2_w_down, m_ffn2_norm_post, v_ffn1_norm_pre, v_ffn1_w_gate, v_ffn1_w_up, v_ffn1_w_down, v_ffn1_norm_post, v_mix_norm_pre, v_w_in, v_gate_bias, v_rel_table, v_w_attn_out, v_conv_glu_bias, v_conv_dw_w, v_conv_dw_b, v_conv_ln_g, v_conv_ln_b, v_conv_w_out, v_w_out, v_mix_norm_post, v_ffn2_norm_pre, v_ffn2_w_gate, v_ffn2_w_up, v_ffn2_w_down, v_ffn2_norm_post):
    args = dict(locals())
    names = ['ffn1_norm_pre', 'ffn1_w_gate', 'ffn1_w_up', 'ffn1_w_down', 'ffn1_norm_post', 'mix_norm_pre', 'w_in',
             'gate_bias', 'rel_table', 'w_attn_out', 'conv_glu_bias', 'conv_dw_w', 'conv_dw_b', 'conv_ln_g',
             'conv_ln_b', 'conv_w_out', 'w_out', 'mix_norm_post', 'ffn2_norm_pre', 'ffn2_w_gate', 'ffn2_w_up',
             'ffn2_w_down', 'ffn2_norm_post']
    big = ['ffn1_w_gate', 'ffn1_w_up', 'ffn1_w_down', 'w_in', 'w_attn_out', 'conv_w_out', 'w_out', 'ffn2_w_gate',
           'ffn2_w_up', 'ffn2_w_down']
    small = [n for n in names if n not in big]

    xs, target = x[0], loss_target[0]
    t, d = xs.shape
    cx, cy = lax.axis_index("x"), lax.axis_index("y")
    chip = 2 * cx + cy

    dw_shard = conv_dw_w[0, :, 0, :]
    cshard = dw_shard.shape[1]
    dw_all, _ = _all_devices("gather_dw", _pack([dw_shard], width=cshard))
    dw_full = jnp.concatenate([dw_all[2 * j, :CONV_WIDTH, :cshard] for j in range(N_CHIPS)], axis=1)
    dw_full = jnp.pad(dw_full, ((0, CONV_HALO - CONV_WIDTH), (0, 0)))
    peers = [(1 - cx, cy), (cx, 1 - cy), (1 - cx, 1 - cy)]
    pos = jnp.stack([lax.axis_index("c"), chip] + [_chip_index(p) for p in peers]).astype(jnp.int32)
    transposed = ("ffn1_w_gate", "ffn1_w_up", "ffn2_w_gate", "ffn2_w_up")
    weight_of = lambda n: n[2:] if n[:2] in ("m_", "v_") else n
    shard = lambda n: jnp.transpose(args[n][0]) if weight_of(n) in transposed else args[n][0]
    unshard = lambda n, v: (jnp.transpose(v) if n in transposed else v)[None]
    own = {n: _cast_into("cast_" + n, pos, shard(n)) for n in big}
    gather = lambda *ns: _gather_carry([own[n] for n in ns])
    res_spec = [(d, F32), (d, F32), (d, BF16)]
    whole = lambda w: w.reshape(-1, w.shape[-1])

    table_pad = jnp.pad(rel_table[0], ((0, 0), (0, REL_PAD - rel_table.shape[2])))
    bias, (wg1,) = _bias_expand(table_pad, gather("ffn1_w_gate"))
    n1, (wu1,) = _rms_fwd("ffn1_pre", xs, ffn1_norm_pre, gather("ffn1_w_up"))
    (a1, b1, s1), (wd1, win, wao, wco, wout) = _ffn_hidden(
        "ffn1_hidden", n1, whole(wg1), whole(wu1),
        carry=gather("ffn1_w_down", "w_in", "w_attn_out", "conv_w_out", "w_out"))
    (f1, h1, u), (wg2,) = _mm_kblk(
        "ffn1_down", [(s1, whole(wd1)[None])], trans_w=False, epilogue=_ep_post_res_pre(0.5), rows=[xs],
        vecs=[ffn1_norm_post, mix_norm_pre], row_outs=res_spec, carry=gather("ffn2_w_gate"))
    proj, (wu2, wd2) = _mm_nblk("mix_in", u, win, trans_w=False, out_blocked=False, out_dtype=BF16,
                                carry=gather("ffn2_w_up", "ffn2_w_down"))
    att, lse = _attn_fwd(proj, bias)
    cs, c_glu, z_conv = _conv_fwd(proj, conv_glu_bias, dw_full, conv_dw_b, conv_ln_g, conv_ln_b)
    y_a, y_b, merged = _mix_merge(att, cs, wao, wco, proj, gate_bias)
    (mo, h2, n2), _ = _mm_kblk(
        "mix_out", [(merged, wout)], trans_w=False, epilogue=_ep_post_res_pre(1.0), rows=[h1],
        vecs=[mix_norm_post, ffn2_norm_pre], row_outs=res_spec)
    (a2, b2, s2), _ = _ffn_hidden("ffn2_hidden", n2, whole(wg2), whole(wu2))
    g = {}
    (dy, df2, err2, g["ffn2_norm_post"]), _ = _mm_kblk(
        "ffn2_down", [(s2, whole(wd2)[None])], trans_w=False, epilogue=_ep_loss(0.5, d), rows=[h2, target],
        vecs=[ffn2_norm_post], row_outs=[(d, F32), (d, BF16)], vec_outs=[d, d])
    loss = lax.psum(0.5 * jnp.sum(err2) / d, ("x", "y", "c"))

    parts, landed = {}, {}

    def ffn_bwd(tag, df, n, a, b, s, wg, wu, wd, **epilogue):
        da, db = _ffn_d_hidden(tag + "_d_hidden", df, whole(wd), a, b)
        group = [tag + "_w_down", tag + "_w_gate", tag + "_w_up"]
        local = [_mm_tn_wide(tag + "_g_" + what, hidden, other, d, a_split=FFN_SPLIT).reshape(wd.shape)
                 for what, hidden, other in (("down", s, df), ("gate", da, n), ("up", db, n))]
        return _mm_kblk(tag + "_d_n", [(da, whole(wg)[None]), (db, whole(wu)[None])], trans_w=False,
                        carry=pair_sums(tag, group, local), **epilogue), group

    def pair_sums(tag, group, local):
        theirs = _pair_exchange("pair_" + tag, local)
        for n, mine, other in zip(group, local, theirs):
            parts[n] = _add_pair("pair_sum_" + n, pos, mine, other)
        return _to_owner_carry([parts[n] for n in group])

    def keep(group, carried):
        for n, val in zip(group, carried):
            landed[n] = val

    ((dh2, dmo, g["ffn2_norm_pre"], g["mix_norm_post"]), carried), group = ffn_bwd(
        "ffn2", df2, n2, a2, b2, s2, wg2, wu2, wd2, epilogue=_ep_pre_bwd_post(1.0), rows=[h2, dy, mo],
        vecs=[ffn2_norm_pre, mix_norm_post], row_outs=[(d, F32), (d, BF16)], vec_outs=[d, d])
    keep(group, carried)
    g_wout = _mm_tn("mix_g_out", merged, "col", dmo, "full")
    dy_a, dy_b, dgates, datt, dcs, g["gate_bias"] = _mix_d_merge(dmo, wout, y_a, y_b, wao, wco, proj, gate_bias)
    g_wao = _mm_tn("attn_g_out", att, "full", dy_a, "col")
    g_wco = _mm_tn("conv_g_out", cs, "full", dy_b, "col")
    dq, dk, dv, dbias = _attn_bwd(proj, bias, att, lse, datt)
    g["rel_table"] = _bias_fold(dbias)[:, :rel_table.shape[2]]
    dcin, g_dw, g["conv_dw_b"], g["conv_ln_g"], g["conv_ln_b"], g["conv_glu_bias"] = _conv_bwd(
        proj, c_glu, z_conv, dcs, conv_glu_bias, dw_full, conv_ln_g, conv_ln_b)
    pieces = [("q", dq), ("k", dk), ("v", dv), ("conv", dcin), ("gates", dgates)]
    n_in = win.shape[0] * win.shape[2]
    win_cols = jnp.transpose(jnp.transpose(win, (1, 0, 2)).reshape(d, n_in // COL, COL), (1, 0, 2))
    g_cols = jnp.concatenate([_mm_tn_wide("mix_g_in_" + tag, u, piece, COL) for tag, piece in pieces], axis=0)
    g_win = jnp.transpose(jnp.transpose(g_cols, (1, 0, 2)).reshape(d, win.shape[0], win.shape[2]), (1, 0, 2))
    bounds = np.cumsum([0] + [piece.shape[1] // COL for _, piece in pieces])
    group = ["w_out", "w_attn_out", "conv_w_out", "w_in"]
    (dh1, df1, g["mix_norm_pre"], g["ffn1_norm_post"]), carried = _mm_kblk(
        "mix_d_in", [(piece, win_cols[lo:hi]) for (_, piece), lo, hi in zip(pieces, bounds[:-1], bounds[1:])],
        trans_w=True, epilogue=_ep_pre_bwd_post(0.5), rows=[h1, dh2, f1],
        vecs=[mix_norm_pre, ffn1_norm_post], row_outs=[(d, F32), (d, BF16)], vec_outs=[d, d],
        carry=pair_sums("mix", group, [g_wout, g_wao, g_wco, g_win]))
    keep(group, carried)
    ((grad_x, g["ffn1_norm_pre"]), carried), group = ffn_bwd(
        "ffn1", df1, n1, a1, b1, s1, wg1, wu1, wd1, epilogue=_ep_pre_bwd_first(), rows=[xs, dh1],
        vecs=[ffn1_norm_pre], row_outs=[(d, F32)], vec_outs=[d])
    keep(group, carried)

    halves = [_add_chips("chip_sum_" + n, pos, parts[n], landed[n]) for n in big]
    other_halves = _swap_halves(halves)

    g["conv_dw_w"] = g_dw[:CONV_WIDTH]
    _, small_sum = _all_devices("sum_small", _pack([g[n] for n in small]))
    for n, val in zip(small, _unpack(small_sum, [g[n].shape for n in small])):
        g[n] = val
    g["conv_dw_w"] = lax.dynamic_slice_in_dim(g["conv_dw_w"], chip * cshard, cshard, axis=1)

    grads, deltas, new_m, new_v = {}, {}, {}, {}
    for n, mine, other in zip(big, halves, other_halves):
        gr, dl, m2, v2 = _adamw_halves("adamw_" + n, pos, shard(n), shard("m_" + n), shard("v_" + n), mine, other)
        grads[n], deltas[n], new_m[n], new_v[n] = unshard(n, gr), unshard(n, dl), unshard(n, m2), unshard(n, v2)
    shapes = [g[n].shape for n in small]
    packed = lambda pre: _pack([args[pre + n].reshape(shp) for n, shp in zip(small, shapes)])
    dl, m2, v2 = _adamw("adamw_small", packed(""), _pack([g[n] for n in small]), packed("m_"), packed("v_"))
    for n, a_, b_, c_ in zip(small, _unpack(dl, shapes), _unpack(m2, shapes), _unpack(v2, shapes)):
        shape = args[n].shape
        grads[n], deltas[n], new_m[n], new_v[n] = (g[n].reshape(shape), a_.reshape(shape), b_.reshape(shape),
                                                   c_.reshape(shape))

    return (loss, grad_x[None], *[grads[n] for n in names], *[deltas[n] for n in names],
            *[new_m[n] for n in names], *[new_v[n] for n in names])
```

```python
import functools

import numpy as np
import jax
import jax.numpy as jnp
from jax import lax
from jax.experimental import pallas as pl
from jax.experimental.pallas import tpu as pltpu

F32 = jnp.float32
BF16 = jnp.bfloat16
MESH = pl.DeviceIdType.MESH
ANY = pl.BlockSpec(memory_space=pl.ANY)

EPS = 1e-6
CHUNK = 64
LEFT_CHUNKS = 8
N_HEADS = 8
HEAD_DIM = 64
D_ATTN = N_HEADS * HEAD_DIM
D_CONV = 512
CONV_WIDTH = 31
REL_CLIP = 128
N_CHIPS = 4
N_DEV = 8
Q_BLOCK = 4 * CHUNK
K_PAD = LEFT_CHUNKS * CHUNK
K_WIN = K_PAD + Q_BLOCK
REL_EXT = 1024
REL_PAD = 384
CONV_HALO = 32
CONV_TILE = 512
COL = 512
NEG = -1e30

ADAM_LR = 0.001
ADAM_B1 = 0.9
ADAM_B2 = 0.999
ADAM_EPS = 1e-08
ADAM_WD = 0.01
ADAM_STEP = 10

VMEM_LIMIT_BYTES = 56 * 1024 * 1024


def _cparams(n_grid):
    return pltpu.CompilerParams(dimension_semantics=("arbitrary",) * n_grid, vmem_limit_bytes=VMEM_LIMIT_BYTES)


def _row_tile(rows, want):
    if rows <= want:
        return rows
    for t in range(want - want % 16, 0, -16):
        if rows % t == 0:
            return t
    raise ValueError((rows, want))


def _dot(a, w, trans_w):
    dims = (((1,), (1,)), ((), ())) if trans_w else (((1,), (0,)), ((), ()))
    return lax.dot_general(a, w, dims, preferred_element_type=F32)


class _Carry:
    LATE_STEPS = 2

    def __init__(self, ins, outs, aliases, sems, phases):
        self.ins, self.outs, self.aliases, self.sems, self.phases = ins, outs, aliases, sems, phases


def _call(name, core, grid, in_specs, out_specs, out_shape, scratch, args, carry=None):
    n_in, n_out, n_scr = len(in_specs), len(out_specs), len(scratch)
    if carry is None:
        out = pl.pallas_call(core, name=name, grid=grid, in_specs=in_specs, out_specs=out_specs, out_shape=out_shape,
                             scratch_shapes=scratch, compiler_params=_cparams(len(grid)))(*args)
        return list(out), []
    c_in, c_out = len(carry.ins), len(carry.outs)
    total = int(np.prod(grid))
    late = max(total - 1 - _Carry.LATE_STEPS, 0)

    def body(*refs):
        ins, refs = refs[:n_in], refs[n_in:]
        c_ins, refs = refs[:c_in], refs[c_in:]
        outs, refs = refs[:n_out], refs[n_out:]
        c_outs, refs = refs[:c_out], refs[c_out:]
        scr, c_sems = refs[:n_scr], refs[n_scr:]
        step = pl.program_id(0)
        for axis in range(1, len(grid)):
            step = step * grid[axis] + pl.program_id(axis)

        def run(when, at):
            for w, fn in carry.phases:
                if w == when:
                    pl.when(step == at)(functools.partial(fn, c_ins, c_outs, c_sems))

        run("first", 0)
        core(*ins, *outs, *scr)
        run("late", late)
        run("last", total - 1)

    out = pl.pallas_call(
        body, name=name, grid=grid, in_specs=list(in_specs) + [ANY] * c_in, out_specs=list(out_specs) + [ANY] * c_out,
        out_shape=list(out_shape) + list(carry.outs), scratch_shapes=list(scratch) + list(carry.sems),
        input_output_aliases={n_in + a: n_out + b for a, b in carry.aliases.items()},
        compiler_params=_cparams(len(grid)),
    )(*args, *carry.ins)
    return list(out[:n_out]), list(out[n_out:])


def _mm_nblk(name, a, w, *, trans_w, out_blocked, out_dtype, tm=1024, carry=None):
    m, k = a.shape
    nj = w.shape[0]
    nb = w.shape[1] if trans_w else w.shape[2]
    tm = _row_tile(m, tm)

    def core(a_ref, w_ref, o_ref):
        o_ref[...] = _dot(a_ref[...], w_ref[...], trans_w).astype(o_ref.dtype)

    if out_blocked:
        out_shape, out_spec = (nj, m, nb), pl.BlockSpec((None, tm, nb), lambda j, i: (j, i, 0))
    else:
        out_shape, out_spec = (m, nj * nb), pl.BlockSpec((tm, nb), lambda j, i: (i, j))
    out, carried = _call(
        name, core, (nj, m // tm),
        [pl.BlockSpec((tm, k), lambda j, i: (i, 0)), pl.BlockSpec((None,) + w.shape[1:], lambda j, i: (j, 0, 0))],
        [out_spec], [jax.ShapeDtypeStruct(out_shape, out_dtype)], [], [a, w], carry)
    return out[0] if carry is None else (out[0], carried)


def _mm_kblk(name, pairs, *, trans_w, out_dtype=F32, tm=512, sub=256, epilogue=None, rows=(), vecs=(), row_outs=None,
             vec_outs=(), carry=None):
    w0 = pairs[0][1]
    n = w0.shape[1] if trans_w else w0.shape[2]
    blocks = [(w.shape[0], w.shape[2] if trans_w else w.shape[1]) for _, w in pairs]
    m = pairs[0][0].shape[-2]
    tm = _row_tile(m, tm)
    ts = _row_tile(tm, sub)
    n_pairs, n_rows, n_vecs = len(pairs), len(rows), len(vecs)
    if epilogue is None:
        epilogue, row_outs = (lambda acc, r, v: ([acc], [])), [(n, out_dtype)]
    n_ro, n_vo = len(row_outs), len(vec_outs)

    def core(*refs):
        pair_refs, refs = refs[:2 * n_pairs], refs[2 * n_pairs:]
        row_refs, refs = refs[:n_rows], refs[n_rows:]
        vec_refs, refs = refs[:n_vecs], refs[n_vecs:]
        ro_refs, vo_refs = refs[:n_ro], refs[n_ro:]
        if n_vo:
            @pl.when(pl.program_id(0) == 0)
            def _():
                for ref in vo_refs:
                    ref[...] = jnp.zeros_like(ref)

        vec_vals = [v[...] for v in vec_refs]
        sums = None
        for r0 in range(0, tm, ts):
            sub_rows = slice(r0, r0 + ts)
            acc = None
            for p in range(n_pairs):
                a_ref, w_ref = pair_refs[2 * p], pair_refs[2 * p + 1]
                nj, kb = blocks[p]
                for j in range(nj):
                    a_blk = a_ref[j, sub_rows, :] if len(a_ref.shape) == 3 else a_ref[sub_rows, j * kb:(j + 1) * kb]
                    part = _dot(a_blk, w_ref[j], trans_w)
                    acc = part if acc is None else acc + part
            ro, vo = epilogue(acc, [r[sub_rows, :] for r in row_refs], vec_vals)
            for ref, val in zip(ro_refs, ro):
                ref[sub_rows, :] = val.astype(ref.dtype)
            sums = vo if sums is None else [s + v for s, v in zip(sums, vo)]
        for ref, val in zip(vo_refs, sums or []):
            ref[...] += val

    in_specs, args = [], []
    for (a, w), (nj, kb) in zip(pairs, blocks):
        if a.ndim == 3:
            in_specs.append(pl.BlockSpec((nj, tm, kb), lambda i: (0, i, 0)))
        else:
            in_specs.append(pl.BlockSpec((tm, nj * kb), lambda i: (i, 0)))
        in_specs.append(pl.BlockSpec(w.shape, lambda i: (0, 0, 0), pipeline_mode=pl.Buffered(1)))
        args += [a, w]
    in_specs += [pl.BlockSpec((tm, r.shape[1]), lambda i: (i, 0)) for r in rows]
    in_specs += [pl.BlockSpec(v.shape, lambda i: (0, 0)) for v in vecs]
    out_specs = [pl.BlockSpec((tm, cols), lambda i: (i, 0)) for cols, _ in row_outs]
    out_specs += [pl.BlockSpec((1, cols), lambda i: (0, 0)) for cols in vec_outs]
    out_shape = [jax.ShapeDtypeStruct((m, cols), dt) for cols, dt in row_outs]
    out_shape += [jax.ShapeDtypeStruct((1, cols), F32) for cols in vec_outs]
    return _call(name, core, (m // tm,), in_specs, out_specs, out_shape, [], args + list(rows) + list(vecs), carry)


def _mm_tn(name, a, a_mode, b, b_mode, *, out_dtype=BF16, tt=2048):
    nj = N_CHIPS
    t = a.shape[-2]
    tt = _row_tile(t, tt)

    def spec(x, mode):
        if mode == "full":
            return x.shape[1], pl.BlockSpec((tt, x.shape[1]), lambda j, s: (s, 0))
        if mode == "col":
            cb = x.shape[1] // nj
            return cb, pl.BlockSpec((tt, cb), lambda j, s: (s, j))
        return x.shape[2], pl.BlockSpec((None, tt, x.shape[2]), lambda j, s: (j, s, 0))

    ca, a_spec = spec(a, a_mode)
    cb, b_spec = spec(b, b_mode)
    n_steps = t // tt

    def body(a_ref, b_ref, o_ref, acc_ref):
        s = pl.program_id(1)

        @pl.when(s == 0)
        def _():
            acc_ref[...] = jnp.zeros_like(acc_ref)

        acc_ref[...] += lax.dot_general(a_ref[...], b_ref[...], (((0,), (0,)), ((), ())),
                                        preferred_element_type=F32)

        @pl.when(s == n_steps - 1)
        def _():
            o_ref[...] = acc_ref[...].astype(o_ref.dtype)

    return pl.pallas_call(
        body, name=name, grid=(nj, n_steps), in_specs=[a_spec, b_spec],
        out_specs=pl.BlockSpec((None, ca, cb), lambda j, s: (j, 0, 0)),
        out_shape=jax.ShapeDtypeStruct((nj, ca, cb), out_dtype),
        scratch_shapes=[pltpu.VMEM((ca, cb), F32)], compiler_params=_cparams(2),
    )(a, b)


def _mm_tn_wide(name, a, b, cb, *, a_split=1, out_dtype=BF16, tt=1024, carry=None):
    t, ca = a.shape
    nb = b.shape[1] // cb
    tt = _row_tile(t, tt)
    n_steps = t // tt
    piece = ca // a_split

    def body(a_ref, b_ref, o_ref, acc_ref):
        s = pl.program_id(0)

        @pl.when(s == 0)
        def _():
            acc_ref[...] = jnp.zeros_like(acc_ref)

        for j in range(nb):
            bv = b_ref[:, j * cb:(j + 1) * cb]
            for c in range(a_split):
                rows = slice(c * piece, (c + 1) * piece)
                acc_ref[j, rows, :] += lax.dot_general(a_ref[:, rows], bv, (((0,), (0,)), ((), ())),
                                                       preferred_element_type=F32)

        @pl.when(s == n_steps - 1)
        def _():
            o_ref[...] = acc_ref[...].astype(o_ref.dtype)

    out, carried = _call(
        name, body, (n_steps,),
        [pl.BlockSpec((tt, ca), lambda s: (s, 0)), pl.BlockSpec((tt, nb * cb), lambda s: (s, 0))],
        [pl.BlockSpec((nb, ca, cb), lambda s: (0, 0, 0))], [jax.ShapeDtypeStruct((nb, ca, cb), out_dtype)],
        [pltpu.VMEM((nb, ca, cb), F32)], [a, b], carry)
    return out[0] if carry is None else (out[0], carried)


def _rowwise(name, fn, rows, vecs, row_outs, vec_outs, *, tm=256, carry=None):
    nrows = rows[0][0].shape[0]
    tm = _row_tile(nrows, tm)
    n_r, n_v, n_ro, n_vo = len(rows), len(vecs), len(row_outs), len(vec_outs)

    def body(*refs):
        r_vals = [r[...] for r in refs[:n_r]]
        v_vals = [r[...] for r in refs[n_r:n_r + n_v]]
        ro_refs = refs[n_r + n_v:n_r + n_v + n_ro]
        vo_refs = refs[n_r + n_v + n_ro:]
        ro, vo = fn(r_vals, v_vals)
        for ref, val in zip(ro_refs, ro):
            ref[...] = val.astype(ref.dtype)
        if n_vo:
            @pl.when(pl.program_id(0) == 0)
            def _():
                for ref in vo_refs:
                    ref[...] = jnp.zeros_like(ref)

            for ref, val in zip(vo_refs, vo):
                ref[...] += val

    in_specs = [pl.BlockSpec((tm, cols), functools.partial(lambda i, cb: (i, cb), cb=cb)) for _, cols, cb in rows]
    in_specs += [pl.BlockSpec(v.shape, functools.partial(lambda i, nd: (0,) * nd, nd=v.ndim)) for v in vecs]
    out_specs = [pl.BlockSpec((tm, cols), lambda i: (i, 0)) for cols, _ in row_outs]
    out_specs += [pl.BlockSpec((1, cols), lambda i: (0, 0)) for cols in vec_outs]
    out_shape = [jax.ShapeDtypeStruct((nrows, cols), dt) for cols, dt in row_outs]
    out_shape += [jax.ShapeDtypeStruct((1, cols), F32) for cols in vec_outs]
    out, carried = _call(name, body, (nrows // tm,), in_specs, out_specs, out_shape, [],
                         [r[0] for r in rows] + list(vecs), carry)
    return out if carry is None else (out, carried)


def _whole(x):
    return (x, x.shape[1], 0)


def _colsum(x):
    return jnp.sum(x, axis=0, keepdims=True)


def _rstd(x):
    return lax.rsqrt(jnp.mean(x * x, axis=-1, keepdims=True) + EPS)


def _rms_bwd(dn, x, g):
    r = _rstd(x)
    c = dn * g
    dx = r * c - x * (r * r * r) * jnp.mean(c * x, axis=-1, keepdims=True)
    return dx, _colsum(dn * x * r)


def _rms_fwd(name, x, g, carry):
    def fn(r, v):
        (xv,), (gv,) = r, v
        return [xv * _rstd(xv) * gv], []

    (n,), carried = _rowwise(name, fn, [_whole(x)], [g], [(x.shape[1], BF16)], [], carry=carry)
    return n, carried


def _ep_post_res_pre(scale):
    def epilogue(acc, rows, vecs):
        (resid,), (g_post, g_next) = rows, vecs
        h = resid + scale * (acc * _rstd(acc) * g_post)
        return [acc, h, h * _rstd(h) * g_next], []

    return epilogue


def _post_bwd(dh, f, g_post, scale):
    return _rms_bwd(scale * dh, f, g_post)


def _ep_loss(scale, d):
    def epilogue(acc, rows, vecs):
        (resid, target), (g_post,) = rows, vecs
        err = resid + scale * (acc * _rstd(acc) * g_post) - target
        dy = err * (1.0 / d)
        df, dg_post = _post_bwd(dy, acc, g_post, scale)
        return [dy, df], [_colsum(err * err), dg_post]

    return epilogue


def _ep_pre_bwd_post(scale_prev):
    def epilogue(acc, rows, vecs):
        (h, dh_up, f_prev), (g_pre, g_post_prev) = rows, vecs
        dx, dg_pre = _rms_bwd(acc, h, g_pre)
        dh = dh_up + dx
        df, dg_post = _post_bwd(dh, f_prev, g_post_prev, scale_prev)
        return [dh, df], [dg_pre, dg_post]

    return epilogue


def _ep_pre_bwd_first():
    def epilogue(acc, rows, vecs):
        (x, dh_up), (g_pre,) = rows, vecs
        dx, dg_pre = _rms_bwd(acc, x, g_pre)
        return [dh_up + dx], [dg_pre]

    return epilogue


def _gate_specs(d, tm):
    first = (3 * D_ATTN + 2 * D_CONV) // COL
    return [pl.BlockSpec((tm, COL), functools.partial(lambda i, cb: (i, cb), cb=first + p)) for p in range(2 * d // COL)]


def _gate(piece_refs, bias_ref, c0, width):
    p, off = divmod(c0, COL)
    return jax.nn.sigmoid(piece_refs[p][:, off:off + width].astype(F32) + bias_ref[:, c0:c0 + width])


def _resident(w):
    return pl.BlockSpec(w.shape, functools.partial(lambda i, nd: (0,) * nd, nd=w.ndim), pipeline_mode=pl.Buffered(1))


def _mix_merge(att, cs, wao, wco, proj, gate_bias, tm=512):
    t = att.shape[0]
    nj, _, nb = wao.shape
    d = nj * nb
    tm = _row_tile(t, tm)
    gate_specs = _gate_specs(d, tm)
    n_p = len(gate_specs)

    def body(att_ref, cs_ref, wao_ref, wco_ref, *rest):
        pieces, (gb_ref, ya_ref, yb_ref, m_ref) = rest[:n_p], rest[n_p:]
        av, cv = att_ref[...], cs_ref[...]
        for j in range(nj):
            cols = slice(j * nb, (j + 1) * nb)
            ya = _dot(av, wao_ref[j], False)
            yb = _dot(cv, wco_ref[j], False)
            merged = _gate(pieces, gb_ref, j * nb, nb) * ya + _gate(pieces, gb_ref, d + j * nb, nb) * yb
            ya_ref[:, cols] = ya.astype(ya_ref.dtype)
            yb_ref[:, cols] = yb.astype(yb_ref.dtype)
            m_ref[:, cols] = merged.astype(m_ref.dtype)

    row = lambda x: pl.BlockSpec((tm, x.shape[1]), lambda i: (i, 0))
    out_spec = pl.BlockSpec((tm, d), lambda i: (i, 0))
    return pl.pallas_call(
        body, name="mix_merge", grid=(t // tm,),
        in_specs=[row(att), row(cs), _resident(wao), _resident(wco)] + gate_specs + [_resident(gate_bias)],
        out_specs=[out_spec] * 3, out_shape=[jax.ShapeDtypeStruct((t, d), BF16)] * 3, compiler_params=_cparams(1),
    )(att, cs, wao, wco, *([proj] * n_p), gate_bias)


def _mix_d_merge(dmo, wout, y_a, y_b, wao, wco, proj, gate_bias, tm=512):
    t, d = dmo.shape
    nj, _, nb = wao.shape
    ka, kc = wao.shape[1], wco.shape[1]
    tm = _row_tile(t, tm)
    gate_specs = _gate_specs(d, tm)
    n_p = len(gate_specs)

    def body(dmo_ref, wout_ref, ya_ref, yb_ref, wao_ref, wco_ref, *rest):
        pieces, (gb_ref, dya_ref, dyb_ref, dg_ref, datt_ref, dcs_ref, dgb_ref) = rest[:n_p], rest[n_p:]

        @pl.when(pl.program_id(0) == 0)
        def _():
            dgb_ref[...] = jnp.zeros_like(dgb_ref)

        dmo_v = dmo_ref[...]
        datt = dcs = None
        for j in range(nj):
            cols, cols_b = slice(j * nb, (j + 1) * nb), slice(d + j * nb, d + (j + 1) * nb)
            dm = _dot(dmo_v, wout_ref[j], True)
            ga, gb = _gate(pieces, gb_ref, j * nb, nb), _gate(pieces, gb_ref, d + j * nb, nb)
            dya, dyb = (dm * ga).astype(BF16), (dm * gb).astype(BF16)
            dga = dm * ya_ref[:, cols].astype(F32) * ga * (1.0 - ga)
            dgb = dm * yb_ref[:, cols].astype(F32) * gb * (1.0 - gb)
            dya_ref[:, cols], dyb_ref[:, cols] = dya, dyb
            dg_ref[:, cols], dg_ref[:, cols_b] = dga.astype(dg_ref.dtype), dgb.astype(dg_ref.dtype)
            dgb_ref[:, cols] += _colsum(dga)
            dgb_ref[:, cols_b] += _colsum(dgb)
            pa, pc = _dot(dya, wao_ref[j], True), _dot(dyb, wco_ref[j], True)
            datt, dcs = (pa, pc) if datt is None else (datt + pa, dcs + pc)
        datt_ref[...] = datt.astype(datt_ref.dtype)
        dcs_ref[...] = dcs.astype(dcs_ref.dtype)

    row = lambda cols: pl.BlockSpec((tm, cols), lambda i: (i, 0))
    return pl.pallas_call(
        body, name="mix_d_merge", grid=(t // tm,),
        in_specs=[row(d), _resident(wout), row(d), row(d), _resident(wao), _resident(wco)] + gate_specs
        + [_resident(gate_bias)],
        out_specs=[row(d), row(d), row(2 * d), row(ka), row(kc), pl.BlockSpec((1, 2 * d), lambda i: (0, 0))],
        out_shape=[jax.ShapeDtypeStruct((t, d), BF16), jax.ShapeDtypeStruct((t, d), BF16),
                   jax.ShapeDtypeStruct((t, 2 * d), BF16), jax.ShapeDtypeStruct((t, ka), BF16),
                   jax.ShapeDtypeStruct((t, kc), F32), jax.ShapeDtypeStruct((1, 2 * d), F32)],
        compiler_params=_cparams(1),
    )(dmo, wout, y_a, y_b, wao, wco, *([proj] * n_p), gate_bias)


def _adamw_math(wv, gv, mv, vv):
    m2 = ADAM_B1 * mv + (1.0 - ADAM_B1) * gv
    v2 = ADAM_B2 * vv + (1.0 - ADAM_B2) * (gv * gv)
    m_hat = m2 / (1.0 - ADAM_B1 ** ADAM_STEP)
    v_hat = v2 / (1.0 - ADAM_B2 ** ADAM_STEP)
    delta = -ADAM_LR * (m_hat / (jnp.sqrt(v_hat) + ADAM_EPS) + ADAM_WD * wv)
    return delta, m2, v2


def _adamw(name, w, g, m, v):
    def fn(r, _):
        return list(_adamw_math(*r)), []

    c = w.shape[1]
    return _rowwise(name, fn, [_whole(w), _whole(g), _whole(m), _whole(v)], [], [(c, F32)] * 3, [], tm=256)


POS_C, POS_CHIP, POS_PEER = 0, 1, 2


def _placed_call(body, name, pos, grid, in_specs, out_specs, out_shape, args):
    return pl.pallas_call(
        body, name=name, out_shape=out_shape, compiler_params=_cparams(len(grid)),
        grid_spec=pltpu.PrefetchScalarGridSpec(num_scalar_prefetch=1, grid=grid, in_specs=in_specs,
                                               out_specs=out_specs),
    )(pos, *args)


def _cast_into(name, pos, w):
    r, cols = w.shape
    tm = _row_tile(r, 1024)

    def body(pos_ref, w_ref, o_ref):
        o_ref[...] = w_ref[...].astype(o_ref.dtype)

    return _placed_call(
        body, name, pos, (r // tm,), [pl.BlockSpec((tm, cols), lambda i, pos: (i, 0))],
        pl.BlockSpec((None, tm, cols), lambda i, pos: (pos[POS_CHIP], i, 0)),
        jax.ShapeDtypeStruct((N_CHIPS, r, cols), BF16), [w])


def _add_pair(name, pos, grad, landed):
    nj, half, cols = landed.shape
    tm = _row_tile(half, 512)
    nb = half // tm

    def body(pos_ref, g_ref, l_ref, o_ref):
        o_ref[...] = (g_ref[...].astype(F32) + l_ref[...].astype(F32)).astype(o_ref.dtype)

    spec = pl.BlockSpec((None, tm, cols), lambda j, i, pos: (j, i, 0))
    return _placed_call(
        body, name, pos, (nj, nb),
        [pl.BlockSpec((None, tm, cols), lambda j, i, pos: (j, pos[POS_C] * nb + i, 0)), spec], spec,
        jax.ShapeDtypeStruct(landed.shape, BF16), [grad, landed])


def _add_chips(name, pos, part, landed):
    _, half, cols = landed.shape
    tm = _row_tile(half, 512)

    def body(pos_ref, p_ref, l0_ref, l1_ref, l2_ref, o_ref):
        acc = p_ref[...].astype(F32)
        for ref in (l0_ref, l1_ref, l2_ref):
            acc = acc + ref[...].astype(F32)
        o_ref[...] = acc

    slot = lambda at: pl.BlockSpec((None, tm, cols), functools.partial(lambda i, pos, at: (pos[at], i, 0), at=at))
    return _placed_call(
        body, name, pos, (half // tm,), [slot(POS_CHIP)] + [slot(POS_PEER + k) for k in range(3)],
        pl.BlockSpec((tm, cols), lambda i, pos: (i, 0)), jax.ShapeDtypeStruct((half, cols), F32),
        [part, landed, landed, landed])


def _adamw_halves(name, pos, w, m, v, own, landed):
    r, cols = w.shape
    half = own.shape[0]
    tm = _row_tile(half, 256)
    nb = half // tm

    def body(pos_ref, w_ref, m_ref, v_ref, own_ref, land_ref, g_out, d_out, m_out, v_out):
        mine = pl.program_id(0) == pos_ref[POS_C]
        g = jnp.where(mine, own_ref[...], land_ref[...])
        delta, m2, v2 = _adamw_math(w_ref[...], g, m_ref[...], v_ref[...])
        g_out[...] = g
        d_out[...] = delta
        m_out[...] = m2
        v_out[...] = v2

    full = pl.BlockSpec((tm, cols), lambda h, i, pos: (h * nb + i, 0))
    part = pl.BlockSpec((tm, cols), lambda h, i, pos: (i, 0))
    return _placed_call(
        body, name, pos, (2, nb), [full, full, full, part, part], [full] * 4,
        [jax.ShapeDtypeStruct((r, cols), F32)] * 4, [w, m, v, own, landed])


N_START = K_PAD // Q_BLOCK


def _rel_onehot(n_q):
    e = np.arange(REL_EXT)
    dist = K_PAD - (e - (n_q - 1))
    idx = np.clip(dist, -REL_CLIP, REL_CLIP) + REL_CLIP
    return (np.arange(REL_PAD)[:, None] == idx[None, :]).astype(np.float32)


def _skew(x, left):
    row = lax.broadcasted_iota(jnp.int32, x.shape, 0)
    for bit in range(x.shape[0].bit_length() - 1):
        amount = 1 << bit
        rolled = pltpu.roll(x, REL_EXT - amount if left else amount, 1)
        x = jnp.where((row >> bit) & 1 == 1, rolled, x)
    return x


def _bias_expand(table_pad, carry):
    onehot = jnp.asarray(_rel_onehot(Q_BLOCK))

    def core(t_ref, oh_ref, o_ref):
        ext = jnp.dot(t_ref[...], oh_ref[...], precision=lax.Precision.HIGHEST, preferred_element_type=F32)
        qc = lax.broadcasted_iota(jnp.int32, (Q_BLOCK, K_WIN), 0) // CHUNK
        kpos = lax.broadcasted_iota(jnp.int32, (Q_BLOCK, K_WIN), 1)
        band = (kpos // CHUNK >= qc) & (kpos // CHUNK <= qc + LEFT_CHUNKS)
        rows = jnp.broadcast_to(ext, (Q_BLOCK, REL_EXT))
        rolled = _skew(pltpu.roll(rows, REL_EXT - (Q_BLOCK - 1), 1), left=False)[:, :K_WIN]
        for v in range(N_START + 1):
            o_ref[v] = jnp.where(band & (kpos + v * Q_BLOCK >= K_PAD), rolled, NEG)

    (bias,), carried = _call(
        "bias_expand", core, (N_HEADS,),
        [pl.BlockSpec((None, 1, REL_PAD), lambda h: (h, 0, 0)), pl.BlockSpec(onehot.shape, lambda h: (0, 0))],
        [pl.BlockSpec((N_START + 1, None, Q_BLOCK, K_WIN), lambda h: (0, h, 0, 0))],
        [jax.ShapeDtypeStruct((N_START + 1, N_HEADS, Q_BLOCK, K_WIN), F32)], [],
        [table_pad.reshape(N_HEADS, 1, REL_PAD), onehot], carry)
    return bias, carried


def _bias_fold(dbias):
    onehot_t = jnp.asarray(_rel_onehot(CHUNK).T)

    def body(d_ref, oh_ref, o_ref, ext_ref):
        for h in range(N_HEADS):
            x = jnp.concatenate([d_ref[h], jnp.zeros((CHUNK, REL_EXT - K_WIN), F32)], axis=1)
            rolled = _skew(pltpu.roll(x, CHUNK - 1, 1), left=True)
            ext_ref[h:h + 1, :] = jnp.sum(rolled, axis=0, keepdims=True)
        o_ref[...] = jnp.dot(ext_ref[...], oh_ref[...], precision=lax.Precision.HIGHEST,
                             preferred_element_type=F32)

    return pl.pallas_call(
        body, name="bias_fold", out_shape=jax.ShapeDtypeStruct((N_HEADS, REL_PAD), F32),
        scratch_shapes=[pltpu.VMEM((N_HEADS, REL_EXT), F32)],
        compiler_params=pltpu.CompilerParams(vmem_limit_bytes=VMEM_LIMIT_BYTES),
    )(dbias, onehot_t)


def _head_lanes():
    lane = lax.broadcasted_iota(jnp.int32, (1, 2 * HEAD_DIM), 1)
    return [lane < HEAD_DIM, lane >= HEAD_DIM]


def _only(mask, x, scale=None):
    x = jnp.where(mask, x, jnp.zeros_like(x))
    return x if scale is None else x * scale


def _contract_lanes(a, b):
    return lax.dot_general(a, b, (((1,), (1,)), ((), ())), preferred_element_type=F32)


def _contract_rows(a, b):
    return lax.dot_general(a, b, (((0,), (0,)), ((), ())), preferred_element_type=F32)


PAIR = 2 * HEAD_DIM
N_PAIRS = D_ATTN // PAIR


def _attn_specs(pairs):
    width = pairs * PAIR
    per = D_ATTN // width
    row_spec = pl.BlockSpec((Q_BLOCK, width), lambda g, i: (i, g))
    kv_specs = [pl.BlockSpec((Q_BLOCK, width),
                             functools.partial(lambda g, i, kk, c0: (jnp.maximum(i + kk - N_START, 0), c0 + g),
                                               kk=kk, c0=c0))
                for c0 in (per, 2 * per) for kk in range(K_WIN // Q_BLOCK)]
    bias_spec = pl.BlockSpec((None, 2 * pairs, Q_BLOCK, K_WIN), lambda g, i: (jnp.minimum(i, N_START), g, 0, 0))
    return row_spec, kv_specs, bias_spec


def _attn_fwd(proj, bias, pairs=N_PAIRS):
    t = proj.shape[0]
    n_win = K_WIN // Q_BLOCK

    def body(q_ref, *refs):
        k_refs, v_refs = refs[:n_win], refs[n_win:2 * n_win]
        b_ref, o_ref, lse_ref = refs[2 * n_win:]
        for pp in range(pairs):
            cols = slice(pp * PAIR, (pp + 1) * PAIR)
            k = jnp.concatenate([r[:, cols] for r in k_refs], axis=0)
            v = jnp.concatenate([r[:, cols] for r in v_refs], axis=0)
            q = q_ref[:, cols]
            o = lse = None
            for hh, lanes in enumerate(_head_lanes()):
                s = _contract_lanes(_only(lanes, q, HEAD_DIM ** -0.5), k) + b_ref[2 * pp + hh]
                m = jnp.max(s, axis=1, keepdims=True)
                p = jnp.exp(s - m)
                l = jnp.sum(p, axis=1, keepdims=True)
                oh = jnp.dot(p.astype(BF16), v, preferred_element_type=F32) / l
                lse_h = jnp.broadcast_to(m + jnp.log(l), oh.shape)
                o, lse = (oh, lse_h) if o is None else (jnp.where(lanes, oh, o), jnp.where(lanes, lse_h, lse))
            o_ref[:, cols] = o.astype(o_ref.dtype)
            lse_ref[:, cols] = lse

    row_spec, kv_specs, bias_spec = _attn_specs(pairs)
    return pl.pallas_call(
        body, name="attn_fwd", grid=(N_PAIRS // pairs, t // Q_BLOCK),
        in_specs=[row_spec] + kv_specs + [bias_spec], out_specs=[row_spec, row_spec],
        out_shape=[jax.ShapeDtypeStruct((t, D_ATTN), BF16), jax.ShapeDtypeStruct((t, D_ATTN), F32)],
        compiler_params=_cparams(2),
    )(*([proj] * (1 + 2 * n_win)), bias)


def _attn_bwd(proj, bias, att, lse, datt, pairs=2):
    t = proj.shape[0]
    n_win = K_WIN // Q_BLOCK
    n_blocks = t // Q_BLOCK

    def body(q_ref, *refs):
        k_refs, v_refs = refs[:n_win], refs[n_win:2 * n_win]
        b_ref, o_ref, lse_ref, do_ref, dq_ref, dk_ref, dv_ref, db_ref, dk_acc, dv_acc = refs[2 * n_win:]
        i = pl.program_id(1)

        @pl.when(i == 0)
        def _():
            dk_acc[...] = jnp.zeros_like(dk_acc)
            dv_acc[...] = jnp.zeros_like(dv_acc)
            db_ref[...] = jnp.zeros_like(db_ref)

        rows = pl.ds(pl.multiple_of(i * Q_BLOCK, Q_BLOCK), K_WIN)
        scale = HEAD_DIM ** -0.5
        for pp in range(pairs):
            cols = slice(pp * PAIR, (pp + 1) * PAIR)
            k = jnp.concatenate([r[:, cols] for r in k_refs], axis=0)
            v = jnp.concatenate([r[:, cols] for r in v_refs], axis=0)
            q, do, o = q_ref[:, cols], do_ref[:, cols], o_ref[:, cols].astype(F32)
            dq = dk = dv = None
            for hh, lanes in enumerate(_head_lanes()):
                qh, doh = _only(lanes, q, scale), _only(lanes, do)
                s = _contract_lanes(qh, k) + b_ref[2 * pp + hh]
                lse_col = pp * PAIR + hh * HEAD_DIM
                p = jnp.exp(s - lse_ref[:, lse_col:lse_col + 1])
                delta = jnp.sum(doh.astype(F32) * o, axis=1, keepdims=True)
                ds = p * (_contract_lanes(doh, v) - delta)
                folded = ds[:CHUNK]
                for c in range(1, Q_BLOCK // CHUNK):
                    folded = folded + pltpu.roll(ds[c * CHUNK:(c + 1) * CHUNK], K_WIN - c * CHUNK, 1)
                db_ref[2 * pp + hh] += folded
                dsb = ds.astype(BF16)
                dqh = jnp.dot(dsb, k, preferred_element_type=F32)
                dq = dqh if dq is None else jnp.where(lanes, dqh, dq)
                dkh, dvh = _contract_rows(dsb, qh), _contract_rows(p.astype(BF16), doh)
                dk, dv = (dkh, dvh) if dk is None else (dk + dkh, dv + dvh)
            dq_ref[:, cols] = (dq * scale).astype(dq_ref.dtype)
            dk_acc[rows, cols] += dk
            dv_acc[rows, cols] += dv

        @pl.when(i == n_blocks - 1)
        def _():
            dk_ref[...] = dk_acc[K_PAD:, :].astype(dk_ref.dtype)
            dv_ref[...] = dv_acc[K_PAD:, :].astype(dv_ref.dtype)

    width = pairs * PAIR
    row_spec, kv_specs, bias_spec = _attn_specs(pairs)
    full_spec = pl.BlockSpec((t, width), lambda g, i: (0, g))
    return pl.pallas_call(
        body, name="attn_bwd", grid=(N_PAIRS // pairs, n_blocks),
        in_specs=[row_spec] + kv_specs + [bias_spec, row_spec, row_spec, row_spec],
        out_specs=[row_spec, full_spec, full_spec,
                   pl.BlockSpec((2 * pairs, CHUNK, K_WIN), lambda g, i: (g, 0, 0))],
        out_shape=[jax.ShapeDtypeStruct((t, D_ATTN), BF16)] * 3 + [jax.ShapeDtypeStruct((N_HEADS, CHUNK, K_WIN), F32)],
        scratch_shapes=[pltpu.VMEM((t + K_PAD, width), F32)] * 2, compiler_params=_cparams(2),
    )(*([proj] * (1 + 2 * n_win)), bias, att, lse, datt)


CONV_LEAD = CONV_HALO - (CONV_WIDTH - 1)
CONV_LANES = 128
CONV_ROWS = 64


def _conv_specs(t):
    tt = _row_tile(t, CONV_TILE)
    per = tt // CONV_HALO
    n_halo = t // CONV_HALO
    tile = lambda cb: pl.BlockSpec((tt, COL), functools.partial(lambda i, cb: (i, cb), cb=cb))
    prev = lambda cb: pl.BlockSpec((CONV_HALO, COL),
                                   functools.partial(lambda i, cb: (jnp.maximum(i * per - 1, 0), cb), cb=cb))
    nxt = lambda cb: pl.BlockSpec((CONV_HALO, COL),
                                  functools.partial(lambda i, cb: (jnp.minimum((i + 1) * per, n_halo - 1), cb), cb=cb))
    vec = lambda shape: pl.BlockSpec(shape, lambda i: (0, 0))
    return tt, tile, prev, nxt, vec


def _glu(ca, cg, bias):
    return (ca.astype(F32) + bias[:, :D_CONV]) * jax.nn.sigmoid(cg.astype(F32) + bias[:, D_CONV:])


SUBLANES = 8


def _shift_copies(ext_ref):
    n = ext_ref.shape[1] - SUBLANES
    for s in range(1, SUBLANES):
        ext_ref[s, 0:n, :] = ext_ref[0, s:s + n, :]


def _tap_tiles(ext_ref, first_row, r0, lanes):
    n_g = CONV_ROWS // SUBLANES
    for s in range(SUBLANES):
        taps = [w for w in range(CONV_WIDTH) if first_row(w) % SUBLANES == s]
        if not taps:
            continue
        lo = min(first_row(w) for w in taps) - s
        n_tiles = (max(first_row(w) for w in taps) - s - lo) // SUBLANES + n_g
        tiles = [ext_ref[s, r0 + lo + SUBLANES * b:r0 + lo + SUBLANES * (b + 1), lanes] for b in range(n_tiles)]
        for w in taps:
            k = (first_row(w) - s - lo) // SUBLANES
            yield w, tiles[k:k + n_g]


def _taps(ext_ref, tt, first_row, w_ref, out_ref):
    n_g = CONV_ROWS // SUBLANES
    for l0 in range(0, D_CONV, CONV_LANES):
        lanes = slice(l0, l0 + CONV_LANES)
        for r0 in range(0, tt, CONV_ROWS):
            acc = [jnp.zeros((SUBLANES, CONV_LANES), F32)] * n_g
            for w, tiles in _tap_tiles(ext_ref, first_row, r0, lanes):
                weight = jnp.broadcast_to(w_ref[w:w + 1, lanes], (SUBLANES, CONV_LANES))
                acc = [a + t * weight for a, t in zip(acc, tiles)]
            for g in range(n_g):
                out_ref[r0 + SUBLANES * g:r0 + SUBLANES * (g + 1), lanes] = acc[g]


def _tap_sums(ext_ref, tt, first_row, x_ref, out_ref):
    n_g = CONV_ROWS // SUBLANES
    for l0 in range(0, D_CONV, CONV_LANES):
        lanes = slice(l0, l0 + CONV_LANES)
        acc = [jnp.zeros((SUBLANES, CONV_LANES), F32)] * CONV_WIDTH
        for r0 in range(0, tt, CONV_ROWS):
            x = [x_ref[0, r0 + SUBLANES * g:r0 + SUBLANES * (g + 1), lanes] for g in range(n_g)]
            for w, tiles in _tap_tiles(ext_ref, first_row, r0, lanes):
                part = tiles[0] * x[0]
                for g in range(1, n_g):
                    part = part + tiles[g] * x[g]
                acc[w] = acc[w] + part
        for w in range(CONV_WIDTH):
            out_ref[w:w + 1, lanes] += jnp.sum(acc[w], axis=0, keepdims=True)


def _conv_fwd(proj, glu_bias, dw, dw_b, ln_g, ln_b):
    t = proj.shape[0]
    tt, tile, prev, nxt, vec = _conv_specs(t)
    ca_blk, cg_blk = 3 * D_ATTN // COL, 3 * D_ATTN // COL + 1

    def body(ca_ref, cg_ref, pa_ref, pg_ref, gb_ref, dw_ref, dwb_ref, g_ref, b_ref, cs_ref, c_ref, z_ref, ext_ref):
        i = pl.program_id(0)
        bias = gb_ref[...]
        c = _glu(ca_ref[...], cg_ref[...], bias)
        halo = _glu(pa_ref[...], pg_ref[...], bias)
        ext_ref[0, 0:CONV_HALO, :] = jnp.where(i == 0, 0.0, halo)
        ext_ref[0, CONV_HALO:, :] = c
        _shift_copies(ext_ref)
        c_ref[...] = c
        _taps(ext_ref, tt, lambda w: CONV_LEAD + w, dw_ref, z_ref)
        z = z_ref[...] + dwb_ref[...]
        z_ref[...] = z
        mu = jnp.mean(z, axis=-1, keepdims=True)
        zc = z - mu
        y = zc * lax.rsqrt(jnp.mean(zc * zc, axis=-1, keepdims=True) + EPS) * g_ref[...] + b_ref[...]
        cs_ref[...] = (y * jax.nn.sigmoid(y)).astype(cs_ref.dtype)

    out_spec = pl.BlockSpec((tt, D_CONV), lambda i: (i, 0))
    return pl.pallas_call(
        body, name="conv_fwd", grid=(t // tt,),
        in_specs=[tile(ca_blk), tile(cg_blk), prev(ca_blk), prev(cg_blk), vec(glu_bias.shape), vec(dw.shape),
                  vec(dw_b.shape), vec(ln_g.shape), vec(ln_b.shape)],
        out_specs=[out_spec] * 3,
        out_shape=[jax.ShapeDtypeStruct((t, D_CONV), BF16), jax.ShapeDtypeStruct((t, D_CONV), F32),
                   jax.ShapeDtypeStruct((t, D_CONV), F32)],
        scratch_shapes=[pltpu.VMEM((SUBLANES, tt + CONV_HALO, D_CONV), F32)], compiler_params=_cparams(1),
    )(proj, proj, proj, proj, glu_bias, dw, dw_b, ln_g, ln_b)


def _conv_bwd(proj, c, z, dcs, glu_bias, dw, ln_g, ln_b):
    t = proj.shape[0]
    tt, tile, prev, nxt, vec = _conv_specs(t)
    n_tiles = t // tt
    ca_blk, cg_blk = 3 * D_ATTN // COL, 3 * D_ATTN // COL + 1

    def ln_bwd(zv, dcsv, g, b):
        mu = jnp.mean(zv, axis=-1, keepdims=True)
        zc = zv - mu
        rstd = lax.rsqrt(jnp.mean(zc * zc, axis=-1, keepdims=True) + EPS)
        zhat = zc * rstd
        y = zhat * g + b
        sig = jax.nn.sigmoid(y)
        dy = dcsv * sig * (1.0 + y * (1.0 - sig))
        dzh = dy * g
        dz = rstd * (dzh - jnp.mean(dzh, axis=-1, keepdims=True) - zhat * jnp.mean(dzh * zhat, axis=-1, keepdims=True))
        return dz, dy, zhat

    def body(ca_ref, cg_ref, c_ref, cprev_ref, z_ref, znext_ref, dcs_ref, dcsnext_ref, gb_ref, dw_ref, g_ref, b_ref,
             dcin_ref, ddw_ref, ddwb_ref, dg_ref, db_ref, dgb_ref, cext_ref, dzext_ref, dc_ref):
        i = pl.program_id(0)

        @pl.when(i == 0)
        def _():
            for ref in (ddw_ref, ddwb_ref, dg_ref, db_ref, dgb_ref):
                ref[...] = jnp.zeros_like(ref)

        g, b = g_ref[...], b_ref[...]
        dz, dy, zhat = ln_bwd(z_ref[...], dcs_ref[...], g, b)
        dz_next, _, _ = ln_bwd(znext_ref[...], dcsnext_ref[...], g, b)
        dg_ref[...] += _colsum(dy * zhat)
        db_ref[...] += _colsum(dy)
        ddwb_ref[...] += _colsum(dz)
        dzext_ref[0, 0:tt, :] = dz
        dzext_ref[0, tt:, :] = jnp.where(i == n_tiles - 1, 0.0, dz_next)
        _shift_copies(dzext_ref)
        cext_ref[0, 0:CONV_HALO, :] = jnp.where(i == 0, 0.0, cprev_ref[...])
        cext_ref[0, CONV_HALO:, :] = c_ref[...]
        _shift_copies(cext_ref)
        _taps(dzext_ref, tt, lambda w: CONV_WIDTH - 1 - w, dw_ref, dc_ref)
        _tap_sums(cext_ref, tt, lambda w: CONV_LEAD + w, dzext_ref, ddw_ref)
        bias = gb_ref[...]
        a_in = ca_ref[...].astype(F32) + bias[:, :D_CONV]
        sg = jax.nn.sigmoid(cg_ref[...].astype(F32) + bias[:, D_CONV:])
        dc = dc_ref[...]
        dcin = jnp.concatenate([dc * sg, dc * a_in * sg * (1.0 - sg)], axis=1)
        dcin_ref[...] = dcin.astype(dcin_ref.dtype)
        dgb_ref[...] += _colsum(dcin)

    row = lambda: pl.BlockSpec((tt, D_CONV), lambda i: (i, 0))
    per = tt // CONV_HALO
    n_halo = t // CONV_HALO
    prev_row = pl.BlockSpec((CONV_HALO, D_CONV), lambda i: (jnp.maximum(i * per - 1, 0), 0))
    next_row = lambda: pl.BlockSpec((CONV_HALO, D_CONV), lambda i: (jnp.minimum((i + 1) * per, n_halo - 1), 0))
    acc = lambda shape: pl.BlockSpec(shape, lambda i: (0, 0))
    return pl.pallas_call(
        body, name="conv_bwd", grid=(n_tiles,),
        in_specs=[tile(ca_blk), tile(cg_blk), row(), prev_row, row(), next_row(), row(), next_row(),
                  vec(glu_bias.shape), vec(dw.shape), vec(ln_g.shape), vec(ln_b.shape)],
        out_specs=[pl.BlockSpec((tt, 2 * D_CONV), lambda i: (i, 0)), acc(dw.shape), acc((1, D_CONV)),
                   acc((1, D_CONV)), acc((1, D_CONV)), acc((1, 2 * D_CONV))],
        out_shape=[jax.ShapeDtypeStruct((t, 2 * D_CONV), BF16), jax.ShapeDtypeStruct(dw.shape, F32),
                   jax.ShapeDtypeStruct((1, D_CONV), F32), jax.ShapeDtypeStruct((1, D_CONV), F32),
                   jax.ShapeDtypeStruct((1, D_CONV), F32), jax.ShapeDtypeStruct((1, 2 * D_CONV), F32)],
        scratch_shapes=[pltpu.VMEM((SUBLANES, tt + CONV_HALO, D_CONV), F32),
                        pltpu.VMEM((SUBLANES, tt + CONV_HALO, D_CONV), F32), pltpu.VMEM((tt, D_CONV), F32)],
        compiler_params=_cparams(1),
    )(proj, proj, c, c, z, z, dcs, dcs, glu_bias, dw, ln_g, ln_b)


def _place():
    x, y, c = lax.axis_index("x"), lax.axis_index("y"), lax.axis_index("c")
    chips = [(1 - x, y), (x, 1 - y), (1 - x, 1 - y)]
    return x, y, c, chips


def _chip_index(chip):
    return 2 * chip[0] + chip[1]


def _half_rows(c, half):
    return pl.ds(pl.multiple_of(c * half, 16), half)


def _gather_carry(blocked):
    n = len(blocked)

    def over_ici(o_refs, send_sems, recv_sems):
        x, y, c, chips = _place()
        me = _chip_index((x, y))
        copies = []
        for a in range(n):
            mine = o_refs[a].at[me, _half_rows(c, o_refs[a].shape[1] // 2), :]
            for k, chip in enumerate(chips):
                copies.append(pltpu.make_async_remote_copy(
                    src_ref=mine, dst_ref=mine, send_sem=send_sems.at[6 * a + k], recv_sem=recv_sems.at[6 * a + k],
                    device_id=(chip[0], chip[1], c), device_id_type=MESH))
        return copies

    def to_sibling(o_refs, send_sems, recv_sems, sent_by_me):
        x, y, c, chips = _place()
        copies = []
        for a in range(n):
            rows = _half_rows(c if sent_by_me else 1 - c, o_refs[a].shape[1] // 2)
            for k, chip in enumerate(chips):
                landed = o_refs[a].at[_chip_index(chip), rows, :]
                copies.append(pltpu.make_async_remote_copy(
                    src_ref=landed, dst_ref=landed, send_sem=send_sems.at[6 * a + 3 + k],
                    recv_sem=recv_sems.at[6 * a + 3 + k], device_id=(x, y, 1 - c), device_id_type=MESH))
        return copies

    def start(ins, outs, sems):
        for cp in over_ici(outs, *sems):
            cp.start()

    def hand_on(ins, outs, sems):
        for arrived, onward in zip(over_ici(outs, *sems), to_sibling(outs, *sems, True)):
            arrived.wait_recv()
            onward.start()

    def finish(ins, outs, sems):
        for cp in to_sibling(outs, *sems, False):
            cp.wait_recv()
        for cp in over_ici(outs, *sems) + to_sibling(outs, *sems, True):
            cp.wait_send()

    return _Carry(
        ins=list(blocked), outs=[jax.ShapeDtypeStruct(w.shape, w.dtype) for w in blocked],
        aliases={a: a for a in range(n)},
        sems=[pltpu.SemaphoreType.DMA((6 * n,)), pltpu.SemaphoreType.DMA((6 * n,))],
        phases=[("first", start), ("late", hand_on), ("last", finish)])


def _pair_exchange(name, grads):
    n = len(grads)

    def body(*refs):
        g_refs, land_refs = refs[:n], refs[n:2 * n]
        send_sems, recv_sems = refs[2 * n:]
        x, y, c, _ = _place()
        copies = []
        for a in range(n):
            half = g_refs[a].shape[1] // 2
            cp = pltpu.make_async_remote_copy(
                src_ref=g_refs[a].at[:, _half_rows(1 - c, half), :], dst_ref=land_refs[a],
                send_sem=send_sems.at[a], recv_sem=recv_sems.at[a], device_id=(x, y, 1 - c), device_id_type=MESH)
            cp.start()
            copies.append(cp)
        for cp in copies:
            cp.wait()

    return pl.pallas_call(
        body, name=name, in_specs=[ANY] * n, out_specs=[ANY] * n,
        out_shape=[jax.ShapeDtypeStruct((g.shape[0], g.shape[1] // 2, g.shape[2]), g.dtype) for g in grads],
        scratch_shapes=[pltpu.SemaphoreType.DMA((n,)), pltpu.SemaphoreType.DMA((n,))],
    )(*grads)


def _pair_carry(grads):
    n = len(grads)

    def copies(g_refs, land_refs, send_sems, recv_sems):
        x, y, c, _ = _place()
        return [pltpu.make_async_remote_copy(
            src_ref=g_refs[a].at[:, _half_rows(1 - c, g_refs[a].shape[1] // 2), :], dst_ref=land_refs[a],
            send_sem=send_sems.at[a], recv_sem=recv_sems.at[a], device_id=(x, y, 1 - c), device_id_type=MESH)
            for a in range(n)]

    def start(ins, outs, sems):
        for cp in copies(ins, outs, *sems):
            cp.start()

    def finish(ins, outs, sems):
        for cp in copies(ins, outs, *sems):
            cp.wait()

    return _Carry(
        ins=list(grads), aliases={},
        outs=[jax.ShapeDtypeStruct((g.shape[0], g.shape[1] // 2, g.shape[2]), g.dtype) for g in grads],
        sems=[pltpu.SemaphoreType.DMA((n,)), pltpu.SemaphoreType.DMA((n,))],
        phases=[("first", start), ("last", finish)])


def _to_owner_carry(parts):
    n = len(parts)

    def sends(p_refs, l_refs, send_sems, recv_sems):
        x, y, c, chips = _place()
        me = _chip_index((x, y))
        return [pltpu.make_async_remote_copy(
            src_ref=p_refs[a].at[_chip_index(chip)], dst_ref=l_refs[a].at[me],
            send_sem=send_sems.at[3 * a + k], recv_sem=recv_sems.at[3 * a + k],
            device_id=(chip[0], chip[1], c), device_id_type=MESH) for a in range(n) for k, chip in enumerate(chips)]

    def start(ins, outs, sems):
        for cp in sends(ins, outs, *sems):
            cp.start()

    def finish(ins, outs, sems):
        x, y, c, chips = _place()
        send_sems, recv_sems = sems
        for a in range(n):
            for k, chip in enumerate(chips):
                slot = outs[a].at[_chip_index(chip)]
                pltpu.make_async_remote_copy(
                    src_ref=slot, dst_ref=slot, send_sem=send_sems.at[3 * a + k], recv_sem=recv_sems.at[3 * a + k],
                    device_id=(chip[0], chip[1], c), device_id_type=MESH).wait_recv()
        for cp in sends(ins, outs, *sems):
            cp.wait_send()

    return _Carry(
        ins=list(parts), outs=[jax.ShapeDtypeStruct(p.shape, p.dtype) for p in parts], aliases={},
        sems=[pltpu.SemaphoreType.DMA((3 * n,)), pltpu.SemaphoreType.DMA((3 * n,))],
        phases=[("first", start), ("last", finish)])


def _swap_halves(halves):
    n = len(halves)

    def body(*refs):
        h_refs, o_refs = refs[:n], refs[n:2 * n]
        send_sems, recv_sems = refs[2 * n:]
        x, y, c, _ = _place()
        copies = []
        for a in range(n):
            cp = pltpu.make_async_remote_copy(
                src_ref=h_refs[a], dst_ref=o_refs[a], send_sem=send_sems.at[a], recv_sem=recv_sems.at[a],
                device_id=(x, y, 1 - c), device_id_type=MESH)
            cp.start()
            copies.append(cp)
        for cp in copies:
            cp.wait()

    return pl.pallas_call(
        body, name="grad_swap_halves", in_specs=[ANY] * n, out_specs=[ANY] * n,
        out_shape=[jax.ShapeDtypeStruct(h.shape, h.dtype) for h in halves],
        scratch_shapes=[pltpu.SemaphoreType.DMA((n,)), pltpu.SemaphoreType.DMA((n,))],
    )(*halves)


def _all_devices(name, block):
    r, cols = block.shape

    def body(b_ref, all_ref, sum_ref, send_sems, recv_sems):
        x, y, c, _ = _place()
        me = 4 * x + 2 * y + c
        all_ref[me] = b_ref[...]
        flips = [(fx, fy, fc) for fx in (0, 1) for fy in (0, 1) for fc in (0, 1)][1:]
        copies = []
        for k, (fx, fy, fc) in enumerate(flips):
            cp = pltpu.make_async_remote_copy(
                src_ref=b_ref, dst_ref=all_ref.at[me], send_sem=send_sems.at[k], recv_sem=recv_sems.at[k],
                device_id=(x ^ fx, y ^ fy, c ^ fc), device_id_type=MESH)
            cp.start()
            copies.append(cp)
        for k, (fx, fy, fc) in enumerate(flips):
            slot = all_ref.at[4 * (x ^ fx) + 2 * (y ^ fy) + (c ^ fc)]
            pltpu.make_async_remote_copy(
                src_ref=slot, dst_ref=slot, send_sem=send_sems.at[k], recv_sem=recv_sems.at[k],
                device_id=(x ^ fx, y ^ fy, c ^ fc), device_id_type=MESH).wait_recv()
        for cp in copies:
            cp.wait_send()
        acc = all_ref[0]
        for d in range(1, N_DEV):
            acc = acc + all_ref[d]
        sum_ref[...] = acc

    vmem = pl.BlockSpec(memory_space=pltpu.VMEM)
    return pl.pallas_call(
        body, name=name, in_specs=[vmem], out_specs=[vmem, vmem],
        out_shape=[jax.ShapeDtypeStruct((N_DEV, r, cols), F32), jax.ShapeDtypeStruct((r, cols), F32)],
        scratch_shapes=[pltpu.SemaphoreType.DMA((N_DEV - 1,)), pltpu.SemaphoreType.DMA((N_DEV - 1,))],
    )(block)


PACK = 1024


def _packed_rows(shape, width):
    size, last = int(np.prod(shape)), shape[-1]
    cols = last if last <= width else width
    assert size % cols == 0
    return size // cols, cols


def _pack(vals, width=PACK):
    rows = []
    for v in vals:
        n_rows, cols = _packed_rows(v.shape, width)
        rows.append(jnp.pad(v.reshape(n_rows, cols).astype(F32), ((0, 0), (0, width - cols))))
    buf = jnp.concatenate(rows, axis=0)
    return jnp.pad(buf, ((0, (-buf.shape[0]) % 8), (0, 0)))


def _unpack(buf, shapes, width=PACK):
    out, r = [], 0
    for shape in shapes:
        n_rows, cols = _packed_rows(shape, width)
        out.append(buf[r:r + n_rows, :cols].reshape(shape))
        r += n_rows
    return out


FFN_SPLIT = 2


def _ffn_hidden(name, n, wg, wu, tm=1024, carry=None):
    m, k = n.shape
    f = wg.shape[0]
    fb = f // FFN_SPLIT
    tm = _row_tile(m, tm)

    def core(n_ref, wg_ref, wu_ref, a_ref, b_ref, s_ref):
        nv = n_ref[...]
        a = _dot(nv, wg_ref[...], True)
        b = _dot(nv, wu_ref[...], True)
        a_ref[...] = a.astype(a_ref.dtype)
        b_ref[...] = b.astype(b_ref.dtype)
        s_ref[...] = (a * jax.nn.sigmoid(a) * b).astype(s_ref.dtype)

    w_spec = pl.BlockSpec((fb, k), lambda j, i: (j, 0))
    out_spec = pl.BlockSpec((tm, fb), lambda j, i: (i, j))
    return _call(name, core, (FFN_SPLIT, m // tm), [pl.BlockSpec((tm, k), lambda j, i: (i, 0)), w_spec, w_spec],
                 [out_spec] * 3, [jax.ShapeDtypeStruct((m, f), BF16)] * 3, [], [n, wg, wu], carry)


def _ffn_d_hidden(name, df, wd, a, b, tm=512):
    m, k = df.shape
    f = wd.shape[0]
    fb = f // FFN_SPLIT
    tm = _row_tile(m, tm)

    def body(df_ref, wd_ref, a_ref, b_ref, da_ref, db_ref):
        dfv = df_ref[...]
        for j in range(FFN_SPLIT):
            cols = slice(j * fb, (j + 1) * fb)
            ds = _dot(dfv, wd_ref[cols, :], True)
            av, bv = a_ref[:, cols].astype(F32), b_ref[:, cols].astype(F32)
            sig = jax.nn.sigmoid(av)
            da_ref[:, cols] = (ds * bv * sig * (1.0 + av * (1.0 - sig))).astype(da_ref.dtype)
            db_ref[:, cols] = (ds * av * sig).astype(db_ref.dtype)

    row = pl.BlockSpec((tm, f), lambda i: (i, 0))
    return pl.pallas_call(
        body, name=name, grid=(m // tm,),
        in_specs=[pl.BlockSpec((tm, k), lambda i: (i, 0)), _resident(wd), row, row],
        out_specs=[row, row], out_shape=[jax.ShapeDtypeStruct((m, f), BF16)] * 2,
        compiler_params=_cparams(1),
    )(df, wd, a, b)


def kernel(x, ffn1_norm_pre, ffn1_w_gate, ffn1_w_up, ffn1_w_down, ffn1_norm_post, mix_norm_pre, w_in, gate_bias, rel_table, w_attn_out, conv_glu_bias, conv_dw_w, conv_dw_b, conv_ln_g, conv_ln_b, conv_w_out, w_out, mix_norm_post, ffn2_norm_pre, ffn2_w_gate, ffn2_w_up, ffn2_w_down, ffn2_norm_post, loss_target, m_ffn1_norm_pre, m_ffn1_w_gate, m_ffn1_w_up, m_ffn1_w_down, m_ffn1_norm_post, m_mix_norm_pre, m_w_in, m_gate_bias, m_rel_table, m_w_attn_out, m_conv_glu_bias, m_conv_dw_w, m_conv_dw_b, m_conv_ln_g, m_conv_ln_b, m_conv_w_out, m_w_out, m_mix_norm_post, m_ffn2_norm_pre, m_ffn2_w_gate, m_ffn2_w_up, m_ffn2_w_down, m_ffn2_norm_post, v_ffn1_norm_pre, v_ffn1_w_gate, v_ffn1_w_up, v_ffn1_w_down, v_ffn1_norm_post, v_mix_norm_pre, v_w_in, v_gate_bias, v_rel_table, v_w_attn_out, v_conv_glu_bias, v_conv_dw_w, v_conv_dw_b, v_conv_ln_g, v_conv_ln_b, v_conv_w_out, v_w_out, v_mix_norm_post, v_ffn2_norm_pre, v_ffn2_w_gate, v_ffn2_w_up, v_ffn2_w_down, v_ffn2_norm_post):
    args = dict(locals())
    names = ['ffn1_norm_pre', 'ffn1_w_gate', 'ffn1_w_up', 'ffn1_w_down', 'ffn1_norm_post', 'mix_norm_pre', 'w_in',
             'gate_bias', 'rel_table', 'w_attn_out', 'conv_glu_bias', 'conv_dw_w', 'conv_dw_b', 'conv_ln_g',
             'conv_ln_b', 'conv_w_out', 'w_out', 'mix_norm_post', 'ffn2_norm_pre', 'ffn2_w_gate', 'ffn2_w_up',
             'ffn2_w_down', 'ffn2_norm_post']
    big = ['ffn1_w_gate', 'ffn1_w_up', 'ffn1_w_down', 'w_in', 'w_attn_out', 'conv_w_out', 'w_out', 'ffn2_w_gate',
           'ffn2_w_up', 'ffn2_w_down']
    small = [n for n in names if n not in big]

    xs, target = x[0], loss_target[0]
    t, d = xs.shape
    cx, cy = lax.axis_index("x"), lax.axis_index("y")
    chip = 2 * cx + cy

    dw_shard = conv_dw_w[0, :, 0, :]
    cshard = dw_shard.shape[1]
    dw_all, _ = _all_devices("gather_dw", _pack([dw_shard], width=cshard))
    dw_full = jnp.concatenate([dw_all[2 * j, :CONV_WIDTH, :cshard] for j in range(N_CHIPS)], axis=1)
    dw_full = jnp.pad(dw_full, ((0, CONV_HALO - CONV_WIDTH), (0, 0)))
    peers = [(1 - cx, cy), (cx, 1 - cy), (1 - cx, 1 - cy)]
    pos = jnp.stack([lax.axis_index("c"), chip] + [_chip_index(p) for p in peers]).astype(jnp.int32)
    transposed = ("ffn1_w_gate", "ffn1_w_up", "ffn2_w_gate", "ffn2_w_up")
    weight_of = lambda n: n[2:] if n[:2] in ("m_", "v_") else n
    shard = lambda n: jnp.transpose(args[n][0]) if weight_of(n) in transposed else args[n][0]
    unshard = lambda n, v: (jnp.transpose(v) if n in transposed else v)[None]
    own = {n: _cast_into("cast_" + n, pos, shard(n)) for n in big}
    gather = lambda *ns: _gather_carry([own[n] for n in ns])
    res_spec = [(d, F32), (d, F32), (d, BF16)]
    whole = lambda w: w.reshape(-1, w.shape[-1])

    table_pad = jnp.pad(rel_table[0], ((0, 0), (0, REL_PAD - rel_table.shape[2])))
    bias, (wg1,) = _bias_expand(table_pad, gather("ffn1_w_gate"))
    n1, (wu1,) = _rms_fwd("ffn1_pre", xs, ffn1_norm_pre, gather("ffn1_w_up"))
    (a1, b1, s1), (wd1, win, wao, wco, wout) = _ffn_hidden(
        "ffn1_hidden", n1, whole(wg1), whole(wu1),
        carry=gather("ffn1_w_down", "w_in", "w_attn_out", "conv_w_out", "w_out"))
    (f1, h1, u), (wg2,) = _mm_kblk(
        "ffn1_down", [(s1, whole(wd1)[None])], trans_w=False, epilogue=_ep_post_res_pre(0.5), rows=[xs],
        vecs=[ffn1_norm_post, mix_norm_pre], row_outs=res_spec, carry=gather("ffn2_w_gate"))
    proj, (wu2, wd2) = _mm_nblk("mix_in", u, win, trans_w=False, out_blocked=False, out_dtype=BF16,
                                carry=gather("ffn2_w_up", "ffn2_w_down"))
    att, lse = _attn_fwd(proj, bias)
    cs, c_glu, z_conv = _conv_fwd(proj, conv_glu_bias, dw_full, conv_dw_b, conv_ln_g, conv_ln_b)
    y_a, y_b, merged = _mix_merge(att, cs, wao, wco, proj, gate_bias)
    (mo, h2, n2), _ = _mm_kblk(
        "mix_out", [(merged, wout)], trans_w=False, epilogue=_ep_post_res_pre(1.0), rows=[h1],
        vecs=[mix_norm_post, ffn2_norm_pre], row_outs=res_spec)
    (a2, b2, s2), _ = _ffn_hidden("ffn2_hidden", n2, whole(wg2), whole(wu2))
    g = {}
    (dy, df2, err2, g["ffn2_norm_post"]), _ = _mm_kblk(
        "ffn2_down", [(s2, whole(wd2)[None])], trans_w=False, epilogue=_ep_loss(0.5, d), rows=[h2, target],
        vecs=[ffn2_norm_post], row_outs=[(d, F32), (d, BF16)], vec_outs=[d, d])
    loss = lax.psum(0.5 * jnp.sum(err2) / d, ("x", "y", "c"))

    parts, landed = {}, {}

    def ffn_bwd(tag, df, n, a, b, s, wg, wu, wd, **epilogue):
        da, db = _ffn_d_hidden(tag + "_d_hidden", df, whole(wd), a, b)
        group = [tag + "_w_down", tag + "_w_gate", tag + "_w_up"]
        local, theirs = [], []
        for what, hidden, other in (("down", s, df), ("gate", da, n), ("up", db, n)):
            carry = _pair_carry(local[-1:]) if local else None
            out = _mm_tn_wide(tag + "_g_" + what, hidden, other, d, a_split=FFN_SPLIT, carry=carry)
            if local:
                out, landed = out
                theirs += landed
            local.append(out.reshape(wd.shape))
        theirs += _pair_exchange("pair_" + tag, local[-1:])
        return _mm_kblk(tag + "_d_n", [(da, whole(wg)[None]), (db, whole(wu)[None])], trans_w=False,
                        carry=pair_sums(tag, group, local, theirs), **epilogue), group

    def pair_sums(tag, group, local, theirs=None):
        theirs = theirs or _pair_exchange("pair_" + tag, local)
        for n, mine, other in zip(group, local, theirs):
            parts[n] = _add_pair("pair_sum_" + n, pos, mine, other)
        return _to_owner_carry([parts[n] for n in group])

    def keep(group, carried):
        for n, val in zip(group, carried):
            landed[n] = val

    ((dh2, dmo, g["ffn2_norm_pre"], g["mix_norm_post"]), carried), group = ffn_bwd(
        "ffn2", df2, n2, a2, b2, s2, wg2, wu2, wd2, epilogue=_ep_pre_bwd_post(1.0), rows=[h2, dy, mo],
        vecs=[ffn2_norm_pre, mix_norm_post], row_outs=[(d, F32), (d, BF16)], vec_outs=[d, d])
    keep(group, carried)
    g_wout = _mm_tn("mix_g_out", merged, "col", dmo, "full")
    dy_a, dy_b, dgates, datt, dcs, g["gate_bias"] = _mix_d_merge(dmo, wout, y_a, y_b, wao, wco, proj, gate_bias)
    g_wao = _mm_tn("attn_g_out", att, "full", dy_a, "col")
    g_wco = _mm_tn("conv_g_out", cs, "full", dy_b, "col")
    dq, dk, dv, dbias = _attn_bwd(proj, bias, att, lse, datt)
    g["rel_table"] = _bias_fold(dbias)[:, :rel_table.shape[2]]
    dcin, g_dw, g["conv_dw_b"], g["conv_ln_g"], g["conv_ln_b"], g["conv_glu_bias"] = _conv_bwd(
        proj, c_glu, z_conv, dcs, conv_glu_bias, dw_full, conv_ln_g, conv_ln_b)
    pieces = [("q", dq), ("k", dk), ("v", dv), ("conv", dcin), ("gates", dgates)]
    n_in = win.shape[0] * win.shape[2]
    win_cols = jnp.transpose(jnp.transpose(win, (1, 0, 2)).reshape(d, n_in // COL, COL), (1, 0, 2))
    g_cols = jnp.concatenate([_mm_tn_wide("mix_g_in_" + tag, u, piece, COL) for tag, piece in pieces], axis=0)
    g_win = jnp.transpose(jnp.transpose(g_cols, (1, 0, 2)).reshape(d, win.shape[0], win.shape[2]), (1, 0, 2))
    bounds = np.cumsum([0] + [piece.shape[1] // COL for _, piece in pieces])
    group = ["w_out", "w_attn_out", "conv_w_out", "w_in"]
    (dh1, df1, g["mix_norm_pre"], g["ffn1_norm_post"]), carried = _mm_kblk(
        "mix_d_in", [(piece, win_cols[lo:hi]) for (_, piece), lo, hi in zip(pieces, bounds[:-1], bounds[1:])],
        trans_w=True, epilogue=_ep_pre_bwd_post(0.5), rows=[h1, dh2, f1],
        vecs=[mix_norm_pre, ffn1_norm_post], row_outs=[(d, F32), (d, BF16)], vec_outs=[d, d],
        carry=pair_sums("mix", group, [g_wout, g_wao, g_wco, g_win]))
    keep(group, carried)
    ((grad_x, g["ffn1_norm_pre"]), carried), group = ffn_bwd(
        "ffn1", df1, n1, a1, b1, s1, wg1, wu1, wd1, epilogue=_ep_pre_bwd_first(), rows=[xs, dh1],
        vecs=[ffn1_norm_pre], row_outs=[(d, F32)], vec_outs=[d])
    keep(group, carried)

    halves = [_add_chips("chip_sum_" + n, pos, parts[n], landed[n]) for n in big]
    other_halves = _swap_halves(halves)

    g["conv_dw_w"] = g_dw[:CONV_WIDTH]
    _, small_sum = _all_devices("sum_small", _pack([g[n] for n in small]))
    for n, val in zip(small, _unpack(small_sum, [g[n].shape for n in small])):
        g[n] = val
    g["conv_dw_w"] = lax.dynamic_slice_in_dim(g["conv_dw_w"], chip * cshard, cshard, axis=1)

    grads, deltas, new_m, new_v = {}, {}, {}, {}
    for n, mine, other in zip(big, halves, other_halves):
        gr, dl, m2, v2 = _adamw_halves("adamw_" + n, pos, shard(n), shard("m_" + n), shard("v_" + n), mine, other)
        grads[n], deltas[n], new_m[n], new_v[n] = unshard(n, gr), unshard(n, dl), unshard(n, m2), unshard(n, v2)
    shapes = [g[n].shape for n in small]
    packed = lambda pre: _pack([args[pre + n].reshape(shp) for n, shp in zip(small, shapes)])
    dl, m2, v2 = _adamw("adamw_small", packed(""), _pack([g[n] for n in small]), packed("m_"), packed("v_"))
    for n, a_, b_, c_ in zip(small, _unpack(dl, shapes), _unpack(m2, shapes), _unpack(v2, shapes)):
        shape = args[n].shape
        grads[n], deltas[n], new_m[n], new_v[n] = (g[n].reshape(shape), a_.reshape(shape), b_.reshape(shape),
                                                   c_.reshape(shape))

    return (loss, grad_x[None], *[grads[n] for n in names], *[deltas[n] for n in names],
            *[new_m[n] for n in names], *[new_v[n] for n in names])
```

```python
import functools

import numpy as np
import jax
import jax.numpy as jnp
from jax import lax
from jax.experimental import pallas as pl
from jax.experimental.pallas import tpu as pltpu

F32 = jnp.float32
BF16 = jnp.bfloat16
MESH = pl.DeviceIdType.MESH
ANY = pl.BlockSpec(memory_space=pl.ANY)

EPS = 1e-6
CHUNK = 64
LEFT_CHUNKS = 8
N_HEADS = 8
HEAD_DIM = 64
D_ATTN = N_HEADS * HEAD_DIM
D_CONV = 512
CONV_WIDTH = 31
REL_CLIP = 128
N_CHIPS = 4
N_DEV = 8
Q_BLOCK = 4 * CHUNK
K_PAD = LEFT_CHUNKS * CHUNK
K_WIN = K_PAD + Q_BLOCK
REL_EXT = 1024
REL_PAD = 384
CONV_HALO = 32
CONV_TILE = 512
COL = 512
NEG = -1e30

ADAM_LR = 0.001
ADAM_B1 = 0.9
ADAM_B2 = 0.999
ADAM_EPS = 1e-08
ADAM_WD = 0.01
ADAM_STEP = 10

VMEM_LIMIT_BYTES = 56 * 1024 * 1024


def _cparams(n_grid):
    return pltpu.CompilerParams(dimension_semantics=("arbitrary",) * n_grid, vmem_limit_bytes=VMEM_LIMIT_BYTES)


def _row_tile(rows, want):
    if rows <= want:
        return rows
    for t in range(want - want % 16, 0, -16):
        if rows % t == 0:
            return t
    raise ValueError((rows, want))


def _dot(a, w, trans_w):
    dims = (((1,), (1,)), ((), ())) if trans_w else (((1,), (0,)), ((), ()))
    return lax.dot_general(a, w, dims, preferred_element_type=F32)


class _Carry:
    LATE_STEPS = 2

    def __init__(self, ins, outs, aliases, sems, phases):
        self.ins, self.outs, self.aliases, self.sems, self.phases = ins, outs, aliases, sems, phases


def _call(name, core, grid, in_specs, out_specs, out_shape, scratch, args, carry=None):
    n_in, n_out, n_scr = len(in_specs), len(out_specs), len(scratch)
    if carry is None:
        out = pl.pallas_call(core, name=name, grid=grid, in_specs=in_specs, out_specs=out_specs, out_shape=out_shape,
                             scratch_shapes=scratch, compiler_params=_cparams(len(grid)))(*args)
        return list(out), []
    c_in, c_out = len(carry.ins), len(carry.outs)
    total = int(np.prod(grid))
    late = max(total - 1 - _Carry.LATE_STEPS, 0)

    def body(*refs):
        ins, refs = refs[:n_in], refs[n_in:]
        c_ins, refs = refs[:c_in], refs[c_in:]
        outs, refs = refs[:n_out], refs[n_out:]
        c_outs, refs = refs[:c_out], refs[c_out:]
        scr, c_sems = refs[:n_scr], refs[n_scr:]
        step = pl.program_id(0)
        for axis in range(1, len(grid)):
            step = step * grid[axis] + pl.program_id(axis)

        def run(when, at):
            for w, fn in carry.phases:
                if w == when:
                    pl.when(step == at)(functools.partial(fn, c_ins, c_outs, c_sems))

        run("first", 0)
        core(*ins, *outs, *scr)
        run("late", late)
        run("last", total - 1)

    out = pl.pallas_call(
        body, name=name, grid=grid, in_specs=list(in_specs) + [ANY] * c_in, out_specs=list(out_specs) + [ANY] * c_out,
        out_shape=list(out_shape) + list(carry.outs), scratch_shapes=list(scratch) + list(carry.sems),
        input_output_aliases={n_in + a: n_out + b for a, b in carry.aliases.items()},
        compiler_params=_cparams(len(grid)),
    )(*args, *carry.ins)
    return list(out[:n_out]), list(out[n_out:])


def _mm_nblk(name, a, w, *, trans_w, out_blocked, out_dtype, tm=1024, carry=None):
    m, k = a.shape
    nj = w.shape[0]
    nb = w.shape[1] if trans_w else w.shape[2]
    tm = _row_tile(m, tm)

    def core(a_ref, w_ref, o_ref):
        o_ref[...] = _dot(a_ref[...], w_ref[...], trans_w).astype(o_ref.dtype)

    if out_blocked:
        out_shape, out_spec = (nj, m, nb), pl.BlockSpec((None, tm, nb), lambda j, i: (j, i, 0))
    else:
        out_shape, out_spec = (m, nj * nb), pl.BlockSpec((tm, nb), lambda j, i: (i, j))
    out, carried = _call(
        name, core, (nj, m // tm),
        [pl.BlockSpec((tm, k), lambda j, i: (i, 0)), pl.BlockSpec((None,) + w.shape[1:], lambda j, i: (j, 0, 0))],
        [out_spec], [jax.ShapeDtypeStruct(out_shape, out_dtype)], [], [a, w], carry)
    return out[0] if carry is None else (out[0], carried)


def _mm_kblk(name, pairs, *, trans_w, out_dtype=F32, tm=512, sub=256, epilogue=None, rows=(), vecs=(), row_outs=None,
             vec_outs=(), carry=None):
    w0 = pairs[0][1]
    n = w0.shape[1] if trans_w else w0.shape[2]
    blocks = [(w.shape[0], w.shape[2] if trans_w else w.shape[1]) for _, w in pairs]
    m = pairs[0][0].shape[-2]
    tm = _row_tile(m, tm)
    ts = _row_tile(tm, sub)
    n_pairs, n_rows, n_vecs = len(pairs), len(rows), len(vecs)
    if epilogue is None:
        epilogue, row_outs = (lambda acc, r, v: ([acc], [])), [(n, out_dtype)]
    n_ro, n_vo = len(row_outs), len(vec_outs)

    def core(*refs):
        pair_refs, refs = refs[:2 * n_pairs], refs[2 * n_pairs:]
        row_refs, refs = refs[:n_rows], refs[n_rows:]
        vec_refs, refs = refs[:n_vecs], refs[n_vecs:]
        ro_refs, vo_refs = refs[:n_ro], refs[n_ro:]
        if n_vo:
            @pl.when(pl.program_id(0) == 0)
            def _():
                for ref in vo_refs:
                    ref[...] = jnp.zeros_like(ref)

        vec_vals = [v[...] for v in vec_refs]
        sums = None
        for r0 in range(0, tm, ts):
            sub_rows = slice(r0, r0 + ts)
            acc = None
            for p in range(n_pairs):
                a_ref, w_ref = pair_refs[2 * p], pair_refs[2 * p + 1]
                nj, kb = blocks[p]
                for j in range(nj):
                    a_blk = a_ref[j, sub_rows, :] if len(a_ref.shape) == 3 else a_ref[sub_rows, j * kb:(j + 1) * kb]
                    part = _dot(a_blk, w_ref[j], trans_w)
                    acc = part if acc is None else acc + part
            ro, vo = epilogue(acc, [r[sub_rows, :] for r in row_refs], vec_vals)
            for ref, val in zip(ro_refs, ro):
                ref[sub_rows, :] = val.astype(ref.dtype)
            sums = vo if sums is None else [s + v for s, v in zip(sums, vo)]
        for ref, val in zip(vo_refs, sums or []):
            ref[...] += val

    in_specs, args = [], []
    for (a, w), (nj, kb) in zip(pairs, blocks):
        if a.ndim == 3:
            in_specs.append(pl.BlockSpec((nj, tm, kb), lambda i: (0, i, 0)))
        else:
            in_specs.append(pl.BlockSpec((tm, nj * kb), lambda i: (i, 0)))
        in_specs.append(pl.BlockSpec(w.shape, lambda i: (0, 0, 0), pipeline_mode=pl.Buffered(1)))
        args += [a, w]
    in_specs += [pl.BlockSpec((tm, r.shape[1]), lambda i: (i, 0)) for r in rows]
    in_specs += [pl.BlockSpec(v.shape, lambda i: (0, 0)) for v in vecs]
    out_specs = [pl.BlockSpec((tm, cols), lambda i: (i, 0)) for cols, _ in row_outs]
    out_specs += [pl.BlockSpec((1, cols), lambda i: (0, 0)) for cols in vec_outs]
    out_shape = [jax.ShapeDtypeStruct((m, cols), dt) for cols, dt in row_outs]
    out_shape += [jax.ShapeDtypeStruct((1, cols), F32) for cols in vec_outs]
    return _call(name, core, (m // tm,), in_specs, out_specs, out_shape, [], args + list(rows) + list(vecs), carry)


def _mm_tn(name, a, a_mode, b, b_mode, *, out_dtype=BF16, tt=2048):
    nj = N_CHIPS
    t = a.shape[-2]
    tt = _row_tile(t, tt)

    def spec(x, mode):
        if mode == "full":
            return x.shape[1], pl.BlockSpec((tt, x.shape[1]), lambda j, s: (s, 0))
        if mode == "col":
            cb = x.shape[1] // nj
            return cb, pl.BlockSpec((tt, cb), lambda j, s: (s, j))
        return x.shape[2], pl.BlockSpec((None, tt, x.shape[2]), lambda j, s: (j, s, 0))

    ca, a_spec = spec(a, a_mode)
    cb, b_spec = spec(b, b_mode)
    n_steps = t // tt

    def body(a_ref, b_ref, o_ref, acc_ref):
        s = pl.program_id(1)

        @pl.when(s == 0)
        def _():
            acc_ref[...] = jnp.zeros_like(acc_ref)

        acc_ref[...] += lax.dot_general(a_ref[...], b_ref[...], (((0,), (0,)), ((), ())),
                                        preferred_element_type=F32)

        @pl.when(s == n_steps - 1)
        def _():
            o_ref[...] = acc_ref[...].astype(o_ref.dtype)

    return pl.pallas_call(
        body, name=name, grid=(nj, n_steps), in_specs=[a_spec, b_spec],
        out_specs=pl.BlockSpec((None, ca, cb), lambda j, s: (j, 0, 0)),
        out_shape=jax.ShapeDtypeStruct((nj, ca, cb), out_dtype),
        scratch_shapes=[pltpu.VMEM((ca, cb), F32)], compiler_params=_cparams(2),
    )(a, b)


def _mm_tn_wide(name, a, b, cb, *, a_split=1, out_dtype=BF16, tt=1024):
    t, ca = a.shape
    nb = b.shape[1] // cb
    tt = _row_tile(t, tt)
    n_steps = t // tt
    piece = ca // a_split

    def body(a_ref, b_ref, o_ref, acc_ref):
        s = pl.program_id(0)

        @pl.when(s == 0)
        def _():
            acc_ref[...] = jnp.zeros_like(acc_ref)

        for j in range(nb):
            bv = b_ref[:, j * cb:(j + 1) * cb]
            for c in range(a_split):
                rows = slice(c * piece, (c + 1) * piece)
                acc_ref[j, rows, :] += lax.dot_general(a_ref[:, rows], bv, (((0,), (0,)), ((), ())),
                                                       preferred_element_type=F32)

        @pl.when(s == n_steps - 1)
        def _():
            o_ref[...] = acc_ref[...].astype(o_ref.dtype)

    return pl.pallas_call(
        body, name=name, grid=(n_steps,),
        in_specs=[pl.BlockSpec((tt, ca), lambda s: (s, 0)), pl.BlockSpec((tt, nb * cb), lambda s: (s, 0))],
        out_specs=pl.BlockSpec((nb, ca, cb), lambda s: (0, 0, 0)),
        out_shape=jax.ShapeDtypeStruct((nb, ca, cb), out_dtype),
        scratch_shapes=[pltpu.VMEM((nb, ca, cb), F32)], compiler_params=_cparams(1),
    )(a, b)


def _rowwise(name, fn, rows, vecs, row_outs, vec_outs, *, tm=256, carry=None):
    nrows = rows[0][0].shape[0]
    tm = _row_tile(nrows, tm)
    n_r, n_v, n_ro, n_vo = len(rows), len(vecs), len(row_outs), len(vec_outs)

    def body(*refs):
        r_vals = [r[...] for r in refs[:n_r]]
        v_vals = [r[...] for r in refs[n_r:n_r + n_v]]
        ro_refs = refs[n_r + n_v:n_r + n_v + n_ro]
        vo_refs = refs[n_r + n_v + n_ro:]
        ro, vo = fn(r_vals, v_vals)
        for ref, val in zip(ro_refs, ro):
            ref[...] = val.astype(ref.dtype)
        if n_vo:
            @pl.when(pl.program_id(0) == 0)
            def _():
                for ref in vo_refs:
                    ref[...] = jnp.zeros_like(ref)

            for ref, val in zip(vo_refs, vo):
                ref[...] += val

    in_specs = [pl.BlockSpec((tm, cols), functools.partial(lambda i, cb: (i, cb), cb=cb)) for _, cols, cb in rows]
    in_specs += [pl.BlockSpec(v.shape, functools.partial(lambda i, nd: (0,) * nd, nd=v.ndim)) for v in vecs]
    out_specs = [pl.BlockSpec((tm, cols), lambda i: (i, 0)) for cols, _ in row_outs]
    out_specs += [pl.BlockSpec((1, cols), lambda i: (0, 0)) for cols in vec_outs]
    out_shape = [jax.ShapeDtypeStruct((nrows, cols), dt) for cols, dt in row_outs]
    out_shape += [jax.ShapeDtypeStruct((1, cols), F32) for cols in vec_outs]
    out, carried = _call(name, body, (nrows // tm,), in_specs, out_specs, out_shape, [],
                         [r[0] for r in rows] + list(vecs), carry)
    return out if carry is None else (out, carried)


def _whole(x):
    return (x, x.shape[1], 0)


def _colsum(x):
    return jnp.sum(x, axis=0, keepdims=True)


def _rstd(x):
    return lax.rsqrt(jnp.mean(x * x, axis=-1, keepdims=True) + EPS)


def _rms_bwd(dn, x, g):
    r = _rstd(x)
    c = dn * g
    dx = r * c - x * (r * r * r) * jnp.mean(c * x, axis=-1, keepdims=True)
    return dx, _colsum(dn * x * r)


def _rms_fwd(name, x, g, carry):
    def fn(r, v):
        (xv,), (gv,) = r, v
        return [xv * _rstd(xv) * gv], []

    (n,), carried = _rowwise(name, fn, [_whole(x)], [g], [(x.shape[1], BF16)], [], carry=carry)
    return n, carried


def _ep_post_res_pre(scale):
    def epilogue(acc, rows, vecs):
        (resid,), (g_post, g_next) = rows, vecs
        h = resid + scale * (acc * _rstd(acc) * g_post)
        return [acc, h, h * _rstd(h) * g_next], []

    return epilogue


def _post_bwd(dh, f, g_post, scale):
    return _rms_bwd(scale * dh, f, g_post)


def _ep_loss(scale, d):
    def epilogue(acc, rows, vecs):
        (resid, target), (g_post,) = rows, vecs
        err = resid + scale * (acc * _rstd(acc) * g_post) - target
        dy = err * (1.0 / d)
        df, dg_post = _post_bwd(dy, acc, g_post, scale)
        return [dy, df], [_colsum(err * err), dg_post]

    return epilogue


def _ep_pre_bwd_post(scale_prev):
    def epilogue(acc, rows, vecs):
        (h, dh_up, f_prev), (g_pre, g_post_prev) = rows, vecs
        dx, dg_pre = _rms_bwd(acc, h, g_pre)
        dh = dh_up + dx
        df, dg_post = _post_bwd(dh, f_prev, g_post_prev, scale_prev)
        return [dh, df], [dg_pre, dg_post]

    return epilogue


def _ep_pre_bwd_first():
    def epilogue(acc, rows, vecs):
        (x, dh_up), (g_pre,) = rows, vecs
        dx, dg_pre = _rms_bwd(acc, x, g_pre)
        return [dh_up + dx], [dg_pre]

    return epilogue


def _gate_specs(d, tm):
    first = (3 * D_ATTN + 2 * D_CONV) // COL
    return [pl.BlockSpec((tm, COL), functools.partial(lambda i, cb: (i, cb), cb=first + p)) for p in range(2 * d // COL)]


def _gate(piece_refs, bias_ref, c0, width):
    p, off = divmod(c0, COL)
    return jax.nn.sigmoid(piece_refs[p][:, off:off + width].astype(F32) + bias_ref[:, c0:c0 + width])


def _resident(w):
    return pl.BlockSpec(w.shape, functools.partial(lambda i, nd: (0,) * nd, nd=w.ndim), pipeline_mode=pl.Buffered(1))


def _mix_merge(att, cs, wao, wco, proj, gate_bias, tm=1024):
    t = att.shape[0]
    nj, _, nb = wao.shape
    d = nj * nb
    tm = _row_tile(t, tm)
    gate_specs = _gate_specs(d, tm)
    n_p = len(gate_specs)

    def body(att_ref, cs_ref, wao_ref, wco_ref, *rest):
        pieces, (gb_ref, ya_ref, yb_ref, m_ref) = rest[:n_p], rest[n_p:]
        av, cv = att_ref[...], cs_ref[...]
        for j in range(nj):
            cols = slice(j * nb, (j + 1) * nb)
            ya = _dot(av, wao_ref[j], False)
            yb = _dot(cv, wco_ref[j], False)
            merged = _gate(pieces, gb_ref, j * nb, nb) * ya + _gate(pieces, gb_ref, d + j * nb, nb) * yb
            ya_ref[:, cols] = ya.astype(ya_ref.dtype)
            yb_ref[:, cols] = yb.astype(yb_ref.dtype)
            m_ref[:, cols] = merged.astype(m_ref.dtype)

    row = lambda x: pl.BlockSpec((tm, x.shape[1]), lambda i: (i, 0))
    out_spec = pl.BlockSpec((tm, d), lambda i: (i, 0))
    return pl.pallas_call(
        body, name="mix_merge", grid=(t // tm,),
        in_specs=[row(att), row(cs), _resident(wao), _resident(wco)] + gate_specs + [_resident(gate_bias)],
        out_specs=[out_spec] * 3, out_shape=[jax.ShapeDtypeStruct((t, d), BF16)] * 3, compiler_params=_cparams(1),
    )(att, cs, wao, wco, *([proj] * n_p), gate_bias)


def _mix_d_merge(dmo, wout, y_a, y_b, wao, wco, proj, gate_bias, tm=512):
    t, d = dmo.shape
    nj, _, nb = wao.shape
    ka, kc = wao.shape[1], wco.shape[1]
    tm = _row_tile(t, tm)
    gate_specs = _gate_specs(d, tm)
    n_p = len(gate_specs)

    def body(dmo_ref, wout_ref, ya_ref, yb_ref, wao_ref, wco_ref, *rest):
        pieces, (gb_ref, dya_ref, dyb_ref, dg_ref, datt_ref, dcs_ref, dgb_ref) = rest[:n_p], rest[n_p:]

        @pl.when(pl.program_id(0) == 0)
        def _():
            dgb_ref[...] = jnp.zeros_like(dgb_ref)

        dmo_v = dmo_ref[...]
        datt = dcs = None
        for j in range(nj):
            cols, cols_b = slice(j * nb, (j + 1) * nb), slice(d + j * nb, d + (j + 1) * nb)
            dm = _dot(dmo_v, wout_ref[j], True)
            ga, gb = _gate(pieces, gb_ref, j * nb, nb), _gate(pieces, gb_ref, d + j * nb, nb)
            dya, dyb = (dm * ga).astype(BF16), (dm * gb).astype(BF16)
            dga = dm * ya_ref[:, cols].astype(F32) * ga * (1.0 - ga)
            dgb = dm * yb_ref[:, cols].astype(F32) * gb * (1.0 - gb)
            dya_ref[:, cols], dyb_ref[:, cols] = dya, dyb
            dg_ref[:, cols], dg_ref[:, cols_b] = dga.astype(dg_ref.dtype), dgb.astype(dg_ref.dtype)
            dgb_ref[:, cols] += _colsum(dga)
            dgb_ref[:, cols_b] += _colsum(dgb)
            pa, pc = _dot(dya, wao_ref[j], True), _dot(dyb, wco_ref[j], True)
            datt, dcs = (pa, pc) if datt is None else (datt + pa, dcs + pc)
        datt_ref[...] = datt.astype(datt_ref.dtype)
        dcs_ref[...] = dcs.astype(dcs_ref.dtype)

    row = lambda cols: pl.BlockSpec((tm, cols), lambda i: (i, 0))
    return pl.pallas_call(
        body, name="mix_d_merge", grid=(t // tm,),
        in_specs=[row(d), _resident(wout), row(d), row(d), _resident(wao), _resident(wco)] + gate_specs
        + [_resident(gate_bias)],
        out_specs=[row(d), row(d), row(2 * d), row(ka), row(kc), pl.BlockSpec((1, 2 * d), lambda i: (0, 0))],
        out_shape=[jax.ShapeDtypeStruct((t, d), BF16), jax.ShapeDtypeStruct((t, d), BF16),
                   jax.ShapeDtypeStruct((t, 2 * d), BF16), jax.ShapeDtypeStruct((t, ka), BF16),
                   jax.ShapeDtypeStruct((t, kc), F32), jax.ShapeDtypeStruct((1, 2 * d), F32)],
        compiler_params=_cparams(1),
    )(dmo, wout, y_a, y_b, wao, wco, *([proj] * n_p), gate_bias)


def _adamw_math(wv, gv, mv, vv):
    m2 = ADAM_B1 * mv + (1.0 - ADAM_B1) * gv
    v2 = ADAM_B2 * vv + (1.0 - ADAM_B2) * (gv * gv)
    m_hat = m2 / (1.0 - ADAM_B1 ** ADAM_STEP)
    v_hat = v2 / (1.0 - ADAM_B2 ** ADAM_STEP)
    delta = -ADAM_LR * (m_hat / (jnp.sqrt(v_hat) + ADAM_EPS) + ADAM_WD * wv)
    return delta, m2, v2


def _adamw(name, w, g, m, v):
    def fn(r, _):
        return list(_adamw_math(*r)), []

    c = w.shape[1]
    return _rowwise(name, fn, [_whole(w), _whole(g), _whole(m), _whole(v)], [], [(c, F32)] * 3, [], tm=256)


POS_C, POS_CHIP, POS_PEER = 0, 1, 2


def _placed_call(body, name, pos, grid, in_specs, out_specs, out_shape, args):
    return pl.pallas_call(
        body, name=name, out_shape=out_shape, compiler_params=_cparams(len(grid)),
        grid_spec=pltpu.PrefetchScalarGridSpec(num_scalar_prefetch=1, grid=grid, in_specs=in_specs,
                                               out_specs=out_specs),
    )(pos, *args)


def _cast_into(name, pos, w):
    r, cols = w.shape
    tm = _row_tile(r, 1024)

    def body(pos_ref, w_ref, o_ref):
        o_ref[...] = w_ref[...].astype(o_ref.dtype)

    return _placed_call(
        body, name, pos, (r // tm,), [pl.BlockSpec((tm, cols), lambda i, pos: (i, 0))],
        pl.BlockSpec((None, tm, cols), lambda i, pos: (pos[POS_CHIP], i, 0)),
        jax.ShapeDtypeStruct((N_CHIPS, r, cols), BF16), [w])


def _add_pair(name, pos, grad, landed):
    nj, half, cols = landed.shape
    tm = _row_tile(half, 512)
    nb = half // tm

    def body(pos_ref, g_ref, l_ref, o_ref):
        o_ref[...] = (g_ref[...].astype(F32) + l_ref[...].astype(F32)).astype(o_ref.dtype)

    spec = pl.BlockSpec((None, tm, cols), lambda j, i, pos: (j, i, 0))
    return _placed_call(
        body, name, pos, (nj, nb),
        [pl.BlockSpec((None, tm, cols), lambda j, i, pos: (j, pos[POS_C] * nb + i, 0)), spec], spec,
        jax.ShapeDtypeStruct(landed.shape, BF16), [grad, landed])


def _add_chips(name, pos, part, landed):
    _, half, cols = landed.shape
    tm = _row_tile(half, 512)

    def body(pos_ref, p_ref, l0_ref, l1_ref, l2_ref, o_ref):
        acc = p_ref[...].astype(F32)
        for ref in (l0_ref, l1_ref, l2_ref):
            acc = acc + ref[...].astype(F32)
        o_ref[...] = acc

    slot = lambda at: pl.BlockSpec((None, tm, cols), functools.partial(lambda i, pos, at: (pos[at], i, 0), at=at))
    return _placed_call(
        body, name, pos, (half // tm,), [slot(POS_CHIP)] + [slot(POS_PEER + k) for k in range(3)],
        pl.BlockSpec((tm, cols), lambda i, pos: (i, 0)), jax.ShapeDtypeStruct((half, cols), F32),
        [part, landed, landed, landed])


def _adamw_halves(name, pos, w, m, v, own, landed):
    r, cols = w.shape
    half = own.shape[0]
    tm = _row_tile(half, 384 * 1024 // cols)
    nb = half // tm

    def body(pos_ref, w_ref, m_ref, v_ref, own_ref, land_ref, g_out, d_out, m_out, v_out):
        mine = pl.program_id(0) == pos_ref[POS_C]
        g = jnp.where(mine, own_ref[...], land_ref[...])
        delta, m2, v2 = _adamw_math(w_ref[...], g, m_ref[...], v_ref[...])
        g_out[...] = g
        d_out[...] = delta
        m_out[...] = m2
        v_out[...] = v2

    full = pl.BlockSpec((tm, cols), lambda h, i, pos: (h * nb + i, 0))
    used = lambda h, i, pos: (jnp.where(h == pos[POS_C], i, 0), 0)
    unused = lambda h, i, pos: (jnp.where(h == pos[POS_C], 0, i), 0)
    return _placed_call(
        body, name, pos, (2, nb), [full, full, full, pl.BlockSpec((tm, cols), used), pl.BlockSpec((tm, cols), unused)],
        [full] * 4,
        [jax.ShapeDtypeStruct((r, cols), F32)] * 4, [w, m, v, own, landed])


N_START = K_PAD // Q_BLOCK


def _rel_onehot(n_q):
    e = np.arange(REL_EXT)
    dist = K_PAD - (e - (n_q - 1))
    idx = np.clip(dist, -REL_CLIP, REL_CLIP) + REL_CLIP
    return (np.arange(REL_PAD)[:, None] == idx[None, :]).astype(np.float32)


def _skew(x, left):
    row = lax.broadcasted_iota(jnp.int32, x.shape, 0)
    for bit in range(x.shape[0].bit_length() - 1):
        amount = 1 << bit
        rolled = pltpu.roll(x, REL_EXT - amount if left else amount, 1)
        x = jnp.where((row >> bit) & 1 == 1, rolled, x)
    return x


def _bias_expand(table_pad, carry):
    onehot = jnp.asarray(_rel_onehot(Q_BLOCK))

    def core(t_ref, oh_ref, o_ref):
        ext = jnp.dot(t_ref[...], oh_ref[...], precision=lax.Precision.HIGHEST, preferred_element_type=F32)
        qc = lax.broadcasted_iota(jnp.int32, (Q_BLOCK, K_WIN), 0) // CHUNK
        kpos = lax.broadcasted_iota(jnp.int32, (Q_BLOCK, K_WIN), 1)
        band = (kpos // CHUNK >= qc) & (kpos // CHUNK <= qc + LEFT_CHUNKS)
        rows = jnp.broadcast_to(ext, (Q_BLOCK, REL_EXT))
        rolled = _skew(pltpu.roll(rows, REL_EXT - (Q_BLOCK - 1), 1), left=False)[:, :K_WIN]
        for v in range(N_START + 1):
            o_ref[v] = jnp.where(band & (kpos + v * Q_BLOCK >= K_PAD), rolled, NEG)

    (bias,), carried = _call(
        "bias_expand", core, (N_HEADS,),
        [pl.BlockSpec((None, 1, REL_PAD), lambda h: (h, 0, 0)), pl.BlockSpec(onehot.shape, lambda h: (0, 0))],
        [pl.BlockSpec((N_START + 1, None, Q_BLOCK, K_WIN), lambda h: (0, h, 0, 0))],
        [jax.ShapeDtypeStruct((N_START + 1, N_HEADS, Q_BLOCK, K_WIN), F32)], [],
        [table_pad.reshape(N_HEADS, 1, REL_PAD), onehot], carry)
    return bias, carried


def _bias_fold(dbias):
    onehot_t = jnp.asarray(_rel_onehot(CHUNK).T)

    def body(d_ref, oh_ref, o_ref, ext_ref):
        for h in range(N_HEADS):
            x = jnp.concatenate([d_ref[h], jnp.zeros((CHUNK, REL_EXT - K_WIN), F32)], axis=1)
            rolled = _skew(pltpu.roll(x, CHUNK - 1, 1), left=True)
            ext_ref[h:h + 1, :] = jnp.sum(rolled, axis=0, keepdims=True)
        o_ref[...] = jnp.dot(ext_ref[...], oh_ref[...], precision=lax.Precision.HIGHEST,
                             preferred_element_type=F32)

    return pl.pallas_call(
        body, name="bias_fold", out_shape=jax.ShapeDtypeStruct((N_HEADS, REL_PAD), F32),
        scratch_shapes=[pltpu.VMEM((N_HEADS, REL_EXT), F32)],
        compiler_params=pltpu.CompilerParams(vmem_limit_bytes=VMEM_LIMIT_BYTES),
    )(dbias, onehot_t)


def _head_lanes():
    lane = lax.broadcasted_iota(jnp.int32, (1, 2 * HEAD_DIM), 1)
    return [lane < HEAD_DIM, lane >= HEAD_DIM]


def _only(mask, x, scale=None):
    x = jnp.where(mask, x, jnp.zeros_like(x))
    return x if scale is None else x * scale


def _contract_lanes(a, b):
    return lax.dot_general(a, b, (((1,), (1,)), ((), ())), preferred_element_type=F32)


def _contract_rows(a, b):
    return lax.dot_general(a, b, (((0,), (0,)), ((), ())), preferred_element_type=F32)


PAIR = 2 * HEAD_DIM
N_PAIRS = D_ATTN // PAIR


def _attn_specs(pairs):
    width = pairs * PAIR
    per = D_ATTN // width
    row_spec = pl.BlockSpec((Q_BLOCK, width), lambda g, i: (i, g))
    kv_specs = [pl.BlockSpec((Q_BLOCK, width),
                             functools.partial(lambda g, i, kk, c0: (jnp.maximum(i + kk - N_START, 0), c0 + g),
                                               kk=kk, c0=c0))
                for c0 in (per, 2 * per) for kk in range(K_WIN // Q_BLOCK)]
    bias_spec = pl.BlockSpec((None, 2 * pairs, Q_BLOCK, K_WIN), lambda g, i: (jnp.minimum(i, N_START), g, 0, 0))
    return row_spec, kv_specs, bias_spec


def _attn_fwd(proj, bias, pairs=N_PAIRS):
    t = proj.shape[0]
    n_win = K_WIN // Q_BLOCK

    def body(q_ref, *refs):
        k_refs, v_refs = refs[:n_win], refs[n_win:2 * n_win]
        b_ref, o_ref, lse_ref = refs[2 * n_win:]
        for pp in range(pairs):
            cols = slice(pp * PAIR, (pp + 1) * PAIR)
            k = jnp.concatenate([r[:, cols] for r in k_refs], axis=0)
            v = jnp.concatenate([r[:, cols] for r in v_refs], axis=0)
            q = q_ref[:, cols]
            o = lse = None
            for hh, lanes in enumerate(_head_lanes()):
                s = _contract_lanes(_only(lanes, q, HEAD_DIM ** -0.5), k) + b_ref[2 * pp + hh]
                m = jnp.max(s, axis=1, keepdims=True)
                p = jnp.exp(s - m)
                l = jnp.sum(p, axis=1, keepdims=True)
                oh = jnp.dot(p.astype(BF16), v, preferred_element_type=F32) / l
                lse_h = jnp.broadcast_to(m + jnp.log(l), oh.shape)
                o, lse = (oh, lse_h) if o is None else (jnp.where(lanes, oh, o), jnp.where(lanes, lse_h, lse))
            o_ref[:, cols] = o.astype(o_ref.dtype)
            lse_ref[:, cols] = lse

    row_spec, kv_specs, bias_spec = _attn_specs(pairs)
    return pl.pallas_call(
        body, name="attn_fwd", grid=(N_PAIRS // pairs, t // Q_BLOCK),
        in_specs=[row_spec] + kv_specs + [bias_spec], out_specs=[row_spec, row_spec],
        out_shape=[jax.ShapeDtypeStruct((t, D_ATTN), BF16), jax.ShapeDtypeStruct((t, D_ATTN), F32)],
        compiler_params=_cparams(2),
    )(*([proj] * (1 + 2 * n_win)), bias)


def _attn_bwd(proj, bias, att, lse, datt, pairs=2):
    t = proj.shape[0]
    n_win = K_WIN // Q_BLOCK
    n_blocks = t // Q_BLOCK

    def body(q_ref, *refs):
        k_refs, v_refs = refs[:n_win], refs[n_win:2 * n_win]
        b_ref, o_ref, lse_ref, do_ref, dq_ref, dk_ref, dv_ref, db_ref, dk_acc, dv_acc = refs[2 * n_win:]
        i = pl.program_id(1)

        @pl.when(i == 0)
        def _():
            dk_acc[...] = jnp.zeros_like(dk_acc)
            dv_acc[...] = jnp.zeros_like(dv_acc)
            db_ref[...] = jnp.zeros_like(db_ref)

        rows = pl.ds(pl.multiple_of(i * Q_BLOCK, Q_BLOCK), K_WIN)
        scale = HEAD_DIM ** -0.5
        for pp in range(pairs):
            cols = slice(pp * PAIR, (pp + 1) * PAIR)
            k = jnp.concatenate([r[:, cols] for r in k_refs], axis=0)
            v = jnp.concatenate([r[:, cols] for r in v_refs], axis=0)
            q, do, o = q_ref[:, cols], do_ref[:, cols], o_ref[:, cols].astype(F32)
            dq = dk = dv = None
            for hh, lanes in enumerate(_head_lanes()):
                qh, doh = _only(lanes, q, scale), _only(lanes, do)
                s = _contract_lanes(qh, k) + b_ref[2 * pp + hh]
                lse_col = pp * PAIR + hh * HEAD_DIM
                p = jnp.exp(s - lse_ref[:, lse_col:lse_col + 1])
                delta = jnp.sum(doh.astype(F32) * o, axis=1, keepdims=True)
                ds = p * (_contract_lanes(doh, v) - delta)
                folded = ds[:CHUNK]
                for c in range(1, Q_BLOCK // CHUNK):
                    folded = folded + pltpu.roll(ds[c * CHUNK:(c + 1) * CHUNK], K_WIN - c * CHUNK, 1)
                db_ref[2 * pp + hh] += folded
                dsb = ds.astype(BF16)
                dqh = jnp.dot(dsb, k, preferred_element_type=F32)
                dq = dqh if dq is None else jnp.where(lanes, dqh, dq)
                dkh, dvh = _contract_rows(dsb, qh), _contract_rows(p.astype(BF16), doh)
                dk, dv = (dkh, dvh) if dk is None else (dk + dkh, dv + dvh)
            dq_ref[:, cols] = (dq * scale).astype(dq_ref.dtype)
            dk_acc[rows, cols] += dk
            dv_acc[rows, cols] += dv

        @pl.when(i == n_blocks - 1)
        def _():
            dk_ref[...] = dk_acc[K_PAD:, :].astype(dk_ref.dtype)
            dv_ref[...] = dv_acc[K_PAD:, :].astype(dv_ref.dtype)

    width = pairs * PAIR
    row_spec, kv_specs, bias_spec = _attn_specs(pairs)
    full_spec = pl.BlockSpec((t, width), lambda g, i: (0, g))
    return pl.pallas_call(
        body, name="attn_bwd", grid=(N_PAIRS // pairs, n_blocks),
        in_specs=[row_spec] + kv_specs + [bias_spec, row_spec, row_spec, row_spec],
        out_specs=[row_spec, full_spec, full_spec,
                   pl.BlockSpec((2 * pairs, CHUNK, K_WIN), lambda g, i: (g, 0, 0))],
        out_shape=[jax.ShapeDtypeStruct((t, D_ATTN), BF16)] * 3 + [jax.ShapeDtypeStruct((N_HEADS, CHUNK, K_WIN), F32)],
        scratch_shapes=[pltpu.VMEM((t + K_PAD, width), F32)] * 2, compiler_params=_cparams(2),
    )(*([proj] * (1 + 2 * n_win)), bias, att, lse, datt)


CONV_LEAD = CONV_HALO - (CONV_WIDTH - 1)
CONV_LANES = 128
CONV_ROWS = 64


def _conv_specs(t):
    tt = _row_tile(t, CONV_TILE)
    per = tt // CONV_HALO
    n_halo = t // CONV_HALO
    tile = lambda cb: pl.BlockSpec((tt, COL), functools.partial(lambda i, cb: (i, cb), cb=cb))
    prev = lambda cb: pl.BlockSpec((CONV_HALO, COL),
                                   functools.partial(lambda i, cb: (jnp.maximum(i * per - 1, 0), cb), cb=cb))
    nxt = lambda cb: pl.BlockSpec((CONV_HALO, COL),
                                  functools.partial(lambda i, cb: (jnp.minimum((i + 1) * per, n_halo - 1), cb), cb=cb))
    vec = lambda shape: pl.BlockSpec(shape, lambda i: (0, 0))
    return tt, tile, prev, nxt, vec


def _glu(ca, cg, bias):
    return (ca.astype(F32) + bias[:, :D_CONV]) * jax.nn.sigmoid(cg.astype(F32) + bias[:, D_CONV:])


SUBLANES = 8


def _shift_copies(ext_ref):
    n = ext_ref.shape[1] - SUBLANES
    for s in range(1, SUBLANES):
        ext_ref[s, 0:n, :] = ext_ref[0, s:s + n, :]


def _tap_tiles(ext_ref, first_row, r0, lanes):
    n_g = CONV_ROWS // SUBLANES
    for s in range(SUBLANES):
        taps = [w for w in range(CONV_WIDTH) if first_row(w) % SUBLANES == s]
        if not taps:
            continue
        lo = min(first_row(w) for w in taps) - s
        n_tiles = (max(first_row(w) for w in taps) - s - lo) // SUBLANES + n_g
        tiles = [ext_ref[s, r0 + lo + SUBLANES * b:r0 + lo + SUBLANES * (b + 1), lanes] for b in range(n_tiles)]
        for w in taps:
            k = (first_row(w) - s - lo) // SUBLANES
            yield w, tiles[k:k + n_g]


def _taps(ext_ref, tt, first_row, w_ref, out_ref):
    n_g = CONV_ROWS // SUBLANES
    for l0 in range(0, D_CONV, CONV_LANES):
        lanes = slice(l0, l0 + CONV_LANES)
        for r0 in range(0, tt, CONV_ROWS):
            acc = [jnp.zeros((SUBLANES, CONV_LANES), F32)] * n_g
            for w, tiles in _tap_tiles(ext_ref, first_row, r0, lanes):
                weight = jnp.broadcast_to(w_ref[w:w + 1, lanes], (SUBLANES, CONV_LANES))
                acc = [a + t * weight for a, t in zip(acc, tiles)]
            for g in range(n_g):
                out_ref[r0 + SUBLANES * g:r0 + SUBLANES * (g + 1), lanes] = acc[g]


def _tap_sums(ext_ref, tt, first_row, x_ref, out_ref):
    n_g = CONV_ROWS // SUBLANES
    for l0 in range(0, D_CONV, CONV_LANES):
        lanes = slice(l0, l0 + CONV_LANES)
        acc = [jnp.zeros((SUBLANES, CONV_LANES), F32)] * CONV_WIDTH
        for r0 in range(0, tt, CONV_ROWS):
            x = [x_ref[0, r0 + SUBLANES * g:r0 + SUBLANES * (g + 1), lanes] for g in range(n_g)]
            for w, tiles in _tap_tiles(ext_ref, first_row, r0, lanes):
                part = tiles[0] * x[0]
                for g in range(1, n_g):
                    part = part + tiles[g] * x[g]
                acc[w] = acc[w] + part
        for w in range(CONV_WIDTH):
            out_ref[w:w + 1, lanes] += jnp.sum(acc[w], axis=0, keepdims=True)


def _conv_fwd(proj, glu_bias, dw, dw_b, ln_g, ln_b):
    t = proj.shape[0]
    tt, tile, prev, nxt, vec = _conv_specs(t)
    ca_blk, cg_blk = 3 * D_ATTN // COL, 3 * D_ATTN // COL + 1

    def body(ca_ref, cg_ref, pa_ref, pg_ref, gb_ref, dw_ref, dwb_ref, g_ref, b_ref, cs_ref, c_ref, z_ref, ext_ref):
        i = pl.program_id(0)
        bias = gb_ref[...]
        c = _glu(ca_ref[...], cg_ref[...], bias)
        halo = _glu(pa_ref[...], pg_ref[...], bias)
        ext_ref[0, 0:CONV_HALO, :] = jnp.where(i == 0, 0.0, halo)
        ext_ref[0, CONV_HALO:, :] = c
        _shift_copies(ext_ref)
        c_ref[...] = c
        _taps(ext_ref, tt, lambda w: CONV_LEAD + w, dw_ref, z_ref)
        z = z_ref[...] + dwb_ref[...]
        z_ref[...] = z
        mu = jnp.mean(z, axis=-1, keepdims=True)
        zc = z - mu
        y = zc * lax.rsqrt(jnp.mean(zc * zc, axis=-1, keepdims=True) + EPS) * g_ref[...] + b_ref[...]
        cs_ref[...] = (y * jax.nn.sigmoid(y)).astype(cs_ref.dtype)

    out_spec = pl.BlockSpec((tt, D_CONV), lambda i: (i, 0))
    return pl.pallas_call(
        body, name="conv_fwd", grid=(t // tt,),
        in_specs=[tile(ca_blk), tile(cg_blk), prev(ca_blk), prev(cg_blk), vec(glu_bias.shape), vec(dw.shape),
                  vec(dw_b.shape), vec(ln_g.shape), vec(ln_b.shape)],
        out_specs=[out_spec] * 3,
        out_shape=[jax.ShapeDtypeStruct((t, D_CONV), BF16), jax.ShapeDtypeStruct((t, D_CONV), F32),
                   jax.ShapeDtypeStruct((t, D_CONV), F32)],
        scratch_shapes=[pltpu.VMEM((SUBLANES, tt + CONV_HALO, D_CONV), F32)], compiler_params=_cparams(1),
    )(proj, proj, proj, proj, glu_bias, dw, dw_b, ln_g, ln_b)


def _conv_bwd(proj, c, z, dcs, glu_bias, dw, ln_g, ln_b):
    t = proj.shape[0]
    tt, tile, prev, nxt, vec = _conv_specs(t)
    n_tiles = t // tt
    ca_blk, cg_blk = 3 * D_ATTN // COL, 3 * D_ATTN // COL + 1

    def ln_bwd(zv, dcsv, g, b):
        mu = jnp.mean(zv, axis=-1, keepdims=True)
        zc = zv - mu
        rstd = lax.rsqrt(jnp.mean(zc * zc, axis=-1, keepdims=True) + EPS)
        zhat = zc * rstd
        y = zhat * g + b
        sig = jax.nn.sigmoid(y)
        dy = dcsv * sig * (1.0 + y * (1.0 - sig))
        dzh = dy * g
        dz = rstd * (dzh - jnp.mean(dzh, axis=-1, keepdims=True) - zhat * jnp.mean(dzh * zhat, axis=-1, keepdims=True))
        return dz, dy, zhat

    def body(ca_ref, cg_ref, c_ref, cprev_ref, z_ref, znext_ref, dcs_ref, dcsnext_ref, gb_ref, dw_ref, g_ref, b_ref,
             dcin_ref, ddw_ref, ddwb_ref, dg_ref, db_ref, dgb_ref, cext_ref, dzext_ref, dc_ref):
        i = pl.program_id(0)

        @pl.when(i == 0)
        def _():
            for ref in (ddw_ref, ddwb_ref, dg_ref, db_ref, dgb_ref):
                ref[...] = jnp.zeros_like(ref)

        g, b = g_ref[...], b_ref[...]
        dz, dy, zhat = ln_bwd(z_ref[...], dcs_ref[...], g, b)
        dz_next, _, _ = ln_bwd(znext_ref[...], dcsnext_ref[...], g, b)
        dg_ref[...] += _colsum(dy * zhat)
        db_ref[...] += _colsum(dy)
        ddwb_ref[...] += _colsum(dz)
        dzext_ref[0, 0:tt, :] = dz
        dzext_ref[0, tt:, :] = jnp.where(i == n_tiles - 1, 0.0, dz_next)
        _shift_copies(dzext_ref)
        cext_ref[0, 0:CONV_HALO, :] = jnp.where(i == 0, 0.0, cprev_ref[...])
        cext_ref[0, CONV_HALO:, :] = c_ref[...]
        _shift_copies(cext_ref)
        _taps(dzext_ref, tt, lambda w: CONV_WIDTH - 1 - w, dw_ref, dc_ref)
        _tap_sums(cext_ref, tt, lambda w: CONV_LEAD + w, dzext_ref, ddw_ref)
        bias = gb_ref[...]
        a_in = ca_ref[...].astype(F32) + bias[:, :D_CONV]
        sg = jax.nn.sigmoid(cg_ref[...].astype(F32) + bias[:, D_CONV:])
        dc = dc_ref[...]
        dcin = jnp.concatenate([dc * sg, dc * a_in * sg * (1.0 - sg)], axis=1)
        dcin_ref[...] = dcin.astype(dcin_ref.dtype)
        dgb_ref[...] += _colsum(dcin)

    row = lambda: pl.BlockSpec((tt, D_CONV), lambda i: (i, 0))
    per = tt // CONV_HALO
    n_halo = t // CONV_HALO
    prev_row = pl.BlockSpec((CONV_HALO, D_CONV), lambda i: (jnp.maximum(i * per - 1, 0), 0))
    next_row = lambda: pl.BlockSpec((CONV_HALO, D_CONV), lambda i: (jnp.minimum((i + 1) * per, n_halo - 1), 0))
    acc = lambda shape: pl.BlockSpec(shape, lambda i: (0, 0))
    return pl.pallas_call(
        body, name="conv_bwd", grid=(n_tiles,),
        in_specs=[tile(ca_blk), tile(cg_blk), row(), prev_row, row(), next_row(), row(), next_row(),
                  vec(glu_bias.shape), vec(dw.shape), vec(ln_g.shape), vec(ln_b.shape)],
        out_specs=[pl.BlockSpec((tt, 2 * D_CONV), lambda i: (i, 0)), acc(dw.shape), acc((1, D_CONV)),
                   acc((1, D_CONV)), acc((1, D_CONV)), acc((1, 2 * D_CONV))],
        out_shape=[jax.ShapeDtypeStruct((t, 2 * D_CONV), BF16), jax.ShapeDtypeStruct(dw.shape, F32),
                   jax.ShapeDtypeStruct((1, D_CONV), F32), jax.ShapeDtypeStruct((1, D_CONV), F32),
                   jax.ShapeDtypeStruct((1, D_CONV), F32), jax.ShapeDtypeStruct((1, 2 * D_CONV), F32)],
        scratch_shapes=[pltpu.VMEM((SUBLANES, tt + CONV_HALO, D_CONV), F32),
                        pltpu.VMEM((SUBLANES, tt + CONV_HALO, D_CONV), F32), pltpu.VMEM((tt, D_CONV), F32)],
        compiler_params=_cparams(1),
    )(proj, proj, c, c, z, z, dcs, dcs, glu_bias, dw, ln_g, ln_b)


def _place():
    x, y, c = lax.axis_index("x"), lax.axis_index("y"), lax.axis_index("c")
    chips = [(1 - x, y), (x, 1 - y), (1 - x, 1 - y)]
    return x, y, c, chips


def _chip_index(chip):
    return 2 * chip[0] + chip[1]


def _half_rows(c, half):
    return pl.ds(pl.multiple_of(c * half, 16), half)


def _gather_carry(blocked):
    n = len(blocked)

    def over_ici(o_refs, send_sems, recv_sems):
        x, y, c, chips = _place()
        me = _chip_index((x, y))
        copies = []
        for a in range(n):
            mine = o_refs[a].at[me, _half_rows(c, o_refs[a].shape[1] // 2), :]
            for k, chip in enumerate(chips):
                copies.append(pltpu.make_async_remote_copy(
                    src_ref=mine, dst_ref=mine, send_sem=send_sems.at[6 * a + k], recv_sem=recv_sems.at[6 * a + k],
                    device_id=(chip[0], chip[1], c), device_id_type=MESH))
        return copies

    def to_sibling(o_refs, send_sems, recv_sems, sent_by_me):
        x, y, c, chips = _place()
        copies = []
        for a in range(n):
            rows = _half_rows(c if sent_by_me else 1 - c, o_refs[a].shape[1] // 2)
            for k, chip in enumerate(chips):
                landed = o_refs[a].at[_chip_index(chip), rows, :]
                copies.append(pltpu.make_async_remote_copy(
                    src_ref=landed, dst_ref=landed, send_sem=send_sems.at[6 * a + 3 + k],
                    recv_sem=recv_sems.at[6 * a + 3 + k], device_id=(x, y, 1 - c), device_id_type=MESH))
        return copies

    def start(ins, outs, sems):
        for cp in over_ici(outs, *sems):
            cp.start()

    def hand_on(ins, outs, sems):
        for arrived, onward in zip(over_ici(outs, *sems), to_sibling(outs, *sems, True)):
            arrived.wait_recv()
            onward.start()

    def finish(ins, outs, sems):
        for cp in to_sibling(outs, *sems, False):
            cp.wait_recv()
        for cp in over_ici(outs, *sems) + to_sibling(outs, *sems, True):
            cp.wait_send()

    return _Carry(
        ins=list(blocked), outs=[jax.ShapeDtypeStruct(w.shape, w.dtype) for w in blocked],
        aliases={a: a for a in range(n)},
        sems=[pltpu.SemaphoreType.DMA((6 * n,)), pltpu.SemaphoreType.DMA((6 * n,))],
        phases=[("first", start), ("late", hand_on), ("last", finish)])


def _pair_exchange(name, grads):
    n = len(grads)

    def body(*refs):
        g_refs, land_refs = refs[:n], refs[n:2 * n]
        send_sems, recv_sems = refs[2 * n:]
        x, y, c, _ = _place()
        copies = []
        for a in range(n):
            half = g_refs[a].shape[1] // 2
            cp = pltpu.make_async_remote_copy(
                src_ref=g_refs[a].at[:, _half_rows(1 - c, half), :], dst_ref=land_refs[a],
                send_sem=send_sems.at[a], recv_sem=recv_sems.at[a], device_id=(x, y, 1 - c), device_id_type=MESH)
            cp.start()
            copies.append(cp)
        for cp in copies:
            cp.wait()

    return pl.pallas_call(
        body, name=name, in_specs=[ANY] * n, out_specs=[ANY] * n,
        out_shape=[jax.ShapeDtypeStruct((g.shape[0], g.shape[1] // 2, g.shape[2]), g.dtype) for g in grads],
        scratch_shapes=[pltpu.SemaphoreType.DMA((n,)), pltpu.SemaphoreType.DMA((n,))],
    )(*grads)


def _to_owner_carry(parts):
    n = len(parts)

    def sends(p_refs, l_refs, send_sems, recv_sems):
        x, y, c, chips = _place()
        me = _chip_index((x, y))
        return [pltpu.make_async_remote_copy(
            src_ref=p_refs[a].at[_chip_index(chip)], dst_ref=l_refs[a].at[me],
            send_sem=send_sems.at[3 * a + k], recv_sem=recv_sems.at[3 * a + k],
            device_id=(chip[0], chip[1], c), device_id_type=MESH) for a in range(n) for k, chip in enumerate(chips)]

    def start(ins, outs, sems):
        for cp in sends(ins, outs, *sems):
            cp.start()

    def finish(ins, outs, sems):
        x, y, c, chips = _place()
        send_sems, recv_sems = sems
        for a in range(n):
            for k, chip in enumerate(chips):
                slot = outs[a].at[_chip_index(chip)]
                pltpu.make_async_remote_copy(
                    src_ref=slot, dst_ref=slot, send_sem=send_sems.at[3 * a + k], recv_sem=recv_sems.at[3 * a + k],
                    device_id=(chip[0], chip[1], c), device_id_type=MESH).wait_recv()
        for cp in sends(ins, outs, *sems):
            cp.wait_send()

    return _Carry(
        ins=list(parts), outs=[jax.ShapeDtypeStruct(p.shape, p.dtype) for p in parts], aliases={},
        sems=[pltpu.SemaphoreType.DMA((3 * n,)), pltpu.SemaphoreType.DMA((3 * n,))],
        phases=[("first", start), ("last", finish)])


def _swap_halves(halves):
    n = len(halves)

    def body(*refs):
        h_refs, o_refs = refs[:n], refs[n:2 * n]
        send_sems, recv_sems = refs[2 * n:]
        x, y, c, _ = _place()
        copies = []
        for a in range(n):
            cp = pltpu.make_async_remote_copy(
                src_ref=h_refs[a], dst_ref=o_refs[a], send_sem=send_sems.at[a], recv_sem=recv_sems.at[a],
                device_id=(x, y, 1 - c), device_id_type=MESH)
            cp.start()
            copies.append(cp)
        for cp in copies:
            cp.wait()

    return pl.pallas_call(
        body, name="grad_swap_halves", in_specs=[ANY] * n, out_specs=[ANY] * n,
        out_shape=[jax.ShapeDtypeStruct(h.shape, h.dtype) for h in halves],
        scratch_shapes=[pltpu.SemaphoreType.DMA((n,)), pltpu.SemaphoreType.DMA((n,))],
    )(*halves)


def _all_devices(name, block):
    r, cols = block.shape

    def body(b_ref, all_ref, sum_ref, send_sems, recv_sems):
        x, y, c, _ = _place()
        me = 4 * x + 2 * y + c
        all_ref[me] = b_ref[...]
        flips = [(fx, fy, fc) for fx in (0, 1) for fy in (0, 1) for fc in (0, 1)][1:]
        copies = []
        for k, (fx, fy, fc) in enumerate(flips):
            cp = pltpu.make_async_remote_copy(
                src_ref=b_ref, dst_ref=all_ref.at[me], send_sem=send_sems.at[k], recv_sem=recv_sems.at[k],
                device_id=(x ^ fx, y ^ fy, c ^ fc), device_id_type=MESH)
            cp.start()
            copies.append(cp)
        for k, (fx, fy, fc) in enumerate(flips):
            slot = all_ref.at[4 * (x ^ fx) + 2 * (y ^ fy) + (c ^ fc)]
            pltpu.make_async_remote_copy(
                src_ref=slot, dst_ref=slot, send_sem=send_sems.at[k], recv_sem=recv_sems.at[k],
                device_id=(x ^ fx, y ^ fy, c ^ fc), device_id_type=MESH).wait_recv()
        for cp in copies:
            cp.wait_send()
        acc = all_ref[0]
        for d in range(1, N_DEV):
            acc = acc + all_ref[d]
        sum_ref[...] = acc

    vmem = pl.BlockSpec(memory_space=pltpu.VMEM)
    return pl.pallas_call(
        body, name=name, in_specs=[vmem], out_specs=[vmem, vmem],
        out_shape=[jax.ShapeDtypeStruct((N_DEV, r, cols), F32), jax.ShapeDtypeStruct((r, cols), F32)],
        scratch_shapes=[pltpu.SemaphoreType.DMA((N_DEV - 1,)), pltpu.SemaphoreType.DMA((N_DEV - 1,))],
    )(block)


PACK = 1024


def _packed_rows(shape, width):
    size, last = int(np.prod(shape)), shape[-1]
    cols = last if last <= width else width
    assert size % cols == 0
    return size // cols, cols


def _pack(vals, width=PACK):
    rows = []
    for v in vals:
        n_rows, cols = _packed_rows(v.shape, width)
        rows.append(jnp.pad(v.reshape(n_rows, cols).astype(F32), ((0, 0), (0, width - cols))))
    buf = jnp.concatenate(rows, axis=0)
    return jnp.pad(buf, ((0, (-buf.shape[0]) % 8), (0, 0)))


def _unpack(buf, shapes, width=PACK):
    out, r = [], 0
    for shape in shapes:
        n_rows, cols = _packed_rows(shape, width)
        out.append(buf[r:r + n_rows, :cols].reshape(shape))
        r += n_rows
    return out


FFN_SPLIT = 2


def _ffn_hidden(name, n, wg, wu, tm=1024, carry=None):
    m, k = n.shape
    f = wg.shape[0]
    fb = f // FFN_SPLIT
    tm = _row_tile(m, tm)

    def core(n_ref, wg_ref, wu_ref, a_ref, b_ref, s_ref):
        nv = n_ref[...]
        a = _dot(nv, wg_ref[...], True)
        b = _dot(nv, wu_ref[...], True)
        a_ref[...] = a.astype(a_ref.dtype)
        b_ref[...] = b.astype(b_ref.dtype)
        s_ref[...] = (a * jax.nn.sigmoid(a) * b).astype(s_ref.dtype)

    w_spec = pl.BlockSpec((fb, k), lambda j, i: (j, 0))
    out_spec = pl.BlockSpec((tm, fb), lambda j, i: (i, j))
    return _call(name, core, (FFN_SPLIT, m // tm), [pl.BlockSpec((tm, k), lambda j, i: (i, 0)), w_spec, w_spec],
                 [out_spec] * 3, [jax.ShapeDtypeStruct((m, f), BF16)] * 3, [], [n, wg, wu], carry)


def _ffn_d_hidden(name, df, wd, a, b, tm=512):
    m, k = df.shape
    f = wd.shape[0]
    fb = f // FFN_SPLIT
    tm = _row_tile(m, tm)

    def body(df_ref, wd_ref, a_ref, b_ref, da_ref, db_ref):
        dfv = df_ref[...]
        for j in range(FFN_SPLIT):
            cols = slice(j * fb, (j + 1) * fb)
            ds = _dot(dfv, wd_ref[cols, :], True)
            av, bv = a_ref[:, cols].astype(F32), b_ref[:, cols].astype(F32)
            sig = jax.nn.sigmoid(av)
            da_ref[:, cols] = (ds * bv * sig * (1.0 + av * (1.0 - sig))).astype(da_ref.dtype)
            db_ref[:, cols] = (ds * av * sig).astype(db_ref.dtype)

    row = pl.BlockSpec((tm, f), lambda i: (i, 0))
    return pl.pallas_call(
        body, name=name, grid=(m // tm,),
        in_specs=[pl.BlockSpec((tm, k), lambda i: (i, 0)), _resident(wd), row, row],
        out_specs=[row, row], out_shape=[jax.ShapeDtypeStruct((m, f), BF16)] * 2,
        compiler_params=_cparams(1),
    )(df, wd, a, b)


def kernel(x, ffn1_norm_pre, ffn1_w_gate, ffn1_w_up, ffn1_w_down, ffn1_norm_post, mix_norm_pre, w_in, gate_bias, rel_table, w_attn_out, conv_glu_bias, conv_dw_w, conv_dw_b, conv_ln_g, conv_ln_b, conv_w_out, w_out, mix_norm_post, ffn2_norm_pre, ffn2_w_gate, ffn2_w_up, ffn2_w_down, ffn2_norm_post, loss_target, m_ffn1_norm_pre, m_ffn1_w_gate, m_ffn1_w_up, m_ffn1_w_down, m_ffn1_norm_post, m_mix_norm_pre, m_w_in, m_gate_bias, m_rel_table, m_w_attn_out, m_conv_glu_bias, m_conv_dw_w, m_conv_dw_b, m_conv_ln_g, m_conv_ln_b, m_conv_w_out, m_w_out, m_mix_norm_post, m_ffn2_norm_pre, m_ffn2_w_gate, m_ffn2_w_up, m_ffn2_w_down, m_ffn2_norm_post, v_ffn1_norm_pre, v_ffn1_w_gate, v_ffn1_w_up, v_ffn1_w_down, v_ffn1_norm_post, v_mix_norm_pre, v_w_in, v_gate_bias, v_rel_table, v_w_attn_out, v_conv_glu_bias, v_conv_dw_w, v_conv_dw_b, v_conv_ln_g, v_conv_ln_b, v_conv_w_out, v_w_out, v_mix_norm_post, v_ffn2_norm_pre, v_ffn2_w_gate, v_ffn2_w_up, v_ffn2_w_down, v_ffn2_norm_post):
    args = dict(locals())
    names = ['ffn1_norm_pre', 'ffn1_w_gate', 'ffn1_w_up', 'ffn1_w_down', 'ffn1_norm_post', 'mix_norm_pre', 'w_in',
             'gate_bias', 'rel_table', 'w_attn_out', 'conv_glu_bias', 'conv_dw_w', 'conv_dw_b', 'conv_ln_g',
             'conv_ln_b', 'conv_w_out', 'w_out', 'mix_norm_post', 'ffn2_norm_pre', 'ffn2_w_gate', 'ffn2_w_up',
             'ffn2_w_down', 'ffn2_norm_post']
    big = ['ffn1_w_gate', 'ffn1_w_up', 'ffn1_w_down', 'w_in', 'w_attn_out', 'conv_w_out', 'w_out', 'ffn2_w_gate',
           'ffn2_w_up', 'ffn2_w_down']
    small = [n for n in names if n not in big]

    xs, target = x[0], loss_target[0]
    t, d = xs.shape
    cx, cy = lax.axis_index("x"), lax.axis_index("y")
    chip = 2 * cx + cy

    dw_shard = conv_dw_w[0, :, 0, :]
    cshard = dw_shard.shape[1]
    dw_all, _ = _all_devices("gather_dw", _pack([dw_shard], width=cshard))
    dw_full = jnp.concatenate([dw_all[2 * j, :CONV_WIDTH, :cshard] for j in range(N_CHIPS)], axis=1)
    dw_full = jnp.pad(dw_full, ((0, CONV_HALO - CONV_WIDTH), (0, 0)))
    peers = [(1 - cx, cy), (cx, 1 - cy), (1 - cx, 1 - cy)]
    pos = jnp.stack([lax.axis_index("c"), chip] + [_chip_index(p) for p in peers]).astype(jnp.int32)
    transposed = ("ffn1_w_gate", "ffn1_w_up", "ffn2_w_gate", "ffn2_w_up")
    weight_of = lambda n: n[2:] if n[:2] in ("m_", "v_") else n
    shard = lambda n: jnp.transpose(args[n][0]) if weight_of(n) in transposed else args[n][0]
    unshard = lambda n, v: (jnp.transpose(v) if n in transposed else v)[None]
    own = {n: _cast_into("cast_" + n, pos, shard(n)) for n in big}
    gather = lambda *ns: _gather_carry([own[n] for n in ns])
    res_spec = [(d, F32), (d, F32), (d, BF16)]
    whole = lambda w: w.reshape(-1, w.shape[-1])

    table_pad = jnp.pad(rel_table[0], ((0, 0), (0, REL_PAD - rel_table.shape[2])))
    bias, (wg1,) = _bias_expand(table_pad, gather("ffn1_w_gate"))
    n1, (wu1,) = _rms_fwd("ffn1_pre", xs, ffn1_norm_pre, gather("ffn1_w_up"))
    (a1, b1, s1), (wd1, win, wao, wco, wout) = _ffn_hidden(
        "ffn1_hidden", n1, whole(wg1), whole(wu1),
        carry=gather("ffn1_w_down", "w_in", "w_attn_out", "conv_w_out", "w_out"))
    (f1, h1, u), (wg2,) = _mm_kblk(
        "ffn1_down", [(s1, whole(wd1)[None])], trans_w=False, epilogue=_ep_post_res_pre(0.5), rows=[xs],
        vecs=[ffn1_norm_post, mix_norm_pre], row_outs=res_spec, carry=gather("ffn2_w_gate"))
    proj, (wu2, wd2) = _mm_nblk("mix_in", u, win, trans_w=False, out_blocked=False, out_dtype=BF16, tm=2048,
                                carry=gather("ffn2_w_up", "ffn2_w_down"))
    att, lse = _attn_fwd(proj, bias)
    cs, c_glu, z_conv = _conv_fwd(proj, conv_glu_bias, dw_full, conv_dw_b, conv_ln_g, conv_ln_b)
    y_a, y_b, merged = _mix_merge(att, cs, wao, wco, proj, gate_bias)
    (mo, h2, n2), _ = _mm_kblk(
        "mix_out", [(merged, wout)], trans_w=False, epilogue=_ep_post_res_pre(1.0), rows=[h1],
        vecs=[mix_norm_post, ffn2_norm_pre], row_outs=res_spec, tm=1024)
    (a2, b2, s2), _ = _ffn_hidden("ffn2_hidden", n2, whole(wg2), whole(wu2))
    g = {}
    (dy, df2, err2, g["ffn2_norm_post"]), _ = _mm_kblk(
        "ffn2_down", [(s2, whole(wd2)[None])], trans_w=False, epilogue=_ep_loss(0.5, d), rows=[h2, target],
        vecs=[ffn2_norm_post], row_outs=[(d, F32), (d, BF16)], vec_outs=[d, d])
    loss = lax.psum(0.5 * jnp.sum(err2) / d, ("x", "y", "c"))

    parts, landed = {}, {}

    def ffn_bwd(tag, df, n, a, b, s, wg, wu, wd, **epilogue):
        da, db = _ffn_d_hidden(tag + "_d_hidden", df, whole(wd), a, b)
        group = [tag + "_w_down", tag + "_w_gate", tag + "_w_up"]
        local = [_mm_tn_wide(tag + "_g_" + what, hidden, other, d, a_split=FFN_SPLIT).reshape(wd.shape)
                 for what, hidden, other in (("down", s, df), ("gate", da, n), ("up", db, n))]
        return _mm_kblk(tag + "_d_n", [(da, whole(wg)[None]), (db, whole(wu)[None])], trans_w=False,
                        carry=pair_sums(tag, group, local), **epilogue), group

    def pair_sums(tag, group, local):
        theirs = _pair_exchange("pair_" + tag, local)
        for n, mine, other in zip(group, local, theirs):
            parts[n] = _add_pair("pair_sum_" + n, pos, mine, other)
        return _to_owner_carry([parts[n] for n in group])

    def keep(group, carried):
        for n, val in zip(group, carried):
            landed[n] = val

    ((dh2, dmo, g["ffn2_norm_pre"], g["mix_norm_post"]), carried), group = ffn_bwd(
        "ffn2", df2, n2, a2, b2, s2, wg2, wu2, wd2, epilogue=_ep_pre_bwd_post(1.0), rows=[h2, dy, mo],
        vecs=[ffn2_norm_pre, mix_norm_post], row_outs=[(d, F32), (d, BF16)], vec_outs=[d, d])
    keep(group, carried)
    g_wout = _mm_tn("mix_g_out", merged, "col", dmo, "full")
    dy_a, dy_b, dgates, datt, dcs, g["gate_bias"] = _mix_d_merge(dmo, wout, y_a, y_b, wao, wco, proj, gate_bias)
    g_wao = _mm_tn("attn_g_out", att, "full", dy_a, "col")
    g_wco = _mm_tn("conv_g_out", cs, "full", dy_b, "col")
    dq, dk, dv, dbias = _attn_bwd(proj, bias, att, lse, datt)
    g["rel_table"] = _bias_fold(dbias)[:, :rel_table.shape[2]]
    dcin, g_dw, g["conv_dw_b"], g["conv_ln_g"], g["conv_ln_b"], g["conv_glu_bias"] = _conv_bwd(
        proj, c_glu, z_conv, dcs, conv_glu_bias, dw_full, conv_ln_g, conv_ln_b)
    pieces = [("q", dq), ("k", dk), ("v", dv), ("conv", dcin), ("gates", dgates)]
    n_in = win.shape[0] * win.shape[2]
    win_cols = jnp.transpose(jnp.transpose(win, (1, 0, 2)).reshape(d, n_in // COL, COL), (1, 0, 2))
    g_cols = jnp.concatenate([_mm_tn_wide("mix_g_in_" + tag, u, piece, COL, tt=2048) for tag, piece in pieces],
                             axis=0)
    g_win = jnp.transpose(jnp.transpose(g_cols, (1, 0, 2)).reshape(d, win.shape[0], win.shape[2]), (1, 0, 2))
    bounds = np.cumsum([0] + [piece.shape[1] // COL for _, piece in pieces])
    group = ["w_out", "w_attn_out", "conv_w_out", "w_in"]
    (dh1, df1, g["mix_norm_pre"], g["ffn1_norm_post"]), carried = _mm_kblk(
        "mix_d_in", [(piece, win_cols[lo:hi]) for (_, piece), lo, hi in zip(pieces, bounds[:-1], bounds[1:])],
        trans_w=True, epilogue=_ep_pre_bwd_post(0.5), rows=[h1, dh2, f1],
        vecs=[mix_norm_pre, ffn1_norm_post], row_outs=[(d, F32), (d, BF16)], vec_outs=[d, d],
        carry=pair_sums("mix", group, [g_wout, g_wao, g_wco, g_win]))
    keep(group, carried)
    ((grad_x, g["ffn1_norm_pre"]), carried), group = ffn_bwd(
        "ffn1", df1, n1, a1, b1, s1, wg1, wu1, wd1, epilogue=_ep_pre_bwd_first(), rows=[xs, dh1],
        vecs=[ffn1_norm_pre], row_outs=[(d, F32)], vec_outs=[d])
    keep(group, carried)

    halves = [_add_chips("chip_sum_" + n, pos, parts[n], landed[n]) for n in big]
    other_halves = _swap_halves(halves)

    g["conv_dw_w"] = g_dw[:CONV_WIDTH]
    _, small_sum = _all_devices("sum_small", _pack([g[n] for n in small]))
    for n, val in zip(small, _unpack(small_sum, [g[n].shape for n in small])):
        g[n] = val
    g["conv_dw_w"] = lax.dynamic_slice_in_dim(g["conv_dw_w"], chip * cshard, cshard, axis=1)

    grads, deltas, new_m, new_v = {}, {}, {}, {}
    for n, mine, other in zip(big, halves, other_halves):
        gr, dl, m2, v2 = _adamw_halves("adamw_" + n, pos, shard(n), shard("m_" + n), shard("v_" + n), mine, other)
        grads[n], deltas[n], new_m[n], new_v[n] = unshard(n, gr), unshard(n, dl), unshard(n, m2), unshard(n, v2)
    shapes = [g[n].shape for n in small]
    packed = lambda pre: _pack([args[pre + n].reshape(shp) for n, shp in zip(small, shapes)])
    dl, m2, v2 = _adamw("adamw_small", packed(""), _pack([g[n] for n in small]), packed("m_"), packed("v_"))
    for n, a_, b_, c_ in zip(small, _unpack(dl, shapes), _unpack(m2, shapes), _unpack(v2, shapes)):
        shape = args[n].shape
        grads[n], deltas[n], new_m[n], new_v[n] = (g[n].reshape(shape), a_.reshape(shape), b_.reshape(shape),
                                                   c_.reshape(shape))

    return (loss, grad_x[None], *[grads[n] for n in names], *[deltas[n] for n in names],
            *[new_m[n] for n in names], *[new_v[n] for n in names])
```

```python
import functools

import numpy as np
import jax
import jax.numpy as jnp
from jax import lax
from jax.experimental import pallas as pl
from jax.experimental.pallas import tpu as pltpu

F32 = jnp.float32
BF16 = jnp.bfloat16
MESH = pl.DeviceIdType.MESH
ANY = pl.BlockSpec(memory_space=pl.ANY)

EPS = 1e-6
CHUNK = 64
LEFT_CHUNKS = 8
N_HEADS = 8
HEAD_DIM = 64
D_ATTN = N_HEADS * HEAD_DIM
D_CONV = 512
CONV_WIDTH = 31
REL_CLIP = 128
N_CHIPS = 4
N_DEV = 8
Q_BLOCK = 4 * CHUNK
K_PAD = LEFT_CHUNKS * CHUNK
K_WIN = K_PAD + Q_BLOCK
REL_EXT = 1024
REL_PAD = 384
CONV_HALO = 32
CONV_TILE = 512
COL = 512
NEG = -1e30

ADAM_LR = 0.001
ADAM_B1 = 0.9
ADAM_B2 = 0.999
ADAM_EPS = 1e-08
ADAM_WD = 0.01
ADAM_STEP = 10

VMEM_LIMIT_BYTES = 56 * 1024 * 1024


def _cparams(n_grid):
    return pltpu.CompilerParams(dimension_semantics=("arbitrary",) * n_grid, vmem_limit_bytes=VMEM_LIMIT_BYTES)


def _row_tile(rows, want):
    if rows <= want:
        return rows
    for t in range(want - want % 16, 0, -16):
        if rows % t == 0:
            return t
    raise ValueError((rows, want))


def _dot(a, w, trans_w):
    dims = (((1,), (1,)), ((), ())) if trans_w else (((1,), (0,)), ((), ()))
    return lax.dot_general(a, w, dims, preferred_element_type=F32)


class _Carry:
    LATE_STEPS = 2

    def __init__(self, ins, outs, aliases, sems, phases):
        self.ins, self.outs, self.aliases, self.sems, self.phases = ins, outs, aliases, sems, phases


def _call(name, core, grid, in_specs, out_specs, out_shape, scratch, args, carry=None):
    n_in, n_out, n_scr = len(in_specs), len(out_specs), len(scratch)
    if carry is None:
        out = pl.pallas_call(core, name=name, grid=grid, in_specs=in_specs, out_specs=out_specs, out_shape=out_shape,
                             scratch_shapes=scratch, compiler_params=_cparams(len(grid)))(*args)
        return list(out), []
    c_in, c_out = len(carry.ins), len(carry.outs)
    total = int(np.prod(grid))
    late = max(total - 1 - _Carry.LATE_STEPS, 0)

    def body(*refs):
        ins, refs = refs[:n_in], refs[n_in:]
        c_ins, refs = refs[:c_in], refs[c_in:]
        outs, refs = refs[:n_out], refs[n_out:]
        c_outs, refs = refs[:c_out], refs[c_out:]
        scr, c_sems = refs[:n_scr], refs[n_scr:]
        step = pl.program_id(0)
        for axis in range(1, len(grid)):
            step = step * grid[axis] + pl.program_id(axis)

        def run(when, at):
            for w, fn in carry.phases:
                if w == when:
                    pl.when(step == at)(functools.partial(fn, c_ins, c_outs, c_sems))

        run("first", 0)
        core(*ins, *outs, *scr)
        run("late", late)
        run("last", total - 1)

    out = pl.pallas_call(
        body, name=name, grid=grid, in_specs=list(in_specs) + [ANY] * c_in, out_specs=list(out_specs) + [ANY] * c_out,
        out_shape=list(out_shape) + list(carry.outs), scratch_shapes=list(scratch) + list(carry.sems),
        input_output_aliases={n_in + a: n_out + b for a, b in carry.aliases.items()},
        compiler_params=_cparams(len(grid)),
    )(*args, *carry.ins)
    return list(out[:n_out]), list(out[n_out:])


def _mm_nblk(name, a, w, *, trans_w, out_blocked, out_dtype, tm=1024, carry=None):
    m, k = a.shape
    nj = w.shape[0]
    nb = w.shape[1] if trans_w else w.shape[2]
    tm = _row_tile(m, tm)

    def core(a_ref, w_ref, o_ref):
        o_ref[...] = _dot(a_ref[...], w_ref[...], trans_w).astype(o_ref.dtype)

    if out_blocked:
        out_shape, out_spec = (nj, m, nb), pl.BlockSpec((None, tm, nb), lambda j, i: (j, i, 0))
    else:
        out_shape, out_spec = (m, nj * nb), pl.BlockSpec((tm, nb), lambda j, i: (i, j))
    out, carried = _call(
        name, core, (nj, m // tm),
        [pl.BlockSpec((tm, k), lambda j, i: (i, 0)), pl.BlockSpec((None,) + w.shape[1:], lambda j, i: (j, 0, 0))],
        [out_spec], [jax.ShapeDtypeStruct(out_shape, out_dtype)], [], [a, w], carry)
    return out[0] if carry is None else (out[0], carried)


def _mm_kblk(name, pairs, *, trans_w, out_dtype=F32, tm=512, sub=256, epilogue=None, rows=(), vecs=(), row_outs=None,
             vec_outs=(), carry=None):
    w0 = pairs[0][1]
    n = w0.shape[1] if trans_w else w0.shape[2]
    blocks = [(w.shape[0], w.shape[2] if trans_w else w.shape[1]) for _, w in pairs]
    m = pairs[0][0].shape[-2]
    tm = _row_tile(m, tm)
    ts = _row_tile(tm, sub)
    n_pairs, n_rows, n_vecs = len(pairs), len(rows), len(vecs)
    if epilogue is None:
        epilogue, row_outs = (lambda acc, r, v: ([acc], [])), [(n, out_dtype)]
    n_ro, n_vo = len(row_outs), len(vec_outs)

    def core(*refs):
        pair_refs, refs = refs[:2 * n_pairs], refs[2 * n_pairs:]
        row_refs, refs = refs[:n_rows], refs[n_rows:]
        vec_refs, refs = refs[:n_vecs], refs[n_vecs:]
        ro_refs, vo_refs = refs[:n_ro], refs[n_ro:]
        if n_vo:
            @pl.when(pl.program_id(0) == 0)
            def _():
                for ref in vo_refs:
                    ref[...] = jnp.zeros_like(ref)

        vec_vals = [v[...] for v in vec_refs]
        sums = None
        for r0 in range(0, tm, ts):
            sub_rows = slice(r0, r0 + ts)
            acc = None
            for p in range(n_pairs):
                a_ref, w_ref = pair_refs[2 * p], pair_refs[2 * p + 1]
                nj, kb = blocks[p]
                for j in range(nj):
                    a_blk = a_ref[j, sub_rows, :] if len(a_ref.shape) == 3 else a_ref[sub_rows, j * kb:(j + 1) * kb]
                    part = _dot(a_blk, w_ref[j], trans_w)
                    acc = part if acc is None else acc + part
            ro, vo = epilogue(acc, [r[sub_rows, :] for r in row_refs], vec_vals)
            for ref, val in zip(ro_refs, ro):
                ref[sub_rows, :] = val.astype(ref.dtype)
            sums = vo if sums is None else [s + v for s, v in zip(sums, vo)]
        for ref, val in zip(vo_refs, sums or []):
            ref[...] += val

    in_specs, args = [], []
    for (a, w), (nj, kb) in zip(pairs, blocks):
        if a.ndim == 3:
            in_specs.append(pl.BlockSpec((nj, tm, kb), lambda i: (0, i, 0)))
        else:
            in_specs.append(pl.BlockSpec((tm, nj * kb), lambda i: (i, 0)))
        in_specs.append(pl.BlockSpec(w.shape, lambda i: (0, 0, 0), pipeline_mode=pl.Buffered(1)))
        args += [a, w]
    in_specs += [pl.BlockSpec((tm, r.shape[1]), lambda i: (i, 0)) for r in rows]
    in_specs += [pl.BlockSpec(v.shape, lambda i: (0, 0)) for v in vecs]
    out_specs = [pl.BlockSpec((tm, cols), lambda i: (i, 0)) for cols, _ in row_outs]
    out_specs += [pl.BlockSpec((1, cols), lambda i: (0, 0)) for cols in vec_outs]
    out_shape = [jax.ShapeDtypeStruct((m, cols), dt) for cols, dt in row_outs]
    out_shape += [jax.ShapeDtypeStruct((1, cols), F32) for cols in vec_outs]
    return _call(name, core, (m // tm,), in_specs, out_specs, out_shape, [], args + list(rows) + list(vecs), carry)


def _mm_tn(name, a, a_mode, b, b_mode, *, out_dtype=BF16, tt=2048):
    nj = N_CHIPS
    t = a.shape[-2]
    tt = _row_tile(t, tt)

    def spec(x, mode):
        if mode == "full":
            return x.shape[1], pl.BlockSpec((tt, x.shape[1]), lambda j, s: (s, 0))
        if mode == "col":
            cb = x.shape[1] // nj
            return cb, pl.BlockSpec((tt, cb), lambda j, s: (s, j))
        return x.shape[2], pl.BlockSpec((None, tt, x.shape[2]), lambda j, s: (j, s, 0))

    ca, a_spec = spec(a, a_mode)
    cb, b_spec = spec(b, b_mode)
    n_steps = t // tt

    def body(a_ref, b_ref, o_ref, acc_ref):
        s = pl.program_id(1)

        @pl.when(s == 0)
        def _():
            acc_ref[...] = jnp.zeros_like(acc_ref)

        acc_ref[...] += lax.dot_general(a_ref[...], b_ref[...], (((0,), (0,)), ((), ())),
                                        preferred_element_type=F32)

        @pl.when(s == n_steps - 1)
        def _():
            o_ref[...] = acc_ref[...].astype(o_ref.dtype)

    return pl.pallas_call(
        body, name=name, grid=(nj, n_steps), in_specs=[a_spec, b_spec],
        out_specs=pl.BlockSpec((None, ca, cb), lambda j, s: (j, 0, 0)),
        out_shape=jax.ShapeDtypeStruct((nj, ca, cb), out_dtype),
        scratch_shapes=[pltpu.VMEM((ca, cb), F32)], compiler_params=_cparams(2),
    )(a, b)


def _mm_tn_wide(name, a, b, cb, *, a_split=1, out_dtype=BF16, tt=1024):
    t, ca = a.shape
    nb = b.shape[1] // cb
    tt = _row_tile(t, tt)
    n_steps = t // tt
    piece = ca // a_split

    def body(a_ref, b_ref, o_ref, acc_ref):
        s = pl.program_id(0)

        @pl.when(s == 0)
        def _():
            acc_ref[...] = jnp.zeros_like(acc_ref)

        for j in range(nb):
            bv = b_ref[:, j * cb:(j + 1) * cb]
            for c in range(a_split):
                rows = slice(c * piece, (c + 1) * piece)
                acc_ref[j, rows, :] += lax.dot_general(a_ref[:, rows], bv, (((0,), (0,)), ((), ())),
                                                       preferred_element_type=F32)

        @pl.when(s == n_steps - 1)
        def _():
            o_ref[...] = acc_ref[...].astype(o_ref.dtype)

    return pl.pallas_call(
        body, name=name, grid=(n_steps,),
        in_specs=[pl.BlockSpec((tt, ca), lambda s: (s, 0)), pl.BlockSpec((tt, nb * cb), lambda s: (s, 0))],
        out_specs=pl.BlockSpec((nb, ca, cb), lambda s: (0, 0, 0)),
        out_shape=jax.ShapeDtypeStruct((nb, ca, cb), out_dtype),
        scratch_shapes=[pltpu.VMEM((nb, ca, cb), F32)], compiler_params=_cparams(1),
    )(a, b)


def _rowwise(name, fn, rows, vecs, row_outs, vec_outs, *, tm=256, carry=None):
    nrows = rows[0][0].shape[0]
    tm = _row_tile(nrows, tm)
    n_r, n_v, n_ro, n_vo = len(rows), len(vecs), len(row_outs), len(vec_outs)

    def body(*refs):
        r_vals = [r[...] for r in refs[:n_r]]
        v_vals = [r[...] for r in refs[n_r:n_r + n_v]]
        ro_refs = refs[n_r + n_v:n_r + n_v + n_ro]
        vo_refs = refs[n_r + n_v + n_ro:]
        ro, vo = fn(r_vals, v_vals)
        for ref, val in zip(ro_refs, ro):
            ref[...] = val.astype(ref.dtype)
        if n_vo:
            @pl.when(pl.program_id(0) == 0)
            def _():
                for ref in vo_refs:
                    ref[...] = jnp.zeros_like(ref)

            for ref, val in zip(vo_refs, vo):
                ref[...] += val

    in_specs = [pl.BlockSpec((tm, cols), functools.partial(lambda i, cb: (i, cb), cb=cb)) for _, cols, cb in rows]
    in_specs += [pl.BlockSpec(v.shape, functools.partial(lambda i, nd: (0,) * nd, nd=v.ndim)) for v in vecs]
    out_specs = [pl.BlockSpec((tm, cols), lambda i: (i, 0)) for cols, _ in row_outs]
    out_specs += [pl.BlockSpec((1, cols), lambda i: (0, 0)) for cols in vec_outs]
    out_shape = [jax.ShapeDtypeStruct((nrows, cols), dt) for cols, dt in row_outs]
    out_shape += [jax.ShapeDtypeStruct((1, cols), F32) for cols in vec_outs]
    out, carried = _call(name, body, (nrows // tm,), in_specs, out_specs, out_shape, [],
                         [r[0] for r in rows] + list(vecs), carry)
    return out if carry is None else (out, carried)


def _whole(x):
    return (x, x.shape[1], 0)


def _colsum(x):
    return jnp.sum(x, axis=0, keepdims=True)


def _rstd(x):
    return lax.rsqrt(jnp.mean(x * x, axis=-1, keepdims=True) + EPS)


def _rms_bwd(dn, x, g):
    r = _rstd(x)
    c = dn * g
    dx = r * c - x * (r * r * r) * jnp.mean(c * x, axis=-1, keepdims=True)
    return dx, _colsum(dn * x * r)


def _rms_fwd(name, x, g, carry):
    def fn(r, v):
        (xv,), (gv,) = r, v
        return [xv * _rstd(xv) * gv], []

    (n,), carried = _rowwise(name, fn, [_whole(x)], [g], [(x.shape[1], BF16)], [], carry=carry)
    return n, carried


def _ep_post_res_pre(scale):
    def epilogue(acc, rows, vecs):
        (resid,), (g_post, g_next) = rows, vecs
        h = resid + scale * (acc * _rstd(acc) * g_post)
        return [acc, h, h * _rstd(h) * g_next], []

    return epilogue


def _post_bwd(dh, f, g_post, scale):
    return _rms_bwd(scale * dh, f, g_post)


def _ep_loss(scale, d):
    def epilogue(acc, rows, vecs):
        (resid, target), (g_post,) = rows, vecs
        err = resid + scale * (acc * _rstd(acc) * g_post) - target
        dy = err * (1.0 / d)
        df, dg_post = _post_bwd(dy, acc, g_post, scale)
        return [dy, df], [_colsum(err * err), dg_post]

    return epilogue


def _ep_pre_bwd_post(scale_prev):
    def epilogue(acc, rows, vecs):
        (h, dh_up, f_prev), (g_pre, g_post_prev) = rows, vecs
        dx, dg_pre = _rms_bwd(acc, h, g_pre)
        dh = dh_up + dx
        df, dg_post = _post_bwd(dh, f_prev, g_post_prev, scale_prev)
        return [dh, df], [dg_pre, dg_post]

    return epilogue


def _ep_pre_bwd_first():
    def epilogue(acc, rows, vecs):
        (x, dh_up), (g_pre,) = rows, vecs
        dx, dg_pre = _rms_bwd(acc, x, g_pre)
        return [dh_up + dx], [dg_pre]

    return epilogue


def _gate_specs(d, tm):
    first = (3 * D_ATTN + 2 * D_CONV) // COL
    return [pl.BlockSpec((tm, COL), functools.partial(lambda i, cb: (i, cb), cb=first + p)) for p in range(2 * d // COL)]


def _gate(piece_refs, bias_ref, c0, width):
    p, off = divmod(c0, COL)
    return jax.nn.sigmoid(piece_refs[p][:, off:off + width].astype(F32) + bias_ref[:, c0:c0 + width])


def _resident(w):
    return pl.BlockSpec(w.shape, functools.partial(lambda i, nd: (0,) * nd, nd=w.ndim), pipeline_mode=pl.Buffered(1))


def _mix_merge(att, cs, wao, wco, proj, gate_bias, tm=1024):
    t = att.shape[0]
    nj, _, nb = wao.shape
    d = nj * nb
    tm = _row_tile(t, tm)
    gate_specs = _gate_specs(d, tm)
    n_p = len(gate_specs)

    def body(att_ref, cs_ref, wao_ref, wco_ref, *rest):
        pieces, (gb_ref, ya_ref, yb_ref, m_ref) = rest[:n_p], rest[n_p:]
        av, cv = att_ref[...], cs_ref[...]
        for j in range(nj):
            cols = slice(j * nb, (j + 1) * nb)
            ya = _dot(av, wao_ref[j], False)
            yb = _dot(cv, wco_ref[j], False)
            merged = _gate(pieces, gb_ref, j * nb, nb) * ya + _gate(pieces, gb_ref, d + j * nb, nb) * yb
            ya_ref[:, cols] = ya.astype(ya_ref.dtype)
            yb_ref[:, cols] = yb.astype(yb_ref.dtype)
            m_ref[:, cols] = merged.astype(m_ref.dtype)

    row = lambda x: pl.BlockSpec((tm, x.shape[1]), lambda i: (i, 0))
    out_spec = pl.BlockSpec((tm, d), lambda i: (i, 0))
    return pl.pallas_call(
        body, name="mix_merge", grid=(t // tm,),
        in_specs=[row(att), row(cs), _resident(wao), _resident(wco)] + gate_specs + [_resident(gate_bias)],
        out_specs=[out_spec] * 3, out_shape=[jax.ShapeDtypeStruct((t, d), BF16)] * 3, compiler_params=_cparams(1),
    )(att, cs, wao, wco, *([proj] * n_p), gate_bias)


def _mix_d_merge(dmo, wout, y_a, y_b, wao, wco, proj, gate_bias, tm=512):
    t, d = dmo.shape
    nj, _, nb = wao.shape
    ka, kc = wao.shape[1], wco.shape[1]
    tm = _row_tile(t, tm)
    gate_specs = _gate_specs(d, tm)
    n_p = len(gate_specs)

    def body(dmo_ref, wout_ref, ya_ref, yb_ref, wao_ref, wco_ref, *rest):
        pieces, (gb_ref, dya_ref, dyb_ref, dg_ref, datt_ref, dcs_ref, dgb_ref) = rest[:n_p], rest[n_p:]

        @pl.when(pl.program_id(0) == 0)
        def _():
            dgb_ref[...] = jnp.zeros_like(dgb_ref)

        dmo_v = dmo_ref[...]
        datt = dcs = None
        for j in range(nj):
            cols, cols_b = slice(j * nb, (j + 1) * nb), slice(d + j * nb, d + (j + 1) * nb)
            dm = _dot(dmo_v, wout_ref[j], True)
            ga, gb = _gate(pieces, gb_ref, j * nb, nb), _gate(pieces, gb_ref, d + j * nb, nb)
            dya, dyb = (dm * ga).astype(BF16), (dm * gb).astype(BF16)
            dga = dm * ya_ref[:, cols].astype(F32) * ga * (1.0 - ga)
            dgb = dm * yb_ref[:, cols].astype(F32) * gb * (1.0 - gb)
            dya_ref[:, cols], dyb_ref[:, cols] = dya, dyb
            dg_ref[:, cols], dg_ref[:, cols_b] = dga.astype(dg_ref.dtype), dgb.astype(dg_ref.dtype)
            dgb_ref[:, cols] += _colsum(dga)
            dgb_ref[:, cols_b] += _colsum(dgb)
            pa, pc = _dot(dya, wao_ref[j], True), _dot(dyb, wco_ref[j], True)
            datt, dcs = (pa, pc) if datt is None else (datt + pa, dcs + pc)
        datt_ref[...] = datt.astype(datt_ref.dtype)
        dcs_ref[...] = dcs.astype(dcs_ref.dtype)

    row = lambda cols: pl.BlockSpec((tm, cols), lambda i: (i, 0))
    return pl.pallas_call(
        body, name="mix_d_merge", grid=(t // tm,),
        in_specs=[row(d), _resident(wout), row(d), row(d), _resident(wao), _resident(wco)] + gate_specs
        + [_resident(gate_bias)],
        out_specs=[row(d), row(d), row(2 * d), row(ka), row(kc), pl.BlockSpec((1, 2 * d), lambda i: (0, 0))],
        out_shape=[jax.ShapeDtypeStruct((t, d), BF16), jax.ShapeDtypeStruct((t, d), BF16),
                   jax.ShapeDtypeStruct((t, 2 * d), BF16), jax.ShapeDtypeStruct((t, ka), BF16),
                   jax.ShapeDtypeStruct((t, kc), F32), jax.ShapeDtypeStruct((1, 2 * d), F32)],
        compiler_params=_cparams(1),
    )(dmo, wout, y_a, y_b, wao, wco, *([proj] * n_p), gate_bias)


def _adamw_math(wv, gv, mv, vv):
    m2 = ADAM_B1 * mv + (1.0 - ADAM_B1) * gv
    v2 = ADAM_B2 * vv + (1.0 - ADAM_B2) * (gv * gv)
    m_hat = m2 / (1.0 - ADAM_B1 ** ADAM_STEP)
    v_hat = v2 / (1.0 - ADAM_B2 ** ADAM_STEP)
    delta = -ADAM_LR * (m_hat / (jnp.sqrt(v_hat) + ADAM_EPS) + ADAM_WD * wv)
    return delta, m2, v2


def _adamw(name, w, g, m, v):
    def fn(r, _):
        return list(_adamw_math(*r)), []

    c = w.shape[1]
    return _rowwise(name, fn, [_whole(w), _whole(g), _whole(m), _whole(v)], [], [(c, F32)] * 3, [], tm=256)


POS_C, POS_CHIP, POS_PEER = 0, 1, 2


def _placed_call(body, name, pos, grid, in_specs, out_specs, out_shape, args):
    return pl.pallas_call(
        body, name=name, out_shape=out_shape, compiler_params=_cparams(len(grid)),
        grid_spec=pltpu.PrefetchScalarGridSpec(num_scalar_prefetch=1, grid=grid, in_specs=in_specs,
                                               out_specs=out_specs),
    )(pos, *args)


def _cast_into(name, pos, w):
    r, cols = w.shape
    tm = _row_tile(r, 1024)

    def body(pos_ref, w_ref, o_ref):
        o_ref[...] = w_ref[...].astype(o_ref.dtype)

    return _placed_call(
        body, name, pos, (r // tm,), [pl.BlockSpec((tm, cols), lambda i, pos: (i, 0))],
        pl.BlockSpec((None, tm, cols), lambda i, pos: (pos[POS_CHIP], i, 0)),
        jax.ShapeDtypeStruct((N_CHIPS, r, cols), BF16), [w])


def _add_pair(name, pos, grad, landed):
    nj, half, cols = landed.shape
    tm = _row_tile(half, 512)
    nb = half // tm

    def body(pos_ref, g_ref, l_ref, o_ref):
        o_ref[...] = (g_ref[...].astype(F32) + l_ref[...].astype(F32)).astype(o_ref.dtype)

    spec = pl.BlockSpec((None, tm, cols), lambda j, i, pos: (j, i, 0))
    return _placed_call(
        body, name, pos, (nj, nb),
        [pl.BlockSpec((None, tm, cols), lambda j, i, pos: (j, pos[POS_C] * nb + i, 0)), spec], spec,
        jax.ShapeDtypeStruct(landed.shape, BF16), [grad, landed])


def _add_chips(name, pos, part, landed):
    _, half, cols = landed.shape
    tm = _row_tile(half, 512)

    def body(pos_ref, p_ref, l0_ref, l1_ref, l2_ref, o_ref):
        acc = p_ref[...].astype(F32)
        for ref in (l0_ref, l1_ref, l2_ref):
            acc = acc + ref[...].astype(F32)
        o_ref[...] = acc

    slot = lambda at: pl.BlockSpec((None, tm, cols), functools.partial(lambda i, pos, at: (pos[at], i, 0), at=at))
    return _placed_call(
        body, name, pos, (half // tm,), [slot(POS_CHIP)] + [slot(POS_PEER + k) for k in range(3)],
        pl.BlockSpec((tm, cols), lambda i, pos: (i, 0)), jax.ShapeDtypeStruct((half, cols), F32),
        [part, landed, landed, landed])


def _adamw_halves(name, pos, w, m, v, own, landed):
    r, cols = w.shape
    half = own.shape[0]
    tm = _row_tile(half, 384 * 1024 // cols)
    nb = half // tm

    def body(pos_ref, w_ref, m_ref, v_ref, own_ref, land_ref, g_out, d_out, m_out, v_out):
        mine = pl.program_id(0) == pos_ref[POS_C]
        g = jnp.where(mine, own_ref[...], land_ref[...])
        delta, m2, v2 = _adamw_math(w_ref[...], g, m_ref[...], v_ref[...])
        g_out[...] = g
        d_out[...] = delta
        m_out[...] = m2
        v_out[...] = v2

    full = pl.BlockSpec((tm, cols), lambda h, i, pos: (h * nb + i, 0))
    used = lambda h, i, pos: (jnp.where(h == pos[POS_C], i, 0), 0)
    unused = lambda h, i, pos: (jnp.where(h == pos[POS_C], 0, i), 0)
    return _placed_call(
        body, name, pos, (2, nb), [full, full, full, pl.BlockSpec((tm, cols), used), pl.BlockSpec((tm, cols), unused)],
        [full] * 4,
        [jax.ShapeDtypeStruct((r, cols), F32)] * 4, [w, m, v, own, landed])


N_START = K_PAD // Q_BLOCK


def _rel_onehot(n_q):
    e = np.arange(REL_EXT)
    dist = K_PAD - (e - (n_q - 1))
    idx = np.clip(dist, -REL_CLIP, REL_CLIP) + REL_CLIP
    return (np.arange(REL_PAD)[:, None] == idx[None, :]).astype(np.float32)


def _skew(x, left):
    row = lax.broadcasted_iota(jnp.int32, x.shape, 0)
    for bit in range(x.shape[0].bit_length() - 1):
        amount = 1 << bit
        rolled = pltpu.roll(x, REL_EXT - amount if left else amount, 1)
        x = jnp.where((row >> bit) & 1 == 1, rolled, x)
    return x


def _bias_expand(table_pad, carry):
    onehot = jnp.asarray(_rel_onehot(Q_BLOCK))

    def core(t_ref, oh_ref, o_ref):
        ext = jnp.dot(t_ref[...], oh_ref[...], precision=lax.Precision.HIGHEST, preferred_element_type=F32)
        qc = lax.broadcasted_iota(jnp.int32, (Q_BLOCK, K_WIN), 0) // CHUNK
        kpos = lax.broadcasted_iota(jnp.int32, (Q_BLOCK, K_WIN), 1)
        band = (kpos // CHUNK >= qc) & (kpos // CHUNK <= qc + LEFT_CHUNKS)
        rows = jnp.broadcast_to(ext, (Q_BLOCK, REL_EXT))
        rolled = _skew(pltpu.roll(rows, REL_EXT - (Q_BLOCK - 1), 1), left=False)[:, :K_WIN]
        for v in range(N_START + 1):
            o_ref[v] = jnp.where(band & (kpos + v * Q_BLOCK >= K_PAD), rolled, NEG)

    (bias,), carried = _call(
        "bias_expand", core, (N_HEADS,),
        [pl.BlockSpec((None, 1, REL_PAD), lambda h: (h, 0, 0)), pl.BlockSpec(onehot.shape, lambda h: (0, 0))],
        [pl.BlockSpec((N_START + 1, None, Q_BLOCK, K_WIN), lambda h: (0, h, 0, 0))],
        [jax.ShapeDtypeStruct((N_START + 1, N_HEADS, Q_BLOCK, K_WIN), F32)], [],
        [table_pad.reshape(N_HEADS, 1, REL_PAD), onehot], carry)
    return bias, carried


def _bias_fold(dbias):
    onehot_t = jnp.asarray(_rel_onehot(CHUNK).T)

    def body(d_ref, oh_ref, o_ref, ext_ref):
        for h in range(N_HEADS):
            x = jnp.concatenate([d_ref[h], jnp.zeros((CHUNK, REL_EXT - K_WIN), F32)], axis=1)
            rolled = _skew(pltpu.roll(x, CHUNK - 1, 1), left=True)
            ext_ref[h:h + 1, :] = jnp.sum(rolled, axis=0, keepdims=True)
        o_ref[...] = jnp.dot(ext_ref[...], oh_ref[...], precision=lax.Precision.HIGHEST,
                             preferred_element_type=F32)

    return pl.pallas_call(
        body, name="bias_fold", out_shape=jax.ShapeDtypeStruct((N_HEADS, REL_PAD), F32),
        scratch_shapes=[pltpu.VMEM((N_HEADS, REL_EXT), F32)],
        compiler_params=pltpu.CompilerParams(vmem_limit_bytes=VMEM_LIMIT_BYTES),
    )(dbias, onehot_t)


def _head_lanes():
    lane = lax.broadcasted_iota(jnp.int32, (1, 2 * HEAD_DIM), 1)
    return [lane < HEAD_DIM, lane >= HEAD_DIM]


def _only(mask, x, scale=None):
    x = jnp.where(mask, x, jnp.zeros_like(x))
    return x if scale is None else x * scale


def _contract_lanes(a, b):
    return lax.dot_general(a, b, (((1,), (1,)), ((), ())), preferred_element_type=F32)


def _contract_rows(a, b):
    return lax.dot_general(a, b, (((0,), (0,)), ((), ())), preferred_element_type=F32)


PAIR = 2 * HEAD_DIM
N_PAIRS = D_ATTN // PAIR


def _attn_specs(pairs):
    width = pairs * PAIR
    per = D_ATTN // width
    row_spec = pl.BlockSpec((Q_BLOCK, width), lambda g, i: (i, g))
    kv_specs = [pl.BlockSpec((Q_BLOCK, width),
                             functools.partial(lambda g, i, kk, c0: (jnp.maximum(i + kk - N_START, 0), c0 + g),
                                               kk=kk, c0=c0))
                for c0 in (per, 2 * per) for kk in range(K_WIN // Q_BLOCK)]
    bias_spec = pl.BlockSpec((None, 2 * pairs, Q_BLOCK, K_WIN), lambda g, i: (jnp.minimum(i, N_START), g, 0, 0))
    return row_spec, kv_specs, bias_spec


def _attn_fwd(proj, bias, pairs=N_PAIRS):
    t = proj.shape[0]
    n_win = K_WIN // Q_BLOCK

    def body(q_ref, *refs):
        k_refs, v_refs = refs[:n_win], refs[n_win:2 * n_win]
        b_ref, o_ref, lse_ref = refs[2 * n_win:]
        for pp in range(pairs):
            cols = slice(pp * PAIR, (pp + 1) * PAIR)
            k = jnp.concatenate([r[:, cols] for r in k_refs], axis=0)
            v = jnp.concatenate([r[:, cols] for r in v_refs], axis=0)
            q = q_ref[:, cols]
            o = lse = None
            for hh, lanes in enumerate(_head_lanes()):
                s = _contract_lanes(_only(lanes, q, HEAD_DIM ** -0.5), k) + b_ref[2 * pp + hh]
                m = jnp.max(s, axis=1, keepdims=True)
                p = jnp.exp(s - m)
                l = jnp.sum(p, axis=1, keepdims=True)
                oh = jnp.dot(p.astype(BF16), v, preferred_element_type=F32) / l
                lse_h = jnp.broadcast_to(m + jnp.log(l), oh.shape)
                o, lse = (oh, lse_h) if o is None else (jnp.where(lanes, oh, o), jnp.where(lanes, lse_h, lse))
            o_ref[:, cols] = o.astype(o_ref.dtype)
            lse_ref[:, cols] = lse

    row_spec, kv_specs, bias_spec = _attn_specs(pairs)
    return pl.pallas_call(
        body, name="attn_fwd", grid=(N_PAIRS // pairs, t // Q_BLOCK),
        in_specs=[row_spec] + kv_specs + [bias_spec], out_specs=[row_spec, row_spec],
        out_shape=[jax.ShapeDtypeStruct((t, D_ATTN), BF16), jax.ShapeDtypeStruct((t, D_ATTN), F32)],
        compiler_params=_cparams(2),
    )(*([proj] * (1 + 2 * n_win)), bias)


def _attn_bwd(proj, bias, att, lse, datt, pairs=2):
    t = proj.shape[0]
    n_win = K_WIN // Q_BLOCK
    n_blocks = t // Q_BLOCK

    def body(q_ref, *refs):
        k_refs, v_refs = refs[:n_win], refs[n_win:2 * n_win]
        b_ref, o_ref, lse_ref, do_ref, dq_ref, dk_ref, dv_ref, db_ref, dk_acc, dv_acc = refs[2 * n_win:]
        i = pl.program_id(1)

        @pl.when(i == 0)
        def _():
            dk_acc[...] = jnp.zeros_like(dk_acc)
            dv_acc[...] = jnp.zeros_like(dv_acc)
            db_ref[...] = jnp.zeros_like(db_ref)

        rows = pl.ds(pl.multiple_of(i * Q_BLOCK, Q_BLOCK), K_WIN)
        scale = HEAD_DIM ** -0.5
        for pp in range(pairs):
            cols = slice(pp * PAIR, (pp + 1) * PAIR)
            k = jnp.concatenate([r[:, cols] for r in k_refs], axis=0)
            v = jnp.concatenate([r[:, cols] for r in v_refs], axis=0)
            q, do, o = q_ref[:, cols], do_ref[:, cols], o_ref[:, cols].astype(F32)
            dq = dk = dv = None
            for hh, lanes in enumerate(_head_lanes()):
                qh, doh = _only(lanes, q, scale), _only(lanes, do)
                s = _contract_lanes(qh, k) + b_ref[2 * pp + hh]
                lse_col = pp * PAIR + hh * HEAD_DIM
                p = jnp.exp(s - lse_ref[:, lse_col:lse_col + 1])
                delta = jnp.sum(doh.astype(F32) * o, axis=1, keepdims=True)
                ds = p * (_contract_lanes(doh, v) - delta)
                folded = ds[:CHUNK]
                for c in range(1, Q_BLOCK // CHUNK):
                    folded = folded + pltpu.roll(ds[c * CHUNK:(c + 1) * CHUNK], K_WIN - c * CHUNK, 1)
                db_ref[2 * pp + hh] += folded
                dsb = ds.astype(BF16)
                dqh = jnp.dot(dsb, k, preferred_element_type=F32)
                dq = dqh if dq is None else jnp.where(lanes, dqh, dq)
                dkh, dvh = _contract_rows(dsb, qh), _contract_rows(p.astype(BF16), doh)
                dk, dv = (dkh, dvh) if dk is None else (dk + dkh, dv + dvh)
            dq_ref[:, cols] = (dq * scale).astype(dq_ref.dtype)
            dk_acc[rows, cols] += dk
            dv_acc[rows, cols] += dv

        @pl.when(i == n_blocks - 1)
        def _():
            dk_ref[...] = dk_acc[K_PAD:, :].astype(dk_ref.dtype)
            dv_ref[...] = dv_acc[K_PAD:, :].astype(dv_ref.dtype)

    width = pairs * PAIR
    row_spec, kv_specs, bias_spec = _attn_specs(pairs)
    full_spec = pl.BlockSpec((t, width), lambda g, i: (0, g))
    return pl.pallas_call(
        body, name="attn_bwd", grid=(N_PAIRS // pairs, n_blocks),
        in_specs=[row_spec] + kv_specs + [bias_spec, row_spec, row_spec, row_spec],
        out_specs=[row_spec, full_spec, full_spec,
                   pl.BlockSpec((2 * pairs, CHUNK, K_WIN), lambda g, i: (g, 0, 0))],
        out_shape=[jax.ShapeDtypeStruct((t, D_ATTN), BF16)] * 3 + [jax.ShapeDtypeStruct((N_HEADS, CHUNK, K_WIN), F32)],
        scratch_shapes=[pltpu.VMEM((t + K_PAD, width), F32)] * 2, compiler_params=_cparams(2),
    )(*([proj] * (1 + 2 * n_win)), bias, att, lse, datt)


CONV_LEAD = CONV_HALO - (CONV_WIDTH - 1)
CONV_LANES = 128
CONV_ROWS = 64


def _conv_specs(t):
    tt = _row_tile(t, CONV_TILE)
    per = tt // CONV_HALO
    n_halo = t // CONV_HALO
    tile = lambda cb: pl.BlockSpec((tt, COL), functools.partial(lambda i, cb: (i, cb), cb=cb))
    prev = lambda cb: pl.BlockSpec((CONV_HALO, COL),
                                   functools.partial(lambda i, cb: (jnp.maximum(i * per - 1, 0), cb), cb=cb))
    nxt = lambda cb: pl.BlockSpec((CONV_HALO, COL),
                                  functools.partial(lambda i, cb: (jnp.minimum((i + 1) * per, n_halo - 1), cb), cb=cb))
    vec = lambda shape: pl.BlockSpec(shape, lambda i: (0, 0))
    return tt, tile, prev, nxt, vec


def _glu(ca, cg, bias):
    return (ca.astype(F32) + bias[:, :D_CONV]) * jax.nn.sigmoid(cg.astype(F32) + bias[:, D_CONV:])


SUBLANES = 8


def _shift_copies(ext_ref):
    n = ext_ref.shape[1] - SUBLANES
    for s in range(1, SUBLANES):
        ext_ref[s, 0:n, :] = ext_ref[0, s:s + n, :]


def _tap_tiles(ext_ref, first_row, r0, lanes):
    n_g = CONV_ROWS // SUBLANES
    for s in range(SUBLANES):
        taps = [w for w in range(CONV_WIDTH) if first_row(w) % SUBLANES == s]
        if not taps:
            continue
        lo = min(first_row(w) for w in taps) - s
        n_tiles = (max(first_row(w) for w in taps) - s - lo) // SUBLANES + n_g
        tiles = [ext_ref[s, r0 + lo + SUBLANES * b:r0 + lo + SUBLANES * (b + 1), lanes] for b in range(n_tiles)]
        for w in taps:
            k = (first_row(w) - s - lo) // SUBLANES
            yield w, tiles[k:k + n_g]


def _taps(ext_ref, tt, first_row, w_ref, out_ref):
    n_g = CONV_ROWS // SUBLANES
    for l0 in range(0, D_CONV, CONV_LANES):
        lanes = slice(l0, l0 + CONV_LANES)
        for r0 in range(0, tt, CONV_ROWS):
            acc = [jnp.zeros((SUBLANES, CONV_LANES), F32)] * n_g
            for w, tiles in _tap_tiles(ext_ref, first_row, r0, lanes):
                weight = jnp.broadcast_to(w_ref[w:w + 1, lanes], (SUBLANES, CONV_LANES))
                acc = [a + t * weight for a, t in zip(acc, tiles)]
            for g in range(n_g):
                out_ref[r0 + SUBLANES * g:r0 + SUBLANES * (g + 1), lanes] = acc[g]


def _tap_sums(ext_ref, tt, first_row, x_ref, out_ref):
    n_g = CONV_ROWS // SUBLANES
    for l0 in range(0, D_CONV, CONV_LANES):
        lanes = slice(l0, l0 + CONV_LANES)
        acc = [jnp.zeros((SUBLANES, CONV_LANES), F32)] * CONV_WIDTH
        for r0 in range(0, tt, CONV_ROWS):
            x = [x_ref[0, r0 + SUBLANES * g:r0 + SUBLANES * (g + 1), lanes] for g in range(n_g)]
            for w, tiles in _tap_tiles(ext_ref, first_row, r0, lanes):
                part = tiles[0] * x[0]
                for g in range(1, n_g):
                    part = part + tiles[g] * x[g]
                acc[w] = acc[w] + part
        for w in range(CONV_WIDTH):
            out_ref[w:w + 1, lanes] += jnp.sum(acc[w], axis=0, keepdims=True)


def _conv_fwd(proj, glu_bias, dw, dw_b, ln_g, ln_b):
    t = proj.shape[0]
    tt, tile, prev, nxt, vec = _conv_specs(t)
    ca_blk, cg_blk = 3 * D_ATTN // COL, 3 * D_ATTN // COL + 1

    def body(ca_ref, cg_ref, pa_ref, pg_ref, gb_ref, dw_ref, dwb_ref, g_ref, b_ref, cs_ref, c_ref, z_ref, ext_ref):
        i = pl.program_id(0)
        bias = gb_ref[...]
        c = _glu(ca_ref[...], cg_ref[...], bias)
        halo = _glu(pa_ref[...], pg_ref[...], bias)
        ext_ref[0, 0:CONV_HALO, :] = jnp.where(i == 0, 0.0, halo)
        ext_ref[0, CONV_HALO:, :] = c
        _shift_copies(ext_ref)
        c_ref[...] = c
        _taps(ext_ref, tt, lambda w: CONV_LEAD + w, dw_ref, z_ref)
        z = z_ref[...] + dwb_ref[...]
        z_ref[...] = z
        mu = jnp.mean(z, axis=-1, keepdims=True)
        zc = z - mu
        y = zc * lax.rsqrt(jnp.mean(zc * zc, axis=-1, keepdims=True) + EPS) * g_ref[...] + b_ref[...]
        cs_ref[...] = (y * jax.nn.sigmoid(y)).astype(cs_ref.dtype)

    out_spec = pl.BlockSpec((tt, D_CONV), lambda i: (i, 0))
    return pl.pallas_call(
        body, name="conv_fwd", grid=(t // tt,),
        in_specs=[tile(ca_blk), tile(cg_blk), prev(ca_blk), prev(cg_blk), vec(glu_bias.shape), vec(dw.shape),
                  vec(dw_b.shape), vec(ln_g.shape), vec(ln_b.shape)],
        out_specs=[out_spec] * 3,
        out_shape=[jax.ShapeDtypeStruct((t, D_CONV), BF16), jax.ShapeDtypeStruct((t, D_CONV), F32),
                   jax.ShapeDtypeStruct((t, D_CONV), F32)],
        scratch_shapes=[pltpu.VMEM((SUBLANES, tt + CONV_HALO, D_CONV), F32)], compiler_params=_cparams(1),
    )(proj, proj, proj, proj, glu_bias, dw, dw_b, ln_g, ln_b)


def _conv_bwd(proj, c, z, dcs, glu_bias, dw, ln_g, ln_b):
    t = proj.shape[0]
    tt, tile, prev, nxt, vec = _conv_specs(t)
    n_tiles = t // tt
    ca_blk, cg_blk = 3 * D_ATTN // COL, 3 * D_ATTN // COL + 1

    def ln_bwd(zv, dcsv, g, b):
        mu = jnp.mean(zv, axis=-1, keepdims=True)
        zc = zv - mu
        rstd = lax.rsqrt(jnp.mean(zc * zc, axis=-1, keepdims=True) + EPS)
        zhat = zc * rstd
        y = zhat * g + b
        sig = jax.nn.sigmoid(y)
        dy = dcsv * sig * (1.0 + y * (1.0 - sig))
        dzh = dy * g
        dz = rstd * (dzh - jnp.mean(dzh, axis=-1, keepdims=True) - zhat * jnp.mean(dzh * zhat, axis=-1, keepdims=True))
        return dz, dy, zhat

    def body(ca_ref, cg_ref, c_ref, cprev_ref, z_ref, znext_ref, dcs_ref, dcsnext_ref, gb_ref, dw_ref, g_ref, b_ref,
             dcin_ref, ddw_ref, ddwb_ref, dg_ref, db_ref, dgb_ref, cext_ref, dzext_ref, dc_ref):
        i = pl.program_id(0)

        @pl.when(i == 0)
        def _():
            for ref in (ddw_ref, ddwb_ref, dg_ref, db_ref, dgb_ref):
                ref[...] = jnp.zeros_like(ref)

        g, b = g_ref[...], b_ref[...]
        dz, dy, zhat = ln_bwd(z_ref[...], dcs_ref[...], g, b)
        dz_next, _, _ = ln_bwd(znext_ref[...], dcsnext_ref[...], g, b)
        dg_ref[...] += _colsum(dy * zhat)
        db_ref[...] += _colsum(dy)
        ddwb_ref[...] += _colsum(dz)
        dzext_ref[0, 0:tt, :] = dz
        dzext_ref[0, tt:, :] = jnp.where(i == n_tiles - 1, 0.0, dz_next)
        _shift_copies(dzext_ref)
        cext_ref[0, 0:CONV_HALO, :] = jnp.where(i == 0, 0.0, cprev_ref[...])
        cext_ref[0, CONV_HALO:, :] = c_ref[...]
        _shift_copies(cext_ref)
        _taps(dzext_ref, tt, lambda w: CONV_WIDTH - 1 - w, dw_ref, dc_ref)
        _tap_sums(cext_ref, tt, lambda w: CONV_LEAD + w, dzext_ref, ddw_ref)
        bias = gb_ref[...]
        a_in = ca_ref[...].astype(F32) + bias[:, :D_CONV]
        sg = jax.nn.sigmoid(cg_ref[...].astype(F32) + bias[:, D_CONV:])
        dc = dc_ref[...]
        dcin = jnp.concatenate([dc * sg, dc * a_in * sg * (1.0 - sg)], axis=1)
        dcin_ref[...] = dcin.astype(dcin_ref.dtype)
        dgb_ref[...] += _colsum(dcin)

    row = lambda: pl.BlockSpec((tt, D_CONV), lambda i: (i, 0))
    per = tt // CONV_HALO
    n_halo = t // CONV_HALO
    prev_row = pl.BlockSpec((CONV_HALO, D_CONV), lambda i: (jnp.maximum(i * per - 1, 0), 0))
    next_row = lambda: pl.BlockSpec((CONV_HALO, D_CONV), lambda i: (jnp.minimum((i + 1) * per, n_halo - 1), 0))
    acc = lambda shape: pl.BlockSpec(shape, lambda i: (0, 0))
    return pl.pallas_call(
        body, name="conv_bwd", grid=(n_tiles,),
        in_specs=[tile(ca_blk), tile(cg_blk), row(), prev_row, row(), next_row(), row(), next_row(),
                  vec(glu_bias.shape), vec(dw.shape), vec(ln_g.shape), vec(ln_b.shape)],
        out_specs=[pl.BlockSpec((tt, 2 * D_CONV), lambda i: (i, 0)), acc(dw.shape), acc((1, D_CONV)),
                   acc((1, D_CONV)), acc((1, D_CONV)), acc((1, 2 * D_CONV))],
        out_shape=[jax.ShapeDtypeStruct((t, 2 * D_CONV), BF16), jax.ShapeDtypeStruct(dw.shape, F32),
                   jax.ShapeDtypeStruct((1, D_CONV), F32), jax.ShapeDtypeStruct((1, D_CONV), F32),
                   jax.ShapeDtypeStruct((1, D_CONV), F32), jax.ShapeDtypeStruct((1, 2 * D_CONV), F32)],
        scratch_shapes=[pltpu.VMEM((SUBLANES, tt + CONV_HALO, D_CONV), F32),
                        pltpu.VMEM((SUBLANES, tt + CONV_HALO, D_CONV), F32), pltpu.VMEM((tt, D_CONV), F32)],
        compiler_params=_cparams(1),
    )(proj, proj, c, c, z, z, dcs, dcs, glu_bias, dw, ln_g, ln_b)


def _place():
    x, y, c = lax.axis_index("x"), lax.axis_index("y"), lax.axis_index("c")
    chips = [(1 - x, y), (x, 1 - y), (1 - x, 1 - y)]
    return x, y, c, chips


def _chip_index(chip):
    return 2 * chip[0] + chip[1]


def _half_rows(c, half):
    return pl.ds(pl.multiple_of(c * half, 16), half)


def _gather_carry(blocked):
    n = len(blocked)

    def over_ici(o_refs, send_sems, recv_sems):
        x, y, c, chips = _place()
        me = _chip_index((x, y))
        copies = []
        for a in range(n):
            mine = o_refs[a].at[me, _half_rows(c, o_refs[a].shape[1] // 2), :]
            for k, chip in enumerate(chips):
                copies.append(pltpu.make_async_remote_copy(
                    src_ref=mine, dst_ref=mine, send_sem=send_sems.at[6 * a + k], recv_sem=recv_sems.at[6 * a + k],
                    device_id=(chip[0], chip[1], c), device_id_type=MESH))
        return copies

    def to_sibling(o_refs, send_sems, recv_sems, sent_by_me):
        x, y, c, chips = _place()
        copies = []
        for a in range(n):
            rows = _half_rows(c if sent_by_me else 1 - c, o_refs[a].shape[1] // 2)
            for k, chip in enumerate(chips):
                landed = o_refs[a].at[_chip_index(chip), rows, :]
                copies.append(pltpu.make_async_remote_copy(
                    src_ref=landed, dst_ref=landed, send_sem=send_sems.at[6 * a + 3 + k],
                    recv_sem=recv_sems.at[6 * a + 3 + k], device_id=(x, y, 1 - c), device_id_type=MESH))
        return copies

    def start(ins, outs, sems):
        for cp in over_ici(outs, *sems):
            cp.start()

    def hand_on(ins, outs, sems):
        for arrived, onward in zip(over_ici(outs, *sems), to_sibling(outs, *sems, True)):
            arrived.wait_recv()
            onward.start()

    def finish(ins, outs, sems):
        for cp in to_sibling(outs, *sems, False):
            cp.wait_recv()
        for cp in over_ici(outs, *sems) + to_sibling(outs, *sems, True):
            cp.wait_send()

    return _Carry(
        ins=list(blocked), outs=[jax.ShapeDtypeStruct(w.shape, w.dtype) for w in blocked],
        aliases={a: a for a in range(n)},
        sems=[pltpu.SemaphoreType.DMA((6 * n,)), pltpu.SemaphoreType.DMA((6 * n,))],
        phases=[("first", start), ("late", hand_on), ("last", finish)])


def _pair_exchange(name, grads):
    n = len(grads)

    def body(*refs):
        g_refs, land_refs = refs[:n], refs[n:2 * n]
        send_sems, recv_sems = refs[2 * n:]
        x, y, c, _ = _place()
        copies = []
        for a in range(n):
            half = g_refs[a].shape[1] // 2
            cp = pltpu.make_async_remote_copy(
                src_ref=g_refs[a].at[:, _half_rows(1 - c, half), :], dst_ref=land_refs[a],
                send_sem=send_sems.at[a], recv_sem=recv_sems.at[a], device_id=(x, y, 1 - c), device_id_type=MESH)
            cp.start()
            copies.append(cp)
        for cp in copies:
            cp.wait()

    return pl.pallas_call(
        body, name=name, in_specs=[ANY] * n, out_specs=[ANY] * n,
        out_shape=[jax.ShapeDtypeStruct((g.shape[0], g.shape[1] // 2, g.shape[2]), g.dtype) for g in grads],
        scratch_shapes=[pltpu.SemaphoreType.DMA((n,)), pltpu.SemaphoreType.DMA((n,))],
    )(*grads)


def _to_owner_carry(parts):
    n = len(parts)

    def sends(p_refs, l_refs, send_sems, recv_sems):
        x, y, c, chips = _place()
        me = _chip_index((x, y))
        return [pltpu.make_async_remote_copy(
            src_ref=p_refs[a].at[_chip_index(chip)], dst_ref=l_refs[a].at[me],
            send_sem=send_sems.at[3 * a + k], recv_sem=recv_sems.at[3 * a + k],
            device_id=(chip[0], chip[1], c), device_id_type=MESH) for a in range(n) for k, chip in enumerate(chips)]

    def start(ins, outs, sems):
        for cp in sends(ins, outs, *sems):
            cp.start()

    def finish(ins, outs, sems):
        x, y, c, chips = _place()
        send_sems, recv_sems = sems
        for a in range(n):
            for k, chip in enumerate(chips):
                slot = outs[a].at[_chip_index(chip)]
                pltpu.make_async_remote_copy(
                    src_ref=slot, dst_ref=slot, send_sem=send_sems.at[3 * a + k], recv_sem=recv_sems.at[3 * a + k],
                    device_id=(chip[0], chip[1], c), device_id_type=MESH).wait_recv()
        for cp in sends(ins, outs, *sems):
            cp.wait_send()

    return _Carry(
        ins=list(parts), outs=[jax.ShapeDtypeStruct(p.shape, p.dtype) for p in parts], aliases={},
        sems=[pltpu.SemaphoreType.DMA((3 * n,)), pltpu.SemaphoreType.DMA((3 * n,))],
        phases=[("first", start), ("last", finish)])


def _swap_halves(halves):
    n = len(halves)

    def body(*refs):
        h_refs, o_refs = refs[:n], refs[n:2 * n]
        send_sems, recv_sems = refs[2 * n:]
        x, y, c, _ = _place()
        copies = []
        for a in range(n):
            cp = pltpu.make_async_remote_copy(
                src_ref=h_refs[a], dst_ref=o_refs[a], send_sem=send_sems.at[a], recv_sem=recv_sems.at[a],
                device_id=(x, y, 1 - c), device_id_type=MESH)
            cp.start()
            copies.append(cp)
        for cp in copies:
            cp.wait()

    return pl.pallas_call(
        body, name="grad_swap_halves", in_specs=[ANY] * n, out_specs=[ANY] * n,
        out_shape=[jax.ShapeDtypeStruct(h.shape, h.dtype) for h in halves],
        scratch_shapes=[pltpu.SemaphoreType.DMA((n,)), pltpu.SemaphoreType.DMA((n,))],
    )(*halves)


def _all_devices(name, block):
    r, cols = block.shape

    def body(b_ref, all_ref, sum_ref, send_sems, recv_sems):
        x, y, c, _ = _place()
        me = 4 * x + 2 * y + c
        all_ref[me] = b_ref[...]
        flips = [(fx, fy, fc) for fx in (0, 1) for fy in (0, 1) for fc in (0, 1)][1:]
        copies = []
        for k, (fx, fy, fc) in enumerate(flips):
            cp = pltpu.make_async_remote_copy(
                src_ref=b_ref, dst_ref=all_ref.at[me], send_sem=send_sems.at[k], recv_sem=recv_sems.at[k],
                device_id=(x ^ fx, y ^ fy, c ^ fc), device_id_type=MESH)
            cp.start()
            copies.append(cp)
        for k, (fx, fy, fc) in enumerate(flips):
            slot = all_ref.at[4 * (x ^ fx) + 2 * (y ^ fy) + (c ^ fc)]
            pltpu.make_async_remote_copy(
                src_ref=slot, dst_ref=slot, send_sem=send_sems.at[k], recv_sem=recv_sems.at[k],
                device_id=(x ^ fx, y ^ fy, c ^ fc), device_id_type=MESH).wait_recv()
        for cp in copies:
            cp.wait_send()
        acc = all_ref[0]
        for d in range(1, N_DEV):
            acc = acc + all_ref[d]
        sum_ref[...] = acc

    vmem = pl.BlockSpec(memory_space=pltpu.VMEM)
    return pl.pallas_call(
        body, name=name, in_specs=[vmem], out_specs=[vmem, vmem],
        out_shape=[jax.ShapeDtypeStruct((N_DEV, r, cols), F32), jax.ShapeDtypeStruct((r, cols), F32)],
        scratch_shapes=[pltpu.SemaphoreType.DMA((N_DEV - 1,)), pltpu.SemaphoreType.DMA((N_DEV - 1,))],
    )(block)


PACK = 1024


def _packed_rows(shape, width):
    size, last = int(np.prod(shape)), shape[-1]
    cols = last if last <= width else width
    assert size % cols == 0
    return size // cols, cols


def _pack(vals, width=PACK):
    rows = []
    for v in vals:
        n_rows, cols = _packed_rows(v.shape, width)
        rows.append(jnp.pad(v.reshape(n_rows, cols).astype(F32), ((0, 0), (0, width - cols))))
    buf = jnp.concatenate(rows, axis=0)
    return jnp.pad(buf, ((0, (-buf.shape[0]) % 8), (0, 0)))


def _unpack(buf, shapes, width=PACK):
    out, r = [], 0
    for shape in shapes:
        n_rows, cols = _packed_rows(shape, width)
        out.append(buf[r:r + n_rows, :cols].reshape(shape))
        r += n_rows
    return out


FFN_SPLIT = 2


def _ffn_hidden(name, n, wg, wu, tm=1024, carry=None):
    m, k = n.shape
    f = wg.shape[0]
    fb = f // FFN_SPLIT
    tm = _row_tile(m, tm)

    def core(n_ref, wg_ref, wu_ref, a_ref, b_ref, s_ref):
        nv = n_ref[...]
        a = _dot(nv, wg_ref[...], True)
        b = _dot(nv, wu_ref[...], True)
        a_ref[...] = a.astype(a_ref.dtype)
        b_ref[...] = b.astype(b_ref.dtype)
        s_ref[...] = (a * jax.nn.sigmoid(a) * b).astype(s_ref.dtype)

    w_spec = pl.BlockSpec((fb, k), lambda j, i: (j, 0))
    out_spec = pl.BlockSpec((tm, fb), lambda j, i: (i, j))
    return _call(name, core, (FFN_SPLIT, m // tm), [pl.BlockSpec((tm, k), lambda j, i: (i, 0)), w_spec, w_spec],
                 [out_spec] * 3, [jax.ShapeDtypeStruct((m, f), BF16)] * 3, [], [n, wg, wu], carry)


def _ffn_d_hidden(name, df, wd, a, b, tm=512):
    m, k = df.shape
    f = wd.shape[0]
    fb = f // FFN_SPLIT
    tm = _row_tile(m, tm)

    def body(df_ref, wd_ref, a_ref, b_ref, da_ref, db_ref):
        dfv = df_ref[...]
        for j in range(FFN_SPLIT):
            cols = slice(j * fb, (j + 1) * fb)
            ds = _dot(dfv, wd_ref[cols, :], True)
            av, bv = a_ref[:, cols].astype(F32), b_ref[:, cols].astype(F32)
            sig = jax.nn.sigmoid(av)
            da_ref[:, cols] = (ds * bv * sig * (1.0 + av * (1.0 - sig))).astype(da_ref.dtype)
            db_ref[:, cols] = (ds * av * sig).astype(db_ref.dtype)

    row = pl.BlockSpec((tm, f), lambda i: (i, 0))
    return pl.pallas_call(
        body, name=name, grid=(m // tm,),
        in_specs=[pl.BlockSpec((tm, k), lambda i: (i, 0)), _resident(wd), row, row],
        out_specs=[row, row], out_shape=[jax.ShapeDtypeStruct((m, f), BF16)] * 2,
        compiler_params=_cparams(1),
    )(df, wd, a, b)


def kernel(x, ffn1_norm_pre, ffn1_w_gate, ffn1_w_up, ffn1_w_down, ffn1_norm_post, mix_norm_pre, w_in, gate_bias, rel_table, w_attn_out, conv_glu_bias, conv_dw_w, conv_dw_b, conv_ln_g, conv_ln_b, conv_w_out, w_out, mix_norm_post, ffn2_norm_pre, ffn2_w_gate, ffn2_w_up, ffn2_w_down, ffn2_norm_post, loss_target, m_ffn1_norm_pre, m_ffn1_w_gate, m_ffn1_w_up, m_ffn1_w_down, m_ffn1_norm_post, m_mix_norm_pre, m_w_in, m_gate_bias, m_rel_table, m_w_attn_out, m_conv_glu_bias, m_conv_dw_w, m_conv_dw_b, m_conv_ln_g, m_conv_ln_b, m_conv_w_out, m_w_out, m_mix_norm_post, m_ffn2_norm_pre, m_ffn2_w_gate, m_ffn2_w_up, m_ffn2_w_down, m_ffn2_norm_post, v_ffn1_norm_pre, v_ffn1_w_gate, v_ffn1_w_up, v_ffn1_w_down, v_ffn1_norm_post, v_mix_norm_pre, v_w_in, v_gate_bias, v_rel_table, v_w_attn_out, v_conv_glu_bias, v_conv_dw_w, v_conv_dw_b, v_conv_ln_g, v_conv_ln_b, v_conv_w_out, v_w_out, v_mix_norm_post, v_ffn2_norm_pre, v_ffn2_w_gate, v_ffn2_w_up, v_ffn2_w_down, v_ffn2_norm_post):
    args = dict(locals())
    names = ['ffn1_norm_pre', 'ffn1_w_gate', 'ffn1_w_up', 'ffn1_w_down', 'ffn1_norm_post', 'mix_norm_pre', 'w_in',
             'gate_bias', 'rel_table', 'w_attn_out', 'conv_glu_bias', 'conv_dw_w', 'conv_dw_b', 'conv_ln_g',
             'conv_ln_b', 'conv_w_out', 'w_out', 'mix_norm_post', 'ffn2_norm_pre', 'ffn2_w_gate', 'ffn2_w_up',
             'ffn2_w_down', 'ffn2_norm_post']
    big = ['ffn1_w_gate', 'ffn1_w_up', 'ffn1_w_down', 'w_in', 'w_attn_out', 'conv_w_out', 'w_out', 'ffn2_w_gate',
           'ffn2_w_up', 'ffn2_w_down']
    small = [n for n in names if n not in big]

    xs, target = x[0], loss_target[0]
    t, d = xs.shape
    cx, cy = lax.axis_index("x"), lax.axis_index("y")
    chip = 2 * cx + cy

    dw_shard = conv_dw_w[0, :, 0, :]
    cshard = dw_shard.shape[1]
    dw_all, _ = _all_devices("gather_dw", _pack([dw_shard], width=cshard))
    dw_full = jnp.concatenate([dw_all[2 * j, :CONV_WIDTH, :cshard] for j in range(N_CHIPS)], axis=1)
    dw_full = jnp.pad(dw_full, ((0, CONV_HALO - CONV_WIDTH), (0, 0)))
    peers = [(1 - cx, cy), (cx, 1 - cy), (1 - cx, 1 - cy)]
    pos = jnp.stack([lax.axis_index("c"), chip] + [_chip_index(p) for p in peers]).astype(jnp.int32)
    transposed = ("ffn1_w_gate", "ffn1_w_up", "ffn2_w_gate", "ffn2_w_up")
    weight_of = lambda n: n[2:] if n[:2] in ("m_", "v_") else n
    shard = lambda n: jnp.transpose(args[n][0]) if weight_of(n) in transposed else args[n][0]
    unshard = lambda n, v: (jnp.transpose(v) if n in transposed else v)[None]
    own = {n: _cast_into("cast_" + n, pos, shard(n)) for n in big}
    gather = lambda *ns: _gather_carry([own[n] for n in ns])
    res_spec = [(d, F32), (d, F32), (d, BF16)]
    whole = lambda w: w.reshape(-1, w.shape[-1])

    table_pad = jnp.pad(rel_table[0], ((0, 0), (0, REL_PAD - rel_table.shape[2])))
    bias, (wg1,) = _bias_expand(table_pad, gather("ffn1_w_gate"))
    n1, (wu1,) = _rms_fwd("ffn1_pre", xs, ffn1_norm_pre, gather("ffn1_w_up"))
    (a1, b1, s1), (wd1, win) = _ffn_hidden(
        "ffn1_hidden", n1, whole(wg1), whole(wu1), carry=gather("ffn1_w_down", "w_in"))
    (f1, h1, u), (wg2, wao, wco, wout) = _mm_kblk(
        "ffn1_down", [(s1, whole(wd1)[None])], trans_w=False, epilogue=_ep_post_res_pre(0.5), rows=[xs],
        vecs=[ffn1_norm_post, mix_norm_pre], row_outs=res_spec,
        carry=gather("ffn2_w_gate", "w_attn_out", "conv_w_out", "w_out"))
    proj, (wu2, wd2) = _mm_nblk("mix_in", u, win, trans_w=False, out_blocked=False, out_dtype=BF16, tm=2048,
                                carry=gather("ffn2_w_up", "ffn2_w_down"))
    att, lse = _attn_fwd(proj, bias)
    cs, c_glu, z_conv = _conv_fwd(proj, conv_glu_bias, dw_full, conv_dw_b, conv_ln_g, conv_ln_b)
    y_a, y_b, merged = _mix_merge(att, cs, wao, wco, proj, gate_bias)
    (mo, h2, n2), _ = _mm_kblk(
        "mix_out", [(merged, wout)], trans_w=False, epilogue=_ep_post_res_pre(1.0), rows=[h1],
        vecs=[mix_norm_post, ffn2_norm_pre], row_outs=res_spec, tm=1024)
    (a2, b2, s2), _ = _ffn_hidden("ffn2_hidden", n2, whole(wg2), whole(wu2))
    g = {}
    (dy, df2, err2, g["ffn2_norm_post"]), _ = _mm_kblk(
        "ffn2_down", [(s2, whole(wd2)[None])], trans_w=False, epilogue=_ep_loss(0.5, d), rows=[h2, target],
        vecs=[ffn2_norm_post], row_outs=[(d, F32), (d, BF16)], vec_outs=[d, d])
    loss = lax.psum(0.5 * jnp.sum(err2) / d, ("x", "y", "c"))

    parts, landed = {}, {}

    def ffn_bwd(tag, df, n, a, b, s, wg, wu, wd, **epilogue):
        da, db = _ffn_d_hidden(tag + "_d_hidden", df, whole(wd), a, b)
        group = [tag + "_w_down", tag + "_w_gate", tag + "_w_up"]
        local = [_mm_tn_wide(tag + "_g_" + what, hidden, other, d, a_split=FFN_SPLIT).reshape(wd.shape)
                 for what, hidden, other in (("down", s, df), ("gate", da, n), ("up", db, n))]
        return _mm_kblk(tag + "_d_n", [(da, whole(wg)[None]), (db, whole(wu)[None])], trans_w=False, sub=512,
                        carry=pair_sums(tag, group, local), **epilogue), group

    def pair_sums(tag, group, local):
        theirs = _pair_exchange("pair_" + tag, local)
        for n, mine, other in zip(group, local, theirs):
            parts[n] = _add_pair("pair_sum_" + n, pos, mine, other)
        return _to_owner_carry([parts[n] for n in group])

    def keep(group, carried):
        for n, val in zip(group, carried):
            landed[n] = val

    ((dh2, dmo, g["ffn2_norm_pre"], g["mix_norm_post"]), carried), group = ffn_bwd(
        "ffn2", df2, n2, a2, b2, s2, wg2, wu2, wd2, epilogue=_ep_pre_bwd_post(1.0), rows=[h2, dy, mo],
        vecs=[ffn2_norm_pre, mix_norm_post], row_outs=[(d, F32), (d, BF16)], vec_outs=[d, d])
    keep(group, carried)
    g_wout = _mm_tn("mix_g_out", merged, "col", dmo, "full")
    dy_a, dy_b, dgates, datt, dcs, g["gate_bias"] = _mix_d_merge(dmo, wout, y_a, y_b, wao, wco, proj, gate_bias)
    g_wao = _mm_tn("attn_g_out", att, "full", dy_a, "col")
    g_wco = _mm_tn("conv_g_out", cs, "full", dy_b, "col")
    dq, dk, dv, dbias = _attn_bwd(proj, bias, att, lse, datt)
    g["rel_table"] = _bias_fold(dbias)[:, :rel_table.shape[2]]
    dcin, g_dw, g["conv_dw_b"], g["conv_ln_g"], g["conv_ln_b"], g["conv_glu_bias"] = _conv_bwd(
        proj, c_glu, z_conv, dcs, conv_glu_bias, dw_full, conv_ln_g, conv_ln_b)
    pieces = [("q", dq), ("k", dk), ("v", dv), ("conv", dcin), ("gates", dgates)]
    n_in = win.shape[0] * win.shape[2]
    win_cols = jnp.transpose(jnp.transpose(win, (1, 0, 2)).reshape(d, n_in // COL, COL), (1, 0, 2))
    g_cols = jnp.concatenate([_mm_tn_wide("mix_g_in_" + tag, u, piece, COL, tt=2048) for tag, piece in pieces],
                             axis=0)
    g_win = jnp.transpose(jnp.transpose(g_cols, (1, 0, 2)).reshape(d, win.shape[0], win.shape[2]), (1, 0, 2))
    bounds = np.cumsum([0] + [piece.shape[1] // COL for _, piece in pieces])
    group = ["w_out", "w_attn_out", "conv_w_out", "w_in"]
    (dh1, df1, g["mix_norm_pre"], g["ffn1_norm_post"]), carried = _mm_kblk(
        "mix_d_in", [(piece, win_cols[lo:hi]) for (_, piece), lo, hi in zip(pieces, bounds[:-1], bounds[1:])],
        trans_w=True, epilogue=_ep_pre_bwd_post(0.5), rows=[h1, dh2, f1],
        vecs=[mix_norm_pre, ffn1_norm_post], row_outs=[(d, F32), (d, BF16)], vec_outs=[d, d],
        carry=pair_sums("mix", group, [g_wout, g_wao, g_wco, g_win]))
    keep(group, carried)
    ((grad_x, g["ffn1_norm_pre"]), carried), group = ffn_bwd(
        "ffn1", df1, n1, a1, b1, s1, wg1, wu1, wd1, epilogue=_ep_pre_bwd_first(), rows=[xs, dh1],
        vecs=[ffn1_norm_pre], row_outs=[(d, F32)], vec_outs=[d])
    keep(group, carried)

    halves = [_add_chips("chip_sum_" + n, pos, parts[n], landed[n]) for n in big]
    other_halves = _swap_halves(halves)

    g["conv_dw_w"] = g_dw[:CONV_WIDTH]
    _, small_sum = _all_devices("sum_small", _pack([g[n] for n in small]))
    for n, val in zip(small, _unpack(small_sum, [g[n].shape for n in small])):
        g[n] = val
    g["conv_dw_w"] = lax.dynamic_slice_in_dim(g["conv_dw_w"], chip * cshard, cshard, axis=1)

    grads, deltas, new_m, new_v = {}, {}, {}, {}
    for n, mine, other in zip(big, halves, other_halves):
        gr, dl, m2, v2 = _adamw_halves("adamw_" + n, pos, shard(n), shard("m_" + n), shard("v_" + n), mine, other)
        grads[n], deltas[n], new_m[n], new_v[n] = unshard(n, gr), unshard(n, dl), unshard(n, m2), unshard(n, v2)
    shapes = [g[n].shape for n in small]
    packed = lambda pre: _pack([args[pre + n].reshape(shp) for n, shp in zip(small, shapes)])
    dl, m2, v2 = _adamw("adamw_small", packed(""), _pack([g[n] for n in small]), packed("m_"), packed("v_"))
    for n, a_, b_, c_ in zip(small, _unpack(dl, shapes), _unpack(m2, shapes), _unpack(v2, shapes)):
        shape = args[n].shape
        grads[n], deltas[n], new_m[n], new_v[n] = (g[n].reshape(shape), a_.reshape(shape), b_.reshape(shape),
                                                   c_.reshape(shape))

    return (loss, grad_x[None], *[grads[n] for n in names], *[deltas[n] for n in names],
            *[new_m[n] for n in names], *[new_v[n] for n in names])
```

```python
import functools

import numpy as np
import jax
import jax.numpy as jnp
from jax import lax
from jax.experimental import pallas as pl
from jax.experimental.pallas import tpu as pltpu

F32 = jnp.float32
BF16 = jnp.bfloat16
MESH = pl.DeviceIdType.MESH
ANY = pl.BlockSpec(memory_space=pl.ANY)

EPS = 1e-6
CHUNK = 64
LEFT_CHUNKS = 8
N_HEADS = 8
HEAD_DIM = 64
D_ATTN = N_HEADS * HEAD_DIM
D_CONV = 512
CONV_WIDTH = 31
REL_CLIP = 128
N_CHIPS = 4
N_DEV = 8
Q_BLOCK = 4 * CHUNK
K_PAD = LEFT_CHUNKS * CHUNK
K_WIN = K_PAD + Q_BLOCK
REL_EXT = 1024
REL_PAD = 384
CONV_HALO = 32
CONV_TILE = 512
COL = 512
NEG = -1e30

ADAM_LR = 0.001
ADAM_B1 = 0.9
ADAM_B2 = 0.999
ADAM_EPS = 1e-08
ADAM_WD = 0.01
ADAM_STEP = 10

VMEM_LIMIT_BYTES = 56 * 1024 * 1024


def _cparams(n_grid):
    return pltpu.CompilerParams(dimension_semantics=("arbitrary",) * n_grid, vmem_limit_bytes=VMEM_LIMIT_BYTES)


def _row_tile(rows, want):
    if rows <= want:
        return rows
    for t in range(want - want % 16, 0, -16):
        if rows % t == 0:
            return t
    raise ValueError((rows, want))


def _dot(a, w, trans_w):
    dims = (((1,), (1,)), ((), ())) if trans_w else (((1,), (0,)), ((), ()))
    return lax.dot_general(a, w, dims, preferred_element_type=F32)


class _Carry:
    LATE_STEPS = 2

    def __init__(self, ins, outs, aliases, sems, phases):
        self.ins, self.outs, self.aliases, self.sems, self.phases = ins, outs, aliases, sems, phases


def _call(name, core, grid, in_specs, out_specs, out_shape, scratch, args, carry=None):
    n_in, n_out, n_scr = len(in_specs), len(out_specs), len(scratch)
    if carry is None:
        out = pl.pallas_call(core, name=name, grid=grid, in_specs=in_specs, out_specs=out_specs, out_shape=out_shape,
                             scratch_shapes=scratch, compiler_params=_cparams(len(grid)))(*args)
        return list(out), []
    c_in, c_out = len(carry.ins), len(carry.outs)
    total = int(np.prod(grid))
    late = max(total - 1 - _Carry.LATE_STEPS, 0)

    def body(*refs):
        ins, refs = refs[:n_in], refs[n_in:]
        c_ins, refs = refs[:c_in], refs[c_in:]
        outs, refs = refs[:n_out], refs[n_out:]
        c_outs, refs = refs[:c_out], refs[c_out:]
        scr, c_sems = refs[:n_scr], refs[n_scr:]
        step = pl.program_id(0)
        for axis in range(1, len(grid)):
            step = step * grid[axis] + pl.program_id(axis)

        def run(when, at):
            for w, fn in carry.phases:
                if w == when:
                    pl.when(step == at)(functools.partial(fn, c_ins, c_outs, c_sems))

        run("first", 0)
        core(*ins, *outs, *scr)
        run("late", late)
        run("last", total - 1)

    out = pl.pallas_call(
        body, name=name, grid=grid, in_specs=list(in_specs) + [ANY] * c_in, out_specs=list(out_specs) + [ANY] * c_out,
        out_shape=list(out_shape) + list(carry.outs), scratch_shapes=list(scratch) + list(carry.sems),
        input_output_aliases={n_in + a: n_out + b for a, b in carry.aliases.items()},
        compiler_params=_cparams(len(grid)),
    )(*args, *carry.ins)
    return list(out[:n_out]), list(out[n_out:])


def _mm_nblk(name, a, w, *, trans_w, out_blocked, out_dtype, tm=1024, carry=None):
    m, k = a.shape
    nj = w.shape[0]
    nb = w.shape[1] if trans_w else w.shape[2]
    tm = _row_tile(m, tm)

    def core(a_ref, w_ref, o_ref):
        o_ref[...] = _dot(a_ref[...], w_ref[...], trans_w).astype(o_ref.dtype)

    if out_blocked:
        out_shape, out_spec = (nj, m, nb), pl.BlockSpec((None, tm, nb), lambda j, i: (j, i, 0))
    else:
        out_shape, out_spec = (m, nj * nb), pl.BlockSpec((tm, nb), lambda j, i: (i, j))
    out, carried = _call(
        name, core, (nj, m // tm),
        [pl.BlockSpec((tm, k), lambda j, i: (i, 0)), pl.BlockSpec((None,) + w.shape[1:], lambda j, i: (j, 0, 0))],
        [out_spec], [jax.ShapeDtypeStruct(out_shape, out_dtype)], [], [a, w], carry)
    return out[0] if carry is None else (out[0], carried)


def _mm_kblk(name, pairs, *, trans_w, out_dtype=F32, tm=512, sub=256, epilogue=None, rows=(), vecs=(), row_outs=None,
             vec_outs=(), carry=None):
    w0 = pairs[0][1]
    n = w0.shape[1] if trans_w else w0.shape[2]
    blocks = [(w.shape[0], w.shape[2] if trans_w else w.shape[1]) for _, w in pairs]
    m = pairs[0][0].shape[-2]
    tm = _row_tile(m, tm)
    ts = _row_tile(tm, sub)
    n_pairs, n_rows, n_vecs = len(pairs), len(rows), len(vecs)
    if epilogue is None:
        epilogue, row_outs = (lambda acc, r, v: ([acc], [])), [(n, out_dtype)]
    n_ro, n_vo = len(row_outs), len(vec_outs)

    def core(*refs):
        pair_refs, refs = refs[:2 * n_pairs], refs[2 * n_pairs:]
        row_refs, refs = refs[:n_rows], refs[n_rows:]
        vec_refs, refs = refs[:n_vecs], refs[n_vecs:]
        ro_refs, vo_refs = refs[:n_ro], refs[n_ro:]
        if n_vo:
            @pl.when(pl.program_id(0) == 0)
            def _():
                for ref in vo_refs:
                    ref[...] = jnp.zeros_like(ref)

        vec_vals = [v[...] for v in vec_refs]
        sums = None
        for r0 in range(0, tm, ts):
            sub_rows = slice(r0, r0 + ts)
            acc = None
            for p in range(n_pairs):
                a_ref, w_ref = pair_refs[2 * p], pair_refs[2 * p + 1]
                nj, kb = blocks[p]
                for j in range(nj):
                    a_blk = a_ref[j, sub_rows, :] if len(a_ref.shape) == 3 else a_ref[sub_rows, j * kb:(j + 1) * kb]
                    part = _dot(a_blk, w_ref[j], trans_w)
                    acc = part if acc is None else acc + part
            ro, vo = epilogue(acc, [r[sub_rows, :] for r in row_refs], vec_vals)
            for ref, val in zip(ro_refs, ro):
                ref[sub_rows, :] = val.astype(ref.dtype)
            sums = vo if sums is None else [s + v for s, v in zip(sums, vo)]
        for ref, val in zip(vo_refs, sums or []):
            ref[...] += val

    in_specs, args = [], []
    for (a, w), (nj, kb) in zip(pairs, blocks):
        if a.ndim == 3:
            in_specs.append(pl.BlockSpec((nj, tm, kb), lambda i: (0, i, 0)))
        else:
            in_specs.append(pl.BlockSpec((tm, nj * kb), lambda i: (i, 0)))
        in_specs.append(pl.BlockSpec(w.shape, lambda i: (0, 0, 0), pipeline_mode=pl.Buffered(1)))
        args += [a, w]
    in_specs += [pl.BlockSpec((tm, r.shape[1]), lambda i: (i, 0)) for r in rows]
    in_specs += [pl.BlockSpec(v.shape, lambda i: (0, 0)) for v in vecs]
    out_specs = [pl.BlockSpec((tm, cols), lambda i: (i, 0)) for cols, _ in row_outs]
    out_specs += [pl.BlockSpec((1, cols), lambda i: (0, 0)) for cols in vec_outs]
    out_shape = [jax.ShapeDtypeStruct((m, cols), dt) for cols, dt in row_outs]
    out_shape += [jax.ShapeDtypeStruct((1, cols), F32) for cols in vec_outs]
    return _call(name, core, (m // tm,), in_specs, out_specs, out_shape, [], args + list(rows) + list(vecs), carry)


def _mm_tn(name, a, a_mode, b, b_mode, *, out_dtype=BF16, tt=2048):
    nj = N_CHIPS
    t = a.shape[-2]
    tt = _row_tile(t, tt)

    def spec(x, mode):
        if mode == "full":
            return x.shape[1], pl.BlockSpec((tt, x.shape[1]), lambda j, s: (s, 0))
        if mode == "col":
            cb = x.shape[1] // nj
            return cb, pl.BlockSpec((tt, cb), lambda j, s: (s, j))
        return x.shape[2], pl.BlockSpec((None, tt, x.shape[2]), lambda j, s: (j, s, 0))

    ca, a_spec = spec(a, a_mode)
    cb, b_spec = spec(b, b_mode)
    n_steps = t // tt

    def body(a_ref, b_ref, o_ref, acc_ref):
        s = pl.program_id(1)

        @pl.when(s == 0)
        def _():
            acc_ref[...] = jnp.zeros_like(acc_ref)

        acc_ref[...] += lax.dot_general(a_ref[...], b_ref[...], (((0,), (0,)), ((), ())),
                                        preferred_element_type=F32)

        @pl.when(s == n_steps - 1)
        def _():
            o_ref[...] = acc_ref[...].astype(o_ref.dtype)

    return pl.pallas_call(
        body, name=name, grid=(nj, n_steps), in_specs=[a_spec, b_spec],
        out_specs=pl.BlockSpec((None, ca, cb), lambda j, s: (j, 0, 0)),
        out_shape=jax.ShapeDtypeStruct((nj, ca, cb), out_dtype),
        scratch_shapes=[pltpu.VMEM((ca, cb), F32)], compiler_params=_cparams(2),
    )(a, b)


def _mm_tn_wide(name, a, b, cb, *, a_split=1, out_dtype=BF16, tt=1024):
    t, ca = a.shape
    nb = b.shape[1] // cb
    tt = _row_tile(t, tt)
    n_steps = t // tt
    piece = ca // a_split

    def body(a_ref, b_ref, o_ref, acc_ref):
        s = pl.program_id(0)

        @pl.when(s == 0)
        def _():
            acc_ref[...] = jnp.zeros_like(acc_ref)

        for j in range(nb):
            bv = b_ref[:, j * cb:(j + 1) * cb]
            for c in range(a_split):
                rows = slice(c * piece, (c + 1) * piece)
                acc_ref[j, rows, :] += lax.dot_general(a_ref[:, rows], bv, (((0,), (0,)), ((), ())),
                                                       preferred_element_type=F32)

        @pl.when(s == n_steps - 1)
        def _():
            o_ref[...] = acc_ref[...].astype(o_ref.dtype)

    return pl.pallas_call(
        body, name=name, grid=(n_steps,),
        in_specs=[pl.BlockSpec((tt, ca), lambda s: (s, 0)), pl.BlockSpec((tt, nb * cb), lambda s: (s, 0))],
        out_specs=pl.BlockSpec((nb, ca, cb), lambda s: (0, 0, 0)),
        out_shape=jax.ShapeDtypeStruct((nb, ca, cb), out_dtype),
        scratch_shapes=[pltpu.VMEM((nb, ca, cb), F32)], compiler_params=_cparams(1),
    )(a, b)


def _rowwise(name, fn, rows, vecs, row_outs, vec_outs, *, tm=256, carry=None):
    nrows = rows[0][0].shape[0]
    tm = _row_tile(nrows, tm)
    n_r, n_v, n_ro, n_vo = len(rows), len(vecs), len(row_outs), len(vec_outs)

    def body(*refs):
        r_vals = [r[...] for r in refs[:n_r]]
        v_vals = [r[...] for r in refs[n_r:n_r + n_v]]
        ro_refs = refs[n_r + n_v:n_r + n_v + n_ro]
        vo_refs = refs[n_r + n_v + n_ro:]
        ro, vo = fn(r_vals, v_vals)
        for ref, val in zip(ro_refs, ro):
            ref[...] = val.astype(ref.dtype)
        if n_vo:
            @pl.when(pl.program_id(0) == 0)
            def _():
                for ref in vo_refs:
                    ref[...] = jnp.zeros_like(ref)

            for ref, val in zip(vo_refs, vo):
                ref[...] += val

    in_specs = [pl.BlockSpec((tm, cols), functools.partial(lambda i, cb: (i, cb), cb=cb)) for _, cols, cb in rows]
    in_specs += [pl.BlockSpec(v.shape, functools.partial(lambda i, nd: (0,) * nd, nd=v.ndim)) for v in vecs]
    out_specs = [pl.BlockSpec((tm, cols), lambda i: (i, 0)) for cols, _ in row_outs]
    out_specs += [pl.BlockSpec((1, cols), lambda i: (0, 0)) for cols in vec_outs]
    out_shape = [jax.ShapeDtypeStruct((nrows, cols), dt) for cols, dt in row_outs]
    out_shape += [jax.ShapeDtypeStruct((1, cols), F32) for cols in vec_outs]
    out, carried = _call(name, body, (nrows // tm,), in_specs, out_specs, out_shape, [],
                         [r[0] for r in rows] + list(vecs), carry)
    return out if carry is None else (out, carried)


def _whole(x):
    return (x, x.shape[1], 0)


def _colsum(x):
    return jnp.sum(x, axis=0, keepdims=True)


def _rstd(x):
    return lax.rsqrt(jnp.mean(x * x, axis=-1, keepdims=True) + EPS)


def _rms_bwd(dn, x, g):
    r = _rstd(x)
    c = dn * g
    dx = r * c - x * (r * r * r) * jnp.mean(c * x, axis=-1, keepdims=True)
    return dx, _colsum(dn * x * r)


def _rms_fwd(name, x, g, carry):
    def fn(r, v):
        (xv,), (gv,) = r, v
        return [xv * _rstd(xv) * gv], []

    (n,), carried = _rowwise(name, fn, [_whole(x)], [g], [(x.shape[1], BF16)], [], carry=carry)
    return n, carried


def _ep_post_res_pre(scale):
    def epilogue(acc, rows, vecs):
        (resid,), (g_post, g_next) = rows, vecs
        h = resid + scale * (acc * _rstd(acc) * g_post)
        return [acc, h, h * _rstd(h) * g_next], []

    return epilogue


def _post_bwd(dh, f, g_post, scale):
    return _rms_bwd(scale * dh, f, g_post)


def _ep_loss(scale, d):
    def epilogue(acc, rows, vecs):
        (resid, target), (g_post,) = rows, vecs
        err = resid + scale * (acc * _rstd(acc) * g_post) - target
        dy = err * (1.0 / d)
        df, dg_post = _post_bwd(dy, acc, g_post, scale)
        return [dy, df], [_colsum(err * err), dg_post]

    return epilogue


def _ep_pre_bwd_post(scale_prev):
    def epilogue(acc, rows, vecs):
        (h, dh_up, f_prev), (g_pre, g_post_prev) = rows, vecs
        dx, dg_pre = _rms_bwd(acc, h, g_pre)
        dh = dh_up + dx
        df, dg_post = _post_bwd(dh, f_prev, g_post_prev, scale_prev)
        return [dh, df], [dg_pre, dg_post]

    return epilogue


def _ep_pre_bwd_first():
    def epilogue(acc, rows, vecs):
        (x, dh_up), (g_pre,) = rows, vecs
        dx, dg_pre = _rms_bwd(acc, x, g_pre)
        return [dh_up + dx], [dg_pre]

    return epilogue


def _gate_specs(d, tm):
    first = (3 * D_ATTN + 2 * D_CONV) // COL
    return [pl.BlockSpec((tm, COL), functools.partial(lambda i, cb: (i, cb), cb=first + p)) for p in range(2 * d // COL)]


def _gate(piece_refs, bias_ref, c0, width):
    p, off = divmod(c0, COL)
    return jax.nn.sigmoid(piece_refs[p][:, off:off + width].astype(F32) + bias_ref[:, c0:c0 + width])


def _resident(w):
    return pl.BlockSpec(w.shape, functools.partial(lambda i, nd: (0,) * nd, nd=w.ndim), pipeline_mode=pl.Buffered(1))


def _mix_merge(att, cs, wao, wco, proj, gate_bias, tm=1024):
    t = att.shape[0]
    nj, _, nb = wao.shape
    d = nj * nb
    tm = _row_tile(t, tm)
    gate_specs = _gate_specs(d, tm)
    n_p = len(gate_specs)

    def body(att_ref, cs_ref, wao_ref, wco_ref, *rest):
        pieces, (gb_ref, ya_ref, yb_ref, m_ref) = rest[:n_p], rest[n_p:]
        av, cv = att_ref[...], cs_ref[...]
        for j in range(nj):
            cols = slice(j * nb, (j + 1) * nb)
            ya = _dot(av, wao_ref[j], False)
            yb = _dot(cv, wco_ref[j], False)
            merged = _gate(pieces, gb_ref, j * nb, nb) * ya + _gate(pieces, gb_ref, d + j * nb, nb) * yb
            ya_ref[:, cols] = ya.astype(ya_ref.dtype)
            yb_ref[:, cols] = yb.astype(yb_ref.dtype)
            m_ref[:, cols] = merged.astype(m_ref.dtype)

    row = lambda x: pl.BlockSpec((tm, x.shape[1]), lambda i: (i, 0))
    out_spec = pl.BlockSpec((tm, d), lambda i: (i, 0))
    return pl.pallas_call(
        body, name="mix_merge", grid=(t // tm,),
        in_specs=[row(att), row(cs), _resident(wao), _resident(wco)] + gate_specs + [_resident(gate_bias)],
        out_specs=[out_spec] * 3, out_shape=[jax.ShapeDtypeStruct((t, d), BF16)] * 3, compiler_params=_cparams(1),
    )(att, cs, wao, wco, *([proj] * n_p), gate_bias)


def _mix_d_merge(dmo, wout, y_a, y_b, wao, wco, proj, gate_bias, tm=512):
    t, d = dmo.shape
    nj, _, nb = wao.shape
    ka, kc = wao.shape[1], wco.shape[1]
    tm = _row_tile(t, tm)
    gate_specs = _gate_specs(d, tm)
    n_p = len(gate_specs)

    def body(dmo_ref, wout_ref, ya_ref, yb_ref, wao_ref, wco_ref, *rest):
        pieces, (gb_ref, dya_ref, dyb_ref, dg_ref, datt_ref, dcs_ref, dgb_ref) = rest[:n_p], rest[n_p:]

        @pl.when(pl.program_id(0) == 0)
        def _():
            dgb_ref[...] = jnp.zeros_like(dgb_ref)

        dmo_v = dmo_ref[...]
        datt = dcs = None
        for j in range(nj):
            cols, cols_b = slice(j * nb, (j + 1) * nb), slice(d + j * nb, d + (j + 1) * nb)
            dm = _dot(dmo_v, wout_ref[j], True)
            ga, gb = _gate(pieces, gb_ref, j * nb, nb), _gate(pieces, gb_ref, d + j * nb, nb)
            dya, dyb = (dm * ga).astype(BF16), (dm * gb).astype(BF16)
            dga = dm * ya_ref[:, cols].astype(F32) * ga * (1.0 - ga)
            dgb = dm * yb_ref[:, cols].astype(F32) * gb * (1.0 - gb)
            dya_ref[:, cols], dyb_ref[:, cols] = dya, dyb
            dg_ref[:, cols], dg_ref[:, cols_b] = dga.astype(dg_ref.dtype), dgb.astype(dg_ref.dtype)
            dgb_ref[:, cols] += _colsum(dga)
            dgb_ref[:, cols_b] += _colsum(dgb)
            pa, pc = _dot(dya, wao_ref[j], True), _dot(dyb, wco_ref[j], True)
            datt, dcs = (pa, pc) if datt is None else (datt + pa, dcs + pc)
        datt_ref[...] = datt.astype(datt_ref.dtype)
        dcs_ref[...] = dcs.astype(dcs_ref.dtype)

    row = lambda cols: pl.BlockSpec((tm, cols), lambda i: (i, 0))
    return pl.pallas_call(
        body, name="mix_d_merge", grid=(t // tm,),
        in_specs=[row(d), _resident(wout), row(d), row(d), _resident(wao), _resident(wco)] + gate_specs
        + [_resident(gate_bias)],
        out_specs=[row(d), row(d), row(2 * d), row(ka), row(kc), pl.BlockSpec((1, 2 * d), lambda i: (0, 0))],
        out_shape=[jax.ShapeDtypeStruct((t, d), BF16), jax.ShapeDtypeStruct((t, d), BF16),
                   jax.ShapeDtypeStruct((t, 2 * d), BF16), jax.ShapeDtypeStruct((t, ka), BF16),
                   jax.ShapeDtypeStruct((t, kc), F32), jax.ShapeDtypeStruct((1, 2 * d), F32)],
        compiler_params=_cparams(1),
    )(dmo, wout, y_a, y_b, wao, wco, *([proj] * n_p), gate_bias)


def _adamw_math(wv, gv, mv, vv):
    m2 = ADAM_B1 * mv + (1.0 - ADAM_B1) * gv
    v2 = ADAM_B2 * vv + (1.0 - ADAM_B2) * (gv * gv)
    m_hat = m2 / (1.0 - ADAM_B1 ** ADAM_STEP)
    v_hat = v2 / (1.0 - ADAM_B2 ** ADAM_STEP)
    delta = -ADAM_LR * (m_hat / (jnp.sqrt(v_hat) + ADAM_EPS) + ADAM_WD * wv)
    return delta, m2, v2


def _adamw(name, w, g, m, v):
    def fn(r, _):
        return list(_adamw_math(*r)), []

    c = w.shape[1]
    return _rowwise(name, fn, [_whole(w), _whole(g), _whole(m), _whole(v)], [], [(c, F32)] * 3, [], tm=256)


POS_C, POS_CHIP, POS_PEER = 0, 1, 2


def _placed_call(body, name, pos, grid, in_specs, out_specs, out_shape, args):
    return pl.pallas_call(
        body, name=name, out_shape=out_shape, compiler_params=_cparams(len(grid)),
        grid_spec=pltpu.PrefetchScalarGridSpec(num_scalar_prefetch=1, grid=grid, in_specs=in_specs,
                                               out_specs=out_specs),
    )(pos, *args)


def _cast_into(name, pos, w):
    r, cols = w.shape
    tm = _row_tile(r, 1024)

    def body(pos_ref, w_ref, o_ref):
        o_ref[...] = w_ref[...].astype(o_ref.dtype)

    return _placed_call(
        body, name, pos, (r // tm,), [pl.BlockSpec((tm, cols), lambda i, pos: (i, 0))],
        pl.BlockSpec((None, tm, cols), lambda i, pos: (pos[POS_CHIP], i, 0)),
        jax.ShapeDtypeStruct((N_CHIPS, r, cols), BF16), [w])


def _add_pair(name, pos, grad, landed):
    nj, half, cols = landed.shape
    tm = _row_tile(half, 512)
    nb = half // tm

    def body(pos_ref, g_ref, l_ref, o_ref):
        o_ref[...] = (g_ref[...].astype(F32) + l_ref[...].astype(F32)).astype(o_ref.dtype)

    spec = pl.BlockSpec((None, tm, cols), lambda j, i, pos: (j, i, 0))
    return _placed_call(
        body, name, pos, (nj, nb),
        [pl.BlockSpec((None, tm, cols), lambda j, i, pos: (j, pos[POS_C] * nb + i, 0)), spec], spec,
        jax.ShapeDtypeStruct(landed.shape, BF16), [grad, landed])


def _add_chips(name, pos, part, landed):
    _, half, cols = landed.shape
    tm = _row_tile(half, 512)

    def body(pos_ref, p_ref, l0_ref, l1_ref, l2_ref, o_ref):
        acc = p_ref[...].astype(F32)
        for ref in (l0_ref, l1_ref, l2_ref):
            acc = acc + ref[...].astype(F32)
        o_ref[...] = acc

    slot = lambda at: pl.BlockSpec((None, tm, cols), functools.partial(lambda i, pos, at: (pos[at], i, 0), at=at))
    return _placed_call(
        body, name, pos, (half // tm,), [slot(POS_CHIP)] + [slot(POS_PEER + k) for k in range(3)],
        pl.BlockSpec((tm, cols), lambda i, pos: (i, 0)), jax.ShapeDtypeStruct((half, cols), F32),
        [part, landed, landed, landed])


def _adamw_halves(name, pos, w, m, v, own, landed):
    r, cols = w.shape
    half = own.shape[0]
    tm = _row_tile(half, 384 * 1024 // cols)
    nb = half // tm

    def body(pos_ref, w_ref, m_ref, v_ref, own_ref, land_ref, g_out, d_out, m_out, v_out):
        mine = pl.program_id(0) == pos_ref[POS_C]
        g = jnp.where(mine, own_ref[...], land_ref[...])
        delta, m2, v2 = _adamw_math(w_ref[...], g, m_ref[...], v_ref[...])
        g_out[...] = g
        d_out[...] = delta
        m_out[...] = m2
        v_out[...] = v2

    full = pl.BlockSpec((tm, cols), lambda h, i, pos: (h * nb + i, 0))
    used = lambda h, i, pos: (jnp.where(h == pos[POS_C], i, 0), 0)
    unused = lambda h, i, pos: (jnp.where(h == pos[POS_C], 0, i), 0)
    return _placed_call(
        body, name, pos, (2, nb), [full, full, full, pl.BlockSpec((tm, cols), used), pl.BlockSpec((tm, cols), unused)],
        [full] * 4,
        [jax.ShapeDtypeStruct((r, cols), F32)] * 4, [w, m, v, own, landed])


N_START = K_PAD // Q_BLOCK


def _rel_onehot(n_q):
    e = np.arange(REL_EXT)
    dist = K_PAD - (e - (n_q - 1))
    idx = np.clip(dist, -REL_CLIP, REL_CLIP) + REL_CLIP
    return (np.arange(REL_PAD)[:, None] == idx[None, :]).astype(np.float32)


def _skew(x, left):
    row = lax.broadcasted_iota(jnp.int32, x.shape, 0)
    for bit in range(x.shape[0].bit_length() - 1):
        amount = 1 << bit
        rolled = pltpu.roll(x, REL_EXT - amount if left else amount, 1)
        x = jnp.where((row >> bit) & 1 == 1, rolled, x)
    return x


def _bias_expand(table_pad, carry):
    onehot = jnp.asarray(_rel_onehot(Q_BLOCK))

    def core(t_ref, oh_ref, o_ref):
        ext = jnp.dot(t_ref[...], oh_ref[...], precision=lax.Precision.HIGHEST, preferred_element_type=F32)
        qc = lax.broadcasted_iota(jnp.int32, (Q_BLOCK, K_WIN), 0) // CHUNK
        kpos = lax.broadcasted_iota(jnp.int32, (Q_BLOCK, K_WIN), 1)
        band = (kpos // CHUNK >= qc) & (kpos // CHUNK <= qc + LEFT_CHUNKS)
        rows = jnp.broadcast_to(ext, (Q_BLOCK, REL_EXT))
        rolled = _skew(pltpu.roll(rows, REL_EXT - (Q_BLOCK - 1), 1), left=False)[:, :K_WIN]
        for v in range(N_START + 1):
            o_ref[v] = jnp.where(band & (kpos + v * Q_BLOCK >= K_PAD), rolled, NEG)

    (bias,), carried = _call(
        "bias_expand", core, (N_HEADS,),
        [pl.BlockSpec((None, 1, REL_PAD), lambda h: (h, 0, 0)), pl.BlockSpec(onehot.shape, lambda h: (0, 0))],
        [pl.BlockSpec((N_START + 1, None, Q_BLOCK, K_WIN), lambda h: (0, h, 0, 0))],
        [jax.ShapeDtypeStruct((N_START + 1, N_HEADS, Q_BLOCK, K_WIN), F32)], [],
        [table_pad.reshape(N_HEADS, 1, REL_PAD), onehot], carry)
    return bias, carried


def _bias_fold(dbias):
    onehot_t = jnp.asarray(_rel_onehot(CHUNK).T)

    def body(d_ref, oh_ref, o_ref, ext_ref):
        for h in range(N_HEADS):
            x = jnp.concatenate([d_ref[h], jnp.zeros((CHUNK, REL_EXT - K_WIN), F32)], axis=1)
            rolled = _skew(pltpu.roll(x, CHUNK - 1, 1), left=True)
            ext_ref[h:h + 1, :] = jnp.sum(rolled, axis=0, keepdims=True)
        o_ref[...] = jnp.dot(ext_ref[...], oh_ref[...], precision=lax.Precision.HIGHEST,
                             preferred_element_type=F32)

    return pl.pallas_call(
        body, name="bias_fold", out_shape=jax.ShapeDtypeStruct((N_HEADS, REL_PAD), F32),
        scratch_shapes=[pltpu.VMEM((N_HEADS, REL_EXT), F32)],
        compiler_params=pltpu.CompilerParams(vmem_limit_bytes=VMEM_LIMIT_BYTES),
    )(dbias, onehot_t)


def _head_lanes():
    lane = lax.broadcasted_iota(jnp.int32, (1, 2 * HEAD_DIM), 1)
    return [lane < HEAD_DIM, lane >= HEAD_DIM]


def _only(mask, x, scale=None):
    x = jnp.where(mask, x, jnp.zeros_like(x))
    return x if scale is None else x * scale


def _contract_lanes(a, b):
    return lax.dot_general(a, b, (((1,), (1,)), ((), ())), preferred_element_type=F32)


def _contract_rows(a, b):
    return lax.dot_general(a, b, (((0,), (0,)), ((), ())), preferred_element_type=F32)


PAIR = 2 * HEAD_DIM
N_PAIRS = D_ATTN // PAIR


def _attn_specs(pairs):
    width = pairs * PAIR
    per = D_ATTN // width
    row_spec = pl.BlockSpec((Q_BLOCK, width), lambda g, i: (i, g))
    kv_specs = [pl.BlockSpec((Q_BLOCK, width),
                             functools.partial(lambda g, i, kk, c0: (jnp.maximum(i + kk - N_START, 0), c0 + g),
                                               kk=kk, c0=c0))
                for c0 in (per, 2 * per) for kk in range(K_WIN // Q_BLOCK)]
    bias_spec = pl.BlockSpec((None, 2 * pairs, Q_BLOCK, K_WIN), lambda g, i: (jnp.minimum(i, N_START), g, 0, 0))
    return row_spec, kv_specs, bias_spec


def _attn_fwd(proj, bias, pairs=N_PAIRS):
    t = proj.shape[0]
    n_win = K_WIN // Q_BLOCK

    def body(q_ref, *refs):
        k_refs, v_refs = refs[:n_win], refs[n_win:2 * n_win]
        b_ref, o_ref, lse_ref = refs[2 * n_win:]
        for pp in range(pairs):
            cols = slice(pp * PAIR, (pp + 1) * PAIR)
            k = jnp.concatenate([r[:, cols] for r in k_refs], axis=0)
            v = jnp.concatenate([r[:, cols] for r in v_refs], axis=0)
            q = q_ref[:, cols]
            o = lse = None
            for hh, lanes in enumerate(_head_lanes()):
                s = _contract_lanes(_only(lanes, q, HEAD_DIM ** -0.5), k) + b_ref[2 * pp + hh]
                m = jnp.max(s, axis=1, keepdims=True)
                p = jnp.exp(s - m)
                l = jnp.sum(p, axis=1, keepdims=True)
                oh = jnp.dot(p.astype(BF16), v, preferred_element_type=F32) / l
                lse_h = jnp.broadcast_to(m + jnp.log(l), oh.shape)
                o, lse = (oh, lse_h) if o is None else (jnp.where(lanes, oh, o), jnp.where(lanes, lse_h, lse))
            o_ref[:, cols] = o.astype(o_ref.dtype)
            lse_ref[:, cols] = lse

    row_spec, kv_specs, bias_spec = _attn_specs(pairs)
    return pl.pallas_call(
        body, name="attn_fwd", grid=(N_PAIRS // pairs, t // Q_BLOCK),
        in_specs=[row_spec] + kv_specs + [bias_spec], out_specs=[row_spec, row_spec],
        out_shape=[jax.ShapeDtypeStruct((t, D_ATTN), BF16), jax.ShapeDtypeStruct((t, D_ATTN), F32)],
        compiler_params=_cparams(2),
    )(*([proj] * (1 + 2 * n_win)), bias)


def _attn_bwd(proj, bias, att, lse, datt, pairs=2):
    t = proj.shape[0]
    n_win = K_WIN // Q_BLOCK
    n_blocks = t // Q_BLOCK

    def body(q_ref, *refs):
        k_refs, v_refs = refs[:n_win], refs[n_win:2 * n_win]
        b_ref, o_ref, lse_ref, do_ref, dq_ref, dk_ref, dv_ref, db_ref, dk_acc, dv_acc = refs[2 * n_win:]
        i = pl.program_id(1)

        @pl.when(i == 0)
        def _():
            dk_acc[...] = jnp.zeros_like(dk_acc)
            dv_acc[...] = jnp.zeros_like(dv_acc)
            db_ref[...] = jnp.zeros_like(db_ref)

        rows = pl.ds(pl.multiple_of(i * Q_BLOCK, Q_BLOCK), K_WIN)
        scale = HEAD_DIM ** -0.5
        for pp in range(pairs):
            cols = slice(pp * PAIR, (pp + 1) * PAIR)
            k = jnp.concatenate([r[:, cols] for r in k_refs], axis=0)
            v = jnp.concatenate([r[:, cols] for r in v_refs], axis=0)
            q, do, o = q_ref[:, cols], do_ref[:, cols], o_ref[:, cols].astype(F32)
            dq = dk = dv = None
            for hh, lanes in enumerate(_head_lanes()):
                qh, doh = _only(lanes, q, scale), _only(lanes, do)
                s = _contract_lanes(qh, k) + b_ref[2 * pp + hh]
                lse_col = pp * PAIR + hh * HEAD_DIM
                p = jnp.exp(s - lse_ref[:, lse_col:lse_col + 1])
                delta = jnp.sum(doh.astype(F32) * o, axis=1, keepdims=True)
                ds = p * (_contract_lanes(doh, v) - delta)
                folded = ds[:CHUNK]
                for c in range(1, Q_BLOCK // CHUNK):
                    folded = folded + pltpu.roll(ds[c * CHUNK:(c + 1) * CHUNK], K_WIN - c * CHUNK, 1)
                db_ref[2 * pp + hh] += folded
                dsb = ds.astype(BF16)
                dqh = jnp.dot(dsb, k, preferred_element_type=F32)
                dq = dqh if dq is None else jnp.where(lanes, dqh, dq)
                dkh, dvh = _contract_rows(dsb, qh), _contract_rows(p.astype(BF16), doh)
                dk, dv = (dkh, dvh) if dk is None else (dk + dkh, dv + dvh)
            dq_ref[:, cols] = (dq * scale).astype(dq_ref.dtype)
            dk_acc[rows, cols] += dk
            dv_acc[rows, cols] += dv

        @pl.when(i == n_blocks - 1)
        def _():
            dk_ref[...] = dk_acc[K_PAD:, :].astype(dk_ref.dtype)
            dv_ref[...] = dv_acc[K_PAD:, :].astype(dv_ref.dtype)

    width = pairs * PAIR
    row_spec, kv_specs, bias_spec = _attn_specs(pairs)
    full_spec = pl.BlockSpec((t, width), lambda g, i: (0, g))
    return pl.pallas_call(
        body, name="attn_bwd", grid=(N_PAIRS // pairs, n_blocks),
        in_specs=[row_spec] + kv_specs + [bias_spec, row_spec, row_spec, row_spec],
        out_specs=[row_spec, full_spec, full_spec,
                   pl.BlockSpec((2 * pairs, CHUNK, K_WIN), lambda g, i: (g, 0, 0))],
        out_shape=[jax.ShapeDtypeStruct((t, D_ATTN), BF16)] * 3 + [jax.ShapeDtypeStruct((N_HEADS, CHUNK, K_WIN), F32)],
        scratch_shapes=[pltpu.VMEM((t + K_PAD, width), F32)] * 2, compiler_params=_cparams(2),
    )(*([proj] * (1 + 2 * n_win)), bias, att, lse, datt)


CONV_LEAD = CONV_HALO - (CONV_WIDTH - 1)
CONV_LANES = 128
CONV_ROWS = 64


def _conv_specs(t):
    tt = _row_tile(t, CONV_TILE)
    per = tt // CONV_HALO
    n_halo = t // CONV_HALO
    tile = lambda cb: pl.BlockSpec((tt, COL), functools.partial(lambda i, cb: (i, cb), cb=cb))
    prev = lambda cb: pl.BlockSpec((CONV_HALO, COL),
                                   functools.partial(lambda i, cb: (jnp.maximum(i * per - 1, 0), cb), cb=cb))
    nxt = lambda cb: pl.BlockSpec((CONV_HALO, COL),
                                  functools.partial(lambda i, cb: (jnp.minimum((i + 1) * per, n_halo - 1), cb), cb=cb))
    vec = lambda shape: pl.BlockSpec(shape, lambda i: (0, 0))
    return tt, tile, prev, nxt, vec


def _glu(ca, cg, bias):
    return (ca.astype(F32) + bias[:, :D_CONV]) * jax.nn.sigmoid(cg.astype(F32) + bias[:, D_CONV:])


SUBLANES = 8


def _shift_copies(ext_ref):
    n = ext_ref.shape[1] - SUBLANES
    for s in range(1, SUBLANES):
        ext_ref[s, 0:n, :] = ext_ref[0, s:s + n, :]


def _tap_tiles(ext_ref, first_row, r0, lanes):
    n_g = CONV_ROWS // SUBLANES
    for s in range(SUBLANES):
        taps = [w for w in range(CONV_WIDTH) if first_row(w) % SUBLANES == s]
        if not taps:
            continue
        lo = min(first_row(w) for w in taps) - s
        n_tiles = (max(first_row(w) for w in taps) - s - lo) // SUBLANES + n_g
        tiles = [ext_ref[s, r0 + lo + SUBLANES * b:r0 + lo + SUBLANES * (b + 1), lanes] for b in range(n_tiles)]
        for w in taps:
            k = (first_row(w) - s - lo) // SUBLANES
            yield w, tiles[k:k + n_g]


def _taps(ext_ref, tt, first_row, w_ref, out_ref):
    n_g = CONV_ROWS // SUBLANES
    for l0 in range(0, D_CONV, CONV_LANES):
        lanes = slice(l0, l0 + CONV_LANES)
        for r0 in range(0, tt, CONV_ROWS):
            acc = [jnp.zeros((SUBLANES, CONV_LANES), F32)] * n_g
            for w, tiles in _tap_tiles(ext_ref, first_row, r0, lanes):
                weight = jnp.broadcast_to(w_ref[w:w + 1, lanes], (SUBLANES, CONV_LANES))
                acc = [a + t * weight for a, t in zip(acc, tiles)]
            for g in range(n_g):
                out_ref[r0 + SUBLANES * g:r0 + SUBLANES * (g + 1), lanes] = acc[g]


def _tap_sums(ext_ref, tt, first_row, x_ref, out_ref):
    n_g = CONV_ROWS // SUBLANES
    for l0 in range(0, D_CONV, CONV_LANES):
        lanes = slice(l0, l0 + CONV_LANES)
        acc = [jnp.zeros((SUBLANES, CONV_LANES), F32)] * CONV_WIDTH
        for r0 in range(0, tt, CONV_ROWS):
            x = [x_ref[0, r0 + SUBLANES * g:r0 + SUBLANES * (g + 1), lanes] for g in range(n_g)]
            for w, tiles in _tap_tiles(ext_ref, first_row, r0, lanes):
                part = tiles[0] * x[0]
                for g in range(1, n_g):
                    part = part + tiles[g] * x[g]
                acc[w] = acc[w] + part
        for w in range(CONV_WIDTH):
            out_ref[w:w + 1, lanes] += jnp.sum(acc[w], axis=0, keepdims=True)


def _conv_fwd(proj, glu_bias, dw, dw_b, ln_g, ln_b):
    t = proj.shape[0]
    tt, tile, prev, nxt, vec = _conv_specs(t)
    ca_blk, cg_blk = 3 * D_ATTN // COL, 3 * D_ATTN // COL + 1

    def body(ca_ref, cg_ref, pa_ref, pg_ref, gb_ref, dw_ref, dwb_ref, g_ref, b_ref, cs_ref, c_ref, z_ref, ext_ref):
        i = pl.program_id(0)
        bias = gb_ref[...]
        c = _glu(ca_ref[...], cg_ref[...], bias)
        halo = _glu(pa_ref[...], pg_ref[...], bias)
        ext_ref[0, 0:CONV_HALO, :] = jnp.where(i == 0, 0.0, halo)
        ext_ref[0, CONV_HALO:, :] = c
        _shift_copies(ext_ref)
        c_ref[...] = c
        _taps(ext_ref, tt, lambda w: CONV_LEAD + w, dw_ref, z_ref)
        z = z_ref[...] + dwb_ref[...]
        z_ref[...] = z
        mu = jnp.mean(z, axis=-1, keepdims=True)
        zc = z - mu
        y = zc * lax.rsqrt(jnp.mean(zc * zc, axis=-1, keepdims=True) + EPS) * g_ref[...] + b_ref[...]
        cs_ref[...] = (y * jax.nn.sigmoid(y)).astype(cs_ref.dtype)

    out_spec = pl.BlockSpec((tt, D_CONV), lambda i: (i, 0))
    return pl.pallas_call(
        body, name="conv_fwd", grid=(t // tt,),
        in_specs=[tile(ca_blk), tile(cg_blk), prev(ca_blk), prev(cg_blk), vec(glu_bias.shape), vec(dw.shape),
                  vec(dw_b.shape), vec(ln_g.shape), vec(ln_b.shape)],
        out_specs=[out_spec] * 3,
        out_shape=[jax.ShapeDtypeStruct((t, D_CONV), BF16), jax.ShapeDtypeStruct((t, D_CONV), F32),
                   jax.ShapeDtypeStruct((t, D_CONV), F32)],
        scratch_shapes=[pltpu.VMEM((SUBLANES, tt + CONV_HALO, D_CONV), F32)], compiler_params=_cparams(1),
    )(proj, proj, proj, proj, glu_bias, dw, dw_b, ln_g, ln_b)


def _conv_bwd(proj, c, z, dcs, glu_bias, dw, ln_g, ln_b):
    t = proj.shape[0]
    tt, tile, prev, nxt, vec = _conv_specs(t)
    n_tiles = t // tt
    ca_blk, cg_blk = 3 * D_ATTN // COL, 3 * D_ATTN // COL + 1

    def ln_bwd(zv, dcsv, g, b):
        mu = jnp.mean(zv, axis=-1, keepdims=True)
        zc = zv - mu
        rstd = lax.rsqrt(jnp.mean(zc * zc, axis=-1, keepdims=True) + EPS)
        zhat = zc * rstd
        y = zhat * g + b
        sig = jax.nn.sigmoid(y)
        dy = dcsv * sig * (1.0 + y * (1.0 - sig))
        dzh = dy * g
        dz = rstd * (dzh - jnp.mean(dzh, axis=-1, keepdims=True) - zhat * jnp.mean(dzh * zhat, axis=-1, keepdims=True))
        return dz, dy, zhat

    def body(ca_ref, cg_ref, c_ref, cprev_ref, z_ref, znext_ref, dcs_ref, dcsnext_ref, gb_ref, dw_ref, g_ref, b_ref,
             dcin_ref, ddw_ref, ddwb_ref, dg_ref, db_ref, dgb_ref, cext_ref, dzext_ref, dc_ref):
        i = pl.program_id(0)

        @pl.when(i == 0)
        def _():
            for ref in (ddw_ref, ddwb_ref, dg_ref, db_ref, dgb_ref):
                ref[...] = jnp.zeros_like(ref)

        g, b = g_ref[...], b_ref[...]
        dz, dy, zhat = ln_bwd(z_ref[...], dcs_ref[...], g, b)
        dz_next, _, _ = ln_bwd(znext_ref[...], dcsnext_ref[...], g, b)
        dg_ref[...] += _colsum(dy * zhat)
        db_ref[...] += _colsum(dy)
        ddwb_ref[...] += _colsum(dz)
        dzext_ref[0, 0:tt, :] = dz
        dzext_ref[0, tt:, :] = jnp.where(i == n_tiles - 1, 0.0, dz_next)
        _shift_copies(dzext_ref)
        cext_ref[0, 0:CONV_HALO, :] = jnp.where(i == 0, 0.0, cprev_ref[...])
        cext_ref[0, CONV_HALO:, :] = c_ref[...]
        _shift_copies(cext_ref)
        _taps(dzext_ref, tt, lambda w: CONV_WIDTH - 1 - w, dw_ref, dc_ref)
        _tap_sums(cext_ref, tt, lambda w: CONV_LEAD + w, dzext_ref, ddw_ref)
        bias = gb_ref[...]
        a_in = ca_ref[...].astype(F32) + bias[:, :D_CONV]
        sg = jax.nn.sigmoid(cg_ref[...].astype(F32) + bias[:, D_CONV:])
        dc = dc_ref[...]
        dcin = jnp.concatenate([dc * sg, dc * a_in * sg * (1.0 - sg)], axis=1)
        dcin_ref[...] = dcin.astype(dcin_ref.dtype)
        dgb_ref[...] += _colsum(dcin)

    row = lambda: pl.BlockSpec((tt, D_CONV), lambda i: (i, 0))
    per = tt // CONV_HALO
    n_halo = t // CONV_HALO
    prev_row = pl.BlockSpec((CONV_HALO, D_CONV), lambda i: (jnp.maximum(i * per - 1, 0), 0))
    next_row = lambda: pl.BlockSpec((CONV_HALO, D_CONV), lambda i: (jnp.minimum((i + 1) * per, n_halo - 1), 0))
    acc = lambda shape: pl.BlockSpec(shape, lambda i: (0, 0))
    return pl.pallas_call(
        body, name="conv_bwd", grid=(n_tiles,),
        in_specs=[tile(ca_blk), tile(cg_blk), row(), prev_row, row(), next_row(), row(), next_row(),
                  vec(glu_bias.shape), vec(dw.shape), vec(ln_g.shape), vec(ln_b.shape)],
        out_specs=[pl.BlockSpec((tt, 2 * D_CONV), lambda i: (i, 0)), acc(dw.shape), acc((1, D_CONV)),
                   acc((1, D_CONV)), acc((1, D_CONV)), acc((1, 2 * D_CONV))],
        out_shape=[jax.ShapeDtypeStruct((t, 2 * D_CONV), BF16), jax.ShapeDtypeStruct(dw.shape, F32),
                   jax.ShapeDtypeStruct((1, D_CONV), F32), jax.ShapeDtypeStruct((1, D_CONV), F32),
                   jax.ShapeDtypeStruct((1, D_CONV), F32), jax.ShapeDtypeStruct((1, 2 * D_CONV), F32)],
        scratch_shapes=[pltpu.VMEM((SUBLANES, tt + CONV_HALO, D_CONV), F32),
                        pltpu.VMEM((SUBLANES, tt + CONV_HALO, D_CONV), F32), pltpu.VMEM((tt, D_CONV), F32)],
        compiler_params=_cparams(1),
    )(proj, proj, c, c, z, z, dcs, dcs, glu_bias, dw, ln_g, ln_b)


def _place():
    x, y, c = lax.axis_index("x"), lax.axis_index("y"), lax.axis_index("c")
    chips = [(1 - x, y), (x, 1 - y), (1 - x, 1 - y)]
    return x, y, c, chips


def _chip_index(chip):
    return 2 * chip[0] + chip[1]


def _half_rows(c, half):
    return pl.ds(pl.multiple_of(c * half, 16), half)


def _gather_carry(blocked):
    n = len(blocked)

    def over_ici(o_refs, send_sems, recv_sems):
        x, y, c, chips = _place()
        me = _chip_index((x, y))
        copies = []
        for a in range(n):
            mine = o_refs[a].at[me, _half_rows(c, o_refs[a].shape[1] // 2), :]
            for k, chip in enumerate(chips):
                copies.append(pltpu.make_async_remote_copy(
                    src_ref=mine, dst_ref=mine, send_sem=send_sems.at[6 * a + k], recv_sem=recv_sems.at[6 * a + k],
                    device_id=(chip[0], chip[1], c), device_id_type=MESH))
        return copies

    def to_sibling(o_refs, send_sems, recv_sems, sent_by_me):
        x, y, c, chips = _place()
        copies = []
        for a in range(n):
            rows = _half_rows(c if sent_by_me else 1 - c, o_refs[a].shape[1] // 2)
            for k, chip in enumerate(chips):
                landed = o_refs[a].at[_chip_index(chip), rows, :]
                copies.append(pltpu.make_async_remote_copy(
                    src_ref=landed, dst_ref=landed, send_sem=send_sems.at[6 * a + 3 + k],
                    recv_sem=recv_sems.at[6 * a + 3 + k], device_id=(x, y, 1 - c), device_id_type=MESH))
        return copies

    def start(ins, outs, sems):
        for cp in over_ici(outs, *sems):
            cp.start()

    def hand_on(ins, outs, sems):
        for arrived, onward in zip(over_ici(outs, *sems), to_sibling(outs, *sems, True)):
            arrived.wait_recv()
            onward.start()

    def finish(ins, outs, sems):
        for cp in to_sibling(outs, *sems, False):
            cp.wait_recv()
        for cp in over_ici(outs, *sems) + to_sibling(outs, *sems, True):
            cp.wait_send()

    return _Carry(
        ins=list(blocked), outs=[jax.ShapeDtypeStruct(w.shape, w.dtype) for w in blocked],
        aliases={a: a for a in range(n)},
        sems=[pltpu.SemaphoreType.DMA((6 * n,)), pltpu.SemaphoreType.DMA((6 * n,))],
        phases=[("first", start), ("late", hand_on), ("last", finish)])


def _pair_exchange(name, grads):
    n = len(grads)

    def body(*refs):
        g_refs, land_refs = refs[:n], refs[n:2 * n]
        send_sems, recv_sems = refs[2 * n:]
        x, y, c, _ = _place()
        copies = []
        for a in range(n):
            half = g_refs[a].shape[1] // 2
            cp = pltpu.make_async_remote_copy(
                src_ref=g_refs[a].at[:, _half_rows(1 - c, half), :], dst_ref=land_refs[a],
                send_sem=send_sems.at[a], recv_sem=recv_sems.at[a], device_id=(x, y, 1 - c), device_id_type=MESH)
            cp.start()
            copies.append(cp)
        for cp in copies:
            cp.wait()

    return pl.pallas_call(
        body, name=name, in_specs=[ANY] * n, out_specs=[ANY] * n,
        out_shape=[jax.ShapeDtypeStruct((g.shape[0], g.shape[1] // 2, g.shape[2]), g.dtype) for g in grads],
        scratch_shapes=[pltpu.SemaphoreType.DMA((n,)), pltpu.SemaphoreType.DMA((n,))],
    )(*grads)


def _to_owner_carry(parts):
    n = len(parts)

    def sends(p_refs, l_refs, send_sems, recv_sems):
        x, y, c, chips = _place()
        me = _chip_index((x, y))
        return [pltpu.make_async_remote_copy(
            src_ref=p_refs[a].at[_chip_index(chip)], dst_ref=l_refs[a].at[me],
            send_sem=send_sems.at[3 * a + k], recv_sem=recv_sems.at[3 * a + k],
            device_id=(chip[0], chip[1], c), device_id_type=MESH) for a in range(n) for k, chip in enumerate(chips)]

    def start(ins, outs, sems):
        for cp in sends(ins, outs, *sems):
            cp.start()

    def finish(ins, outs, sems):
        x, y, c, chips = _place()
        send_sems, recv_sems = sems
        for a in range(n):
            for k, chip in enumerate(chips):
                slot = outs[a].at[_chip_index(chip)]
                pltpu.make_async_remote_copy(
                    src_ref=slot, dst_ref=slot, send_sem=send_sems.at[3 * a + k], recv_sem=recv_sems.at[3 * a + k],
                    device_id=(chip[0], chip[1], c), device_id_type=MESH).wait_recv()
        for cp in sends(ins, outs, *sems):
            cp.wait_send()

    return _Carry(
        ins=list(parts), outs=[jax.ShapeDtypeStruct(p.shape, p.dtype) for p in parts], aliases={},
        sems=[pltpu.SemaphoreType.DMA((3 * n,)), pltpu.SemaphoreType.DMA((3 * n,))],
        phases=[("first", start), ("last", finish)])


def _swap_halves(halves):
    n = len(halves)

    def body(*refs):
        h_refs, o_refs = refs[:n], refs[n:2 * n]
        send_sems, recv_sems = refs[2 * n:]
        x, y, c, _ = _place()
        copies = []
        for a in range(n):
            cp = pltpu.make_async_remote_copy(
                src_ref=h_refs[a], dst_ref=o_refs[a], send_sem=send_sems.at[a], recv_sem=recv_sems.at[a],
                device_id=(x, y, 1 - c), device_id_type=MESH)
            cp.start()
            copies.append(cp)
        for cp in copies:
            cp.wait()

    return pl.pallas_call(
        body, name="grad_swap_halves", in_specs=[ANY] * n, out_specs=[ANY] * n,
        out_shape=[jax.ShapeDtypeStruct(h.shape, h.dtype) for h in halves],
        scratch_shapes=[pltpu.SemaphoreType.DMA((n,)), pltpu.SemaphoreType.DMA((n,))],
    )(*halves)


def _all_devices(name, block):
    r, cols = block.shape

    def body(b_ref, all_ref, sum_ref, send_sems, recv_sems):
        x, y, c, _ = _place()
        me = 4 * x + 2 * y + c
        all_ref[me] = b_ref[...]
        flips = [(fx, fy, fc) for fx in (0, 1) for fy in (0, 1) for fc in (0, 1)][1:]
        copies = []
        for k, (fx, fy, fc) in enumerate(flips):
            cp = pltpu.make_async_remote_copy(
                src_ref=b_ref, dst_ref=all_ref.at[me], send_sem=send_sems.at[k], recv_sem=recv_sems.at[k],
                device_id=(x ^ fx, y ^ fy, c ^ fc), device_id_type=MESH)
            cp.start()
            copies.append(cp)
        for k, (fx, fy, fc) in enumerate(flips):
            slot = all_ref.at[4 * (x ^ fx) + 2 * (y ^ fy) + (c ^ fc)]
            pltpu.make_async_remote_copy(
                src_ref=slot, dst_ref=slot, send_sem=send_sems.at[k], recv_sem=recv_sems.at[k],
                device_id=(x ^ fx, y ^ fy, c ^ fc), device_id_type=MESH).wait_recv()
        for cp in copies:
            cp.wait_send()
        acc = all_ref[0]
        for d in range(1, N_DEV):
            acc = acc + all_ref[d]
        sum_ref[...] = acc

    vmem = pl.BlockSpec(memory_space=pltpu.VMEM)
    return pl.pallas_call(
        body, name=name, in_specs=[vmem], out_specs=[vmem, vmem],
        out_shape=[jax.ShapeDtypeStruct((N_DEV, r, cols), F32), jax.ShapeDtypeStruct((r, cols), F32)],
        scratch_shapes=[pltpu.SemaphoreType.DMA((N_DEV - 1,)), pltpu.SemaphoreType.DMA((N_DEV - 1,))],
    )(block)


PACK = 1024


def _packed_rows(shape, width):
    size, last = int(np.prod(shape)), shape[-1]
    cols = last if last <= width else width
    assert size % cols == 0
    return size // cols, cols


def _pack(vals, width=PACK):
    rows = []
    for v in vals:
        n_rows, cols = _packed_rows(v.shape, width)
        rows.append(jnp.pad(v.reshape(n_rows, cols).astype(F32), ((0, 0), (0, width - cols))))
    buf = jnp.concatenate(rows, axis=0)
    return jnp.pad(buf, ((0, (-buf.shape[0]) % 8), (0, 0)))


def _unpack(buf, shapes, width=PACK):
    out, r = [], 0
    for shape in shapes:
        n_rows, cols = _packed_rows(shape, width)
        out.append(buf[r:r + n_rows, :cols].reshape(shape))
        r += n_rows
    return out


FFN_SPLIT = 2


def _ffn_hidden(name, n, wg, wu, tm=1024, carry=None):
    m, k = n.shape
    f = wg.shape[0]
    fb = f // FFN_SPLIT
    tm = _row_tile(m, tm)

    def core(n_ref, wg_ref, wu_ref, a_ref, b_ref, s_ref):
        nv = n_ref[...]
        a = _dot(nv, wg_ref[...], True)
        b = _dot(nv, wu_ref[...], True)
        a_ref[...] = a.astype(a_ref.dtype)
        b_ref[...] = b.astype(b_ref.dtype)
        s_ref[...] = (a * jax.nn.sigmoid(a) * b).astype(s_ref.dtype)

    w_spec = pl.BlockSpec((fb, k), lambda j, i: (j, 0))
    out_spec = pl.BlockSpec((tm, fb), lambda j, i: (i, j))
    return _call(name, core, (FFN_SPLIT, m // tm), [pl.BlockSpec((tm, k), lambda j, i: (i, 0)), w_spec, w_spec],
                 [out_spec] * 3, [jax.ShapeDtypeStruct((m, f), BF16)] * 3, [], [n, wg, wu], carry)


def _ffn_d_hidden(name, df, wd, a, b, tm=512):
    m, k = df.shape
    f = wd.shape[0]
    fb = f // FFN_SPLIT
    tm = _row_tile(m, tm)

    def body(df_ref, wd_ref, a_ref, b_ref, da_ref, db_ref):
        dfv = df_ref[...]
        for j in range(FFN_SPLIT):
            cols = slice(j * fb, (j + 1) * fb)
            ds = _dot(dfv, wd_ref[cols, :], True)
            av, bv = a_ref[:, cols].astype(F32), b_ref[:, cols].astype(F32)
            sig = jax.nn.sigmoid(av)
            da_ref[:, cols] = (ds * bv * sig * (1.0 + av * (1.0 - sig))).astype(da_ref.dtype)
            db_ref[:, cols] = (ds * av * sig).astype(db_ref.dtype)

    row = pl.BlockSpec((tm, f), lambda i: (i, 0))
    return pl.pallas_call(
        body, name=name, grid=(m // tm,),
        in_specs=[pl.BlockSpec((tm, k), lambda i: (i, 0)), _resident(wd), row, row],
        out_specs=[row, row], out_shape=[jax.ShapeDtypeStruct((m, f), BF16)] * 2,
        compiler_params=_cparams(1),
    )(df, wd, a, b)


def kernel(x, ffn1_norm_pre, ffn1_w_gate, ffn1_w_up, ffn1_w_down, ffn1_norm_post, mix_norm_pre, w_in, gate_bias, rel_table, w_attn_out, conv_glu_bias, conv_dw_w, conv_dw_b, conv_ln_g, conv_ln_b, conv_w_out, w_out, mix_norm_post, ffn2_norm_pre, ffn2_w_gate, ffn2_w_up, ffn2_w_down, ffn2_norm_post, loss_target, m_ffn1_norm_pre, m_ffn1_w_gate, m_ffn1_w_up, m_ffn1_w_down, m_ffn1_norm_post, m_mix_norm_pre, m_w_in, m_gate_bias, m_rel_table, m_w_attn_out, m_conv_glu_bias, m_conv_dw_w, m_conv_dw_b, m_conv_ln_g, m_conv_ln_b, m_conv_w_out, m_w_out, m_mix_norm_post, m_ffn2_norm_pre, m_ffn2_w_gate, m_ffn2_w_up, m_ffn2_w_down, m_ffn2_norm_post, v_ffn1_norm_pre, v_ffn1_w_gate, v_ffn1_w_up, v_ffn1_w_down, v_ffn1_norm_post, v_mix_norm_pre, v_w_in, v_gate_bias, v_rel_table, v_w_attn_out, v_conv_glu_bias, v_conv_dw_w, v_conv_dw_b, v_conv_ln_g, v_conv_ln_b, v_conv_w_out, v_w_out, v_mix_norm_post, v_ffn2_norm_pre, v_ffn2_w_gate, v_ffn2_w_up, v_ffn2_w_down, v_ffn2_norm_post):
    args = dict(locals())
    names = ['ffn1_norm_pre', 'ffn1_w_gate', 'ffn1_w_up', 'ffn1_w_down', 'ffn1_norm_post', 'mix_norm_pre', 'w_in',
             'gate_bias', 'rel_table', 'w_attn_out', 'conv_glu_bias', 'conv_dw_w', 'conv_dw_b', 'conv_ln_g',
             'conv_ln_b', 'conv_w_out', 'w_out', 'mix_norm_post', 'ffn2_norm_pre', 'ffn2_w_gate', 'ffn2_w_up',
             'ffn2_w_down', 'ffn2_norm_post']
    big = ['ffn1_w_gate', 'ffn1_w_up', 'ffn1_w_down', 'w_in', 'w_attn_out', 'conv_w_out', 'w_out', 'ffn2_w_gate',
           'ffn2_w_up', 'ffn2_w_down']
    small = [n for n in names if n not in big]

    xs, target = x[0], loss_target[0]
    t, d = xs.shape
    cx, cy = lax.axis_index("x"), lax.axis_index("y")
    chip = 2 * cx + cy

    dw_shard = conv_dw_w[0, :, 0, :]
    cshard = dw_shard.shape[1]
    dw_all, _ = _all_devices("gather_dw", _pack([dw_shard], width=cshard))
    dw_full = jnp.concatenate([dw_all[2 * j, :CONV_WIDTH, :cshard] for j in range(N_CHIPS)], axis=1)
    dw_full = jnp.pad(dw_full, ((0, CONV_HALO - CONV_WIDTH), (0, 0)))
    peers = [(1 - cx, cy), (cx, 1 - cy), (1 - cx, 1 - cy)]
    pos = jnp.stack([lax.axis_index("c"), chip] + [_chip_index(p) for p in peers]).astype(jnp.int32)
    transposed = ("ffn1_w_gate", "ffn1_w_up", "ffn2_w_gate", "ffn2_w_up")
    weight_of = lambda n: n[2:] if n[:2] in ("m_", "v_") else n
    shard = lambda n: jnp.transpose(args[n][0]) if weight_of(n) in transposed else args[n][0]
    unshard = lambda n, v: (jnp.transpose(v) if n in transposed else v)[None]
    own = {n: _cast_into("cast_" + n, pos, shard(n)) for n in big}
    gather = lambda *ns: _gather_carry([own[n] for n in ns])
    res_spec = [(d, F32), (d, F32), (d, BF16)]
    whole = lambda w: w.reshape(-1, w.shape[-1])

    table_pad = jnp.pad(rel_table[0], ((0, 0), (0, REL_PAD - rel_table.shape[2])))
    bias, (wg1,) = _bias_expand(table_pad, gather("ffn1_w_gate"))
    n1, (wu1,) = _rms_fwd("ffn1_pre", xs, ffn1_norm_pre, gather("ffn1_w_up"))
    (a1, b1, s1), (wd1, win) = _ffn_hidden(
        "ffn1_hidden", n1, whole(wg1), whole(wu1), carry=gather("ffn1_w_down", "w_in"))
    (f1, h1, u), (wg2, wao, wco, wout) = _mm_kblk(
        "ffn1_down", [(s1, whole(wd1)[None])], trans_w=False, epilogue=_ep_post_res_pre(0.5), rows=[xs],
        vecs=[ffn1_norm_post, mix_norm_pre], row_outs=res_spec,
        carry=gather("ffn2_w_gate", "w_attn_out", "conv_w_out", "w_out"))
    proj, (wu2, wd2) = _mm_nblk("mix_in", u, win, trans_w=False, out_blocked=False, out_dtype=BF16, tm=2048,
                                carry=gather("ffn2_w_up", "ffn2_w_down"))
    att, lse = _attn_fwd(proj, bias)
    cs, c_glu, z_conv = _conv_fwd(proj, conv_glu_bias, dw_full, conv_dw_b, conv_ln_g, conv_ln_b)
    y_a, y_b, merged = _mix_merge(att, cs, wao, wco, proj, gate_bias)
    (mo, h2, n2), _ = _mm_kblk(
        "mix_out", [(merged, wout)], trans_w=False, epilogue=_ep_post_res_pre(1.0), rows=[h1],
        vecs=[mix_norm_post, ffn2_norm_pre], row_outs=res_spec, tm=1024)
    (a2, b2, s2), _ = _ffn_hidden("ffn2_hidden", n2, whole(wg2), whole(wu2))
    g = {}
    (dy, df2, err2, g["ffn2_norm_post"]), _ = _mm_kblk(
        "ffn2_down", [(s2, whole(wd2)[None])], trans_w=False, epilogue=_ep_loss(0.5, d), rows=[h2, target],
        vecs=[ffn2_norm_post], row_outs=[(d, F32), (d, BF16)], vec_outs=[d, d])
    loss = lax.psum(0.5 * jnp.sum(err2) / d, ("x", "y", "c"))

    parts, landed = {}, {}

    def ffn_bwd(tag, df, n, a, b, s, wg, wu, wd, **epilogue):
        da, db = _ffn_d_hidden(tag + "_d_hidden", df, whole(wd), a, b)
        group = [tag + "_w_down", tag + "_w_gate", tag + "_w_up"]
        local = [_mm_tn_wide(tag + "_g_" + what, hidden, other, d, a_split=FFN_SPLIT).reshape(wd.shape)
                 for what, hidden, other in (("down", s, df), ("gate", da, n), ("up", db, n))]
        return _mm_kblk(tag + "_d_n", [(da, whole(wg)[None]), (db, whole(wu)[None])], trans_w=False, sub=512,
                        carry=pair_sums(tag, group, local), **epilogue), group

    def pair_sums(tag, group, local):
        theirs = _pair_exchange("pair_" + tag, local)
        for n, mine, other in zip(group, local, theirs):
            parts[n] = _add_pair("pair_sum_" + n, pos, mine, other)
        return _to_owner_carry([parts[n] for n in group])

    def keep(group, carried):
        for n, val in zip(group, carried):
            landed[n] = val

    ((dh2, dmo, g["ffn2_norm_pre"], g["mix_norm_post"]), carried), group = ffn_bwd(
        "ffn2", df2, n2, a2, b2, s2, wg2, wu2, wd2, epilogue=_ep_pre_bwd_post(1.0), rows=[h2, dy, mo],
        vecs=[ffn2_norm_pre, mix_norm_post], row_outs=[(d, F32), (d, BF16)], vec_outs=[d, d])
    keep(group, carried)
    g_wout = _mm_tn("mix_g_out", merged, "col", dmo, "full", tt=4096)
    dy_a, dy_b, dgates, datt, dcs, g["gate_bias"] = _mix_d_merge(dmo, wout, y_a, y_b, wao, wco, proj, gate_bias)
    g_wao = _mm_tn("attn_g_out", att, "full", dy_a, "col", tt=4096)
    g_wco = _mm_tn("conv_g_out", cs, "full", dy_b, "col", tt=4096)
    dq, dk, dv, dbias = _attn_bwd(proj, bias, att, lse, datt)
    g["rel_table"] = _bias_fold(dbias)[:, :rel_table.shape[2]]
    dcin, g_dw, g["conv_dw_b"], g["conv_ln_g"], g["conv_ln_b"], g["conv_glu_bias"] = _conv_bwd(
        proj, c_glu, z_conv, dcs, conv_glu_bias, dw_full, conv_ln_g, conv_ln_b)
    pieces = [("q", dq), ("k", dk), ("v", dv), ("conv", dcin), ("gates", dgates)]
    n_in = win.shape[0] * win.shape[2]
    win_cols = jnp.transpose(jnp.transpose(win, (1, 0, 2)).reshape(d, n_in // COL, COL), (1, 0, 2))
    g_cols = jnp.concatenate([_mm_tn_wide("mix_g_in_" + tag, u, piece, COL, tt=2048) for tag, piece in pieces],
                             axis=0)
    g_win = jnp.transpose(jnp.transpose(g_cols, (1, 0, 2)).reshape(d, win.shape[0], win.shape[2]), (1, 0, 2))
    bounds = np.cumsum([0] + [piece.shape[1] // COL for _, piece in pieces])
    group = ["w_out", "w_attn_out", "conv_w_out", "w_in"]
    (dh1, df1, g["mix_norm_pre"], g["ffn1_norm_post"]), carried = _mm_kblk(
        "mix_d_in", [(piece, win_cols[lo:hi]) for (_, piece), lo, hi in zip(pieces, bounds[:-1], bounds[1:])],
        trans_w=True, sub=512, epilogue=_ep_pre_bwd_post(0.5), rows=[h1, dh2, f1],
        vecs=[mix_norm_pre, ffn1_norm_post], row_outs=[(d, F32), (d, BF16)], vec_outs=[d, d],
        carry=pair_sums("mix", group, [g_wout, g_wao, g_wco, g_win]))
    keep(group, carried)
    ((grad_x, g["ffn1_norm_pre"]), carried), group = ffn_bwd(
        "ffn1", df1, n1, a1, b1, s1, wg1, wu1, wd1, epilogue=_ep_pre_bwd_first(), rows=[xs, dh1],
        vecs=[ffn1_norm_pre], row_outs=[(d, F32)], vec_outs=[d])
    keep(group, carried)

    halves = [_add_chips("chip_sum_" + n, pos, parts[n], landed[n]) for n in big]
    other_halves = _swap_halves(halves)

    g["conv_dw_w"] = g_dw[:CONV_WIDTH]
    _, small_sum = _all_devices("sum_small", _pack([g[n] for n in small]))
    for n, val in zip(small, _unpack(small_sum, [g[n].shape for n in small])):
        g[n] = val
    g["conv_dw_w"] = lax.dynamic_slice_in_dim(g["conv_dw_w"], chip * cshard, cshard, axis=1)

    grads, deltas, new_m, new_v = {}, {}, {}, {}
    for n, mine, other in zip(big, halves, other_halves):
        gr, dl, m2, v2 = _adamw_halves("adamw_" + n, pos, shard(n), shard("m_" + n), shard("v_" + n), mine, other)
        grads[n], deltas[n], new_m[n], new_v[n] = unshard(n, gr), unshard(n, dl), unshard(n, m2), unshard(n, v2)
    shapes = [g[n].shape for n in small]
    packed = lambda pre: _pack([args[pre + n].reshape(shp) for n, shp in zip(small, shapes)])
    dl, m2, v2 = _adamw("adamw_small", packed(""), _pack([g[n] for n in small]), packed("m_"), packed("v_"))
    for n, a_, b_, c_ in zip(small, _unpack(dl, shapes), _unpack(m2, shapes), _unpack(v2, shapes)):
        shape = args[n].shape
        grads[n], deltas[n], new_m[n], new_v[n] = (g[n].reshape(shape), a_.reshape(shape), b_.reshape(shape),
                                                   c_.reshape(shape))

    return (loss, grad_x[None], *[grads[n] for n in names], *[deltas[n] for n in names],
            *[new_m[n] for n in names], *[new_v[n] for n in names])
```

```python
import functools

import numpy as np
import jax
import jax.numpy as jnp
from jax import lax
from jax.experimental import pallas as pl
from jax.experimental.pallas import tpu as pltpu

F32 = jnp.float32
BF16 = jnp.bfloat16
MESH = pl.DeviceIdType.MESH
ANY = pl.BlockSpec(memory_space=pl.ANY)

EPS = 1e-6
CHUNK = 64
LEFT_CHUNKS = 8
N_HEADS = 8
HEAD_DIM = 64
D_ATTN = N_HEADS * HEAD_DIM
D_CONV = 512
CONV_WIDTH = 31
REL_CLIP = 128
N_CHIPS = 4
N_DEV = 8
Q_BLOCK = 4 * CHUNK
K_PAD = LEFT_CHUNKS * CHUNK
K_WIN = K_PAD + Q_BLOCK
REL_EXT = 1024
REL_PAD = 384
CONV_HALO = 32
CONV_TILE = 512
COL = 512
NEG = -1e30

ADAM_LR = 0.001
ADAM_B1 = 0.9
ADAM_B2 = 0.999
ADAM_EPS = 1e-08
ADAM_WD = 0.01
ADAM_STEP = 10

VMEM_LIMIT_BYTES = 56 * 1024 * 1024


def _cparams(n_grid):
    return pltpu.CompilerParams(dimension_semantics=("arbitrary",) * n_grid, vmem_limit_bytes=VMEM_LIMIT_BYTES)


def _row_tile(rows, want):
    if rows <= want:
        return rows
    for t in range(want - want % 16, 0, -16):
        if rows % t == 0:
            return t
    raise ValueError((rows, want))


def _dot(a, w, trans_w):
    dims = (((1,), (1,)), ((), ())) if trans_w else (((1,), (0,)), ((), ()))
    return lax.dot_general(a, w, dims, preferred_element_type=F32)


class _Carry:
    LATE_STEPS = 2

    def __init__(self, ins, outs, aliases, sems, phases):
        self.ins, self.outs, self.aliases, self.sems, self.phases = ins, outs, aliases, sems, phases


def _call(name, core, grid, in_specs, out_specs, out_shape, scratch, args, carry=None):
    n_in, n_out, n_scr = len(in_specs), len(out_specs), len(scratch)
    if carry is None:
        out = pl.pallas_call(core, name=name, grid=grid, in_specs=in_specs, out_specs=out_specs, out_shape=out_shape,
                             scratch_shapes=scratch, compiler_params=_cparams(len(grid)))(*args)
        return list(out), []
    c_in, c_out = len(carry.ins), len(carry.outs)
    total = int(np.prod(grid))
    late = max(total - 1 - _Carry.LATE_STEPS, 0)

    def body(*refs):
        ins, refs = refs[:n_in], refs[n_in:]
        c_ins, refs = refs[:c_in], refs[c_in:]
        outs, refs = refs[:n_out], refs[n_out:]
        c_outs, refs = refs[:c_out], refs[c_out:]
        scr, c_sems = refs[:n_scr], refs[n_scr:]
        step = pl.program_id(0)
        for axis in range(1, len(grid)):
            step = step * grid[axis] + pl.program_id(axis)

        def run(when, at):
            for w, fn in carry.phases:
                if w == when:
                    pl.when(step == at)(functools.partial(fn, c_ins, c_outs, c_sems))

        run("first", 0)
        core(*ins, *outs, *scr)
        run("late", late)
        run("last", total - 1)

    out = pl.pallas_call(
        body, name=name, grid=grid, in_specs=list(in_specs) + [ANY] * c_in, out_specs=list(out_specs) + [ANY] * c_out,
        out_shape=list(out_shape) + list(carry.outs), scratch_shapes=list(scratch) + list(carry.sems),
        input_output_aliases={n_in + a: n_out + b for a, b in carry.aliases.items()},
        compiler_params=_cparams(len(grid)),
    )(*args, *carry.ins)
    return list(out[:n_out]), list(out[n_out:])


def _mm_nblk(name, a, w, *, trans_w, out_blocked, out_dtype, tm=1024, carry=None):
    m, k = a.shape
    nj = w.shape[0]
    nb = w.shape[1] if trans_w else w.shape[2]
    tm = _row_tile(m, tm)

    def core(a_ref, w_ref, o_ref):
        o_ref[...] = _dot(a_ref[...], w_ref[...], trans_w).astype(o_ref.dtype)

    if out_blocked:
        out_shape, out_spec = (nj, m, nb), pl.BlockSpec((None, tm, nb), lambda j, i: (j, i, 0))
    else:
        out_shape, out_spec = (m, nj * nb), pl.BlockSpec((tm, nb), lambda j, i: (i, j))
    out, carried = _call(
        name, core, (nj, m // tm),
        [pl.BlockSpec((tm, k), lambda j, i: (i, 0)), pl.BlockSpec((None,) + w.shape[1:], lambda j, i: (j, 0, 0))],
        [out_spec], [jax.ShapeDtypeStruct(out_shape, out_dtype)], [], [a, w], carry)
    return out[0] if carry is None else (out[0], carried)


def _mm_kblk(name, pairs, *, trans_w, out_dtype=F32, tm=512, sub=256, epilogue=None, rows=(), vecs=(), row_outs=None,
             vec_outs=(), carry=None):
    w0 = pairs[0][1]
    n = w0.shape[1] if trans_w else w0.shape[2]
    blocks = [(w.shape[0], w.shape[2] if trans_w else w.shape[1]) for _, w in pairs]
    m = pairs[0][0].shape[-2]
    tm = _row_tile(m, tm)
    ts = _row_tile(tm, sub)
    n_pairs, n_rows, n_vecs = len(pairs), len(rows), len(vecs)
    if epilogue is None:
        epilogue, row_outs = (lambda acc, r, v: ([acc], [])), [(n, out_dtype)]
    n_ro, n_vo = len(row_outs), len(vec_outs)

    def core(*refs):
        pair_refs, refs = refs[:2 * n_pairs], refs[2 * n_pairs:]
        row_refs, refs = refs[:n_rows], refs[n_rows:]
        vec_refs, refs = refs[:n_vecs], refs[n_vecs:]
        ro_refs, vo_refs = refs[:n_ro], refs[n_ro:]
        if n_vo:
            @pl.when(pl.program_id(0) == 0)
            def _():
                for ref in vo_refs:
                    ref[...] = jnp.zeros_like(ref)

        vec_vals = [v[...] for v in vec_refs]
        sums = None
        for r0 in range(0, tm, ts):
            sub_rows = slice(r0, r0 + ts)
            acc = None
            for p in range(n_pairs):
                a_ref, w_ref = pair_refs[2 * p], pair_refs[2 * p + 1]
                nj, kb = blocks[p]
                for j in range(nj):
                    a_blk = a_ref[j, sub_rows, :] if len(a_ref.shape) == 3 else a_ref[sub_rows, j * kb:(j + 1) * kb]
                    part = _dot(a_blk, w_ref[j], trans_w)
                    acc = part if acc is None else acc + part
            ro, vo = epilogue(acc, [r[sub_rows, :] for r in row_refs], vec_vals)
            for ref, val in zip(ro_refs, ro):
                ref[sub_rows, :] = val.astype(ref.dtype)
            sums = vo if sums is None else [s + v for s, v in zip(sums, vo)]
        for ref, val in zip(vo_refs, sums or []):
            ref[...] += val

    in_specs, args = [], []
    for (a, w), (nj, kb) in zip(pairs, blocks):
        if a.ndim == 3:
            in_specs.append(pl.BlockSpec((nj, tm, kb), lambda i: (0, i, 0)))
        else:
            in_specs.append(pl.BlockSpec((tm, nj * kb), lambda i: (i, 0)))
        in_specs.append(pl.BlockSpec(w.shape, lambda i: (0, 0, 0), pipeline_mode=pl.Buffered(1)))
        args += [a, w]
    in_specs += [pl.BlockSpec((tm, r.shape[1]), lambda i: (i, 0)) for r in rows]
    in_specs += [pl.BlockSpec(v.shape, lambda i: (0, 0)) for v in vecs]
    out_specs = [pl.BlockSpec((tm, cols), lambda i: (i, 0)) for cols, _ in row_outs]
    out_specs += [pl.BlockSpec((1, cols), lambda i: (0, 0)) for cols in vec_outs]
    out_shape = [jax.ShapeDtypeStruct((m, cols), dt) for cols, dt in row_outs]
    out_shape += [jax.ShapeDtypeStruct((1, cols), F32) for cols in vec_outs]
    return _call(name, core, (m // tm,), in_specs, out_specs, out_shape, [], args + list(rows) + list(vecs), carry)


def _mm_tn(name, a, a_mode, b, b_mode, *, out_dtype=BF16, tt=2048):
    nj = N_CHIPS
    t = a.shape[-2]
    tt = _row_tile(t, tt)

    def spec(x, mode):
        if mode == "full":
            return x.shape[1], pl.BlockSpec((tt, x.shape[1]), lambda j, s: (s, 0))
        if mode == "col":
            cb = x.shape[1] // nj
            return cb, pl.BlockSpec((tt, cb), lambda j, s: (s, j))
        return x.shape[2], pl.BlockSpec((None, tt, x.shape[2]), lambda j, s: (j, s, 0))

    ca, a_spec = spec(a, a_mode)
    cb, b_spec = spec(b, b_mode)
    n_steps = t // tt

    def body(a_ref, b_ref, o_ref, acc_ref):
        s = pl.program_id(1)

        @pl.when(s == 0)
        def _():
            acc_ref[...] = jnp.zeros_like(acc_ref)

        acc_ref[...] += lax.dot_general(a_ref[...], b_ref[...], (((0,), (0,)), ((), ())),
                                        preferred_element_type=F32)

        @pl.when(s == n_steps - 1)
        def _():
            o_ref[...] = acc_ref[...].astype(o_ref.dtype)

    return pl.pallas_call(
        body, name=name, grid=(nj, n_steps), in_specs=[a_spec, b_spec],
        out_specs=pl.BlockSpec((None, ca, cb), lambda j, s: (j, 0, 0)),
        out_shape=jax.ShapeDtypeStruct((nj, ca, cb), out_dtype),
        scratch_shapes=[pltpu.VMEM((ca, cb), F32)], compiler_params=_cparams(2),
    )(a, b)


def _mm_tn_wide(name, a, b, cb, *, a_split=1, out_dtype=BF16, tt=1024):
    t, ca = a.shape
    nb = b.shape[1] // cb
    tt = _row_tile(t, tt)
    n_steps = t // tt
    piece = ca // a_split

    def body(a_ref, b_ref, o_ref, acc_ref):
        s = pl.program_id(0)

        @pl.when(s == 0)
        def _():
            acc_ref[...] = jnp.zeros_like(acc_ref)

        for j in range(nb):
            bv = b_ref[:, j * cb:(j + 1) * cb]
            for c in range(a_split):
                rows = slice(c * piece, (c + 1) * piece)
                acc_ref[j, rows, :] += lax.dot_general(a_ref[:, rows], bv, (((0,), (0,)), ((), ())),
                                                       preferred_element_type=F32)

        @pl.when(s == n_steps - 1)
        def _():
            o_ref[...] = acc_ref[...].astype(o_ref.dtype)

    return pl.pallas_call(
        body, name=name, grid=(n_steps,),
        in_specs=[pl.BlockSpec((tt, ca), lambda s: (s, 0)), pl.BlockSpec((tt, nb * cb), lambda s: (s, 0))],
        out_specs=pl.BlockSpec((nb, ca, cb), lambda s: (0, 0, 0)),
        out_shape=jax.ShapeDtypeStruct((nb, ca, cb), out_dtype),
        scratch_shapes=[pltpu.VMEM((nb, ca, cb), F32)], compiler_params=_cparams(1),
    )(a, b)


def _rowwise(name, fn, rows, vecs, row_outs, vec_outs, *, tm=256, carry=None):
    nrows = rows[0][0].shape[0]
    tm = _row_tile(nrows, tm)
    n_r, n_v, n_ro, n_vo = len(rows), len(vecs), len(row_outs), len(vec_outs)

    def body(*refs):
        r_vals = [r[...] for r in refs[:n_r]]
        v_vals = [r[...] for r in refs[n_r:n_r + n_v]]
        ro_refs = refs[n_r + n_v:n_r + n_v + n_ro]
        vo_refs = refs[n_r + n_v + n_ro:]
        ro, vo = fn(r_vals, v_vals)
        for ref, val in zip(ro_refs, ro):
            ref[...] = val.astype(ref.dtype)
        if n_vo:
            @pl.when(pl.program_id(0) == 0)
            def _():
                for ref in vo_refs:
                    ref[...] = jnp.zeros_like(ref)

            for ref, val in zip(vo_refs, vo):
                ref[...] += val

    in_specs = [pl.BlockSpec((tm, cols), functools.partial(lambda i, cb: (i, cb), cb=cb)) for _, cols, cb in rows]
    in_specs += [pl.BlockSpec(v.shape, functools.partial(lambda i, nd: (0,) * nd, nd=v.ndim)) for v in vecs]
    out_specs = [pl.BlockSpec((tm, cols), lambda i: (i, 0)) for cols, _ in row_outs]
    out_specs += [pl.BlockSpec((1, cols), lambda i: (0, 0)) for cols in vec_outs]
    out_shape = [jax.ShapeDtypeStruct((nrows, cols), dt) for cols, dt in row_outs]
    out_shape += [jax.ShapeDtypeStruct((1, cols), F32) for cols in vec_outs]
    out, carried = _call(name, body, (nrows // tm,), in_specs, out_specs, out_shape, [],
                         [r[0] for r in rows] + list(vecs), carry)
    return out if carry is None else (out, carried)


def _whole(x):
    return (x, x.shape[1], 0)


def _colsum(x):
    return jnp.sum(x, axis=0, keepdims=True)


def _rstd(x):
    return lax.rsqrt(jnp.mean(x * x, axis=-1, keepdims=True) + EPS)


def _rms_bwd(dn, x, g):
    r = _rstd(x)
    c = dn * g
    dx = r * c - x * (r * r * r) * jnp.mean(c * x, axis=-1, keepdims=True)
    return dx, _colsum(dn * x * r)


def _rms_fwd(name, x, g, carry):
    def fn(r, v):
        (xv,), (gv,) = r, v
        return [xv * _rstd(xv) * gv], []

    (n,), carried = _rowwise(name, fn, [_whole(x)], [g], [(x.shape[1], BF16)], [], carry=carry)
    return n, carried


def _ep_post_res_pre(scale):
    def epilogue(acc, rows, vecs):
        (resid,), (g_post, g_next) = rows, vecs
        h = resid + scale * (acc * _rstd(acc) * g_post)
        return [acc, h, h * _rstd(h) * g_next], []

    return epilogue


def _post_bwd(dh, f, g_post, scale):
    return _rms_bwd(scale * dh, f, g_post)


def _ep_loss(scale, d):
    def epilogue(acc, rows, vecs):
        (resid, target), (g_post,) = rows, vecs
        err = resid + scale * (acc * _rstd(acc) * g_post) - target
        dy = err * (1.0 / d)
        df, dg_post = _post_bwd(dy, acc, g_post, scale)
        return [dy, df], [_colsum(err * err), dg_post]

    return epilogue


def _ep_pre_bwd_post(scale_prev):
    def epilogue(acc, rows, vecs):
        (h, dh_up, f_prev), (g_pre, g_post_prev) = rows, vecs
        dx, dg_pre = _rms_bwd(acc, h, g_pre)
        dh = dh_up + dx
        df, dg_post = _post_bwd(dh, f_prev, g_post_prev, scale_prev)
        return [dh, df], [dg_pre, dg_post]

    return epilogue


def _ep_pre_bwd_first():
    def epilogue(acc, rows, vecs):
        (x, dh_up), (g_pre,) = rows, vecs
        dx, dg_pre = _rms_bwd(acc, x, g_pre)
        return [dh_up + dx], [dg_pre]

    return epilogue


def _gate_specs(d, tm):
    first = (3 * D_ATTN + 2 * D_CONV) // COL
    return [pl.BlockSpec((tm, COL), functools.partial(lambda i, cb: (i, cb), cb=first + p)) for p in range(2 * d // COL)]


def _gate(piece_refs, bias_ref, c0, width):
    p, off = divmod(c0, COL)
    return jax.nn.sigmoid(piece_refs[p][:, off:off + width].astype(F32) + bias_ref[:, c0:c0 + width])


def _resident(w):
    return pl.BlockSpec(w.shape, functools.partial(lambda i, nd: (0,) * nd, nd=w.ndim), pipeline_mode=pl.Buffered(1))


def _mix_merge(att, cs, wao, wco, proj, gate_bias, tm=1024):
    t = att.shape[0]
    nj, _, nb = wao.shape
    d = nj * nb
    tm = _row_tile(t, tm)
    gate_specs = _gate_specs(d, tm)
    n_p = len(gate_specs)

    def body(att_ref, cs_ref, wao_ref, wco_ref, *rest):
        pieces, (gb_ref, ya_ref, yb_ref, m_ref) = rest[:n_p], rest[n_p:]
        av, cv = att_ref[...], cs_ref[...]
        for j in range(nj):
            cols = slice(j * nb, (j + 1) * nb)
            ya = _dot(av, wao_ref[j], False)
            yb = _dot(cv, wco_ref[j], False)
            merged = _gate(pieces, gb_ref, j * nb, nb) * ya + _gate(pieces, gb_ref, d + j * nb, nb) * yb
            ya_ref[:, cols] = ya.astype(ya_ref.dtype)
            yb_ref[:, cols] = yb.astype(yb_ref.dtype)
            m_ref[:, cols] = merged.astype(m_ref.dtype)

    row = lambda x: pl.BlockSpec((tm, x.shape[1]), lambda i: (i, 0))
    out_spec = pl.BlockSpec((tm, d), lambda i: (i, 0))
    return pl.pallas_call(
        body, name="mix_merge", grid=(t // tm,),
        in_specs=[row(att), row(cs), _resident(wao), _resident(wco)] + gate_specs + [_resident(gate_bias)],
        out_specs=[out_spec] * 3, out_shape=[jax.ShapeDtypeStruct((t, d), BF16)] * 3, compiler_params=_cparams(1),
    )(att, cs, wao, wco, *([proj] * n_p), gate_bias)


def _mix_d_merge(dmo, wout, y_a, y_b, wao, wco, proj, gate_bias, tm=512):
    t, d = dmo.shape
    nj, _, nb = wao.shape
    ka, kc = wao.shape[1], wco.shape[1]
    tm = _row_tile(t, tm)
    gate_specs = _gate_specs(d, tm)
    n_p = len(gate_specs)

    def body(dmo_ref, wout_ref, ya_ref, yb_ref, wao_ref, wco_ref, *rest):
        pieces, (gb_ref, dya_ref, dyb_ref, dg_ref, datt_ref, dcs_ref, dgb_ref) = rest[:n_p], rest[n_p:]

        @pl.when(pl.program_id(0) == 0)
        def _():
            dgb_ref[...] = jnp.zeros_like(dgb_ref)

        dmo_v = dmo_ref[...]
        datt = dcs = None
        for j in range(nj):
            cols, cols_b = slice(j * nb, (j + 1) * nb), slice(d + j * nb, d + (j + 1) * nb)
            dm = _dot(dmo_v, wout_ref[j], True)
            ga, gb = _gate(pieces, gb_ref, j * nb, nb), _gate(pieces, gb_ref, d + j * nb, nb)
            dya, dyb = (dm * ga).astype(BF16), (dm * gb).astype(BF16)
            dga = dm * ya_ref[:, cols].astype(F32) * ga * (1.0 - ga)
            dgb = dm * yb_ref[:, cols].astype(F32) * gb * (1.0 - gb)
            dya_ref[:, cols], dyb_ref[:, cols] = dya, dyb
            dg_ref[:, cols], dg_ref[:, cols_b] = dga.astype(dg_ref.dtype), dgb.astype(dg_ref.dtype)
            dgb_ref[:, cols] += _colsum(dga)
            dgb_ref[:, cols_b] += _colsum(dgb)
            pa, pc = _dot(dya, wao_ref[j], True), _dot(dyb, wco_ref[j], True)
            datt, dcs = (pa, pc) if datt is None else (datt + pa, dcs + pc)
        datt_ref[...] = datt.astype(datt_ref.dtype)
        dcs_ref[...] = dcs.astype(dcs_ref.dtype)

    row = lambda cols: pl.BlockSpec((tm, cols), lambda i: (i, 0))
    return pl.pallas_call(
        body, name="mix_d_merge", grid=(t // tm,),
        in_specs=[row(d), _resident(wout), row(d), row(d), _resident(wao), _resident(wco)] + gate_specs
        + [_resident(gate_bias)],
        out_specs=[row(d), row(d), row(2 * d), row(ka), row(kc), pl.BlockSpec((1, 2 * d), lambda i: (0, 0))],
        out_shape=[jax.ShapeDtypeStruct((t, d), BF16), jax.ShapeDtypeStruct((t, d), BF16),
                   jax.ShapeDtypeStruct((t, 2 * d), BF16), jax.ShapeDtypeStruct((t, ka), BF16),
                   jax.ShapeDtypeStruct((t, kc), F32), jax.ShapeDtypeStruct((1, 2 * d), F32)],
        compiler_params=_cparams(1),
    )(dmo, wout, y_a, y_b, wao, wco, *([proj] * n_p), gate_bias)


def _adamw_math(wv, gv, mv, vv):
    m2 = ADAM_B1 * mv + (1.0 - ADAM_B1) * gv
    v2 = ADAM_B2 * vv + (1.0 - ADAM_B2) * (gv * gv)
    m_hat = m2 / (1.0 - ADAM_B1 ** ADAM_STEP)
    v_hat = v2 / (1.0 - ADAM_B2 ** ADAM_STEP)
    delta = -ADAM_LR * (m_hat / (jnp.sqrt(v_hat) + ADAM_EPS) + ADAM_WD * wv)
    return delta, m2, v2


def _adamw(name, w, g, m, v):
    def fn(r, _):
        return list(_adamw_math(*r)), []

    c = w.shape[1]
    return _rowwise(name, fn, [_whole(w), _whole(g), _whole(m), _whole(v)], [], [(c, F32)] * 3, [], tm=256)


POS_C, POS_CHIP, POS_PEER = 0, 1, 2


def _placed_call(body, name, pos, grid, in_specs, out_specs, out_shape, args):
    return pl.pallas_call(
        body, name=name, out_shape=out_shape, compiler_params=_cparams(len(grid)),
        grid_spec=pltpu.PrefetchScalarGridSpec(num_scalar_prefetch=1, grid=grid, in_specs=in_specs,
                                               out_specs=out_specs),
    )(pos, *args)


def _cast_into(name, pos, w):
    r, cols = w.shape
    tm = _row_tile(r, 1024)

    def body(pos_ref, w_ref, o_ref):
        o_ref[...] = w_ref[...].astype(o_ref.dtype)

    return _placed_call(
        body, name, pos, (r // tm,), [pl.BlockSpec((tm, cols), lambda i, pos: (i, 0))],
        pl.BlockSpec((None, tm, cols), lambda i, pos: (pos[POS_CHIP], i, 0)),
        jax.ShapeDtypeStruct((N_CHIPS, r, cols), BF16), [w])


def _add_pair(name, pos, grad, landed):
    nj, half, cols = landed.shape
    tm = _row_tile(half, 512)
    nb = half // tm

    def body(pos_ref, g_ref, l_ref, o_ref):
        o_ref[...] = (g_ref[...].astype(F32) + l_ref[...].astype(F32)).astype(o_ref.dtype)

    spec = pl.BlockSpec((None, tm, cols), lambda j, i, pos: (j, i, 0))
    return _placed_call(
        body, name, pos, (nj, nb),
        [pl.BlockSpec((None, tm, cols), lambda j, i, pos: (j, pos[POS_C] * nb + i, 0)), spec], spec,
        jax.ShapeDtypeStruct(landed.shape, BF16), [grad, landed])


def _add_chips(name, pos, part, landed):
    _, half, cols = landed.shape
    tm = _row_tile(half, 512)

    def body(pos_ref, p_ref, l0_ref, l1_ref, l2_ref, o_ref):
        acc = p_ref[...].astype(F32)
        for ref in (l0_ref, l1_ref, l2_ref):
            acc = acc + ref[...].astype(F32)
        o_ref[...] = acc

    slot = lambda at: pl.BlockSpec((None, tm, cols), functools.partial(lambda i, pos, at: (pos[at], i, 0), at=at))
    return _placed_call(
        body, name, pos, (half // tm,), [slot(POS_CHIP)] + [slot(POS_PEER + k) for k in range(3)],
        pl.BlockSpec((tm, cols), lambda i, pos: (i, 0)), jax.ShapeDtypeStruct((half, cols), F32),
        [part, landed, landed, landed])


def _adamw_halves(name, pos, w, m, v, own, landed):
    r, cols = w.shape
    half = own.shape[0]
    tm = _row_tile(half, 384 * 1024 // cols)
    nb = half // tm

    def body(pos_ref, w_ref, m_ref, v_ref, own_ref, land_ref, g_out, d_out, m_out, v_out):
        mine = pl.program_id(0) == pos_ref[POS_C]
        g = jnp.where(mine, own_ref[...], land_ref[...])
        delta, m2, v2 = _adamw_math(w_ref[...], g, m_ref[...], v_ref[...])
        g_out[...] = g
        d_out[...] = delta
        m_out[...] = m2
        v_out[...] = v2

    full = pl.BlockSpec((tm, cols), lambda h, i, pos: (h * nb + i, 0))
    used = lambda h, i, pos: (jnp.where(h == pos[POS_C], i, 0), 0)
    unused = lambda h, i, pos: (jnp.where(h == pos[POS_C], 0, i), 0)
    return _placed_call(
        body, name, pos, (2, nb), [full, full, full, pl.BlockSpec((tm, cols), used), pl.BlockSpec((tm, cols), unused)],
        [full] * 4,
        [jax.ShapeDtypeStruct((r, cols), F32)] * 4, [w, m, v, own, landed])


N_START = K_PAD // Q_BLOCK


def _rel_onehot(n_q):
    e = np.arange(REL_EXT)
    dist = K_PAD - (e - (n_q - 1))
    idx = np.clip(dist, -REL_CLIP, REL_CLIP) + REL_CLIP
    return (np.arange(REL_PAD)[:, None] == idx[None, :]).astype(np.float32)


def _skew(x, left):
    row = lax.broadcasted_iota(jnp.int32, x.shape, 0)
    for bit in range(x.shape[0].bit_length() - 1):
        amount = 1 << bit
        rolled = pltpu.roll(x, REL_EXT - amount if left else amount, 1)
        x = jnp.where((row >> bit) & 1 == 1, rolled, x)
    return x


def _bias_expand(table_pad, carry):
    onehot = jnp.asarray(_rel_onehot(Q_BLOCK))

    def core(t_ref, oh_ref, o_ref):
        ext = jnp.dot(t_ref[...], oh_ref[...], precision=lax.Precision.HIGHEST, preferred_element_type=F32)
        qc = lax.broadcasted_iota(jnp.int32, (Q_BLOCK, K_WIN), 0) // CHUNK
        kpos = lax.broadcasted_iota(jnp.int32, (Q_BLOCK, K_WIN), 1)
        band = (kpos // CHUNK >= qc) & (kpos // CHUNK <= qc + LEFT_CHUNKS)
        rows = jnp.broadcast_to(ext, (Q_BLOCK, REL_EXT))
        rolled = _skew(pltpu.roll(rows, REL_EXT - (Q_BLOCK - 1), 1), left=False)[:, :K_WIN]
        for v in range(N_START + 1):
            o_ref[v] = jnp.where(band & (kpos + v * Q_BLOCK >= K_PAD), rolled, NEG)

    (bias,), carried = _call(
        "bias_expand", core, (N_HEADS,),
        [pl.BlockSpec((None, 1, REL_PAD), lambda h: (h, 0, 0)), pl.BlockSpec(onehot.shape, lambda h: (0, 0))],
        [pl.BlockSpec((N_START + 1, None, Q_BLOCK, K_WIN), lambda h: (0, h, 0, 0))],
        [jax.ShapeDtypeStruct((N_START + 1, N_HEADS, Q_BLOCK, K_WIN), F32)], [],
        [table_pad.reshape(N_HEADS, 1, REL_PAD), onehot], carry)
    return bias, carried


def _bias_fold(dbias):
    onehot_t = jnp.asarray(_rel_onehot(CHUNK).T)

    def body(d_ref, oh_ref, o_ref, ext_ref):
        for h in range(N_HEADS):
            x = jnp.concatenate([d_ref[h], jnp.zeros((CHUNK, REL_EXT - K_WIN), F32)], axis=1)
            rolled = _skew(pltpu.roll(x, CHUNK - 1, 1), left=True)
            ext_ref[h:h + 1, :] = jnp.sum(rolled, axis=0, keepdims=True)
        o_ref[...] = jnp.dot(ext_ref[...], oh_ref[...], precision=lax.Precision.HIGHEST,
                             preferred_element_type=F32)

    return pl.pallas_call(
        body, name="bias_fold", out_shape=jax.ShapeDtypeStruct((N_HEADS, REL_PAD), F32),
        scratch_shapes=[pltpu.VMEM((N_HEADS, REL_EXT), F32)],
        compiler_params=pltpu.CompilerParams(vmem_limit_bytes=VMEM_LIMIT_BYTES),
    )(dbias, onehot_t)


def _head_lanes():
    lane = lax.broadcasted_iota(jnp.int32, (1, 2 * HEAD_DIM), 1)
    return [lane < HEAD_DIM, lane >= HEAD_DIM]


def _only(mask, x, scale=None):
    x = jnp.where(mask, x, jnp.zeros_like(x))
    return x if scale is None else x * scale


def _contract_lanes(a, b):
    return lax.dot_general(a, b, (((1,), (1,)), ((), ())), preferred_element_type=F32)


def _contract_rows(a, b):
    return lax.dot_general(a, b, (((0,), (0,)), ((), ())), preferred_element_type=F32)


PAIR = 2 * HEAD_DIM
N_PAIRS = D_ATTN // PAIR


def _attn_specs(pairs):
    width = pairs * PAIR
    per = D_ATTN // width
    row_spec = pl.BlockSpec((Q_BLOCK, width), lambda g, i: (i, g))
    kv_specs = [pl.BlockSpec((Q_BLOCK, width),
                             functools.partial(lambda g, i, kk, c0: (jnp.maximum(i + kk - N_START, 0), c0 + g),
                                               kk=kk, c0=c0))
                for c0 in (per, 2 * per) for kk in range(K_WIN // Q_BLOCK)]
    bias_spec = pl.BlockSpec((None, 2 * pairs, Q_BLOCK, K_WIN), lambda g, i: (jnp.minimum(i, N_START), g, 0, 0))
    return row_spec, kv_specs, bias_spec


def _attn_fwd(proj, bias, pairs=N_PAIRS):
    t = proj.shape[0]
    n_win = K_WIN // Q_BLOCK

    def body(q_ref, *refs):
        k_refs, v_refs = refs[:n_win], refs[n_win:2 * n_win]
        b_ref, o_ref, lse_ref = refs[2 * n_win:]
        for pp in range(pairs):
            cols = slice(pp * PAIR, (pp + 1) * PAIR)
            k = jnp.concatenate([r[:, cols] for r in k_refs], axis=0)
            v = jnp.concatenate([r[:, cols] for r in v_refs], axis=0)
            q = q_ref[:, cols]
            o = lse = None
            for hh, lanes in enumerate(_head_lanes()):
                s = _contract_lanes(_only(lanes, q, HEAD_DIM ** -0.5), k) + b_ref[2 * pp + hh]
                m = jnp.max(s, axis=1, keepdims=True)
                p = jnp.exp(s - m)
                l = jnp.sum(p, axis=1, keepdims=True)
                oh = jnp.dot(p.astype(BF16), v, preferred_element_type=F32) / l
                lse_h = jnp.broadcast_to(m + jnp.log(l), oh.shape)
                o, lse = (oh, lse_h) if o is None else (jnp.where(lanes, oh, o), jnp.where(lanes, lse_h, lse))
            o_ref[:, cols] = o.astype(o_ref.dtype)
            lse_ref[:, cols] = lse

    row_spec, kv_specs, bias_spec = _attn_specs(pairs)
    return pl.pallas_call(
        body, name="attn_fwd", grid=(N_PAIRS // pairs, t // Q_BLOCK),
        in_specs=[row_spec] + kv_specs + [bias_spec], out_specs=[row_spec, row_spec],
        out_shape=[jax.ShapeDtypeStruct((t, D_ATTN), BF16), jax.ShapeDtypeStruct((t, D_ATTN), F32)],
        compiler_params=_cparams(2),
    )(*([proj] * (1 + 2 * n_win)), bias)


def _attn_bwd(proj, bias, att, lse, datt, pairs=2):
    t = proj.shape[0]
    n_win = K_WIN // Q_BLOCK
    n_blocks = t // Q_BLOCK

    def body(q_ref, *refs):
        k_refs, v_refs = refs[:n_win], refs[n_win:2 * n_win]
        b_ref, o_ref, lse_ref, do_ref, dq_ref, dk_ref, dv_ref, db_ref, dk_acc, dv_acc = refs[2 * n_win:]
        i = pl.program_id(1)

        @pl.when(i == 0)
        def _():
            dk_acc[...] = jnp.zeros_like(dk_acc)
            dv_acc[...] = jnp.zeros_like(dv_acc)
            db_ref[...] = jnp.zeros_like(db_ref)

        rows = pl.ds(pl.multiple_of(i * Q_BLOCK, Q_BLOCK), K_WIN)
        scale = HEAD_DIM ** -0.5
        for pp in range(pairs):
            cols = slice(pp * PAIR, (pp + 1) * PAIR)
            k = jnp.concatenate([r[:, cols] for r in k_refs], axis=0)
            v = jnp.concatenate([r[:, cols] for r in v_refs], axis=0)
            q, do, o = q_ref[:, cols], do_ref[:, cols], o_ref[:, cols].astype(F32)
            dq = dk = dv = None
            for hh, lanes in enumerate(_head_lanes()):
                qh, doh = _only(lanes, q, scale), _only(lanes, do)
                s = _contract_lanes(qh, k) + b_ref[2 * pp + hh]
                lse_col = pp * PAIR + hh * HEAD_DIM
                p = jnp.exp(s - lse_ref[:, lse_col:lse_col + 1])
                delta = jnp.sum(doh.astype(F32) * o, axis=1, keepdims=True)
                ds = p * (_contract_lanes(doh, v) - delta)
                folded = ds[:CHUNK]
                for c in range(1, Q_BLOCK // CHUNK):
                    folded = folded + pltpu.roll(ds[c * CHUNK:(c + 1) * CHUNK], K_WIN - c * CHUNK, 1)
                db_ref[2 * pp + hh] += folded
                dsb = ds.astype(BF16)
                dqh = jnp.dot(dsb, k, preferred_element_type=F32)
                dq = dqh if dq is None else jnp.where(lanes, dqh, dq)
                dkh, dvh = _contract_rows(dsb, qh), _contract_rows(p.astype(BF16), doh)
                dk, dv = (dkh, dvh) if dk is None else (dk + dkh, dv + dvh)
            dq_ref[:, cols] = (dq * scale).astype(dq_ref.dtype)
            dk_acc[rows, cols] += dk
            dv_acc[rows, cols] += dv

        @pl.when(i == n_blocks - 1)
        def _():
            dk_ref[...] = dk_acc[K_PAD:, :].astype(dk_ref.dtype)
            dv_ref[...] = dv_acc[K_PAD:, :].astype(dv_ref.dtype)

    width = pairs * PAIR
    row_spec, kv_specs, bias_spec = _attn_specs(pairs)
    full_spec = pl.BlockSpec((t, width), lambda g, i: (0, g))
    return pl.pallas_call(
        body, name="attn_bwd", grid=(N_PAIRS // pairs, n_blocks),
        in_specs=[row_spec] + kv_specs + [bias_spec, row_spec, row_spec, row_spec],
        out_specs=[row_spec, full_spec, full_spec,
                   pl.BlockSpec((2 * pairs, CHUNK, K_WIN), lambda g, i: (g, 0, 0))],
        out_shape=[jax.ShapeDtypeStruct((t, D_ATTN), BF16)] * 3 + [jax.ShapeDtypeStruct((N_HEADS, CHUNK, K_WIN), F32)],
        scratch_shapes=[pltpu.VMEM((t + K_PAD, width), F32)] * 2, compiler_params=_cparams(2),
    )(*([proj] * (1 + 2 * n_win)), bias, att, lse, datt)


CONV_LEAD = CONV_HALO - (CONV_WIDTH - 1)
CONV_LANES = 128
CONV_ROWS = 64


def _conv_specs(t):
    tt = _row_tile(t, CONV_TILE)
    per = tt // CONV_HALO
    n_halo = t // CONV_HALO
    tile = lambda cb: pl.BlockSpec((tt, COL), functools.partial(lambda i, cb: (i, cb), cb=cb))
    prev = lambda cb: pl.BlockSpec((CONV_HALO, COL),
                                   functools.partial(lambda i, cb: (jnp.maximum(i * per - 1, 0), cb), cb=cb))
    nxt = lambda cb: pl.BlockSpec((CONV_HALO, COL),
                                  functools.partial(lambda i, cb: (jnp.minimum((i + 1) * per, n_halo - 1), cb), cb=cb))
    vec = lambda shape: pl.BlockSpec(shape, lambda i: (0, 0))
    return tt, tile, prev, nxt, vec


def _glu(ca, cg, bias):
    return (ca.astype(F32) + bias[:, :D_CONV]) * jax.nn.sigmoid(cg.astype(F32) + bias[:, D_CONV:])


SUBLANES = 8


def _shift_copies(ext_ref):
    n = ext_ref.shape[1] - SUBLANES
    for s in range(1, SUBLANES):
        ext_ref[s, 0:n, :] = ext_ref[0, s:s + n, :]


def _tap_tiles(ext_ref, first_row, r0, lanes):
    n_g = CONV_ROWS // SUBLANES
    for s in range(SUBLANES):
        taps = [w for w in range(CONV_WIDTH) if first_row(w) % SUBLANES == s]
        if not taps:
            continue
        lo = min(first_row(w) for w in taps) - s
        n_tiles = (max(first_row(w) for w in taps) - s - lo) // SUBLANES + n_g
        tiles = [ext_ref[s, r0 + lo + SUBLANES * b:r0 + lo + SUBLANES * (b + 1), lanes] for b in range(n_tiles)]
        for w in taps:
            k = (first_row(w) - s - lo) // SUBLANES
            yield w, tiles[k:k + n_g]


def _taps(ext_ref, tt, first_row, w_ref, out_ref):
    n_g = CONV_ROWS // SUBLANES
    for l0 in range(0, D_CONV, CONV_LANES):
        lanes = slice(l0, l0 + CONV_LANES)
        for r0 in range(0, tt, CONV_ROWS):
            acc = [jnp.zeros((SUBLANES, CONV_LANES), F32)] * n_g
            for w, tiles in _tap_tiles(ext_ref, first_row, r0, lanes):
                weight = jnp.broadcast_to(w_ref[w:w + 1, lanes], (SUBLANES, CONV_LANES))
                acc = [a + t * weight for a, t in zip(acc, tiles)]
            for g in range(n_g):
                out_ref[r0 + SUBLANES * g:r0 + SUBLANES * (g + 1), lanes] = acc[g]


def _tap_sums(ext_ref, tt, first_row, x_ref, out_ref):
    n_g = CONV_ROWS // SUBLANES
    for l0 in range(0, D_CONV, CONV_LANES):
        lanes = slice(l0, l0 + CONV_LANES)
        acc = [jnp.zeros((SUBLANES, CONV_LANES), F32)] * CONV_WIDTH
        for r0 in range(0, tt, CONV_ROWS):
            x = [x_ref[0, r0 + SUBLANES * g:r0 + SUBLANES * (g + 1), lanes] for g in range(n_g)]
            for w, tiles in _tap_tiles(ext_ref, first_row, r0, lanes):
                part = tiles[0] * x[0]
                for g in range(1, n_g):
                    part = part + tiles[g] * x[g]
                acc[w] = acc[w] + part
        for w in range(CONV_WIDTH):
            out_ref[w:w + 1, lanes] += jnp.sum(acc[w], axis=0, keepdims=True)


def _conv_fwd(proj, glu_bias, dw, dw_b, ln_g, ln_b):
    t = proj.shape[0]
    tt, tile, prev, nxt, vec = _conv_specs(t)
    ca_blk, cg_blk = 3 * D_ATTN // COL, 3 * D_ATTN // COL + 1

    def body(ca_ref, cg_ref, pa_ref, pg_ref, gb_ref, dw_ref, dwb_ref, g_ref, b_ref, cs_ref, c_ref, z_ref, ext_ref):
        i = pl.program_id(0)
        bias = gb_ref[...]
        c = _glu(ca_ref[...], cg_ref[...], bias)
        halo = _glu(pa_ref[...], pg_ref[...], bias)
        ext_ref[0, 0:CONV_HALO, :] = jnp.where(i == 0, 0.0, halo)
        ext_ref[0, CONV_HALO:, :] = c
        _shift_copies(ext_ref)
        c_ref[...] = c
        _taps(ext_ref, tt, lambda w: CONV_LEAD + w, dw_ref, z_ref)
        z = z_ref[...] + dwb_ref[...]
        z_ref[...] = z
        mu = jnp.mean(z, axis=-1, keepdims=True)
        zc = z - mu
        y = zc * lax.rsqrt(jnp.mean(zc * zc, axis=-1, keepdims=True) + EPS) * g_ref[...] + b_ref[...]
        cs_ref[...] = (y * jax.nn.sigmoid(y)).astype(cs_ref.dtype)

    out_spec = pl.BlockSpec((tt, D_CONV), lambda i: (i, 0))
    return pl.pallas_call(
        body, name="conv_fwd", grid=(t // tt,),
        in_specs=[tile(ca_blk), tile(cg_blk), prev(ca_blk), prev(cg_blk), vec(glu_bias.shape), vec(dw.shape),
                  vec(dw_b.shape), vec(ln_g.shape), vec(ln_b.shape)],
        out_specs=[out_spec] * 3,
        out_shape=[jax.ShapeDtypeStruct((t, D_CONV), BF16), jax.ShapeDtypeStruct((t, D_CONV), F32),
                   jax.ShapeDtypeStruct((t, D_CONV), F32)],
        scratch_shapes=[pltpu.VMEM((SUBLANES, tt + CONV_HALO, D_CONV), F32)], compiler_params=_cparams(1),
    )(proj, proj, proj, proj, glu_bias, dw, dw_b, ln_g, ln_b)


def _conv_bwd(proj, c, z, dcs, glu_bias, dw, ln_g, ln_b):
    t = proj.shape[0]
    tt, tile, prev, nxt, vec = _conv_specs(t)
    n_tiles = t // tt
    ca_blk, cg_blk = 3 * D_ATTN // COL, 3 * D_ATTN // COL + 1

    def ln_bwd(zv, dcsv, g, b):
        mu = jnp.mean(zv, axis=-1, keepdims=True)
        zc = zv - mu
        rstd = lax.rsqrt(jnp.mean(zc * zc, axis=-1, keepdims=True) + EPS)
        zhat = zc * rstd
        y = zhat * g + b
        sig = jax.nn.sigmoid(y)
        dy = dcsv * sig * (1.0 + y * (1.0 - sig))
        dzh = dy * g
        dz = rstd * (dzh - jnp.mean(dzh, axis=-1, keepdims=True) - zhat * jnp.mean(dzh * zhat, axis=-1, keepdims=True))
        return dz, dy, zhat

    def body(ca_ref, cg_ref, c_ref, cprev_ref, z_ref, znext_ref, dcs_ref, dcsnext_ref, gb_ref, dw_ref, g_ref, b_ref,
             dcin_ref, ddw_ref, ddwb_ref, dg_ref, db_ref, dgb_ref, cext_ref, dzext_ref, dc_ref):
        i = pl.program_id(0)

        @pl.when(i == 0)
        def _():
            for ref in (ddw_ref, ddwb_ref, dg_ref, db_ref, dgb_ref):
                ref[...] = jnp.zeros_like(ref)

        g, b = g_ref[...], b_ref[...]
        dz, dy, zhat = ln_bwd(z_ref[...], dcs_ref[...], g, b)
        dz_next, _, _ = ln_bwd(znext_ref[...], dcsnext_ref[...], g, b)
        dg_ref[...] += _colsum(dy * zhat)
        db_ref[...] += _colsum(dy)
        ddwb_ref[...] += _colsum(dz)
        dzext_ref[0, 0:tt, :] = dz
        dzext_ref[0, tt:, :] = jnp.where(i == n_tiles - 1, 0.0, dz_next)
        _shift_copies(dzext_ref)
        cext_ref[0, 0:CONV_HALO, :] = jnp.where(i == 0, 0.0, cprev_ref[...])
        cext_ref[0, CONV_HALO:, :] = c_ref[...]
        _shift_copies(cext_ref)
        _taps(dzext_ref, tt, lambda w: CONV_WIDTH - 1 - w, dw_ref, dc_ref)
        _tap_sums(cext_ref, tt, lambda w: CONV_LEAD + w, dzext_ref, ddw_ref)
        bias = gb_ref[...]
        a_in = ca_ref[...].astype(F32) + bias[:, :D_CONV]
        sg = jax.nn.sigmoid(cg_ref[...].astype(F32) + bias[:, D_CONV:])
        dc = dc_ref[...]
        dcin = jnp.concatenate([dc * sg, dc * a_in * sg * (1.0 - sg)], axis=1)
        dcin_ref[...] = dcin.astype(dcin_ref.dtype)
        dgb_ref[...] += _colsum(dcin)

    row = lambda: pl.BlockSpec((tt, D_CONV), lambda i: (i, 0))
    per = tt // CONV_HALO
    n_halo = t // CONV_HALO
    prev_row = pl.BlockSpec((CONV_HALO, D_CONV), lambda i: (jnp.maximum(i * per - 1, 0), 0))
    next_row = lambda: pl.BlockSpec((CONV_HALO, D_CONV), lambda i: (jnp.minimum((i + 1) * per, n_halo - 1), 0))
    acc = lambda shape: pl.BlockSpec(shape, lambda i: (0, 0))
    return pl.pallas_call(
        body, name="conv_bwd", grid=(n_tiles,),
        in_specs=[tile(ca_blk), tile(cg_blk), row(), prev_row, row(), next_row(), row(), next_row(),
                  vec(glu_bias.shape), vec(dw.shape), vec(ln_g.shape), vec(ln_b.shape)],
        out_specs=[pl.BlockSpec((tt, 2 * D_CONV), lambda i: (i, 0)), acc(dw.shape), acc((1, D_CONV)),
                   acc((1, D_CONV)), acc((1, D_CONV)), acc((1, 2 * D_CONV))],
        out_shape=[jax.ShapeDtypeStruct((t, 2 * D_CONV), BF16), jax.ShapeDtypeStruct(dw.shape, F32),
                   jax.ShapeDtypeStruct((1, D_CONV), F32), jax.ShapeDtypeStruct((1, D_CONV), F32),
                   jax.ShapeDtypeStruct((1, D_CONV), F32), jax.ShapeDtypeStruct((1, 2 * D_CONV), F32)],
        scratch_shapes=[pltpu.VMEM((SUBLANES, tt + CONV_HALO, D_CONV), F32),
                        pltpu.VMEM((SUBLANES, tt + CONV_HALO, D_CONV), F32), pltpu.VMEM((tt, D_CONV), F32)],
        compiler_params=_cparams(1),
    )(proj, proj, c, c, z, z, dcs, dcs, glu_bias, dw, ln_g, ln_b)


def _place():
    x, y, c = lax.axis_index("x"), lax.axis_index("y"), lax.axis_index("c")
    chips = [(1 - x, y), (x, 1 - y), (1 - x, 1 - y)]
    return x, y, c, chips


def _chip_index(chip):
    return 2 * chip[0] + chip[1]


def _half_rows(c, half):
    return pl.ds(pl.multiple_of(c * half, 16), half)


def _gather_carry(blocked):
    n = len(blocked)

    def over_ici(o_refs, send_sems, recv_sems):
        x, y, c, chips = _place()
        me = _chip_index((x, y))
        copies = []
        for a in range(n):
            mine = o_refs[a].at[me, _half_rows(c, o_refs[a].shape[1] // 2), :]
            for k, chip in enumerate(chips):
                copies.append(pltpu.make_async_remote_copy(
                    src_ref=mine, dst_ref=mine, send_sem=send_sems.at[6 * a + k], recv_sem=recv_sems.at[6 * a + k],
                    device_id=(chip[0], chip[1], c), device_id_type=MESH))
        return copies

    def to_sibling(o_refs, send_sems, recv_sems, sent_by_me):
        x, y, c, chips = _place()
        copies = []
        for a in range(n):
            rows = _half_rows(c if sent_by_me else 1 - c, o_refs[a].shape[1] // 2)
            for k, chip in enumerate(chips):
                landed = o_refs[a].at[_chip_index(chip), rows, :]
                copies.append(pltpu.make_async_remote_copy(
                    src_ref=landed, dst_ref=landed, send_sem=send_sems.at[6 * a + 3 + k],
                    recv_sem=recv_sems.at[6 * a + 3 + k], device_id=(x, y, 1 - c), device_id_type=MESH))
        return copies

    def start(ins, outs, sems):
        for cp in over_ici(outs, *sems):
            cp.start()

    def hand_on(ins, outs, sems):
        for arrived, onward in zip(over_ici(outs, *sems), to_sibling(outs, *sems, True)):
            arrived.wait_recv()
            onward.start()

    def finish(ins, outs, sems):
        for cp in to_sibling(outs, *sems, False):
            cp.wait_recv()
        for cp in over_ici(outs, *sems) + to_sibling(outs, *sems, True):
            cp.wait_send()

    return _Carry(
        ins=list(blocked), outs=[jax.ShapeDtypeStruct(w.shape, w.dtype) for w in blocked],
        aliases={a: a for a in range(n)},
        sems=[pltpu.SemaphoreType.DMA((6 * n,)), pltpu.SemaphoreType.DMA((6 * n,))],
        phases=[("first", start), ("late", hand_on), ("last", finish)])


def _pair_exchange(name, grads):
    n = len(grads)

    def body(*refs):
        g_refs, land_refs = refs[:n], refs[n:2 * n]
        send_sems, recv_sems = refs[2 * n:]
        x, y, c, _ = _place()
        copies = []
        for a in range(n):
            half = g_refs[a].shape[1] // 2
            cp = pltpu.make_async_remote_copy(
                src_ref=g_refs[a].at[:, _half_rows(1 - c, half), :], dst_ref=land_refs[a],
                send_sem=send_sems.at[a], recv_sem=recv_sems.at[a], device_id=(x, y, 1 - c), device_id_type=MESH)
            cp.start()
            copies.append(cp)
        for cp in copies:
            cp.wait()

    return pl.pallas_call(
        body, name=name, in_specs=[ANY] * n, out_specs=[ANY] * n,
        out_shape=[jax.ShapeDtypeStruct((g.shape[0], g.shape[1] // 2, g.shape[2]), g.dtype) for g in grads],
        scratch_shapes=[pltpu.SemaphoreType.DMA((n,)), pltpu.SemaphoreType.DMA((n,))],
    )(*grads)


def _to_owner_carry(parts):
    n = len(parts)

    def sends(p_refs, l_refs, send_sems, recv_sems):
        x, y, c, chips = _place()
        me = _chip_index((x, y))
        return [pltpu.make_async_remote_copy(
            src_ref=p_refs[a].at[_chip_index(chip)], dst_ref=l_refs[a].at[me],
            send_sem=send_sems.at[3 * a + k], recv_sem=recv_sems.at[3 * a + k],
            device_id=(chip[0], chip[1], c), device_id_type=MESH) for a in range(n) for k, chip in enumerate(chips)]

    def start(ins, outs, sems):
        for cp in sends(ins, outs, *sems):
            cp.start()

    def finish(ins, outs, sems):
        x, y, c, chips = _place()
        send_sems, recv_sems = sems
        for a in range(n):
            for k, chip in enumerate(chips):
                slot = outs[a].at[_chip_index(chip)]
                pltpu.make_async_remote_copy(
                    src_ref=slot, dst_ref=slot, send_sem=send_sems.at[3 * a + k], recv_sem=recv_sems.at[3 * a + k],
                    device_id=(chip[0], chip[1], c), device_id_type=MESH).wait_recv()
        for cp in sends(ins, outs, *sems):
            cp.wait_send()

    return _Carry(
        ins=list(parts), outs=[jax.ShapeDtypeStruct(p.shape, p.dtype) for p in parts], aliases={},
        sems=[pltpu.SemaphoreType.DMA((3 * n,)), pltpu.SemaphoreType.DMA((3 * n,))],
        phases=[("first", start), ("last", finish)])


def _swap_halves(halves):
    n = len(halves)

    def body(*refs):
        h_refs, o_refs = refs[:n], refs[n:2 * n]
        send_sems, recv_sems = refs[2 * n:]
        x, y, c, _ = _place()
        copies = []
        for a in range(n):
            cp = pltpu.make_async_remote_copy(
                src_ref=h_refs[a], dst_ref=o_refs[a], send_sem=send_sems.at[a], recv_sem=recv_sems.at[a],
                device_id=(x, y, 1 - c), device_id_type=MESH)
            cp.start()
            copies.append(cp)
        for cp in copies:
            cp.wait()

    return pl.pallas_call(
        body, name="grad_swap_halves", in_specs=[ANY] * n, out_specs=[ANY] * n,
        out_shape=[jax.ShapeDtypeStruct(h.shape, h.dtype) for h in halves],
        scratch_shapes=[pltpu.SemaphoreType.DMA((n,)), pltpu.SemaphoreType.DMA((n,))],
    )(*halves)


def _all_devices(name, block):
    r, cols = block.shape

    def body(b_ref, all_ref, sum_ref, send_sems, recv_sems):
        x, y, c, _ = _place()
        me = 4 * x + 2 * y + c
        all_ref[me] = b_ref[...]
        flips = [(fx, fy, fc) for fx in (0, 1) for fy in (0, 1) for fc in (0, 1)][1:]
        copies = []
        for k, (fx, fy, fc) in enumerate(flips):
            cp = pltpu.make_async_remote_copy(
                src_ref=b_ref, dst_ref=all_ref.at[me], send_sem=send_sems.at[k], recv_sem=recv_sems.at[k],
                device_id=(x ^ fx, y ^ fy, c ^ fc), device_id_type=MESH)
            cp.start()
            copies.append(cp)
        for k, (fx, fy, fc) in enumerate(flips):
            slot = all_ref.at[4 * (x ^ fx) + 2 * (y ^ fy) + (c ^ fc)]
            pltpu.make_async_remote_copy(
                src_ref=slot, dst_ref=slot, send_sem=send_sems.at[k], recv_sem=recv_sems.at[k],
                device_id=(x ^ fx, y ^ fy, c ^ fc), device_id_type=MESH).wait_recv()
        for cp in copies:
            cp.wait_send()
        acc = all_ref[0]
        for d in range(1, N_DEV):
            acc = acc + all_ref[d]
        sum_ref[...] = acc

    vmem = pl.BlockSpec(memory_space=pltpu.VMEM)
    return pl.pallas_call(
        body, name=name, in_specs=[vmem], out_specs=[vmem, vmem],
        out_shape=[jax.ShapeDtypeStruct((N_DEV, r, cols), F32), jax.ShapeDtypeStruct((r, cols), F32)],
        scratch_shapes=[pltpu.SemaphoreType.DMA((N_DEV - 1,)), pltpu.SemaphoreType.DMA((N_DEV - 1,))],
    )(block)


PACK = 1024


def _packed_rows(shape, width):
    size, last = int(np.prod(shape)), shape[-1]
    cols = last if last <= width else width
    assert size % cols == 0
    return size // cols, cols


def _pack(vals, width=PACK):
    rows = []
    for v in vals:
        n_rows, cols = _packed_rows(v.shape, width)
        rows.append(jnp.pad(v.reshape(n_rows, cols).astype(F32), ((0, 0), (0, width - cols))))
    buf = jnp.concatenate(rows, axis=0)
    return jnp.pad(buf, ((0, (-buf.shape[0]) % 8), (0, 0)))


def _unpack(buf, shapes, width=PACK):
    out, r = [], 0
    for shape in shapes:
        n_rows, cols = _packed_rows(shape, width)
        out.append(buf[r:r + n_rows, :cols].reshape(shape))
        r += n_rows
    return out


FFN_SPLIT = 2


def _ffn_hidden(name, n, wg, wu, tm=512, split=1, carry=None):
    m, k = n.shape
    f = wg.shape[0]
    fb = f // split
    tm = _row_tile(m, tm)

    def core(n_ref, wg_ref, wu_ref, a_ref, b_ref, s_ref):
        nv = n_ref[...]
        a = _dot(nv, wg_ref[...], True)
        b = _dot(nv, wu_ref[...], True)
        a_ref[...] = a.astype(a_ref.dtype)
        b_ref[...] = b.astype(b_ref.dtype)
        s_ref[...] = (a * jax.nn.sigmoid(a) * b).astype(s_ref.dtype)

    w_spec = pl.BlockSpec((fb, k), lambda j, i: (j, 0), pipeline_mode=pl.Buffered(1))
    out_spec = pl.BlockSpec((tm, fb), lambda j, i: (i, j))
    return _call(name, core, (split, m // tm), [pl.BlockSpec((tm, k), lambda j, i: (i, 0)), w_spec, w_spec],
                 [out_spec] * 3, [jax.ShapeDtypeStruct((m, f), BF16)] * 3, [], [n, wg, wu], carry)


def _ffn_d_hidden(name, df, wd, a, b, tm=512):
    m, k = df.shape
    f = wd.shape[0]
    fb = f // FFN_SPLIT
    tm = _row_tile(m, tm)

    def body(df_ref, wd_ref, a_ref, b_ref, da_ref, db_ref):
        dfv = df_ref[...]
        for j in range(FFN_SPLIT):
            cols = slice(j * fb, (j + 1) * fb)
            ds = _dot(dfv, wd_ref[cols, :], True)
            av, bv = a_ref[:, cols].astype(F32), b_ref[:, cols].astype(F32)
            sig = jax.nn.sigmoid(av)
            da_ref[:, cols] = (ds * bv * sig * (1.0 + av * (1.0 - sig))).astype(da_ref.dtype)
            db_ref[:, cols] = (ds * av * sig).astype(db_ref.dtype)

    row = pl.BlockSpec((tm, f), lambda i: (i, 0))
    return pl.pallas_call(
        body, name=name, grid=(m // tm,),
        in_specs=[pl.BlockSpec((tm, k), lambda i: (i, 0)), _resident(wd), row, row],
        out_specs=[row, row], out_shape=[jax.ShapeDtypeStruct((m, f), BF16)] * 2,
        compiler_params=_cparams(1),
    )(df, wd, a, b)


def kernel(x, ffn1_norm_pre, ffn1_w_gate, ffn1_w_up, ffn1_w_down, ffn1_norm_post, mix_norm_pre, w_in, gate_bias, rel_table, w_attn_out, conv_glu_bias, conv_dw_w, conv_dw_b, conv_ln_g, conv_ln_b, conv_w_out, w_out, mix_norm_post, ffn2_norm_pre, ffn2_w_gate, ffn2_w_up, ffn2_w_down, ffn2_norm_post, loss_target, m_ffn1_norm_pre, m_ffn1_w_gate, m_ffn1_w_up, m_ffn1_w_down, m_ffn1_norm_post, m_mix_norm_pre, m_w_in, m_gate_bias, m_rel_table, m_w_attn_out, m_conv_glu_bias, m_conv_dw_w, m_conv_dw_b, m_conv_ln_g, m_conv_ln_b, m_conv_w_out, m_w_out, m_mix_norm_post, m_ffn2_norm_pre, m_ffn2_w_gate, m_ffn2_w_up, m_ffn2_w_down, m_ffn2_norm_post, v_ffn1_norm_pre, v_ffn1_w_gate, v_ffn1_w_up, v_ffn1_w_down, v_ffn1_norm_post, v_mix_norm_pre, v_w_in, v_gate_bias, v_rel_table, v_w_attn_out, v_conv_glu_bias, v_conv_dw_w, v_conv_dw_b, v_conv_ln_g, v_conv_ln_b, v_conv_w_out, v_w_out, v_mix_norm_post, v_ffn2_norm_pre, v_ffn2_w_gate, v_ffn2_w_up, v_ffn2_w_down, v_ffn2_norm_post):
    args = dict(locals())
    names = ['ffn1_norm_pre', 'ffn1_w_gate', 'ffn1_w_up', 'ffn1_w_down', 'ffn1_norm_post', 'mix_norm_pre', 'w_in',
             'gate_bias', 'rel_table', 'w_attn_out', 'conv_glu_bias', 'conv_dw_w', 'conv_dw_b', 'conv_ln_g',
             'conv_ln_b', 'conv_w_out', 'w_out', 'mix_norm_post', 'ffn2_norm_pre', 'ffn2_w_gate', 'ffn2_w_up',
             'ffn2_w_down', 'ffn2_norm_post']
    big = ['ffn1_w_gate', 'ffn1_w_up', 'ffn1_w_down', 'w_in', 'w_attn_out', 'conv_w_out', 'w_out', 'ffn2_w_gate',
           'ffn2_w_up', 'ffn2_w_down']
    small = [n for n in names if n not in big]

    xs, target = x[0], loss_target[0]
    t, d = xs.shape
    cx, cy = lax.axis_index("x"), lax.axis_index("y")
    chip = 2 * cx + cy

    dw_shard = conv_dw_w[0, :, 0, :]
    cshard = dw_shard.shape[1]
    dw_all, _ = _all_devices("gather_dw", _pack([dw_shard], width=cshard))
    dw_full = jnp.concatenate([dw_all[2 * j, :CONV_WIDTH, :cshard] for j in range(N_CHIPS)], axis=1)
    dw_full = jnp.pad(dw_full, ((0, CONV_HALO - CONV_WIDTH), (0, 0)))
    peers = [(1 - cx, cy), (cx, 1 - cy), (1 - cx, 1 - cy)]
    pos = jnp.stack([lax.axis_index("c"), chip] + [_chip_index(p) for p in peers]).astype(jnp.int32)
    transposed = ("ffn1_w_gate", "ffn1_w_up", "ffn2_w_gate", "ffn2_w_up")
    weight_of = lambda n: n[2:] if n[:2] in ("m_", "v_") else n
    shard = lambda n: jnp.transpose(args[n][0]) if weight_of(n) in transposed else args[n][0]
    unshard = lambda n, v: (jnp.transpose(v) if n in transposed else v)[None]
    own = {n: _cast_into("cast_" + n, pos, shard(n)) for n in big}
    gather = lambda *ns: _gather_carry([own[n] for n in ns])
    res_spec = [(d, F32), (d, F32), (d, BF16)]
    whole = lambda w: w.reshape(-1, w.shape[-1])

    table_pad = jnp.pad(rel_table[0], ((0, 0), (0, REL_PAD - rel_table.shape[2])))
    bias, (wg1,) = _bias_expand(table_pad, gather("ffn1_w_gate"))
    n1, (wu1,) = _rms_fwd("ffn1_pre", xs, ffn1_norm_pre, gather("ffn1_w_up"))
    (a1, b1, s1), (wd1, win) = _ffn_hidden(
        "ffn1_hidden", n1, whole(wg1), whole(wu1), carry=gather("ffn1_w_down", "w_in"))
    (f1, h1, u), (wg2, wao, wco, wout) = _mm_kblk(
        "ffn1_down", [(s1, whole(wd1)[None])], trans_w=False, epilogue=_ep_post_res_pre(0.5), rows=[xs],
        vecs=[ffn1_norm_post, mix_norm_pre], row_outs=res_spec,
        carry=gather("ffn2_w_gate", "w_attn_out", "conv_w_out", "w_out"))
    proj, (wu2, wd2) = _mm_nblk("mix_in", u, win, trans_w=False, out_blocked=False, out_dtype=BF16, tm=2048,
                                carry=gather("ffn2_w_up", "ffn2_w_down"))
    att, lse = _attn_fwd(proj, bias)
    cs, c_glu, z_conv = _conv_fwd(proj, conv_glu_bias, dw_full, conv_dw_b, conv_ln_g, conv_ln_b)
    y_a, y_b, merged = _mix_merge(att, cs, wao, wco, proj, gate_bias)
    (mo, h2, n2), _ = _mm_kblk(
        "mix_out", [(merged, wout)], trans_w=False, epilogue=_ep_post_res_pre(1.0), rows=[h1],
        vecs=[mix_norm_post, ffn2_norm_pre], row_outs=res_spec, tm=1024)
    (a2, b2, s2), _ = _ffn_hidden("ffn2_hidden", n2, whole(wg2), whole(wu2))
    g = {}
    (dy, df2, err2, g["ffn2_norm_post"]), _ = _mm_kblk(
        "ffn2_down", [(s2, whole(wd2)[None])], trans_w=False, epilogue=_ep_loss(0.5, d), rows=[h2, target],
        vecs=[ffn2_norm_post], row_outs=[(d, F32), (d, BF16)], vec_outs=[d, d])
    loss = lax.psum(0.5 * jnp.sum(err2) / d, ("x", "y", "c"))

    parts, landed = {}, {}

    def ffn_bwd(tag, df, n, a, b, s, wg, wu, wd, **epilogue):
        da, db = _ffn_d_hidden(tag + "_d_hidden", df, whole(wd), a, b)
        group = [tag + "_w_down", tag + "_w_gate", tag + "_w_up"]
        local = [_mm_tn_wide(tag + "_g_" + what, hidden, other, d, a_split=FFN_SPLIT).reshape(wd.shape)
                 for what, hidden, other in (("down", s, df), ("gate", da, n), ("up", db, n))]
        return _mm_kblk(tag + "_d_n", [(da, whole(wg)[None]), (db, whole(wu)[None])], trans_w=False, sub=512,
                        carry=pair_sums(tag, group, local), **epilogue), group

    def pair_sums(tag, group, local):
        theirs = _pair_exchange("pair_" + tag, local)
        for n, mine, other in zip(group, local, theirs):
            parts[n] = _add_pair("pair_sum_" + n, pos, mine, other)
        return _to_owner_carry([parts[n] for n in group])

    def keep(group, carried):
        for n, val in zip(group, carried):
            landed[n] = val

    ((dh2, dmo, g["ffn2_norm_pre"], g["mix_norm_post"]), carried), group = ffn_bwd(
        "ffn2", df2, n2, a2, b2, s2, wg2, wu2, wd2, epilogue=_ep_pre_bwd_post(1.0), rows=[h2, dy, mo],
        vecs=[ffn2_norm_pre, mix_norm_post], row_outs=[(d, F32), (d, BF16)], vec_outs=[d, d])
    keep(group, carried)
    g_wout = _mm_tn("mix_g_out", merged, "col", dmo, "full", tt=4096)
    dy_a, dy_b, dgates, datt, dcs, g["gate_bias"] = _mix_d_merge(dmo, wout, y_a, y_b, wao, wco, proj, gate_bias)
    g_wao = _mm_tn("attn_g_out", att, "full", dy_a, "col", tt=4096)
    g_wco = _mm_tn("conv_g_out", cs, "full", dy_b, "col", tt=4096)
    dq, dk, dv, dbias = _attn_bwd(proj, bias, att, lse, datt)
    g["rel_table"] = _bias_fold(dbias)[:, :rel_table.shape[2]]
    dcin, g_dw, g["conv_dw_b"], g["conv_ln_g"], g["conv_ln_b"], g["conv_glu_bias"] = _conv_bwd(
        proj, c_glu, z_conv, dcs, conv_glu_bias, dw_full, conv_ln_g, conv_ln_b)
    pieces = [("q", dq), ("k", dk), ("v", dv), ("conv", dcin), ("gates", dgates)]
    n_in = win.shape[0] * win.shape[2]
    win_cols = jnp.transpose(jnp.transpose(win, (1, 0, 2)).reshape(d, n_in // COL, COL), (1, 0, 2))
    g_cols = jnp.concatenate([_mm_tn_wide("mix_g_in_" + tag, u, piece, COL, tt=2048) for tag, piece in pieces],
                             axis=0)
    g_win = jnp.transpose(jnp.transpose(g_cols, (1, 0, 2)).reshape(d, win.shape[0], win.shape[2]), (1, 0, 2))
    bounds = np.cumsum([0] + [piece.shape[1] // COL for _, piece in pieces])
    group = ["w_out", "w_attn_out", "conv_w_out", "w_in"]
    (dh1, df1, g["mix_norm_pre"], g["ffn1_norm_post"]), carried = _mm_kblk(
        "mix_d_in", [(piece, win_cols[lo:hi]) for (_, piece), lo, hi in zip(pieces, bounds[:-1], bounds[1:])],
        trans_w=True, sub=512, epilogue=_ep_pre_bwd_post(0.5), rows=[h1, dh2, f1],
        vecs=[mix_norm_pre, ffn1_norm_post], row_outs=[(d, F32), (d, BF16)], vec_outs=[d, d],
        carry=pair_sums("mix", group, [g_wout, g_wao, g_wco, g_win]))
    keep(group, carried)
    ((grad_x, g["ffn1_norm_pre"]), carried), group = ffn_bwd(
        "ffn1", df1, n1, a1, b1, s1, wg1, wu1, wd1, epilogue=_ep_pre_bwd_first(), rows=[xs, dh1],
        vecs=[ffn1_norm_pre], row_outs=[(d, F32)], vec_outs=[d])
    keep(group, carried)

    halves = [_add_chips("chip_sum_" + n, pos, parts[n], landed[n]) for n in big]
    other_halves = _swap_halves(halves)

    g["conv_dw_w"] = g_dw[:CONV_WIDTH]
    _, small_sum = _all_devices("sum_small", _pack([g[n] for n in small]))
    for n, val in zip(small, _unpack(small_sum, [g[n].shape for n in small])):
        g[n] = val
    g["conv_dw_w"] = lax.dynamic_slice_in_dim(g["conv_dw_w"], chip * cshard, cshard, axis=1)

    grads, deltas, new_m, new_v = {}, {}, {}, {}
    for n, mine, other in zip(big, halves, other_halves):
        gr, dl, m2, v2 = _adamw_halves("adamw_" + n, pos, shard(n), shard("m_" + n), shard("v_" + n), mine, other)
        grads[n], deltas[n], new_m[n], new_v[n] = unshard(n, gr), unshard(n, dl), unshard(n, m2), unshard(n, v2)
    shapes = [g[n].shape for n in small]
    packed = lambda pre: _pack([args[pre + n].reshape(shp) for n, shp in zip(small, shapes)])
    dl, m2, v2 = _adamw("adamw_small", packed(""), _pack([g[n] for n in small]), packed("m_"), packed("v_"))
    for n, a_, b_, c_ in zip(small, _unpack(dl, shapes), _unpack(m2, shapes), _unpack(v2, shapes)):
        shape = args[n].shape
        grads[n], deltas[n], new_m[n], new_v[n] = (g[n].reshape(shape), a_.reshape(shape), b_.reshape(shape),
                                                   c_.reshape(shape))

    return (loss, grad_x[None], *[grads[n] for n in names], *[deltas[n] for n in names],
            *[new_m[n] for n in names], *[new_v[n] for n in names])
```
